```python
import jax, jax.numpy as jnp
from jax import lax
import numpy as np

D_MODEL = 1024
BATCH = 16
SEQ = 4096
DEPTH = 1

HEAD_DIM = 64
N_HEADS = D_MODEL // HEAD_DIM
N_HEADS_A = N_HEADS // 2
N_KV_A = N_HEADS_A // 4
N_HEADS_B = N_HEADS - N_HEADS_A
N_KV_B = N_HEADS_B // 4
D_FF = 4 * D_MODEL
GRID_W = 64
BLOCK = 128
WINDOW = 128
N_BUCKETS = 32
MAX_DISTANCE = 128
ROPE_THETA = 10000.0
EPS = 1e-6
NEG_INF = -1e30
IN_WIDTHS = (N_HEADS_A * HEAD_DIM, N_KV_A * HEAD_DIM, N_KV_A * HEAD_DIM,
             N_HEADS_B * HEAD_DIM, N_KV_B * HEAD_DIM, N_KV_B * HEAD_DIM)
IN_TOTAL = sum(IN_WIDTHS)

kernel_name = "hybrid_axial_global_window_sink_encoder"


def rmsnorm(x, g):
    xf = x.astype(jnp.float32)
    y = xf * lax.rsqrt(jnp.mean(xf * xf, axis=-1, keepdims=True) + EPS)
    return (y * g.astype(jnp.float32)).astype(x.dtype)


def _rope_half(x, ang):
    cos = jnp.cos(ang).astype(x.dtype)
    sin = jnp.sin(ang).astype(x.dtype)
    x1, x2 = jnp.split(x, 2, axis=-1)
    return jnp.concatenate([x1 * cos - x2 * sin, x2 * cos + x1 * sin], axis=-1)


def axial_rope(x, row, col):
    half = HEAD_DIM // 2
    nf = half // 2
    freqs = ROPE_THETA ** (-jnp.arange(nf, dtype=jnp.float32) / nf)
    ang_r = row.astype(jnp.float32)[:, None] * freqs[None, :]
    ang_c = col.astype(jnp.float32)[:, None] * freqs[None, :]
    return jnp.concatenate([_rope_half(x[..., :half], ang_r),
                            _rope_half(x[..., half:], ang_c)], axis=-1)


def t5_bucket(rel):
    nb = N_BUCKETS // 2
    ret = (rel > 0).astype(jnp.int32) * nb
    n = jnp.abs(rel)
    max_exact = nb // 2
    nf = jnp.maximum(n, 1).astype(jnp.float32)
    large = max_exact + (jnp.log(nf / max_exact) / np.float32(np.log(MAX_DISTANCE / max_exact))
                         * (nb - max_exact)).astype(jnp.int32)
    large = jnp.minimum(large, nb - 1)
    return ret + jnp.where(n < max_exact, n, large)


def global_axial_gqa(q, k, v, gq, gk):
    bsz, s_len = q.shape[0], q.shape[1]
    rows = s_len // GRID_W
    row = jnp.repeat(jnp.arange(rows, dtype=jnp.int32), GRID_W)
    col = jnp.tile(jnp.arange(GRID_W, dtype=jnp.int32), rows)
    q = rmsnorm(q, gq).transpose(0, 2, 1, 3)
    k = rmsnorm(k, gk).transpose(0, 2, 1, 3)
    v = v.transpose(0, 2, 1, 3)
    q = axial_rope(q, row, col)
    k = axial_rope(k, row, col)
    grp = N_HEADS_A // N_KV_A
    nblk = s_len // BLOCK
    qb = jnp.moveaxis(q.reshape(bsz, N_KV_A, grp, nblk, BLOCK, HEAD_DIM), 3, 0)
    scale = HEAD_DIM ** -0.5

    def one_block(qblk):
        s = jnp.einsum('bkgqd,bksd->bkgqs', qblk, k).astype(jnp.float32) * scale
        p = jax.nn.softmax(s, axis=-1).astype(v.dtype)
        return jnp.einsum('bkgqs,bksd->bkgqd', p, v)

    o = lax.map(one_block, qb)
    return o.transpose(1, 0, 4, 2, 3, 5).reshape(bsz, s_len, N_HEADS_A * HEAD_DIM)


def window_sink_gqa(q, k, v, sink, rel_table):
    bsz, s_len = q.shape[0], q.shape[1]
    grp = N_HEADS_B // N_KV_B
    nblk = s_len // BLOCK
    span = 3 * BLOCK
    qb = jnp.moveaxis(q.transpose(0, 2, 1, 3).reshape(bsz, N_KV_B, grp, nblk, BLOCK, HEAD_DIM), 3, 0)
    pad = ((0, 0), (0, 0), (BLOCK, BLOCK), (0, 0))
    kp = jnp.pad(k.transpose(0, 2, 1, 3), pad)
    vp = jnp.pad(v.transpose(0, 2, 1, 3), pad)
    a = jnp.arange(BLOCK, dtype=jnp.int32)
    c = jnp.arange(span, dtype=jnp.int32)
    rel = c[None, :] - BLOCK - a[:, None]
    band = jnp.abs(rel) <= WINDOW
    bias = rel_table[t5_bucket(rel)]
    bias = bias.transpose(2, 0, 1).reshape(N_KV_B, grp, BLOCK, span).astype(jnp.float32)
    sink_l = sink.reshape(N_KV_B, grp, 1).astype(jnp.float32)
    scale = HEAD_DIM ** -0.5

    def one_block(args):
        qblk, n = args
        start = n * BLOCK
        kblk = lax.dynamic_slice_in_dim(kp, start, span, axis=2)
        vblk = lax.dynamic_slice_in_dim(vp, start, span, axis=2)
        kpos = start - BLOCK + c
        valid = band & ((kpos >= 0) & (kpos < s_len))[None, :]
        s = jnp.einsum('bkgqd,bksd->bkgqs', qblk, kblk).astype(jnp.float32) * scale + bias
        s = jnp.where(valid, s, NEG_INF)
        m = jnp.maximum(jnp.max(s, axis=-1), sink_l)
        p = jnp.exp(s - m[..., None])
        denom = jnp.sum(p, axis=-1) + jnp.exp(sink_l - m)
        w = (p / denom[..., None]).astype(vblk.dtype)
        return jnp.einsum('bkgqs,bksd->bkgqd', w, vblk)

    o = lax.map(one_block, (qb, jnp.arange(nblk, dtype=jnp.int32)))
    return o.transpose(1, 0, 4, 2, 3, 5).reshape(bsz, s_len, N_HEADS_B * HEAD_DIM)


def _fwd_setup_inputs(seed: int = 0) -> dict:
    key = jax.random.key(seed)
    ks = jax.random.split(key, 16)
    f32 = jnp.float32

    def gain(k, shape):
        return 1.0 + 0.02 * jax.random.normal(k, shape, f32)

    return {
        "x": jax.random.normal(ks[0], (BATCH, SEQ, D_MODEL), f32),
        "w_in": jax.random.normal(ks[1], (DEPTH, D_MODEL, IN_TOTAL), f32) * D_MODEL ** -0.5,
        "w_o": jax.random.normal(ks[2], (DEPTH, D_MODEL, D_MODEL), f32) * D_MODEL ** -0.5,
        "g_pre_mix": gain(ks[3], (DEPTH, D_MODEL)),
        "g_post_mix": gain(ks[4], (DEPTH, D_MODEL)),
        "q_norm_a": gain(ks[5], (DEPTH, HEAD_DIM)),
        "k_norm_a": gain(ks[6], (DEPTH, HEAD_DIM)),
        "sink_b": 0.5 * jax.random.normal(ks[7], (DEPTH, N_HEADS_B), f32),
        "rel_bias": 0.5 * jax.random.normal(ks[8], (N_BUCKETS, N_HEADS_B), f32),
        "g_pre_ffn": gain(ks[9], (DEPTH, D_MODEL)),
        "w_ffn_up": jax.random.normal(ks[10], (DEPTH, D_MODEL, D_FF), f32) * D_MODEL ** -0.5,
        "w_ffn_down": jax.random.normal(ks[11], (DEPTH, D_FF, D_MODEL), f32) * D_FF ** -0.5,
        "g_post_ffn": gain(ks[12], (DEPTH, D_MODEL)),
    }


def _fwd_reference(x, w_in, w_o, g_pre_mix, g_post_mix, q_norm_a, k_norm_a, sink_b, rel_bias,
              g_pre_ffn, w_ffn_up, w_ffn_down, g_post_ffn):
    bsz, s_len, _ = x.shape
    split_idx = [int(i) for i in np.cumsum(IN_WIDTHS)[:-1]]
    for l in range(DEPTH):
        h = rmsnorm(x, g_pre_mix[l])
        proj = h @ w_in[l]
        qa, ka, va, qb, kb, vb = jnp.split(proj, split_idx, axis=-1)
        qa = qa.reshape(bsz, s_len, N_HEADS_A, HEAD_DIM)
        ka = ka.reshape(bsz, s_len, N_KV_A, HEAD_DIM)
        va = va.reshape(bsz, s_len, N_KV_A, HEAD_DIM)
        qb = qb.reshape(bsz, s_len, N_HEADS_B, HEAD_DIM)
        kb = kb.reshape(bsz, s_len, N_KV_B, HEAD_DIM)
        vb = vb.reshape(bsz, s_len, N_KV_B, HEAD_DIM)
        out_a = global_axial_gqa(qa, ka, va, q_norm_a[l], k_norm_a[l])
        out_b = window_sink_gqa(qb, kb, vb, sink_b[l], rel_bias)
        mix = jnp.concatenate([out_a, out_b], axis=-1) @ w_o[l]
        x = x + rmsnorm(mix, g_post_mix[l])
        h = rmsnorm(x, g_pre_ffn[l])
        u = jax.nn.relu(h @ w_ffn_up[l])
        f = (u * u) @ w_ffn_down[l]
        x = x + rmsnorm(f, g_post_ffn[l])
    return x


import jax as _jax
import jax.numpy as _jnp

TWIN_FORMAT = 'train_step'
FWD_PARAMS = ['x', 'w_in', 'w_o', 'g_pre_mix', 'g_post_mix', 'q_norm_a', 'k_norm_a', 'sink_b', 'rel_bias', 'g_pre_ffn', 'w_ffn_up', 'w_ffn_down', 'g_post_ffn']
TWIN_WEIGHTS = ['w_in', 'w_o', 'g_pre_mix', 'g_post_mix', 'q_norm_a', 'k_norm_a', 'sink_b', 'rel_bias', 'g_pre_ffn', 'w_ffn_up', 'w_ffn_down', 'g_post_ffn']
TWIN_DIFF_INPUT = 'x'
TWIN_INPUTS = ['x', 'w_in', 'w_o', 'g_pre_mix', 'g_post_mix', 'q_norm_a', 'k_norm_a', 'sink_b', 'rel_bias', 'g_pre_ffn', 'w_ffn_up', 'w_ffn_down', 'g_post_ffn', 'loss_target', 'm_w_in', 'm_w_o', 'm_g_pre_mix', 'm_g_post_mix', 'm_q_norm_a', 'm_k_norm_a', 'm_sink_b', 'm_rel_bias', 'm_g_pre_ffn', 'm_w_ffn_up', 'm_w_ffn_down', 'm_g_post_ffn', 'v_w_in', 'v_w_o', 'v_g_pre_mix', 'v_g_post_mix', 'v_q_norm_a', 'v_k_norm_a', 'v_sink_b', 'v_rel_bias', 'v_g_pre_ffn', 'v_w_ffn_up', 'v_w_ffn_down', 'v_g_post_ffn']
TWIN_OUTPUTS = ['loss', 'grad_x', 'grad_w_in', 'grad_w_o', 'grad_g_pre_mix', 'grad_g_post_mix', 'grad_q_norm_a', 'grad_k_norm_a', 'grad_sink_b', 'grad_rel_bias', 'grad_g_pre_ffn', 'grad_w_ffn_up', 'grad_w_ffn_down', 'grad_g_post_ffn', 'delta_w_in', 'delta_w_o', 'delta_g_pre_mix', 'delta_g_post_mix', 'delta_q_norm_a', 'delta_k_norm_a', 'delta_sink_b', 'delta_rel_bias', 'delta_g_pre_ffn', 'delta_w_ffn_up', 'delta_w_ffn_down', 'delta_g_post_ffn', 'new_m_w_in', 'new_m_w_o', 'new_m_g_pre_mix', 'new_m_g_post_mix', 'new_m_q_norm_a', 'new_m_k_norm_a', 'new_m_sink_b', 'new_m_rel_bias', 'new_m_g_pre_ffn', 'new_m_w_ffn_up', 'new_m_w_ffn_down', 'new_m_g_post_ffn', 'new_v_w_in', 'new_v_w_o', 'new_v_g_pre_mix', 'new_v_g_post_mix', 'new_v_q_norm_a', 'new_v_k_norm_a', 'new_v_sink_b', 'new_v_rel_bias', 'new_v_g_pre_ffn', 'new_v_w_ffn_up', 'new_v_w_ffn_down', 'new_v_g_post_ffn']
TWIN_LEAF_KINDS = {'loss': 'loss', 'grad_x': 'grad_x', 'grad_w_in': 'grad_w', 'grad_w_o': 'grad_w', 'grad_g_pre_mix': 'grad_w', 'grad_g_post_mix': 'grad_w', 'grad_q_norm_a': 'grad_w', 'grad_k_norm_a': 'grad_w', 'grad_sink_b': 'grad_w', 'grad_rel_bias': 'grad_w', 'grad_g_pre_ffn': 'grad_w', 'grad_w_ffn_up': 'grad_w', 'grad_w_ffn_down': 'grad_w', 'grad_g_post_ffn': 'grad_w', 'delta_w_in': 'delta_w', 'delta_w_o': 'delta_w', 'delta_g_pre_mix': 'delta_w', 'delta_g_post_mix': 'delta_w', 'delta_q_norm_a': 'delta_w', 'delta_k_norm_a': 'delta_w', 'delta_sink_b': 'delta_w', 'delta_rel_bias': 'delta_w', 'delta_g_pre_ffn': 'delta_w', 'delta_w_ffn_up': 'delta_w', 'delta_w_ffn_down': 'delta_w', 'delta_g_post_ffn': 'delta_w', 'new_m_w_in': 'new_m', 'new_m_w_o': 'new_m', 'new_m_g_pre_mix': 'new_m', 'new_m_g_post_mix': 'new_m', 'new_m_q_norm_a': 'new_m', 'new_m_k_norm_a': 'new_m', 'new_m_sink_b': 'new_m', 'new_m_rel_bias': 'new_m', 'new_m_g_pre_ffn': 'new_m', 'new_m_w_ffn_up': 'new_m', 'new_m_w_ffn_down': 'new_m', 'new_m_g_post_ffn': 'new_m', 'new_v_w_in': 'new_v', 'new_v_w_o': 'new_v', 'new_v_g_pre_mix': 'new_v', 'new_v_g_post_mix': 'new_v', 'new_v_q_norm_a': 'new_v', 'new_v_k_norm_a': 'new_v', 'new_v_sink_b': 'new_v', 'new_v_rel_bias': 'new_v', 'new_v_g_pre_ffn': 'new_v', 'new_v_w_ffn_up': 'new_v', 'new_v_w_ffn_down': 'new_v', 'new_v_g_post_ffn': 'new_v'}


def _forward(args):
    return _fwd_reference(*[args[k] for k in FWD_PARAMS])


def _output_shape():
    out = _jax.eval_shape(lambda: _forward(_fwd_setup_inputs(0)))
    return out.shape, out.dtype

N_MICROBATCH = 1
ADAM_LR = 0.001
ADAM_B1 = 0.9
ADAM_B2 = 0.999
ADAM_EPS = 1e-08
ADAM_WD = 0.01
ADAM_STEP = 10
PER_EXAMPLE_BATCH_AXIS = {'x': 0, 'loss_target': 0}
SHARED_INPUTS = []
_WEIGHT_DTYPES = {'w_in': _jnp.float32, 'w_o': _jnp.float32, 'g_pre_mix': _jnp.float32, 'g_post_mix': _jnp.float32, 'q_norm_a': _jnp.float32, 'k_norm_a': _jnp.float32, 'sink_b': _jnp.float32, 'rel_bias': _jnp.float32, 'g_pre_ffn': _jnp.float32, 'w_ffn_up': _jnp.float32, 'w_ffn_down': _jnp.float32, 'g_post_ffn': _jnp.float32}
MOMENT_SCALE = {'w_in': 1.628611e+00, 'w_o': 2.097774e+00, 'g_pre_mix': 2.107637e+00, 'g_post_mix': 6.166620e+01, 'q_norm_a': 1.498554e+00, 'k_norm_a': 1.647077e+00, 'sink_b': 5.256339e-02, 'rel_bias': 9.869879e-01, 'g_pre_ffn': 1.265839e+00, 'w_ffn_up': 6.741900e-01, 'w_ffn_down': 2.138215e+00, 'g_post_ffn': 6.509986e+01}


def _to_microbatches(a, axis):
    t = _jnp.moveaxis(a, axis, 0)
    t = t.reshape((N_MICROBATCH, t.shape[0] // N_MICROBATCH) + t.shape[1:])
    return _jnp.moveaxis(t, 1, axis + 1)


def setup_inputs(seed: int = 0) -> dict:
    inp = _fwd_setup_inputs(seed)
    key = _jax.random.fold_in(_jax.random.key(seed), 7919)
    shape, _ = _output_shape()
    out = dict(inp)
    out["loss_target"] = _jax.random.normal(_jax.random.fold_in(key, 0), shape, _jnp.float32)
    for i, name in enumerate(TWIN_WEIGHTS):
        w = inp[name].astype(_jnp.float32)
        if MOMENT_SCALE is None:
            s = _jnp.sqrt(_jnp.mean(_jnp.square(w)) + 1e-30)
        else:
            s = MOMENT_SCALE[name]
        km, kv = _jax.random.split(_jax.random.fold_in(key, i + 1))
        out[name] = w
        out["m_" + name] = s * _jax.random.normal(km, w.shape, _jnp.float32)
        out["v_" + name] = (s * s) * _jax.random.uniform(kv, w.shape, _jnp.float32, 0.5, 1.5)
    if N_MICROBATCH > 1:
        for name, axis in PER_EXAMPLE_BATCH_AXIS.items():
            out[name] = _to_microbatches(out[name], axis)
    return {'x': out['x'], 'w_in': out['w_in'], 'w_o': out['w_o'], 'g_pre_mix': out['g_pre_mix'], 'g_post_mix': out['g_post_mix'], 'q_norm_a': out['q_norm_a'], 'k_norm_a': out['k_norm_a'], 'sink_b': out['sink_b'], 'rel_bias': out['rel_bias'], 'g_pre_ffn': out['g_pre_ffn'], 'w_ffn_up': out['w_ffn_up'], 'w_ffn_down': out['w_ffn_down'], 'g_post_ffn': out['g_post_ffn'], 'loss_target': out['loss_target'], 'm_w_in': out['m_w_in'], 'm_w_o': out['m_w_o'], 'm_g_pre_mix': out['m_g_pre_mix'], 'm_g_post_mix': out['m_g_post_mix'], 'm_q_norm_a': out['m_q_norm_a'], 'm_k_norm_a': out['m_k_norm_a'], 'm_sink_b': out['m_sink_b'], 'm_rel_bias': out['m_rel_bias'], 'm_g_pre_ffn': out['m_g_pre_ffn'], 'm_w_ffn_up': out['m_w_ffn_up'], 'm_w_ffn_down': out['m_w_ffn_down'], 'm_g_post_ffn': out['m_g_post_ffn'], 'v_w_in': out['v_w_in'], 'v_w_o': out['v_w_o'], 'v_g_pre_mix': out['v_g_pre_mix'], 'v_g_post_mix': out['v_g_post_mix'], 'v_q_norm_a': out['v_q_norm_a'], 'v_k_norm_a': out['v_k_norm_a'], 'v_sink_b': out['v_sink_b'], 'v_rel_bias': out['v_rel_bias'], 'v_g_pre_ffn': out['v_g_pre_ffn'], 'v_w_ffn_up': out['v_w_ffn_up'], 'v_w_ffn_down': out['v_w_ffn_down'], 'v_g_post_ffn': out['v_g_post_ffn']}


def _loss(weights, diff, rest, loss_target):
    with _jax.named_scope("forward"):
        args = {**rest, TWIN_DIFF_INPUT: diff, **{k: w.astype(_WEIGHT_DTYPES[k]) for k, w in weights.items()}}
        y = _forward(args)
    with _jax.named_scope("loss_head"):
        err = _jnp.square(y.astype(_jnp.float32) - loss_target)
        return 0.5 * _jnp.sum(_jnp.mean(err, axis=-1)) if err.ndim else 0.5 * err


def _adamw(w, g, m, v):
    m = ADAM_B1 * m + (1.0 - ADAM_B1) * g
    v = ADAM_B2 * v + (1.0 - ADAM_B2) * _jnp.square(g)
    m_hat = m / (1.0 - ADAM_B1 ** ADAM_STEP)
    v_hat = v / (1.0 - ADAM_B2 ** ADAM_STEP)
    delta = -ADAM_LR * (m_hat / (_jnp.sqrt(v_hat) + ADAM_EPS) + ADAM_WD * w)
    return delta, m, v


def reference(x, w_in, w_o, g_pre_mix, g_post_mix, q_norm_a, k_norm_a, sink_b, rel_bias, g_pre_ffn, w_ffn_up, w_ffn_down, g_post_ffn, loss_target, m_w_in, m_w_o, m_g_pre_mix, m_g_post_mix, m_q_norm_a, m_k_norm_a, m_sink_b, m_rel_bias, m_g_pre_ffn, m_w_ffn_up, m_w_ffn_down, m_g_post_ffn, v_w_in, v_w_o, v_g_pre_mix, v_g_post_mix, v_q_norm_a, v_k_norm_a, v_sink_b, v_rel_bias, v_g_pre_ffn, v_w_ffn_up, v_w_ffn_down, v_g_post_ffn):
    given = dict(x=x, w_in=w_in, w_o=w_o, g_pre_mix=g_pre_mix, g_post_mix=g_post_mix, q_norm_a=q_norm_a, k_norm_a=k_norm_a, sink_b=sink_b, rel_bias=rel_bias, g_pre_ffn=g_pre_ffn, w_ffn_up=w_ffn_up, w_ffn_down=w_ffn_down, g_post_ffn=g_post_ffn, loss_target=loss_target, m_w_in=m_w_in, m_w_o=m_w_o, m_g_pre_mix=m_g_pre_mix, m_g_post_mix=m_g_post_mix, m_q_norm_a=m_q_norm_a, m_k_norm_a=m_k_norm_a, m_sink_b=m_sink_b, m_rel_bias=m_rel_bias, m_g_pre_ffn=m_g_pre_ffn, m_w_ffn_up=m_w_ffn_up, m_w_ffn_down=m_w_ffn_down, m_g_post_ffn=m_g_post_ffn, v_w_in=v_w_in, v_w_o=v_w_o, v_g_pre_mix=v_g_pre_mix, v_g_post_mix=v_g_post_mix, v_q_norm_a=v_q_norm_a, v_k_norm_a=v_k_norm_a, v_sink_b=v_sink_b, v_rel_bias=v_rel_bias, v_g_pre_ffn=v_g_pre_ffn, v_w_ffn_up=v_w_ffn_up, v_w_ffn_down=v_w_ffn_down, v_g_post_ffn=v_g_post_ffn)
    weights = {n: given[n] for n in TWIN_WEIGHTS}
    shared = {n: given[n] for n in SHARED_INPUTS}
    per_example = {n: given[n] for n in ['x']}
    grad_fn = _jax.value_and_grad(_loss, argnums=(0, 1))

    def one_microbatch(ex, loss_target):
        ex = dict(ex)
        diff = ex.pop(TWIN_DIFF_INPUT)
        return grad_fn(weights, diff, {**shared, **ex}, loss_target)

    if N_MICROBATCH == 1:
        loss, (grad_w, grad_x) = one_microbatch(per_example, given["loss_target"])
    else:
        def body(carry, xs):
            loss_sum, grad_sum = carry
            l_k, (gw_k, gx_k) = one_microbatch(xs[0], xs[1])
            with _jax.named_scope("update"):
                return (loss_sum + l_k, _jax.tree.map(_jnp.add, grad_sum, gw_k)), gx_k

        init = (_jnp.zeros((), _jnp.float32), _jax.tree.map(_jnp.zeros_like, weights))
        (loss, grad_w), grad_x = _jax.lax.scan(body, init, (per_example, given["loss_target"]))
    with _jax.named_scope("update"):
        delta_w, new_m, new_v = {}, {}, {}
        for n in TWIN_WEIGHTS:
            delta_w[n], new_m[n], new_v[n] = _adamw(weights[n], grad_w[n], given["m_" + n], given["v_" + n])
    return (loss, grad_x, *[grad_w[n] for n in TWIN_WEIGHTS], *[delta_w[n] for n in TWIN_WEIGHTS],
            *[new_m[n] for n in TWIN_WEIGHTS], *[new_v[n] for n in TWIN_WEIGHTS])
```

```python
import functools

import jax
import jax.numpy as jnp
import numpy as np
from jax import lax
from jax.experimental import pallas as pl
from jax.experimental.pallas import tpu as pltpu

F32 = jnp.float32
MM = jnp.bfloat16

D_MODEL = 1024
HEAD_DIM = 64
N_KV = 2
GROUP = 4
Q_WIDTH = 512
KV_WIDTH = 128
D_FF = 4096
GRID_W = 64
BLOCK = 128
N_BUCKETS = 32
MAX_DISTANCE = 128
ROPE_THETA = 10000.0
EPS = 1e-6
NEG_INF = -1e30
SCALE = HEAD_DIM ** -0.5
IN_TOTAL = 1536
N_CHIPS = 4
N_DEV = 8
IN_CHUNK = IN_TOTAL // N_CHIPS
FF_CHUNK = D_FF // N_CHIPS
O_CHUNK = D_MODEL // N_CHIPS
QK_RAW = 640

ADAM_LR = 0.001
ADAM_B1 = 0.9
ADAM_B2 = 0.999
ADAM_EPS = 1e-08
ADAM_WD = 0.01
ADAM_STEP = 10

LANES = 128
MESH = pl.DeviceIdType.MESH
HBM = pl.BlockSpec(memory_space=pl.ANY)
VMEM = pl.BlockSpec(memory_space=pltpu.VMEM)
SMEM = pl.BlockSpec(memory_space=pltpu.SMEM)

ROW_G1, ROW_G2, ROW_G3, ROW_G4, ROW_MISC = 0, 1, 2, 3, 4
MISC_GQ, MISC_GK, MISC_SINK, MISC_REL, MISC_LOSS = 0, 64, 128, 256, 512


def _cparams(sem, vmem_mb):
    return pltpu.CompilerParams(dimension_semantics=sem, vmem_limit_bytes=vmem_mb * 1024 * 1024)


def _dot(a, b):
    return jnp.dot(a, b, preferred_element_type=F32)


def _dot_nt(a, b):
    return lax.dot_general(a, b, (((1,), (1,)), ((), ())), preferred_element_type=F32)


def _dot_tn(a, b):
    return lax.dot_general(a, b, (((0,), (0,)), ((), ())), preferred_element_type=F32)


def _rms_r(x):
    return lax.rsqrt(jnp.mean(x * x, axis=-1, keepdims=True) + EPS)


def _rms_bwd(x, r, g, dy):
    n = x * r
    dn = dy * g
    dx = r * (dn - n * jnp.mean(dn * n, axis=-1, keepdims=True))
    return dx, dy * n


def _seg64_sum(v):
    rows, width = v.shape
    lane = lax.broadcasted_iota(jnp.int32, (rows, LANES), 1)
    lo = lane < HEAD_DIM
    outs = []
    for c in range(width // LANES):
        ch = v[:, c * LANES:(c + 1) * LANES]
        s_lo = jnp.sum(jnp.where(lo, ch, 0.0), axis=-1, keepdims=True)
        s_hi = jnp.sum(jnp.where(lo, 0.0, ch), axis=-1, keepdims=True)
        outs.append(jnp.where(lo, s_lo, s_hi))
    return outs[0] if len(outs) == 1 else jnp.concatenate(outs, axis=-1)


def _head_r(v):
    return lax.rsqrt(_seg64_sum(v * v) * (1.0 / HEAD_DIM) + EPS)


def _swap16(v):
    rows, width = v.shape
    lane = lax.broadcasted_iota(jnp.int32, (rows, LANES), 1)
    first = (lane % 32) < 16
    outs = []
    for c in range(width // LANES):
        ch = v[:, c * LANES:(c + 1) * LANES]
        outs.append(jnp.where(first, pltpu.roll(ch, LANES - 16, 1), pltpu.roll(ch, 16, 1)))
    return outs[0] if len(outs) == 1 else jnp.concatenate(outs, axis=-1)


def _rope(v, cos, sin_signed):
    return v * cos + _swap16(v) * sin_signed


def _rope_t(g, cos, sin_signed):
    return g * cos + _swap16(g * sin_signed)


def _rope_tables(seq):
    nf = HEAD_DIM // 4
    freqs = ROPE_THETA ** (-jnp.arange(nf, dtype=F32) / nf)
    pos = jnp.arange(seq, dtype=jnp.int32)
    row = (pos // GRID_W).astype(F32)
    col = (pos % GRID_W).astype(F32)
    ang_r = row[:, None] * freqs[None, :]
    ang_c = col[:, None] * freqs[None, :]
    cr, sr, cc, sc = jnp.cos(ang_r), jnp.sin(ang_r), jnp.cos(ang_c), jnp.sin(ang_c)
    cos = jnp.concatenate([cr, cr, cc, cc], axis=1)
    sin = jnp.concatenate([-sr, sr, -sc, sc], axis=1)
    return cos, sin


def _t5_bucket(rel):
    nb = N_BUCKETS // 2
    ret = (rel > 0).astype(jnp.int32) * nb
    n = jnp.abs(rel)
    max_exact = nb // 2
    nf = jnp.maximum(n, 1).astype(jnp.float32)
    large = max_exact + (jnp.log(nf / max_exact) / np.float32(np.log(MAX_DISTANCE / max_exact))
                         * (nb - max_exact)).astype(jnp.int32)
    large = jnp.minimum(large, nb - 1)
    return ret + jnp.where(n < max_exact, n, large)


def _window_tables():
    a = jnp.arange(BLOCK, dtype=jnp.int32)
    c = jnp.arange(3 * BLOCK, dtype=jnp.int32)
    rel = c[None, :] - BLOCK - a[:, None]
    bucket = _t5_bucket(rel)
    band = (jnp.abs(rel) <= BLOCK).astype(jnp.int32)
    to3 = lambda t: t.reshape(BLOCK, 3, BLOCK).transpose(1, 0, 2)
    return to3(bucket), to3(band)


def _pre_proj(x, g1, w_in, gq, gk, cq, sq, ck, sk, *, seq, tm):
    tokens = x.shape[0]
    n_seq = seq // tm
    nblk = tm // BLOCK
    batch = tokens // seq

    def body(x_ref, g1_ref, w_ref, gq_ref, gk_ref, cq_ref, sq_ref, ck_ref, sk_ref,
             h1_ref, raw_ref, qa_ref, ka_ref, kta_ref, va_ref, vta_ref,
             qb_ref, kb_ref, ktb_ref, vb_ref, vtb_ref, proj):
        xv = x_ref[...]
        h = (xv * _rms_r(xv) * g1_ref[...]).astype(MM)
        h1_ref[...] = h
        for j in range(N_CHIPS):
            proj[:, j * IN_CHUNK:(j + 1) * IN_CHUNK] = _dot(h, w_ref[j])
        qa = proj[:, 0:Q_WIDTH]
        ka = proj[:, Q_WIDTH:QK_RAW]
        raw_ref[...] = proj[:, 0:QK_RAW]
        qn = qa * _head_r(qa) * gq_ref[...]
        qa_ref[...] = _rope(qn, cq_ref[...], sq_ref[...]).astype(MM)
        kn = ka * _head_r(ka) * gk_ref[...]
        kr = _rope(kn, ck_ref[...], sk_ref[...])
        ka_ref[...] = kr.astype(MM)
        kta_ref[0] = kr.T.astype(MM)
        va = proj[:, 640:768]
        va_ref[...] = va.astype(MM)
        vta_ref[0] = va.T.astype(MM)
        qb_ref[...] = (proj[:, 768:1280] * SCALE).astype(MM)
        kb = proj[:, 1280:1408]
        vb = proj[:, 1408:1536]
        kb_ref[...] = kb.astype(MM)
        vb_ref[...] = vb.astype(MM)
        for j in range(nblk):
            ktb_ref[j] = kb[j * BLOCK:(j + 1) * BLOCK, :].T.astype(MM)
            vtb_ref[j] = vb[j * BLOCK:(j + 1) * BLOCK, :].T.astype(MM)

    tok = lambda w: pl.BlockSpec((tm, w), lambda i: (i, 0))
    tab = lambda w: pl.BlockSpec((tm, w), lambda i: (i % n_seq, 0))
    row = lambda w: pl.BlockSpec((1, w), lambda i: (0, 0))
    tposed = pl.BlockSpec((1, LANES, tm), lambda i: (i // n_seq, 0, i % n_seq))
    blocks = pl.BlockSpec((nblk, BLOCK, LANES), lambda i: (i, 0, 0))
    tok_mm = lambda w: jax.ShapeDtypeStruct((tokens, w), MM)
    return pl.pallas_call(
        body, name="pre_proj",
        grid=(tokens // tm,),
        in_specs=[tok(D_MODEL), row(D_MODEL),
                  pl.BlockSpec((N_CHIPS, D_MODEL, IN_CHUNK), lambda i: (0, 0, 0)),
                  row(Q_WIDTH), row(KV_WIDTH), tab(Q_WIDTH), tab(Q_WIDTH), tab(KV_WIDTH), tab(KV_WIDTH)],
        out_specs=[tok(D_MODEL), tok(QK_RAW), tok(Q_WIDTH), tok(KV_WIDTH), tposed, tok(KV_WIDTH), tposed,
                   tok(Q_WIDTH), tok(KV_WIDTH), blocks, tok(KV_WIDTH), blocks],
        out_shape=[
            tok_mm(D_MODEL),
            jax.ShapeDtypeStruct((tokens, QK_RAW), F32),
            tok_mm(Q_WIDTH),
            tok_mm(KV_WIDTH),
            jax.ShapeDtypeStruct((batch, KV_WIDTH, seq), MM),
            tok_mm(KV_WIDTH),
            jax.ShapeDtypeStruct((batch, KV_WIDTH, seq), MM),
            tok_mm(Q_WIDTH),
            tok_mm(KV_WIDTH),
            jax.ShapeDtypeStruct((tokens // BLOCK, KV_WIDTH, BLOCK), MM),
            tok_mm(KV_WIDTH),
            jax.ShapeDtypeStruct((tokens // BLOCK, KV_WIDTH, BLOCK), MM),
        ],
        scratch_shapes=[pltpu.VMEM((tm, IN_TOTAL), F32)],
        compiler_params=_cparams(("parallel",), 48),
    )(x, g1, w_in, gq, gk, cq, sq, ck, sk)


def _kv_half(v2, kv):
    return jnp.where(kv == 0, v2[:, :HEAD_DIM], v2[:, HEAD_DIM:])


def _attn_a_fwd(qa, kta, va, *, seq, bq):
    tokens = qa.shape[0]
    batch = tokens // seq
    nq = seq // bq

    def body(q_ref, kt_ref, v_ref, o_ref, lse_ref):
        kv = pl.program_id(1)
        kt = kt_ref[0]
        v = v_ref[...]
        for g in range(GROUP):
            sl = slice(g * HEAD_DIM, (g + 1) * HEAD_DIM)
            s = _dot(q_ref[:, sl], kt)
            m = jnp.max(s, axis=-1, keepdims=True)
            p = jnp.exp(s - m)
            l = jnp.sum(p, axis=-1, keepdims=True)
            o2 = _dot(p.astype(MM), v)
            o_ref[:, sl] = _kv_half(o2, kv) / l
            lse_ref[0, :, g:g + 1] = m + jnp.log(l)

    return pl.pallas_call(
        body, name="attn_a_fwd",
        grid=(batch, N_KV, nq),
        in_specs=[pl.BlockSpec((bq, GROUP * HEAD_DIM), lambda b, k, i: (b * nq + i, k)),
                  pl.BlockSpec((1, HEAD_DIM, seq), lambda b, k, i: (b, k, 0)),
                  pl.BlockSpec((seq, KV_WIDTH), lambda b, k, i: (b, 0))],
        out_specs=[pl.BlockSpec((bq, GROUP * HEAD_DIM), lambda b, k, i: (b * nq + i, k)),
                   pl.BlockSpec((1, bq, GROUP), lambda b, k, i: (k, b * nq + i, 0))],
        out_shape=[jax.ShapeDtypeStruct((tokens, Q_WIDTH), F32),
                   jax.ShapeDtypeStruct((N_KV, tokens, GROUP), F32)],
        compiler_params=_cparams(("parallel", "parallel", "parallel"), 48),
    )(qa, kta, va)


def _attn_a_bwd(qa, kta, ka, vta, do, o, lse, *, seq, bq):
    tokens = qa.shape[0]
    batch = tokens // seq
    nq = seq // bq

    def body(q_ref, kt_ref, k_ref, vt_ref, do_ref, o_ref, lse_ref, dq_ref, dkt_ref, dvt_ref):
        kv = pl.program_id(1)

        @pl.when(pl.program_id(2) == 0)
        def _():
            dkt_ref[...] = jnp.zeros_like(dkt_ref)
            dvt_ref[...] = jnp.zeros_like(dvt_ref)

        kt = kt_ref[0]
        vt = vt_ref[0]
        k2 = k_ref[...]
        for g in range(GROUP):
            sl = slice(g * HEAD_DIM, (g + 1) * HEAD_DIM)
            q = q_ref[:, sl]
            dof = do_ref[:, sl]
            delta = jnp.sum(dof * o_ref[:, sl], axis=-1, keepdims=True)
            dob = dof.astype(MM)
            p = jnp.exp(_dot(q, kt) - lse_ref[0, :, g:g + 1])
            dp = _dot(dob, vt)
            ds = (p * (dp - delta)).astype(MM)
            dq_ref[:, sl] = _kv_half(_dot(ds, k2), kv)
            dkt_ref[0] += _dot_tn(q, ds)
            dvt_ref[0] += _dot_tn(dob, p.astype(MM))

    qspec = pl.BlockSpec((bq, GROUP * HEAD_DIM), lambda b, k, i: (b * nq + i, k))
    tspec = pl.BlockSpec((1, HEAD_DIM, seq), lambda b, k, i: (b, k, 0))
    return pl.pallas_call(
        body, name="attn_a_bwd",
        grid=(batch, N_KV, nq),
        in_specs=[qspec, tspec, pl.BlockSpec((seq, KV_WIDTH), lambda b, k, i: (b, 0)), tspec, qspec, qspec,
                  pl.BlockSpec((1, bq, GROUP), lambda b, k, i: (k, b * nq + i, 0))],
        out_specs=[qspec, tspec, tspec],
        out_shape=[jax.ShapeDtypeStruct((tokens, Q_WIDTH), F32),
                   jax.ShapeDtypeStruct((batch, KV_WIDTH, seq), F32),
                   jax.ShapeDtypeStruct((batch, KV_WIDTH, seq), F32)],
        compiler_params=_cparams(("parallel", "parallel", "arbitrary"), 56),
    )(qa, kta, ka, vta, do, o, lse)


def _bias_build(rel_bias_t, bucket, band):
    def body(tab_ref, bucket_ref, band_ref, bias_ref):
        for h in range(GROUP * N_KV):
            for piece in range(3):
                bk = bucket_ref[piece]
                acc = jnp.zeros((BLOCK, BLOCK), F32)
                for b in range(N_BUCKETS):
                    acc = jnp.where(bk == b, tab_ref[h, b], acc)
                bias_ref[h, piece] = jnp.where(band_ref[piece] != 0, acc, NEG_INF)

    return pl.pallas_call(
        body, name="bias_build",
        in_specs=[SMEM, VMEM, VMEM], out_specs=VMEM,
        out_shape=jax.ShapeDtypeStruct((GROUP * N_KV, 3, BLOCK, BLOCK), F32),
    )(rel_bias_t, bucket, band)


def _window_scores(q, kts, bias_ref, g, n, nblk):
    pieces = []
    for piece in range(3):
        s = _dot(q, kts[piece]) + bias_ref[g, piece]
        if piece == 0:
            s = jnp.where(n > 0, s, NEG_INF)
        if piece == 2:
            s = jnp.where(n < nblk - 1, s, NEG_INF)
        pieces.append(s)
    return pieces


def _attn_b_fwd(qb, ktb, vb3, bias, sink, *, seq):
    tokens = qb.shape[0]
    batch = tokens // seq
    nblk = seq // BLOCK

    def body(sink_ref, q_ref, kt_ref, v_ref, bias_ref, o_ref, lse_ref):
        kv = pl.program_id(0)

        def block(n, carry):
            nl = jnp.maximum(n - 1, 0)
            nr = jnp.minimum(n + 1, nblk - 1)
            rows = pl.ds(pl.multiple_of(n * BLOCK, BLOCK), BLOCK)
            kts = (kt_ref[nl], kt_ref[n], kt_ref[nr])
            vs = (v_ref[nl], v_ref[n], v_ref[nr])
            for g in range(GROUP):
                sl = slice(g * HEAD_DIM, (g + 1) * HEAD_DIM)
                ss = _window_scores(q_ref[rows, sl], kts, bias_ref, g, n, nblk)
                sink_l = sink_ref[0, kv * GROUP + g]
                m = jnp.maximum(jnp.maximum(jnp.max(ss[0], axis=-1, keepdims=True),
                                            jnp.max(ss[1], axis=-1, keepdims=True)),
                                jnp.maximum(jnp.max(ss[2], axis=-1, keepdims=True), sink_l))
                ps = [jnp.exp(s - m) for s in ss]
                denom = (jnp.sum(ps[0], axis=-1, keepdims=True) + jnp.sum(ps[1], axis=-1, keepdims=True)
                         + jnp.sum(ps[2], axis=-1, keepdims=True) + jnp.exp(sink_l - m))
                o2 = (_dot(ps[0].astype(MM), vs[0]) + _dot(ps[1].astype(MM), vs[1])
                      + _dot(ps[2].astype(MM), vs[2]))
                o_ref[rows, sl] = _kv_half(o2, kv) / denom
                lse_ref[0, rows, g:g + 1] = m + jnp.log(denom)
            return carry

        lax.fori_loop(0, nblk, block, 0)

    return pl.pallas_call(
        body, name="attn_b_fwd",
        grid=(N_KV, batch),
        in_specs=[SMEM,
                  pl.BlockSpec((seq, GROUP * HEAD_DIM), lambda k, b: (b, k)),
                  pl.BlockSpec((nblk, HEAD_DIM, BLOCK), lambda k, b: (b, k, 0)),
                  pl.BlockSpec((nblk, BLOCK, KV_WIDTH), lambda k, b: (b, 0, 0)),
                  pl.BlockSpec((GROUP, 3, BLOCK, BLOCK), lambda k, b: (k, 0, 0, 0))],
        out_specs=[pl.BlockSpec((seq, GROUP * HEAD_DIM), lambda k, b: (b, k)),
                   pl.BlockSpec((1, seq, GROUP), lambda k, b: (k, b, 0))],
        out_shape=[jax.ShapeDtypeStruct((tokens, Q_WIDTH), F32),
                   jax.ShapeDtypeStruct((N_KV, tokens, GROUP), F32)],
        compiler_params=_cparams(("parallel", "parallel"), 48),
    )(sink, qb, ktb, vb3, bias)


def _attn_b_bwd(qb, ktb, kb3, vtb, do, o, lse, bias, sink, *, seq):
    tokens = qb.shape[0]
    batch = tokens // seq
    nblk = seq // BLOCK

    def body(sink_ref, q_ref, kt_ref, k_ref, vt_ref, do_ref, o_ref, lse_ref, bias_ref,
             dq_ref, dkt_ref, dvt_ref, dbias_ref, dsink_ref):
        kv = pl.program_id(0)

        @pl.when(pl.program_id(1) == 0)
        def _():
            dbias_ref[...] = jnp.zeros_like(dbias_ref)
            dsink_ref[...] = jnp.zeros_like(dsink_ref)

        dkt_ref[...] = jnp.zeros_like(dkt_ref)
        dvt_ref[...] = jnp.zeros_like(dvt_ref)

        def block(n, dsink):
            idx = (jnp.maximum(n - 1, 0), n, jnp.minimum(n + 1, nblk - 1))
            rows = pl.ds(pl.multiple_of(n * BLOCK, BLOCK), BLOCK)
            kts = tuple(kt_ref[i] for i in idx)
            new_dsink = []
            for g in range(GROUP):
                sl = slice(g * HEAD_DIM, (g + 1) * HEAD_DIM)
                q = q_ref[rows, sl]
                dof = do_ref[rows, sl]
                delta = jnp.sum(dof * o_ref[rows, sl], axis=-1, keepdims=True)
                dob = dof.astype(MM)
                lse_g = lse_ref[0, rows, g:g + 1]
                ss = _window_scores(q, kts, bias_ref, g, n, nblk)
                dq2 = jnp.zeros((BLOCK, KV_WIDTH), F32)
                for piece in range(3):
                    p = jnp.exp(ss[piece] - lse_g)
                    dp = _dot(dob, vt_ref[idx[piece]])
                    ds = p * (dp - delta)
                    dsb = ds.astype(MM)
                    dbias_ref[g, piece] += ds
                    dq2 = dq2 + _dot(dsb, k_ref[idx[piece]])
                    dkt_ref[idx[piece]] += _dot_tn(q, dsb)
                    dvt_ref[idx[piece]] += _dot_tn(dob, p.astype(MM))
                dq_ref[rows, sl] = _kv_half(dq2, kv)
                p_sink = jnp.exp(sink_ref[0, kv * GROUP + g] - lse_g)
                new_dsink.append(dsink[g] - p_sink * delta)
            return tuple(new_dsink)

        zero = jnp.zeros((BLOCK, 1), F32)
        dsink = lax.fori_loop(0, nblk, block, (zero,) * GROUP)
        for g in range(GROUP):
            dsink_ref[0, g:g + 1, :] += jnp.broadcast_to(jnp.sum(dsink[g], axis=0, keepdims=True), (1, LANES))

    qspec = pl.BlockSpec((seq, GROUP * HEAD_DIM), lambda k, b: (b, k))
    tspec = pl.BlockSpec((nblk, HEAD_DIM, BLOCK), lambda k, b: (b, k, 0))
    return pl.pallas_call(
        body, name="attn_b_bwd",
        grid=(N_KV, batch),
        in_specs=[SMEM, qspec, tspec,
                  pl.BlockSpec((nblk, BLOCK, KV_WIDTH), lambda k, b: (b, 0, 0)),
                  tspec, qspec, qspec,
                  pl.BlockSpec((1, seq, GROUP), lambda k, b: (k, b, 0)),
                  pl.BlockSpec((GROUP, 3, BLOCK, BLOCK), lambda k, b: (k, 0, 0, 0))],
        out_specs=[qspec, tspec, tspec,
                   pl.BlockSpec((GROUP, 3, BLOCK, BLOCK), lambda k, b: (k, 0, 0, 0)),
                   pl.BlockSpec((1, 8, LANES), lambda k, b: (k, 0, 0))],
        out_shape=[jax.ShapeDtypeStruct((tokens, Q_WIDTH), F32),
                   jax.ShapeDtypeStruct((tokens // BLOCK, KV_WIDTH, BLOCK), F32),
                   jax.ShapeDtypeStruct((tokens // BLOCK, KV_WIDTH, BLOCK), F32),
                   jax.ShapeDtypeStruct((GROUP * N_KV, 3, BLOCK, BLOCK), F32),
                   jax.ShapeDtypeStruct((N_KV, 8, LANES), F32)],
        compiler_params=_cparams(("parallel", "arbitrary"), 48),
    )(sink, qb, ktb, kb3, vtb, do, o, lse, bias)


def _wo_post(oa, ob, w_o, x, g2, g3, *, tm):
    tokens = x.shape[0]

    def body(oa_ref, ob_ref, w_ref, x_ref, g2_ref, g3_ref, mix_ref, x1_ref, h2_ref):
        mix = (_dot(oa_ref[...].astype(MM), w_ref[0:Q_WIDTH, :])
               + _dot(ob_ref[...].astype(MM), w_ref[Q_WIDTH:D_MODEL, :]))
        mix_ref[...] = mix
        x1 = x_ref[...] + mix * _rms_r(mix) * g2_ref[...]
        x1_ref[...] = x1
        h2_ref[...] = (x1 * _rms_r(x1) * g3_ref[...]).astype(MM)

    tok = lambda w: pl.BlockSpec((tm, w), lambda i: (i, 0))
    row = pl.BlockSpec((1, D_MODEL), lambda i: (0, 0))
    return pl.pallas_call(
        body, name="wo_post",
        grid=(tokens // tm,),
        in_specs=[tok(Q_WIDTH), tok(Q_WIDTH), pl.BlockSpec((D_MODEL, D_MODEL), lambda i: (0, 0)),
                  tok(D_MODEL), row, row],
        out_specs=[tok(D_MODEL), tok(D_MODEL), tok(D_MODEL)],
        out_shape=[jax.ShapeDtypeStruct((tokens, D_MODEL), F32),
                   jax.ShapeDtypeStruct((tokens, D_MODEL), F32),
                   jax.ShapeDtypeStruct((tokens, D_MODEL), MM)],
        compiler_params=_cparams(("parallel",), 40),
    )(oa, ob, w_o, x, g2, g3)


def _ffn_fwd_loss(h2, w_up, w_down, x1, target, g4, *, tm):
    tokens = h2.shape[0]
    nt = tokens // tm

    def body(h2_ref, wu_ref, wd_ref, x1_ref, t_ref, g4_ref, u_ref, df_ref, dy_ref, loss_ref, dg4_ref, acc):
        c = pl.program_id(1)
        u = jnp.maximum(_dot(h2_ref[...], wu_ref[0]), 0.0)
        u_ref[...] = u
        part = _dot((u * u).astype(MM), wd_ref[...])

        @pl.when(c == 0)
        def _():
            acc[...] = part

        @pl.when(c > 0)
        def _():
            acc[...] += part

        @pl.when(c == N_CHIPS - 1)
        def _():
            f = acc[...]
            r = _rms_r(f)
            g4v = g4_ref[...]
            err = x1_ref[...] + f * r * g4v - t_ref[...]
            sq = jnp.sum(err * err, axis=-1, keepdims=True)
            loss_ref[0] = jnp.broadcast_to(jnp.sum(sq, axis=0, keepdims=True) * (0.5 / D_MODEL), (8, LANES))
            dy = err * (1.0 / D_MODEL)
            dy_ref[...] = dy
            dfv, dgv = _rms_bwd(f, r, g4v, dy)
            df_ref[...] = dfv.astype(MM)
            dg4_ref[0] = jnp.sum(dgv, axis=0, keepdims=True)

    tok = pl.BlockSpec((tm, D_MODEL), lambda i, c: (i, 0))
    return pl.pallas_call(
        body, name="ffn_fwd_loss",
        grid=(nt, N_CHIPS),
        in_specs=[tok, pl.BlockSpec((1, D_MODEL, FF_CHUNK), lambda i, c: (c, 0, 0)),
                  pl.BlockSpec((FF_CHUNK, D_MODEL), lambda i, c: (c, 0)),
                  tok, tok, pl.BlockSpec((1, D_MODEL), lambda i, c: (0, 0))],
        out_specs=[pl.BlockSpec((tm, FF_CHUNK), lambda i, c: (i, c)), tok, tok,
                   pl.BlockSpec((1, 8, LANES), lambda i, c: (i, 0, 0)),
                   pl.BlockSpec((1, 1, D_MODEL), lambda i, c: (i, 0, 0))],
        out_shape=[jax.ShapeDtypeStruct((tokens, D_FF), F32),
                   jax.ShapeDtypeStruct((tokens, D_MODEL), MM),
                   jax.ShapeDtypeStruct((tokens, D_MODEL), F32),
                   jax.ShapeDtypeStruct((nt, 8, LANES), F32),
                   jax.ShapeDtypeStruct((nt, 1, D_MODEL), F32)],
        scratch_shapes=[pltpu.VMEM((tm, D_MODEL), F32)],
        compiler_params=_cparams(("parallel", "arbitrary"), 56),
    )(h2, w_up, w_down, x1, target, g4)


def _ffn_bwd_act(df, w_down, u, w_up, x1, dy, mix, g3, g2, *, tm):
    tokens = df.shape[0]
    nt = tokens // tm

    def body(df_ref, wd_ref, u_ref, wu_ref, x1_ref, dy_ref, mix_ref, g3_ref, g2_ref,
             dz_ref, dx1_ref, dmix_ref, dg3_ref, dg2_ref, acc):
        c = pl.program_id(1)
        da = _dot_nt(df_ref[...], wd_ref[...])
        dz = (da * (2.0 * u_ref[...])).astype(MM)
        dz_ref[...] = dz
        part = _dot_nt(dz, wu_ref[0])

        @pl.when(c == 0)
        def _():
            acc[...] = part

        @pl.when(c > 0)
        def _():
            acc[...] += part

        @pl.when(c == N_CHIPS - 1)
        def _():
            dh2 = acc[...]
            x1 = x1_ref[...]
            dxn, dg3v = _rms_bwd(x1, _rms_r(x1), g3_ref[...], dh2)
            dx1 = dy_ref[...] + dxn
            dx1_ref[...] = dx1
            dg3_ref[0] = jnp.sum(dg3v, axis=0, keepdims=True)
            mix = mix_ref[...]
            dmix, dg2v = _rms_bwd(mix, _rms_r(mix), g2_ref[...], dx1)
            dmix_ref[...] = dmix.astype(MM)
            dg2_ref[0] = jnp.sum(dg2v, axis=0, keepdims=True)

    tok = pl.BlockSpec((tm, D_MODEL), lambda i, c: (i, 0))
    chunk = pl.BlockSpec((tm, FF_CHUNK), lambda i, c: (i, c))
    row = pl.BlockSpec((1, D_MODEL), lambda i, c: (0, 0))
    part = pl.BlockSpec((1, 1, D_MODEL), lambda i, c: (i, 0, 0))
    return pl.pallas_call(
        body, name="ffn_bwd_act",
        grid=(nt, N_CHIPS),
        in_specs=[tok, pl.BlockSpec((FF_CHUNK, D_MODEL), lambda i, c: (c, 0)), chunk,
                  pl.BlockSpec((1, D_MODEL, FF_CHUNK), lambda i, c: (c, 0, 0)), tok, tok, tok, row, row],
        out_specs=[chunk, tok, tok, part, part],
        out_shape=[jax.ShapeDtypeStruct((tokens, D_FF), MM),
                   jax.ShapeDtypeStruct((tokens, D_MODEL), F32),
                   jax.ShapeDtypeStruct((tokens, D_MODEL), MM),
                   jax.ShapeDtypeStruct((nt, 1, D_MODEL), F32),
                   jax.ShapeDtypeStruct((nt, 1, D_MODEL), F32)],
        scratch_shapes=[pltpu.VMEM((tm, D_MODEL), F32)],
        compiler_params=_cparams(("parallel", "arbitrary"), 56),
    )(df, w_down, u, w_up, x1, dy, mix, g3, g2)


def _tn_matmul(a, b, *, name, tm, tn, tk, chunked=False, square_a=False, vmem_mb=48):
    tokens, m_dim = a.shape
    n_dim = b.shape[1]
    if chunked:
        assert tm == m_dim

    def body(a_ref, b_ref, o_ref):
        av = a_ref[...]
        if square_a:
            av = av * av
        part = _dot_tn(av.astype(MM), b_ref[...].astype(MM))
        part = part[None] if chunked else part

        @pl.when(pl.program_id(2) == 0)
        def _():
            o_ref[...] = part

        @pl.when(pl.program_id(2) > 0)
        def _():
            o_ref[...] += part

    if chunked:
        out_spec = pl.BlockSpec((1, tm, tn), lambda i, j, k: (j, 0, 0))
        out_shape = jax.ShapeDtypeStruct((n_dim // tn, m_dim, tn), F32)
    else:
        out_spec = pl.BlockSpec((tm, tn), lambda i, j, k: (i, j))
        out_shape = jax.ShapeDtypeStruct((m_dim, n_dim), F32)
    return pl.pallas_call(
        body, name=name,
        grid=(m_dim // tm, n_dim // tn, tokens // tk),
        in_specs=[pl.BlockSpec((tk, tm), lambda i, j, k: (k, i)),
                  pl.BlockSpec((tk, tn), lambda i, j, k: (k, j))],
        out_specs=out_spec, out_shape=out_shape,
        compiler_params=_cparams(("parallel", "parallel", "arbitrary"), vmem_mb),
    )(a, b)


def _wo_bwd(dmix, w_o, *, tm):
    tokens = dmix.shape[0]

    def body(dm_ref, w_ref, doa_ref, dob_ref):
        dm = dm_ref[...]
        doa_ref[...] = _dot_nt(dm, w_ref[0:Q_WIDTH, :])
        dob_ref[...] = _dot_nt(dm, w_ref[Q_WIDTH:D_MODEL, :])

    tok = lambda w: pl.BlockSpec((tm, w), lambda i: (i, 0))
    return pl.pallas_call(
        body, name="wo_bwd",
        grid=(tokens // tm,),
        in_specs=[tok(D_MODEL), pl.BlockSpec((D_MODEL, D_MODEL), lambda i: (0, 0))],
        out_specs=[tok(Q_WIDTH), tok(Q_WIDTH)],
        out_shape=[jax.ShapeDtypeStruct((tokens, Q_WIDTH), F32)] * 2,
        compiler_params=_cparams(("parallel",), 40),
    )(dmix, w_o)


def _proj_bwd(dqa, dkta, dvta, dqb, dktb, dvtb, raw, x, dx1, g1, w_in, gq, gk, cq, sq, ck, sk, *, seq, tm):
    tokens = x.shape[0]
    nt = tokens // tm
    n_seq = seq // tm
    nblk = tm // BLOCK

    def body(dqa_ref, dkta_ref, dvta_ref, dqb_ref, dktb_ref, dvtb_ref, raw_ref, x_ref, dx1_ref, g1_ref, w_ref,
             gq_ref, gk_ref, cq_ref, sq_ref, ck_ref, sk_ref,
             gx_ref, dproj_ref, dg1_ref, dgq_ref, dgk_ref, dp):
        qa = raw_ref[:, 0:Q_WIDTH]
        dqn = _rope_t(dqa_ref[...], cq_ref[...], sq_ref[...])
        rq = _head_r(qa)
        nq = qa * rq
        dnq = dqn * gq_ref[...]
        dp[:, 0:Q_WIDTH] = rq * (dnq - nq * (_seg64_sum(dnq * nq) * (1.0 / HEAD_DIM)))
        dgq_ref[0] = jnp.sum(dqn * nq, axis=0, keepdims=True)

        ka = raw_ref[:, Q_WIDTH:QK_RAW]
        dkn = _rope_t(dkta_ref[0].T, ck_ref[...], sk_ref[...])
        rk = _head_r(ka)
        nk = ka * rk
        dnk = dkn * gk_ref[...]
        dp[:, 512:640] = rk * (dnk - nk * (_seg64_sum(dnk * nk) * (1.0 / HEAD_DIM)))
        dgk_ref[0] = jnp.sum(dkn * nk, axis=0, keepdims=True)

        dp[:, 640:768] = dvta_ref[0].T
        dp[:, 768:1280] = dqb_ref[...] * SCALE
        for j in range(nblk):
            dp[j * BLOCK:(j + 1) * BLOCK, 1280:1408] = dktb_ref[j].T
            dp[j * BLOCK:(j + 1) * BLOCK, 1408:1536] = dvtb_ref[j].T

        dproj = dp[...].astype(MM)
        dproj_ref[...] = dproj
        dh1 = _dot_nt(dproj[:, 0:IN_CHUNK], w_ref[0])
        for j in range(1, N_CHIPS):
            dh1 = dh1 + _dot_nt(dproj[:, j * IN_CHUNK:(j + 1) * IN_CHUNK], w_ref[j])
        xv = x_ref[...]
        dxn, dg1v = _rms_bwd(xv, _rms_r(xv), g1_ref[...], dh1)
        gx_ref[...] = dx1_ref[...] + dxn
        dg1_ref[0] = jnp.sum(dg1v, axis=0, keepdims=True)

    tok = lambda w: pl.BlockSpec((tm, w), lambda i: (i, 0))
    tab = lambda w: pl.BlockSpec((tm, w), lambda i: (i % n_seq, 0))
    row = lambda w: pl.BlockSpec((1, w), lambda i: (0, 0))
    tposed = pl.BlockSpec((1, KV_WIDTH, tm), lambda i: (i // n_seq, 0, i % n_seq))
    blocks = pl.BlockSpec((nblk, KV_WIDTH, BLOCK), lambda i: (i, 0, 0))
    part = lambda w: pl.BlockSpec((1, 1, w), lambda i: (i, 0, 0))
    return pl.pallas_call(
        body, name="proj_bwd",
        grid=(nt,),
        in_specs=[tok(Q_WIDTH), tposed, tposed, tok(Q_WIDTH), blocks, blocks, tok(QK_RAW), tok(D_MODEL),
                  tok(D_MODEL), row(D_MODEL),
                  pl.BlockSpec((N_CHIPS, D_MODEL, IN_CHUNK), lambda i: (0, 0, 0)),
                  row(Q_WIDTH), row(KV_WIDTH), tab(Q_WIDTH), tab(Q_WIDTH), tab(KV_WIDTH), tab(KV_WIDTH)],
        out_specs=[tok(D_MODEL), tok(IN_TOTAL), part(D_MODEL), part(Q_WIDTH), part(KV_WIDTH)],
        out_shape=[jax.ShapeDtypeStruct((tokens, D_MODEL), F32),
                   jax.ShapeDtypeStruct((tokens, IN_TOTAL), MM),
                   jax.ShapeDtypeStruct((nt, 1, D_MODEL), F32),
                   jax.ShapeDtypeStruct((nt, 1, Q_WIDTH), F32),
                   jax.ShapeDtypeStruct((nt, 1, KV_WIDTH), F32)],
        scratch_shapes=[pltpu.VMEM((tm, IN_TOTAL), F32)],
        compiler_params=_cparams(("parallel",), 56),
    )(dqa, dkta, dvta, dqb, dktb, dvtb, raw, x, dx1, g1, w_in, gq, gk, cq, sq, ck, sk)


def _pack_small(dg1, dg2, dg3, dg4, dgq, dgk, dsink, dbias, bucket, loss):
    def body(dg1_ref, dg2_ref, dg3_ref, dg4_ref, dgq_ref, dgk_ref, dsink_ref, dbias_ref, bucket_ref, loss_ref,
             out_ref):
        out_ref[...] = jnp.zeros_like(out_ref)
        for r, ref in ((ROW_G1, dg1_ref), (ROW_G2, dg2_ref), (ROW_G3, dg3_ref), (ROW_G4, dg4_ref)):
            acc = ref[0]
            for t in range(1, ref.shape[0]):
                acc = acc + ref[t]
            out_ref[r:r + 1, :] = acc

        def fold(ref, heads):
            acc = ref[0]
            for t in range(1, ref.shape[0]):
                acc = acc + ref[t]
            tot = acc[:, 0:HEAD_DIM]
            for h in range(1, heads):
                tot = tot + acc[:, h * HEAD_DIM:(h + 1) * HEAD_DIM]
            return tot

        out_ref[ROW_MISC:ROW_MISC + 1, MISC_GQ:MISC_GQ + HEAD_DIM] = fold(dgq_ref, GROUP * N_KV)
        out_ref[ROW_MISC:ROW_MISC + 1, MISC_GK:MISC_GK + HEAD_DIM] = fold(dgk_ref, N_KV)
        for h in range(GROUP * N_KV):
            out_ref[ROW_MISC:ROW_MISC + 1, MISC_SINK + h:MISC_SINK + h + 1] = (
                dsink_ref[h // GROUP, h % GROUP:h % GROUP + 1, 0:1])
        lacc = loss_ref[0, 0:1, 0:1]
        for t in range(1, loss_ref.shape[0]):
            lacc = lacc + loss_ref[t, 0:1, 0:1]
        out_ref[ROW_MISC:ROW_MISC + 1, MISC_LOSS:MISC_LOSS + 1] = lacc
        heads = GROUP * N_KV
        lane = lax.broadcasted_iota(jnp.int32, (heads, N_BUCKETS), 1)
        head = lax.broadcasted_iota(jnp.int32, (heads, N_BUCKETS), 0)

        def per_bucket(b, acc):
            for h in range(heads):
                tot = jnp.zeros((1, 1), F32)
                for piece in range(3):
                    sel = jnp.where(bucket_ref[piece] == b, dbias_ref[h, piece], 0.0)
                    tot = tot + jnp.sum(jnp.sum(sel, axis=-1, keepdims=True), axis=0, keepdims=True)
                acc = jnp.where((lane == b) & (head == h), tot, acc)
            return acc

        drel = lax.fori_loop(0, N_BUCKETS, per_bucket, jnp.zeros((heads, N_BUCKETS), F32))
        for h in range(heads):
            out_ref[ROW_MISC:ROW_MISC + 1, MISC_REL + h * N_BUCKETS:MISC_REL + (h + 1) * N_BUCKETS] = drel[h:h + 1, :]

    return pl.pallas_call(
        body, name="pack_small",
        in_specs=[VMEM] * 10, out_specs=VMEM,
        out_shape=jax.ShapeDtypeStruct((8, D_MODEL), F32),
        compiler_params=pltpu.CompilerParams(vmem_limit_bytes=32 * 1024 * 1024),
    )(dg1, dg2, dg3, dg4, dgq, dgk, dsink, dbias, bucket, loss)


def _place():
    x, y, c = lax.axis_index("x"), lax.axis_index("y"), lax.axis_index("c")
    return x, y, c


_CHIP_FLIPS = ((1, 0), (0, 1), (1, 1))


def _flip(v, bit):
    return 1 - v if bit else v


def _gather_weights(shards):
    n = len(shards)

    def body(*refs):
        ins, outs = refs[:n], refs[n:2 * n]
        stage = refs[2 * n:3 * n]
        local_sem, ici_send, ici_recv, d2d_send, d2d_recv = refs[3 * n:]
        x, y, c = _place()
        k = 2 * x + y
        sibling = (x, y, 1 - c)
        copies = []
        for t in range(n):
            stage[t][...] = ins[t][...].astype(MM)
            mine = pltpu.make_async_copy(stage[t], outs[t].at[k], local_sem.at[t])
            mine.start()
            copies.append(mine)
        sends = []
        for t in range(n):
            half = ins[t].shape[0] // 2
            rows = pl.ds(c * half, half)
            for r, (fx, fy) in enumerate(_CHIP_FLIPS):
                cp = pltpu.make_async_remote_copy(
                    src_ref=stage[t].at[rows], dst_ref=outs[t].at[k, rows],
                    send_sem=ici_send.at[t, r], recv_sem=ici_recv.at[t, r],
                    device_id=(_flip(x, fx), _flip(y, fy), c), device_id_type=MESH)
                cp.start()
                sends.append(cp)
        for t in range(n):
            half = ins[t].shape[0] // 2
            rows = pl.ds(c * half, half)
            for r, (fx, fy) in enumerate(_CHIP_FLIPS):
                kk = 2 * _flip(x, fx) + _flip(y, fy)
                landed = outs[t].at[kk, rows]
                pltpu.make_async_remote_copy(
                    src_ref=landed, dst_ref=landed, send_sem=ici_send.at[t, r], recv_sem=ici_recv.at[t, r],
                    device_id=sibling, device_id_type=MESH).wait_recv()
                fwd = pltpu.make_async_remote_copy(
                    src_ref=landed, dst_ref=landed, send_sem=d2d_send.at[t, r], recv_sem=d2d_recv.at[t, r],
                    device_id=sibling, device_id_type=MESH)
                fwd.start()
                sends.append(fwd)
        for t in range(n):
            half = ins[t].shape[0] // 2
            other = pl.ds((1 - c) * half, half)
            for r, (fx, fy) in enumerate(_CHIP_FLIPS):
                kk = 2 * _flip(x, fx) + _flip(y, fy)
                theirs = outs[t].at[kk, other]
                pltpu.make_async_remote_copy(
                    src_ref=theirs, dst_ref=theirs, send_sem=d2d_send.at[t, r], recv_sem=d2d_recv.at[t, r],
                    device_id=sibling, device_id_type=MESH).wait_recv()
        for cp in sends:
            cp.wait_send()
        for cp in copies:
            cp.wait()

    return pl.pallas_call(
        body, name="gather_weights",
        in_specs=[VMEM] * n, out_specs=[HBM] * n,
        out_shape=[jax.ShapeDtypeStruct((N_CHIPS,) + s.shape, MM) for s in shards],
        scratch_shapes=[pltpu.VMEM(s.shape, MM) for s in shards] + [
            pltpu.SemaphoreType.DMA((n,)),
            pltpu.SemaphoreType.DMA((n, 3)), pltpu.SemaphoreType.DMA((n, 3)),
            pltpu.SemaphoreType.DMA((n, 3)), pltpu.SemaphoreType.DMA((n, 3))],
        compiler_params=pltpu.CompilerParams(vmem_limit_bytes=40 * 1024 * 1024),
    )(*shards)


def _swap_halves(grads):
    n = len(grads)

    def body(*refs):
        ins, outs = refs[:n], refs[n:2 * n]
        send_sem, recv_sem = refs[2 * n:]
        x, y, c = _place()
        sibling = (x, y, 1 - c)
        cps = []
        for t in range(n):
            half = ins[t].shape[1] // 2
            cp = pltpu.make_async_remote_copy(
                src_ref=ins[t].at[:, pl.ds((1 - c) * half, half), :], dst_ref=outs[t],
                send_sem=send_sem.at[t], recv_sem=recv_sem.at[t], device_id=sibling, device_id_type=MESH)
            cp.start()
            cps.append(cp)
        for cp in cps:
            cp.wait()

    return pl.pallas_call(
        body, name="swap_halves",
        in_specs=[HBM] * n, out_specs=[HBM] * n,
        out_shape=[jax.ShapeDtypeStruct((g.shape[0], g.shape[1] // 2, g.shape[2]), F32) for g in grads],
        scratch_shapes=[pltpu.SemaphoreType.DMA((n,)), pltpu.SemaphoreType.DMA((n,))],
    )(*grads)


def _add_half(grad, got, core, *, name, tr):
    nch, half, cols = got.shape
    nblk = half // tr

    def body(core_ref, g_ref, r_ref, o_ref):
        o_ref[...] = g_ref[...] + r_ref[...]

    return pl.pallas_call(
        body, name=name,
        grid_spec=pltpu.PrefetchScalarGridSpec(
            num_scalar_prefetch=1, grid=(nch, nblk),
            in_specs=[pl.BlockSpec((1, tr, cols), lambda j, i, core_ref: (j, core_ref[0] * nblk + i, 0)),
                      pl.BlockSpec((1, tr, cols), lambda j, i, core_ref: (j, i, 0))],
            out_specs=pl.BlockSpec((1, tr, cols), lambda j, i, core_ref: (j, i, 0))),
        out_shape=jax.ShapeDtypeStruct(got.shape, F32),
        compiler_params=_cparams(("parallel", "parallel"), 32),
    )(core, grad, got)


def _exchange_chips(sums):
    n = len(sums)

    def body(*refs):
        ins, outs = refs[:n], refs[n:2 * n]
        send_sem, recv_sem = refs[2 * n:]
        x, y, c = _place()
        cps = []
        for t in range(n):
            for r, (fx, fy) in enumerate(_CHIP_FLIPS):
                kk = 2 * _flip(x, fx) + _flip(y, fy)
                cp = pltpu.make_async_remote_copy(
                    src_ref=ins[t].at[kk], dst_ref=outs[t].at[r],
                    send_sem=send_sem.at[t, r], recv_sem=recv_sem.at[t, r],
                    device_id=(_flip(x, fx), _flip(y, fy), c), device_id_type=MESH)
                cp.start()
                cps.append(cp)
        for cp in cps:
            cp.wait()

    return pl.pallas_call(
        body, name="exchange_chips",
        in_specs=[HBM] * n, out_specs=[HBM] * n,
        out_shape=[jax.ShapeDtypeStruct((3,) + s.shape[1:], F32) for s in sums],
        scratch_shapes=[pltpu.SemaphoreType.DMA((n, 3)), pltpu.SemaphoreType.DMA((n, 3))],
    )(*sums)


def _add_chips(own, got, chip, *, name, tr):
    _, half, cols = own.shape

    def body(chip_ref, o_ref, g_ref, out_ref):
        out_ref[...] = ((o_ref[0] + g_ref[0]) + g_ref[1]) + g_ref[2]

    return pl.pallas_call(
        body, name=name,
        grid_spec=pltpu.PrefetchScalarGridSpec(
            num_scalar_prefetch=1, grid=(half // tr,),
            in_specs=[pl.BlockSpec((1, tr, cols), lambda i, chip_ref: (chip_ref[0], i, 0)),
                      pl.BlockSpec((3, tr, cols), lambda i, chip_ref: (0, i, 0))],
            out_specs=pl.BlockSpec((tr, cols), lambda i, chip_ref: (i, 0))),
        out_shape=jax.ShapeDtypeStruct((half, cols), F32),
        compiler_params=_cparams(("parallel",), 32),
    )(chip, own, got)


def _join_halves(halves):
    n = len(halves)

    def body(*refs):
        ins, outs = refs[:n], refs[n:2 * n]
        local_sem, send_sem, recv_sem = refs[2 * n:]
        x, y, c = _place()
        sibling = (x, y, 1 - c)
        cps = []
        for t in range(n):
            half = ins[t].shape[0]
            rows = pl.ds(c * half, half)
            mine = pltpu.make_async_copy(ins[t], outs[t].at[rows], local_sem.at[t])
            mine.start()
            cp = pltpu.make_async_remote_copy(
                src_ref=ins[t], dst_ref=outs[t].at[rows], send_sem=send_sem.at[t], recv_sem=recv_sem.at[t],
                device_id=sibling, device_id_type=MESH)
            cp.start()
            cps += [mine, cp]
        for cp in cps:
            cp.wait()

    return pl.pallas_call(
        body, name="join_halves",
        in_specs=[HBM] * n, out_specs=[HBM] * n,
        out_shape=[jax.ShapeDtypeStruct((2 * h.shape[0], h.shape[1]), F32) for h in halves],
        scratch_shapes=[pltpu.SemaphoreType.DMA((n,)), pltpu.SemaphoreType.DMA((n,)),
                        pltpu.SemaphoreType.DMA((n,))],
    )(*halves)


def _gather_small(packed):
    def body(in_ref, out_ref, local_sem, send_sem, recv_sem):
        x, y, c = _place()
        me = 4 * x + 2 * y + c
        mine = pltpu.make_async_copy(in_ref, out_ref.at[me], local_sem)
        mine.start()
        cps = []
        for r in range(1, N_DEV):
            fx, fy, fc = (r >> 2) & 1, (r >> 1) & 1, r & 1
            cp = pltpu.make_async_remote_copy(
                src_ref=in_ref, dst_ref=out_ref.at[me], send_sem=send_sem.at[r - 1], recv_sem=recv_sem.at[r - 1],
                device_id=(_flip(x, fx), _flip(y, fy), _flip(c, fc)), device_id_type=MESH)
            cp.start()
            cps.append(cp)
        for cp in cps:
            cp.wait()
        mine.wait()

    return pl.pallas_call(
        body, name="gather_small",
        in_specs=[VMEM], out_specs=VMEM,
        out_shape=jax.ShapeDtypeStruct((N_DEV,) + packed.shape, F32),
        scratch_shapes=[pltpu.SemaphoreType.DMA, pltpu.SemaphoreType.DMA((N_DEV - 1,)),
                        pltpu.SemaphoreType.DMA((N_DEV - 1,))],
    )(packed)


def _adamw_math(w, g, m, v):
    m = ADAM_B1 * m + (1.0 - ADAM_B1) * g
    v = ADAM_B2 * v + (1.0 - ADAM_B2) * (g * g)
    m_hat = m / (1.0 - ADAM_B1 ** ADAM_STEP)
    v_hat = v / (1.0 - ADAM_B2 ** ADAM_STEP)
    delta = -ADAM_LR * (m_hat / (jnp.sqrt(v_hat) + ADAM_EPS) + ADAM_WD * w)
    return delta, m, v


def _adamw(w, g, m, v, *, name, tr):
    rows, cols = w.shape

    def body(w_ref, g_ref, m_ref, v_ref, d_ref, nm_ref, nv_ref):
        d_ref[...], nm_ref[...], nv_ref[...] = _adamw_math(w_ref[...], g_ref[...], m_ref[...], v_ref[...])

    spec = pl.BlockSpec((tr, cols), lambda i: (i, 0))
    return pl.pallas_call(
        body, name=name,
        grid=(rows // tr,),
        in_specs=[spec] * 4, out_specs=[spec] * 3,
        out_shape=[jax.ShapeDtypeStruct(w.shape, F32)] * 3,
        compiler_params=_cparams(("parallel",), 32),
    )(w, g, m, v)


def _small_adamw(gathered, w_rows, m_rows, v_rows):
    def body(all_ref, w_ref, m_ref, v_ref, g_ref, d_ref, nm_ref, nv_ref):
        g = all_ref[0]
        for d in range(1, N_DEV):
            g = g + all_ref[d]
        g_ref[...] = g
        d_ref[...], nm_ref[...], nv_ref[...] = _adamw_math(w_ref[...], g, m_ref[...], v_ref[...])

    return pl.pallas_call(
        body, name="small_adamw",
        in_specs=[VMEM] * 4, out_specs=[VMEM] * 4,
        out_shape=[jax.ShapeDtypeStruct(w_rows.shape, F32)] * 4,
    )(gathered, w_rows, m_rows, v_rows)


def _pack_rows(g1, g2, g3, g4, gq, gk, sink, rel):
    misc = jnp.zeros((1, D_MODEL), F32)
    misc = misc.at[:, MISC_GQ:MISC_GQ + HEAD_DIM].set(gq)
    misc = misc.at[:, MISC_GK:MISC_GK + HEAD_DIM].set(gk)
    misc = misc.at[:, MISC_SINK:MISC_SINK + 8].set(sink)
    misc = misc.at[:, MISC_REL:MISC_REL + 8 * N_BUCKETS].set(rel.T.reshape(1, 8 * N_BUCKETS))
    return jnp.concatenate([g1, g2, g3, g4, misc, jnp.zeros((3, D_MODEL), F32)], axis=0)


def _unpack_rows(t):
    misc = t[ROW_MISC:ROW_MISC + 1]
    rel = misc[:, MISC_REL:MISC_REL + 8 * N_BUCKETS].reshape(8, N_BUCKETS).T
    return dict(g1=t[ROW_G1:ROW_G1 + 1], g2=t[ROW_G2:ROW_G2 + 1], g3=t[ROW_G3:ROW_G3 + 1], g4=t[ROW_G4:ROW_G4 + 1],
                gq=misc[:, MISC_GQ:MISC_GQ + HEAD_DIM], gk=misc[:, MISC_GK:MISC_GK + HEAD_DIM],
                sink=misc[:, MISC_SINK:MISC_SINK + 8], rel=rel)


def _local_step(x, target, w_in, w_o, w_up, w_down, g1, g2, gq, gk, sink, rel_bias, g3, g4, *, seq):
    tokens = x.shape[0]
    cos, sin = _rope_tables(seq)
    cq, sq = jnp.tile(cos, (1, 8)) * SCALE, jnp.tile(sin, (1, 8)) * SCALE
    ck, sk = jnp.tile(cos, (1, 2)), jnp.tile(sin, (1, 2))
    gq8, gk2 = jnp.tile(gq, (1, 8)), jnp.tile(gk, (1, 2))
    bucket, band = _window_tables()
    bias = _bias_build(rel_bias.T, bucket, band)

    (h1, raw, qa, ka, kta, va, vta, qb, kb, ktb, vb, vtb) = _pre_proj(
        x, g1, w_in, gq8, gk2, cq, sq, ck, sk, seq=seq, tm=min(512, seq))
    oa, lse_a = _attn_a_fwd(qa, kta, va, seq=seq, bq=min(256, seq))
    kb3 = kb.reshape(tokens // BLOCK, BLOCK, KV_WIDTH)
    vb3 = vb.reshape(tokens // BLOCK, BLOCK, KV_WIDTH)
    ob, lse_b = _attn_b_fwd(qb, ktb, vb3, bias, sink, seq=seq)
    mix, x1, h2 = _wo_post(oa, ob, w_o, x, g2, g3, tm=512)
    u, df, dy, loss_t, dg4 = _ffn_fwd_loss(h2, w_up, w_down, x1, target, g4, tm=512)

    dz, dx1, dmix, dg3, dg2 = _ffn_bwd_act(df, w_down, u, w_up, x1, dy, mix, g3, g2, tm=512)
    gw_down = _tn_matmul(u, df, name="grad_w_down", tm=1024, tn=1024, tk=512, square_a=True)
    gw_up = _tn_matmul(h2, dz, name="grad_w_up", tm=1024, tn=1024, tk=512, chunked=True)
    doa, dob = _wo_bwd(dmix, w_o, tm=512)
    gw_o = jnp.concatenate([
        _tn_matmul(oa, dmix, name="grad_w_o_a", tm=512, tn=1024, tk=512),
        _tn_matmul(ob, dmix, name="grad_w_o_b", tm=512, tn=1024, tk=512)], axis=0)
    dqa, dkta, dvta = _attn_a_bwd(qa, kta, ka, vta, doa, oa, lse_a, seq=seq, bq=min(256, seq))
    dqb, dktb, dvtb, dbias, dsink = _attn_b_bwd(qb, ktb, kb3, vtb, dob, ob, lse_b, bias, sink, seq=seq)
    grad_x, dproj, dg1, dgq, dgk = _proj_bwd(
        dqa, dkta, dvta, dqb, dktb, dvtb, raw, x, dx1, g1, w_in, gq8, gk2, cq, sq, ck, sk,
        seq=seq, tm=min(512, seq))
    gw_in = _tn_matmul(h1, dproj, name="grad_w_in", tm=1024, tn=IN_CHUNK, tk=512, chunked=True)
    packed = _pack_small(dg1, dg2, dg3, dg4, dgq, dgk, dsink, dbias, bucket, loss_t)
    return grad_x, gw_in, gw_o, gw_up, gw_down, packed


def kernel(x, w_in, w_o, g_pre_mix, g_post_mix, q_norm_a, k_norm_a, sink_b, rel_bias, g_pre_ffn, w_ffn_up, w_ffn_down, g_post_ffn, loss_target, m_w_in, m_w_o, m_g_pre_mix, m_g_post_mix, m_q_norm_a, m_k_norm_a, m_sink_b, m_rel_bias, m_g_pre_ffn, m_w_ffn_up, m_w_ffn_down, m_g_post_ffn, v_w_in, v_w_o, v_g_pre_mix, v_g_post_mix, v_q_norm_a, v_k_norm_a, v_sink_b, v_rel_bias, v_g_pre_ffn, v_w_ffn_up, v_w_ffn_down, v_g_post_ffn):
    batch, seq, _ = x.shape
    tokens = batch * seq
    core = lax.axis_index("c").astype(jnp.int32).reshape(1)
    chip = (2 * lax.axis_index("x") + lax.axis_index("y")).astype(jnp.int32).reshape(1)

    shards = (w_in[0], w_o[0], w_ffn_up[0], w_ffn_down[0])
    w_in_g, w_o_g, w_up_g, w_down_g = _gather_weights(shards)

    grad_x, gw_in, gw_o, gw_up, gw_down, packed = _local_step(
        x.reshape(tokens, D_MODEL), loss_target.reshape(tokens, D_MODEL),
        w_in_g, w_o_g.reshape(D_MODEL, D_MODEL), w_up_g, w_down_g.reshape(D_FF, D_MODEL),
        g_pre_mix, g_post_mix, q_norm_a, k_norm_a, sink_b, rel_bias, g_pre_ffn, g_post_ffn, seq=seq)

    local = (gw_in, gw_o.reshape(N_CHIPS, O_CHUNK, D_MODEL), gw_up, gw_down.reshape(N_CHIPS, FF_CHUNK, D_MODEL))
    names = ("w_in", "w_o", "w_up", "w_down")
    got = _swap_halves(local)
    sums = [_add_half(g, r, core, name="add_half_" + nm, tr=min(128, r.shape[1]))
            for g, r, nm in zip(local, got, names)]
    got = _exchange_chips(sums)
    halves = [_add_chips(s, r, chip, name="add_chips_" + nm, tr=min(128, s.shape[1]))
              for s, r, nm in zip(sums, got, names)]
    g_in, g_o, g_up, g_down = _join_halves(halves)

    big = []
    for nm, w, g, m, v in (("w_in", w_in, g_in, m_w_in, v_w_in), ("w_o", w_o, g_o, m_w_o, v_w_o),
                           ("w_up", w_ffn_up, g_up, m_w_ffn_up, v_w_ffn_up),
                           ("w_down", w_ffn_down, g_down, m_w_ffn_down, v_w_ffn_down)):
        d, nm_, nv_ = _adamw(w[0], g, m[0], v[0], name="adamw_" + nm, tr=128)
        big.append((g[None], d[None], nm_[None], nv_[None]))

    gathered = _gather_small(packed)
    w_rows = _pack_rows(g_pre_mix, g_post_mix, g_pre_ffn, g_post_ffn, q_norm_a, k_norm_a, sink_b, rel_bias)
    m_rows = _pack_rows(m_g_pre_mix, m_g_post_mix, m_g_pre_ffn, m_g_post_ffn, m_q_norm_a, m_k_norm_a, m_sink_b,
                        m_rel_bias)
    v_rows = _pack_rows(v_g_pre_mix, v_g_post_mix, v_g_pre_ffn, v_g_post_ffn, v_q_norm_a, v_k_norm_a, v_sink_b,
                        v_rel_bias)
    g_rows, d_rows, nm_rows, nv_rows = _small_adamw(gathered, w_rows, m_rows, v_rows)
    loss = g_rows[ROW_MISC, MISC_LOSS]
    small = [_unpack_rows(t) for t in (g_rows, d_rows, nm_rows, nv_rows)]

    def leaves(i):
        s = small[i]
        return (big[0][i], big[1][i], s["g1"], s["g2"], s["gq"], s["gk"], s["sink"], s["rel"], s["g3"],
                big[2][i], big[3][i], s["g4"])

    return (loss, grad_x.reshape(batch, seq, D_MODEL), *leaves(0), *leaves(1), *leaves(2), *leaves(3))
```

```python
import functools

import jax
import jax.numpy as jnp
import numpy as np
from jax import lax
from jax.experimental import pallas as pl
from jax.experimental.pallas import tpu as pltpu

F32 = jnp.float32
MM = jnp.bfloat16

D_MODEL = 1024
HEAD_DIM = 64
N_KV = 2
GROUP = 4
Q_WIDTH = 512
KV_WIDTH = 128
D_FF = 4096
GRID_W = 64
BLOCK = 128
N_BUCKETS = 32
MAX_DISTANCE = 128
ROPE_THETA = 10000.0
EPS = 1e-6
NEG_INF = -1e30
SCALE = HEAD_DIM ** -0.5
IN_TOTAL = 1536
N_CHIPS = 4
N_DEV = 8
IN_CHUNK = IN_TOTAL // N_CHIPS
FF_CHUNK = D_FF // N_CHIPS
O_CHUNK = D_MODEL // N_CHIPS
QK_RAW = 640

ADAM_LR = 0.001
ADAM_B1 = 0.9
ADAM_B2 = 0.999
ADAM_EPS = 1e-08
ADAM_WD = 0.01
ADAM_STEP = 10

LANES = 128
MESH = pl.DeviceIdType.MESH
HBM = pl.BlockSpec(memory_space=pl.ANY)
VMEM = pl.BlockSpec(memory_space=pltpu.VMEM)
SMEM = pl.BlockSpec(memory_space=pltpu.SMEM)

ROW_G1, ROW_G2, ROW_G3, ROW_G4, ROW_MISC = 0, 1, 2, 3, 4
MISC_GQ, MISC_GK, MISC_SINK, MISC_REL, MISC_LOSS = 0, 64, 128, 256, 512


def _cparams(sem, vmem_mb):
    return pltpu.CompilerParams(dimension_semantics=sem, vmem_limit_bytes=vmem_mb * 1024 * 1024)


class _Job:
    def __init__(self, operands, out_shapes, sems, copies, alias=None):
        self.operands, self.out_shapes, self.sems, self.copies = list(operands), list(out_shapes), list(sems), copies
        self.alias = dict(alias or {})


def _place():
    return lax.axis_index("x"), lax.axis_index("y"), lax.axis_index("c")


_CHIP_FLIPS = ((1, 0), (0, 1), (1, 1))


def _flip(v, bit):
    return 1 - v if bit else v


def _remote(src, dst, send, recv, dev):
    return pltpu.make_async_remote_copy(src_ref=src, dst_ref=dst, send_sem=send, recv_sem=recv,
                                        device_id=dev, device_id_type=MESH)


def _swap_job(grads):
    n = len(grads)

    def copies(ins, outs, sems):
        x, y, c = _place()
        send, recv = sems
        cps = []
        for t in range(n):
            half = ins[t].shape[1] // 2
            cps.append(_remote(ins[t].at[:, pl.ds((1 - c) * half, half), :], outs[t], send.at[t], recv.at[t],
                               (x, y, 1 - c)))
        return cps

    shapes = [jax.ShapeDtypeStruct((g.shape[0], g.shape[1] // 2, g.shape[2]), F32) for g in grads]
    return _Job(grads, shapes, [pltpu.SemaphoreType.DMA((n,)), pltpu.SemaphoreType.DMA((n,))], copies)


def _exchange_job(sums):
    n = len(sums)

    def copies(ins, outs, sems):
        x, y, c = _place()
        send, recv = sems
        cps = []
        for t in range(n):
            for r, (fx, fy) in enumerate(_CHIP_FLIPS):
                kk = 2 * _flip(x, fx) + _flip(y, fy)
                cps.append(_remote(ins[t].at[kk], outs[t].at[r], send.at[t, r], recv.at[t, r],
                                   (_flip(x, fx), _flip(y, fy), c)))
        return cps

    shapes = [jax.ShapeDtypeStruct((3,) + s.shape[1:], F32) for s in sums]
    return _Job(sums, shapes, [pltpu.SemaphoreType.DMA((n, 3)), pltpu.SemaphoreType.DMA((n, 3))], copies)


def _join_job(fulls):
    n = len(fulls)

    def copies(ins, outs, sems):
        x, y, c = _place()
        send, recv = sems
        cps = []
        for t in range(n):
            half = ins[t].shape[0] // 2
            rows = pl.ds(c * half, half)
            cps.append(_remote(ins[t].at[rows], outs[t].at[rows], send.at[t], recv.at[t], (x, y, 1 - c)))
        return cps

    shapes = [jax.ShapeDtypeStruct(f.shape, f.dtype) for f in fulls]
    return _Job(fulls, shapes, [pltpu.SemaphoreType.DMA((n,)), pltpu.SemaphoreType.DMA((n,))], copies,
                alias={t: t for t in range(n)})


def _gather_job(bufs, forward):
    n = len(bufs)

    def copies(ins, outs, sems):
        x, y, c = _place()
        send, recv = sems
        cps = []
        for t in range(n):
            half = ins[t].shape[1] // 2
            rows = pl.ds(c * half, half)
            for r, (fx, fy) in enumerate(_CHIP_FLIPS):
                if forward:
                    kk = 2 * _flip(x, fx) + _flip(y, fy)
                    dev = (x, y, 1 - c)
                else:
                    kk = 2 * x + y
                    dev = (_flip(x, fx), _flip(y, fy), c)
                cps.append(_remote(ins[t].at[kk, rows], outs[t].at[kk, rows], send.at[t, r], recv.at[t, r], dev))
        return cps

    shapes = [jax.ShapeDtypeStruct(b.shape, b.dtype) for b in bufs]
    return _Job(bufs, shapes, [pltpu.SemaphoreType.DMA((n, 3)), pltpu.SemaphoreType.DMA((n, 3))], copies,
                alias={t: t for t in range(n)})


def _call(body, args, *, name, grid, in_specs, out_specs, out_shape, scratch_shapes=(), params=None, jobs=()):
    n_in, n_out, n_scr = len(in_specs), len(out_specs), len(scratch_shapes)
    job_in = [len(j.operands) for j in jobs]
    job_out = [len(j.out_shapes) for j in jobs]
    job_sem = [len(j.sems) for j in jobs]

    def wrapped(*refs):
        pos = 0
        ins = refs[pos:pos + n_in]; pos += n_in
        jins = []
        for k in job_in:
            jins.append(refs[pos:pos + k]); pos += k
        outs = refs[pos:pos + n_out]; pos += n_out
        jouts = []
        for k in job_out:
            jouts.append(refs[pos:pos + k]); pos += k
        scr = refs[pos:pos + n_scr]; pos += n_scr
        jsems = []
        for k in job_sem:
            jsems.append(refs[pos:pos + k]); pos += k
        if jobs:
            ids = [pl.program_id(d) for d in range(len(grid))]
            first = functools.reduce(jnp.logical_and, [i == 0 for i in ids])
            last = functools.reduce(jnp.logical_and, [i == g - 1 for i, g in zip(ids, grid)])

            @pl.when(first)
            def _():
                for j, ji, jo, js in zip(jobs, jins, jouts, jsems):
                    for cp in j.copies(ji, jo, js):
                        cp.start()

        body(*ins, *outs, *scr)
        if jobs:
            @pl.when(last)
            def _():
                for j, ji, jo, js in zip(jobs, jins, jouts, jsems):
                    for cp in j.copies(ji, jo, js):
                        cp.wait()

    aliases = {}
    in_pos, out_pos = n_in, n_out
    for j in jobs:
        for i, o in j.alias.items():
            aliases[in_pos + i] = out_pos + o
        in_pos += len(j.operands)
        out_pos += len(j.out_shapes)
    res = pl.pallas_call(
        wrapped, name=name, grid=grid,
        in_specs=list(in_specs) + [HBM] * sum(job_in),
        out_specs=list(out_specs) + [HBM] * sum(job_out),
        out_shape=list(out_shape) + [s for j in jobs for s in j.out_shapes],
        scratch_shapes=list(scratch_shapes) + [s for j in jobs for s in j.sems],
        input_output_aliases=aliases,
        compiler_params=params,
    )(*args, *[a for j in jobs for a in j.operands])
    own, rest = list(res[:n_out]), list(res[n_out:])
    job_res = []
    for k in job_out:
        job_res.append(rest[:k])
        rest = rest[k:]
    return own, job_res


def _dot(a, b):
    return jnp.dot(a, b, preferred_element_type=F32)


def _dot_nt(a, b):
    return lax.dot_general(a, b, (((1,), (1,)), ((), ())), preferred_element_type=F32)


def _dot_tn(a, b):
    return lax.dot_general(a, b, (((0,), (0,)), ((), ())), preferred_element_type=F32)


def _rms_r(x):
    return lax.rsqrt(jnp.mean(x * x, axis=-1, keepdims=True) + EPS)


def _rms_bwd(x, r, g, dy):
    n = x * r
    dn = dy * g
    dx = r * (dn - n * jnp.mean(dn * n, axis=-1, keepdims=True))
    return dx, dy * n


def _seg64_sum(v):
    rows, width = v.shape
    lane = lax.broadcasted_iota(jnp.int32, (rows, LANES), 1)
    lo = lane < HEAD_DIM
    outs = []
    for c in range(width // LANES):
        ch = v[:, c * LANES:(c + 1) * LANES]
        s_lo = jnp.sum(jnp.where(lo, ch, 0.0), axis=-1, keepdims=True)
        s_hi = jnp.sum(jnp.where(lo, 0.0, ch), axis=-1, keepdims=True)
        outs.append(jnp.where(lo, s_lo, s_hi))
    return outs[0] if len(outs) == 1 else jnp.concatenate(outs, axis=-1)


def _head_r(v):
    return lax.rsqrt(_seg64_sum(v * v) * (1.0 / HEAD_DIM) + EPS)


def _swap16(v):
    rows, width = v.shape
    lane = lax.broadcasted_iota(jnp.int32, (rows, LANES), 1)
    first = (lane % 32) < 16
    outs = []
    for c in range(width // LANES):
        ch = v[:, c * LANES:(c + 1) * LANES]
        outs.append(jnp.where(first, pltpu.roll(ch, LANES - 16, 1), pltpu.roll(ch, 16, 1)))
    return outs[0] if len(outs) == 1 else jnp.concatenate(outs, axis=-1)


def _rope(v, cos, sin_signed):
    return v * cos + _swap16(v) * sin_signed


def _rope_t(g, cos, sin_signed):
    return g * cos + _swap16(g * sin_signed)


def _rope_tables(seq):
    nf = HEAD_DIM // 4
    freqs = ROPE_THETA ** (-jnp.arange(nf, dtype=F32) / nf)
    pos = jnp.arange(seq, dtype=jnp.int32)
    row = (pos // GRID_W).astype(F32)
    col = (pos % GRID_W).astype(F32)
    ang_r = row[:, None] * freqs[None, :]
    ang_c = col[:, None] * freqs[None, :]
    cr, sr, cc, sc = jnp.cos(ang_r), jnp.sin(ang_r), jnp.cos(ang_c), jnp.sin(ang_c)
    cos = jnp.concatenate([cr, cr, cc, cc], axis=1)
    sin = jnp.concatenate([-sr, sr, -sc, sc], axis=1)
    return cos, sin


def _t5_bucket(rel):
    nb = N_BUCKETS // 2
    ret = (rel > 0).astype(jnp.int32) * nb
    n = jnp.abs(rel)
    max_exact = nb // 2
    nf = jnp.maximum(n, 1).astype(jnp.float32)
    large = max_exact + (jnp.log(nf / max_exact) / np.float32(np.log(MAX_DISTANCE / max_exact))
                         * (nb - max_exact)).astype(jnp.int32)
    large = jnp.minimum(large, nb - 1)
    return ret + jnp.where(n < max_exact, n, large)


def _window_tables():
    a = jnp.arange(BLOCK, dtype=jnp.int32)
    c = jnp.arange(3 * BLOCK, dtype=jnp.int32)
    rel = c[None, :] - BLOCK - a[:, None]
    bucket = _t5_bucket(rel)
    band = (jnp.abs(rel) <= BLOCK).astype(jnp.int32)
    to3 = lambda t: t.reshape(BLOCK, 3, BLOCK).transpose(1, 0, 2)
    return to3(bucket), to3(band)


def _pre_proj(x, g1, w_in, gq, gk, cq, sq, ck, sk, *, seq, tm):
    tokens = x.shape[0]
    n_seq = seq // tm
    nblk = tm // BLOCK
    batch = tokens // seq

    def body(x_ref, g1_ref, w_ref, gq_ref, gk_ref, cq_ref, sq_ref, ck_ref, sk_ref,
             h1_ref, raw_ref, qa_ref, ka_ref, kta_ref, va_ref, vta_ref,
             qb_ref, kb_ref, ktb_ref, vb_ref, vtb_ref, proj):
        xv = x_ref[...]
        h = (xv * _rms_r(xv) * g1_ref[...]).astype(MM)
        h1_ref[...] = h
        for j in range(N_CHIPS):
            proj[:, j * IN_CHUNK:(j + 1) * IN_CHUNK] = _dot(h, w_ref[j])
        qa = proj[:, 0:Q_WIDTH]
        ka = proj[:, Q_WIDTH:QK_RAW]
        raw_ref[...] = proj[:, 0:QK_RAW]
        qn = qa * _head_r(qa) * gq_ref[...]
        qa_ref[...] = _rope(qn, cq_ref[...], sq_ref[...]).astype(MM)
        kn = ka * _head_r(ka) * gk_ref[...]
        kr = _rope(kn, ck_ref[...], sk_ref[...])
        ka_ref[...] = kr.astype(MM)
        kta_ref[0] = kr.T.astype(MM)
        va = proj[:, 640:768]
        va_ref[...] = va.astype(MM)
        vta_ref[0] = va.T.astype(MM)
        qb_ref[...] = (proj[:, 768:1280] * SCALE).astype(MM)
        kb = proj[:, 1280:1408]
        vb = proj[:, 1408:1536]
        kb_ref[...] = kb.astype(MM)
        vb_ref[...] = vb.astype(MM)
        for j in range(nblk):
            ktb_ref[j] = kb[j * BLOCK:(j + 1) * BLOCK, :].T.astype(MM)
            vtb_ref[j] = vb[j * BLOCK:(j + 1) * BLOCK, :].T.astype(MM)

    tok = lambda w: pl.BlockSpec((tm, w), lambda i: (i, 0))
    tab = lambda w: pl.BlockSpec((tm, w), lambda i: (i % n_seq, 0))
    row = lambda w: pl.BlockSpec((1, w), lambda i: (0, 0))
    tposed = pl.BlockSpec((1, LANES, tm), lambda i: (i // n_seq, 0, i % n_seq))
    blocks = pl.BlockSpec((nblk, BLOCK, LANES), lambda i: (i, 0, 0))
    tok_mm = lambda w: jax.ShapeDtypeStruct((tokens, w), MM)
    return pl.pallas_call(
        body, name="pre_proj",
        grid=(tokens // tm,),
        in_specs=[tok(D_MODEL), row(D_MODEL),
                  pl.BlockSpec((N_CHIPS, D_MODEL, IN_CHUNK), lambda i: (0, 0, 0)),
                  row(Q_WIDTH), row(KV_WIDTH), tab(Q_WIDTH), tab(Q_WIDTH), tab(KV_WIDTH), tab(KV_WIDTH)],
        out_specs=[tok(D_MODEL), tok(QK_RAW), tok(Q_WIDTH), tok(KV_WIDTH), tposed, tok(KV_WIDTH), tposed,
                   tok(Q_WIDTH), tok(KV_WIDTH), blocks, tok(KV_WIDTH), blocks],
        out_shape=[
            tok_mm(D_MODEL),
            jax.ShapeDtypeStruct((tokens, QK_RAW), F32),
            tok_mm(Q_WIDTH),
            tok_mm(KV_WIDTH),
            jax.ShapeDtypeStruct((batch, KV_WIDTH, seq), MM),
            tok_mm(KV_WIDTH),
            jax.ShapeDtypeStruct((batch, KV_WIDTH, seq), MM),
            tok_mm(Q_WIDTH),
            tok_mm(KV_WIDTH),
            jax.ShapeDtypeStruct((tokens // BLOCK, KV_WIDTH, BLOCK), MM),
            tok_mm(KV_WIDTH),
            jax.ShapeDtypeStruct((tokens // BLOCK, KV_WIDTH, BLOCK), MM),
        ],
        scratch_shapes=[pltpu.VMEM((tm, IN_TOTAL), F32)],
        compiler_params=_cparams(("parallel",), 48),
    )(x, g1, w_in, gq, gk, cq, sq, ck, sk)


def _kv_half(v2, kv):
    return jnp.where(kv == 0, v2[:, :HEAD_DIM], v2[:, HEAD_DIM:])


def _attn_a_fwd(qa, kta, va, *, seq, bq, jobs=()):
    tokens = qa.shape[0]
    batch = tokens // seq
    nq = seq // bq

    def body(q_ref, kt_ref, v_ref, o_ref, lse_ref):
        kv = pl.program_id(1)
        kt = kt_ref[0]
        v = v_ref[...]
        for g in range(GROUP):
            sl = slice(g * HEAD_DIM, (g + 1) * HEAD_DIM)
            s = _dot(q_ref[:, sl], kt)
            m = jnp.max(s, axis=-1, keepdims=True)
            p = jnp.exp(s - m)
            l = jnp.sum(p, axis=-1, keepdims=True)
            o2 = _dot(p.astype(MM), v)
            o_ref[:, sl] = _kv_half(o2, kv) / l
            lse_ref[0, :, g:g + 1] = m + jnp.log(l)

    return _call(
        body, (qa, kta, va), name="attn_a_fwd", jobs=jobs,
        grid=(batch, N_KV, nq),
        in_specs=[pl.BlockSpec((bq, GROUP * HEAD_DIM), lambda b, k, i: (b * nq + i, k)),
                  pl.BlockSpec((1, HEAD_DIM, seq), lambda b, k, i: (b, k, 0)),
                  pl.BlockSpec((seq, KV_WIDTH), lambda b, k, i: (b, 0))],
        out_specs=[pl.BlockSpec((bq, GROUP * HEAD_DIM), lambda b, k, i: (b * nq + i, k)),
                   pl.BlockSpec((1, bq, GROUP), lambda b, k, i: (k, b * nq + i, 0))],
        out_shape=[jax.ShapeDtypeStruct((tokens, Q_WIDTH), F32),
                   jax.ShapeDtypeStruct((N_KV, tokens, GROUP), F32)],
        params=_cparams(("arbitrary", "arbitrary", "arbitrary"), 48))


def _attn_a_bwd(qa, kta, ka, vta, do, o, lse, *, seq, bq, jobs=()):
    tokens = qa.shape[0]
    batch = tokens // seq
    nq = seq // bq

    def body(q_ref, kt_ref, k_ref, vt_ref, do_ref, o_ref, lse_ref, dq_ref, dkt_ref, dvt_ref):
        kv = pl.program_id(1)

        @pl.when(pl.program_id(2) == 0)
        def _():
            dkt_ref[...] = jnp.zeros_like(dkt_ref)
            dvt_ref[...] = jnp.zeros_like(dvt_ref)

        kt = kt_ref[0]
        vt = vt_ref[0]
        k2 = k_ref[...]
        for g in range(GROUP):
            sl = slice(g * HEAD_DIM, (g + 1) * HEAD_DIM)
            q = q_ref[:, sl]
            dof = do_ref[:, sl]
            delta = jnp.sum(dof * o_ref[:, sl], axis=-1, keepdims=True)
            dob = dof.astype(MM)
            p = jnp.exp(_dot(q, kt) - lse_ref[0, :, g:g + 1])
            dp = _dot(dob, vt)
            ds = (p * (dp - delta)).astype(MM)
            dq_ref[:, sl] = _kv_half(_dot(ds, k2), kv)
            dkt_ref[0] += _dot_tn(q, ds)
            dvt_ref[0] += _dot_tn(dob, p.astype(MM))

    qspec = pl.BlockSpec((bq, GROUP * HEAD_DIM), lambda b, k, i: (b * nq + i, k))
    tspec = pl.BlockSpec((1, HEAD_DIM, seq), lambda b, k, i: (b, k, 0))
    return _call(
        body, (qa, kta, ka, vta, do, o, lse), name="attn_a_bwd", jobs=jobs,
        grid=(batch, N_KV, nq),
        in_specs=[qspec, tspec, pl.BlockSpec((seq, KV_WIDTH), lambda b, k, i: (b, 0)), tspec, qspec, qspec,
                  pl.BlockSpec((1, bq, GROUP), lambda b, k, i: (k, b * nq + i, 0))],
        out_specs=[qspec, tspec, tspec],
        out_shape=[jax.ShapeDtypeStruct((tokens, Q_WIDTH), F32),
                   jax.ShapeDtypeStruct((batch, KV_WIDTH, seq), F32),
                   jax.ShapeDtypeStruct((batch, KV_WIDTH, seq), F32)],
        params=_cparams(("arbitrary", "arbitrary", "arbitrary"), 56))


def _bias_build(rel_bias_t, bucket, band):
    def body(tab_ref, bucket_ref, band_ref, bias_ref):
        for h in range(GROUP * N_KV):
            for piece in range(3):
                bk = bucket_ref[piece]
                acc = jnp.zeros((BLOCK, BLOCK), F32)
                for b in range(N_BUCKETS):
                    acc = jnp.where(bk == b, tab_ref[h, b], acc)
                bias_ref[h, piece] = jnp.where(band_ref[piece] != 0, acc, NEG_INF)

    return pl.pallas_call(
        body, name="bias_build",
        in_specs=[SMEM, VMEM, VMEM], out_specs=VMEM,
        out_shape=jax.ShapeDtypeStruct((GROUP * N_KV, 3, BLOCK, BLOCK), F32),
    )(rel_bias_t, bucket, band)


def _window_scores(q, kts, bias_ref, g, n, nblk):
    pieces = []
    for piece in range(3):
        s = _dot(q, kts[piece]) + bias_ref[g, piece]
        if piece == 0:
            s = jnp.where(n > 0, s, NEG_INF)
        if piece == 2:
            s = jnp.where(n < nblk - 1, s, NEG_INF)
        pieces.append(s)
    return pieces


def _attn_b_fwd(qb, ktb, vb3, bias, sink, *, seq, jobs=()):
    tokens = qb.shape[0]
    batch = tokens // seq
    nblk = seq // BLOCK

    def body(sink_ref, q_ref, kt_ref, v_ref, bias_ref, o_ref, lse_ref):
        kv = pl.program_id(0)

        def block(n, carry):
            nl = jnp.maximum(n - 1, 0)
            nr = jnp.minimum(n + 1, nblk - 1)
            rows = pl.ds(pl.multiple_of(n * BLOCK, BLOCK), BLOCK)
            kts = (kt_ref[nl], kt_ref[n], kt_ref[nr])
            vs = (v_ref[nl], v_ref[n], v_ref[nr])
            for g in range(GROUP):
                sl = slice(g * HEAD_DIM, (g + 1) * HEAD_DIM)
                ss = _window_scores(q_ref[rows, sl], kts, bias_ref, g, n, nblk)
                sink_l = sink_ref[0, kv * GROUP + g]
                m = jnp.maximum(jnp.maximum(jnp.max(ss[0], axis=-1, keepdims=True),
                                            jnp.max(ss[1], axis=-1, keepdims=True)),
                                jnp.maximum(jnp.max(ss[2], axis=-1, keepdims=True), sink_l))
                ps = [jnp.exp(s - m) for s in ss]
                denom = (jnp.sum(ps[0], axis=-1, keepdims=True) + jnp.sum(ps[1], axis=-1, keepdims=True)
                         + jnp.sum(ps[2], axis=-1, keepdims=True) + jnp.exp(sink_l - m))
                o2 = (_dot(ps[0].astype(MM), vs[0]) + _dot(ps[1].astype(MM), vs[1])
                      + _dot(ps[2].astype(MM), vs[2]))
                o_ref[rows, sl] = _kv_half(o2, kv) / denom
                lse_ref[0, rows, g:g + 1] = m + jnp.log(denom)
            return carry

        lax.fori_loop(0, nblk, block, 0, unroll=2)

    return _call(
        body, (sink, qb, ktb, vb3, bias), name="attn_b_fwd", jobs=jobs,
        grid=(N_KV, batch),
        in_specs=[SMEM,
                  pl.BlockSpec((seq, GROUP * HEAD_DIM), lambda k, b: (b, k)),
                  pl.BlockSpec((nblk, HEAD_DIM, BLOCK), lambda k, b: (b, k, 0)),
                  pl.BlockSpec((nblk, BLOCK, KV_WIDTH), lambda k, b: (b, 0, 0)),
                  pl.BlockSpec((GROUP, 3, BLOCK, BLOCK), lambda k, b: (k, 0, 0, 0))],
        out_specs=[pl.BlockSpec((seq, GROUP * HEAD_DIM), lambda k, b: (b, k)),
                   pl.BlockSpec((1, seq, GROUP), lambda k, b: (k, b, 0))],
        out_shape=[jax.ShapeDtypeStruct((tokens, Q_WIDTH), F32),
                   jax.ShapeDtypeStruct((N_KV, tokens, GROUP), F32)],
        params=_cparams(("arbitrary", "arbitrary"), 48))


def _attn_b_bwd(qb, ktb, kb3, vtb, do, o, lse, bias, sink, *, seq, jobs=()):
    tokens = qb.shape[0]
    batch = tokens // seq
    nblk = seq // BLOCK

    def body(sink_ref, q_ref, kt_ref, k_ref, vt_ref, do_ref, o_ref, lse_ref, bias_ref,
             dq_ref, dkt_ref, dvt_ref, dbias_ref, dsink_ref):
        kv = pl.program_id(0)

        @pl.when(pl.program_id(1) == 0)
        def _():
            dbias_ref[...] = jnp.zeros_like(dbias_ref)
            dsink_ref[...] = jnp.zeros_like(dsink_ref)

        dkt_ref[...] = jnp.zeros_like(dkt_ref)
        dvt_ref[...] = jnp.zeros_like(dvt_ref)

        def block(n, dsink):
            idx = (jnp.maximum(n - 1, 0), n, jnp.minimum(n + 1, nblk - 1))
            rows = pl.ds(pl.multiple_of(n * BLOCK, BLOCK), BLOCK)
            kts = tuple(kt_ref[i] for i in idx)
            new_dsink = []
            for g in range(GROUP):
                sl = slice(g * HEAD_DIM, (g + 1) * HEAD_DIM)
                q = q_ref[rows, sl]
                dof = do_ref[rows, sl]
                delta = jnp.sum(dof * o_ref[rows, sl], axis=-1, keepdims=True)
                dob = dof.astype(MM)
                lse_g = lse_ref[0, rows, g:g + 1]
                ss = _window_scores(q, kts, bias_ref, g, n, nblk)
                dq2 = jnp.zeros((BLOCK, KV_WIDTH), F32)
                for piece in range(3):
                    p = jnp.exp(ss[piece] - lse_g)
                    dp = _dot(dob, vt_ref[idx[piece]])
                    ds = p * (dp - delta)
                    dsb = ds.astype(MM)
                    dbias_ref[g, piece] += ds
                    dq2 = dq2 + _dot(dsb, k_ref[idx[piece]])
                    dkt_ref[idx[piece]] += _dot_tn(q, dsb)
                    dvt_ref[idx[piece]] += _dot_tn(dob, p.astype(MM))
                dq_ref[rows, sl] = _kv_half(dq2, kv)
                p_sink = jnp.exp(sink_ref[0, kv * GROUP + g] - lse_g)
                new_dsink.append(dsink[g] - p_sink * delta)
            return tuple(new_dsink)

        zero = jnp.zeros((BLOCK, 1), F32)
        dsink = lax.fori_loop(0, nblk, block, (zero,) * GROUP, unroll=2)
        for g in range(GROUP):
            dsink_ref[0, g:g + 1, :] += jnp.broadcast_to(jnp.sum(dsink[g], axis=0, keepdims=True), (1, LANES))

    qspec = pl.BlockSpec((seq, GROUP * HEAD_DIM), lambda k, b: (b, k))
    tspec = pl.BlockSpec((nblk, HEAD_DIM, BLOCK), lambda k, b: (b, k, 0))
    return _call(
        body, (sink, qb, ktb, kb3, vtb, do, o, lse, bias), name="attn_b_bwd", jobs=jobs,
        grid=(N_KV, batch),
        in_specs=[SMEM, qspec, tspec,
                  pl.BlockSpec((nblk, BLOCK, KV_WIDTH), lambda k, b: (b, 0, 0)),
                  tspec, qspec, qspec,
                  pl.BlockSpec((1, seq, GROUP), lambda k, b: (k, b, 0)),
                  pl.BlockSpec((GROUP, 3, BLOCK, BLOCK), lambda k, b: (k, 0, 0, 0))],
        out_specs=[qspec, tspec, tspec,
                   pl.BlockSpec((GROUP, 3, BLOCK, BLOCK), lambda k, b: (k, 0, 0, 0)),
                   pl.BlockSpec((1, 8, LANES), lambda k, b: (k, 0, 0))],
        out_shape=[jax.ShapeDtypeStruct((tokens, Q_WIDTH), F32),
                   jax.ShapeDtypeStruct((tokens // BLOCK, KV_WIDTH, BLOCK), F32),
                   jax.ShapeDtypeStruct((tokens // BLOCK, KV_WIDTH, BLOCK), F32),
                   jax.ShapeDtypeStruct((GROUP * N_KV, 3, BLOCK, BLOCK), F32),
                   jax.ShapeDtypeStruct((N_KV, 8, LANES), F32)],
        params=_cparams(("arbitrary", "arbitrary"), 48))


def _wo_post(oa, ob, w_o, x, g2, g3, *, tm):
    tokens = x.shape[0]

    def body(oa_ref, ob_ref, w_ref, x_ref, g2_ref, g3_ref, mix_ref, x1_ref, h2_ref):
        mix = (_dot(oa_ref[...].astype(MM), w_ref[0:Q_WIDTH, :])
               + _dot(ob_ref[...].astype(MM), w_ref[Q_WIDTH:D_MODEL, :]))
        mix_ref[...] = mix
        x1 = x_ref[...] + mix * _rms_r(mix) * g2_ref[...]
        x1_ref[...] = x1
        h2_ref[...] = (x1 * _rms_r(x1) * g3_ref[...]).astype(MM)

    tok = lambda w: pl.BlockSpec((tm, w), lambda i: (i, 0))
    row = pl.BlockSpec((1, D_MODEL), lambda i: (0, 0))
    return pl.pallas_call(
        body, name="wo_post",
        grid=(tokens // tm,),
        in_specs=[tok(Q_WIDTH), tok(Q_WIDTH), pl.BlockSpec((D_MODEL, D_MODEL), lambda i: (0, 0)),
                  tok(D_MODEL), row, row],
        out_specs=[tok(D_MODEL), tok(D_MODEL), tok(D_MODEL)],
        out_shape=[jax.ShapeDtypeStruct((tokens, D_MODEL), F32),
                   jax.ShapeDtypeStruct((tokens, D_MODEL), F32),
                   jax.ShapeDtypeStruct((tokens, D_MODEL), MM)],
        compiler_params=_cparams(("parallel",), 40),
    )(oa, ob, w_o, x, g2, g3)


def _ffn_fwd_loss(h2, w_up, w_down, x1, target, g4, *, tm):
    tokens = h2.shape[0]
    nt = tokens // tm

    def body(h2_ref, wu_ref, wd_ref, x1_ref, t_ref, g4_ref, u_ref, df_ref, dy_ref, loss_ref, dg4_ref, acc):
        c = pl.program_id(1)
        u = jnp.maximum(_dot(h2_ref[...], wu_ref[0]), 0.0)
        u_ref[...] = u
        part = _dot((u * u).astype(MM), wd_ref[...])

        @pl.when(c == 0)
        def _():
            acc[...] = part

        @pl.when(c > 0)
        def _():
            acc[...] += part

        @pl.when(c == N_CHIPS - 1)
        def _():
            f = acc[...]
            r = _rms_r(f)
            g4v = g4_ref[...]
            err = x1_ref[...] + f * r * g4v - t_ref[...]
            sq = jnp.sum(err * err, axis=-1, keepdims=True)
            loss_ref[0] = jnp.broadcast_to(jnp.sum(sq, axis=0, keepdims=True) * (0.5 / D_MODEL), (8, LANES))
            dy = err * (1.0 / D_MODEL)
            dy_ref[...] = dy
            dfv, dgv = _rms_bwd(f, r, g4v, dy)
            df_ref[...] = dfv.astype(MM)
            dg4_ref[0] = jnp.sum(dgv, axis=0, keepdims=True)

    tok = pl.BlockSpec((tm, D_MODEL), lambda i, c: (i, 0))
    return pl.pallas_call(
        body, name="ffn_fwd_loss",
        grid=(nt, N_CHIPS),
        in_specs=[tok, pl.BlockSpec((1, D_MODEL, FF_CHUNK), lambda i, c: (c, 0, 0)),
                  pl.BlockSpec((FF_CHUNK, D_MODEL), lambda i, c: (c, 0)),
                  tok, tok, pl.BlockSpec((1, D_MODEL), lambda i, c: (0, 0))],
        out_specs=[pl.BlockSpec((tm, FF_CHUNK), lambda i, c: (i, c)), tok, tok,
                   pl.BlockSpec((1, 8, LANES), lambda i, c: (i, 0, 0)),
                   pl.BlockSpec((1, 1, D_MODEL), lambda i, c: (i, 0, 0))],
        out_shape=[jax.ShapeDtypeStruct((tokens, D_FF), F32),
                   jax.ShapeDtypeStruct((tokens, D_MODEL), MM),
                   jax.ShapeDtypeStruct((tokens, D_MODEL), F32),
                   jax.ShapeDtypeStruct((nt, 8, LANES), F32),
                   jax.ShapeDtypeStruct((nt, 1, D_MODEL), F32)],
        scratch_shapes=[pltpu.VMEM((tm, D_MODEL), F32)],
        compiler_params=_cparams(("parallel", "arbitrary"), 56),
    )(h2, w_up, w_down, x1, target, g4)


def _ffn_bwd_act(df, w_down, u, w_up, x1, dy, mix, g3, g2, *, tm):
    tokens = df.shape[0]
    nt = tokens // tm

    def body(df_ref, wd_ref, u_ref, wu_ref, x1_ref, dy_ref, mix_ref, g3_ref, g2_ref,
             dz_ref, dx1_ref, dmix_ref, dg3_ref, dg2_ref, acc):
        c = pl.program_id(1)
        da = _dot_nt(df_ref[...], wd_ref[...])
        dz = (da * (2.0 * u_ref[...])).astype(MM)
        dz_ref[...] = dz
        part = _dot_nt(dz, wu_ref[0])

        @pl.when(c == 0)
        def _():
            acc[...] = part

        @pl.when(c > 0)
        def _():
            acc[...] += part

        @pl.when(c == N_CHIPS - 1)
        def _():
            dh2 = acc[...]
            x1 = x1_ref[...]
            dxn, dg3v = _rms_bwd(x1, _rms_r(x1), g3_ref[...], dh2)
            dx1 = dy_ref[...] + dxn
            dx1_ref[...] = dx1
            dg3_ref[0] = jnp.sum(dg3v, axis=0, keepdims=True)
            mix = mix_ref[...]
            dmix, dg2v = _rms_bwd(mix, _rms_r(mix), g2_ref[...], dx1)
            dmix_ref[...] = dmix.astype(MM)
            dg2_ref[0] = jnp.sum(dg2v, axis=0, keepdims=True)

    tok = pl.BlockSpec((tm, D_MODEL), lambda i, c: (i, 0))
    chunk = pl.BlockSpec((tm, FF_CHUNK), lambda i, c: (i, c))
    row = pl.BlockSpec((1, D_MODEL), lambda i, c: (0, 0))
    part = pl.BlockSpec((1, 1, D_MODEL), lambda i, c: (i, 0, 0))
    return pl.pallas_call(
        body, name="ffn_bwd_act",
        grid=(nt, N_CHIPS),
        in_specs=[tok, pl.BlockSpec((FF_CHUNK, D_MODEL), lambda i, c: (c, 0)), chunk,
                  pl.BlockSpec((1, D_MODEL, FF_CHUNK), lambda i, c: (c, 0, 0)), tok, tok, tok, row, row],
        out_specs=[chunk, tok, tok, part, part],
        out_shape=[jax.ShapeDtypeStruct((tokens, D_FF), MM),
                   jax.ShapeDtypeStruct((tokens, D_MODEL), F32),
                   jax.ShapeDtypeStruct((tokens, D_MODEL), MM),
                   jax.ShapeDtypeStruct((nt, 1, D_MODEL), F32),
                   jax.ShapeDtypeStruct((nt, 1, D_MODEL), F32)],
        scratch_shapes=[pltpu.VMEM((tm, D_MODEL), F32)],
        compiler_params=_cparams(("parallel", "arbitrary"), 56),
    )(df, w_down, u, w_up, x1, dy, mix, g3, g2)


def _tn_matmul(a, b, *, name, tm, tn, tk, chunked=False, square_a=False, vmem_mb=48, jobs=()):
    tokens, m_dim = a.shape
    n_dim = b.shape[1]
    if chunked:
        assert tm == m_dim

    def body(a_ref, b_ref, o_ref):
        av = a_ref[...]
        if square_a:
            av = av * av
        part = _dot_tn(av.astype(MM), b_ref[...].astype(MM))
        part = part[None] if chunked else part

        @pl.when(pl.program_id(2) == 0)
        def _():
            o_ref[...] = part

        @pl.when(pl.program_id(2) > 0)
        def _():
            o_ref[...] += part

    if chunked:
        out_spec = pl.BlockSpec((1, tm, tn), lambda i, j, k: (j, 0, 0))
        out_shape = jax.ShapeDtypeStruct((n_dim // tn, m_dim, tn), F32)
    else:
        out_spec = pl.BlockSpec((tm, tn), lambda i, j, k: (i, j))
        out_shape = jax.ShapeDtypeStruct((m_dim, n_dim), F32)
    (out,), job_res = _call(
        body, (a, b), name=name, jobs=jobs,
        grid=(m_dim // tm, n_dim // tn, tokens // tk),
        in_specs=[pl.BlockSpec((tk, tm), lambda i, j, k: (k, i)),
                  pl.BlockSpec((tk, tn), lambda i, j, k: (k, j))],
        out_specs=[out_spec], out_shape=[out_shape],
        params=_cparams(("arbitrary", "arbitrary", "arbitrary"), vmem_mb))
    return out, job_res


def _wo_bwd(dmix, w_o, *, tm):
    tokens = dmix.shape[0]

    def body(dm_ref, w_ref, doa_ref, dob_ref):
        dm = dm_ref[...]
        doa_ref[...] = _dot_nt(dm, w_ref[0:Q_WIDTH, :])
        dob_ref[...] = _dot_nt(dm, w_ref[Q_WIDTH:D_MODEL, :])

    tok = lambda w: pl.BlockSpec((tm, w), lambda i: (i, 0))
    return pl.pallas_call(
        body, name="wo_bwd",
        grid=(tokens // tm,),
        in_specs=[tok(D_MODEL), pl.BlockSpec((D_MODEL, D_MODEL), lambda i: (0, 0))],
        out_specs=[tok(Q_WIDTH), tok(Q_WIDTH)],
        out_shape=[jax.ShapeDtypeStruct((tokens, Q_WIDTH), F32)] * 2,
        compiler_params=_cparams(("parallel",), 40),
    )(dmix, w_o)


def _proj_bwd(dqa, dkta, dvta, dqb, dktb, dvtb, raw, x, dx1, g1, w_in, gq, gk, cq, sq, ck, sk, *, seq, tm, jobs=()):
    tokens = x.shape[0]
    nt = tokens // tm
    n_seq = seq // tm
    nblk = tm // BLOCK

    def body(dqa_ref, dkta_ref, dvta_ref, dqb_ref, dktb_ref, dvtb_ref, raw_ref, x_ref, dx1_ref, g1_ref, w_ref,
             gq_ref, gk_ref, cq_ref, sq_ref, ck_ref, sk_ref,
             gx_ref, dproj_ref, dg1_ref, dgq_ref, dgk_ref, dp):
        qa = raw_ref[:, 0:Q_WIDTH]
        dqn = _rope_t(dqa_ref[...], cq_ref[...], sq_ref[...])
        rq = _head_r(qa)
        nq = qa * rq
        dnq = dqn * gq_ref[...]
        dp[:, 0:Q_WIDTH] = rq * (dnq - nq * (_seg64_sum(dnq * nq) * (1.0 / HEAD_DIM)))
        dgq_ref[0] = jnp.sum(dqn * nq, axis=0, keepdims=True)

        ka = raw_ref[:, Q_WIDTH:QK_RAW]
        dkn = _rope_t(dkta_ref[0].T, ck_ref[...], sk_ref[...])
        rk = _head_r(ka)
        nk = ka * rk
        dnk = dkn * gk_ref[...]
        dp[:, 512:640] = rk * (dnk - nk * (_seg64_sum(dnk * nk) * (1.0 / HEAD_DIM)))
        dgk_ref[0] = jnp.sum(dkn * nk, axis=0, keepdims=True)

        dp[:, 640:768] = dvta_ref[0].T
        dp[:, 768:1280] = dqb_ref[...] * SCALE
        for j in range(nblk):
            dp[j * BLOCK:(j + 1) * BLOCK, 1280:1408] = dktb_ref[j].T
            dp[j * BLOCK:(j + 1) * BLOCK, 1408:1536] = dvtb_ref[j].T

        dproj = dp[...].astype(MM)
        dproj_ref[...] = dproj
        dh1 = _dot_nt(dproj[:, 0:IN_CHUNK], w_ref[0])
        for j in range(1, N_CHIPS):
            dh1 = dh1 + _dot_nt(dproj[:, j * IN_CHUNK:(j + 1) * IN_CHUNK], w_ref[j])
        xv = x_ref[...]
        dxn, dg1v = _rms_bwd(xv, _rms_r(xv), g1_ref[...], dh1)
        gx_ref[...] = dx1_ref[...] + dxn
        dg1_ref[0] = jnp.sum(dg1v, axis=0, keepdims=True)

    tok = lambda w: pl.BlockSpec((tm, w), lambda i: (i, 0))
    tab = lambda w: pl.BlockSpec((tm, w), lambda i: (i % n_seq, 0))
    row = lambda w: pl.BlockSpec((1, w), lambda i: (0, 0))
    tposed = pl.BlockSpec((1, KV_WIDTH, tm), lambda i: (i // n_seq, 0, i % n_seq))
    blocks = pl.BlockSpec((nblk, KV_WIDTH, BLOCK), lambda i: (i, 0, 0))
    part = lambda w: pl.BlockSpec((1, 1, w), lambda i: (i, 0, 0))
    return _call(
        body, (dqa, dkta, dvta, dqb, dktb, dvtb, raw, x, dx1, g1, w_in, gq, gk, cq, sq, ck, sk),
        name="proj_bwd", jobs=jobs,
        grid=(nt,),
        in_specs=[tok(Q_WIDTH), tposed, tposed, tok(Q_WIDTH), blocks, blocks, tok(QK_RAW), tok(D_MODEL),
                  tok(D_MODEL), row(D_MODEL),
                  pl.BlockSpec((N_CHIPS, D_MODEL, IN_CHUNK), lambda i: (0, 0, 0)),
                  row(Q_WIDTH), row(KV_WIDTH), tab(Q_WIDTH), tab(Q_WIDTH), tab(KV_WIDTH), tab(KV_WIDTH)],
        out_specs=[tok(D_MODEL), tok(IN_TOTAL), part(D_MODEL), part(Q_WIDTH), part(KV_WIDTH)],
        out_shape=[jax.ShapeDtypeStruct((tokens, D_MODEL), F32),
                   jax.ShapeDtypeStruct((tokens, IN_TOTAL), MM),
                   jax.ShapeDtypeStruct((nt, 1, D_MODEL), F32),
                   jax.ShapeDtypeStruct((nt, 1, Q_WIDTH), F32),
                   jax.ShapeDtypeStruct((nt, 1, KV_WIDTH), F32)],
        scratch_shapes=[pltpu.VMEM((tm, IN_TOTAL), F32)],
        params=_cparams(("arbitrary",), 56))


def _pack_small(dg1, dg2, dg3, dg4, dgq, dgk, dsink, dbias, bucket, loss):
    def body(dg1_ref, dg2_ref, dg3_ref, dg4_ref, dgq_ref, dgk_ref, dsink_ref, dbias_ref, bucket_ref, loss_ref,
             out_ref):
        out_ref[...] = jnp.zeros_like(out_ref)
        for r, ref in ((ROW_G1, dg1_ref), (ROW_G2, dg2_ref), (ROW_G3, dg3_ref), (ROW_G4, dg4_ref)):
            acc = ref[0]
            for t in range(1, ref.shape[0]):
                acc = acc + ref[t]
            out_ref[r:r + 1, :] = acc

        def fold(ref, heads):
            acc = ref[0]
            for t in range(1, ref.shape[0]):
                acc = acc + ref[t]
            tot = acc[:, 0:HEAD_DIM]
            for h in range(1, heads):
                tot = tot + acc[:, h * HEAD_DIM:(h + 1) * HEAD_DIM]
            return tot

        out_ref[ROW_MISC:ROW_MISC + 1, MISC_GQ:MISC_GQ + HEAD_DIM] = fold(dgq_ref, GROUP * N_KV)
        out_ref[ROW_MISC:ROW_MISC + 1, MISC_GK:MISC_GK + HEAD_DIM] = fold(dgk_ref, N_KV)
        for h in range(GROUP * N_KV):
            out_ref[ROW_MISC:ROW_MISC + 1, MISC_SINK + h:MISC_SINK + h + 1] = (
                dsink_ref[h // GROUP, h % GROUP:h % GROUP + 1, 0:1])
        lacc = loss_ref[0, 0:1, 0:1]
        for t in range(1, loss_ref.shape[0]):
            lacc = lacc + loss_ref[t, 0:1, 0:1]
        out_ref[ROW_MISC:ROW_MISC + 1, MISC_LOSS:MISC_LOSS + 1] = lacc
        heads = GROUP * N_KV
        lane = lax.broadcasted_iota(jnp.int32, (heads, N_BUCKETS), 1)
        head = lax.broadcasted_iota(jnp.int32, (heads, N_BUCKETS), 0)

        def per_bucket(b, acc):
            for h in range(heads):
                tot = jnp.zeros((1, 1), F32)
                for piece in range(3):
                    sel = jnp.where(bucket_ref[piece] == b, dbias_ref[h, piece], 0.0)
                    tot = tot + jnp.sum(jnp.sum(sel, axis=-1, keepdims=True), axis=0, keepdims=True)
                acc = jnp.where((lane == b) & (head == h), tot, acc)
            return acc

        drel = lax.fori_loop(0, N_BUCKETS, per_bucket, jnp.zeros((heads, N_BUCKETS), F32))
        for h in range(heads):
            out_ref[ROW_MISC:ROW_MISC + 1, MISC_REL + h * N_BUCKETS:MISC_REL + (h + 1) * N_BUCKETS] = drel[h:h + 1, :]

    return pl.pallas_call(
        body, name="pack_small",
        in_specs=[VMEM] * 10, out_specs=VMEM,
        out_shape=jax.ShapeDtypeStruct((8, D_MODEL), F32),
        compiler_params=pltpu.CompilerParams(vmem_limit_bytes=32 * 1024 * 1024),
    )(dg1, dg2, dg3, dg4, dgq, dgk, dsink, dbias, bucket, loss)


def _gather_weights(shards, whole):
    n = len(shards)
    full = [t for t in range(n) if whole[t]]

    def body(*refs):
        ins, outs = refs[:n], refs[n:2 * n]
        stage = refs[2 * n:3 * n]
        local_sem, ici_send, ici_recv, d2d_send, d2d_recv = refs[3 * n:]
        x, y, c = _place()
        k = 2 * x + y
        sibling = (x, y, 1 - c)
        copies = []
        for t in range(n):
            stage[t][...] = ins[t][...].astype(MM)
            mine = pltpu.make_async_copy(stage[t], outs[t].at[k], local_sem.at[t])
            mine.start()
            copies.append(mine)
        sends = []
        for t in full:
            half = ins[t].shape[0] // 2
            rows = pl.ds(c * half, half)
            for r, (fx, fy) in enumerate(_CHIP_FLIPS):
                cp = _remote(stage[t].at[rows], outs[t].at[k, rows], ici_send.at[t, r], ici_recv.at[t, r],
                             (_flip(x, fx), _flip(y, fy), c))
                cp.start()
                sends.append(cp)
        for t in full:
            half = ins[t].shape[0] // 2
            rows = pl.ds(c * half, half)
            for r, (fx, fy) in enumerate(_CHIP_FLIPS):
                kk = 2 * _flip(x, fx) + _flip(y, fy)
                landed = outs[t].at[kk, rows]
                _remote(landed, landed, ici_send.at[t, r], ici_recv.at[t, r], sibling).wait_recv()
                fwd = _remote(landed, landed, d2d_send.at[t, r], d2d_recv.at[t, r], sibling)
                fwd.start()
                sends.append(fwd)
        for t in full:
            half = ins[t].shape[0] // 2
            other = pl.ds((1 - c) * half, half)
            for r, (fx, fy) in enumerate(_CHIP_FLIPS):
                kk = 2 * _flip(x, fx) + _flip(y, fy)
                theirs = outs[t].at[kk, other]
                _remote(theirs, theirs, d2d_send.at[t, r], d2d_recv.at[t, r], sibling).wait_recv()
        for cp in sends:
            cp.wait_send()
        for cp in copies:
            cp.wait()

    return pl.pallas_call(
        body, name="gather_weights",
        in_specs=[VMEM] * n, out_specs=[HBM] * n,
        out_shape=[jax.ShapeDtypeStruct((N_CHIPS,) + s.shape, MM) for s in shards],
        scratch_shapes=[pltpu.VMEM(s.shape, MM) for s in shards] + [
            pltpu.SemaphoreType.DMA((n,)),
            pltpu.SemaphoreType.DMA((n, 3)), pltpu.SemaphoreType.DMA((n, 3)),
            pltpu.SemaphoreType.DMA((n, 3)), pltpu.SemaphoreType.DMA((n, 3))],
        compiler_params=pltpu.CompilerParams(vmem_limit_bytes=40 * 1024 * 1024),
    )(*shards)


def _add_half(grad, got, where, *, name, tr):
    nch, half, cols = got.shape
    nblk = half // tr

    def body(where_ref, g_ref, r_ref, o_ref):
        o_ref[...] = g_ref[...] + r_ref[...]

    return pl.pallas_call(
        body, name=name,
        grid_spec=pltpu.PrefetchScalarGridSpec(
            num_scalar_prefetch=1, grid=(nch, nblk),
            in_specs=[pl.BlockSpec((1, tr, cols), lambda j, i, where_ref: (j, where_ref[1] * nblk + i, 0)),
                      pl.BlockSpec((1, tr, cols), lambda j, i, where_ref: (j, i, 0))],
            out_specs=pl.BlockSpec((1, tr, cols), lambda j, i, where_ref: (j, i, 0))),
        out_shape=jax.ShapeDtypeStruct(got.shape, F32),
        compiler_params=_cparams(("parallel", "parallel"), 32),
    )(where, grad, got)


def _add_chips(own, got, where, *, name, tr):
    _, half, cols = own.shape
    nblk = half // tr

    def body(where_ref, o_ref, g_ref, out_ref):
        out_ref[...] = ((o_ref[0] + g_ref[0]) + g_ref[1]) + g_ref[2]

    return pl.pallas_call(
        body, name=name,
        grid_spec=pltpu.PrefetchScalarGridSpec(
            num_scalar_prefetch=1, grid=(nblk,),
            in_specs=[pl.BlockSpec((1, tr, cols), lambda i, where_ref: (where_ref[0], i, 0)),
                      pl.BlockSpec((3, tr, cols), lambda i, where_ref: (0, i, 0))],
            out_specs=pl.BlockSpec((tr, cols), lambda i, where_ref: (where_ref[1] * nblk + i, 0))),
        out_shape=jax.ShapeDtypeStruct((2 * half, cols), F32),
        compiler_params=_cparams(("parallel",), 32),
    )(where, own, got)


def _small_job(packed):
    def copies(ins, outs, sems):
        x, y, c = _place()
        me = 4 * x + 2 * y + c
        local, send, recv = sems
        cps = [pltpu.make_async_copy(ins[0], outs[0].at[me], local.at[0])]
        for r in range(1, N_DEV):
            fx, fy, fc = (r >> 2) & 1, (r >> 1) & 1, r & 1
            cps.append(_remote(ins[0], outs[0].at[me], send.at[r - 1], recv.at[r - 1],
                               (_flip(x, fx), _flip(y, fy), _flip(c, fc))))
        return cps

    return _Job([packed], [jax.ShapeDtypeStruct((N_DEV,) + packed.shape, F32)],
                [pltpu.SemaphoreType.DMA((1,)), pltpu.SemaphoreType.DMA((N_DEV - 1,)),
                 pltpu.SemaphoreType.DMA((N_DEV - 1,))], copies)


def _adamw_math(w, g, m, v):
    m = ADAM_B1 * m + (1.0 - ADAM_B1) * g
    v = ADAM_B2 * v + (1.0 - ADAM_B2) * (g * g)
    m_hat = m / (1.0 - ADAM_B1 ** ADAM_STEP)
    v_hat = v / (1.0 - ADAM_B2 ** ADAM_STEP)
    delta = -ADAM_LR * (m_hat / (jnp.sqrt(v_hat) + ADAM_EPS) + ADAM_WD * w)
    return delta, m, v


def _adamw(w, g, m, v, *, name, tr, jobs=()):
    rows, cols = w.shape

    def body(w_ref, g_ref, m_ref, v_ref, d_ref, nm_ref, nv_ref):
        d_ref[...], nm_ref[...], nv_ref[...] = _adamw_math(w_ref[...], g_ref[...], m_ref[...], v_ref[...])

    spec = pl.BlockSpec((tr, cols), lambda i: (i, 0))
    return _call(
        body, (w, g, m, v), name=name, jobs=jobs,
        grid=(rows // tr,),
        in_specs=[spec] * 4, out_specs=[spec] * 3,
        out_shape=[jax.ShapeDtypeStruct(w.shape, F32)] * 3,
        params=_cparams(("arbitrary",), 32))


def _small_adamw(gathered, w_rows, m_rows, v_rows):
    def body(all_ref, w_ref, m_ref, v_ref, g_ref, d_ref, nm_ref, nv_ref):
        g = all_ref[0]
        for d in range(1, N_DEV):
            g = g + all_ref[d]
        g_ref[...] = g
        d_ref[...], nm_ref[...], nv_ref[...] = _adamw_math(w_ref[...], g, m_ref[...], v_ref[...])

    return pl.pallas_call(
        body, name="small_adamw",
        in_specs=[VMEM] * 4, out_specs=[VMEM] * 4,
        out_shape=[jax.ShapeDtypeStruct(w_rows.shape, F32)] * 4,
    )(gathered, w_rows, m_rows, v_rows)


def _pack_rows(g1, g2, g3, g4, gq, gk, sink, rel):
    misc = jnp.zeros((1, D_MODEL), F32)
    misc = misc.at[:, MISC_GQ:MISC_GQ + HEAD_DIM].set(gq)
    misc = misc.at[:, MISC_GK:MISC_GK + HEAD_DIM].set(gk)
    misc = misc.at[:, MISC_SINK:MISC_SINK + 8].set(sink)
    misc = misc.at[:, MISC_REL:MISC_REL + 8 * N_BUCKETS].set(rel.T.reshape(1, 8 * N_BUCKETS))
    return jnp.concatenate([g1, g2, g3, g4, misc, jnp.zeros((3, D_MODEL), F32)], axis=0)


def _unpack_rows(t):
    misc = t[ROW_MISC:ROW_MISC + 1]
    rel = misc[:, MISC_REL:MISC_REL + 8 * N_BUCKETS].reshape(8, N_BUCKETS).T
    return dict(g1=t[ROW_G1:ROW_G1 + 1], g2=t[ROW_G2:ROW_G2 + 1], g3=t[ROW_G3:ROW_G3 + 1], g4=t[ROW_G4:ROW_G4 + 1],
                gq=misc[:, MISC_GQ:MISC_GQ + HEAD_DIM], gk=misc[:, MISC_GK:MISC_GK + HEAD_DIM],
                sink=misc[:, MISC_SINK:MISC_SINK + 8], rel=rel)


def kernel(x, w_in, w_o, g_pre_mix, g_post_mix, q_norm_a, k_norm_a, sink_b, rel_bias, g_pre_ffn, w_ffn_up, w_ffn_down, g_post_ffn, loss_target, m_w_in, m_w_o, m_g_pre_mix, m_g_post_mix, m_q_norm_a, m_k_norm_a, m_sink_b, m_rel_bias, m_g_pre_ffn, m_w_ffn_up, m_w_ffn_down, m_g_post_ffn, v_w_in, v_w_o, v_g_pre_mix, v_g_post_mix, v_q_norm_a, v_k_norm_a, v_sink_b, v_rel_bias, v_g_pre_ffn, v_w_ffn_up, v_w_ffn_down, v_g_post_ffn):
    batch, seq, _ = x.shape
    tokens = batch * seq
    where = jnp.stack([2 * lax.axis_index("x") + lax.axis_index("y"), lax.axis_index("c")]).astype(jnp.int32)
    x2 = x.reshape(tokens, D_MODEL)
    g1, g2, g3, g4 = g_pre_mix, g_post_mix, g_pre_ffn, g_post_ffn

    cos, sin = _rope_tables(seq)
    cq, sq = jnp.tile(cos, (1, 8)) * SCALE, jnp.tile(sin, (1, 8)) * SCALE
    ck, sk = jnp.tile(cos, (1, 2)), jnp.tile(sin, (1, 2))
    gq8, gk2 = jnp.tile(q_norm_a, (1, 8)), jnp.tile(k_norm_a, (1, 2))
    bucket, band = _window_tables()
    bias = _bias_build(rel_bias.T, bucket, band)

    w_in_g, w_o_p, w_up_p, w_down_p = _gather_weights(
        (w_in[0], w_o[0], w_ffn_up[0], w_ffn_down[0]), whole=(True, False, False, False))
    (h1, raw, qa, ka, kta, va, vta, qb, kb, ktb, vb, vtb) = _pre_proj(
        x2, g1, w_in_g, gq8, gk2, cq, sq, ck, sk, seq=seq, tm=min(512, seq))
    (oa, lse_a), (w_part,) = _attn_a_fwd(
        qa, kta, va, seq=seq, bq=min(256, seq), jobs=[_gather_job([w_o_p, w_up_p, w_down_p], forward=False)])
    kb3 = kb.reshape(tokens // BLOCK, BLOCK, KV_WIDTH)
    vb3 = vb.reshape(tokens // BLOCK, BLOCK, KV_WIDTH)
    (ob, lse_b), ((w_o_g, w_up_g, w_down_g),) = _attn_b_fwd(
        qb, ktb, vb3, bias, sink_b, seq=seq, jobs=[_gather_job(w_part, forward=True)])
    w_o2 = w_o_g.reshape(D_MODEL, D_MODEL)
    w_down2 = w_down_g.reshape(D_FF, D_MODEL)
    mix, x1, h2 = _wo_post(oa, ob, w_o2, x2, g2, g3, tm=512)
    u, df, dy, loss_t, dg4 = _ffn_fwd_loss(h2, w_up_g, w_down2, x1, loss_target.reshape(tokens, D_MODEL), g4, tm=512)

    dz, dx1, dmix, dg3, dg2 = _ffn_bwd_act(df, w_down2, u, w_up_g, x1, dy, mix, g3, g2, tm=512)
    gw_down, _ = _tn_matmul(u, df, name="grad_w_down", tm=1024, tn=1024, tk=1024, square_a=True)
    gw_down = gw_down.reshape(N_CHIPS, FF_CHUNK, D_MODEL)
    gw_up, ((got_down,),) = _tn_matmul(h2, dz, name="grad_w_up", tm=1024, tn=1024, tk=min(2048, tokens), chunked=True,
                                        jobs=[_swap_job([gw_down])])
    doa, dob = _wo_bwd(dmix, w_o2, tm=512)
    gw_o = jnp.concatenate([
        _tn_matmul(oa, dmix, name="grad_w_o_a", tm=512, tn=1024, tk=min(2048, tokens))[0],
        _tn_matmul(ob, dmix, name="grad_w_o_b", tm=512, tn=1024, tk=min(2048, tokens))[0]], axis=0)
    gw_o = gw_o.reshape(N_CHIPS, O_CHUNK, D_MODEL)
    sum_down = _add_half(gw_down, got_down, where, name="add_half_w_down", tr=128)
    (dqa, dkta, dvta), ((ex_down,), (got_up,)) = _attn_a_bwd(
        qa, kta, ka, vta, doa, oa, lse_a, seq=seq, bq=min(256, seq),
        jobs=[_exchange_job([sum_down]), _swap_job([gw_up])])
    full_down = _add_chips(sum_down, ex_down, where, name="add_chips_w_down", tr=128)
    sum_up = _add_half(gw_up, got_up, where, name="add_half_w_up", tr=128)
    (dqb, dktb, dvtb, dbias, dsink), ((ex_up,), (g_down,), (got_o,)) = _attn_b_bwd(
        qb, ktb, kb3, vtb, dob, ob, lse_b, bias, sink_b, seq=seq,
        jobs=[_exchange_job([sum_up]), _join_job([full_down]), _swap_job([gw_o])])
    full_up = _add_chips(sum_up, ex_up, where, name="add_chips_w_up", tr=128)
    sum_o = _add_half(gw_o, got_o, where, name="add_half_w_o", tr=128)
    (grad_x, dproj, dg1, dgq, dgk), ((ex_o,), (g_up,)) = _proj_bwd(
        dqa, dkta, dvta, dqb, dktb, dvtb, raw, x2, dx1, g1, w_in_g, gq8, gk2, cq, sq, ck, sk,
        seq=seq, tm=min(512, seq), jobs=[_exchange_job([sum_o]), _join_job([full_up])])
    full_o = _add_chips(sum_o, ex_o, where, name="add_chips_w_o", tr=128)
    packed = _pack_small(dg1, dg2, dg3, dg4, dgq, dgk, dsink, dbias, bucket, loss_t)
    gw_in, ((g_o,), (gathered,)) = _tn_matmul(h1, dproj, name="grad_w_in", tm=1024, tn=IN_CHUNK, tk=min(2048, tokens),
                                              chunked=True, jobs=[_join_job([full_o]), _small_job(packed)])

    upd_down, ((got_in,),) = _adamw(w_ffn_down[0], g_down, m_w_ffn_down[0], v_w_ffn_down[0], name="adamw_w_down",
                                    tr=128, jobs=[_swap_job([gw_in])])
    sum_in = _add_half(gw_in, got_in, where, name="add_half_w_in", tr=128)
    upd_up, ((ex_in,),) = _adamw(w_ffn_up[0], g_up, m_w_ffn_up[0], v_w_ffn_up[0], name="adamw_w_up", tr=128,
                                 jobs=[_exchange_job([sum_in])])
    full_in = _add_chips(sum_in, ex_in, where, name="add_chips_w_in", tr=128)
    upd_o, ((g_in,),) = _adamw(w_o[0], g_o, m_w_o[0], v_w_o[0], name="adamw_w_o", tr=128,
                               jobs=[_join_job([full_in])])
    upd_in, _ = _adamw(w_in[0], g_in, m_w_in[0], v_w_in[0], name="adamw_w_in", tr=128)
    big = [[t[None] for t in (g, *upd)] for g, upd in
           ((g_in, upd_in), (g_o, upd_o), (g_up, upd_up), (g_down, upd_down))]

    w_rows = _pack_rows(g1, g2, g3, g4, q_norm_a, k_norm_a, sink_b, rel_bias)
    m_rows = _pack_rows(m_g_pre_mix, m_g_post_mix, m_g_pre_ffn, m_g_post_ffn, m_q_norm_a, m_k_norm_a, m_sink_b,
                        m_rel_bias)
    v_rows = _pack_rows(v_g_pre_mix, v_g_post_mix, v_g_pre_ffn, v_g_post_ffn, v_q_norm_a, v_k_norm_a, v_sink_b,
                        v_rel_bias)
    g_rows, d_rows, nm_rows, nv_rows = _small_adamw(gathered, w_rows, m_rows, v_rows)
    loss = g_rows[ROW_MISC, MISC_LOSS]
    small = [_unpack_rows(t) for t in (g_rows, d_rows, nm_rows, nv_rows)]

    def leaves(i):
        s = small[i]
        return (big[0][i], big[1][i], s["g1"], s["g2"], s["gq"], s["gk"], s["sink"], s["rel"], s["g3"],
                big[2][i], big[3][i], s["g4"])

    return (loss, grad_x.reshape(batch, seq, D_MODEL), *leaves(0), *leaves(1), *leaves(2), *leaves(3))
```

```python
import functools

import jax
import jax.numpy as jnp
import numpy as np
from jax import lax
from jax.experimental import pallas as pl
from jax.experimental.pallas import tpu as pltpu

F32 = jnp.float32
MM = jnp.bfloat16

D_MODEL = 1024
HEAD_DIM = 64
N_KV = 2
GROUP = 4
Q_WIDTH = 512
KV_WIDTH = 128
D_FF = 4096
GRID_W = 64
BLOCK = 128
N_BUCKETS = 32
MAX_DISTANCE = 128
ROPE_THETA = 10000.0
EPS = 1e-6
NEG_INF = -1e30
SCALE = HEAD_DIM ** -0.5
IN_TOTAL = 1536
N_CHIPS = 4
N_DEV = 8
IN_CHUNK = IN_TOTAL // N_CHIPS
FF_CHUNK = D_FF // N_CHIPS
O_CHUNK = D_MODEL // N_CHIPS
QK_RAW = 640

ADAM_LR = 0.001
ADAM_B1 = 0.9
ADAM_B2 = 0.999
ADAM_EPS = 1e-08
ADAM_WD = 0.01
ADAM_STEP = 10

LANES = 128
MESH = pl.DeviceIdType.MESH
HBM = pl.BlockSpec(memory_space=pl.ANY)
VMEM = pl.BlockSpec(memory_space=pltpu.VMEM)
SMEM = pl.BlockSpec(memory_space=pltpu.SMEM)

ROW_G1, ROW_G2, ROW_G3, ROW_G4, ROW_MISC = 0, 1, 2, 3, 4
MISC_GQ, MISC_GK, MISC_SINK, MISC_REL, MISC_LOSS = 0, 64, 128, 256, 512


def _cparams(sem, vmem_mb):
    return pltpu.CompilerParams(dimension_semantics=sem, vmem_limit_bytes=vmem_mb * 1024 * 1024)


class _Job:
    def __init__(self, operands, out_shapes, sems, copies, alias=None):
        self.operands, self.out_shapes, self.sems, self.copies = list(operands), list(out_shapes), list(sems), copies
        self.alias = dict(alias or {})


def _place():
    return lax.axis_index("x"), lax.axis_index("y"), lax.axis_index("c")


_CHIP_FLIPS = ((1, 0), (0, 1), (1, 1))


def _flip(v, bit):
    return 1 - v if bit else v


def _remote(src, dst, send, recv, dev):
    return pltpu.make_async_remote_copy(src_ref=src, dst_ref=dst, send_sem=send, recv_sem=recv,
                                        device_id=dev, device_id_type=MESH)


def _swap_job(grads):
    n = len(grads)

    def copies(ins, outs, sems):
        x, y, c = _place()
        send, recv = sems
        cps = []
        for t in range(n):
            half = ins[t].shape[1] // 2
            cps.append(_remote(ins[t].at[:, pl.ds((1 - c) * half, half), :], outs[t], send.at[t], recv.at[t],
                               (x, y, 1 - c)))
        return cps

    shapes = [jax.ShapeDtypeStruct((g.shape[0], g.shape[1] // 2, g.shape[2]), F32) for g in grads]
    return _Job(grads, shapes, [pltpu.SemaphoreType.DMA((n,)), pltpu.SemaphoreType.DMA((n,))], copies)


def _exchange_job(sums):
    n = len(sums)

    def copies(ins, outs, sems):
        x, y, c = _place()
        send, recv = sems
        cps = []
        for t in range(n):
            for r, (fx, fy) in enumerate(_CHIP_FLIPS):
                kk = 2 * _flip(x, fx) + _flip(y, fy)
                cps.append(_remote(ins[t].at[kk], outs[t].at[r], send.at[t, r], recv.at[t, r],
                                   (_flip(x, fx), _flip(y, fy), c)))
        return cps

    shapes = [jax.ShapeDtypeStruct((3,) + s.shape[1:], F32) for s in sums]
    return _Job(sums, shapes, [pltpu.SemaphoreType.DMA((n, 3)), pltpu.SemaphoreType.DMA((n, 3))], copies)


def _join_job(fulls):
    n = len(fulls)

    def copies(ins, outs, sems):
        x, y, c = _place()
        send, recv = sems
        cps = []
        for t in range(n):
            half = ins[t].shape[0] // 2
            rows = pl.ds(c * half, half)
            cps.append(_remote(ins[t].at[rows], outs[t].at[rows], send.at[t], recv.at[t], (x, y, 1 - c)))
        return cps

    shapes = [jax.ShapeDtypeStruct(f.shape, f.dtype) for f in fulls]
    return _Job(fulls, shapes, [pltpu.SemaphoreType.DMA((n,)), pltpu.SemaphoreType.DMA((n,))], copies,
                alias={t: t for t in range(n)})


def _gather_job(bufs, forward):
    n = len(bufs)

    def copies(ins, outs, sems):
        x, y, c = _place()
        send, recv = sems
        cps = []
        for t in range(n):
            half = ins[t].shape[1] // 2
            rows = pl.ds(c * half, half)
            for r, (fx, fy) in enumerate(_CHIP_FLIPS):
                if forward:
                    kk = 2 * _flip(x, fx) + _flip(y, fy)
                    dev = (x, y, 1 - c)
                else:
                    kk = 2 * x + y
                    dev = (_flip(x, fx), _flip(y, fy), c)
                cps.append(_remote(ins[t].at[kk, rows], outs[t].at[kk, rows], send.at[t, r], recv.at[t, r], dev))
        return cps

    shapes = [jax.ShapeDtypeStruct(b.shape, b.dtype) for b in bufs]
    return _Job(bufs, shapes, [pltpu.SemaphoreType.DMA((n, 3)), pltpu.SemaphoreType.DMA((n, 3))], copies,
                alias={t: t for t in range(n)})


def _call(body, args, *, name, grid, in_specs, out_specs, out_shape, scratch_shapes=(), params=None, jobs=()):
    n_in, n_out, n_scr = len(in_specs), len(out_specs), len(scratch_shapes)
    job_in = [len(j.operands) for j in jobs]
    job_out = [len(j.out_shapes) for j in jobs]
    job_sem = [len(j.sems) for j in jobs]

    def wrapped(*refs):
        pos = 0
        ins = refs[pos:pos + n_in]; pos += n_in
        jins = []
        for k in job_in:
            jins.append(refs[pos:pos + k]); pos += k
        outs = refs[pos:pos + n_out]; pos += n_out
        jouts = []
        for k in job_out:
            jouts.append(refs[pos:pos + k]); pos += k
        scr = refs[pos:pos + n_scr]; pos += n_scr
        jsems = []
        for k in job_sem:
            jsems.append(refs[pos:pos + k]); pos += k
        if jobs:
            ids = [pl.program_id(d) for d in range(len(grid))]
            first = functools.reduce(jnp.logical_and, [i == 0 for i in ids])
            last = functools.reduce(jnp.logical_and, [i == g - 1 for i, g in zip(ids, grid)])

            @pl.when(first)
            def _():
                for j, ji, jo, js in zip(jobs, jins, jouts, jsems):
                    for cp in j.copies(ji, jo, js):
                        cp.start()

        body(*ins, *outs, *scr)
        if jobs:
            @pl.when(last)
            def _():
                for j, ji, jo, js in zip(jobs, jins, jouts, jsems):
                    for cp in j.copies(ji, jo, js):
                        cp.wait()

    aliases = {}
    in_pos, out_pos = n_in, n_out
    for j in jobs:
        for i, o in j.alias.items():
            aliases[in_pos + i] = out_pos + o
        in_pos += len(j.operands)
        out_pos += len(j.out_shapes)
    res = pl.pallas_call(
        wrapped, name=name, grid=grid,
        in_specs=list(in_specs) + [HBM] * sum(job_in),
        out_specs=list(out_specs) + [HBM] * sum(job_out),
        out_shape=list(out_shape) + [s for j in jobs for s in j.out_shapes],
        scratch_shapes=list(scratch_shapes) + [s for j in jobs for s in j.sems],
        input_output_aliases=aliases,
        compiler_params=params,
    )(*args, *[a for j in jobs for a in j.operands])
    own, rest = list(res[:n_out]), list(res[n_out:])
    job_res = []
    for k in job_out:
        job_res.append(rest[:k])
        rest = rest[k:]
    return own, job_res


def _dot(a, b):
    return jnp.dot(a, b, preferred_element_type=F32)


def _dot_nt(a, b):
    return lax.dot_general(a, b, (((1,), (1,)), ((), ())), preferred_element_type=F32)


def _dot_tn(a, b):
    return lax.dot_general(a, b, (((0,), (0,)), ((), ())), preferred_element_type=F32)


def _rms_r(x):
    return lax.rsqrt(jnp.mean(x * x, axis=-1, keepdims=True) + EPS)


def _rms_bwd(x, r, g, dy):
    n = x * r
    dn = dy * g
    dx = r * (dn - n * jnp.mean(dn * n, axis=-1, keepdims=True))
    return dx, dy * n


def _seg64_sum(v):
    rows, width = v.shape
    lane = lax.broadcasted_iota(jnp.int32, (rows, LANES), 1)
    lo = lane < HEAD_DIM
    outs = []
    for c in range(width // LANES):
        ch = v[:, c * LANES:(c + 1) * LANES]
        s_lo = jnp.sum(jnp.where(lo, ch, 0.0), axis=-1, keepdims=True)
        s_hi = jnp.sum(jnp.where(lo, 0.0, ch), axis=-1, keepdims=True)
        outs.append(jnp.where(lo, s_lo, s_hi))
    return outs[0] if len(outs) == 1 else jnp.concatenate(outs, axis=-1)


def _head_r(v):
    return lax.rsqrt(_seg64_sum(v * v) * (1.0 / HEAD_DIM) + EPS)


def _swap16(ch):
    lane = lax.broadcasted_iota(jnp.int32, ch.shape, 1)
    return jnp.where((lane % 32) < 16, pltpu.roll(ch, LANES - 16, 1), pltpu.roll(ch, 16, 1))


def _by_chunk(fn, v):
    outs = [fn(v[:, c * LANES:(c + 1) * LANES]) for c in range(v.shape[1] // LANES)]
    return outs[0] if len(outs) == 1 else jnp.concatenate(outs, axis=-1)


def _rope(v, cos, sin_signed):
    return _by_chunk(lambda ch: ch * cos + _swap16(ch) * sin_signed, v)


def _rope_t(g, cos, sin_signed):
    return _by_chunk(lambda ch: ch * cos + _swap16(ch * sin_signed), g)


def _rope_tables(seq):
    nf = HEAD_DIM // 4
    freqs = ROPE_THETA ** (-jnp.arange(nf, dtype=F32) / nf)
    pos = jnp.arange(seq, dtype=jnp.int32)
    row = (pos // GRID_W).astype(F32)
    col = (pos % GRID_W).astype(F32)
    ang_r = row[:, None] * freqs[None, :]
    ang_c = col[:, None] * freqs[None, :]
    cr, sr, cc, sc = jnp.cos(ang_r), jnp.sin(ang_r), jnp.cos(ang_c), jnp.sin(ang_c)
    cos = jnp.concatenate([cr, cr, cc, cc], axis=1)
    sin = jnp.concatenate([-sr, sr, -sc, sc], axis=1)
    return cos, sin


def _t5_bucket(rel):
    nb = N_BUCKETS // 2
    ret = (rel > 0).astype(jnp.int32) * nb
    n = jnp.abs(rel)
    max_exact = nb // 2
    nf = jnp.maximum(n, 1).astype(jnp.float32)
    large = max_exact + (jnp.log(nf / max_exact) / np.float32(np.log(MAX_DISTANCE / max_exact))
                         * (nb - max_exact)).astype(jnp.int32)
    large = jnp.minimum(large, nb - 1)
    return ret + jnp.where(n < max_exact, n, large)


def _window_tables():
    a = jnp.arange(BLOCK, dtype=jnp.int32)
    c = jnp.arange(3 * BLOCK, dtype=jnp.int32)
    rel = c[None, :] - BLOCK - a[:, None]
    bucket = _t5_bucket(rel)
    band = (jnp.abs(rel) <= BLOCK).astype(jnp.int32)
    to3 = lambda t: t.reshape(BLOCK, 3, BLOCK).transpose(1, 2, 0)
    return to3(bucket), to3(band)


def _pre_proj(x, g1, w_in, gq, gk, cq, sq, ck, sk, *, seq, tm):
    tokens = x.shape[0]
    n_seq = seq // tm
    nblk = tm // BLOCK
    batch = tokens // seq

    def body(x_ref, g1_ref, w_ref, gq_ref, gk_ref, cq_ref, sq_ref, ck_ref, sk_ref,
             h1_ref, raw_ref, qa_ref, ka_ref, kta_ref, va_ref, vta_ref,
             qtb_ref, kb_ref, ktb_ref, vb_ref, vtb_ref, proj):
        xv = x_ref[...]
        h = (xv * _rms_r(xv) * g1_ref[...]).astype(MM)
        h1_ref[...] = h
        for j in range(N_CHIPS):
            proj[:, j * IN_CHUNK:(j + 1) * IN_CHUNK] = _dot(h, w_ref[j])
        qa = proj[:, 0:Q_WIDTH]
        ka = proj[:, Q_WIDTH:QK_RAW]
        raw_ref[...] = proj[:, 0:QK_RAW]
        qn = qa * _head_r(qa) * gq_ref[...]
        qa_ref[...] = _rope(qn, cq_ref[...], sq_ref[...]).astype(MM)
        kn = ka * _head_r(ka) * gk_ref[...]
        kr = _rope(kn, ck_ref[...], sk_ref[...])
        ka_ref[...] = kr.astype(MM)
        kta_ref[0] = kr.T.astype(MM)
        va = proj[:, 640:768]
        va_ref[...] = va.astype(MM)
        vta_ref[0] = va.T.astype(MM)
        qb = proj[:, 768:1280] * SCALE
        kb = proj[:, 1280:1408]
        vb = proj[:, 1408:1536]
        kb_ref[...] = kb.astype(MM)
        vb_ref[...] = vb.astype(MM)
        for j in range(nblk):
            rows = slice(j * BLOCK, (j + 1) * BLOCK)
            qtb_ref[j] = qb[rows, :].T.astype(MM)
            ktb_ref[j] = kb[rows, :].T.astype(MM)
            vtb_ref[j] = vb[rows, :].T.astype(MM)

    tok = lambda w: pl.BlockSpec((tm, w), lambda i: (i, 0))
    tab = lambda w: pl.BlockSpec((tm, w), lambda i: (i % n_seq, 0))
    row = lambda w: pl.BlockSpec((1, w), lambda i: (0, 0))
    tposed = pl.BlockSpec((1, LANES, tm), lambda i: (i // n_seq, 0, i % n_seq))
    blocks = pl.BlockSpec((nblk, BLOCK, LANES), lambda i: (i, 0, 0))
    qblocks = pl.BlockSpec((nblk, Q_WIDTH, BLOCK), lambda i: (i, 0, 0))
    tok_mm = lambda w: jax.ShapeDtypeStruct((tokens, w), MM)
    return pl.pallas_call(
        body, name="pre_proj",
        grid=(tokens // tm,),
        in_specs=[tok(D_MODEL), row(D_MODEL),
                  pl.BlockSpec((N_CHIPS, D_MODEL, IN_CHUNK), lambda i: (0, 0, 0)),
                  row(Q_WIDTH), row(KV_WIDTH), tab(KV_WIDTH), tab(KV_WIDTH), tab(KV_WIDTH), tab(KV_WIDTH)],
        out_specs=[tok(D_MODEL), tok(QK_RAW), tok(Q_WIDTH), tok(KV_WIDTH), tposed, tok(KV_WIDTH), tposed,
                   qblocks, tok(KV_WIDTH), blocks, tok(KV_WIDTH), blocks],
        out_shape=[
            tok_mm(D_MODEL),
            jax.ShapeDtypeStruct((tokens, QK_RAW), F32),
            tok_mm(Q_WIDTH),
            tok_mm(KV_WIDTH),
            jax.ShapeDtypeStruct((batch, KV_WIDTH, seq), MM),
            tok_mm(KV_WIDTH),
            jax.ShapeDtypeStruct((batch, KV_WIDTH, seq), MM),
            jax.ShapeDtypeStruct((tokens // BLOCK, Q_WIDTH, BLOCK), MM),
            tok_mm(KV_WIDTH),
            jax.ShapeDtypeStruct((tokens // BLOCK, KV_WIDTH, BLOCK), MM),
            tok_mm(KV_WIDTH),
            jax.ShapeDtypeStruct((tokens // BLOCK, KV_WIDTH, BLOCK), MM),
        ],
        scratch_shapes=[pltpu.VMEM((tm, IN_TOTAL), F32)],
        compiler_params=_cparams(("parallel",), 48),
    )(x, g1, w_in, gq, gk, cq, sq, ck, sk)


def _kv_half(v2, kv):
    return jnp.where(kv == 0, v2[:, :HEAD_DIM], v2[:, HEAD_DIM:])


def _attn_a_fwd(qa, kta, va, *, seq, bq, jobs=()):
    tokens = qa.shape[0]
    batch = tokens // seq
    nq = seq // bq

    def body(q_ref, kt_ref, v_ref, o_ref, lse_ref):
        kv = pl.program_id(1)
        kt = kt_ref[0]
        v = v_ref[...]
        for g in range(GROUP):
            sl = slice(g * HEAD_DIM, (g + 1) * HEAD_DIM)
            s = _dot(q_ref[:, sl], kt)
            m = jnp.max(s, axis=-1, keepdims=True)
            p = jnp.exp(s - m)
            l = jnp.sum(p, axis=-1, keepdims=True)
            o2 = _dot(p.astype(MM), v)
            o_ref[:, sl] = _kv_half(o2, kv) / l
            lse_ref[0, :, g:g + 1] = m + jnp.log(l)

    return _call(
        body, (qa, kta, va), name="attn_a_fwd", jobs=jobs,
        grid=(batch, N_KV, nq),
        in_specs=[pl.BlockSpec((bq, GROUP * HEAD_DIM), lambda b, k, i: (b * nq + i, k)),
                  pl.BlockSpec((1, HEAD_DIM, seq), lambda b, k, i: (b, k, 0)),
                  pl.BlockSpec((seq, KV_WIDTH), lambda b, k, i: (b, 0))],
        out_specs=[pl.BlockSpec((bq, GROUP * HEAD_DIM), lambda b, k, i: (b * nq + i, k)),
                   pl.BlockSpec((1, bq, GROUP), lambda b, k, i: (k, b * nq + i, 0))],
        out_shape=[jax.ShapeDtypeStruct((tokens, Q_WIDTH), F32),
                   jax.ShapeDtypeStruct((N_KV, tokens, GROUP), F32)],
        params=_cparams(("arbitrary", "arbitrary", "arbitrary"), 48))


def _attn_a_bwd(qa, kta, ka, vta, do, o, lse, *, seq, bq, jobs=()):
    tokens = qa.shape[0]
    batch = tokens // seq
    nq = seq // bq

    def body(q_ref, kt_ref, k_ref, vt_ref, do_ref, o_ref, lse_ref, dq_ref, dkt_ref, dvt_ref):
        kv = pl.program_id(1)

        @pl.when(pl.program_id(2) == 0)
        def _():
            dkt_ref[...] = jnp.zeros_like(dkt_ref)
            dvt_ref[...] = jnp.zeros_like(dvt_ref)

        kt = kt_ref[0]
        vt = vt_ref[0]
        k2 = k_ref[...]
        for g in range(GROUP):
            sl = slice(g * HEAD_DIM, (g + 1) * HEAD_DIM)
            q = q_ref[:, sl]
            dof = do_ref[:, sl]
            delta = jnp.sum(dof * o_ref[:, sl], axis=-1, keepdims=True)
            dob = dof.astype(MM)
            p = jnp.exp(_dot(q, kt) - lse_ref[0, :, g:g + 1])
            dp = _dot(dob, vt)
            ds = (p * (dp - delta)).astype(MM)
            dq_ref[:, sl] = _kv_half(_dot(ds, k2), kv)
            dkt_ref[0] += _dot_tn(q, ds)
            dvt_ref[0] += _dot_tn(dob, p.astype(MM))

    qspec = pl.BlockSpec((bq, GROUP * HEAD_DIM), lambda b, k, i: (b * nq + i, k))
    tspec = pl.BlockSpec((1, HEAD_DIM, seq), lambda b, k, i: (b, k, 0))
    return _call(
        body, (qa, kta, ka, vta, do, o, lse), name="attn_a_bwd", jobs=jobs,
        grid=(batch, N_KV, nq),
        in_specs=[qspec, tspec, pl.BlockSpec((seq, KV_WIDTH), lambda b, k, i: (b, 0)), tspec, qspec, qspec,
                  pl.BlockSpec((1, bq, GROUP), lambda b, k, i: (k, b * nq + i, 0))],
        out_specs=[qspec, tspec, tspec],
        out_shape=[jax.ShapeDtypeStruct((tokens, Q_WIDTH), F32),
                   jax.ShapeDtypeStruct((batch, KV_WIDTH, seq), F32),
                   jax.ShapeDtypeStruct((batch, KV_WIDTH, seq), F32)],
        params=_cparams(("arbitrary", "arbitrary", "arbitrary"), 56))


def _bias_build(rel_bias_t, bucket_t, band_t):
    def body(tab_ref, bucket_ref, band_ref, bias_ref):
        for h in range(GROUP * N_KV):
            for piece in range(3):
                bk = bucket_ref[piece]
                acc = jnp.zeros((BLOCK, BLOCK), F32)
                for b in range(N_BUCKETS):
                    acc = jnp.where(bk == b, tab_ref[h, b], acc)
                g = h % GROUP
                bias_ref[h // GROUP, piece, :, g * BLOCK:(g + 1) * BLOCK] = jnp.where(band_ref[piece] != 0, acc, NEG_INF)

    return pl.pallas_call(
        body, name="bias_build",
        in_specs=[SMEM, VMEM, VMEM], out_specs=VMEM,
        out_shape=jax.ShapeDtypeStruct((N_KV, 3, BLOCK, GROUP * BLOCK), F32),
    )(rel_bias_t, bucket_t, band_t)


def _pad_heads(t, kv):
    outs = []
    for g in range(GROUP):
        tg = t[g * HEAD_DIM:(g + 1) * HEAD_DIM, :]
        zero = jnp.zeros_like(tg)
        outs.append(jnp.concatenate([jnp.where(kv == 0, tg, zero), jnp.where(kv == 0, zero, tg)], axis=0))
    return jnp.concatenate(outs, axis=-1)


def _unpad_heads(t, kv):
    outs = [_kv_half(t[:, g * BLOCK:(g + 1) * BLOCK].T, kv) for g in range(GROUP)]
    return jnp.concatenate(outs, axis=-1)


def _sink_row(sink_ref, kv):
    lane_head = lax.broadcasted_iota(jnp.int32, (1, GROUP * BLOCK), 1) // BLOCK
    row = jnp.zeros((1, GROUP * BLOCK), F32)
    for g in range(GROUP):
        row = jnp.where(lane_head == g, sink_ref[0, kv * GROUP + g], row)
    return row


def _window_scores_t(k_ref, idx, qpad, bias_ref, n, nblk):
    pieces = []
    for piece in range(3):
        s = _dot(k_ref[idx[piece]], qpad) + bias_ref[0, piece]
        if piece == 0:
            s = jnp.where(n > 0, s, NEG_INF)
        if piece == 2:
            s = jnp.where(n < nblk - 1, s, NEG_INF)
        pieces.append(s)
    return pieces


def _attn_b_fwd(qtb, kb3, vtb, bias, sink, *, seq, jobs=()):
    nblk_all = qtb.shape[0]
    tokens = nblk_all * BLOCK
    batch = tokens // seq
    nblk = seq // BLOCK

    def body(sink_ref, q_ref, k_ref, vt_ref, bias_ref, o_ref, lse_ref):
        kv = pl.program_id(0)
        sink_row = _sink_row(sink_ref, kv)

        def block(n, carry):
            idx = (jnp.maximum(n - 1, 0), n, jnp.minimum(n + 1, nblk - 1))
            rows = pl.ds(pl.multiple_of(n * BLOCK, BLOCK), BLOCK)
            qpad = _pad_heads(q_ref[n], kv)
            ss = _window_scores_t(k_ref, idx, qpad, bias_ref, n, nblk)
            m = jnp.maximum(jnp.maximum(jnp.max(ss[0], axis=0, keepdims=True),
                                        jnp.max(ss[1], axis=0, keepdims=True)),
                            jnp.maximum(jnp.max(ss[2], axis=0, keepdims=True), sink_row))
            ps = [jnp.exp(s - m) for s in ss]
            denom = (jnp.sum(ps[0], axis=0, keepdims=True) + jnp.sum(ps[1], axis=0, keepdims=True)
                     + jnp.sum(ps[2], axis=0, keepdims=True) + jnp.exp(sink_row - m))
            ot = (_dot(vt_ref[idx[0]], ps[0].astype(MM)) + _dot(vt_ref[idx[1]], ps[1].astype(MM))
                  + _dot(vt_ref[idx[2]], ps[2].astype(MM)))
            o_ref[rows, :] = _unpad_heads(ot * (1.0 / denom), kv)
            lse_ref[0, n] = jnp.broadcast_to(m + jnp.log(denom), (8, GROUP * BLOCK))
            return carry

        lax.fori_loop(0, nblk, block, 0)

    both = pl.BlockSpec((nblk, BLOCK, KV_WIDTH), lambda k, b: (b, 0, 0))
    return _call(
        body, (sink, qtb, kb3, vtb, bias), name="attn_b_fwd", jobs=jobs,
        grid=(N_KV, batch),
        in_specs=[SMEM, pl.BlockSpec((nblk, GROUP * HEAD_DIM, BLOCK), lambda k, b: (b, k, 0)), both, both,
                  pl.BlockSpec((1, 3, BLOCK, GROUP * BLOCK), lambda k, b: (k, 0, 0, 0))],
        out_specs=[pl.BlockSpec((seq, GROUP * HEAD_DIM), lambda k, b: (b, k)),
                   pl.BlockSpec((1, nblk, 8, GROUP * BLOCK), lambda k, b: (k, b, 0, 0))],
        out_shape=[jax.ShapeDtypeStruct((tokens, Q_WIDTH), F32),
                   jax.ShapeDtypeStruct((N_KV, nblk_all, 8, GROUP * BLOCK), F32)],
        params=_cparams(("arbitrary", "arbitrary"), 48))


def _attn_b_bwd(qtb, kb3, ktb, vb3, do, o, lse, bias, sink, *, seq, jobs=()):
    nblk_all = qtb.shape[0]
    tokens = nblk_all * BLOCK
    batch = tokens // seq
    nblk = seq // BLOCK

    def body(sink_ref, q_ref, k_ref, kt_ref, v_ref, do_ref, o_ref, lse_ref, bias_ref,
             dq_ref, dk_ref, dv_ref, dbias_ref, dsink_ref):
        kv = pl.program_id(0)
        sink_row = _sink_row(sink_ref, kv)

        @pl.when(pl.program_id(1) == 0)
        def _():
            dbias_ref[...] = jnp.zeros_like(dbias_ref)
            dsink_ref[...] = jnp.zeros_like(dsink_ref)

        dk_ref[...] = jnp.zeros_like(dk_ref)
        dv_ref[...] = jnp.zeros_like(dv_ref)

        def block(n, dsink):
            idx = (jnp.maximum(n - 1, 0), n, jnp.minimum(n + 1, nblk - 1))
            rows = pl.ds(pl.multiple_of(n * BLOCK, BLOCK), BLOCK)
            qpad = _pad_heads(q_ref[n], kv)
            dot_t = do_ref[rows, :].T
            prod = dot_t * o_ref[rows, :].T
            delta = jnp.concatenate(
                [jnp.sum(prod[g * HEAD_DIM:(g + 1) * HEAD_DIM, :], axis=0, keepdims=True) for g in range(GROUP)],
                axis=-1)
            dopad = _pad_heads(dot_t.astype(MM), kv)
            lse_row = lse_ref[0, n][0:1, :]
            ss = _window_scores_t(k_ref, idx, qpad, bias_ref, n, nblk)
            dqt = jnp.zeros((KV_WIDTH, GROUP * BLOCK), F32)
            for piece in range(3):
                pt = jnp.exp(ss[piece] - lse_row)
                dst = pt * (_dot(v_ref[idx[piece]], dopad) - delta)
                dsb = dst.astype(MM)
                dbias_ref[0, piece] += dst
                dqt = dqt + _dot(kt_ref[idx[piece]], dsb)
                dk_ref[0, idx[piece]] += _dot_nt(dsb, qpad)
                dv_ref[0, idx[piece]] += _dot_nt(pt.astype(MM), dopad)
            dq_ref[rows, :] = _unpad_heads(dqt, kv)
            return dsink - jnp.exp(sink_row - lse_row) * delta

        dsink = lax.fori_loop(0, nblk, block, jnp.zeros((1, GROUP * BLOCK), F32))
        dsink_ref[0] += jnp.broadcast_to(dsink, (8, GROUP * BLOCK))

    qspec = pl.BlockSpec((seq, GROUP * HEAD_DIM), lambda k, b: (b, k))
    both = pl.BlockSpec((nblk, BLOCK, KV_WIDTH), lambda k, b: (b, 0, 0))
    grad = pl.BlockSpec((1, nblk, BLOCK, KV_WIDTH), lambda k, b: (k, b, 0, 0))
    return _call(
        body, (sink, qtb, kb3, ktb, vb3, do, o, lse, bias), name="attn_b_bwd", jobs=jobs,
        grid=(N_KV, batch),
        in_specs=[SMEM, pl.BlockSpec((nblk, GROUP * HEAD_DIM, BLOCK), lambda k, b: (b, k, 0)), both, both, both,
                  qspec, qspec, pl.BlockSpec((1, nblk, 8, GROUP * BLOCK), lambda k, b: (k, b, 0, 0)),
                  pl.BlockSpec((1, 3, BLOCK, GROUP * BLOCK), lambda k, b: (k, 0, 0, 0))],
        out_specs=[qspec, grad, grad,
                   pl.BlockSpec((1, 3, BLOCK, GROUP * BLOCK), lambda k, b: (k, 0, 0, 0)),
                   pl.BlockSpec((1, 8, GROUP * BLOCK), lambda k, b: (k, 0, 0))],
        out_shape=[jax.ShapeDtypeStruct((tokens, Q_WIDTH), F32),
                   jax.ShapeDtypeStruct((N_KV, nblk_all, BLOCK, KV_WIDTH), F32),
                   jax.ShapeDtypeStruct((N_KV, nblk_all, BLOCK, KV_WIDTH), F32),
                   jax.ShapeDtypeStruct((N_KV, 3, BLOCK, GROUP * BLOCK), F32),
                   jax.ShapeDtypeStruct((N_KV, 8, GROUP * BLOCK), F32)],
        params=_cparams(("arbitrary", "arbitrary"), 48))


def _wo_post(oa, ob, w_o, x, g2, g3, *, tm):
    tokens = x.shape[0]

    def body(oa_ref, ob_ref, w_ref, x_ref, g2_ref, g3_ref, mix_ref, x1_ref, h2_ref):
        mix = (_dot(oa_ref[...].astype(MM), w_ref[0:Q_WIDTH, :])
               + _dot(ob_ref[...].astype(MM), w_ref[Q_WIDTH:D_MODEL, :]))
        mix_ref[...] = mix
        x1 = x_ref[...] + mix * _rms_r(mix) * g2_ref[...]
        x1_ref[...] = x1
        h2_ref[...] = (x1 * _rms_r(x1) * g3_ref[...]).astype(MM)

    tok = lambda w: pl.BlockSpec((tm, w), lambda i: (i, 0))
    row = pl.BlockSpec((1, D_MODEL), lambda i: (0, 0))
    return pl.pallas_call(
        body, name="wo_post",
        grid=(tokens // tm,),
        in_specs=[tok(Q_WIDTH), tok(Q_WIDTH), pl.BlockSpec((D_MODEL, D_MODEL), lambda i: (0, 0)),
                  tok(D_MODEL), row, row],
        out_specs=[tok(D_MODEL), tok(D_MODEL), tok(D_MODEL)],
        out_shape=[jax.ShapeDtypeStruct((tokens, D_MODEL), F32),
                   jax.ShapeDtypeStruct((tokens, D_MODEL), F32),
                   jax.ShapeDtypeStruct((tokens, D_MODEL), MM)],
        compiler_params=_cparams(("parallel",), 40),
    )(oa, ob, w_o, x, g2, g3)


def _resident(shape):
    return pl.BlockSpec(shape, lambda i: (0,) * len(shape), pipeline_mode=pl.Buffered(1))


def _ffn_fwd_loss(h2, w_up, w_down, x1, target, g4, *, tm):
    tokens = h2.shape[0]
    nt = tokens // tm

    def body(h2_ref, wu_ref, wd_ref, x1_ref, t_ref, g4_ref, u_ref, df_ref, dy_ref, loss_ref, dg4_ref):
        h2v = h2_ref[...]
        f = jnp.zeros((tm, D_MODEL), F32)
        for c in range(N_CHIPS):
            u = jnp.maximum(_dot(h2v, wu_ref[c]), 0.0)
            u_ref[:, c * FF_CHUNK:(c + 1) * FF_CHUNK] = u.astype(MM)
            f = f + _dot((u * u).astype(MM), wd_ref[c * FF_CHUNK:(c + 1) * FF_CHUNK, :])
        r = _rms_r(f)
        g4v = g4_ref[...]
        err = x1_ref[...] + f * r * g4v - t_ref[...]
        sq = jnp.sum(err * err, axis=-1, keepdims=True)
        loss_ref[0] = jnp.broadcast_to(jnp.sum(sq, axis=0, keepdims=True) * (0.5 / D_MODEL), (8, LANES))
        dy = err * (1.0 / D_MODEL)
        dy_ref[...] = dy
        dfv, dgv = _rms_bwd(f, r, g4v, dy)
        df_ref[...] = dfv.astype(MM)
        dg4_ref[0] = jnp.sum(dgv, axis=0, keepdims=True)

    tok = pl.BlockSpec((tm, D_MODEL), lambda i: (i, 0))
    return pl.pallas_call(
        body, name="ffn_fwd_loss",
        grid=(nt,),
        in_specs=[tok, _resident((N_CHIPS, D_MODEL, FF_CHUNK)), _resident((D_FF, D_MODEL)),
                  tok, tok, pl.BlockSpec((1, D_MODEL), lambda i: (0, 0))],
        out_specs=[pl.BlockSpec((tm, D_FF), lambda i: (i, 0)), tok, tok,
                   pl.BlockSpec((1, 8, LANES), lambda i: (i, 0, 0)),
                   pl.BlockSpec((1, 1, D_MODEL), lambda i: (i, 0, 0))],
        out_shape=[jax.ShapeDtypeStruct((tokens, D_FF), MM),
                   jax.ShapeDtypeStruct((tokens, D_MODEL), MM),
                   jax.ShapeDtypeStruct((tokens, D_MODEL), F32),
                   jax.ShapeDtypeStruct((nt, 8, LANES), F32),
                   jax.ShapeDtypeStruct((nt, 1, D_MODEL), F32)],
        compiler_params=_cparams(("parallel",), 56),
    )(h2, w_up, w_down, x1, target, g4)


def _ffn_bwd_act(df, w_down, u, w_up, x1, dy, mix, g3, g2, *, tm):
    tokens = df.shape[0]
    nt = tokens // tm

    def body(df_ref, wd_ref, u_ref, wu_ref, x1_ref, dy_ref, mix_ref, g3_ref, g2_ref,
             dz_ref, dx1_ref, dmix_ref, dg3_ref, dg2_ref):
        dfv = df_ref[...]
        dh2 = jnp.zeros((tm, D_MODEL), F32)
        for c in range(N_CHIPS):
            cols = slice(c * FF_CHUNK, (c + 1) * FF_CHUNK)
            da = _dot_nt(dfv, wd_ref[cols, :])
            dz = (da * (2.0 * u_ref[:, cols].astype(F32))).astype(MM)
            dz_ref[:, cols] = dz
            dh2 = dh2 + _dot_nt(dz, wu_ref[c])
        x1 = x1_ref[...]
        dxn, dg3v = _rms_bwd(x1, _rms_r(x1), g3_ref[...], dh2)
        dx1 = dy_ref[...] + dxn
        dx1_ref[...] = dx1
        dg3_ref[0] = jnp.sum(dg3v, axis=0, keepdims=True)
        mix = mix_ref[...]
        dmix, dg2v = _rms_bwd(mix, _rms_r(mix), g2_ref[...], dx1)
        dmix_ref[...] = dmix.astype(MM)
        dg2_ref[0] = jnp.sum(dg2v, axis=0, keepdims=True)

    tok = pl.BlockSpec((tm, D_MODEL), lambda i: (i, 0))
    wide = pl.BlockSpec((tm, D_FF), lambda i: (i, 0))
    row = pl.BlockSpec((1, D_MODEL), lambda i: (0, 0))
    part = pl.BlockSpec((1, 1, D_MODEL), lambda i: (i, 0, 0))
    return pl.pallas_call(
        body, name="ffn_bwd_act",
        grid=(nt,),
        in_specs=[tok, _resident((D_FF, D_MODEL)), wide, _resident((N_CHIPS, D_MODEL, FF_CHUNK)),
                  tok, tok, tok, row, row],
        out_specs=[wide, tok, tok, part, part],
        out_shape=[jax.ShapeDtypeStruct((tokens, D_FF), MM),
                   jax.ShapeDtypeStruct((tokens, D_MODEL), F32),
                   jax.ShapeDtypeStruct((tokens, D_MODEL), MM),
                   jax.ShapeDtypeStruct((nt, 1, D_MODEL), F32),
                   jax.ShapeDtypeStruct((nt, 1, D_MODEL), F32)],
        compiler_params=_cparams(("parallel",), 56),
    )(df, w_down, u, w_up, x1, dy, mix, g3, g2)


def _tn_matmul(a, b, *, name, tm, tn, tk, chunked=False, square_a=False, vmem_mb=48, jobs=()):
    tokens, m_dim = a.shape
    n_dim = b.shape[1]
    if chunked:
        assert tm == m_dim

    def body(a_ref, b_ref, o_ref):
        av = a_ref[...]
        if square_a:
            av = av.astype(F32)
            av = av * av
        part = _dot_tn(av.astype(MM), b_ref[...].astype(MM))
        part = part[None] if chunked else part

        @pl.when(pl.program_id(2) == 0)
        def _():
            o_ref[...] = part

        @pl.when(pl.program_id(2) > 0)
        def _():
            o_ref[...] += part

    if chunked:
        out_spec = pl.BlockSpec((1, tm, tn), lambda i, j, k: (j, 0, 0))
        out_shape = jax.ShapeDtypeStruct((n_dim // tn, m_dim, tn), F32)
    else:
        out_spec = pl.BlockSpec((tm, tn), lambda i, j, k: (i, j))
        out_shape = jax.ShapeDtypeStruct((m_dim, n_dim), F32)
    (out,), job_res = _call(
        body, (a, b), name=name, jobs=jobs,
        grid=(m_dim // tm, n_dim // tn, tokens // tk),
        in_specs=[pl.BlockSpec((tk, tm), lambda i, j, k: (k, i)),
                  pl.BlockSpec((tk, tn), lambda i, j, k: (k, j))],
        out_specs=[out_spec], out_shape=[out_shape],
        params=_cparams(("arbitrary", "arbitrary", "arbitrary"), vmem_mb))
    return out, job_res


def _wo_bwd(dmix, w_o, *, tm):
    tokens = dmix.shape[0]

    def body(dm_ref, w_ref, doa_ref, dob_ref):
        dm = dm_ref[...]
        doa_ref[...] = _dot_nt(dm, w_ref[0:Q_WIDTH, :])
        dob_ref[...] = _dot_nt(dm, w_ref[Q_WIDTH:D_MODEL, :])

    tok = lambda w: pl.BlockSpec((tm, w), lambda i: (i, 0))
    return pl.pallas_call(
        body, name="wo_bwd",
        grid=(tokens // tm,),
        in_specs=[tok(D_MODEL), pl.BlockSpec((D_MODEL, D_MODEL), lambda i: (0, 0))],
        out_specs=[tok(Q_WIDTH), tok(Q_WIDTH)],
        out_shape=[jax.ShapeDtypeStruct((tokens, Q_WIDTH), F32)] * 2,
        compiler_params=_cparams(("parallel",), 40),
    )(dmix, w_o)


def _proj_bwd(dqa, dkta, dvta, dqb, dktb, dvtb, raw, x, dx1, g1, w_in, gq, gk, cq, sq, ck, sk, *, seq, tm, jobs=()):
    tokens = x.shape[0]
    nt = tokens // tm
    n_seq = seq // tm
    nblk = tm // BLOCK

    def body(dqa_ref, dkta_ref, dvta_ref, dqb_ref, dkb_ref, dvb_ref, raw_ref, x_ref, dx1_ref, g1_ref, w_ref,
             gq_ref, gk_ref, cq_ref, sq_ref, ck_ref, sk_ref,
             gx_ref, dproj_ref, dg1_ref, dgq_ref, dgk_ref, dp):
        qa = raw_ref[:, 0:Q_WIDTH]
        dqn = _rope_t(dqa_ref[...], cq_ref[...], sq_ref[...])
        rq = _head_r(qa)
        nq = qa * rq
        dnq = dqn * gq_ref[...]
        dp[:, 0:Q_WIDTH] = rq * (dnq - nq * (_seg64_sum(dnq * nq) * (1.0 / HEAD_DIM)))
        dgq_ref[0] = jnp.sum(dqn * nq, axis=0, keepdims=True)

        ka = raw_ref[:, Q_WIDTH:QK_RAW]
        dkn = _rope_t(dkta_ref[0].T, ck_ref[...], sk_ref[...])
        rk = _head_r(ka)
        nk = ka * rk
        dnk = dkn * gk_ref[...]
        dp[:, 512:640] = rk * (dnk - nk * (_seg64_sum(dnk * nk) * (1.0 / HEAD_DIM)))
        dgk_ref[0] = jnp.sum(dkn * nk, axis=0, keepdims=True)

        dp[:, 640:768] = dvta_ref[0].T
        dp[:, 768:1280] = dqb_ref[...] * SCALE
        for j in range(nblk):
            dp[j * BLOCK:(j + 1) * BLOCK, 1280:1408] = dkb_ref[0, j] + dkb_ref[1, j]
            dp[j * BLOCK:(j + 1) * BLOCK, 1408:1536] = dvb_ref[0, j] + dvb_ref[1, j]

        dproj = dp[...].astype(MM)
        dproj_ref[...] = dproj
        dh1 = _dot_nt(dproj[:, 0:IN_CHUNK], w_ref[0])
        for j in range(1, N_CHIPS):
            dh1 = dh1 + _dot_nt(dproj[:, j * IN_CHUNK:(j + 1) * IN_CHUNK], w_ref[j])
        xv = x_ref[...]
        dxn, dg1v = _rms_bwd(xv, _rms_r(xv), g1_ref[...], dh1)
        gx_ref[...] = dx1_ref[...] + dxn
        dg1_ref[0] = jnp.sum(dg1v, axis=0, keepdims=True)

    tok = lambda w: pl.BlockSpec((tm, w), lambda i: (i, 0))
    tab = lambda w: pl.BlockSpec((tm, w), lambda i: (i % n_seq, 0))
    row = lambda w: pl.BlockSpec((1, w), lambda i: (0, 0))
    tposed = pl.BlockSpec((1, KV_WIDTH, tm), lambda i: (i // n_seq, 0, i % n_seq))
    blocks = pl.BlockSpec((N_KV, nblk, BLOCK, KV_WIDTH), lambda i: (0, i, 0, 0))
    part = lambda w: pl.BlockSpec((1, 1, w), lambda i: (i, 0, 0))
    return _call(
        body, (dqa, dkta, dvta, dqb, dktb, dvtb, raw, x, dx1, g1, w_in, gq, gk, cq, sq, ck, sk),
        name="proj_bwd", jobs=jobs,
        grid=(nt,),
        in_specs=[tok(Q_WIDTH), tposed, tposed, tok(Q_WIDTH), blocks, blocks, tok(QK_RAW), tok(D_MODEL),
                  tok(D_MODEL), row(D_MODEL),
                  pl.BlockSpec((N_CHIPS, D_MODEL, IN_CHUNK), lambda i: (0, 0, 0)),
                  row(Q_WIDTH), row(KV_WIDTH), tab(KV_WIDTH), tab(KV_WIDTH), tab(KV_WIDTH), tab(KV_WIDTH)],
        out_specs=[tok(D_MODEL), tok(IN_TOTAL), part(D_MODEL), part(Q_WIDTH), part(KV_WIDTH)],
        out_shape=[jax.ShapeDtypeStruct((tokens, D_MODEL), F32),
                   jax.ShapeDtypeStruct((tokens, IN_TOTAL), MM),
                   jax.ShapeDtypeStruct((nt, 1, D_MODEL), F32),
                   jax.ShapeDtypeStruct((nt, 1, Q_WIDTH), F32),
                   jax.ShapeDtypeStruct((nt, 1, KV_WIDTH), F32)],
        scratch_shapes=[pltpu.VMEM((tm, IN_TOTAL), F32)],
        params=_cparams(("arbitrary",), 56))


def _pack_small(dg1, dg2, dg3, dg4, dgq, dgk, dsink, dbias, bucket, loss):
    def body(dg1_ref, dg2_ref, dg3_ref, dg4_ref, dgq_ref, dgk_ref, dsink_ref, dbias_ref, bucket_ref, loss_ref,
             out_ref):
        out_ref[...] = jnp.zeros_like(out_ref)
        for r, ref in ((ROW_G1, dg1_ref), (ROW_G2, dg2_ref), (ROW_G3, dg3_ref), (ROW_G4, dg4_ref)):
            acc = ref[0]
            for t in range(1, ref.shape[0]):
                acc = acc + ref[t]
            out_ref[r:r + 1, :] = acc

        def fold(ref, heads):
            acc = ref[0]
            for t in range(1, ref.shape[0]):
                acc = acc + ref[t]
            tot = acc[:, 0:HEAD_DIM]
            for h in range(1, heads):
                tot = tot + acc[:, h * HEAD_DIM:(h + 1) * HEAD_DIM]
            return tot

        out_ref[ROW_MISC:ROW_MISC + 1, MISC_GQ:MISC_GQ + HEAD_DIM] = fold(dgq_ref, GROUP * N_KV)
        out_ref[ROW_MISC:ROW_MISC + 1, MISC_GK:MISC_GK + HEAD_DIM] = fold(dgk_ref, N_KV)
        for h in range(GROUP * N_KV):
            g = h % GROUP
            out_ref[ROW_MISC:ROW_MISC + 1, MISC_SINK + h:MISC_SINK + h + 1] = jnp.sum(
                dsink_ref[h // GROUP, 0:1, g * BLOCK:(g + 1) * BLOCK], axis=-1, keepdims=True)
        lacc = loss_ref[0, 0:1, 0:1]
        for t in range(1, loss_ref.shape[0]):
            lacc = lacc + loss_ref[t, 0:1, 0:1]
        out_ref[ROW_MISC:ROW_MISC + 1, MISC_LOSS:MISC_LOSS + 1] = lacc
        heads = GROUP * N_KV
        lane = lax.broadcasted_iota(jnp.int32, (heads, N_BUCKETS), 1)
        head = lax.broadcasted_iota(jnp.int32, (heads, N_BUCKETS), 0)

        def per_bucket(b, acc):
            for h in range(heads):
                tot = jnp.zeros((1, 1), F32)
                for piece in range(3):
                    g = h % GROUP
                    sel = jnp.where(bucket_ref[piece] == b,
                                    dbias_ref[h // GROUP, piece, :, g * BLOCK:(g + 1) * BLOCK], 0.0)
                    tot = tot + jnp.sum(jnp.sum(sel, axis=-1, keepdims=True), axis=0, keepdims=True)
                acc = jnp.where((lane == b) & (head == h), tot, acc)
            return acc

        drel = lax.fori_loop(0, N_BUCKETS, per_bucket, jnp.zeros((heads, N_BUCKETS), F32))
        for h in range(heads):
            out_ref[ROW_MISC:ROW_MISC + 1, MISC_REL + h * N_BUCKETS:MISC_REL + (h + 1) * N_BUCKETS] = drel[h:h + 1, :]

    return pl.pallas_call(
        body, name="pack_small",
        in_specs=[VMEM] * 10, out_specs=VMEM,
        out_shape=jax.ShapeDtypeStruct((8, D_MODEL), F32),
        compiler_params=pltpu.CompilerParams(vmem_limit_bytes=32 * 1024 * 1024),
    )(dg1, dg2, dg3, dg4, dgq, dgk, dsink, dbias, bucket, loss)


def _gather_weights(shards, whole):
    n = len(shards)
    full = [t for t in range(n) if whole[t]]

    def body(*refs):
        ins, outs = refs[:n], refs[n:2 * n]
        stage = refs[2 * n:3 * n]
        local_sem, ici_send, ici_recv, d2d_send, d2d_recv = refs[3 * n:]
        x, y, c = _place()
        k = 2 * x + y
        sibling = (x, y, 1 - c)
        copies = []
        for t in range(n):
            stage[t][...] = ins[t][...].astype(MM)
            mine = pltpu.make_async_copy(stage[t], outs[t].at[k], local_sem.at[t])
            mine.start()
            copies.append(mine)
        sends = []
        for t in full:
            half = ins[t].shape[0] // 2
            rows = pl.ds(c * half, half)
            for r, (fx, fy) in enumerate(_CHIP_FLIPS):
                cp = _remote(stage[t].at[rows], outs[t].at[k, rows], ici_send.at[t, r], ici_recv.at[t, r],
                             (_flip(x, fx), _flip(y, fy), c))
                cp.start()
                sends.append(cp)
        for t in full:
            half = ins[t].shape[0] // 2
            rows = pl.ds(c * half, half)
            for r, (fx, fy) in enumerate(_CHIP_FLIPS):
                kk = 2 * _flip(x, fx) + _flip(y, fy)
                landed = outs[t].at[kk, rows]
                _remote(landed, landed, ici_send.at[t, r], ici_recv.at[t, r], sibling).wait_recv()
                fwd = _remote(landed, landed, d2d_send.at[t, r], d2d_recv.at[t, r], sibling)
                fwd.start()
                sends.append(fwd)
        for t in full:
            half = ins[t].shape[0] // 2
            other = pl.ds((1 - c) * half, half)
            for r, (fx, fy) in enumerate(_CHIP_FLIPS):
                kk = 2 * _flip(x, fx) + _flip(y, fy)
                theirs = outs[t].at[kk, other]
                _remote(theirs, theirs, d2d_send.at[t, r], d2d_recv.at[t, r], sibling).wait_recv()
        for cp in sends:
            cp.wait_send()
        for cp in copies:
            cp.wait()

    return pl.pallas_call(
        body, name="gather_weights",
        in_specs=[VMEM] * n, out_specs=[HBM] * n,
        out_shape=[jax.ShapeDtypeStruct((N_CHIPS,) + s.shape, MM) for s in shards],
        scratch_shapes=[pltpu.VMEM(s.shape, MM) for s in shards] + [
            pltpu.SemaphoreType.DMA((n,)),
            pltpu.SemaphoreType.DMA((n, 3)), pltpu.SemaphoreType.DMA((n, 3)),
            pltpu.SemaphoreType.DMA((n, 3)), pltpu.SemaphoreType.DMA((n, 3))],
        compiler_params=pltpu.CompilerParams(vmem_limit_bytes=40 * 1024 * 1024),
    )(*shards)


def _add_half(grad, got, where, *, name, tr):
    nch, half, cols = got.shape
    nblk = half // tr

    def body(where_ref, g_ref, r_ref, o_ref):
        o_ref[...] = g_ref[...] + r_ref[...]

    return pl.pallas_call(
        body, name=name,
        grid_spec=pltpu.PrefetchScalarGridSpec(
            num_scalar_prefetch=1, grid=(nch, nblk),
            in_specs=[pl.BlockSpec((1, tr, cols), lambda j, i, where_ref: (j, where_ref[1] * nblk + i, 0)),
                      pl.BlockSpec((1, tr, cols), lambda j, i, where_ref: (j, i, 0))],
            out_specs=pl.BlockSpec((1, tr, cols), lambda j, i, where_ref: (j, i, 0))),
        out_shape=jax.ShapeDtypeStruct(got.shape, F32),
        compiler_params=_cparams(("parallel", "parallel"), 32),
    )(where, grad, got)


def _add_chips(own, got, where, *, name, tr):
    _, half, cols = own.shape
    nblk = half // tr

    def body(where_ref, o_ref, g_ref, out_ref):
        out_ref[...] = ((o_ref[0] + g_ref[0]) + g_ref[1]) + g_ref[2]

    return pl.pallas_call(
        body, name=name,
        grid_spec=pltpu.PrefetchScalarGridSpec(
            num_scalar_prefetch=1, grid=(nblk,),
            in_specs=[pl.BlockSpec((1, tr, cols), lambda i, where_ref: (where_ref[0], i, 0)),
                      pl.BlockSpec((3, tr, cols), lambda i, where_ref: (0, i, 0))],
            out_specs=pl.BlockSpec((tr, cols), lambda i, where_ref: (where_ref[1] * nblk + i, 0))),
        out_shape=jax.ShapeDtypeStruct((2 * half, cols), F32),
        compiler_params=_cparams(("parallel",), 32),
    )(where, own, got)


def _small_job(packed):
    def copies(ins, outs, sems):
        x, y, c = _place()
        me = 4 * x + 2 * y + c
        local, send, recv = sems
        cps = [pltpu.make_async_copy(ins[0], outs[0].at[me], local.at[0])]
        for r in range(1, N_DEV):
            fx, fy, fc = (r >> 2) & 1, (r >> 1) & 1, r & 1
            cps.append(_remote(ins[0], outs[0].at[me], send.at[r - 1], recv.at[r - 1],
                               (_flip(x, fx), _flip(y, fy), _flip(c, fc))))
        return cps

    return _Job([packed], [jax.ShapeDtypeStruct((N_DEV,) + packed.shape, F32)],
                [pltpu.SemaphoreType.DMA((1,)), pltpu.SemaphoreType.DMA((N_DEV - 1,)),
                 pltpu.SemaphoreType.DMA((N_DEV - 1,))], copies)


def _adamw_math(w, g, m, v):
    m = ADAM_B1 * m + (1.0 - ADAM_B1) * g
    v = ADAM_B2 * v + (1.0 - ADAM_B2) * (g * g)
    m_hat = m / (1.0 - ADAM_B1 ** ADAM_STEP)
    v_hat = v / (1.0 - ADAM_B2 ** ADAM_STEP)
    delta = -ADAM_LR * (m_hat / (jnp.sqrt(v_hat) + ADAM_EPS) + ADAM_WD * w)
    return delta, m, v


def _adamw(w, g, m, v, *, name, tr, jobs=()):
    rows, cols = w.shape

    def body(w_ref, g_ref, m_ref, v_ref, d_ref, nm_ref, nv_ref):
        d_ref[...], nm_ref[...], nv_ref[...] = _adamw_math(w_ref[...], g_ref[...], m_ref[...], v_ref[...])

    spec = pl.BlockSpec((tr, cols), lambda i: (i, 0))
    return _call(
        body, (w, g, m, v), name=name, jobs=jobs,
        grid=(rows // tr,),
        in_specs=[spec] * 4, out_specs=[spec] * 3,
        out_shape=[jax.ShapeDtypeStruct(w.shape, F32)] * 3,
        params=_cparams(("arbitrary",), 32))


def _small_adamw(gathered, w_rows, m_rows, v_rows):
    def body(all_ref, w_ref, m_ref, v_ref, g_ref, d_ref, nm_ref, nv_ref):
        g = all_ref[0]
        for d in range(1, N_DEV):
            g = g + all_ref[d]
        g_ref[...] = g
        d_ref[...], nm_ref[...], nv_ref[...] = _adamw_math(w_ref[...], g, m_ref[...], v_ref[...])

    return pl.pallas_call(
        body, name="small_adamw",
        in_specs=[VMEM] * 4, out_specs=[VMEM] * 4,
        out_shape=[jax.ShapeDtypeStruct(w_rows.shape, F32)] * 4,
    )(gathered, w_rows, m_rows, v_rows)


def _pack_rows(g1, g2, g3, g4, gq, gk, sink, rel):
    misc = jnp.zeros((1, D_MODEL), F32)
    misc = misc.at[:, MISC_GQ:MISC_GQ + HEAD_DIM].set(gq)
    misc = misc.at[:, MISC_GK:MISC_GK + HEAD_DIM].set(gk)
    misc = misc.at[:, MISC_SINK:MISC_SINK + 8].set(sink)
    misc = misc.at[:, MISC_REL:MISC_REL + 8 * N_BUCKETS].set(rel.T.reshape(1, 8 * N_BUCKETS))
    return jnp.concatenate([g1, g2, g3, g4, misc, jnp.zeros((3, D_MODEL), F32)], axis=0)


def _unpack_rows(t):
    misc = t[ROW_MISC:ROW_MISC + 1]
    rel = misc[:, MISC_REL:MISC_REL + 8 * N_BUCKETS].reshape(8, N_BUCKETS).T
    return dict(g1=t[ROW_G1:ROW_G1 + 1], g2=t[ROW_G2:ROW_G2 + 1], g3=t[ROW_G3:ROW_G3 + 1], g4=t[ROW_G4:ROW_G4 + 1],
                gq=misc[:, MISC_GQ:MISC_GQ + HEAD_DIM], gk=misc[:, MISC_GK:MISC_GK + HEAD_DIM],
                sink=misc[:, MISC_SINK:MISC_SINK + 8], rel=rel)


def kernel(x, w_in, w_o, g_pre_mix, g_post_mix, q_norm_a, k_norm_a, sink_b, rel_bias, g_pre_ffn, w_ffn_up, w_ffn_down, g_post_ffn, loss_target, m_w_in, m_w_o, m_g_pre_mix, m_g_post_mix, m_q_norm_a, m_k_norm_a, m_sink_b, m_rel_bias, m_g_pre_ffn, m_w_ffn_up, m_w_ffn_down, m_g_post_ffn, v_w_in, v_w_o, v_g_pre_mix, v_g_post_mix, v_q_norm_a, v_k_norm_a, v_sink_b, v_rel_bias, v_g_pre_ffn, v_w_ffn_up, v_w_ffn_down, v_g_post_ffn):
    batch, seq, _ = x.shape
    tokens = batch * seq
    where = jnp.stack([2 * lax.axis_index("x") + lax.axis_index("y"), lax.axis_index("c")]).astype(jnp.int32)
    x2 = x.reshape(tokens, D_MODEL)
    g1, g2, g3, g4 = g_pre_mix, g_post_mix, g_pre_ffn, g_post_ffn

    cos, sin = _rope_tables(seq)
    ck, sk = jnp.tile(cos, (1, 2)), jnp.tile(sin, (1, 2))
    cq, sq = ck * SCALE, sk * SCALE
    gq8, gk2 = jnp.tile(q_norm_a, (1, 8)), jnp.tile(k_norm_a, (1, 2))
    bucket, band = _window_tables()
    bias = _bias_build(rel_bias.T, bucket, band)

    w_in_g, w_o_p, w_up_p, w_down_p = _gather_weights(
        (w_in[0], w_o[0], w_ffn_up[0], w_ffn_down[0]), whole=(True, False, False, False))
    (h1, raw, qa, ka, kta, va, vta, qtb, kb, ktb, vb, vtb) = _pre_proj(
        x2, g1, w_in_g, gq8, gk2, cq, sq, ck, sk, seq=seq, tm=min(512, seq))
    (oa, lse_a), (w_part,) = _attn_a_fwd(
        qa, kta, va, seq=seq, bq=min(256, seq), jobs=[_gather_job([w_o_p, w_up_p, w_down_p], forward=False)])
    kb3 = kb.reshape(tokens // BLOCK, BLOCK, KV_WIDTH)
    vb3 = vb.reshape(tokens // BLOCK, BLOCK, KV_WIDTH)
    (ob, lse_b), ((w_o_g, w_up_g, w_down_g),) = _attn_b_fwd(
        qtb, kb3, vtb, bias, sink_b, seq=seq, jobs=[_gather_job(w_part, forward=True)])
    w_o2 = w_o_g.reshape(D_MODEL, D_MODEL)
    w_down2 = w_down_g.reshape(D_FF, D_MODEL)
    mix, x1, h2 = _wo_post(oa, ob, w_o2, x2, g2, g3, tm=512)
    u, df, dy, loss_t, dg4 = _ffn_fwd_loss(h2, w_up_g, w_down2, x1, loss_target.reshape(tokens, D_MODEL), g4, tm=256)

    dz, dx1, dmix, dg3, dg2 = _ffn_bwd_act(df, w_down2, u, w_up_g, x1, dy, mix, g3, g2, tm=256)
    gw_down, _ = _tn_matmul(u, df, name="grad_w_down", tm=1024, tn=1024, tk=min(2048, tokens), square_a=True)
    gw_down = gw_down.reshape(N_CHIPS, FF_CHUNK, D_MODEL)
    gw_up, ((got_down,),) = _tn_matmul(h2, dz, name="grad_w_up", tm=1024, tn=1024, tk=min(2048, tokens), chunked=True,
                                        jobs=[_swap_job([gw_down])])
    doa, dob = _wo_bwd(dmix, w_o2, tm=512)
    gw_o = jnp.concatenate([
        _tn_matmul(oa, dmix, name="grad_w_o_a", tm=512, tn=1024, tk=min(2048, tokens))[0],
        _tn_matmul(ob, dmix, name="grad_w_o_b", tm=512, tn=1024, tk=min(2048, tokens))[0]], axis=0)
    gw_o = gw_o.reshape(N_CHIPS, O_CHUNK, D_MODEL)
    sum_down = _add_half(gw_down, got_down, where, name="add_half_w_down", tr=128)
    (dqa, dkta, dvta), ((ex_down,), (got_up,)) = _attn_a_bwd(
        qa, kta, ka, vta, doa, oa, lse_a, seq=seq, bq=min(256, seq),
        jobs=[_exchange_job([sum_down]), _swap_job([gw_up])])
    full_down = _add_chips(sum_down, ex_down, where, name="add_chips_w_down", tr=128)
    sum_up = _add_half(gw_up, got_up, where, name="add_half_w_up", tr=128)
    (dqb, dkb, dvb, dbias, dsink), ((ex_up,), (g_down,), (got_o,)) = _attn_b_bwd(
        qtb, kb3, ktb, vb3, dob, ob, lse_b, bias, sink_b, seq=seq,
        jobs=[_exchange_job([sum_up]), _join_job([full_down]), _swap_job([gw_o])])
    full_up = _add_chips(sum_up, ex_up, where, name="add_chips_w_up", tr=128)
    sum_o = _add_half(gw_o, got_o, where, name="add_half_w_o", tr=128)
    (grad_x, dproj, dg1, dgq, dgk), ((ex_o,), (g_up,)) = _proj_bwd(
        dqa, dkta, dvta, dqb, dkb, dvb, raw, x2, dx1, g1, w_in_g, gq8, gk2, cq, sq, ck, sk,
        seq=seq, tm=min(512, seq), jobs=[_exchange_job([sum_o]), _join_job([full_up])])
    full_o = _add_chips(sum_o, ex_o, where, name="add_chips_w_o", tr=128)
    packed = _pack_small(dg1, dg2, dg3, dg4, dgq, dgk, dsink, dbias, bucket, loss_t)
    gw_in, ((g_o,), (gathered,)) = _tn_matmul(h1, dproj, name="grad_w_in", tm=1024, tn=IN_CHUNK, tk=min(2048, tokens),
                                              chunked=True, jobs=[_join_job([full_o]), _small_job(packed)])

    upd_down, ((got_in,),) = _adamw(w_ffn_down[0], g_down, m_w_ffn_down[0], v_w_ffn_down[0], name="adamw_w_down",
                                    tr=128, jobs=[_swap_job([gw_in])])
    sum_in = _add_half(gw_in, got_in, where, name="add_half_w_in", tr=128)
    upd_up, ((ex_in,),) = _adamw(w_ffn_up[0], g_up, m_w_ffn_up[0], v_w_ffn_up[0], name="adamw_w_up", tr=128,
                                 jobs=[_exchange_job([sum_in])])
    full_in = _add_chips(sum_in, ex_in, where, name="add_chips_w_in", tr=128)
    upd_o, ((g_in,),) = _adamw(w_o[0], g_o, m_w_o[0], v_w_o[0], name="adamw_w_o", tr=128,
                               jobs=[_join_job([full_in])])
    upd_in, _ = _adamw(w_in[0], g_in, m_w_in[0], v_w_in[0], name="adamw_w_in", tr=128)
    big = [[t[None] for t in (g, *upd)] for g, upd in
           ((g_in, upd_in), (g_o, upd_o), (g_up, upd_up), (g_down, upd_down))]

    w_rows = _pack_rows(g1, g2, g3, g4, q_norm_a, k_norm_a, sink_b, rel_bias)
    m_rows = _pack_rows(m_g_pre_mix, m_g_post_mix, m_g_pre_ffn, m_g_post_ffn, m_q_norm_a, m_k_norm_a, m_sink_b,
                        m_rel_bias)
    v_rows = _pack_rows(v_g_pre_mix, v_g_post_mix, v_g_pre_ffn, v_g_post_ffn, v_q_norm_a, v_k_norm_a, v_sink_b,
                        v_rel_bias)
    g_rows, d_rows, nm_rows, nv_rows = _small_adamw(gathered, w_rows, m_rows, v_rows)
    loss = g_rows[ROW_MISC, MISC_LOSS]
    small = [_unpack_rows(t) for t in (g_rows, d_rows, nm_rows, nv_rows)]

    def leaves(i):
        s = small[i]
        return (big[0][i], big[1][i], s["g1"], s["g2"], s["gq"], s["gk"], s["sink"], s["rel"], s["g3"],
                big[2][i], big[3][i], s["g4"])

    return (loss, grad_x.reshape(batch, seq, D_MODEL), *leaves(0), *leaves(1), *leaves(2), *leaves(3))
```

```python
import functools

import jax
import jax.numpy as jnp
import numpy as np
from jax import lax
from jax.experimental import pallas as pl
from jax.experimental.pallas import tpu as pltpu

F32 = jnp.float32
MM = jnp.bfloat16

D_MODEL = 1024
HEAD_DIM = 64
N_KV = 2
GROUP = 4
Q_WIDTH = 512
KV_WIDTH = 128
D_FF = 4096
GRID_W = 64
BLOCK = 128
N_BUCKETS = 32
MAX_DISTANCE = 128
ROPE_THETA = 10000.0
EPS = 1e-6
NEG_INF = -1e30
SCALE = HEAD_DIM ** -0.5
IN_TOTAL = 1536
N_CHIPS = 4
N_DEV = 8
IN_CHUNK = IN_TOTAL // N_CHIPS
FF_CHUNK = D_FF // N_CHIPS
O_CHUNK = D_MODEL // N_CHIPS
QK_RAW = 640

ADAM_LR = 0.001
ADAM_B1 = 0.9
ADAM_B2 = 0.999
ADAM_EPS = 1e-08
ADAM_WD = 0.01
ADAM_STEP = 10

LANES = 128
MESH = pl.DeviceIdType.MESH
HBM = pl.BlockSpec(memory_space=pl.ANY)
VMEM = pl.BlockSpec(memory_space=pltpu.VMEM)
SMEM = pl.BlockSpec(memory_space=pltpu.SMEM)

ROW_G1, ROW_G2, ROW_G3, ROW_G4, ROW_MISC = 0, 1, 2, 3, 4
MISC_GQ, MISC_GK, MISC_SINK, MISC_LOSS = 0, 64, 128, 512


def _cparams(sem, vmem_mb):
    return pltpu.CompilerParams(dimension_semantics=sem, vmem_limit_bytes=vmem_mb * 1024 * 1024)


class _Job:
    def __init__(self, operands, out_shapes, sems, copies, alias=None):
        self.operands, self.out_shapes, self.sems, self.copies = list(operands), list(out_shapes), list(sems), copies
        self.alias = dict(alias or {})


def _place():
    return lax.axis_index("x"), lax.axis_index("y"), lax.axis_index("c")


_CHIP_FLIPS = ((1, 0), (0, 1), (1, 1))


def _flip(v, bit):
    return 1 - v if bit else v


def _remote(src, dst, send, recv, dev):
    return pltpu.make_async_remote_copy(src_ref=src, dst_ref=dst, send_sem=send, recv_sem=recv,
                                        device_id=dev, device_id_type=MESH)


def _swap_job(grads):
    n = len(grads)

    def copies(ins, outs, sems):
        x, y, c = _place()
        send, recv = sems
        cps = []
        for t in range(n):
            half = ins[t].shape[1] // 2
            cps.append(_remote(ins[t].at[:, pl.ds((1 - c) * half, half), :], outs[t], send.at[t], recv.at[t],
                               (x, y, 1 - c)))
        return cps

    shapes = [jax.ShapeDtypeStruct((g.shape[0], g.shape[1] // 2, g.shape[2]), F32) for g in grads]
    return _Job(grads, shapes, [pltpu.SemaphoreType.DMA((n,)), pltpu.SemaphoreType.DMA((n,))], copies)


def _exchange_job(sums):
    n = len(sums)

    def copies(ins, outs, sems):
        x, y, c = _place()
        send, recv = sems
        cps = []
        for t in range(n):
            for r, (fx, fy) in enumerate(_CHIP_FLIPS):
                kk = 2 * _flip(x, fx) + _flip(y, fy)
                cps.append(_remote(ins[t].at[kk], outs[t].at[r], send.at[t, r], recv.at[t, r],
                                   (_flip(x, fx), _flip(y, fy), c)))
        return cps

    shapes = [jax.ShapeDtypeStruct((3,) + s.shape[1:], F32) for s in sums]
    return _Job(sums, shapes, [pltpu.SemaphoreType.DMA((n, 3)), pltpu.SemaphoreType.DMA((n, 3))], copies)


def _join_job(fulls):
    n = len(fulls)

    def copies(ins, outs, sems):
        x, y, c = _place()
        send, recv = sems
        cps = []
        for t in range(n):
            half = ins[t].shape[0] // 2
            rows = pl.ds(c * half, half)
            cps.append(_remote(ins[t].at[rows], outs[t].at[rows], send.at[t], recv.at[t], (x, y, 1 - c)))
        return cps

    shapes = [jax.ShapeDtypeStruct(f.shape, f.dtype) for f in fulls]
    return _Job(fulls, shapes, [pltpu.SemaphoreType.DMA((n,)), pltpu.SemaphoreType.DMA((n,))], copies,
                alias={t: t for t in range(n)})


def _gather_job(bufs, forward):
    n = len(bufs)

    def copies(ins, outs, sems):
        x, y, c = _place()
        send, recv = sems
        cps = []
        for t in range(n):
            half = ins[t].shape[1] // 2
            rows = pl.ds(c * half, half)
            for r, (fx, fy) in enumerate(_CHIP_FLIPS):
                if forward:
                    kk = 2 * _flip(x, fx) + _flip(y, fy)
                    dev = (x, y, 1 - c)
                else:
                    kk = 2 * x + y
                    dev = (_flip(x, fx), _flip(y, fy), c)
                cps.append(_remote(ins[t].at[kk, rows], outs[t].at[kk, rows], send.at[t, r], recv.at[t, r], dev))
        return cps

    shapes = [jax.ShapeDtypeStruct(b.shape, b.dtype) for b in bufs]
    return _Job(bufs, shapes, [pltpu.SemaphoreType.DMA((n, 3)), pltpu.SemaphoreType.DMA((n, 3))], copies,
                alias={t: t for t in range(n)})


def _call(body, args, *, name, grid, in_specs, out_specs, out_shape, scratch_shapes=(), params=None, jobs=()):
    n_in, n_out, n_scr = len(in_specs), len(out_specs), len(scratch_shapes)
    job_in = [len(j.operands) for j in jobs]
    job_out = [len(j.out_shapes) for j in jobs]
    job_sem = [len(j.sems) for j in jobs]

    def wrapped(*refs):
        pos = 0
        ins = refs[pos:pos + n_in]; pos += n_in
        jins = []
        for k in job_in:
            jins.append(refs[pos:pos + k]); pos += k
        outs = refs[pos:pos + n_out]; pos += n_out
        jouts = []
        for k in job_out:
            jouts.append(refs[pos:pos + k]); pos += k
        scr = refs[pos:pos + n_scr]; pos += n_scr
        jsems = []
        for k in job_sem:
            jsems.append(refs[pos:pos + k]); pos += k
        if jobs:
            ids = [pl.program_id(d) for d in range(len(grid))]
            first = functools.reduce(jnp.logical_and, [i == 0 for i in ids])
            last = functools.reduce(jnp.logical_and, [i == g - 1 for i, g in zip(ids, grid)])

            @pl.when(first)
            def _():
                for j, ji, jo, js in zip(jobs, jins, jouts, jsems):
                    for cp in j.copies(ji, jo, js):
                        cp.start()

        body(*ins, *outs, *scr)
        if jobs:
            @pl.when(last)
            def _():
                for j, ji, jo, js in zip(jobs, jins, jouts, jsems):
                    for cp in j.copies(ji, jo, js):
                        cp.wait()

    aliases = {}
    in_pos, out_pos = n_in, n_out
    for j in jobs:
        for i, o in j.alias.items():
            aliases[in_pos + i] = out_pos + o
        in_pos += len(j.operands)
        out_pos += len(j.out_shapes)
    res = pl.pallas_call(
        wrapped, name=name, grid=grid,
        in_specs=list(in_specs) + [HBM] * sum(job_in),
        out_specs=list(out_specs) + [HBM] * sum(job_out),
        out_shape=list(out_shape) + [s for j in jobs for s in j.out_shapes],
        scratch_shapes=list(scratch_shapes) + [s for j in jobs for s in j.sems],
        input_output_aliases=aliases,
        compiler_params=params,
    )(*args, *[a for j in jobs for a in j.operands])
    own, rest = list(res[:n_out]), list(res[n_out:])
    job_res = []
    for k in job_out:
        job_res.append(rest[:k])
        rest = rest[k:]
    return own, job_res


def _dot(a, b):
    return jnp.dot(a, b, preferred_element_type=F32)


def _dot_nt(a, b):
    return lax.dot_general(a, b, (((1,), (1,)), ((), ())), preferred_element_type=F32)


def _dot_tn(a, b):
    return lax.dot_general(a, b, (((0,), (0,)), ((), ())), preferred_element_type=F32)


def _rms_r(x):
    return lax.rsqrt(jnp.mean(x * x, axis=-1, keepdims=True) + EPS)


def _rms_bwd(x, r, g, dy):
    n = x * r
    dn = dy * g
    dx = r * (dn - n * jnp.mean(dn * n, axis=-1, keepdims=True))
    return dx, dy * n


def _seg64_sum(v):
    rows, width = v.shape
    lane = lax.broadcasted_iota(jnp.int32, (rows, LANES), 1)
    lo = lane < HEAD_DIM
    outs = []
    for c in range(width // LANES):
        ch = v[:, c * LANES:(c + 1) * LANES]
        s_lo = jnp.sum(jnp.where(lo, ch, 0.0), axis=-1, keepdims=True)
        s_hi = jnp.sum(jnp.where(lo, 0.0, ch), axis=-1, keepdims=True)
        outs.append(jnp.where(lo, s_lo, s_hi))
    return outs[0] if len(outs) == 1 else jnp.concatenate(outs, axis=-1)


def _head_r(v):
    return lax.rsqrt(_seg64_sum(v * v) * (1.0 / HEAD_DIM) + EPS)


def _swap16(ch):
    lane = lax.broadcasted_iota(jnp.int32, ch.shape, 1)
    return jnp.where((lane % 32) < 16, pltpu.roll(ch, LANES - 16, 1), pltpu.roll(ch, 16, 1))


def _by_chunk(fn, v):
    outs = [fn(v[:, c * LANES:(c + 1) * LANES]) for c in range(v.shape[1] // LANES)]
    return outs[0] if len(outs) == 1 else jnp.concatenate(outs, axis=-1)


def _rope(v, cos, sin_signed):
    return _by_chunk(lambda ch: ch * cos + _swap16(ch) * sin_signed, v)


def _rope_t(g, cos, sin_signed):
    return _by_chunk(lambda ch: ch * cos + _swap16(ch * sin_signed), g)


def _rope_tables(seq):
    nf = HEAD_DIM // 4
    freqs = ROPE_THETA ** (-jnp.arange(nf, dtype=F32) / nf)
    pos = jnp.arange(seq, dtype=jnp.int32)
    row = (pos // GRID_W).astype(F32)
    col = (pos % GRID_W).astype(F32)
    ang_r = row[:, None] * freqs[None, :]
    ang_c = col[:, None] * freqs[None, :]
    cr, sr, cc, sc = jnp.cos(ang_r), jnp.sin(ang_r), jnp.cos(ang_c), jnp.sin(ang_c)
    cos = jnp.concatenate([cr, cr, cc, cc], axis=1)
    sin = jnp.concatenate([-sr, sr, -sc, sc], axis=1)
    return cos, sin


def _t5_bucket(rel):
    nb = N_BUCKETS // 2
    ret = (rel > 0).astype(jnp.int32) * nb
    n = jnp.abs(rel)
    max_exact = nb // 2
    nf = jnp.maximum(n, 1).astype(jnp.float32)
    large = max_exact + (jnp.log(nf / max_exact) / np.float32(np.log(MAX_DISTANCE / max_exact))
                         * (nb - max_exact)).astype(jnp.int32)
    large = jnp.minimum(large, nb - 1)
    return ret + jnp.where(n < max_exact, n, large)


def _window_tables():
    a = jnp.arange(BLOCK, dtype=jnp.int32)
    c = jnp.arange(3 * BLOCK, dtype=jnp.int32)
    rel = c[None, :] - BLOCK - a[:, None]
    bucket = _t5_bucket(rel)
    band = (jnp.abs(rel) <= BLOCK).astype(jnp.int32)
    to3 = lambda t: t.reshape(BLOCK, 3, BLOCK).transpose(1, 2, 0)
    return to3(bucket), to3(band)


def _pre_proj(x, g1, w_in, gq, gk, cq, sq, ck, sk, *, seq, tm):
    tokens = x.shape[0]
    n_seq = seq // tm
    nblk = tm // BLOCK
    batch = tokens // seq

    def body(x_ref, g1_ref, w_ref, gq_ref, gk_ref, cq_ref, sq_ref, ck_ref, sk_ref,
             h1_ref, raw_ref, qa_ref, ka_ref, kta_ref, va_ref, vta_ref,
             qtb_ref, kb_ref, ktb_ref, vb_ref, vtb_ref, proj):
        xv = x_ref[...]
        h = (xv * _rms_r(xv) * g1_ref[...]).astype(MM)
        h1_ref[...] = h
        for j in range(N_CHIPS):
            proj[:, j * IN_CHUNK:(j + 1) * IN_CHUNK] = _dot(h, w_ref[j])
        qa = proj[:, 0:Q_WIDTH]
        ka = proj[:, Q_WIDTH:QK_RAW]
        raw_ref[...] = proj[:, 0:QK_RAW]
        qn = qa * _head_r(qa) * gq_ref[...]
        qa_ref[...] = _rope(qn, cq_ref[...], sq_ref[...]).astype(MM)
        kn = ka * _head_r(ka) * gk_ref[...]
        kr = _rope(kn, ck_ref[...], sk_ref[...])
        ka_ref[...] = kr.astype(MM)
        kta_ref[0] = kr.T.astype(MM)
        va = proj[:, 640:768]
        va_ref[...] = va.astype(MM)
        vta_ref[0] = va.T.astype(MM)
        qb = proj[:, 768:1280] * SCALE
        kb = proj[:, 1280:1408]
        vb = proj[:, 1408:1536]
        kb_ref[...] = kb.astype(MM)
        vb_ref[...] = vb.astype(MM)
        for j in range(nblk):
            rows = slice(j * BLOCK, (j + 1) * BLOCK)
            qtb_ref[j] = qb[rows, :].T.astype(MM)
            ktb_ref[j] = kb[rows, :].T.astype(MM)
            vtb_ref[j] = vb[rows, :].T.astype(MM)

    tok = lambda w: pl.BlockSpec((tm, w), lambda i: (i, 0))
    tab = lambda w: pl.BlockSpec((tm, w), lambda i: (i % n_seq, 0))
    row = lambda w: pl.BlockSpec((1, w), lambda i: (0, 0))
    tposed = pl.BlockSpec((1, LANES, tm), lambda i: (i // n_seq, 0, i % n_seq))
    blocks = pl.BlockSpec((nblk, BLOCK, LANES), lambda i: (i, 0, 0))
    qblocks = pl.BlockSpec((nblk, Q_WIDTH, BLOCK), lambda i: (i, 0, 0))
    tok_mm = lambda w: jax.ShapeDtypeStruct((tokens, w), MM)
    return pl.pallas_call(
        body, name="pre_proj",
        grid=(tokens // tm,),
        in_specs=[tok(D_MODEL), row(D_MODEL),
                  pl.BlockSpec((N_CHIPS, D_MODEL, IN_CHUNK), lambda i: (0, 0, 0)),
                  row(Q_WIDTH), row(KV_WIDTH), tab(KV_WIDTH), tab(KV_WIDTH), tab(KV_WIDTH), tab(KV_WIDTH)],
        out_specs=[tok(D_MODEL), tok(QK_RAW), tok(Q_WIDTH), tok(KV_WIDTH), tposed, tok(KV_WIDTH), tposed,
                   qblocks, tok(KV_WIDTH), blocks, tok(KV_WIDTH), blocks],
        out_shape=[
            tok_mm(D_MODEL),
            jax.ShapeDtypeStruct((tokens, QK_RAW), F32),
            tok_mm(Q_WIDTH),
            tok_mm(KV_WIDTH),
            jax.ShapeDtypeStruct((batch, KV_WIDTH, seq), MM),
            tok_mm(KV_WIDTH),
            jax.ShapeDtypeStruct((batch, KV_WIDTH, seq), MM),
            jax.ShapeDtypeStruct((tokens // BLOCK, Q_WIDTH, BLOCK), MM),
            tok_mm(KV_WIDTH),
            jax.ShapeDtypeStruct((tokens // BLOCK, KV_WIDTH, BLOCK), MM),
            tok_mm(KV_WIDTH),
            jax.ShapeDtypeStruct((tokens // BLOCK, KV_WIDTH, BLOCK), MM),
        ],
        scratch_shapes=[pltpu.VMEM((tm, IN_TOTAL), F32)],
        compiler_params=_cparams(("parallel",), 48),
    )(x, g1, w_in, gq, gk, cq, sq, ck, sk)


def _kv_half(v2, kv):
    return jnp.where(kv == 0, v2[:, :HEAD_DIM], v2[:, HEAD_DIM:])


def _attn_a_fwd(qa, kta, va, *, seq, bq, jobs=()):
    tokens = qa.shape[0]
    batch = tokens // seq
    nq = seq // bq

    def body(q_ref, kt_ref, v_ref, o_ref, p_ref, linv_ref):
        kv = pl.program_id(1)
        kt = kt_ref[0]
        v = v_ref[...]
        for g in range(GROUP):
            sl = slice(g * HEAD_DIM, (g + 1) * HEAD_DIM)
            s = _dot(q_ref[:, sl], kt)
            p = jnp.exp(s - jnp.max(s, axis=-1, keepdims=True))
            linv = 1.0 / jnp.sum(p, axis=-1, keepdims=True)
            pb = p.astype(MM)
            p_ref[0, g] = pb
            o_ref[:, sl] = _kv_half(_dot(pb, v), kv) * linv
            linv_ref[0, :, g:g + 1] = linv

    return _call(
        body, (qa, kta, va), name="attn_a_fwd", jobs=jobs,
        grid=(batch, N_KV, nq),
        in_specs=[pl.BlockSpec((bq, GROUP * HEAD_DIM), lambda b, k, i: (b * nq + i, k)),
                  pl.BlockSpec((1, HEAD_DIM, seq), lambda b, k, i: (b, k, 0)),
                  pl.BlockSpec((seq, KV_WIDTH), lambda b, k, i: (b, 0))],
        out_specs=[pl.BlockSpec((bq, GROUP * HEAD_DIM), lambda b, k, i: (b * nq + i, k)),
                   pl.BlockSpec((1, GROUP, bq, seq), lambda b, k, i: (k, 0, b * nq + i, 0)),
                   pl.BlockSpec((1, bq, GROUP), lambda b, k, i: (k, b * nq + i, 0))],
        out_shape=[jax.ShapeDtypeStruct((tokens, Q_WIDTH), F32),
                   jax.ShapeDtypeStruct((N_KV, GROUP, tokens, seq), MM),
                   jax.ShapeDtypeStruct((N_KV, tokens, GROUP), F32)],
        params=_cparams(("arbitrary", "arbitrary", "arbitrary"), 56))


def _attn_a_bwd(qa, ka, vta, do, o, p, linv, *, seq, bq, jobs=()):
    tokens = qa.shape[0]
    batch = tokens // seq
    nq = seq // bq

    def body(q_ref, k_ref, vt_ref, do_ref, o_ref, p_ref, linv_ref, dq_ref, dkt_ref, dvt_ref):
        kv = pl.program_id(1)

        @pl.when(pl.program_id(2) == 0)
        def _():
            dkt_ref[...] = jnp.zeros_like(dkt_ref)
            dvt_ref[...] = jnp.zeros_like(dvt_ref)

        vt = vt_ref[0]
        k2 = k_ref[...]
        for g in range(GROUP):
            sl = slice(g * HEAD_DIM, (g + 1) * HEAD_DIM)
            dof = do_ref[:, sl]
            delta = jnp.sum(dof * o_ref[:, sl], axis=-1, keepdims=True)
            linv_g = linv_ref[0, :, g:g + 1]
            pb = p_ref[0, g]
            dp = _dot(dof.astype(MM), vt)
            ds = ((pb.astype(F32) * linv_g) * (dp - delta)).astype(MM)
            dq_ref[:, sl] = _kv_half(_dot(ds, k2), kv)
            dkt_ref[0] += _dot_tn(q_ref[:, sl], ds)
            dvt_ref[0] += _dot_tn((dof * linv_g).astype(MM), pb)

    qspec = pl.BlockSpec((bq, GROUP * HEAD_DIM), lambda b, k, i: (b * nq + i, k))
    tspec = pl.BlockSpec((1, HEAD_DIM, seq), lambda b, k, i: (b, k, 0))
    return _call(
        body, (qa, ka, vta, do, o, p, linv), name="attn_a_bwd", jobs=jobs,
        grid=(batch, N_KV, nq),
        in_specs=[qspec, pl.BlockSpec((seq, KV_WIDTH), lambda b, k, i: (b, 0)), tspec, qspec, qspec,
                  pl.BlockSpec((1, GROUP, bq, seq), lambda b, k, i: (k, 0, b * nq + i, 0)),
                  pl.BlockSpec((1, bq, GROUP), lambda b, k, i: (k, b * nq + i, 0))],
        out_specs=[qspec, tspec, tspec],
        out_shape=[jax.ShapeDtypeStruct((tokens, Q_WIDTH), F32),
                   jax.ShapeDtypeStruct((batch, KV_WIDTH, seq), F32),
                   jax.ShapeDtypeStruct((batch, KV_WIDTH, seq), F32)],
        params=_cparams(("arbitrary", "arbitrary", "arbitrary"), 56))


def _bias_build(rel_bias_t, bucket_t, band_t):
    def body(tab_ref, bucket_ref, band_ref, bias_ref):
        for h in range(GROUP * N_KV):
            for piece in range(3):
                bk = bucket_ref[piece]
                acc = jnp.zeros((BLOCK, BLOCK), F32)
                for b in range(N_BUCKETS):
                    acc = jnp.where(bk == b, tab_ref[h, b], acc)
                g = h % GROUP
                bias_ref[h // GROUP, piece, :, g * BLOCK:(g + 1) * BLOCK] = jnp.where(band_ref[piece] != 0, acc, NEG_INF)

    return pl.pallas_call(
        body, name="bias_build",
        in_specs=[SMEM, VMEM, VMEM], out_specs=VMEM,
        out_shape=jax.ShapeDtypeStruct((N_KV, 3, BLOCK, GROUP * BLOCK), F32),
    )(rel_bias_t, bucket_t, band_t)


def _pad_heads(t, kv):
    outs = []
    for g in range(GROUP):
        tg = t[g * HEAD_DIM:(g + 1) * HEAD_DIM, :]
        zero = jnp.zeros_like(tg)
        outs.append(jnp.concatenate([jnp.where(kv == 0, tg, zero), jnp.where(kv == 0, zero, tg)], axis=0))
    return jnp.concatenate(outs, axis=-1)


def _unpad_heads(t, kv):
    outs = [_kv_half(t[:, g * BLOCK:(g + 1) * BLOCK].T, kv) for g in range(GROUP)]
    return jnp.concatenate(outs, axis=-1)


def _sink_row(sink_ref, kv):
    lane_head = lax.broadcasted_iota(jnp.int32, (1, GROUP * BLOCK), 1) // BLOCK
    row = jnp.zeros((1, GROUP * BLOCK), F32)
    for g in range(GROUP):
        row = jnp.where(lane_head == g, sink_ref[0, kv * GROUP + g], row)
    return row


def _window_scores_t(k_ref, idx, qpad, bias_ref, n, nblk):
    pieces = []
    for piece in range(3):
        s = _dot(k_ref[idx[piece]], qpad) + bias_ref[0, piece]
        if piece == 0:
            s = jnp.where(n > 0, s, NEG_INF)
        if piece == 2:
            s = jnp.where(n < nblk - 1, s, NEG_INF)
        pieces.append(s)
    return pieces


def _attn_b_fwd(qtb, kb3, vtb, bias, sink, *, seq, jobs=()):
    nblk_all = qtb.shape[0]
    tokens = nblk_all * BLOCK
    batch = tokens // seq
    nblk = seq // BLOCK

    def body(sink_ref, q_ref, k_ref, vt_ref, bias_ref, o_ref, lse_ref):
        kv = pl.program_id(0)
        sink_row = _sink_row(sink_ref, kv)

        def block(n, carry):
            idx = (jnp.maximum(n - 1, 0), n, jnp.minimum(n + 1, nblk - 1))
            rows = pl.ds(pl.multiple_of(n * BLOCK, BLOCK), BLOCK)
            qpad = _pad_heads(q_ref[n], kv)
            ss = _window_scores_t(k_ref, idx, qpad, bias_ref, n, nblk)
            m = jnp.maximum(jnp.maximum(jnp.max(ss[0], axis=0, keepdims=True),
                                        jnp.max(ss[1], axis=0, keepdims=True)),
                            jnp.maximum(jnp.max(ss[2], axis=0, keepdims=True), sink_row))
            ps = [jnp.exp(s - m) for s in ss]
            denom = (jnp.sum(ps[0], axis=0, keepdims=True) + jnp.sum(ps[1], axis=0, keepdims=True)
                     + jnp.sum(ps[2], axis=0, keepdims=True) + jnp.exp(sink_row - m))
            ot = (_dot(vt_ref[idx[0]], ps[0].astype(MM)) + _dot(vt_ref[idx[1]], ps[1].astype(MM))
                  + _dot(vt_ref[idx[2]], ps[2].astype(MM)))
            o_ref[rows, :] = _unpad_heads(ot * (1.0 / denom), kv)
            lse_ref[0, n] = jnp.broadcast_to(m + jnp.log(denom), (8, GROUP * BLOCK))
            return carry

        lax.fori_loop(0, nblk, block, 0, unroll=2)

    both = pl.BlockSpec((nblk, BLOCK, KV_WIDTH), lambda k, b: (b, 0, 0))
    return _call(
        body, (sink, qtb, kb3, vtb, bias), name="attn_b_fwd", jobs=jobs,
        grid=(N_KV, batch),
        in_specs=[SMEM, pl.BlockSpec((nblk, GROUP * HEAD_DIM, BLOCK), lambda k, b: (b, k, 0)), both, both,
                  pl.BlockSpec((1, 3, BLOCK, GROUP * BLOCK), lambda k, b: (k, 0, 0, 0))],
        out_specs=[pl.BlockSpec((seq, GROUP * HEAD_DIM), lambda k, b: (b, k)),
                   pl.BlockSpec((1, nblk, 8, GROUP * BLOCK), lambda k, b: (k, b, 0, 0))],
        out_shape=[jax.ShapeDtypeStruct((tokens, Q_WIDTH), F32),
                   jax.ShapeDtypeStruct((N_KV, nblk_all, 8, GROUP * BLOCK), F32)],
        params=_cparams(("arbitrary", "arbitrary"), 48))


def _attn_b_bwd(qtb, kb3, ktb, vb3, do, o, lse, bias, sink, *, seq, jobs=()):
    nblk_all = qtb.shape[0]
    tokens = nblk_all * BLOCK
    batch = tokens // seq
    nblk = seq // BLOCK

    def body(sink_ref, q_ref, k_ref, kt_ref, v_ref, do_ref, o_ref, lse_ref, bias_ref,
             dq_ref, dk_ref, dv_ref, dbias_ref, dsink_ref):
        kv = pl.program_id(0)
        sink_row = _sink_row(sink_ref, kv)

        @pl.when(pl.program_id(1) == 0)
        def _():
            dbias_ref[...] = jnp.zeros_like(dbias_ref)
            dsink_ref[...] = jnp.zeros_like(dsink_ref)

        dk_ref[...] = jnp.zeros_like(dk_ref)
        dv_ref[...] = jnp.zeros_like(dv_ref)

        def block(n, dsink):
            idx = (jnp.maximum(n - 1, 0), n, jnp.minimum(n + 1, nblk - 1))
            rows = pl.ds(pl.multiple_of(n * BLOCK, BLOCK), BLOCK)
            qpad = _pad_heads(q_ref[n], kv)
            dot_t = do_ref[rows, :].T
            prod = dot_t * o_ref[rows, :].T
            delta = jnp.concatenate(
                [jnp.sum(prod[g * HEAD_DIM:(g + 1) * HEAD_DIM, :], axis=0, keepdims=True) for g in range(GROUP)],
                axis=-1)
            dopad = _pad_heads(dot_t.astype(MM), kv)
            lse_row = lse_ref[0, n][0:1, :]
            ss = _window_scores_t(k_ref, idx, qpad, bias_ref, n, nblk)
            dqt = jnp.zeros((KV_WIDTH, GROUP * BLOCK), F32)
            for piece in range(3):
                pt = jnp.exp(ss[piece] - lse_row)
                dst = pt * (_dot(v_ref[idx[piece]], dopad) - delta)
                dsb = dst.astype(MM)
                dbias_ref[0, piece] += dst
                dqt = dqt + _dot(kt_ref[idx[piece]], dsb)
                dk_ref[0, idx[piece]] += _dot_nt(dsb, qpad)
                dv_ref[0, idx[piece]] += _dot_nt(pt.astype(MM), dopad)
            dq_ref[rows, :] = _unpad_heads(dqt, kv)
            return dsink - jnp.exp(sink_row - lse_row) * delta

        dsink = lax.fori_loop(0, nblk // 2, lambda i, c: block(2 * i + 1, block(2 * i, c)),
                              jnp.zeros((1, GROUP * BLOCK), F32))
        dsink_ref[0] += jnp.broadcast_to(dsink, (8, GROUP * BLOCK))

    qspec = pl.BlockSpec((seq, GROUP * HEAD_DIM), lambda k, b: (b, k))
    both = pl.BlockSpec((nblk, BLOCK, KV_WIDTH), lambda k, b: (b, 0, 0))
    grad = pl.BlockSpec((1, nblk, BLOCK, KV_WIDTH), lambda k, b: (k, b, 0, 0))
    return _call(
        body, (sink, qtb, kb3, ktb, vb3, do, o, lse, bias), name="attn_b_bwd", jobs=jobs,
        grid=(N_KV, batch),
        in_specs=[SMEM, pl.BlockSpec((nblk, GROUP * HEAD_DIM, BLOCK), lambda k, b: (b, k, 0)), both, both, both,
                  qspec, qspec, pl.BlockSpec((1, nblk, 8, GROUP * BLOCK), lambda k, b: (k, b, 0, 0)),
                  pl.BlockSpec((1, 3, BLOCK, GROUP * BLOCK), lambda k, b: (k, 0, 0, 0))],
        out_specs=[qspec, grad, grad,
                   pl.BlockSpec((1, 3, BLOCK, GROUP * BLOCK), lambda k, b: (k, 0, 0, 0)),
                   pl.BlockSpec((1, 8, GROUP * BLOCK), lambda k, b: (k, 0, 0))],
        out_shape=[jax.ShapeDtypeStruct((tokens, Q_WIDTH), F32),
                   jax.ShapeDtypeStruct((N_KV, nblk_all, BLOCK, KV_WIDTH), F32),
                   jax.ShapeDtypeStruct((N_KV, nblk_all, BLOCK, KV_WIDTH), F32),
                   jax.ShapeDtypeStruct((N_KV, 3, BLOCK, GROUP * BLOCK), F32),
                   jax.ShapeDtypeStruct((N_KV, 8, GROUP * BLOCK), F32)],
        params=_cparams(("arbitrary", "arbitrary"), 48))


def _wo_post(oa, ob, w_o, x, g2, g3, *, tm):
    tokens = x.shape[0]

    def body(oa_ref, ob_ref, w_ref, x_ref, g2_ref, g3_ref, mix_ref, x1_ref, h2_ref, o_ref):
        o = jnp.concatenate([oa_ref[...].astype(MM), ob_ref[...].astype(MM)], axis=-1)
        o_ref[...] = o
        mix = _dot(o, w_ref[...])
        mix_ref[...] = mix
        x1 = x_ref[...] + mix * _rms_r(mix) * g2_ref[...]
        x1_ref[...] = x1
        h2_ref[...] = (x1 * _rms_r(x1) * g3_ref[...]).astype(MM)

    tok = lambda w: pl.BlockSpec((tm, w), lambda i: (i, 0))
    row = pl.BlockSpec((1, D_MODEL), lambda i: (0, 0))
    return pl.pallas_call(
        body, name="wo_post",
        grid=(tokens // tm,),
        in_specs=[tok(Q_WIDTH), tok(Q_WIDTH), pl.BlockSpec((D_MODEL, D_MODEL), lambda i: (0, 0)),
                  tok(D_MODEL), row, row],
        out_specs=[tok(D_MODEL), tok(D_MODEL), tok(D_MODEL), tok(D_MODEL)],
        out_shape=[jax.ShapeDtypeStruct((tokens, D_MODEL), F32),
                   jax.ShapeDtypeStruct((tokens, D_MODEL), F32),
                   jax.ShapeDtypeStruct((tokens, D_MODEL), MM),
                   jax.ShapeDtypeStruct((tokens, D_MODEL), MM)],
        compiler_params=_cparams(("parallel",), 40),
    )(oa, ob, w_o, x, g2, g3)


def _resident(shape):
    return pl.BlockSpec(shape, lambda i: (0,) * len(shape), pipeline_mode=pl.Buffered(1))


def _ffn_fwd_loss(h2, w_up, w_down, x1, target, g4, *, tm):
    tokens = h2.shape[0]
    nt = tokens // tm

    def body(h2_ref, wu_ref, wd_ref, x1_ref, t_ref, g4_ref, u_ref, df_ref, dy_ref, loss_ref, dg4_ref):
        h2v = h2_ref[...]
        f = jnp.zeros((tm, D_MODEL), F32)
        for c in range(N_CHIPS):
            u = jnp.maximum(_dot(h2v, wu_ref[c]), 0.0)
            u_ref[:, c * FF_CHUNK:(c + 1) * FF_CHUNK] = u.astype(MM)
            f = f + _dot((u * u).astype(MM), wd_ref[c * FF_CHUNK:(c + 1) * FF_CHUNK, :])
        r = _rms_r(f)
        g4v = g4_ref[...]
        err = x1_ref[...] + f * r * g4v - t_ref[...]
        sq = jnp.sum(err * err, axis=-1, keepdims=True)
        loss_ref[0] = jnp.broadcast_to(jnp.sum(sq, axis=0, keepdims=True) * (0.5 / D_MODEL), (8, LANES))
        dy = err * (1.0 / D_MODEL)
        dy_ref[...] = dy
        dfv, dgv = _rms_bwd(f, r, g4v, dy)
        df_ref[...] = dfv.astype(MM)
        dg4_ref[0] = jnp.sum(dgv, axis=0, keepdims=True)

    tok = pl.BlockSpec((tm, D_MODEL), lambda i: (i, 0))
    return pl.pallas_call(
        body, name="ffn_fwd_loss",
        grid=(nt,),
        in_specs=[tok, _resident((N_CHIPS, D_MODEL, FF_CHUNK)), _resident((D_FF, D_MODEL)),
                  tok, tok, pl.BlockSpec((1, D_MODEL), lambda i: (0, 0))],
        out_specs=[pl.BlockSpec((tm, D_FF), lambda i: (i, 0)), tok, tok,
                   pl.BlockSpec((1, 8, LANES), lambda i: (i, 0, 0)),
                   pl.BlockSpec((1, 1, D_MODEL), lambda i: (i, 0, 0))],
        out_shape=[jax.ShapeDtypeStruct((tokens, D_FF), MM),
                   jax.ShapeDtypeStruct((tokens, D_MODEL), MM),
                   jax.ShapeDtypeStruct((tokens, D_MODEL), F32),
                   jax.ShapeDtypeStruct((nt, 8, LANES), F32),
                   jax.ShapeDtypeStruct((nt, 1, D_MODEL), F32)],
        compiler_params=_cparams(("parallel",), 56),
    )(h2, w_up, w_down, x1, target, g4)


def _ffn_bwd_act(df, w_down, u, w_up, x1, dy, mix, g3, g2, *, tm):
    tokens = df.shape[0]
    nt = tokens // tm

    def body(df_ref, wd_ref, u_ref, wu_ref, x1_ref, dy_ref, mix_ref, g3_ref, g2_ref,
             dz_ref, dx1_ref, dmix_ref, dg3_ref, dg2_ref):
        dfv = df_ref[...]
        dh2 = jnp.zeros((tm, D_MODEL), F32)
        for c in range(N_CHIPS):
            cols = slice(c * FF_CHUNK, (c + 1) * FF_CHUNK)
            da = _dot_nt(dfv, wd_ref[cols, :])
            dz = (da * (2.0 * u_ref[:, cols].astype(F32))).astype(MM)
            dz_ref[:, cols] = dz
            dh2 = dh2 + _dot_nt(dz, wu_ref[c])
        x1 = x1_ref[...]
        dxn, dg3v = _rms_bwd(x1, _rms_r(x1), g3_ref[...], dh2)
        dx1 = dy_ref[...] + dxn
        dx1_ref[...] = dx1
        dg3_ref[0] = jnp.sum(dg3v, axis=0, keepdims=True)
        mix = mix_ref[...]
        dmix, dg2v = _rms_bwd(mix, _rms_r(mix), g2_ref[...], dx1)
        dmix_ref[...] = dmix.astype(MM)
        dg2_ref[0] = jnp.sum(dg2v, axis=0, keepdims=True)

    tok = pl.BlockSpec((tm, D_MODEL), lambda i: (i, 0))
    wide = pl.BlockSpec((tm, D_FF), lambda i: (i, 0))
    row = pl.BlockSpec((1, D_MODEL), lambda i: (0, 0))
    part = pl.BlockSpec((1, 1, D_MODEL), lambda i: (i, 0, 0))
    return pl.pallas_call(
        body, name="ffn_bwd_act",
        grid=(nt,),
        in_specs=[tok, _resident((D_FF, D_MODEL)), wide, _resident((N_CHIPS, D_MODEL, FF_CHUNK)),
                  tok, tok, tok, row, row],
        out_specs=[wide, tok, tok, part, part],
        out_shape=[jax.ShapeDtypeStruct((tokens, D_FF), MM),
                   jax.ShapeDtypeStruct((tokens, D_MODEL), F32),
                   jax.ShapeDtypeStruct((tokens, D_MODEL), MM),
                   jax.ShapeDtypeStruct((nt, 1, D_MODEL), F32),
                   jax.ShapeDtypeStruct((nt, 1, D_MODEL), F32)],
        compiler_params=_cparams(("parallel",), 56),
    )(df, w_down, u, w_up, x1, dy, mix, g3, g2)


def _tn_matmul(a, b, *, name, tm, tn, tk, chunked=False, square_a=False, vmem_mb=48, jobs=()):
    tokens, m_dim = a.shape
    n_dim = b.shape[1]
    if chunked:
        assert tm == m_dim

    def body(a_ref, b_ref, o_ref):
        av = a_ref[...]
        if square_a:
            av = av.astype(F32)
            av = av * av
        part = _dot_tn(av.astype(MM), b_ref[...].astype(MM))
        part = part[None] if chunked else part

        @pl.when(pl.program_id(2) == 0)
        def _():
            o_ref[...] = part

        @pl.when(pl.program_id(2) > 0)
        def _():
            o_ref[...] += part

    if chunked:
        out_spec = pl.BlockSpec((1, tm, tn), lambda i, j, k: (j, 0, 0))
        out_shape = jax.ShapeDtypeStruct((n_dim // tn, m_dim, tn), F32)
    else:
        out_spec = pl.BlockSpec((tm, tn), lambda i, j, k: (i, j))
        out_shape = jax.ShapeDtypeStruct((m_dim, n_dim), F32)
    (out,), job_res = _call(
        body, (a, b), name=name, jobs=jobs,
        grid=(m_dim // tm, n_dim // tn, tokens // tk),
        in_specs=[pl.BlockSpec((tk, tm), lambda i, j, k: (k, i)),
                  pl.BlockSpec((tk, tn), lambda i, j, k: (k, j))],
        out_specs=[out_spec], out_shape=[out_shape],
        params=_cparams(("arbitrary", "arbitrary", "arbitrary"), vmem_mb))
    return out, job_res


def _wo_bwd(dmix, w_o, *, tm):
    tokens = dmix.shape[0]

    def body(dm_ref, w_ref, doa_ref, dob_ref):
        dm = dm_ref[...]
        doa_ref[...] = _dot_nt(dm, w_ref[0:Q_WIDTH, :])
        dob_ref[...] = _dot_nt(dm, w_ref[Q_WIDTH:D_MODEL, :])

    tok = lambda w: pl.BlockSpec((tm, w), lambda i: (i, 0))
    return pl.pallas_call(
        body, name="wo_bwd",
        grid=(tokens // tm,),
        in_specs=[tok(D_MODEL), pl.BlockSpec((D_MODEL, D_MODEL), lambda i: (0, 0))],
        out_specs=[tok(Q_WIDTH), tok(Q_WIDTH)],
        out_shape=[jax.ShapeDtypeStruct((tokens, Q_WIDTH), F32)] * 2,
        compiler_params=_cparams(("parallel",), 40),
    )(dmix, w_o)


def _proj_bwd(dqa, dkta, dvta, dqb, dktb, dvtb, raw, x, dx1, g1, w_in, gq, gk, cq, sq, ck, sk, *, seq, tm, jobs=()):
    tokens = x.shape[0]
    nt = tokens // tm
    n_seq = seq // tm
    nblk = tm // BLOCK

    def body(dqa_ref, dkta_ref, dvta_ref, dqb_ref, dkb_ref, dvb_ref, raw_ref, x_ref, dx1_ref, g1_ref, w_ref,
             gq_ref, gk_ref, cq_ref, sq_ref, ck_ref, sk_ref,
             gx_ref, dproj_ref, dg1_ref, dgq_ref, dgk_ref, dp):
        qa = raw_ref[:, 0:Q_WIDTH]
        dqn = _rope_t(dqa_ref[...], cq_ref[...], sq_ref[...])
        rq = _head_r(qa)
        nq = qa * rq
        dnq = dqn * gq_ref[...]
        dp[:, 0:Q_WIDTH] = rq * (dnq - nq * (_seg64_sum(dnq * nq) * (1.0 / HEAD_DIM)))
        dgq_ref[0] = jnp.sum(dqn * nq, axis=0, keepdims=True)

        ka = raw_ref[:, Q_WIDTH:QK_RAW]
        dkn = _rope_t(dkta_ref[0].T, ck_ref[...], sk_ref[...])
        rk = _head_r(ka)
        nk = ka * rk
        dnk = dkn * gk_ref[...]
        dp[:, 512:640] = rk * (dnk - nk * (_seg64_sum(dnk * nk) * (1.0 / HEAD_DIM)))
        dgk_ref[0] = jnp.sum(dkn * nk, axis=0, keepdims=True)

        dp[:, 640:768] = dvta_ref[0].T
        dp[:, 768:1280] = dqb_ref[...] * SCALE
        for j in range(nblk):
            dp[j * BLOCK:(j + 1) * BLOCK, 1280:1408] = dkb_ref[0, j] + dkb_ref[1, j]
            dp[j * BLOCK:(j + 1) * BLOCK, 1408:1536] = dvb_ref[0, j] + dvb_ref[1, j]

        dproj = dp[...].astype(MM)
        dproj_ref[...] = dproj
        dh1 = _dot_nt(dproj[:, 0:IN_CHUNK], w_ref[0])
        for j in range(1, N_CHIPS):
            dh1 = dh1 + _dot_nt(dproj[:, j * IN_CHUNK:(j + 1) * IN_CHUNK], w_ref[j])
        xv = x_ref[...]
        dxn, dg1v = _rms_bwd(xv, _rms_r(xv), g1_ref[...], dh1)
        gx_ref[...] = dx1_ref[...] + dxn
        dg1_ref[0] = jnp.sum(dg1v, axis=0, keepdims=True)

    tok = lambda w: pl.BlockSpec((tm, w), lambda i: (i, 0))
    tab = lambda w: pl.BlockSpec((tm, w), lambda i: (i % n_seq, 0))
    row = lambda w: pl.BlockSpec((1, w), lambda i: (0, 0))
    tposed = pl.BlockSpec((1, KV_WIDTH, tm), lambda i: (i // n_seq, 0, i % n_seq))
    blocks = pl.BlockSpec((N_KV, nblk, BLOCK, KV_WIDTH), lambda i: (0, i, 0, 0))
    part = lambda w: pl.BlockSpec((1, 1, w), lambda i: (i, 0, 0))
    return _call(
        body, (dqa, dkta, dvta, dqb, dktb, dvtb, raw, x, dx1, g1, w_in, gq, gk, cq, sq, ck, sk),
        name="proj_bwd", jobs=jobs,
        grid=(nt,),
        in_specs=[tok(Q_WIDTH), tposed, tposed, tok(Q_WIDTH), blocks, blocks, tok(QK_RAW), tok(D_MODEL),
                  tok(D_MODEL), row(D_MODEL),
                  pl.BlockSpec((N_CHIPS, D_MODEL, IN_CHUNK), lambda i: (0, 0, 0)),
                  row(Q_WIDTH), row(KV_WIDTH), tab(KV_WIDTH), tab(KV_WIDTH), tab(KV_WIDTH), tab(KV_WIDTH)],
        out_specs=[tok(D_MODEL), tok(IN_TOTAL), part(D_MODEL), part(Q_WIDTH), part(KV_WIDTH)],
        out_shape=[jax.ShapeDtypeStruct((tokens, D_MODEL), F32),
                   jax.ShapeDtypeStruct((tokens, IN_TOTAL), MM),
                   jax.ShapeDtypeStruct((nt, 1, D_MODEL), F32),
                   jax.ShapeDtypeStruct((nt, 1, Q_WIDTH), F32),
                   jax.ShapeDtypeStruct((nt, 1, KV_WIDTH), F32)],
        scratch_shapes=[pltpu.VMEM((tm, IN_TOTAL), F32)],
        params=_cparams(("arbitrary",), 56))


def _pack_small(dg1, dg2, dg3, dg4, dgq, dgk, dsink, dbias, bucket, loss):
    def body(dg1_ref, dg2_ref, dg3_ref, dg4_ref, dgq_ref, dgk_ref, dsink_ref, dbias_ref, bucket_ref, loss_ref,
             out_ref, rel_ref):
        out_ref[...] = jnp.zeros_like(out_ref)
        for r, ref in ((ROW_G1, dg1_ref), (ROW_G2, dg2_ref), (ROW_G3, dg3_ref), (ROW_G4, dg4_ref)):
            acc = ref[0]
            for t in range(1, ref.shape[0]):
                acc = acc + ref[t]
            out_ref[r:r + 1, :] = acc

        def fold(ref, heads):
            acc = ref[0]
            for t in range(1, ref.shape[0]):
                acc = acc + ref[t]
            tot = acc[:, 0:HEAD_DIM]
            for h in range(1, heads):
                tot = tot + acc[:, h * HEAD_DIM:(h + 1) * HEAD_DIM]
            return tot

        out_ref[ROW_MISC:ROW_MISC + 1, MISC_GQ:MISC_GQ + HEAD_DIM] = fold(dgq_ref, GROUP * N_KV)
        out_ref[ROW_MISC:ROW_MISC + 1, MISC_GK:MISC_GK + HEAD_DIM] = fold(dgk_ref, N_KV)
        for h in range(GROUP * N_KV):
            g = h % GROUP
            out_ref[ROW_MISC:ROW_MISC + 1, MISC_SINK + h:MISC_SINK + h + 1] = jnp.sum(
                dsink_ref[h // GROUP, 0:1, g * BLOCK:(g + 1) * BLOCK], axis=-1, keepdims=True)
        lacc = loss_ref[0, 0:1, 0:1]
        for t in range(1, loss_ref.shape[0]):
            lacc = lacc + loss_ref[t, 0:1, 0:1]
        out_ref[ROW_MISC:ROW_MISC + 1, MISC_LOSS:MISC_LOSS + 1] = lacc
        lane = lax.broadcasted_iota(jnp.int32, (N_BUCKETS, LANES), 1)
        row = lax.broadcasted_iota(jnp.int32, (N_BUCKETS, LANES), 0)

        def per_bucket(b, acc):
            for h in range(GROUP * N_KV):
                tot = jnp.zeros((1, 1), F32)
                for piece in range(3):
                    g = h % GROUP
                    sel = jnp.where(bucket_ref[piece] == b,
                                    dbias_ref[h // GROUP, piece, :, g * BLOCK:(g + 1) * BLOCK], 0.0)
                    tot = tot + jnp.sum(jnp.sum(sel, axis=-1, keepdims=True), axis=0, keepdims=True)
                acc = jnp.where((row == b) & (lane == h), tot, acc)
            return acc

        rel_ref[...] = lax.fori_loop(0, N_BUCKETS, per_bucket, jnp.zeros((N_BUCKETS, LANES), F32))

    return pl.pallas_call(
        body, name="pack_small",
        in_specs=[VMEM] * 10, out_specs=[VMEM, VMEM],
        out_shape=[jax.ShapeDtypeStruct((8, D_MODEL), F32), jax.ShapeDtypeStruct((N_BUCKETS, LANES), F32)],
        compiler_params=pltpu.CompilerParams(vmem_limit_bytes=32 * 1024 * 1024),
    )(dg1, dg2, dg3, dg4, dgq, dgk, dsink, dbias, bucket, loss)


def _gather_weights(shards, whole):
    n = len(shards)
    full = [t for t in range(n) if whole[t]]

    def body(*refs):
        ins, outs = refs[:n], refs[n:2 * n]
        stage = refs[2 * n:3 * n]
        local_sem, ici_send, ici_recv, d2d_send, d2d_recv = refs[3 * n:]
        x, y, c = _place()
        k = 2 * x + y
        sibling = (x, y, 1 - c)
        copies = []
        for t in range(n):
            stage[t][...] = ins[t][...].astype(MM)
            mine = pltpu.make_async_copy(stage[t], outs[t].at[k], local_sem.at[t])
            mine.start()
            copies.append(mine)
        sends = []
        for t in full:
            half = ins[t].shape[0] // 2
            rows = pl.ds(c * half, half)
            for r, (fx, fy) in enumerate(_CHIP_FLIPS):
                cp = _remote(stage[t].at[rows], outs[t].at[k, rows], ici_send.at[t, r], ici_recv.at[t, r],
                             (_flip(x, fx), _flip(y, fy), c))
                cp.start()
                sends.append(cp)
        for t in full:
            half = ins[t].shape[0] // 2
            rows = pl.ds(c * half, half)
            for r, (fx, fy) in enumerate(_CHIP_FLIPS):
                kk = 2 * _flip(x, fx) + _flip(y, fy)
                landed = outs[t].at[kk, rows]
                _remote(landed, landed, ici_send.at[t, r], ici_recv.at[t, r], sibling).wait_recv()
                fwd = _remote(landed, landed, d2d_send.at[t, r], d2d_recv.at[t, r], sibling)
                fwd.start()
                sends.append(fwd)
        for t in full:
            half = ins[t].shape[0] // 2
            other = pl.ds((1 - c) * half, half)
            for r, (fx, fy) in enumerate(_CHIP_FLIPS):
                kk = 2 * _flip(x, fx) + _flip(y, fy)
                theirs = outs[t].at[kk, other]
                _remote(theirs, theirs, d2d_send.at[t, r], d2d_recv.at[t, r], sibling).wait_recv()
        for cp in sends:
            cp.wait_send()
        for cp in copies:
            cp.wait()

    return pl.pallas_call(
        body, name="gather_weights",
        in_specs=[VMEM] * n, out_specs=[HBM] * n,
        out_shape=[jax.ShapeDtypeStruct((N_CHIPS,) + s.shape, MM) for s in shards],
        scratch_shapes=[pltpu.VMEM(s.shape, MM) for s in shards] + [
            pltpu.SemaphoreType.DMA((n,)),
            pltpu.SemaphoreType.DMA((n, 3)), pltpu.SemaphoreType.DMA((n, 3)),
            pltpu.SemaphoreType.DMA((n, 3)), pltpu.SemaphoreType.DMA((n, 3))],
        compiler_params=pltpu.CompilerParams(vmem_limit_bytes=40 * 1024 * 1024),
    )(*shards)


def _add_half(grad, got, where, *, name, tr):
    nch, half, cols = got.shape
    nblk = half // tr

    def body(where_ref, g_ref, r_ref, o_ref):
        o_ref[...] = g_ref[...] + r_ref[...]

    return pl.pallas_call(
        body, name=name,
        grid_spec=pltpu.PrefetchScalarGridSpec(
            num_scalar_prefetch=1, grid=(nch, nblk),
            in_specs=[pl.BlockSpec((1, tr, cols), lambda j, i, where_ref: (j, where_ref[1] * nblk + i, 0)),
                      pl.BlockSpec((1, tr, cols), lambda j, i, where_ref: (j, i, 0))],
            out_specs=pl.BlockSpec((1, tr, cols), lambda j, i, where_ref: (j, i, 0))),
        out_shape=jax.ShapeDtypeStruct(got.shape, F32),
        compiler_params=_cparams(("parallel", "parallel"), 32),
    )(where, grad, got)


def _add_chips(own, got, where, *, name, tr):
    _, half, cols = own.shape
    nblk = half // tr

    def body(where_ref, o_ref, g_ref, out_ref):
        out_ref[...] = ((o_ref[0] + g_ref[0]) + g_ref[1]) + g_ref[2]

    return pl.pallas_call(
        body, name=name,
        grid_spec=pltpu.PrefetchScalarGridSpec(
            num_scalar_prefetch=1, grid=(nblk,),
            in_specs=[pl.BlockSpec((1, tr, cols), lambda i, where_ref: (where_ref[0], i, 0)),
                      pl.BlockSpec((3, tr, cols), lambda i, where_ref: (0, i, 0))],
            out_specs=pl.BlockSpec((tr, cols), lambda i, where_ref: (where_ref[1] * nblk + i, 0))),
        out_shape=jax.ShapeDtypeStruct((2 * half, cols), F32),
        compiler_params=_cparams(("parallel",), 32),
    )(where, own, got)


def _small_job(tiles):
    n = len(tiles)

    def copies(ins, outs, sems):
        x, y, c = _place()
        me = 4 * x + 2 * y + c
        local, send, recv = sems
        cps = []
        for t in range(n):
            cps.append(pltpu.make_async_copy(ins[t], outs[t].at[me], local.at[t]))
            for r in range(1, N_DEV):
                fx, fy, fc = (r >> 2) & 1, (r >> 1) & 1, r & 1
                cps.append(_remote(ins[t], outs[t].at[me], send.at[t, r - 1], recv.at[t, r - 1],
                                   (_flip(x, fx), _flip(y, fy), _flip(c, fc))))
        return cps

    return _Job(tiles, [jax.ShapeDtypeStruct((N_DEV,) + t.shape, F32) for t in tiles],
                [pltpu.SemaphoreType.DMA((n,)), pltpu.SemaphoreType.DMA((n, N_DEV - 1)),
                 pltpu.SemaphoreType.DMA((n, N_DEV - 1))], copies)


def _adamw_math(w, g, m, v):
    m = ADAM_B1 * m + (1.0 - ADAM_B1) * g
    v = ADAM_B2 * v + (1.0 - ADAM_B2) * (g * g)
    m_hat = m / (1.0 - ADAM_B1 ** ADAM_STEP)
    v_hat = v / (1.0 - ADAM_B2 ** ADAM_STEP)
    delta = -ADAM_LR * (m_hat / (jnp.sqrt(v_hat) + ADAM_EPS) + ADAM_WD * w)
    return delta, m, v


def _adamw(w, g, m, v, *, name, tr, jobs=()):
    rows, cols = w.shape

    def body(w_ref, g_ref, m_ref, v_ref, d_ref, nm_ref, nv_ref):
        d_ref[...], nm_ref[...], nv_ref[...] = _adamw_math(w_ref[...], g_ref[...], m_ref[...], v_ref[...])

    spec = pl.BlockSpec((tr, cols), lambda i: (i, 0))
    return _call(
        body, (w, g, m, v), name=name, jobs=jobs,
        grid=(rows // tr,),
        in_specs=[spec] * 4, out_specs=[spec] * 3,
        out_shape=[jax.ShapeDtypeStruct(w.shape, F32)] * 3,
        params=_cparams(("arbitrary",), 32))


def _small_adamw(gathered, gathered_rel, params, moments_m, moments_v):
    n = len(params)

    def body(all_ref, rel_all_ref, *refs):
        w_refs, m_refs, v_refs = refs[:n], refs[n:2 * n], refs[2 * n:3 * n]
        loss_ref = refs[3 * n]
        out_refs = refs[3 * n + 1:]
        g = all_ref[0]
        rel = rel_all_ref[0]
        for d in range(1, N_DEV):
            g = g + all_ref[d]
            rel = rel + rel_all_ref[d]
        misc = g[ROW_MISC:ROW_MISC + 1]
        loss_ref[...] = misc[:, MISC_LOSS:MISC_LOSS + 1]
        grads = (g[ROW_G1:ROW_G1 + 1], g[ROW_G2:ROW_G2 + 1], g[ROW_G3:ROW_G3 + 1], g[ROW_G4:ROW_G4 + 1],
                 misc[:, MISC_GQ:MISC_GQ + HEAD_DIM], misc[:, MISC_GK:MISC_GK + HEAD_DIM],
                 misc[:, MISC_SINK:MISC_SINK + GROUP * N_KV], rel[:, 0:GROUP * N_KV])
        for i in range(n):
            d, nm, nv = _adamw_math(w_refs[i][...], grads[i], m_refs[i][...], v_refs[i][...])
            for j, val in enumerate((grads[i], d, nm, nv)):
                out_refs[4 * i + j][...] = val

    outs = pl.pallas_call(
        body, name="small_adamw",
        in_specs=[VMEM] * (2 + 3 * n), out_specs=[VMEM] * (1 + 4 * n),
        out_shape=[jax.ShapeDtypeStruct((1, 1), F32)] + [jax.ShapeDtypeStruct(p.shape, F32) for p in params
                                                          for _ in range(4)],
    )(gathered, gathered_rel, *params, *moments_m, *moments_v)
    return outs[0], [outs[1 + 4 * i:5 + 4 * i] for i in range(n)]


def kernel(x, w_in, w_o, g_pre_mix, g_post_mix, q_norm_a, k_norm_a, sink_b, rel_bias, g_pre_ffn, w_ffn_up, w_ffn_down, g_post_ffn, loss_target, m_w_in, m_w_o, m_g_pre_mix, m_g_post_mix, m_q_norm_a, m_k_norm_a, m_sink_b, m_rel_bias, m_g_pre_ffn, m_w_ffn_up, m_w_ffn_down, m_g_post_ffn, v_w_in, v_w_o, v_g_pre_mix, v_g_post_mix, v_q_norm_a, v_k_norm_a, v_sink_b, v_rel_bias, v_g_pre_ffn, v_w_ffn_up, v_w_ffn_down, v_g_post_ffn):
    batch, seq, _ = x.shape
    tokens = batch * seq
    where = jnp.stack([2 * lax.axis_index("x") + lax.axis_index("y"), lax.axis_index("c")]).astype(jnp.int32)
    x2 = x.reshape(tokens, D_MODEL)
    g1, g2, g3, g4 = g_pre_mix, g_post_mix, g_pre_ffn, g_post_ffn

    cos, sin = _rope_tables(seq)
    ck, sk = jnp.tile(cos, (1, 2)), jnp.tile(sin, (1, 2))
    cq, sq = ck * SCALE, sk * SCALE
    gq8, gk2 = jnp.tile(q_norm_a, (1, 8)), jnp.tile(k_norm_a, (1, 2))
    bucket, band = _window_tables()
    bias = _bias_build(rel_bias.T, bucket, band)

    w_in_g, w_o_p, w_up_p, w_down_p = _gather_weights(
        (w_in[0], w_o[0], w_ffn_up[0], w_ffn_down[0]), whole=(True, False, False, False))
    (h1, raw, qa, ka, kta, va, vta, qtb, kb, ktb, vb, vtb) = _pre_proj(
        x2, g1, w_in_g, gq8, gk2, cq, sq, ck, sk, seq=seq, tm=min(512, seq))
    (oa, p_a, linv_a), (w_part,) = _attn_a_fwd(
        qa, kta, va, seq=seq, bq=min(256, seq), jobs=[_gather_job([w_o_p, w_up_p, w_down_p], forward=False)])
    kb3 = kb.reshape(tokens // BLOCK, BLOCK, KV_WIDTH)
    vb3 = vb.reshape(tokens // BLOCK, BLOCK, KV_WIDTH)
    (ob, lse_b), ((w_o_g, w_up_g, w_down_g),) = _attn_b_fwd(
        qtb, kb3, vtb, bias, sink_b, seq=seq, jobs=[_gather_job(w_part, forward=True)])
    w_o2 = w_o_g.reshape(D_MODEL, D_MODEL)
    w_down2 = w_down_g.reshape(D_FF, D_MODEL)
    mix, x1, h2, o_cat = _wo_post(oa, ob, w_o2, x2, g2, g3, tm=512)
    u, df, dy, loss_t, dg4 = _ffn_fwd_loss(h2, w_up_g, w_down2, x1, loss_target.reshape(tokens, D_MODEL), g4, tm=256)

    dz, dx1, dmix, dg3, dg2 = _ffn_bwd_act(df, w_down2, u, w_up_g, x1, dy, mix, g3, g2, tm=256)
    gw_down, _ = _tn_matmul(u, df, name="grad_w_down", tm=1024, tn=1024, tk=min(2048, tokens), square_a=True)
    gw_down = gw_down.reshape(N_CHIPS, FF_CHUNK, D_MODEL)
    gw_up, ((got_down,),) = _tn_matmul(h2, dz, name="grad_w_up", tm=1024, tn=1024, tk=min(2048, tokens), chunked=True,
                                        jobs=[_swap_job([gw_down])])
    doa, dob = _wo_bwd(dmix, w_o2, tm=512)
    gw_o, _ = _tn_matmul(o_cat, dmix, name="grad_w_o", tm=1024, tn=1024, tk=min(2048, tokens))
    gw_o = gw_o.reshape(N_CHIPS, O_CHUNK, D_MODEL)
    sum_down = _add_half(gw_down, got_down, where, name="add_half_w_down", tr=128)
    (dqa, dkta, dvta), ((ex_down,), (got_up,)) = _attn_a_bwd(
        qa, ka, vta, doa, oa, p_a, linv_a, seq=seq, bq=min(256, seq),
        jobs=[_exchange_job([sum_down]), _swap_job([gw_up])])
    full_down = _add_chips(sum_down, ex_down, where, name="add_chips_w_down", tr=128)
    sum_up = _add_half(gw_up, got_up, where, name="add_half_w_up", tr=128)
    (dqb, dkb, dvb, dbias, dsink), ((ex_up,), (g_down,), (got_o,)) = _attn_b_bwd(
        qtb, kb3, ktb, vb3, dob, ob, lse_b, bias, sink_b, seq=seq,
        jobs=[_exchange_job([sum_up]), _join_job([full_down]), _swap_job([gw_o])])
    full_up = _add_chips(sum_up, ex_up, where, name="add_chips_w_up", tr=128)
    sum_o = _add_half(gw_o, got_o, where, name="add_half_w_o", tr=128)
    (grad_x, dproj, dg1, dgq, dgk), ((ex_o,), (g_up,)) = _proj_bwd(
        dqa, dkta, dvta, dqb, dkb, dvb, raw, x2, dx1, g1, w_in_g, gq8, gk2, cq, sq, ck, sk,
        seq=seq, tm=min(512, seq), jobs=[_exchange_job([sum_o]), _join_job([full_up])])
    full_o = _add_chips(sum_o, ex_o, where, name="add_chips_w_o", tr=128)
    packed, packed_rel = _pack_small(dg1, dg2, dg3, dg4, dgq, dgk, dsink, dbias, bucket, loss_t)
    gw_in, ((g_o,), (gathered, gathered_rel)) = _tn_matmul(
        h1, dproj, name="grad_w_in", tm=1024, tn=IN_CHUNK, tk=min(2048, tokens), chunked=True,
        jobs=[_join_job([full_o]), _small_job([packed, packed_rel])])

    upd_down, ((got_in,),) = _adamw(w_ffn_down[0], g_down, m_w_ffn_down[0], v_w_ffn_down[0], name="adamw_w_down",
                                    tr=128, jobs=[_swap_job([gw_in])])
    sum_in = _add_half(gw_in, got_in, where, name="add_half_w_in", tr=128)
    upd_up, ((ex_in,),) = _adamw(w_ffn_up[0], g_up, m_w_ffn_up[0], v_w_ffn_up[0], name="adamw_w_up", tr=128,
                                 jobs=[_exchange_job([sum_in])])
    full_in = _add_chips(sum_in, ex_in, where, name="add_chips_w_in", tr=128)
    upd_o, ((g_in,),) = _adamw(w_o[0], g_o, m_w_o[0], v_w_o[0], name="adamw_w_o", tr=128,
                               jobs=[_join_job([full_in])])
    upd_in, _ = _adamw(w_in[0], g_in, m_w_in[0], v_w_in[0], name="adamw_w_in", tr=128)
    big = [[t[None] for t in (g, *upd)] for g, upd in
           ((g_in, upd_in), (g_o, upd_o), (g_up, upd_up), (g_down, upd_down))]

    loss, small = _small_adamw(
        gathered, gathered_rel,
        (g1, g2, g3, g4, q_norm_a, k_norm_a, sink_b, rel_bias),
        (m_g_pre_mix, m_g_post_mix, m_g_pre_ffn, m_g_post_ffn, m_q_norm_a, m_k_norm_a, m_sink_b, m_rel_bias),
        (v_g_pre_mix, v_g_post_mix, v_g_pre_ffn, v_g_post_ffn, v_q_norm_a, v_k_norm_a, v_sink_b, v_rel_bias))
    s_g1, s_g2, s_g3, s_g4, s_gq, s_gk, s_sink, s_rel = small

    def leaves(i):
        return (big[0][i], big[1][i], s_g1[i], s_g2[i], s_gq[i], s_gk[i], s_sink[i], s_rel[i], s_g3[i],
                big[2][i], big[3][i], s_g4[i])

    loss = loss.reshape(())
    return (loss, grad_x.reshape(batch, seq, D_MODEL), *leaves(0), *leaves(1), *leaves(2), *leaves(3))
```

```python
import functools

import jax
import jax.numpy as jnp
import numpy as np
from jax import lax
from jax.experimental import pallas as pl
from jax.experimental.pallas import tpu as pltpu

F32 = jnp.float32
MM = jnp.bfloat16

D_MODEL = 1024
HEAD_DIM = 64
N_KV = 2
GROUP = 4
Q_WIDTH = 512
KV_WIDTH = 128
D_FF = 4096
GRID_W = 64
BLOCK = 128
N_BUCKETS = 32
MAX_DISTANCE = 128
ROPE_THETA = 10000.0
EPS = 1e-6
NEG_INF = -1e30
SCALE = HEAD_DIM ** -0.5
IN_TOTAL = 1536
N_CHIPS = 4
N_DEV = 8
IN_CHUNK = IN_TOTAL // N_CHIPS
FF_CHUNK = D_FF // N_CHIPS
O_CHUNK = D_MODEL // N_CHIPS
QK_RAW = 640

ADAM_LR = 0.001
ADAM_B1 = 0.9
ADAM_B2 = 0.999
ADAM_EPS = 1e-08
ADAM_WD = 0.01
ADAM_STEP = 10

LANES = 128
MESH = pl.DeviceIdType.MESH
HBM = pl.BlockSpec(memory_space=pl.ANY)
VMEM = pl.BlockSpec(memory_space=pltpu.VMEM)
SMEM = pl.BlockSpec(memory_space=pltpu.SMEM)

ROW_G1, ROW_G2, ROW_G3, ROW_G4, ROW_MISC = 0, 1, 2, 3, 4
MISC_GQ, MISC_GK, MISC_SINK, MISC_LOSS = 0, 64, 128, 512


def _cparams(sem, vmem_mb):
    return pltpu.CompilerParams(dimension_semantics=sem, vmem_limit_bytes=vmem_mb * 1024 * 1024)


class _Job:
    def __init__(self, operands, out_shapes, sems, copies, alias=None):
        self.operands, self.out_shapes, self.sems, self.copies = list(operands), list(out_shapes), list(sems), copies
        self.alias = dict(alias or {})


def _place():
    return lax.axis_index("x"), lax.axis_index("y"), lax.axis_index("c")


_CHIP_FLIPS = ((1, 0), (0, 1), (1, 1))


def _flip(v, bit):
    return 1 - v if bit else v


def _remote(src, dst, send, recv, dev):
    return pltpu.make_async_remote_copy(src_ref=src, dst_ref=dst, send_sem=send, recv_sem=recv,
                                        device_id=dev, device_id_type=MESH)


def _swap_job(grads):
    n = len(grads)

    def copies(ins, outs, sems):
        x, y, c = _place()
        send, recv = sems
        cps = []
        for t in range(n):
            half = ins[t].shape[1] // 2
            cps.append(_remote(ins[t].at[:, pl.ds((1 - c) * half, half), :], outs[t], send.at[t], recv.at[t],
                               (x, y, 1 - c)))
        return cps

    shapes = [jax.ShapeDtypeStruct((g.shape[0], g.shape[1] // 2, g.shape[2]), F32) for g in grads]
    return _Job(grads, shapes, [pltpu.SemaphoreType.DMA((n,)), pltpu.SemaphoreType.DMA((n,))], copies)


def _exchange_job(sums):
    n = len(sums)

    def copies(ins, outs, sems):
        x, y, c = _place()
        send, recv = sems
        cps = []
        for t in range(n):
            for r, (fx, fy) in enumerate(_CHIP_FLIPS):
                kk = 2 * _flip(x, fx) + _flip(y, fy)
                cps.append(_remote(ins[t].at[kk], outs[t].at[r], send.at[t, r], recv.at[t, r],
                                   (_flip(x, fx), _flip(y, fy), c)))
        return cps

    shapes = [jax.ShapeDtypeStruct((3,) + s.shape[1:], F32) for s in sums]
    return _Job(sums, shapes, [pltpu.SemaphoreType.DMA((n, 3)), pltpu.SemaphoreType.DMA((n, 3))], copies)


def _join_job(fulls):
    n = len(fulls)

    def copies(ins, outs, sems):
        x, y, c = _place()
        send, recv = sems
        cps = []
        for t in range(n):
            half = ins[t].shape[0] // 2
            rows = pl.ds(c * half, half)
            cps.append(_remote(ins[t].at[rows], outs[t].at[rows], send.at[t], recv.at[t], (x, y, 1 - c)))
        return cps

    shapes = [jax.ShapeDtypeStruct(f.shape, f.dtype) for f in fulls]
    return _Job(fulls, shapes, [pltpu.SemaphoreType.DMA((n,)), pltpu.SemaphoreType.DMA((n,))], copies,
                alias={t: t for t in range(n)})


def _gather_job(bufs, forward):
    n = len(bufs)

    def copies(ins, outs, sems):
        x, y, c = _place()
        send, recv = sems
        cps = []
        for t in range(n):
            half = ins[t].shape[1] // 2
            rows = pl.ds(c * half, half)
            for r, (fx, fy) in enumerate(_CHIP_FLIPS):
                if forward:
                    kk = 2 * _flip(x, fx) + _flip(y, fy)
                    dev = (x, y, 1 - c)
                else:
                    kk = 2 * x + y
                    dev = (_flip(x, fx), _flip(y, fy), c)
                cps.append(_remote(ins[t].at[kk, rows], outs[t].at[kk, rows], send.at[t, r], recv.at[t, r], dev))
        return cps

    shapes = [jax.ShapeDtypeStruct(b.shape, b.dtype) for b in bufs]
    return _Job(bufs, shapes, [pltpu.SemaphoreType.DMA((n, 3)), pltpu.SemaphoreType.DMA((n, 3))], copies,
                alias={t: t for t in range(n)})


def _call(body, args, *, name, grid, in_specs, out_specs, out_shape, scratch_shapes=(), params=None, jobs=()):
    n_in, n_out, n_scr = len(in_specs), len(out_specs), len(scratch_shapes)
    job_in = [len(j.operands) for j in jobs]
    job_out = [len(j.out_shapes) for j in jobs]
    job_sem = [len(j.sems) for j in jobs]

    def wrapped(*refs):
        pos = 0
        ins = refs[pos:pos + n_in]; pos += n_in
        jins = []
        for k in job_in:
            jins.append(refs[pos:pos + k]); pos += k
        outs = refs[pos:pos + n_out]; pos += n_out
        jouts = []
        for k in job_out:
            jouts.append(refs[pos:pos + k]); pos += k
        scr = refs[pos:pos + n_scr]; pos += n_scr
        jsems = []
        for k in job_sem:
            jsems.append(refs[pos:pos + k]); pos += k
        if jobs:
            ids = [pl.program_id(d) for d in range(len(grid))]
            first = functools.reduce(jnp.logical_and, [i == 0 for i in ids])
            last = functools.reduce(jnp.logical_and, [i == g - 1 for i, g in zip(ids, grid)])

            @pl.when(first)
            def _():
                for j, ji, jo, js in zip(jobs, jins, jouts, jsems):
                    for cp in j.copies(ji, jo, js):
                        cp.start()

        body(*ins, *outs, *scr)
        if jobs:
            @pl.when(last)
            def _():
                for j, ji, jo, js in zip(jobs, jins, jouts, jsems):
                    for cp in j.copies(ji, jo, js):
                        cp.wait()

    aliases = {}
    in_pos, out_pos = n_in, n_out
    for j in jobs:
        for i, o in j.alias.items():
            aliases[in_pos + i] = out_pos + o
        in_pos += len(j.operands)
        out_pos += len(j.out_shapes)
    res = pl.pallas_call(
        wrapped, name=name, grid=grid,
        in_specs=list(in_specs) + [HBM] * sum(job_in),
        out_specs=list(out_specs) + [HBM] * sum(job_out),
        out_shape=list(out_shape) + [s for j in jobs for s in j.out_shapes],
        scratch_shapes=list(scratch_shapes) + [s for j in jobs for s in j.sems],
        input_output_aliases=aliases,
        compiler_params=params,
    )(*args, *[a for j in jobs for a in j.operands])
    own, rest = list(res[:n_out]), list(res[n_out:])
    job_res = []
    for k in job_out:
        job_res.append(rest[:k])
        rest = rest[k:]
    return own, job_res


def _dot(a, b):
    return jnp.dot(a, b, preferred_element_type=F32)


def _dot_nt(a, b):
    return lax.dot_general(a, b, (((1,), (1,)), ((), ())), preferred_element_type=F32)


def _dot_tn(a, b):
    return lax.dot_general(a, b, (((0,), (0,)), ((), ())), preferred_element_type=F32)


def _rms_r(x):
    return lax.rsqrt(jnp.mean(x * x, axis=-1, keepdims=True) + EPS)


def _rms_bwd(x, r, g, dy):
    n = x * r
    dn = dy * g
    dx = r * (dn - n * jnp.mean(dn * n, axis=-1, keepdims=True))
    return dx, dy * n


def _seg64_sum(v):
    rows, width = v.shape
    lane = lax.broadcasted_iota(jnp.int32, (rows, LANES), 1)
    lo = lane < HEAD_DIM
    outs = []
    for c in range(width // LANES):
        ch = v[:, c * LANES:(c + 1) * LANES]
        s_lo = jnp.sum(jnp.where(lo, ch, 0.0), axis=-1, keepdims=True)
        s_hi = jnp.sum(jnp.where(lo, 0.0, ch), axis=-1, keepdims=True)
        outs.append(jnp.where(lo, s_lo, s_hi))
    return outs[0] if len(outs) == 1 else jnp.concatenate(outs, axis=-1)


def _head_r(v):
    return lax.rsqrt(_seg64_sum(v * v) * (1.0 / HEAD_DIM) + EPS)


def _swap16(ch):
    lane = lax.broadcasted_iota(jnp.int32, ch.shape, 1)
    return jnp.where((lane % 32) < 16, pltpu.roll(ch, LANES - 16, 1), pltpu.roll(ch, 16, 1))


def _by_chunk(fn, v):
    outs = [fn(v[:, c * LANES:(c + 1) * LANES]) for c in range(v.shape[1] // LANES)]
    return outs[0] if len(outs) == 1 else jnp.concatenate(outs, axis=-1)


def _rope(v, cos, sin_signed):
    return _by_chunk(lambda ch: ch * cos + _swap16(ch) * sin_signed, v)


def _rope_t(g, cos, sin_signed):
    return _by_chunk(lambda ch: ch * cos + _swap16(ch * sin_signed), g)


def _rope_tables(seq):
    nf = HEAD_DIM // 4
    freqs = ROPE_THETA ** (-jnp.arange(nf, dtype=F32) / nf)
    pos = jnp.arange(seq, dtype=jnp.int32)
    row = (pos // GRID_W).astype(F32)
    col = (pos % GRID_W).astype(F32)
    ang_r = row[:, None] * freqs[None, :]
    ang_c = col[:, None] * freqs[None, :]
    cr, sr, cc, sc = jnp.cos(ang_r), jnp.sin(ang_r), jnp.cos(ang_c), jnp.sin(ang_c)
    cos = jnp.concatenate([cr, cr, cc, cc], axis=1)
    sin = jnp.concatenate([-sr, sr, -sc, sc], axis=1)
    return cos, sin


def _t5_bucket(rel):
    nb = N_BUCKETS // 2
    ret = (rel > 0).astype(jnp.int32) * nb
    n = jnp.abs(rel)
    max_exact = nb // 2
    nf = jnp.maximum(n, 1).astype(jnp.float32)
    large = max_exact + (jnp.log(nf / max_exact) / np.float32(np.log(MAX_DISTANCE / max_exact))
                         * (nb - max_exact)).astype(jnp.int32)
    large = jnp.minimum(large, nb - 1)
    return ret + jnp.where(n < max_exact, n, large)


def _window_tables():
    a = jnp.arange(BLOCK, dtype=jnp.int32)
    c = jnp.arange(3 * BLOCK, dtype=jnp.int32)
    rel = c[None, :] - BLOCK - a[:, None]
    bucket = _t5_bucket(rel)
    band = (jnp.abs(rel) <= BLOCK).astype(jnp.int32)
    to3 = lambda t: t.reshape(BLOCK, 3, BLOCK).transpose(1, 2, 0)
    return to3(bucket), to3(band)


def _pre_proj(x, g1, w_in, gq, gk, cq, sq, ck, sk, *, seq, tm, sub):
    tokens = x.shape[0]
    n_seq = seq // tm
    nblk = tm // BLOCK
    batch = tokens // seq

    def body(x_ref, g1_ref, w_ref, gq_ref, gk_ref, cq_ref, sq_ref, ck_ref, sk_ref,
             h1_ref, raw_ref, qa_ref, ka_ref, kta_ref, va_ref, vta_ref,
             qtb_ref, kb_ref, ktb_ref, vb_ref, vtb_ref, proj):
        for r in range(tm // sub):
            rows = slice(r * sub, (r + 1) * sub)
            xv = x_ref[rows, :]
            h = (xv * _rms_r(xv) * g1_ref[...]).astype(MM)
            h1_ref[rows, :] = h
            for j in range(N_CHIPS):
                proj[rows, j * IN_CHUNK:(j + 1) * IN_CHUNK] = _dot(h, w_ref[j])
            qa = proj[rows, 0:Q_WIDTH]
            ka = proj[rows, Q_WIDTH:QK_RAW]
            raw_ref[rows, :] = proj[rows, 0:QK_RAW]
            qn = qa * _head_r(qa) * gq_ref[...]
            qa_ref[rows, :] = _rope(qn, cq_ref[rows, :], sq_ref[rows, :]).astype(MM)
            kn = ka * _head_r(ka) * gk_ref[...]
            kr = _rope(kn, ck_ref[rows, :], sk_ref[rows, :])
            ka_ref[rows, :] = kr.astype(MM)
            kta_ref[0, :, rows] = kr.T.astype(MM)
            va = proj[rows, 640:768]
            va_ref[rows, :] = va.astype(MM)
            vta_ref[0, :, rows] = va.T.astype(MM)
            qb = proj[rows, 768:1280] * SCALE
            kb = proj[rows, 1280:1408]
            vb = proj[rows, 1408:1536]
            kb_ref[rows, :] = kb.astype(MM)
            vb_ref[rows, :] = vb.astype(MM)
            for j in range(sub // BLOCK):
                blk = slice(j * BLOCK, (j + 1) * BLOCK)
                qtb_ref[r * (sub // BLOCK) + j] = qb[blk, :].T.astype(MM)
                ktb_ref[r * (sub // BLOCK) + j] = kb[blk, :].T.astype(MM)
                vtb_ref[r * (sub // BLOCK) + j] = vb[blk, :].T.astype(MM)

    tok = lambda w: pl.BlockSpec((tm, w), lambda i: (i, 0))
    tab = lambda w: pl.BlockSpec((tm, w), lambda i: (i % n_seq, 0))
    row = lambda w: pl.BlockSpec((1, w), lambda i: (0, 0))
    tposed = pl.BlockSpec((1, LANES, tm), lambda i: (i // n_seq, 0, i % n_seq))
    blocks = pl.BlockSpec((nblk, BLOCK, LANES), lambda i: (i, 0, 0))
    qblocks = pl.BlockSpec((nblk, Q_WIDTH, BLOCK), lambda i: (i, 0, 0))
    tok_mm = lambda w: jax.ShapeDtypeStruct((tokens, w), MM)
    return pl.pallas_call(
        body, name="pre_proj",
        grid=(tokens // tm,),
        in_specs=[tok(D_MODEL), row(D_MODEL),
                  pl.BlockSpec((N_CHIPS, D_MODEL, IN_CHUNK), lambda i: (0, 0, 0)),
                  row(Q_WIDTH), row(KV_WIDTH), tab(KV_WIDTH), tab(KV_WIDTH), tab(KV_WIDTH), tab(KV_WIDTH)],
        out_specs=[tok(D_MODEL), tok(QK_RAW), tok(Q_WIDTH), tok(KV_WIDTH), tposed, tok(KV_WIDTH), tposed,
                   qblocks, tok(KV_WIDTH), blocks, tok(KV_WIDTH), blocks],
        out_shape=[
            tok_mm(D_MODEL),
            jax.ShapeDtypeStruct((tokens, QK_RAW), F32),
            tok_mm(Q_WIDTH),
            tok_mm(KV_WIDTH),
            jax.ShapeDtypeStruct((batch, KV_WIDTH, seq), MM),
            tok_mm(KV_WIDTH),
            jax.ShapeDtypeStruct((batch, KV_WIDTH, seq), MM),
            jax.ShapeDtypeStruct((tokens // BLOCK, Q_WIDTH, BLOCK), MM),
            tok_mm(KV_WIDTH),
            jax.ShapeDtypeStruct((tokens // BLOCK, KV_WIDTH, BLOCK), MM),
            tok_mm(KV_WIDTH),
            jax.ShapeDtypeStruct((tokens // BLOCK, KV_WIDTH, BLOCK), MM),
        ],
        scratch_shapes=[pltpu.VMEM((tm, IN_TOTAL), F32)],
        compiler_params=_cparams(("parallel",), 48),
    )(x, g1, w_in, gq, gk, cq, sq, ck, sk)


def _kv_half(v2, kv):
    return jnp.where(kv == 0, v2[:, :HEAD_DIM], v2[:, HEAD_DIM:])


def _attn_a_fwd(qa, kta, va, *, seq, bq, jobs=()):
    tokens = qa.shape[0]
    batch = tokens // seq
    nq = seq // bq

    def body(q_ref, kt_ref, v_ref, o_ref, p_ref, linv_ref):
        kv = pl.program_id(1)
        kt = kt_ref[0]
        v = v_ref[...]
        for g in range(GROUP):
            sl = slice(g * HEAD_DIM, (g + 1) * HEAD_DIM)
            s = _dot(q_ref[:, sl], kt)
            p = jnp.exp(s - jnp.max(s, axis=-1, keepdims=True))
            linv = 1.0 / jnp.sum(p, axis=-1, keepdims=True)
            pb = p.astype(MM)
            p_ref[0, g] = pb
            o_ref[:, sl] = _kv_half(_dot(pb, v), kv) * linv
            linv_ref[0, :, g:g + 1] = linv

    return _call(
        body, (qa, kta, va), name="attn_a_fwd", jobs=jobs,
        grid=(batch, N_KV, nq),
        in_specs=[pl.BlockSpec((bq, GROUP * HEAD_DIM), lambda b, k, i: (b * nq + i, k)),
                  pl.BlockSpec((1, HEAD_DIM, seq), lambda b, k, i: (b, k, 0)),
                  pl.BlockSpec((seq, KV_WIDTH), lambda b, k, i: (b, 0))],
        out_specs=[pl.BlockSpec((bq, GROUP * HEAD_DIM), lambda b, k, i: (b * nq + i, k)),
                   pl.BlockSpec((1, GROUP, bq, seq), lambda b, k, i: (k, 0, b * nq + i, 0)),
                   pl.BlockSpec((1, bq, GROUP), lambda b, k, i: (k, b * nq + i, 0))],
        out_shape=[jax.ShapeDtypeStruct((tokens, Q_WIDTH), F32),
                   jax.ShapeDtypeStruct((N_KV, GROUP, tokens, seq), MM),
                   jax.ShapeDtypeStruct((N_KV, tokens, GROUP), F32)],
        params=_cparams(("arbitrary", "arbitrary", "arbitrary"), 56))


def _attn_a_bwd(qa, ka, vta, do, o, p, linv, *, seq, bq, jobs=()):
    tokens = qa.shape[0]
    batch = tokens // seq
    nq = seq // bq

    def body(q_ref, k_ref, vt_ref, do_ref, o_ref, p_ref, linv_ref, dq_ref, dkt_ref, dvt_ref):
        kv = pl.program_id(1)

        @pl.when(pl.program_id(2) == 0)
        def _():
            dkt_ref[...] = jnp.zeros_like(dkt_ref)
            dvt_ref[...] = jnp.zeros_like(dvt_ref)

        vt = vt_ref[0]
        k2 = k_ref[...]
        for g in range(GROUP):
            sl = slice(g * HEAD_DIM, (g + 1) * HEAD_DIM)
            dof = do_ref[:, sl]
            delta = jnp.sum(dof * o_ref[:, sl], axis=-1, keepdims=True)
            linv_g = linv_ref[0, :, g:g + 1]
            pb = p_ref[0, g]
            dp = _dot(dof.astype(MM), vt)
            ds = ((pb.astype(F32) * linv_g) * (dp - delta)).astype(MM)
            dq_ref[:, sl] = _kv_half(_dot(ds, k2), kv)
            dkt_ref[0] += _dot_tn(q_ref[:, sl], ds)
            dvt_ref[0] += _dot_tn((dof * linv_g).astype(MM), pb)

    qspec = pl.BlockSpec((bq, GROUP * HEAD_DIM), lambda b, k, i: (b * nq + i, k))
    tspec = pl.BlockSpec((1, HEAD_DIM, seq), lambda b, k, i: (b, k, 0))
    return _call(
        body, (qa, ka, vta, do, o, p, linv), name="attn_a_bwd", jobs=jobs,
        grid=(batch, N_KV, nq),
        in_specs=[qspec, pl.BlockSpec((seq, KV_WIDTH), lambda b, k, i: (b, 0)), tspec, qspec, qspec,
                  pl.BlockSpec((1, GROUP, bq, seq), lambda b, k, i: (k, 0, b * nq + i, 0)),
                  pl.BlockSpec((1, bq, GROUP), lambda b, k, i: (k, b * nq + i, 0))],
        out_specs=[qspec, tspec, tspec],
        out_shape=[jax.ShapeDtypeStruct((tokens, Q_WIDTH), F32),
                   jax.ShapeDtypeStruct((batch, KV_WIDTH, seq), F32),
                   jax.ShapeDtypeStruct((batch, KV_WIDTH, seq), F32)],
        params=_cparams(("arbitrary", "arbitrary", "arbitrary"), 56))


def _bias_build(rel_bias_t, bucket_t, band_t):
    def body(tab_ref, bucket_ref, band_ref, bias_ref):
        for h in range(GROUP * N_KV):
            for piece in range(3):
                bk = bucket_ref[piece]
                acc = jnp.zeros((BLOCK, BLOCK), F32)
                for b in range(N_BUCKETS):
                    acc = jnp.where(bk == b, tab_ref[h, b], acc)
                g = h % GROUP
                bias_ref[h // GROUP, piece, :, g * BLOCK:(g + 1) * BLOCK] = jnp.where(band_ref[piece] != 0, acc, NEG_INF)

    return pl.pallas_call(
        body, name="bias_build",
        in_specs=[SMEM, VMEM, VMEM], out_specs=VMEM,
        out_shape=jax.ShapeDtypeStruct((N_KV, 3, BLOCK, GROUP * BLOCK), F32),
    )(rel_bias_t, bucket_t, band_t)


def _pad_heads(t, kv):
    outs = []
    for g in range(GROUP):
        tg = t[g * HEAD_DIM:(g + 1) * HEAD_DIM, :]
        zero = jnp.zeros_like(tg)
        outs.append(jnp.concatenate([jnp.where(kv == 0, tg, zero), jnp.where(kv == 0, zero, tg)], axis=0))
    return jnp.concatenate(outs, axis=-1)


def _unpad_heads(t, kv):
    outs = [_kv_half(t[:, g * BLOCK:(g + 1) * BLOCK].T, kv) for g in range(GROUP)]
    return jnp.concatenate(outs, axis=-1)


def _sink_row(sink_ref, kv):
    lane_head = lax.broadcasted_iota(jnp.int32, (1, GROUP * BLOCK), 1) // BLOCK
    row = jnp.zeros((1, GROUP * BLOCK), F32)
    for g in range(GROUP):
        row = jnp.where(lane_head == g, sink_ref[0, kv * GROUP + g], row)
    return row


def _window_scores_t(k_ref, idx, qpad, bias_ref, n, nblk):
    pieces = []
    for piece in range(3):
        s = _dot(k_ref[idx[piece]], qpad) + bias_ref[0, piece]
        if piece == 0:
            s = jnp.where(n > 0, s, NEG_INF)
        if piece == 2:
            s = jnp.where(n < nblk - 1, s, NEG_INF)
        pieces.append(s)
    return pieces


def _attn_b_fwd(qtb, kb3, vtb, bias, sink, *, seq, jobs=()):
    nblk_all = qtb.shape[0]
    tokens = nblk_all * BLOCK
    batch = tokens // seq
    nblk = seq // BLOCK

    def body(sink_ref, q_ref, k_ref, vt_ref, bias_ref, o_ref, lse_ref):
        kv = pl.program_id(0)
        sink_row = _sink_row(sink_ref, kv)

        def block(n, carry):
            idx = (jnp.maximum(n - 1, 0), n, jnp.minimum(n + 1, nblk - 1))
            rows = pl.ds(pl.multiple_of(n * BLOCK, BLOCK), BLOCK)
            qpad = _pad_heads(q_ref[n], kv)
            ss = _window_scores_t(k_ref, idx, qpad, bias_ref, n, nblk)
            m = jnp.maximum(jnp.maximum(jnp.max(ss[0], axis=0, keepdims=True),
                                        jnp.max(ss[1], axis=0, keepdims=True)),
                            jnp.maximum(jnp.max(ss[2], axis=0, keepdims=True), sink_row))
            ps = [jnp.exp(s - m) for s in ss]
            denom = (jnp.sum(ps[0], axis=0, keepdims=True) + jnp.sum(ps[1], axis=0, keepdims=True)
                     + jnp.sum(ps[2], axis=0, keepdims=True) + jnp.exp(sink_row - m))
            ot = (_dot(vt_ref[idx[0]], ps[0].astype(MM)) + _dot(vt_ref[idx[1]], ps[1].astype(MM))
                  + _dot(vt_ref[idx[2]], ps[2].astype(MM)))
            o_ref[rows, :] = _unpad_heads(ot * (1.0 / denom), kv)
            lse_ref[0, n] = jnp.broadcast_to(m + jnp.log(denom), (8, GROUP * BLOCK))
            return carry

        lax.fori_loop(0, nblk, block, 0, unroll=4)

    both = pl.BlockSpec((nblk, BLOCK, KV_WIDTH), lambda k, b: (b, 0, 0))
    return _call(
        body, (sink, qtb, kb3, vtb, bias), name="attn_b_fwd", jobs=jobs,
        grid=(N_KV, batch),
        in_specs=[SMEM, pl.BlockSpec((nblk, GROUP * HEAD_DIM, BLOCK), lambda k, b: (b, k, 0)), both, both,
                  pl.BlockSpec((1, 3, BLOCK, GROUP * BLOCK), lambda k, b: (k, 0, 0, 0))],
        out_specs=[pl.BlockSpec((seq, GROUP * HEAD_DIM), lambda k, b: (b, k)),
                   pl.BlockSpec((1, nblk, 8, GROUP * BLOCK), lambda k, b: (k, b, 0, 0))],
        out_shape=[jax.ShapeDtypeStruct((tokens, Q_WIDTH), F32),
                   jax.ShapeDtypeStruct((N_KV, nblk_all, 8, GROUP * BLOCK), F32)],
        params=_cparams(("arbitrary", "arbitrary"), 48))


def _attn_b_bwd(qtb, kb3, ktb, vb3, do, o, lse, bias, sink, *, seq, jobs=()):
    nblk_all = qtb.shape[0]
    tokens = nblk_all * BLOCK
    batch = tokens // seq
    nblk = seq // BLOCK

    def body(sink_ref, q_ref, k_ref, kt_ref, v_ref, do_ref, o_ref, lse_ref, bias_ref,
             dq_ref, dk_ref, dv_ref, dbias_ref, dsink_ref):
        kv = pl.program_id(0)
        sink_row = _sink_row(sink_ref, kv)

        @pl.when(pl.program_id(1) == 0)
        def _():
            dbias_ref[...] = jnp.zeros_like(dbias_ref)
            dsink_ref[...] = jnp.zeros_like(dsink_ref)

        dk_ref[...] = jnp.zeros_like(dk_ref)
        dv_ref[...] = jnp.zeros_like(dv_ref)

        def block(n, dsink):
            idx = (jnp.maximum(n - 1, 0), n, jnp.minimum(n + 1, nblk - 1))
            rows = pl.ds(pl.multiple_of(n * BLOCK, BLOCK), BLOCK)
            qpad = _pad_heads(q_ref[n], kv)
            dot_t = do_ref[rows, :].T
            prod = dot_t * o_ref[rows, :].T
            delta = jnp.concatenate(
                [jnp.sum(prod[g * HEAD_DIM:(g + 1) * HEAD_DIM, :], axis=0, keepdims=True) for g in range(GROUP)],
                axis=-1)
            dopad = _pad_heads(dot_t.astype(MM), kv)
            lse_row = lse_ref[0, n][0:1, :]
            ss = _window_scores_t(k_ref, idx, qpad, bias_ref, n, nblk)
            dqt = jnp.zeros((KV_WIDTH, GROUP * BLOCK), F32)
            for piece in range(3):
                pt = jnp.exp(ss[piece] - lse_row)
                dst = pt * (_dot(v_ref[idx[piece]], dopad) - delta)
                dsb = dst.astype(MM)
                dbias_ref[0, piece] += dst
                dqt = dqt + _dot(kt_ref[idx[piece]], dsb)
                dk_ref[0, idx[piece]] += _dot_nt(dsb, qpad)
                dv_ref[0, idx[piece]] += _dot_nt(pt.astype(MM), dopad)
            dq_ref[rows, :] = _unpad_heads(dqt, kv)
            return dsink - jnp.exp(sink_row - lse_row) * delta

        dsink = lax.fori_loop(
            0, nblk // 4, lambda i, c: block(4 * i + 3, block(4 * i + 2, block(4 * i + 1, block(4 * i, c)))),
            jnp.zeros((1, GROUP * BLOCK), F32))
        dsink_ref[0] += jnp.broadcast_to(dsink, (8, GROUP * BLOCK))

    qspec = pl.BlockSpec((seq, GROUP * HEAD_DIM), lambda k, b: (b, k))
    both = pl.BlockSpec((nblk, BLOCK, KV_WIDTH), lambda k, b: (b, 0, 0))
    grad = pl.BlockSpec((1, nblk, BLOCK, KV_WIDTH), lambda k, b: (k, b, 0, 0))
    return _call(
        body, (sink, qtb, kb3, ktb, vb3, do, o, lse, bias), name="attn_b_bwd", jobs=jobs,
        grid=(N_KV, batch),
        in_specs=[SMEM, pl.BlockSpec((nblk, GROUP * HEAD_DIM, BLOCK), lambda k, b: (b, k, 0)), both, both, both,
                  qspec, qspec, pl.BlockSpec((1, nblk, 8, GROUP * BLOCK), lambda k, b: (k, b, 0, 0)),
                  pl.BlockSpec((1, 3, BLOCK, GROUP * BLOCK), lambda k, b: (k, 0, 0, 0))],
        out_specs=[qspec, grad, grad,
                   pl.BlockSpec((1, 3, BLOCK, GROUP * BLOCK), lambda k, b: (k, 0, 0, 0)),
                   pl.BlockSpec((1, 8, GROUP * BLOCK), lambda k, b: (k, 0, 0))],
        out_shape=[jax.ShapeDtypeStruct((tokens, Q_WIDTH), F32),
                   jax.ShapeDtypeStruct((N_KV, nblk_all, BLOCK, KV_WIDTH), F32),
                   jax.ShapeDtypeStruct((N_KV, nblk_all, BLOCK, KV_WIDTH), F32),
                   jax.ShapeDtypeStruct((N_KV, 3, BLOCK, GROUP * BLOCK), F32),
                   jax.ShapeDtypeStruct((N_KV, 8, GROUP * BLOCK), F32)],
        params=_cparams(("arbitrary", "arbitrary"), 48))


def _wo_post(oa, ob, w_o, x, g2, g3, *, tm, sub):
    tokens = x.shape[0]

    def body(oa_ref, ob_ref, w_ref, x_ref, g2_ref, g3_ref, mix_ref, x1_ref, h2_ref, o_ref):
        for r in range(tm // sub):
            rows = slice(r * sub, (r + 1) * sub)
            o = jnp.concatenate([oa_ref[rows, :].astype(MM), ob_ref[rows, :].astype(MM)], axis=-1)
            o_ref[rows, :] = o
            mix = _dot(o, w_ref[...])
            mix_ref[rows, :] = mix
            x1 = x_ref[rows, :] + mix * _rms_r(mix) * g2_ref[...]
            x1_ref[rows, :] = x1
            h2_ref[rows, :] = (x1 * _rms_r(x1) * g3_ref[...]).astype(MM)

    tok = lambda w: pl.BlockSpec((tm, w), lambda i: (i, 0))
    row = pl.BlockSpec((1, D_MODEL), lambda i: (0, 0))
    return pl.pallas_call(
        body, name="wo_post",
        grid=(tokens // tm,),
        in_specs=[tok(Q_WIDTH), tok(Q_WIDTH), pl.BlockSpec((D_MODEL, D_MODEL), lambda i: (0, 0)),
                  tok(D_MODEL), row, row],
        out_specs=[tok(D_MODEL), tok(D_MODEL), tok(D_MODEL), tok(D_MODEL)],
        out_shape=[jax.ShapeDtypeStruct((tokens, D_MODEL), F32),
                   jax.ShapeDtypeStruct((tokens, D_MODEL), F32),
                   jax.ShapeDtypeStruct((tokens, D_MODEL), MM),
                   jax.ShapeDtypeStruct((tokens, D_MODEL), MM)],
        compiler_params=_cparams(("parallel",), 40),
    )(oa, ob, w_o, x, g2, g3)


def _resident(shape):
    return pl.BlockSpec(shape, lambda i: (0,) * len(shape), pipeline_mode=pl.Buffered(1))


def _ffn_fwd_loss(h2, w_up, w_down, x1, target, g4, *, tm):
    tokens = h2.shape[0]
    nt = tokens // tm

    def body(h2_ref, wu_ref, wd_ref, x1_ref, t_ref, g4_ref, u_ref, df_ref, dy_ref, loss_ref, dg4_ref):
        h2v = h2_ref[...]
        f = jnp.zeros((tm, D_MODEL), F32)
        for c in range(N_CHIPS):
            u = jnp.maximum(_dot(h2v, wu_ref[c]), 0.0)
            u_ref[:, c * FF_CHUNK:(c + 1) * FF_CHUNK] = u.astype(MM)
            f = f + _dot((u * u).astype(MM), wd_ref[c * FF_CHUNK:(c + 1) * FF_CHUNK, :])
        r = _rms_r(f)
        g4v = g4_ref[...]
        err = x1_ref[...] + f * r * g4v - t_ref[...]
        sq = jnp.sum(err * err, axis=-1, keepdims=True)
        loss_ref[0] = jnp.broadcast_to(jnp.sum(sq, axis=0, keepdims=True) * (0.5 / D_MODEL), (8, LANES))
        dy = err * (1.0 / D_MODEL)
        dy_ref[...] = dy
        dfv, dgv = _rms_bwd(f, r, g4v, dy)
        df_ref[...] = dfv.astype(MM)
        dg4_ref[0] = jnp.sum(dgv, axis=0, keepdims=True)

    tok = pl.BlockSpec((tm, D_MODEL), lambda i: (i, 0))
    return pl.pallas_call(
        body, name="ffn_fwd_loss",
        grid=(nt,),
        in_specs=[tok, _resident((N_CHIPS, D_MODEL, FF_CHUNK)), _resident((D_FF, D_MODEL)),
                  tok, tok, pl.BlockSpec((1, D_MODEL), lambda i: (0, 0))],
        out_specs=[pl.BlockSpec((tm, D_FF), lambda i: (i, 0)), tok, tok,
                   pl.BlockSpec((1, 8, LANES), lambda i: (i, 0, 0)),
                   pl.BlockSpec((1, 1, D_MODEL), lambda i: (i, 0, 0))],
        out_shape=[jax.ShapeDtypeStruct((tokens, D_FF), MM),
                   jax.ShapeDtypeStruct((tokens, D_MODEL), MM),
                   jax.ShapeDtypeStruct((tokens, D_MODEL), F32),
                   jax.ShapeDtypeStruct((nt, 8, LANES), F32),
                   jax.ShapeDtypeStruct((nt, 1, D_MODEL), F32)],
        compiler_params=_cparams(("parallel",), 56),
    )(h2, w_up, w_down, x1, target, g4)


def _ffn_bwd_act(df, w_down, u, w_up, x1, dy, mix, g3, g2, *, tm):
    tokens = df.shape[0]
    nt = tokens // tm

    def body(df_ref, wd_ref, u_ref, wu_ref, x1_ref, dy_ref, mix_ref, g3_ref, g2_ref,
             dz_ref, dx1_ref, dmix_ref, dg3_ref, dg2_ref):
        dfv = df_ref[...]
        dh2 = jnp.zeros((tm, D_MODEL), F32)
        for c in range(N_CHIPS):
            cols = slice(c * FF_CHUNK, (c + 1) * FF_CHUNK)
            da = _dot_nt(dfv, wd_ref[cols, :])
            dz = (da * (2.0 * u_ref[:, cols].astype(F32))).astype(MM)
            dz_ref[:, cols] = dz
            dh2 = dh2 + _dot_nt(dz, wu_ref[c])
        x1 = x1_ref[...]
        dxn, dg3v = _rms_bwd(x1, _rms_r(x1), g3_ref[...], dh2)
        dx1 = dy_ref[...] + dxn
        dx1_ref[...] = dx1
        dg3_ref[0] = jnp.sum(dg3v, axis=0, keepdims=True)
        mix = mix_ref[...]
        dmix, dg2v = _rms_bwd(mix, _rms_r(mix), g2_ref[...], dx1)
        dmix_ref[...] = dmix.astype(MM)
        dg2_ref[0] = jnp.sum(dg2v, axis=0, keepdims=True)

    tok = pl.BlockSpec((tm, D_MODEL), lambda i: (i, 0))
    wide = pl.BlockSpec((tm, D_FF), lambda i: (i, 0))
    row = pl.BlockSpec((1, D_MODEL), lambda i: (0, 0))
    part = pl.BlockSpec((1, 1, D_MODEL), lambda i: (i, 0, 0))
    return pl.pallas_call(
        body, name="ffn_bwd_act",
        grid=(nt,),
        in_specs=[tok, _resident((D_FF, D_MODEL)), wide, _resident((N_CHIPS, D_MODEL, FF_CHUNK)),
                  tok, tok, tok, row, row],
        out_specs=[wide, tok, tok, part, part],
        out_shape=[jax.ShapeDtypeStruct((tokens, D_FF), MM),
                   jax.ShapeDtypeStruct((tokens, D_MODEL), F32),
                   jax.ShapeDtypeStruct((tokens, D_MODEL), MM),
                   jax.ShapeDtypeStruct((nt, 1, D_MODEL), F32),
                   jax.ShapeDtypeStruct((nt, 1, D_MODEL), F32)],
        compiler_params=_cparams(("parallel",), 56),
    )(df, w_down, u, w_up, x1, dy, mix, g3, g2)


def _tn_matmul(a, b, *, name, tm, tn, tk, chunked=False, square_a=False, vmem_mb=48, jobs=()):
    tokens, m_dim = a.shape
    n_dim = b.shape[1]
    if chunked:
        assert tm == m_dim

    def body(a_ref, b_ref, o_ref):
        av = a_ref[...]
        if square_a:
            av = av.astype(F32)
            av = av * av
        part = _dot_tn(av.astype(MM), b_ref[...].astype(MM))
        part = part[None] if chunked else part

        @pl.when(pl.program_id(2) == 0)
        def _():
            o_ref[...] = part

        @pl.when(pl.program_id(2) > 0)
        def _():
            o_ref[...] += part

    if chunked:
        out_spec = pl.BlockSpec((1, tm, tn), lambda i, j, k: (j, 0, 0))
        out_shape = jax.ShapeDtypeStruct((n_dim // tn, m_dim, tn), F32)
    else:
        out_spec = pl.BlockSpec((tm, tn), lambda i, j, k: (i, j))
        out_shape = jax.ShapeDtypeStruct((m_dim, n_dim), F32)
    (out,), job_res = _call(
        body, (a, b), name=name, jobs=jobs,
        grid=(m_dim // tm, n_dim // tn, tokens // tk),
        in_specs=[pl.BlockSpec((tk, tm), lambda i, j, k: (k, i)),
                  pl.BlockSpec((tk, tn), lambda i, j, k: (k, j))],
        out_specs=[out_spec], out_shape=[out_shape],
        params=_cparams(("arbitrary", "arbitrary", "arbitrary"), vmem_mb))
    return out, job_res


def _wo_bwd(dmix, w_o, *, tm):
    tokens = dmix.shape[0]

    def body(dm_ref, w_ref, doa_ref, dob_ref):
        dm = dm_ref[...]
        doa_ref[...] = _dot_nt(dm, w_ref[0:Q_WIDTH, :])
        dob_ref[...] = _dot_nt(dm, w_ref[Q_WIDTH:D_MODEL, :])

    tok = lambda w: pl.BlockSpec((tm, w), lambda i: (i, 0))
    return pl.pallas_call(
        body, name="wo_bwd",
        grid=(tokens // tm,),
        in_specs=[tok(D_MODEL), pl.BlockSpec((D_MODEL, D_MODEL), lambda i: (0, 0))],
        out_specs=[tok(Q_WIDTH), tok(Q_WIDTH)],
        out_shape=[jax.ShapeDtypeStruct((tokens, Q_WIDTH), F32)] * 2,
        compiler_params=_cparams(("parallel",), 40),
    )(dmix, w_o)


def _proj_bwd(dqa, dkta, dvta, dqb, dktb, dvtb, raw, x, dx1, g1, w_in, gq, gk, cq, sq, ck, sk, *, seq, tm, sub,
              jobs=()):
    tokens = x.shape[0]
    nt = tokens // tm
    n_seq = seq // tm
    nblk = tm // BLOCK

    def body(dqa_ref, dkta_ref, dvta_ref, dqb_ref, dkb_ref, dvb_ref, raw_ref, x_ref, dx1_ref, g1_ref, w_ref,
             gq_ref, gk_ref, cq_ref, sq_ref, ck_ref, sk_ref,
             gx_ref, dproj_ref, dg1_ref, dgq_ref, dgk_ref, dp):
        parts = []
        for r in range(tm // sub):
            rows = slice(r * sub, (r + 1) * sub)
            qa = raw_ref[rows, 0:Q_WIDTH]
            dqn = _rope_t(dqa_ref[rows, :], cq_ref[rows, :], sq_ref[rows, :])
            rq = _head_r(qa)
            nq = qa * rq
            dnq = dqn * gq_ref[...]
            dp[rows, 0:Q_WIDTH] = rq * (dnq - nq * (_seg64_sum(dnq * nq) * (1.0 / HEAD_DIM)))

            ka = raw_ref[rows, Q_WIDTH:QK_RAW]
            dkn = _rope_t(dkta_ref[0, :, rows].T, ck_ref[rows, :], sk_ref[rows, :])
            rk = _head_r(ka)
            nk = ka * rk
            dnk = dkn * gk_ref[...]
            dp[rows, 512:640] = rk * (dnk - nk * (_seg64_sum(dnk * nk) * (1.0 / HEAD_DIM)))

            dp[rows, 640:768] = dvta_ref[0, :, rows].T
            dp[rows, 768:1280] = dqb_ref[rows, :] * SCALE
            for j in range(r * sub // BLOCK, (r + 1) * sub // BLOCK):
                dp[j * BLOCK:(j + 1) * BLOCK, 1280:1408] = dkb_ref[0, j] + dkb_ref[1, j]
                dp[j * BLOCK:(j + 1) * BLOCK, 1408:1536] = dvb_ref[0, j] + dvb_ref[1, j]

            dproj = dp[rows, :].astype(MM)
            dproj_ref[rows, :] = dproj
            dh1 = _dot_nt(dproj[:, 0:IN_CHUNK], w_ref[0])
            for j in range(1, N_CHIPS):
                dh1 = dh1 + _dot_nt(dproj[:, j * IN_CHUNK:(j + 1) * IN_CHUNK], w_ref[j])
            xv = x_ref[rows, :]
            dxn, dg1v = _rms_bwd(xv, _rms_r(xv), g1_ref[...], dh1)
            gx_ref[rows, :] = dx1_ref[rows, :] + dxn
            parts.append((jnp.sum(dqn * nq, axis=0, keepdims=True), jnp.sum(dkn * nk, axis=0, keepdims=True),
                          jnp.sum(dg1v, axis=0, keepdims=True)))
        dgq_ref[0] = functools.reduce(jnp.add, [p[0] for p in parts])
        dgk_ref[0] = functools.reduce(jnp.add, [p[1] for p in parts])
        dg1_ref[0] = functools.reduce(jnp.add, [p[2] for p in parts])

    tok = lambda w: pl.BlockSpec((tm, w), lambda i: (i, 0))
    tab = lambda w: pl.BlockSpec((tm, w), lambda i: (i % n_seq, 0))
    row = lambda w: pl.BlockSpec((1, w), lambda i: (0, 0))
    tposed = pl.BlockSpec((1, KV_WIDTH, tm), lambda i: (i // n_seq, 0, i % n_seq))
    blocks = pl.BlockSpec((N_KV, nblk, BLOCK, KV_WIDTH), lambda i: (0, i, 0, 0))
    part = lambda w: pl.BlockSpec((1, 1, w), lambda i: (i, 0, 0))
    return _call(
        body, (dqa, dkta, dvta, dqb, dktb, dvtb, raw, x, dx1, g1, w_in, gq, gk, cq, sq, ck, sk),
        name="proj_bwd", jobs=jobs,
        grid=(nt,),
        in_specs=[tok(Q_WIDTH), tposed, tposed, tok(Q_WIDTH), blocks, blocks, tok(QK_RAW), tok(D_MODEL),
                  tok(D_MODEL), row(D_MODEL),
                  pl.BlockSpec((N_CHIPS, D_MODEL, IN_CHUNK), lambda i: (0, 0, 0)),
                  row(Q_WIDTH), row(KV_WIDTH), tab(KV_WIDTH), tab(KV_WIDTH), tab(KV_WIDTH), tab(KV_WIDTH)],
        out_specs=[tok(D_MODEL), tok(IN_TOTAL), part(D_MODEL), part(Q_WIDTH), part(KV_WIDTH)],
        out_shape=[jax.ShapeDtypeStruct((tokens, D_MODEL), F32),
                   jax.ShapeDtypeStruct((tokens, IN_TOTAL), MM),
                   jax.ShapeDtypeStruct((nt, 1, D_MODEL), F32),
                   jax.ShapeDtypeStruct((nt, 1, Q_WIDTH), F32),
                   jax.ShapeDtypeStruct((nt, 1, KV_WIDTH), F32)],
        scratch_shapes=[pltpu.VMEM((tm, IN_TOTAL), F32)],
        params=_cparams(("arbitrary",), 56))


def _pack_small(dg1, dg2, dg3, dg4, dgq, dgk, dsink, dbias, bucket, loss):
    def body(dg1_ref, dg2_ref, dg3_ref, dg4_ref, dgq_ref, dgk_ref, dsink_ref, dbias_ref, bucket_ref, loss_ref,
             out_ref, rel_ref):
        out_ref[...] = jnp.zeros_like(out_ref)
        for r, ref in ((ROW_G1, dg1_ref), (ROW_G2, dg2_ref), (ROW_G3, dg3_ref), (ROW_G4, dg4_ref)):
            acc = ref[0]
            for t in range(1, ref.shape[0]):
                acc = acc + ref[t]
            out_ref[r:r + 1, :] = acc

        def fold(ref, heads):
            acc = ref[0]
            for t in range(1, ref.shape[0]):
                acc = acc + ref[t]
            tot = acc[:, 0:HEAD_DIM]
            for h in range(1, heads):
                tot = tot + acc[:, h * HEAD_DIM:(h + 1) * HEAD_DIM]
            return tot

        out_ref[ROW_MISC:ROW_MISC + 1, MISC_GQ:MISC_GQ + HEAD_DIM] = fold(dgq_ref, GROUP * N_KV)
        out_ref[ROW_MISC:ROW_MISC + 1, MISC_GK:MISC_GK + HEAD_DIM] = fold(dgk_ref, N_KV)
        for h in range(GROUP * N_KV):
            g = h % GROUP
            out_ref[ROW_MISC:ROW_MISC + 1, MISC_SINK + h:MISC_SINK + h + 1] = jnp.sum(
                dsink_ref[h // GROUP, 0:1, g * BLOCK:(g + 1) * BLOCK], axis=-1, keepdims=True)
        lacc = loss_ref[0, 0:1, 0:1]
        for t in range(1, loss_ref.shape[0]):
            lacc = lacc + loss_ref[t, 0:1, 0:1]
        out_ref[ROW_MISC:ROW_MISC + 1, MISC_LOSS:MISC_LOSS + 1] = lacc
        lane = lax.broadcasted_iota(jnp.int32, (N_BUCKETS, LANES), 1)
        row = lax.broadcasted_iota(jnp.int32, (N_BUCKETS, LANES), 0)

        def per_bucket(b, acc):
            for h in range(GROUP * N_KV):
                g = h % GROUP
                sel = jnp.zeros((BLOCK, BLOCK), F32)
                for piece in range(3):
                    sel = sel + jnp.where(bucket_ref[piece] == b,
                                          dbias_ref[h // GROUP, piece, :, g * BLOCK:(g + 1) * BLOCK], 0.0)
                tot = jnp.sum(jnp.sum(sel, axis=0, keepdims=True), axis=-1, keepdims=True)
                acc = jnp.where((row == b) & (lane == h), tot, acc)
            return acc

        rel_ref[...] = lax.fori_loop(0, N_BUCKETS, per_bucket, jnp.zeros((N_BUCKETS, LANES), F32))

    return pl.pallas_call(
        body, name="pack_small",
        in_specs=[VMEM] * 10, out_specs=[VMEM, VMEM],
        out_shape=[jax.ShapeDtypeStruct((8, D_MODEL), F32), jax.ShapeDtypeStruct((N_BUCKETS, LANES), F32)],
        compiler_params=pltpu.CompilerParams(vmem_limit_bytes=32 * 1024 * 1024),
    )(dg1, dg2, dg3, dg4, dgq, dgk, dsink, dbias, bucket, loss)


def _gather_weights(shards, whole):
    n = len(shards)
    full = [t for t in range(n) if whole[t]]

    def body(*refs):
        ins, outs = refs[:n], refs[n:2 * n]
        stage = refs[2 * n:3 * n]
        local_sem, ici_send, ici_recv, d2d_send, d2d_recv = refs[3 * n:]
        x, y, c = _place()
        k = 2 * x + y
        sibling = (x, y, 1 - c)
        copies = []
        for t in range(n):
            stage[t][...] = ins[t][...].astype(MM)
            mine = pltpu.make_async_copy(stage[t], outs[t].at[k], local_sem.at[t])
            mine.start()
            copies.append(mine)
        sends = []
        for t in full:
            half = ins[t].shape[0] // 2
            rows = pl.ds(c * half, half)
            for r, (fx, fy) in enumerate(_CHIP_FLIPS):
                cp = _remote(stage[t].at[rows], outs[t].at[k, rows], ici_send.at[t, r], ici_recv.at[t, r],
                             (_flip(x, fx), _flip(y, fy), c))
                cp.start()
                sends.append(cp)
        for t in full:
            half = ins[t].shape[0] // 2
            rows = pl.ds(c * half, half)
            for r, (fx, fy) in enumerate(_CHIP_FLIPS):
                kk = 2 * _flip(x, fx) + _flip(y, fy)
                landed = outs[t].at[kk, rows]
                _remote(landed, landed, ici_send.at[t, r], ici_recv.at[t, r], sibling).wait_recv()
                fwd = _remote(landed, landed, d2d_send.at[t, r], d2d_recv.at[t, r], sibling)
                fwd.start()
                sends.append(fwd)
        for t in full:
            half = ins[t].shape[0] // 2
            other = pl.ds((1 - c) * half, half)
            for r, (fx, fy) in enumerate(_CHIP_FLIPS):
                kk = 2 * _flip(x, fx) + _flip(y, fy)
                theirs = outs[t].at[kk, other]
                _remote(theirs, theirs, d2d_send.at[t, r], d2d_recv.at[t, r], sibling).wait_recv()
        for cp in sends:
            cp.wait_send()
        for cp in copies:
            cp.wait()

    return pl.pallas_call(
        body, name="gather_weights",
        in_specs=[VMEM] * n, out_specs=[HBM] * n,
        out_shape=[jax.ShapeDtypeStruct((N_CHIPS,) + s.shape, MM) for s in shards],
        scratch_shapes=[pltpu.VMEM(s.shape, MM) for s in shards] + [
            pltpu.SemaphoreType.DMA((n,)),
            pltpu.SemaphoreType.DMA((n, 3)), pltpu.SemaphoreType.DMA((n, 3)),
            pltpu.SemaphoreType.DMA((n, 3)), pltpu.SemaphoreType.DMA((n, 3))],
        compiler_params=pltpu.CompilerParams(vmem_limit_bytes=40 * 1024 * 1024),
    )(*shards)


def _add_half(grad, got, where, *, name, tr):
    nch, half, cols = got.shape
    nblk = half // tr

    def body(where_ref, g_ref, r_ref, o_ref):
        o_ref[...] = g_ref[...] + r_ref[...]

    return pl.pallas_call(
        body, name=name,
        grid_spec=pltpu.PrefetchScalarGridSpec(
            num_scalar_prefetch=1, grid=(nch, nblk),
            in_specs=[pl.BlockSpec((1, tr, cols), lambda j, i, where_ref: (j, where_ref[1] * nblk + i, 0)),
                      pl.BlockSpec((1, tr, cols), lambda j, i, where_ref: (j, i, 0))],
            out_specs=pl.BlockSpec((1, tr, cols), lambda j, i, where_ref: (j, i, 0))),
        out_shape=jax.ShapeDtypeStruct(got.shape, F32),
        compiler_params=_cparams(("parallel", "parallel"), 32),
    )(where, grad, got)


def _add_chips(own, got, where, *, name, tr):
    _, half, cols = own.shape
    nblk = half // tr

    def body(where_ref, o_ref, g_ref, out_ref):
        out_ref[...] = ((o_ref[0] + g_ref[0]) + g_ref[1]) + g_ref[2]

    return pl.pallas_call(
        body, name=name,
        grid_spec=pltpu.PrefetchScalarGridSpec(
            num_scalar_prefetch=1, grid=(nblk,),
            in_specs=[pl.BlockSpec((1, tr, cols), lambda i, where_ref: (where_ref[0], i, 0)),
                      pl.BlockSpec((3, tr, cols), lambda i, where_ref: (0, i, 0))],
            out_specs=pl.BlockSpec((tr, cols), lambda i, where_ref: (where_ref[1] * nblk + i, 0))),
        out_shape=jax.ShapeDtypeStruct((2 * half, cols), F32),
        compiler_params=_cparams(("parallel",), 32),
    )(where, own, got)


def _small_job(tiles):
    n = len(tiles)

    def copies(ins, outs, sems):
        x, y, c = _place()
        me = 4 * x + 2 * y + c
        local, send, recv = sems
        cps = []
        for t in range(n):
            cps.append(pltpu.make_async_copy(ins[t], outs[t].at[me], local.at[t]))
            for r in range(1, N_DEV):
                fx, fy, fc = (r >> 2) & 1, (r >> 1) & 1, r & 1
                cps.append(_remote(ins[t], outs[t].at[me], send.at[t, r - 1], recv.at[t, r - 1],
                                   (_flip(x, fx), _flip(y, fy), _flip(c, fc))))
        return cps

    return _Job(tiles, [jax.ShapeDtypeStruct((N_DEV,) + t.shape, F32) for t in tiles],
                [pltpu.SemaphoreType.DMA((n,)), pltpu.SemaphoreType.DMA((n, N_DEV - 1)),
                 pltpu.SemaphoreType.DMA((n, N_DEV - 1))], copies)


def _adamw_math(w, g, m, v):
    m = ADAM_B1 * m + (1.0 - ADAM_B1) * g
    v = ADAM_B2 * v + (1.0 - ADAM_B2) * (g * g)
    m_hat = m / (1.0 - ADAM_B1 ** ADAM_STEP)
    v_hat = v / (1.0 - ADAM_B2 ** ADAM_STEP)
    delta = -ADAM_LR * (m_hat / (jnp.sqrt(v_hat) + ADAM_EPS) + ADAM_WD * w)
    return delta, m, v


def _adamw(w, g, m, v, *, name, tr, jobs=()):
    rows, cols = w.shape

    def body(w_ref, g_ref, m_ref, v_ref, d_ref, nm_ref, nv_ref):
        d_ref[...], nm_ref[...], nv_ref[...] = _adamw_math(w_ref[...], g_ref[...], m_ref[...], v_ref[...])

    spec = pl.BlockSpec((tr, cols), lambda i: (i, 0))
    return _call(
        body, (w, g, m, v), name=name, jobs=jobs,
        grid=(rows // tr,),
        in_specs=[spec] * 4, out_specs=[spec] * 3,
        out_shape=[jax.ShapeDtypeStruct(w.shape, F32)] * 3,
        params=_cparams(("arbitrary",), 32))


def _small_adamw(gathered, gathered_rel, params, moments_m, moments_v):
    n = len(params)

    def body(all_ref, rel_all_ref, *refs):
        w_refs, m_refs, v_refs = refs[:n], refs[n:2 * n], refs[2 * n:3 * n]
        loss_ref = refs[3 * n]
        out_refs = refs[3 * n + 1:]
        g = all_ref[0]
        rel = rel_all_ref[0]
        for d in range(1, N_DEV):
            g = g + all_ref[d]
            rel = rel + rel_all_ref[d]
        misc = g[ROW_MISC:ROW_MISC + 1]
        loss_ref[...] = misc[:, MISC_LOSS:MISC_LOSS + 1]
        grads = (g[ROW_G1:ROW_G1 + 1], g[ROW_G2:ROW_G2 + 1], g[ROW_G3:ROW_G3 + 1], g[ROW_G4:ROW_G4 + 1],
                 misc[:, MISC_GQ:MISC_GQ + HEAD_DIM], misc[:, MISC_GK:MISC_GK + HEAD_DIM],
                 misc[:, MISC_SINK:MISC_SINK + GROUP * N_KV], rel[:, 0:GROUP * N_KV])
        for i in range(n):
            d, nm, nv = _adamw_math(w_refs[i][...], grads[i], m_refs[i][...], v_refs[i][...])
            for j, val in enumerate((grads[i], d, nm, nv)):
                out_refs[4 * i + j][...] = val

    outs = pl.pallas_call(
        body, name="small_adamw",
        in_specs=[VMEM] * (2 + 3 * n), out_specs=[VMEM] * (1 + 4 * n),
        out_shape=[jax.ShapeDtypeStruct((1, 1), F32)] + [jax.ShapeDtypeStruct(p.shape, F32) for p in params
                                                          for _ in range(4)],
    )(gathered, gathered_rel, *params, *moments_m, *moments_v)
    return outs[0], [outs[1 + 4 * i:5 + 4 * i] for i in range(n)]


def kernel(x, w_in, w_o, g_pre_mix, g_post_mix, q_norm_a, k_norm_a, sink_b, rel_bias, g_pre_ffn, w_ffn_up, w_ffn_down, g_post_ffn, loss_target, m_w_in, m_w_o, m_g_pre_mix, m_g_post_mix, m_q_norm_a, m_k_norm_a, m_sink_b, m_rel_bias, m_g_pre_ffn, m_w_ffn_up, m_w_ffn_down, m_g_post_ffn, v_w_in, v_w_o, v_g_pre_mix, v_g_post_mix, v_q_norm_a, v_k_norm_a, v_sink_b, v_rel_bias, v_g_pre_ffn, v_w_ffn_up, v_w_ffn_down, v_g_post_ffn):
    batch, seq, _ = x.shape
    tokens = batch * seq
    where = jnp.stack([2 * lax.axis_index("x") + lax.axis_index("y"), lax.axis_index("c")]).astype(jnp.int32)
    x2 = x.reshape(tokens, D_MODEL)
    g1, g2, g3, g4 = g_pre_mix, g_post_mix, g_pre_ffn, g_post_ffn

    cos, sin = _rope_tables(seq)
    ck, sk = jnp.tile(cos, (1, 2)), jnp.tile(sin, (1, 2))
    cq, sq = ck * SCALE, sk * SCALE
    gq8, gk2 = jnp.tile(q_norm_a, (1, 8)), jnp.tile(k_norm_a, (1, 2))
    bucket, band = _window_tables()
    bias = _bias_build(rel_bias.T, bucket, band)

    w_in_g, w_o_p, w_up_p, w_down_p = _gather_weights(
        (w_in[0], w_o[0], w_ffn_up[0], w_ffn_down[0]), whole=(True, False, False, False))
    (h1, raw, qa, ka, kta, va, vta, qtb, kb, ktb, vb, vtb) = _pre_proj(
        x2, g1, w_in_g, gq8, gk2, cq, sq, ck, sk, seq=seq, tm=min(512, seq), sub=256)
    (oa, p_a, linv_a), (w_part,) = _attn_a_fwd(
        qa, kta, va, seq=seq, bq=min(256, seq), jobs=[_gather_job([w_o_p, w_up_p, w_down_p], forward=False)])
    kb3 = kb.reshape(tokens // BLOCK, BLOCK, KV_WIDTH)
    vb3 = vb.reshape(tokens // BLOCK, BLOCK, KV_WIDTH)
    (ob, lse_b), ((w_o_g, w_up_g, w_down_g),) = _attn_b_fwd(
        qtb, kb3, vtb, bias, sink_b, seq=seq, jobs=[_gather_job(w_part, forward=True)])
    w_o2 = w_o_g.reshape(D_MODEL, D_MODEL)
    w_down2 = w_down_g.reshape(D_FF, D_MODEL)
    mix, x1, h2, o_cat = _wo_post(oa, ob, w_o2, x2, g2, g3, tm=512, sub=256)
    u, df, dy, loss_t, dg4 = _ffn_fwd_loss(h2, w_up_g, w_down2, x1, loss_target.reshape(tokens, D_MODEL), g4, tm=256)

    dz, dx1, dmix, dg3, dg2 = _ffn_bwd_act(df, w_down2, u, w_up_g, x1, dy, mix, g3, g2, tm=256)
    gw_down, _ = _tn_matmul(u, df, name="grad_w_down", tm=1024, tn=1024, tk=min(2048, tokens), square_a=True)
    gw_down = gw_down.reshape(N_CHIPS, FF_CHUNK, D_MODEL)
    gw_up, ((got_down,),) = _tn_matmul(h2, dz, name="grad_w_up", tm=1024, tn=1024, tk=min(2048, tokens), chunked=True,
                                        jobs=[_swap_job([gw_down])])
    doa, dob = _wo_bwd(dmix, w_o2, tm=512)
    gw_o, _ = _tn_matmul(o_cat, dmix, name="grad_w_o", tm=1024, tn=1024, tk=min(2048, tokens))
    gw_o = gw_o.reshape(N_CHIPS, O_CHUNK, D_MODEL)
    sum_down = _add_half(gw_down, got_down, where, name="add_half_w_down", tr=128)
    (dqa, dkta, dvta), ((ex_down,), (got_up,)) = _attn_a_bwd(
        qa, ka, vta, doa, oa, p_a, linv_a, seq=seq, bq=min(256, seq),
        jobs=[_exchange_job([sum_down]), _swap_job([gw_up])])
    full_down = _add_chips(sum_down, ex_down, where, name="add_chips_w_down", tr=128)
    sum_up = _add_half(gw_up, got_up, where, name="add_half_w_up", tr=128)
    (dqb, dkb, dvb, dbias, dsink), ((ex_up,), (g_down,), (got_o,)) = _attn_b_bwd(
        qtb, kb3, ktb, vb3, dob, ob, lse_b, bias, sink_b, seq=seq,
        jobs=[_exchange_job([sum_up]), _join_job([full_down]), _swap_job([gw_o])])
    full_up = _add_chips(sum_up, ex_up, where, name="add_chips_w_up", tr=128)
    sum_o = _add_half(gw_o, got_o, where, name="add_half_w_o", tr=128)
    (grad_x, dproj, dg1, dgq, dgk), ((ex_o,), (g_up,)) = _proj_bwd(
        dqa, dkta, dvta, dqb, dkb, dvb, raw, x2, dx1, g1, w_in_g, gq8, gk2, cq, sq, ck, sk,
        seq=seq, tm=min(512, seq), sub=128, jobs=[_exchange_job([sum_o]), _join_job([full_up])])
    full_o = _add_chips(sum_o, ex_o, where, name="add_chips_w_o", tr=128)
    packed, packed_rel = _pack_small(dg1, dg2, dg3, dg4, dgq, dgk, dsink, dbias, bucket, loss_t)
    gw_in, ((g_o,), (gathered, gathered_rel)) = _tn_matmul(
        h1, dproj, name="grad_w_in", tm=1024, tn=IN_CHUNK, tk=min(2048, tokens), chunked=True,
        jobs=[_join_job([full_o]), _small_job([packed, packed_rel])])

    upd_down, ((got_in,),) = _adamw(w_ffn_down[0], g_down, m_w_ffn_down[0], v_w_ffn_down[0], name="adamw_w_down",
                                    tr=128, jobs=[_swap_job([gw_in])])
    sum_in = _add_half(gw_in, got_in, where, name="add_half_w_in", tr=128)
    upd_up, ((ex_in,),) = _adamw(w_ffn_up[0], g_up, m_w_ffn_up[0], v_w_ffn_up[0], name="adamw_w_up", tr=128,
                                 jobs=[_exchange_job([sum_in])])
    full_in = _add_chips(sum_in, ex_in, where, name="add_chips_w_in", tr=128)
    upd_o, ((g_in,),) = _adamw(w_o[0], g_o, m_w_o[0], v_w_o[0], name="adamw_w_o", tr=128,
                               jobs=[_join_job([full_in])])
    upd_in, _ = _adamw(w_in[0], g_in, m_w_in[0], v_w_in[0], name="adamw_w_in", tr=128)
    big = [[t[None] for t in (g, *upd)] for g, upd in
           ((g_in, upd_in), (g_o, upd_o), (g_up, upd_up), (g_down, upd_down))]

    loss, small = _small_adamw(
        gathered, gathered_rel,
        (g1, g2, g3, g4, q_norm_a, k_norm_a, sink_b, rel_bias),
        (m_g_pre_mix, m_g_post_mix, m_g_pre_ffn, m_g_post_ffn, m_q_norm_a, m_k_norm_a, m_sink_b, m_rel_bias),
        (v_g_pre_mix, v_g_post_mix, v_g_pre_ffn, v_g_post_ffn, v_q_norm_a, v_k_norm_a, v_sink_b, v_rel_bias))
    s_g1, s_g2, s_g3, s_g4, s_gq, s_gk, s_sink, s_rel = small

    def leaves(i):
        return (big[0][i], big[1][i], s_g1[i], s_g2[i], s_gq[i], s_gk[i], s_sink[i], s_rel[i], s_g3[i],
                big[2][i], big[3][i], s_g4[i])

    loss = loss.reshape(())
    return (loss, grad_x.reshape(batch, seq, D_MODEL), *leaves(0), *leaves(1), *leaves(2), *leaves(3))
```

```python
import functools

import jax
import jax.numpy as jnp
import numpy as np
from jax import lax
from jax.experimental import pallas as pl
from jax.experimental.pallas import tpu as pltpu

F32 = jnp.float32
MM = jnp.bfloat16

D_MODEL = 1024
HEAD_DIM = 64
N_KV = 2
GROUP = 4
Q_WIDTH = 512
KV_WIDTH = 128
D_FF = 4096
GRID_W = 64
BLOCK = 128
N_BUCKETS = 32
MAX_DISTANCE = 128
ROPE_THETA = 10000.0
EPS = 1e-6
NEG_INF = -1e30
SCALE = HEAD_DIM ** -0.5
IN_TOTAL = 1536
N_CHIPS = 4
N_DEV = 8
IN_CHUNK = IN_TOTAL // N_CHIPS
FF_CHUNK = D_FF // N_CHIPS
O_CHUNK = D_MODEL // N_CHIPS
QK_RAW = 640

ADAM_LR = 0.001
ADAM_B1 = 0.9
ADAM_B2 = 0.999
ADAM_EPS = 1e-08
ADAM_WD = 0.01
ADAM_STEP = 10

LANES = 128
MESH = pl.DeviceIdType.MESH
HBM = pl.BlockSpec(memory_space=pl.ANY)
VMEM = pl.BlockSpec(memory_space=pltpu.VMEM)
SMEM = pl.BlockSpec(memory_space=pltpu.SMEM)

ROW_G1, ROW_G2, ROW_G3, ROW_G4, ROW_MISC = 0, 1, 2, 3, 4
MISC_GQ, MISC_GK, MISC_SINK, MISC_LOSS = 0, 64, 128, 512


def _cparams(sem, vmem_mb):
    return pltpu.CompilerParams(dimension_semantics=sem, vmem_limit_bytes=vmem_mb * 1024 * 1024)


class _Job:
    def __init__(self, operands, out_shapes, sems, copies, alias=None):
        self.operands, self.out_shapes, self.sems, self.copies = list(operands), list(out_shapes), list(sems), copies
        self.alias = dict(alias or {})


def _place():
    return lax.axis_index("x"), lax.axis_index("y"), lax.axis_index("c")


_CHIP_FLIPS = ((1, 0), (0, 1), (1, 1))


def _flip(v, bit):
    return 1 - v if bit else v


def _remote(src, dst, send, recv, dev):
    return pltpu.make_async_remote_copy(src_ref=src, dst_ref=dst, send_sem=send, recv_sem=recv,
                                        device_id=dev, device_id_type=MESH)


def _swap_job(grads):
    n = len(grads)

    def copies(ins, outs, sems):
        x, y, c = _place()
        send, recv = sems
        cps = []
        for t in range(n):
            half = ins[t].shape[1] // 2
            cps.append(_remote(ins[t].at[:, pl.ds((1 - c) * half, half), :], outs[t], send.at[t], recv.at[t],
                               (x, y, 1 - c)))
        return cps

    shapes = [jax.ShapeDtypeStruct((g.shape[0], g.shape[1] // 2, g.shape[2]), F32) for g in grads]
    return _Job(grads, shapes, [pltpu.SemaphoreType.DMA((n,)), pltpu.SemaphoreType.DMA((n,))], copies)


def _exchange_job(sums):
    n = len(sums)

    def copies(ins, outs, sems):
        x, y, c = _place()
        send, recv = sems
        cps = []
        for t in range(n):
            for r, (fx, fy) in enumerate(_CHIP_FLIPS):
                kk = 2 * _flip(x, fx) + _flip(y, fy)
                cps.append(_remote(ins[t].at[kk], outs[t].at[r], send.at[t, r], recv.at[t, r],
                                   (_flip(x, fx), _flip(y, fy), c)))
        return cps

    shapes = [jax.ShapeDtypeStruct((3,) + s.shape[1:], F32) for s in sums]
    return _Job(sums, shapes, [pltpu.SemaphoreType.DMA((n, 3)), pltpu.SemaphoreType.DMA((n, 3))], copies)


def _join_job(fulls):
    n = len(fulls)

    def copies(ins, outs, sems):
        x, y, c = _place()
        send, recv = sems
        cps = []
        for t in range(n):
            half = ins[t].shape[0] // 2
            rows = pl.ds(c * half, half)
            cps.append(_remote(ins[t].at[rows], outs[t].at[rows], send.at[t], recv.at[t], (x, y, 1 - c)))
        return cps

    shapes = [jax.ShapeDtypeStruct(f.shape, f.dtype) for f in fulls]
    return _Job(fulls, shapes, [pltpu.SemaphoreType.DMA((n,)), pltpu.SemaphoreType.DMA((n,))], copies,
                alias={t: t for t in range(n)})


def _gather_job(bufs, forward):
    n = len(bufs)

    def copies(ins, outs, sems):
        x, y, c = _place()
        send, recv = sems
        cps = []
        for t in range(n):
            half = ins[t].shape[1] // 2
            rows = pl.ds(c * half, half)
            for r, (fx, fy) in enumerate(_CHIP_FLIPS):
                if forward:
                    kk = 2 * _flip(x, fx) + _flip(y, fy)
                    dev = (x, y, 1 - c)
                else:
                    kk = 2 * x + y
                    dev = (_flip(x, fx), _flip(y, fy), c)
                cps.append(_remote(ins[t].at[kk, rows], outs[t].at[kk, rows], send.at[t, r], recv.at[t, r], dev))
        return cps

    shapes = [jax.ShapeDtypeStruct(b.shape, b.dtype) for b in bufs]
    return _Job(bufs, shapes, [pltpu.SemaphoreType.DMA((n, 3)), pltpu.SemaphoreType.DMA((n, 3))], copies,
                alias={t: t for t in range(n)})


def _call(body, args, *, name, grid, in_specs, out_specs, out_shape, scratch_shapes=(), params=None, jobs=()):
    n_in, n_out, n_scr = len(in_specs), len(out_specs), len(scratch_shapes)
    job_in = [len(j.operands) for j in jobs]
    job_out = [len(j.out_shapes) for j in jobs]
    job_sem = [len(j.sems) for j in jobs]

    def wrapped(*refs):
        pos = 0
        ins = refs[pos:pos + n_in]; pos += n_in
        jins = []
        for k in job_in:
            jins.append(refs[pos:pos + k]); pos += k
        outs = refs[pos:pos + n_out]; pos += n_out
        jouts = []
        for k in job_out:
            jouts.append(refs[pos:pos + k]); pos += k
        scr = refs[pos:pos + n_scr]; pos += n_scr
        jsems = []
        for k in job_sem:
            jsems.append(refs[pos:pos + k]); pos += k
        if jobs:
            ids = [pl.program_id(d) for d in range(len(grid))]
            first = functools.reduce(jnp.logical_and, [i == 0 for i in ids])
            last = functools.reduce(jnp.logical_and, [i == g - 1 for i, g in zip(ids, grid)])

            @pl.when(first)
            def _():
                for j, ji, jo, js in zip(jobs, jins, jouts, jsems):
                    for cp in j.copies(ji, jo, js):
                        cp.start()

        body(*ins, *outs, *scr)
        if jobs:
            @pl.when(last)
            def _():
                for j, ji, jo, js in zip(jobs, jins, jouts, jsems):
                    for cp in j.copies(ji, jo, js):
                        cp.wait()

    aliases = {}
    in_pos, out_pos = n_in, n_out
    for j in jobs:
        for i, o in j.alias.items():
            aliases[in_pos + i] = out_pos + o
        in_pos += len(j.operands)
        out_pos += len(j.out_shapes)
    res = pl.pallas_call(
        wrapped, name=name, grid=grid,
        in_specs=list(in_specs) + [HBM] * sum(job_in),
        out_specs=list(out_specs) + [HBM] * sum(job_out),
        out_shape=list(out_shape) + [s for j in jobs for s in j.out_shapes],
        scratch_shapes=list(scratch_shapes) + [s for j in jobs for s in j.sems],
        input_output_aliases=aliases,
        compiler_params=params,
    )(*args, *[a for j in jobs for a in j.operands])
    own, rest = list(res[:n_out]), list(res[n_out:])
    job_res = []
    for k in job_out:
        job_res.append(rest[:k])
        rest = rest[k:]
    return own, job_res


def _dot(a, b):
    return jnp.dot(a, b, preferred_element_type=F32)


def _dot_nt(a, b):
    return lax.dot_general(a, b, (((1,), (1,)), ((), ())), preferred_element_type=F32)


def _dot_tn(a, b):
    return lax.dot_general(a, b, (((0,), (0,)), ((), ())), preferred_element_type=F32)


def _rms_r(x):
    return lax.rsqrt(jnp.mean(x * x, axis=-1, keepdims=True) + EPS)


def _rms_bwd(x, r, g, dy):
    n = x * r
    dn = dy * g
    dx = r * (dn - n * jnp.mean(dn * n, axis=-1, keepdims=True))
    return dx, dy * n


def _seg64_sum(v):
    rows, width = v.shape
    lane = lax.broadcasted_iota(jnp.int32, (rows, LANES), 1)
    lo = lane < HEAD_DIM
    outs = []
    for c in range(width // LANES):
        ch = v[:, c * LANES:(c + 1) * LANES]
        s_lo = jnp.sum(jnp.where(lo, ch, 0.0), axis=-1, keepdims=True)
        s_hi = jnp.sum(jnp.where(lo, 0.0, ch), axis=-1, keepdims=True)
        outs.append(jnp.where(lo, s_lo, s_hi))
    return outs[0] if len(outs) == 1 else jnp.concatenate(outs, axis=-1)


def _head_r(v):
    return lax.rsqrt(_seg64_sum(v * v) * (1.0 / HEAD_DIM) + EPS)


def _swap16(ch):
    lane = lax.broadcasted_iota(jnp.int32, ch.shape, 1)
    return jnp.where((lane % 32) < 16, pltpu.roll(ch, LANES - 16, 1), pltpu.roll(ch, 16, 1))


def _by_chunk(fn, v):
    outs = [fn(v[:, c * LANES:(c + 1) * LANES]) for c in range(v.shape[1] // LANES)]
    return outs[0] if len(outs) == 1 else jnp.concatenate(outs, axis=-1)


def _rope(v, cos, sin_signed):
    return _by_chunk(lambda ch: ch * cos + _swap16(ch) * sin_signed, v)


def _rope_t(g, cos, sin_signed):
    return _by_chunk(lambda ch: ch * cos + _swap16(ch * sin_signed), g)


def _rope_tables(seq):
    nf = HEAD_DIM // 4
    freqs = ROPE_THETA ** (-jnp.arange(nf, dtype=F32) / nf)
    pos = jnp.arange(seq, dtype=jnp.int32)
    row = (pos // GRID_W).astype(F32)
    col = (pos % GRID_W).astype(F32)
    ang_r = row[:, None] * freqs[None, :]
    ang_c = col[:, None] * freqs[None, :]
    cr, sr, cc, sc = jnp.cos(ang_r), jnp.sin(ang_r), jnp.cos(ang_c), jnp.sin(ang_c)
    cos = jnp.concatenate([cr, cr, cc, cc], axis=1)
    sin = jnp.concatenate([-sr, sr, -sc, sc], axis=1)
    return cos, sin


def _t5_bucket(rel):
    nb = N_BUCKETS // 2
    ret = (rel > 0).astype(jnp.int32) * nb
    n = jnp.abs(rel)
    max_exact = nb // 2
    nf = jnp.maximum(n, 1).astype(jnp.float32)
    large = max_exact + (jnp.log(nf / max_exact) / np.float32(np.log(MAX_DISTANCE / max_exact))
                         * (nb - max_exact)).astype(jnp.int32)
    large = jnp.minimum(large, nb - 1)
    return ret + jnp.where(n < max_exact, n, large)


def _window_tables():
    a = jnp.arange(BLOCK, dtype=jnp.int32)
    c = jnp.arange(3 * BLOCK, dtype=jnp.int32)
    rel = c[None, :] - BLOCK - a[:, None]
    bucket = _t5_bucket(rel)
    band = (jnp.abs(rel) <= BLOCK).astype(jnp.int32)
    to3 = lambda t: t.reshape(BLOCK, 3, BLOCK).transpose(1, 2, 0)
    return to3(bucket), to3(band)


def _pre_proj(x, g1, w_in, gq, gk, cq, sq, ck, sk, *, seq, tm, sub):
    tokens = x.shape[0]
    n_seq = seq // tm
    nblk = tm // BLOCK
    batch = tokens // seq

    def body(x_ref, g1_ref, w_ref, gq_ref, gk_ref, cq_ref, sq_ref, ck_ref, sk_ref,
             h1_ref, raw_ref, qa_ref, ka_ref, kta_ref, va_ref, vta_ref,
             qtb_ref, kb_ref, ktb_ref, vb_ref, vtb_ref, proj):
        for r in range(tm // sub):
            rows = slice(r * sub, (r + 1) * sub)
            xv = x_ref[rows, :]
            h = (xv * _rms_r(xv) * g1_ref[...]).astype(MM)
            h1_ref[rows, :] = h
            for j in range(N_CHIPS):
                proj[rows, j * IN_CHUNK:(j + 1) * IN_CHUNK] = _dot(h, w_ref[j])
            qa = proj[rows, 0:Q_WIDTH]
            ka = proj[rows, Q_WIDTH:QK_RAW]
            raw_ref[rows, :] = proj[rows, 0:QK_RAW]
            qn = qa * _head_r(qa) * gq_ref[...]
            qa_ref[rows, :] = _rope(qn, cq_ref[rows, :], sq_ref[rows, :]).astype(MM)
            kn = ka * _head_r(ka) * gk_ref[...]
            kr = _rope(kn, ck_ref[rows, :], sk_ref[rows, :])
            ka_ref[rows, :] = kr.astype(MM)
            kta_ref[0, :, rows] = kr.T.astype(MM)
            va = proj[rows, 640:768]
            va_ref[rows, :] = va.astype(MM)
            vta_ref[0, :, rows] = va.T.astype(MM)
            qb = proj[rows, 768:1280] * SCALE
            kb = proj[rows, 1280:1408]
            vb = proj[rows, 1408:1536]
            kb_ref[rows, :] = kb.astype(MM)
            vb_ref[rows, :] = vb.astype(MM)
            for j in range(sub // BLOCK):
                blk = slice(j * BLOCK, (j + 1) * BLOCK)
                qtb_ref[r * (sub // BLOCK) + j] = qb[blk, :].T.astype(MM)
                ktb_ref[r * (sub // BLOCK) + j] = kb[blk, :].T.astype(MM)
                vtb_ref[r * (sub // BLOCK) + j] = vb[blk, :].T.astype(MM)

    tok = lambda w: pl.BlockSpec((tm, w), lambda i: (i, 0))
    tab = lambda w: pl.BlockSpec((tm, w), lambda i: (i % n_seq, 0))
    row = lambda w: pl.BlockSpec((1, w), lambda i: (0, 0))
    tposed = pl.BlockSpec((1, LANES, tm), lambda i: (i // n_seq, 0, i % n_seq))
    blocks = pl.BlockSpec((nblk, BLOCK, LANES), lambda i: (i, 0, 0))
    qblocks = pl.BlockSpec((nblk, Q_WIDTH, BLOCK), lambda i: (i, 0, 0))
    tok_mm = lambda w: jax.ShapeDtypeStruct((tokens, w), MM)
    return pl.pallas_call(
        body, name="pre_proj",
        grid=(tokens // tm,),
        in_specs=[tok(D_MODEL), row(D_MODEL),
                  pl.BlockSpec((N_CHIPS, D_MODEL, IN_CHUNK), lambda i: (0, 0, 0)),
                  row(Q_WIDTH), row(KV_WIDTH), tab(KV_WIDTH), tab(KV_WIDTH), tab(KV_WIDTH), tab(KV_WIDTH)],
        out_specs=[tok(D_MODEL), tok(QK_RAW), tok(Q_WIDTH), tok(KV_WIDTH), tposed, tok(KV_WIDTH), tposed,
                   qblocks, tok(KV_WIDTH), blocks, tok(KV_WIDTH), blocks],
        out_shape=[
            tok_mm(D_MODEL),
            jax.ShapeDtypeStruct((tokens, QK_RAW), F32),
            tok_mm(Q_WIDTH),
            tok_mm(KV_WIDTH),
            jax.ShapeDtypeStruct((batch, KV_WIDTH, seq), MM),
            tok_mm(KV_WIDTH),
            jax.ShapeDtypeStruct((batch, KV_WIDTH, seq), MM),
            jax.ShapeDtypeStruct((tokens // BLOCK, Q_WIDTH, BLOCK), MM),
            tok_mm(KV_WIDTH),
            jax.ShapeDtypeStruct((tokens // BLOCK, KV_WIDTH, BLOCK), MM),
            tok_mm(KV_WIDTH),
            jax.ShapeDtypeStruct((tokens // BLOCK, KV_WIDTH, BLOCK), MM),
        ],
        scratch_shapes=[pltpu.VMEM((tm, IN_TOTAL), F32)],
        compiler_params=_cparams(("parallel",), 48),
    )(x, g1, w_in, gq, gk, cq, sq, ck, sk)


def _kv_half(v2, kv):
    return jnp.where(kv == 0, v2[:, :HEAD_DIM], v2[:, HEAD_DIM:])


def _attn_a_fwd(qa, kta, va, *, seq, bq, jobs=()):
    tokens = qa.shape[0]
    batch = tokens // seq
    nq = seq // bq

    def body(q_ref, kt_ref, v_ref, o_ref, p_ref, linv_ref):
        kv = pl.program_id(1)
        kt = kt_ref[0]
        lane = lax.broadcasted_iota(jnp.int32, (seq, KV_WIDTH), 1)
        v = jnp.where((lane < HEAD_DIM) == (kv == 0), v_ref[...], jnp.ones((), MM))
        for g in range(GROUP):
            sl = slice(g * HEAD_DIM, (g + 1) * HEAD_DIM)
            s = _dot(q_ref[:, sl], kt)
            pb = jnp.exp((s - jnp.max(s, axis=-1, keepdims=True)).astype(MM))
            p_ref[0, g] = pb
            o2 = _dot(pb, v)
            linv = 1.0 / _kv_half(o2, 1 - kv)[:, 0:1]
            o_ref[:, sl] = _kv_half(o2, kv) * linv
            linv_ref[0, :, g:g + 1] = linv

    return _call(
        body, (qa, kta, va), name="attn_a_fwd", jobs=jobs,
        grid=(batch, N_KV, nq),
        in_specs=[pl.BlockSpec((bq, GROUP * HEAD_DIM), lambda b, k, i: (b * nq + i, k)),
                  pl.BlockSpec((1, HEAD_DIM, seq), lambda b, k, i: (b, k, 0)),
                  pl.BlockSpec((seq, KV_WIDTH), lambda b, k, i: (b, 0))],
        out_specs=[pl.BlockSpec((bq, GROUP * HEAD_DIM), lambda b, k, i: (b * nq + i, k)),
                   pl.BlockSpec((1, GROUP, bq, seq), lambda b, k, i: (k, 0, b * nq + i, 0)),
                   pl.BlockSpec((1, bq, GROUP), lambda b, k, i: (k, b * nq + i, 0))],
        out_shape=[jax.ShapeDtypeStruct((tokens, Q_WIDTH), F32),
                   jax.ShapeDtypeStruct((N_KV, GROUP, tokens, seq), MM),
                   jax.ShapeDtypeStruct((N_KV, tokens, GROUP), F32)],
        params=_cparams(("arbitrary", "arbitrary", "arbitrary"), 56))


def _attn_a_bwd(qa, ka, vta, do, o, p, linv, *, seq, bq, jobs=()):
    tokens = qa.shape[0]
    batch = tokens // seq
    nq = seq // bq

    def body(q_ref, k_ref, vt_ref, do_ref, o_ref, p_ref, linv_ref, dq_ref, dkt_ref, dvt_ref):
        kv = pl.program_id(1)

        @pl.when(pl.program_id(2) == 0)
        def _():
            dkt_ref[...] = jnp.zeros_like(dkt_ref)
            dvt_ref[...] = jnp.zeros_like(dvt_ref)

        vt = vt_ref[0]
        k2 = k_ref[...]
        for g in range(GROUP):
            sl = slice(g * HEAD_DIM, (g + 1) * HEAD_DIM)
            dof = do_ref[:, sl]
            delta = jnp.sum(dof * o_ref[:, sl], axis=-1, keepdims=True)
            linv_g = linv_ref[0, :, g:g + 1]
            pb = p_ref[0, g]
            dp = _dot(dof.astype(MM), vt)
            ds = pb * ((dp - delta) * linv_g).astype(MM)
            dq_ref[:, sl] = _kv_half(_dot(ds, k2), kv)
            dkt_ref[0] += _dot_tn(q_ref[:, sl], ds)
            dvt_ref[0] += _dot_tn((dof * linv_g).astype(MM), pb)

    qspec = pl.BlockSpec((bq, GROUP * HEAD_DIM), lambda b, k, i: (b * nq + i, k))
    tspec = pl.BlockSpec((1, HEAD_DIM, seq), lambda b, k, i: (b, k, 0))
    return _call(
        body, (qa, ka, vta, do, o, p, linv), name="attn_a_bwd", jobs=jobs,
        grid=(batch, N_KV, nq),
        in_specs=[qspec, pl.BlockSpec((seq, KV_WIDTH), lambda b, k, i: (b, 0)), tspec, qspec, qspec,
                  pl.BlockSpec((1, GROUP, bq, seq), lambda b, k, i: (k, 0, b * nq + i, 0)),
                  pl.BlockSpec((1, bq, GROUP), lambda b, k, i: (k, b * nq + i, 0))],
        out_specs=[qspec, tspec, tspec],
        out_shape=[jax.ShapeDtypeStruct((tokens, Q_WIDTH), F32),
                   jax.ShapeDtypeStruct((batch, KV_WIDTH, seq), F32),
                   jax.ShapeDtypeStruct((batch, KV_WIDTH, seq), F32)],
        params=_cparams(("arbitrary", "arbitrary", "arbitrary"), 56))


def _bias_build(rel_bias_t, bucket_t, band_t):
    def body(tab_ref, bucket_ref, band_ref, bias_ref):
        for h in range(GROUP * N_KV):
            for piece in range(3):
                bk = bucket_ref[piece]
                acc = jnp.zeros((BLOCK, BLOCK), F32)
                for b in range(N_BUCKETS):
                    acc = jnp.where(bk == b, tab_ref[h, b], acc)
                g = h % GROUP
                bias_ref[h // GROUP, piece, :, g * BLOCK:(g + 1) * BLOCK] = jnp.where(band_ref[piece] != 0, acc, NEG_INF)

    return pl.pallas_call(
        body, name="bias_build",
        in_specs=[SMEM, VMEM, VMEM], out_specs=VMEM,
        out_shape=jax.ShapeDtypeStruct((N_KV, 3, BLOCK, GROUP * BLOCK), F32),
    )(rel_bias_t, bucket_t, band_t)


def _pad_heads(t, kv):
    outs = []
    for g in range(GROUP):
        tg = t[g * HEAD_DIM:(g + 1) * HEAD_DIM, :]
        zero = jnp.zeros_like(tg)
        outs.append(jnp.concatenate([jnp.where(kv == 0, tg, zero), jnp.where(kv == 0, zero, tg)], axis=0))
    return jnp.concatenate(outs, axis=-1)


def _unpad_heads(t, kv):
    outs = [_kv_half(t[:, g * BLOCK:(g + 1) * BLOCK].T, kv) for g in range(GROUP)]
    return jnp.concatenate(outs, axis=-1)


def _sink_row(sink_ref, kv):
    lane_head = lax.broadcasted_iota(jnp.int32, (1, GROUP * BLOCK), 1) // BLOCK
    row = jnp.zeros((1, GROUP * BLOCK), F32)
    for g in range(GROUP):
        row = jnp.where(lane_head == g, sink_ref[0, kv * GROUP + g], row)
    return row


def _window_scores_t(k_ref, idx, qpad, bias_ref, n, nblk):
    pieces = []
    for piece in range(3):
        s = _dot(k_ref[idx[piece]], qpad) + bias_ref[0, piece]
        if piece == 0:
            s = jnp.where(n > 0, s, NEG_INF)
        if piece == 2:
            s = jnp.where(n < nblk - 1, s, NEG_INF)
        pieces.append(s)
    return pieces


def _attn_b_fwd(qtb, kb3, vtb, bias, sink, *, seq, jobs=()):
    nblk_all = qtb.shape[0]
    tokens = nblk_all * BLOCK
    batch = tokens // seq
    nblk = seq // BLOCK

    def body(sink_ref, q_ref, k_ref, vt_ref, bias_ref, o_ref, lse_ref):
        kv = pl.program_id(0)
        sink_row = _sink_row(sink_ref, kv)

        def block(n, carry):
            idx = (jnp.maximum(n - 1, 0), n, jnp.minimum(n + 1, nblk - 1))
            rows = pl.ds(pl.multiple_of(n * BLOCK, BLOCK), BLOCK)
            qpad = _pad_heads(q_ref[n], kv)
            ss = _window_scores_t(k_ref, idx, qpad, bias_ref, n, nblk)
            m = jnp.maximum(jnp.maximum(jnp.max(ss[0], axis=0, keepdims=True),
                                        jnp.max(ss[1], axis=0, keepdims=True)),
                            jnp.maximum(jnp.max(ss[2], axis=0, keepdims=True), sink_row))
            ps = [jnp.exp(s - m) for s in ss]
            denom = (jnp.sum(ps[0], axis=0, keepdims=True) + jnp.sum(ps[1], axis=0, keepdims=True)
                     + jnp.sum(ps[2], axis=0, keepdims=True) + jnp.exp(sink_row - m))
            ot = (_dot(vt_ref[idx[0]], ps[0].astype(MM)) + _dot(vt_ref[idx[1]], ps[1].astype(MM))
                  + _dot(vt_ref[idx[2]], ps[2].astype(MM)))
            o_ref[rows, :] = _unpad_heads(ot * (1.0 / denom), kv)
            lse_ref[0, n] = jnp.broadcast_to(m + jnp.log(denom), (8, GROUP * BLOCK))
            return carry

        lax.fori_loop(0, nblk, block, 0, unroll=4)

    both = pl.BlockSpec((nblk, BLOCK, KV_WIDTH), lambda k, b: (b, 0, 0))
    return _call(
        body, (sink, qtb, kb3, vtb, bias), name="attn_b_fwd", jobs=jobs,
        grid=(N_KV, batch),
        in_specs=[SMEM, pl.BlockSpec((nblk, GROUP * HEAD_DIM, BLOCK), lambda k, b: (b, k, 0)), both, both,
                  pl.BlockSpec((1, 3, BLOCK, GROUP * BLOCK), lambda k, b: (k, 0, 0, 0))],
        out_specs=[pl.BlockSpec((seq, GROUP * HEAD_DIM), lambda k, b: (b, k)),
                   pl.BlockSpec((1, nblk, 8, GROUP * BLOCK), lambda k, b: (k, b, 0, 0))],
        out_shape=[jax.ShapeDtypeStruct((tokens, Q_WIDTH), F32),
                   jax.ShapeDtypeStruct((N_KV, nblk_all, 8, GROUP * BLOCK), F32)],
        params=_cparams(("arbitrary", "arbitrary"), 48))


def _attn_b_bwd(qtb, kb3, ktb, vb3, do, o, lse, bias, sink, *, seq, jobs=()):
    nblk_all = qtb.shape[0]
    tokens = nblk_all * BLOCK
    batch = tokens // seq
    nblk = seq // BLOCK

    def body(sink_ref, q_ref, k_ref, kt_ref, v_ref, do_ref, o_ref, lse_ref, bias_ref,
             dq_ref, dk_ref, dv_ref, dbias_ref, dsink_ref):
        kv = pl.program_id(0)
        sink_row = _sink_row(sink_ref, kv)

        @pl.when(pl.program_id(1) == 0)
        def _():
            dbias_ref[...] = jnp.zeros_like(dbias_ref)
            dsink_ref[...] = jnp.zeros_like(dsink_ref)

        dk_ref[...] = jnp.zeros_like(dk_ref)
        dv_ref[...] = jnp.zeros_like(dv_ref)

        def block(n, dsink):
            idx = (jnp.maximum(n - 1, 0), n, jnp.minimum(n + 1, nblk - 1))
            rows = pl.ds(pl.multiple_of(n * BLOCK, BLOCK), BLOCK)
            qpad = _pad_heads(q_ref[n], kv)
            dot_t = do_ref[rows, :].T
            prod = dot_t * o_ref[rows, :].T
            delta = jnp.concatenate(
                [jnp.sum(prod[g * HEAD_DIM:(g + 1) * HEAD_DIM, :], axis=0, keepdims=True) for g in range(GROUP)],
                axis=-1)
            dopad = _pad_heads(dot_t.astype(MM), kv)
            lse_row = lse_ref[0, n][0:1, :]
            ss = _window_scores_t(k_ref, idx, qpad, bias_ref, n, nblk)
            dqt = jnp.zeros((KV_WIDTH, GROUP * BLOCK), F32)
            for piece in range(3):
                pt = jnp.exp(ss[piece] - lse_row)
                dst = pt * (_dot(v_ref[idx[piece]], dopad) - delta)
                dsb = dst.astype(MM)
                dbias_ref[0, piece] += dst
                dqt = dqt + _dot(kt_ref[idx[piece]], dsb)
                dk_ref[0, idx[piece]] += _dot_nt(dsb, qpad)
                dv_ref[0, idx[piece]] += _dot_nt(pt.astype(MM), dopad)
            dq_ref[rows, :] = _unpad_heads(dqt, kv)
            return dsink - jnp.exp(sink_row - lse_row) * delta

        dsink = lax.fori_loop(
            0, nblk // 4, lambda i, c: block(4 * i + 3, block(4 * i + 2, block(4 * i + 1, block(4 * i, c)))),
            jnp.zeros((1, GROUP * BLOCK), F32))
        dsink_ref[0] += jnp.broadcast_to(dsink, (8, GROUP * BLOCK))

    qspec = pl.BlockSpec((seq, GROUP * HEAD_DIM), lambda k, b: (b, k))
    both = pl.BlockSpec((nblk, BLOCK, KV_WIDTH), lambda k, b: (b, 0, 0))
    grad = pl.BlockSpec((1, nblk, BLOCK, KV_WIDTH), lambda k, b: (k, b, 0, 0))
    return _call(
        body, (sink, qtb, kb3, ktb, vb3, do, o, lse, bias), name="attn_b_bwd", jobs=jobs,
        grid=(N_KV, batch),
        in_specs=[SMEM, pl.BlockSpec((nblk, GROUP * HEAD_DIM, BLOCK), lambda k, b: (b, k, 0)), both, both, both,
                  qspec, qspec, pl.BlockSpec((1, nblk, 8, GROUP * BLOCK), lambda k, b: (k, b, 0, 0)),
                  pl.BlockSpec((1, 3, BLOCK, GROUP * BLOCK), lambda k, b: (k, 0, 0, 0))],
        out_specs=[qspec, grad, grad,
                   pl.BlockSpec((1, 3, BLOCK, GROUP * BLOCK), lambda k, b: (k, 0, 0, 0)),
                   pl.BlockSpec((1, 8, GROUP * BLOCK), lambda k, b: (k, 0, 0))],
        out_shape=[jax.ShapeDtypeStruct((tokens, Q_WIDTH), F32),
                   jax.ShapeDtypeStruct((N_KV, nblk_all, BLOCK, KV_WIDTH), F32),
                   jax.ShapeDtypeStruct((N_KV, nblk_all, BLOCK, KV_WIDTH), F32),
                   jax.ShapeDtypeStruct((N_KV, 3, BLOCK, GROUP * BLOCK), F32),
                   jax.ShapeDtypeStruct((N_KV, 8, GROUP * BLOCK), F32)],
        params=_cparams(("arbitrary", "arbitrary"), 48))


def _wo_post(oa, ob, w_o, x, g2, g3, *, tm, sub):
    tokens = x.shape[0]

    def body(oa_ref, ob_ref, w_ref, x_ref, g2_ref, g3_ref, mix_ref, x1_ref, h2_ref, o_ref):
        for r in range(tm // sub):
            rows = slice(r * sub, (r + 1) * sub)
            o = jnp.concatenate([oa_ref[rows, :].astype(MM), ob_ref[rows, :].astype(MM)], axis=-1)
            o_ref[rows, :] = o
            mix = _dot(o, w_ref[...])
            mix_ref[rows, :] = mix
            x1 = x_ref[rows, :] + mix * _rms_r(mix) * g2_ref[...]
            x1_ref[rows, :] = x1
            h2_ref[rows, :] = (x1 * _rms_r(x1) * g3_ref[...]).astype(MM)

    tok = lambda w: pl.BlockSpec((tm, w), lambda i: (i, 0))
    row = pl.BlockSpec((1, D_MODEL), lambda i: (0, 0))
    return pl.pallas_call(
        body, name="wo_post",
        grid=(tokens // tm,),
        in_specs=[tok(Q_WIDTH), tok(Q_WIDTH), pl.BlockSpec((D_MODEL, D_MODEL), lambda i: (0, 0)),
                  tok(D_MODEL), row, row],
        out_specs=[tok(D_MODEL), tok(D_MODEL), tok(D_MODEL), tok(D_MODEL)],
        out_shape=[jax.ShapeDtypeStruct((tokens, D_MODEL), F32),
                   jax.ShapeDtypeStruct((tokens, D_MODEL), F32),
                   jax.ShapeDtypeStruct((tokens, D_MODEL), MM),
                   jax.ShapeDtypeStruct((tokens, D_MODEL), MM)],
        compiler_params=_cparams(("parallel",), 40),
    )(oa, ob, w_o, x, g2, g3)


def _resident(shape):
    return pl.BlockSpec(shape, lambda i: (0,) * len(shape), pipeline_mode=pl.Buffered(1))


def _ffn_fwd_loss(h2, w_up, w_down, x1, target, g4, *, tm):
    tokens = h2.shape[0]
    nt = tokens // tm

    def body(h2_ref, wu_ref, wd_ref, x1_ref, t_ref, g4_ref, u_ref, df_ref, dy_ref, loss_ref, dg4_ref):
        h2v = h2_ref[...]
        f = jnp.zeros((tm, D_MODEL), F32)
        for c in range(N_CHIPS):
            u = jnp.maximum(_dot(h2v, wu_ref[c]), 0.0)
            u_ref[:, c * FF_CHUNK:(c + 1) * FF_CHUNK] = u.astype(MM)
            f = f + _dot((u * u).astype(MM), wd_ref[c * FF_CHUNK:(c + 1) * FF_CHUNK, :])
        r = _rms_r(f)
        g4v = g4_ref[...]
        err = x1_ref[...] + f * r * g4v - t_ref[...]
        sq = jnp.sum(err * err, axis=-1, keepdims=True)
        loss_ref[0] = jnp.broadcast_to(jnp.sum(sq, axis=0, keepdims=True) * (0.5 / D_MODEL), (8, LANES))
        dy = err * (1.0 / D_MODEL)
        dy_ref[...] = dy
        dfv, dgv = _rms_bwd(f, r, g4v, dy)
        df_ref[...] = dfv.astype(MM)
        dg4_ref[0] = jnp.sum(dgv, axis=0, keepdims=True)

    tok = pl.BlockSpec((tm, D_MODEL), lambda i: (i, 0))
    return pl.pallas_call(
        body, name="ffn_fwd_loss",
        grid=(nt,),
        in_specs=[tok, _resident((N_CHIPS, D_MODEL, FF_CHUNK)), _resident((D_FF, D_MODEL)),
                  tok, tok, pl.BlockSpec((1, D_MODEL), lambda i: (0, 0))],
        out_specs=[pl.BlockSpec((tm, D_FF), lambda i: (i, 0)), tok, tok,
                   pl.BlockSpec((1, 8, LANES), lambda i: (i, 0, 0)),
                   pl.BlockSpec((1, 1, D_MODEL), lambda i: (i, 0, 0))],
        out_shape=[jax.ShapeDtypeStruct((tokens, D_FF), MM),
                   jax.ShapeDtypeStruct((tokens, D_MODEL), MM),
                   jax.ShapeDtypeStruct((tokens, D_MODEL), F32),
                   jax.ShapeDtypeStruct((nt, 8, LANES), F32),
                   jax.ShapeDtypeStruct((nt, 1, D_MODEL), F32)],
        compiler_params=_cparams(("parallel",), 56),
    )(h2, w_up, w_down, x1, target, g4)


def _ffn_bwd_act(df, w_down, u, w_up, x1, dy, mix, g3, g2, *, tm):
    tokens = df.shape[0]
    nt = tokens // tm

    def body(df_ref, wd_ref, u_ref, wu_ref, x1_ref, dy_ref, mix_ref, g3_ref, g2_ref,
             dz_ref, dx1_ref, dmix_ref, dg3_ref, dg2_ref):
        dfv = df_ref[...]
        dh2 = jnp.zeros((tm, D_MODEL), F32)
        for c in range(N_CHIPS):
            cols = slice(c * FF_CHUNK, (c + 1) * FF_CHUNK)
            da = _dot_nt(dfv, wd_ref[cols, :])
            dz = (da * (2.0 * u_ref[:, cols].astype(F32))).astype(MM)
            dz_ref[:, cols] = dz
            dh2 = dh2 + _dot_nt(dz, wu_ref[c])
        x1 = x1_ref[...]
        dxn, dg3v = _rms_bwd(x1, _rms_r(x1), g3_ref[...], dh2)
        dx1 = dy_ref[...] + dxn
        dx1_ref[...] = dx1
        dg3_ref[0] = jnp.sum(dg3v, axis=0, keepdims=True)
        mix = mix_ref[...]
        dmix, dg2v = _rms_bwd(mix, _rms_r(mix), g2_ref[...], dx1)
        dmix_ref[...] = dmix.astype(MM)
        dg2_ref[0] = jnp.sum(dg2v, axis=0, keepdims=True)

    tok = pl.BlockSpec((tm, D_MODEL), lambda i: (i, 0))
    wide = pl.BlockSpec((tm, D_FF), lambda i: (i, 0))
    row = pl.BlockSpec((1, D_MODEL), lambda i: (0, 0))
    part = pl.BlockSpec((1, 1, D_MODEL), lambda i: (i, 0, 0))
    return pl.pallas_call(
        body, name="ffn_bwd_act",
        grid=(nt,),
        in_specs=[tok, _resident((D_FF, D_MODEL)), wide, _resident((N_CHIPS, D_MODEL, FF_CHUNK)),
                  tok, tok, tok, row, row],
        out_specs=[wide, tok, tok, part, part],
        out_shape=[jax.ShapeDtypeStruct((tokens, D_FF), MM),
                   jax.ShapeDtypeStruct((tokens, D_MODEL), F32),
                   jax.ShapeDtypeStruct((tokens, D_MODEL), MM),
                   jax.ShapeDtypeStruct((nt, 1, D_MODEL), F32),
                   jax.ShapeDtypeStruct((nt, 1, D_MODEL), F32)],
        compiler_params=_cparams(("parallel",), 56),
    )(df, w_down, u, w_up, x1, dy, mix, g3, g2)


def _tn_matmul(a, b, *, name, tm, tn, tk, chunked=False, square_a=False, vmem_mb=48, jobs=()):
    tokens, m_dim = a.shape
    n_dim = b.shape[1]
    if chunked:
        assert tm == m_dim

    def body(a_ref, b_ref, o_ref):
        av = a_ref[...]
        if square_a:
            av = av.astype(F32)
            av = av * av
        part = _dot_tn(av.astype(MM), b_ref[...].astype(MM))
        part = part[None] if chunked else part

        @pl.when(pl.program_id(2) == 0)
        def _():
            o_ref[...] = part

        @pl.when(pl.program_id(2) > 0)
        def _():
            o_ref[...] += part

    if chunked:
        out_spec = pl.BlockSpec((1, tm, tn), lambda i, j, k: (j, 0, 0))
        out_shape = jax.ShapeDtypeStruct((n_dim // tn, m_dim, tn), F32)
    else:
        out_spec = pl.BlockSpec((tm, tn), lambda i, j, k: (i, j))
        out_shape = jax.ShapeDtypeStruct((m_dim, n_dim), F32)
    (out,), job_res = _call(
        body, (a, b), name=name, jobs=jobs,
        grid=(m_dim // tm, n_dim // tn, tokens // tk),
        in_specs=[pl.BlockSpec((tk, tm), lambda i, j, k: (k, i)),
                  pl.BlockSpec((tk, tn), lambda i, j, k: (k, j))],
        out_specs=[out_spec], out_shape=[out_shape],
        params=_cparams(("arbitrary", "arbitrary", "arbitrary"), vmem_mb))
    return out, job_res


def _wo_bwd(dmix, w_o, *, tm):
    tokens = dmix.shape[0]

    def body(dm_ref, w_ref, doa_ref, dob_ref):
        dm = dm_ref[...]
        doa_ref[...] = _dot_nt(dm, w_ref[0:Q_WIDTH, :])
        dob_ref[...] = _dot_nt(dm, w_ref[Q_WIDTH:D_MODEL, :])

    tok = lambda w: pl.BlockSpec((tm, w), lambda i: (i, 0))
    return pl.pallas_call(
        body, name="wo_bwd",
        grid=(tokens // tm,),
        in_specs=[tok(D_MODEL), pl.BlockSpec((D_MODEL, D_MODEL), lambda i: (0, 0))],
        out_specs=[tok(Q_WIDTH), tok(Q_WIDTH)],
        out_shape=[jax.ShapeDtypeStruct((tokens, Q_WIDTH), F32)] * 2,
        compiler_params=_cparams(("parallel",), 40),
    )(dmix, w_o)


def _proj_bwd(dqa, dkta, dvta, dqb, dktb, dvtb, raw, x, dx1, g1, w_in, gq, gk, cq, sq, ck, sk, *, seq, tm, sub,
              jobs=()):
    tokens = x.shape[0]
    nt = tokens // tm
    n_seq = seq // tm
    nblk = tm // BLOCK

    def body(dqa_ref, dkta_ref, dvta_ref, dqb_ref, dkb_ref, dvb_ref, raw_ref, x_ref, dx1_ref, g1_ref, w_ref,
             gq_ref, gk_ref, cq_ref, sq_ref, ck_ref, sk_ref,
             gx_ref, dproj_ref, dg1_ref, dgq_ref, dgk_ref, dp):
        parts = []
        for r in range(tm // sub):
            rows = slice(r * sub, (r + 1) * sub)
            qa = raw_ref[rows, 0:Q_WIDTH]
            dqn = _rope_t(dqa_ref[rows, :], cq_ref[rows, :], sq_ref[rows, :])
            rq = _head_r(qa)
            nq = qa * rq
            dnq = dqn * gq_ref[...]
            dp[rows, 0:Q_WIDTH] = rq * (dnq - nq * (_seg64_sum(dnq * nq) * (1.0 / HEAD_DIM)))

            ka = raw_ref[rows, Q_WIDTH:QK_RAW]
            dkn = _rope_t(dkta_ref[0, :, rows].T, ck_ref[rows, :], sk_ref[rows, :])
            rk = _head_r(ka)
            nk = ka * rk
            dnk = dkn * gk_ref[...]
            dp[rows, 512:640] = rk * (dnk - nk * (_seg64_sum(dnk * nk) * (1.0 / HEAD_DIM)))

            dp[rows, 640:768] = dvta_ref[0, :, rows].T
            dp[rows, 768:1280] = dqb_ref[rows, :] * SCALE
            for j in range(r * sub // BLOCK, (r + 1) * sub // BLOCK):
                dp[j * BLOCK:(j + 1) * BLOCK, 1280:1408] = dkb_ref[0, j] + dkb_ref[1, j]
                dp[j * BLOCK:(j + 1) * BLOCK, 1408:1536] = dvb_ref[0, j] + dvb_ref[1, j]

            dproj = dp[rows, :].astype(MM)
            dproj_ref[rows, :] = dproj
            dh1 = _dot_nt(dproj[:, 0:IN_CHUNK], w_ref[0])
            for j in range(1, N_CHIPS):
                dh1 = dh1 + _dot_nt(dproj[:, j * IN_CHUNK:(j + 1) * IN_CHUNK], w_ref[j])
            xv = x_ref[rows, :]
            dxn, dg1v = _rms_bwd(xv, _rms_r(xv), g1_ref[...], dh1)
            gx_ref[rows, :] = dx1_ref[rows, :] + dxn
            parts.append((jnp.sum(dqn * nq, axis=0, keepdims=True), jnp.sum(dkn * nk, axis=0, keepdims=True),
                          jnp.sum(dg1v, axis=0, keepdims=True)))
        dgq_ref[0] = functools.reduce(jnp.add, [p[0] for p in parts])
        dgk_ref[0] = functools.reduce(jnp.add, [p[1] for p in parts])
        dg1_ref[0] = functools.reduce(jnp.add, [p[2] for p in parts])

    tok = lambda w: pl.BlockSpec((tm, w), lambda i: (i, 0))
    tab = lambda w: pl.BlockSpec((tm, w), lambda i: (i % n_seq, 0))
    row = lambda w: pl.BlockSpec((1, w), lambda i: (0, 0))
    tposed = pl.BlockSpec((1, KV_WIDTH, tm), lambda i: (i // n_seq, 0, i % n_seq))
    blocks = pl.BlockSpec((N_KV, nblk, BLOCK, KV_WIDTH), lambda i: (0, i, 0, 0))
    part = lambda w: pl.BlockSpec((1, 1, w), lambda i: (i, 0, 0))
    return _call(
        body, (dqa, dkta, dvta, dqb, dktb, dvtb, raw, x, dx1, g1, w_in, gq, gk, cq, sq, ck, sk),
        name="proj_bwd", jobs=jobs,
        grid=(nt,),
        in_specs=[tok(Q_WIDTH), tposed, tposed, tok(Q_WIDTH), blocks, blocks, tok(QK_RAW), tok(D_MODEL),
                  tok(D_MODEL), row(D_MODEL),
                  pl.BlockSpec((N_CHIPS, D_MODEL, IN_CHUNK), lambda i: (0, 0, 0)),
                  row(Q_WIDTH), row(KV_WIDTH), tab(KV_WIDTH), tab(KV_WIDTH), tab(KV_WIDTH), tab(KV_WIDTH)],
        out_specs=[tok(D_MODEL), tok(IN_TOTAL), part(D_MODEL), part(Q_WIDTH), part(KV_WIDTH)],
        out_shape=[jax.ShapeDtypeStruct((tokens, D_MODEL), F32),
                   jax.ShapeDtypeStruct((tokens, IN_TOTAL), MM),
                   jax.ShapeDtypeStruct((nt, 1, D_MODEL), F32),
                   jax.ShapeDtypeStruct((nt, 1, Q_WIDTH), F32),
                   jax.ShapeDtypeStruct((nt, 1, KV_WIDTH), F32)],
        scratch_shapes=[pltpu.VMEM((tm, IN_TOTAL), F32)],
        params=_cparams(("arbitrary",), 56))


def _pack_small(dg1, dg2, dg3, dg4, dgq, dgk, dsink, dbias, bucket, loss):
    def body(dg1_ref, dg2_ref, dg3_ref, dg4_ref, dgq_ref, dgk_ref, dsink_ref, dbias_ref, bucket_ref, loss_ref,
             out_ref, rel_ref):
        out_ref[...] = jnp.zeros_like(out_ref)
        for r, ref in ((ROW_G1, dg1_ref), (ROW_G2, dg2_ref), (ROW_G3, dg3_ref), (ROW_G4, dg4_ref)):
            acc = ref[0]
            for t in range(1, ref.shape[0]):
                acc = acc + ref[t]
            out_ref[r:r + 1, :] = acc

        def fold(ref, heads):
            acc = ref[0]
            for t in range(1, ref.shape[0]):
                acc = acc + ref[t]
            tot = acc[:, 0:HEAD_DIM]
            for h in range(1, heads):
                tot = tot + acc[:, h * HEAD_DIM:(h + 1) * HEAD_DIM]
            return tot

        out_ref[ROW_MISC:ROW_MISC + 1, MISC_GQ:MISC_GQ + HEAD_DIM] = fold(dgq_ref, GROUP * N_KV)
        out_ref[ROW_MISC:ROW_MISC + 1, MISC_GK:MISC_GK + HEAD_DIM] = fold(dgk_ref, N_KV)
        for h in range(GROUP * N_KV):
            g = h % GROUP
            out_ref[ROW_MISC:ROW_MISC + 1, MISC_SINK + h:MISC_SINK + h + 1] = jnp.sum(
                dsink_ref[h // GROUP, 0:1, g * BLOCK:(g + 1) * BLOCK], axis=-1, keepdims=True)
        lacc = loss_ref[0, 0:1, 0:1]
        for t in range(1, loss_ref.shape[0]):
            lacc = lacc + loss_ref[t, 0:1, 0:1]
        out_ref[ROW_MISC:ROW_MISC + 1, MISC_LOSS:MISC_LOSS + 1] = lacc
        lane = lax.broadcasted_iota(jnp.int32, (N_BUCKETS, LANES), 1)
        row = lax.broadcasted_iota(jnp.int32, (N_BUCKETS, LANES), 0)

        def per_bucket(b, acc):
            for h in range(GROUP * N_KV):
                g = h % GROUP
                sel = jnp.zeros((BLOCK, BLOCK), F32)
                for piece in range(3):
                    sel = sel + jnp.where(bucket_ref[piece] == b,
                                          dbias_ref[h // GROUP, piece, :, g * BLOCK:(g + 1) * BLOCK], 0.0)
                tot = jnp.sum(jnp.sum(sel, axis=0, keepdims=True), axis=-1, keepdims=True)
                acc = jnp.where((row == b) & (lane == h), tot, acc)
            return acc

        rel_ref[...] = lax.fori_loop(0, N_BUCKETS, per_bucket, jnp.zeros((N_BUCKETS, LANES), F32))

    return pl.pallas_call(
        body, name="pack_small",
        in_specs=[VMEM] * 10, out_specs=[VMEM, VMEM],
        out_shape=[jax.ShapeDtypeStruct((8, D_MODEL), F32), jax.ShapeDtypeStruct((N_BUCKETS, LANES), F32)],
        compiler_params=pltpu.CompilerParams(vmem_limit_bytes=32 * 1024 * 1024),
    )(dg1, dg2, dg3, dg4, dgq, dgk, dsink, dbias, bucket, loss)


def _gather_weights(shards, whole):
    n = len(shards)
    full = [t for t in range(n) if whole[t]]

    def body(*refs):
        ins, outs = refs[:n], refs[n:2 * n]
        stage = refs[2 * n:3 * n]
        local_sem, ici_send, ici_recv, d2d_send, d2d_recv = refs[3 * n:]
        x, y, c = _place()
        k = 2 * x + y
        sibling = (x, y, 1 - c)
        copies = []
        for t in range(n):
            stage[t][...] = ins[t][...].astype(MM)
            mine = pltpu.make_async_copy(stage[t], outs[t].at[k], local_sem.at[t])
            mine.start()
            copies.append(mine)
        sends = []
        for t in full:
            half = ins[t].shape[0] // 2
            rows = pl.ds(c * half, half)
            for r, (fx, fy) in enumerate(_CHIP_FLIPS):
                cp = _remote(stage[t].at[rows], outs[t].at[k, rows], ici_send.at[t, r], ici_recv.at[t, r],
                             (_flip(x, fx), _flip(y, fy), c))
                cp.start()
                sends.append(cp)
        for t in full:
            half = ins[t].shape[0] // 2
            rows = pl.ds(c * half, half)
            for r, (fx, fy) in enumerate(_CHIP_FLIPS):
                kk = 2 * _flip(x, fx) + _flip(y, fy)
                landed = outs[t].at[kk, rows]
                _remote(landed, landed, ici_send.at[t, r], ici_recv.at[t, r], sibling).wait_recv()
                fwd = _remote(landed, landed, d2d_send.at[t, r], d2d_recv.at[t, r], sibling)
                fwd.start()
                sends.append(fwd)
        for t in full:
            half = ins[t].shape[0] // 2
            other = pl.ds((1 - c) * half, half)
            for r, (fx, fy) in enumerate(_CHIP_FLIPS):
                kk = 2 * _flip(x, fx) + _flip(y, fy)
                theirs = outs[t].at[kk, other]
                _remote(theirs, theirs, d2d_send.at[t, r], d2d_recv.at[t, r], sibling).wait_recv()
        for cp in sends:
            cp.wait_send()
        for cp in copies:
            cp.wait()

    return pl.pallas_call(
        body, name="gather_weights",
        in_specs=[VMEM] * n, out_specs=[HBM] * n,
        out_shape=[jax.ShapeDtypeStruct((N_CHIPS,) + s.shape, MM) for s in shards],
        scratch_shapes=[pltpu.VMEM(s.shape, MM) for s in shards] + [
            pltpu.SemaphoreType.DMA((n,)),
            pltpu.SemaphoreType.DMA((n, 3)), pltpu.SemaphoreType.DMA((n, 3)),
            pltpu.SemaphoreType.DMA((n, 3)), pltpu.SemaphoreType.DMA((n, 3))],
        compiler_params=pltpu.CompilerParams(vmem_limit_bytes=40 * 1024 * 1024),
    )(*shards)


def _add_half(grad, got, where, *, name, tr):
    nch, half, cols = got.shape
    nblk = half // tr

    def body(where_ref, g_ref, r_ref, o_ref):
        o_ref[...] = g_ref[...] + r_ref[...]

    return pl.pallas_call(
        body, name=name,
        grid_spec=pltpu.PrefetchScalarGridSpec(
            num_scalar_prefetch=1, grid=(nch, nblk),
            in_specs=[pl.BlockSpec((1, tr, cols), lambda j, i, where_ref: (j, where_ref[1] * nblk + i, 0)),
                      pl.BlockSpec((1, tr, cols), lambda j, i, where_ref: (j, i, 0))],
            out_specs=pl.BlockSpec((1, tr, cols), lambda j, i, where_ref: (j, i, 0))),
        out_shape=jax.ShapeDtypeStruct(got.shape, F32),
        compiler_params=_cparams(("parallel", "parallel"), 32),
    )(where, grad, got)


def _add_chips(own, got, where, *, name, tr):
    _, half, cols = own.shape
    nblk = half // tr

    def body(where_ref, o_ref, g_ref, out_ref):
        out_ref[...] = ((o_ref[0] + g_ref[0]) + g_ref[1]) + g_ref[2]

    return pl.pallas_call(
        body, name=name,
        grid_spec=pltpu.PrefetchScalarGridSpec(
            num_scalar_prefetch=1, grid=(nblk,),
            in_specs=[pl.BlockSpec((1, tr, cols), lambda i, where_ref: (where_ref[0], i, 0)),
                      pl.BlockSpec((3, tr, cols), lambda i, where_ref: (0, i, 0))],
            out_specs=pl.BlockSpec((tr, cols), lambda i, where_ref: (where_ref[1] * nblk + i, 0))),
        out_shape=jax.ShapeDtypeStruct((2 * half, cols), F32),
        compiler_params=_cparams(("parallel",), 32),
    )(where, own, got)


def _small_job(tiles):
    n = len(tiles)

    def copies(ins, outs, sems):
        x, y, c = _place()
        me = 4 * x + 2 * y + c
        local, send, recv = sems
        cps = []
        for t in range(n):
            cps.append(pltpu.make_async_copy(ins[t], outs[t].at[me], local.at[t]))
            for r in range(1, N_DEV):
                fx, fy, fc = (r >> 2) & 1, (r >> 1) & 1, r & 1
                cps.append(_remote(ins[t], outs[t].at[me], send.at[t, r - 1], recv.at[t, r - 1],
                                   (_flip(x, fx), _flip(y, fy), _flip(c, fc))))
        return cps

    return _Job(tiles, [jax.ShapeDtypeStruct((N_DEV,) + t.shape, F32) for t in tiles],
                [pltpu.SemaphoreType.DMA((n,)), pltpu.SemaphoreType.DMA((n, N_DEV - 1)),
                 pltpu.SemaphoreType.DMA((n, N_DEV - 1))], copies)


def _adamw_math(w, g, m, v):
    m = ADAM_B1 * m + (1.0 - ADAM_B1) * g
    v = ADAM_B2 * v + (1.0 - ADAM_B2) * (g * g)
    m_hat = m / (1.0 - ADAM_B1 ** ADAM_STEP)
    v_hat = v / (1.0 - ADAM_B2 ** ADAM_STEP)
    delta = -ADAM_LR * (m_hat / (jnp.sqrt(v_hat) + ADAM_EPS) + ADAM_WD * w)
    return delta, m, v


def _adamw(w, g, m, v, *, name, tr, jobs=()):
    rows, cols = w.shape

    def body(w_ref, g_ref, m_ref, v_ref, d_ref, nm_ref, nv_ref):
        d_ref[...], nm_ref[...], nv_ref[...] = _adamw_math(w_ref[...], g_ref[...], m_ref[...], v_ref[...])

    spec = pl.BlockSpec((tr, cols), lambda i: (i, 0))
    return _call(
        body, (w, g, m, v), name=name, jobs=jobs,
        grid=(rows // tr,),
        in_specs=[spec] * 4, out_specs=[spec] * 3,
        out_shape=[jax.ShapeDtypeStruct(w.shape, F32)] * 3,
        params=_cparams(("arbitrary",), 32))


def _small_adamw(gathered, gathered_rel, params, moments_m, moments_v):
    n = len(params)

    def body(all_ref, rel_all_ref, *refs):
        w_refs, m_refs, v_refs = refs[:n], refs[n:2 * n], refs[2 * n:3 * n]
        loss_ref = refs[3 * n]
        out_refs = refs[3 * n + 1:]
        g = all_ref[0]
        rel = rel_all_ref[0]
        for d in range(1, N_DEV):
            g = g + all_ref[d]
            rel = rel + rel_all_ref[d]
        misc = g[ROW_MISC:ROW_MISC + 1]
        loss_ref[...] = misc[:, MISC_LOSS:MISC_LOSS + 1]
        grads = (g[ROW_G1:ROW_G1 + 1], g[ROW_G2:ROW_G2 + 1], g[ROW_G3:ROW_G3 + 1], g[ROW_G4:ROW_G4 + 1],
                 misc[:, MISC_GQ:MISC_GQ + HEAD_DIM], misc[:, MISC_GK:MISC_GK + HEAD_DIM],
                 misc[:, MISC_SINK:MISC_SINK + GROUP * N_KV], rel[:, 0:GROUP * N_KV])
        for i in range(n):
            d, nm, nv = _adamw_math(w_refs[i][...], grads[i], m_refs[i][...], v_refs[i][...])
            for j, val in enumerate((grads[i], d, nm, nv)):
                out_refs[4 * i + j][...] = val

    outs = pl.pallas_call(
        body, name="small_adamw",
        in_specs=[VMEM] * (2 + 3 * n), out_specs=[VMEM] * (1 + 4 * n),
        out_shape=[jax.ShapeDtypeStruct((1, 1), F32)] + [jax.ShapeDtypeStruct(p.shape, F32) for p in params
                                                          for _ in range(4)],
    )(gathered, gathered_rel, *params, *moments_m, *moments_v)
    return outs[0], [outs[1 + 4 * i:5 + 4 * i] for i in range(n)]


def kernel(x, w_in, w_o, g_pre_mix, g_post_mix, q_norm_a, k_norm_a, sink_b, rel_bias, g_pre_ffn, w_ffn_up, w_ffn_down, g_post_ffn, loss_target, m_w_in, m_w_o, m_g_pre_mix, m_g_post_mix, m_q_norm_a, m_k_norm_a, m_sink_b, m_rel_bias, m_g_pre_ffn, m_w_ffn_up, m_w_ffn_down, m_g_post_ffn, v_w_in, v_w_o, v_g_pre_mix, v_g_post_mix, v_q_norm_a, v_k_norm_a, v_sink_b, v_rel_bias, v_g_pre_ffn, v_w_ffn_up, v_w_ffn_down, v_g_post_ffn):
    batch, seq, _ = x.shape
    tokens = batch * seq
    where = jnp.stack([2 * lax.axis_index("x") + lax.axis_index("y"), lax.axis_index("c")]).astype(jnp.int32)
    x2 = x.reshape(tokens, D_MODEL)
    g1, g2, g3, g4 = g_pre_mix, g_post_mix, g_pre_ffn, g_post_ffn

    cos, sin = _rope_tables(seq)
    ck, sk = jnp.tile(cos, (1, 2)), jnp.tile(sin, (1, 2))
    cq, sq = ck * SCALE, sk * SCALE
    gq8, gk2 = jnp.tile(q_norm_a, (1, 8)), jnp.tile(k_norm_a, (1, 2))
    bucket, band = _window_tables()
    bias = _bias_build(rel_bias.T, bucket, band)

    w_in_g, w_o_p, w_up_p, w_down_p = _gather_weights(
        (w_in[0], w_o[0], w_ffn_up[0], w_ffn_down[0]), whole=(True, False, False, False))
    (h1, raw, qa, ka, kta, va, vta, qtb, kb, ktb, vb, vtb) = _pre_proj(
        x2, g1, w_in_g, gq8, gk2, cq, sq, ck, sk, seq=seq, tm=min(512, seq), sub=256)
    (oa, p_a, linv_a), (w_part,) = _attn_a_fwd(
        qa, kta, va, seq=seq, bq=min(256, seq), jobs=[_gather_job([w_o_p, w_up_p, w_down_p], forward=False)])
    kb3 = kb.reshape(tokens // BLOCK, BLOCK, KV_WIDTH)
    vb3 = vb.reshape(tokens // BLOCK, BLOCK, KV_WIDTH)
    (ob, lse_b), ((w_o_g, w_up_g, w_down_g),) = _attn_b_fwd(
        qtb, kb3, vtb, bias, sink_b, seq=seq, jobs=[_gather_job(w_part, forward=True)])
    w_o2 = w_o_g.reshape(D_MODEL, D_MODEL)
    w_down2 = w_down_g.reshape(D_FF, D_MODEL)
    mix, x1, h2, o_cat = _wo_post(oa, ob, w_o2, x2, g2, g3, tm=512, sub=256)
    u, df, dy, loss_t, dg4 = _ffn_fwd_loss(h2, w_up_g, w_down2, x1, loss_target.reshape(tokens, D_MODEL), g4, tm=256)

    dz, dx1, dmix, dg3, dg2 = _ffn_bwd_act(df, w_down2, u, w_up_g, x1, dy, mix, g3, g2, tm=256)
    gw_down, _ = _tn_matmul(u, df, name="grad_w_down", tm=1024, tn=1024, tk=min(2048, tokens), square_a=True)
    gw_down = gw_down.reshape(N_CHIPS, FF_CHUNK, D_MODEL)
    gw_up, ((got_down,),) = _tn_matmul(h2, dz, name="grad_w_up", tm=1024, tn=1024, tk=min(2048, tokens), chunked=True,
                                        jobs=[_swap_job([gw_down])])
    doa, dob = _wo_bwd(dmix, w_o2, tm=512)
    gw_o, _ = _tn_matmul(o_cat, dmix, name="grad_w_o", tm=1024, tn=1024, tk=min(2048, tokens))
    gw_o = gw_o.reshape(N_CHIPS, O_CHUNK, D_MODEL)
    sum_down = _add_half(gw_down, got_down, where, name="add_half_w_down", tr=128)
    (dqa, dkta, dvta), ((ex_down,), (got_up,)) = _attn_a_bwd(
        qa, ka, vta, doa, oa, p_a, linv_a, seq=seq, bq=min(256, seq),
        jobs=[_exchange_job([sum_down]), _swap_job([gw_up])])
    full_down = _add_chips(sum_down, ex_down, where, name="add_chips_w_down", tr=128)
    sum_up = _add_half(gw_up, got_up, where, name="add_half_w_up", tr=128)
    (dqb, dkb, dvb, dbias, dsink), ((ex_up,), (g_down,), (got_o,)) = _attn_b_bwd(
        qtb, kb3, ktb, vb3, dob, ob, lse_b, bias, sink_b, seq=seq,
        jobs=[_exchange_job([sum_up]), _join_job([full_down]), _swap_job([gw_o])])
    full_up = _add_chips(sum_up, ex_up, where, name="add_chips_w_up", tr=128)
    sum_o = _add_half(gw_o, got_o, where, name="add_half_w_o", tr=128)
    (grad_x, dproj, dg1, dgq, dgk), ((ex_o,), (g_up,)) = _proj_bwd(
        dqa, dkta, dvta, dqb, dkb, dvb, raw, x2, dx1, g1, w_in_g, gq8, gk2, cq, sq, ck, sk,
        seq=seq, tm=min(512, seq), sub=128, jobs=[_exchange_job([sum_o]), _join_job([full_up])])
    full_o = _add_chips(sum_o, ex_o, where, name="add_chips_w_o", tr=128)
    packed, packed_rel = _pack_small(dg1, dg2, dg3, dg4, dgq, dgk, dsink, dbias, bucket, loss_t)
    gw_in, ((g_o,), (gathered, gathered_rel)) = _tn_matmul(
        h1, dproj, name="grad_w_in", tm=1024, tn=IN_CHUNK, tk=min(2048, tokens), chunked=True,
        jobs=[_join_job([full_o]), _small_job([packed, packed_rel])])

    upd_down, ((got_in,),) = _adamw(w_ffn_down[0], g_down, m_w_ffn_down[0], v_w_ffn_down[0], name="adamw_w_down",
                                    tr=128, jobs=[_swap_job([gw_in])])
    sum_in = _add_half(gw_in, got_in, where, name="add_half_w_in", tr=128)
    upd_up, ((ex_in,),) = _adamw(w_ffn_up[0], g_up, m_w_ffn_up[0], v_w_ffn_up[0], name="adamw_w_up", tr=128,
                                 jobs=[_exchange_job([sum_in])])
    full_in = _add_chips(sum_in, ex_in, where, name="add_chips_w_in", tr=128)
    upd_o, ((g_in,),) = _adamw(w_o[0], g_o, m_w_o[0], v_w_o[0], name="adamw_w_o", tr=128,
                               jobs=[_join_job([full_in])])
    upd_in, _ = _adamw(w_in[0], g_in, m_w_in[0], v_w_in[0], name="adamw_w_in", tr=128)
    big = [[t[None] for t in (g, *upd)] for g, upd in
           ((g_in, upd_in), (g_o, upd_o), (g_up, upd_up), (g_down, upd_down))]

    loss, small = _small_adamw(
        gathered, gathered_rel,
        (g1, g2, g3, g4, q_norm_a, k_norm_a, sink_b, rel_bias),
        (m_g_pre_mix, m_g_post_mix, m_g_pre_ffn, m_g_post_ffn, m_q_norm_a, m_k_norm_a, m_sink_b, m_rel_bias),
        (v_g_pre_mix, v_g_post_mix, v_g_pre_ffn, v_g_post_ffn, v_q_norm_a, v_k_norm_a, v_sink_b, v_rel_bias))
    s_g1, s_g2, s_g3, s_g4, s_gq, s_gk, s_sink, s_rel = small

    def leaves(i):
        return (big[0][i], big[1][i], s_g1[i], s_g2[i], s_gq[i], s_gk[i], s_sink[i], s_rel[i], s_g3[i],
                big[2][i], big[3][i], s_g4[i])

    loss = loss.reshape(())
    return (loss, grad_x.reshape(batch, seq, D_MODEL), *leaves(0), *leaves(1), *leaves(2), *leaves(3))
```

```python
import functools

import jax
import jax.numpy as jnp
import numpy as np
from jax import lax
from jax.experimental import pallas as pl
from jax.experimental.pallas import tpu as pltpu

F32 = jnp.float32
MM = jnp.bfloat16

D_MODEL = 1024
HEAD_DIM = 64
N_KV = 2
GROUP = 4
Q_WIDTH = 512
KV_WIDTH = 128
D_FF = 4096
GRID_W = 64
BLOCK = 128
N_BUCKETS = 32
MAX_DISTANCE = 128
ROPE_THETA = 10000.0
EPS = 1e-6
NEG_INF = -1e30
SCALE = HEAD_DIM ** -0.5
IN_TOTAL = 1536
N_CHIPS = 4
N_DEV = 8
IN_CHUNK = IN_TOTAL // N_CHIPS
FF_CHUNK = D_FF // N_CHIPS
O_CHUNK = D_MODEL // N_CHIPS
QK_RAW = 640

ADAM_LR = 0.001
ADAM_B1 = 0.9
ADAM_B2 = 0.999
ADAM_EPS = 1e-08
ADAM_WD = 0.01
ADAM_STEP = 10

LANES = 128
MESH = pl.DeviceIdType.MESH
HBM = pl.BlockSpec(memory_space=pl.ANY)
VMEM = pl.BlockSpec(memory_space=pltpu.VMEM)
SMEM = pl.BlockSpec(memory_space=pltpu.SMEM)

ROW_G1, ROW_G2, ROW_G3, ROW_G4, ROW_MISC = 0, 1, 2, 3, 4
MISC_GQ, MISC_GK, MISC_SINK, MISC_LOSS = 0, 64, 128, 512


def _cparams(sem, vmem_mb):
    return pltpu.CompilerParams(dimension_semantics=sem, vmem_limit_bytes=vmem_mb * 1024 * 1024)


def _in_hbm(s):
    return pltpu.HBM(s.shape, s.dtype)


class _Job:
    def __init__(self, operands, out_shapes, sems, copies, alias=None):
        self.operands, self.out_shapes, self.sems, self.copies = list(operands), list(out_shapes), list(sems), copies
        self.alias = dict(alias or {})


def _place():
    return lax.axis_index("x"), lax.axis_index("y"), lax.axis_index("c")


_CHIP_FLIPS = ((1, 0), (0, 1), (1, 1))


def _flip(v, bit):
    return 1 - v if bit else v


def _remote(src, dst, send, recv, dev):
    return pltpu.make_async_remote_copy(src_ref=src, dst_ref=dst, send_sem=send, recv_sem=recv,
                                        device_id=dev, device_id_type=MESH)


def _swap_job(grads):
    n = len(grads)

    def copies(ins, outs, sems):
        x, y, c = _place()
        send, recv = sems
        cps = []
        for t in range(n):
            half = ins[t].shape[1] // 2
            cps.append(_remote(ins[t].at[:, pl.ds((1 - c) * half, half), :], outs[t], send.at[t], recv.at[t],
                               (x, y, 1 - c)))
        return cps

    shapes = [jax.ShapeDtypeStruct((g.shape[0], g.shape[1] // 2, g.shape[2]), F32) for g in grads]
    return _Job(grads, shapes, [pltpu.SemaphoreType.DMA((n,)), pltpu.SemaphoreType.DMA((n,))], copies)


def _exchange_job(sums):
    n = len(sums)

    def copies(ins, outs, sems):
        x, y, c = _place()
        send, recv = sems
        cps = []
        for t in range(n):
            for r, (fx, fy) in enumerate(_CHIP_FLIPS):
                kk = 2 * _flip(x, fx) + _flip(y, fy)
                cps.append(_remote(ins[t].at[kk], outs[t].at[r], send.at[t, r], recv.at[t, r],
                                   (_flip(x, fx), _flip(y, fy), c)))
        return cps

    shapes = [jax.ShapeDtypeStruct((3,) + s.shape[1:], F32) for s in sums]
    return _Job(sums, shapes, [pltpu.SemaphoreType.DMA((n, 3)), pltpu.SemaphoreType.DMA((n, 3))], copies)


def _join_job(fulls):
    n = len(fulls)

    def copies(ins, outs, sems):
        x, y, c = _place()
        send, recv = sems
        cps = []
        for t in range(n):
            half = ins[t].shape[0] // 2
            rows = pl.ds(c * half, half)
            cps.append(_remote(ins[t].at[rows], outs[t].at[rows], send.at[t], recv.at[t], (x, y, 1 - c)))
        return cps

    shapes = [jax.ShapeDtypeStruct(f.shape, f.dtype) for f in fulls]
    return _Job(fulls, shapes, [pltpu.SemaphoreType.DMA((n,)), pltpu.SemaphoreType.DMA((n,))], copies,
                alias={t: t for t in range(n)})


def _gather_job(bufs, forward):
    n = len(bufs)

    def copies(ins, outs, sems):
        x, y, c = _place()
        send, recv = sems
        cps = []
        for t in range(n):
            half = ins[t].shape[1] // 2
            rows = pl.ds(c * half, half)
            for r, (fx, fy) in enumerate(_CHIP_FLIPS):
                if forward:
                    kk = 2 * _flip(x, fx) + _flip(y, fy)
                    dev = (x, y, 1 - c)
                else:
                    kk = 2 * x + y
                    dev = (_flip(x, fx), _flip(y, fy), c)
                cps.append(_remote(ins[t].at[kk, rows], outs[t].at[kk, rows], send.at[t, r], recv.at[t, r], dev))
        return cps

    shapes = [jax.ShapeDtypeStruct(b.shape, b.dtype) for b in bufs]
    return _Job(bufs, shapes, [pltpu.SemaphoreType.DMA((n, 3)), pltpu.SemaphoreType.DMA((n, 3))], copies,
                alias={t: t for t in range(n)})


def _call(body, args, *, name, grid, in_specs, out_specs, out_shape, scratch_shapes=(), params=None, jobs=()):
    n_in, n_out, n_scr = len(in_specs), len(out_specs), len(scratch_shapes)
    job_in = [len(j.operands) for j in jobs]
    job_out = [len(j.out_shapes) for j in jobs]
    job_sem = [len(j.sems) for j in jobs]

    def wrapped(*refs):
        pos = 0
        ins = refs[pos:pos + n_in]; pos += n_in
        jins = []
        for k in job_in:
            jins.append(refs[pos:pos + k]); pos += k
        outs = refs[pos:pos + n_out]; pos += n_out
        jouts = []
        for k in job_out:
            jouts.append(refs[pos:pos + k]); pos += k
        scr = refs[pos:pos + n_scr]; pos += n_scr
        jsems = []
        for k in job_sem:
            jsems.append(refs[pos:pos + k]); pos += k
        if jobs:
            ids = [pl.program_id(d) for d in range(len(grid))]
            first = functools.reduce(jnp.logical_and, [i == 0 for i in ids])
            last = functools.reduce(jnp.logical_and, [i == g - 1 for i, g in zip(ids, grid)])

            @pl.when(first)
            def _():
                for j, ji, jo, js in zip(jobs, jins, jouts, jsems):
                    for cp in j.copies(ji, jo, js):
                        cp.start()

        body(*ins, *outs, *scr)
        if jobs:
            @pl.when(last)
            def _():
                for j, ji, jo, js in zip(jobs, jins, jouts, jsems):
                    for cp in j.copies(ji, jo, js):
                        cp.wait()

    aliases = {}
    in_pos, out_pos = n_in, n_out
    for j in jobs:
        for i, o in j.alias.items():
            aliases[in_pos + i] = out_pos + o
        in_pos += len(j.operands)
        out_pos += len(j.out_shapes)
    res = pl.pallas_call(
        wrapped, name=name, grid=grid,
        in_specs=list(in_specs) + [HBM] * sum(job_in),
        out_specs=list(out_specs) + [HBM] * sum(job_out),
        out_shape=list(out_shape) + [_in_hbm(s) for j in jobs for s in j.out_shapes],
        scratch_shapes=list(scratch_shapes) + [s for j in jobs for s in j.sems],
        input_output_aliases=aliases,
        compiler_params=params,
    )(*args, *[a for j in jobs for a in j.operands])
    own, rest = list(res[:n_out]), list(res[n_out:])
    job_res = []
    for k in job_out:
        job_res.append(rest[:k])
        rest = rest[k:]
    return own, job_res


def _dot(a, b):
    return jnp.dot(a, b, preferred_element_type=F32)


def _dot_nt(a, b):
    return lax.dot_general(a, b, (((1,), (1,)), ((), ())), preferred_element_type=F32)


def _dot_tn(a, b):
    return lax.dot_general(a, b, (((0,), (0,)), ((), ())), preferred_element_type=F32)


def _rms_r(x):
    return lax.rsqrt(jnp.mean(x * x, axis=-1, keepdims=True) + EPS)


def _rms_bwd(x, r, g, dy):
    n = x * r
    dn = dy * g
    dx = r * (dn - n * jnp.mean(dn * n, axis=-1, keepdims=True))
    return dx, dy * n


def _seg64_sum(v):
    rows, width = v.shape
    lane = lax.broadcasted_iota(jnp.int32, (rows, LANES), 1)
    lo = lane < HEAD_DIM
    outs = []
    for c in range(width // LANES):
        ch = v[:, c * LANES:(c + 1) * LANES]
        s_lo = jnp.sum(jnp.where(lo, ch, 0.0), axis=-1, keepdims=True)
        s_hi = jnp.sum(jnp.where(lo, 0.0, ch), axis=-1, keepdims=True)
        outs.append(jnp.where(lo, s_lo, s_hi))
    return outs[0] if len(outs) == 1 else jnp.concatenate(outs, axis=-1)


def _head_r(v):
    return lax.rsqrt(_seg64_sum(v * v) * (1.0 / HEAD_DIM) + EPS)


def _swap16(ch):
    lane = lax.broadcasted_iota(jnp.int32, ch.shape, 1)
    return jnp.where((lane % 32) < 16, pltpu.roll(ch, LANES - 16, 1), pltpu.roll(ch, 16, 1))


def _by_chunk(fn, v):
    outs = [fn(v[:, c * LANES:(c + 1) * LANES]) for c in range(v.shape[1] // LANES)]
    return outs[0] if len(outs) == 1 else jnp.concatenate(outs, axis=-1)


def _rope(v, cos, sin_signed):
    return _by_chunk(lambda ch: ch * cos + _swap16(ch) * sin_signed, v)


def _rope_t(g, cos, sin_signed):
    return _by_chunk(lambda ch: ch * cos + _swap16(ch * sin_signed), g)


def _rope_tables(seq):
    nf = HEAD_DIM // 4
    freqs = ROPE_THETA ** (-jnp.arange(nf, dtype=F32) / nf)
    pos = jnp.arange(seq, dtype=jnp.int32)
    row = (pos // GRID_W).astype(F32)
    col = (pos % GRID_W).astype(F32)
    ang_r = row[:, None] * freqs[None, :]
    ang_c = col[:, None] * freqs[None, :]
    cr, sr, cc, sc = jnp.cos(ang_r), jnp.sin(ang_r), jnp.cos(ang_c), jnp.sin(ang_c)
    cos = jnp.concatenate([cr, cr, cc, cc], axis=1)
    sin = jnp.concatenate([-sr, sr, -sc, sc], axis=1)
    return cos, sin


def _t5_bucket(rel):
    nb = N_BUCKETS // 2
    ret = (rel > 0).astype(jnp.int32) * nb
    n = jnp.abs(rel)
    max_exact = nb // 2
    nf = jnp.maximum(n, 1).astype(jnp.float32)
    large = max_exact + (jnp.log(nf / max_exact) / np.float32(np.log(MAX_DISTANCE / max_exact))
                         * (nb - max_exact)).astype(jnp.int32)
    large = jnp.minimum(large, nb - 1)
    return ret + jnp.where(n < max_exact, n, large)


def _window_tables():
    a = jnp.arange(BLOCK, dtype=jnp.int32)
    c = jnp.arange(3 * BLOCK, dtype=jnp.int32)
    rel = c[None, :] - BLOCK - a[:, None]
    bucket = _t5_bucket(rel)
    band = (jnp.abs(rel) <= BLOCK).astype(jnp.int32)
    to3 = lambda t: t.reshape(BLOCK, 3, BLOCK).transpose(1, 2, 0)
    return to3(bucket), to3(band)


def _pre_proj(x, g1, w_in, gq, gk, cq, sq, ck, sk, *, seq, tm, sub):
    tokens = x.shape[0]
    n_seq = seq // tm
    nblk = tm // BLOCK
    batch = tokens // seq

    def body(x_ref, g1_ref, w_ref, gq_ref, gk_ref, cq_ref, sq_ref, ck_ref, sk_ref,
             h1_ref, raw_ref, qa_ref, ka_ref, kta_ref, va_ref, vta_ref,
             qtb_ref, kb_ref, ktb_ref, vb_ref, vtb_ref, proj):
        for r in range(tm // sub):
            rows = slice(r * sub, (r + 1) * sub)
            xv = x_ref[rows, :]
            h = (xv * _rms_r(xv) * g1_ref[...]).astype(MM)
            h1_ref[rows, :] = h
            for j in range(N_CHIPS):
                proj[rows, j * IN_CHUNK:(j + 1) * IN_CHUNK] = _dot(h, w_ref[j])
            qa = proj[rows, 0:Q_WIDTH]
            ka = proj[rows, Q_WIDTH:QK_RAW]
            raw_ref[rows, :] = proj[rows, 0:QK_RAW]
            qn = qa * _head_r(qa) * gq_ref[...]
            qa_ref[rows, :] = _rope(qn, cq_ref[rows, :], sq_ref[rows, :]).astype(MM)
            kn = ka * _head_r(ka) * gk_ref[...]
            kr = _rope(kn, ck_ref[rows, :], sk_ref[rows, :])
            ka_ref[rows, :] = kr.astype(MM)
            kta_ref[0, :, rows] = kr.T.astype(MM)
            va = proj[rows, 640:768]
            va_ref[rows, :] = va.astype(MM)
            vta_ref[0, :, rows] = va.T.astype(MM)
            qb = proj[rows, 768:1280] * SCALE
            kb = proj[rows, 1280:1408]
            vb = proj[rows, 1408:1536]
            kb_ref[rows, :] = kb.astype(MM)
            vb_ref[rows, :] = vb.astype(MM)
            for j in range(sub // BLOCK):
                blk = slice(j * BLOCK, (j + 1) * BLOCK)
                qtb_ref[r * (sub // BLOCK) + j] = qb[blk, :].T.astype(MM)
                ktb_ref[r * (sub // BLOCK) + j] = kb[blk, :].T.astype(MM)
                vtb_ref[r * (sub // BLOCK) + j] = vb[blk, :].T.astype(MM)

    tok = lambda w: pl.BlockSpec((tm, w), lambda i: (i, 0))
    tab = lambda w: pl.BlockSpec((tm, w), lambda i: (i % n_seq, 0))
    row = lambda w: pl.BlockSpec((1, w), lambda i: (0, 0))
    tposed = pl.BlockSpec((1, LANES, tm), lambda i: (i // n_seq, 0, i % n_seq))
    blocks = pl.BlockSpec((nblk, BLOCK, LANES), lambda i: (i, 0, 0))
    qblocks = pl.BlockSpec((nblk, Q_WIDTH, BLOCK), lambda i: (i, 0, 0))
    tok_mm = lambda w: jax.ShapeDtypeStruct((tokens, w), MM)
    return pl.pallas_call(
        body, name="pre_proj",
        grid=(tokens // tm,),
        in_specs=[tok(D_MODEL), row(D_MODEL),
                  pl.BlockSpec((N_CHIPS, D_MODEL, IN_CHUNK), lambda i: (0, 0, 0)),
                  row(Q_WIDTH), row(KV_WIDTH), tab(KV_WIDTH), tab(KV_WIDTH), tab(KV_WIDTH), tab(KV_WIDTH)],
        out_specs=[tok(D_MODEL), tok(QK_RAW), tok(Q_WIDTH), tok(KV_WIDTH), tposed, tok(KV_WIDTH), tposed,
                   qblocks, tok(KV_WIDTH), blocks, tok(KV_WIDTH), blocks],
        out_shape=[
            tok_mm(D_MODEL),
            jax.ShapeDtypeStruct((tokens, QK_RAW), F32),
            tok_mm(Q_WIDTH),
            tok_mm(KV_WIDTH),
            jax.ShapeDtypeStruct((batch, KV_WIDTH, seq), MM),
            tok_mm(KV_WIDTH),
            jax.ShapeDtypeStruct((batch, KV_WIDTH, seq), MM),
            jax.ShapeDtypeStruct((tokens // BLOCK, Q_WIDTH, BLOCK), MM),
            tok_mm(KV_WIDTH),
            jax.ShapeDtypeStruct((tokens // BLOCK, KV_WIDTH, BLOCK), MM),
            tok_mm(KV_WIDTH),
            jax.ShapeDtypeStruct((tokens // BLOCK, KV_WIDTH, BLOCK), MM),
        ],
        scratch_shapes=[pltpu.VMEM((tm, IN_TOTAL), F32)],
        compiler_params=_cparams(("parallel",), 48),
    )(x, g1, w_in, gq, gk, cq, sq, ck, sk)


def _kv_half(v2, kv):
    return jnp.where(kv == 0, v2[:, :HEAD_DIM], v2[:, HEAD_DIM:])


def _attn_a_fwd(qa, kta, va, *, seq, bq, jobs=()):
    tokens = qa.shape[0]
    batch = tokens // seq
    nq = seq // bq

    def body(q_ref, kt_ref, v_ref, o_ref, p_ref, linv_ref):
        kv = pl.program_id(1)
        kt = kt_ref[0]
        lane = lax.broadcasted_iota(jnp.int32, (seq, KV_WIDTH), 1)
        v = jnp.where((lane < HEAD_DIM) == (kv == 0), v_ref[...], jnp.ones((), MM))
        for g in range(GROUP):
            sl = slice(g * HEAD_DIM, (g + 1) * HEAD_DIM)
            s = _dot(q_ref[:, sl], kt)
            pb = jnp.exp((s - jnp.max(s, axis=-1, keepdims=True)).astype(MM))
            p_ref[0, g] = pb
            o2 = _dot(pb, v)
            linv = 1.0 / _kv_half(o2, 1 - kv)[:, 0:1]
            o_ref[:, sl] = _kv_half(o2, kv) * linv
            linv_ref[0, :, g:g + 1] = linv

    return _call(
        body, (qa, kta, va), name="attn_a_fwd", jobs=jobs,
        grid=(batch, N_KV, nq),
        in_specs=[pl.BlockSpec((bq, GROUP * HEAD_DIM), lambda b, k, i: (b * nq + i, k)),
                  pl.BlockSpec((1, HEAD_DIM, seq), lambda b, k, i: (b, k, 0)),
                  pl.BlockSpec((seq, KV_WIDTH), lambda b, k, i: (b, 0))],
        out_specs=[pl.BlockSpec((bq, GROUP * HEAD_DIM), lambda b, k, i: (b * nq + i, k)),
                   pl.BlockSpec((1, GROUP, bq, seq), lambda b, k, i: (k, 0, b * nq + i, 0)),
                   pl.BlockSpec((1, bq, GROUP), lambda b, k, i: (k, b * nq + i, 0))],
        out_shape=[jax.ShapeDtypeStruct((tokens, Q_WIDTH), F32),
                   jax.ShapeDtypeStruct((N_KV, GROUP, tokens, seq), MM),
                   jax.ShapeDtypeStruct((N_KV, tokens, GROUP), F32)],
        params=_cparams(("arbitrary", "arbitrary", "arbitrary"), 56))


def _attn_a_bwd(qa, ka, vta, do, o, p, linv, *, seq, bq, jobs=()):
    tokens = qa.shape[0]
    batch = tokens // seq
    nq = seq // bq

    def body(q_ref, k_ref, vt_ref, do_ref, o_ref, p_ref, linv_ref, dq_ref, dkt_ref, dvt_ref):
        kv = pl.program_id(1)

        @pl.when(pl.program_id(2) == 0)
        def _():
            dkt_ref[...] = jnp.zeros_like(dkt_ref)
            dvt_ref[...] = jnp.zeros_like(dvt_ref)

        vt = vt_ref[0]
        k2 = k_ref[...]
        for g in range(GROUP):
            sl = slice(g * HEAD_DIM, (g + 1) * HEAD_DIM)
            dof = do_ref[:, sl]
            delta = jnp.sum(dof * o_ref[:, sl], axis=-1, keepdims=True)
            linv_g = linv_ref[0, :, g:g + 1]
            pb = p_ref[0, g]
            dp = _dot(dof.astype(MM), vt)
            ds = pb * ((dp - delta) * linv_g).astype(MM)
            dq_ref[:, sl] = _kv_half(_dot(ds, k2), kv)
            dkt_ref[0] += _dot_tn(q_ref[:, sl], ds)
            dvt_ref[0] += _dot_tn((dof * linv_g).astype(MM), pb)

    qspec = pl.BlockSpec((bq, GROUP * HEAD_DIM), lambda b, k, i: (b * nq + i, k))
    tspec = pl.BlockSpec((1, HEAD_DIM, seq), lambda b, k, i: (b, k, 0))
    return _call(
        body, (qa, ka, vta, do, o, p, linv), name="attn_a_bwd", jobs=jobs,
        grid=(batch, N_KV, nq),
        in_specs=[qspec, pl.BlockSpec((seq, KV_WIDTH), lambda b, k, i: (b, 0)), tspec, qspec, qspec,
                  pl.BlockSpec((1, GROUP, bq, seq), lambda b, k, i: (k, 0, b * nq + i, 0)),
                  pl.BlockSpec((1, bq, GROUP), lambda b, k, i: (k, b * nq + i, 0))],
        out_specs=[qspec, tspec, tspec],
        out_shape=[jax.ShapeDtypeStruct((tokens, Q_WIDTH), F32),
                   jax.ShapeDtypeStruct((batch, KV_WIDTH, seq), F32),
                   jax.ShapeDtypeStruct((batch, KV_WIDTH, seq), F32)],
        params=_cparams(("arbitrary", "arbitrary", "arbitrary"), 56))


def _bias_build(rel_bias_t, bucket_t, band_t):
    def body(tab_ref, bucket_ref, band_ref, bias_ref):
        for h in range(GROUP * N_KV):
            for piece in range(3):
                bk = bucket_ref[piece]
                acc = jnp.zeros((BLOCK, BLOCK), F32)
                for b in range(N_BUCKETS):
                    acc = jnp.where(bk == b, tab_ref[h, b], acc)
                g = h % GROUP
                bias_ref[h // GROUP, piece, :, g * BLOCK:(g + 1) * BLOCK] = jnp.where(band_ref[piece] != 0, acc, NEG_INF)

    return pl.pallas_call(
        body, name="bias_build",
        in_specs=[SMEM, VMEM, VMEM], out_specs=VMEM,
        out_shape=jax.ShapeDtypeStruct((N_KV, 3, BLOCK, GROUP * BLOCK), F32),
    )(rel_bias_t, bucket_t, band_t)


def _pad_heads(t, kv):
    outs = []
    for g in range(GROUP):
        tg = t[g * HEAD_DIM:(g + 1) * HEAD_DIM, :]
        zero = jnp.zeros_like(tg)
        outs.append(jnp.concatenate([jnp.where(kv == 0, tg, zero), jnp.where(kv == 0, zero, tg)], axis=0))
    return jnp.concatenate(outs, axis=-1)


def _unpad_heads(t, kv):
    outs = [_kv_half(t[:, g * BLOCK:(g + 1) * BLOCK].T, kv) for g in range(GROUP)]
    return jnp.concatenate(outs, axis=-1)


def _sink_row(sink_ref, kv):
    lane_head = lax.broadcasted_iota(jnp.int32, (1, GROUP * BLOCK), 1) // BLOCK
    row = jnp.zeros((1, GROUP * BLOCK), F32)
    for g in range(GROUP):
        row = jnp.where(lane_head == g, sink_ref[0, kv * GROUP + g], row)
    return row


def _window_scores_t(k_ref, idx, qpad, bias_ref, n, nblk):
    pieces = []
    for piece in range(3):
        s = _dot(k_ref[idx[piece]], qpad) + bias_ref[0, piece]
        if piece == 0:
            s = jnp.where(n > 0, s, NEG_INF)
        if piece == 2:
            s = jnp.where(n < nblk - 1, s, NEG_INF)
        pieces.append(s)
    return pieces


def _attn_b_fwd(qtb, kb3, vtb, bias, sink, *, seq, jobs=()):
    nblk_all = qtb.shape[0]
    tokens = nblk_all * BLOCK
    batch = tokens // seq
    nblk = seq // BLOCK

    def body(sink_ref, q_ref, k_ref, vt_ref, bias_ref, o_ref, lse_ref):
        kv = pl.program_id(0)
        sink_row = _sink_row(sink_ref, kv)

        def block(n, carry):
            idx = (jnp.maximum(n - 1, 0), n, jnp.minimum(n + 1, nblk - 1))
            rows = pl.ds(pl.multiple_of(n * BLOCK, BLOCK), BLOCK)
            qpad = _pad_heads(q_ref[n], kv)
            ss = _window_scores_t(k_ref, idx, qpad, bias_ref, n, nblk)
            m = jnp.maximum(jnp.maximum(jnp.max(ss[0], axis=0, keepdims=True),
                                        jnp.max(ss[1], axis=0, keepdims=True)),
                            jnp.maximum(jnp.max(ss[2], axis=0, keepdims=True), sink_row))
            ps = [jnp.exp(s - m) for s in ss]
            denom = (jnp.sum(ps[0], axis=0, keepdims=True) + jnp.sum(ps[1], axis=0, keepdims=True)
                     + jnp.sum(ps[2], axis=0, keepdims=True) + jnp.exp(sink_row - m))
            ot = (_dot(vt_ref[idx[0]], ps[0].astype(MM)) + _dot(vt_ref[idx[1]], ps[1].astype(MM))
                  + _dot(vt_ref[idx[2]], ps[2].astype(MM)))
            o_ref[rows, :] = _unpad_heads(ot * (1.0 / denom), kv)
            lse_ref[0, n] = jnp.broadcast_to(m + jnp.log(denom), (8, GROUP * BLOCK))
            return carry

        lax.fori_loop(0, nblk, block, 0, unroll=8)

    both = pl.BlockSpec((nblk, BLOCK, KV_WIDTH), lambda k, b: (b, 0, 0))
    return _call(
        body, (sink, qtb, kb3, vtb, bias), name="attn_b_fwd", jobs=jobs,
        grid=(N_KV, batch),
        in_specs=[SMEM, pl.BlockSpec((nblk, GROUP * HEAD_DIM, BLOCK), lambda k, b: (b, k, 0)), both, both,
                  pl.BlockSpec((1, 3, BLOCK, GROUP * BLOCK), lambda k, b: (k, 0, 0, 0))],
        out_specs=[pl.BlockSpec((seq, GROUP * HEAD_DIM), lambda k, b: (b, k)),
                   pl.BlockSpec((1, nblk, 8, GROUP * BLOCK), lambda k, b: (k, b, 0, 0))],
        out_shape=[jax.ShapeDtypeStruct((tokens, Q_WIDTH), F32),
                   jax.ShapeDtypeStruct((N_KV, nblk_all, 8, GROUP * BLOCK), F32)],
        params=_cparams(("arbitrary", "arbitrary"), 48))


def _attn_b_bwd(qtb, kb3, ktb, vb3, do, o, lse, bias, sink, *, seq, jobs=()):
    nblk_all = qtb.shape[0]
    tokens = nblk_all * BLOCK
    batch = tokens // seq
    nblk = seq // BLOCK

    def body(sink_ref, q_ref, k_ref, kt_ref, v_ref, do_ref, o_ref, lse_ref, bias_ref,
             dq_ref, dk_ref, dv_ref, dbias_ref, dsink_ref):
        kv = pl.program_id(0)
        sink_row = _sink_row(sink_ref, kv)

        @pl.when(pl.program_id(1) == 0)
        def _():
            dbias_ref[...] = jnp.zeros_like(dbias_ref)
            dsink_ref[...] = jnp.zeros_like(dsink_ref)

        dk_ref[...] = jnp.zeros_like(dk_ref)
        dv_ref[...] = jnp.zeros_like(dv_ref)

        def block(n, dsink):
            idx = (jnp.maximum(n - 1, 0), n, jnp.minimum(n + 1, nblk - 1))
            rows = pl.ds(pl.multiple_of(n * BLOCK, BLOCK), BLOCK)
            qpad = _pad_heads(q_ref[n], kv)
            dot_t = do_ref[rows, :].T
            prod = dot_t * o_ref[rows, :].T
            delta = jnp.concatenate(
                [jnp.sum(prod[g * HEAD_DIM:(g + 1) * HEAD_DIM, :], axis=0, keepdims=True) for g in range(GROUP)],
                axis=-1)
            dopad = _pad_heads(dot_t.astype(MM), kv)
            lse_row = lse_ref[0, n][0:1, :]
            ss = _window_scores_t(k_ref, idx, qpad, bias_ref, n, nblk)
            dqt = jnp.zeros((KV_WIDTH, GROUP * BLOCK), F32)
            for piece in range(3):
                pt = jnp.exp(ss[piece] - lse_row)
                dst = pt * (_dot(v_ref[idx[piece]], dopad) - delta)
                dsb = dst.astype(MM)
                dbias_ref[0, piece] += dst
                dqt = dqt + _dot(kt_ref[idx[piece]], dsb)
                dk_ref[0, idx[piece]] += _dot_nt(dsb, qpad)
                dv_ref[0, idx[piece]] += _dot_nt(pt.astype(MM), dopad)
            dq_ref[rows, :] = _unpad_heads(dqt, kv)
            return dsink - jnp.exp(sink_row - lse_row) * delta

        dsink = lax.fori_loop(
            0, nblk // 4, lambda i, c: block(4 * i + 3, block(4 * i + 2, block(4 * i + 1, block(4 * i, c)))),
            jnp.zeros((1, GROUP * BLOCK), F32))
        dsink_ref[0] += jnp.broadcast_to(dsink, (8, GROUP * BLOCK))

    qspec = pl.BlockSpec((seq, GROUP * HEAD_DIM), lambda k, b: (b, k))
    both = pl.BlockSpec((nblk, BLOCK, KV_WIDTH), lambda k, b: (b, 0, 0))
    grad = pl.BlockSpec((1, nblk, BLOCK, KV_WIDTH), lambda k, b: (k, b, 0, 0))
    return _call(
        body, (sink, qtb, kb3, ktb, vb3, do, o, lse, bias), name="attn_b_bwd", jobs=jobs,
        grid=(N_KV, batch),
        in_specs=[SMEM, pl.BlockSpec((nblk, GROUP * HEAD_DIM, BLOCK), lambda k, b: (b, k, 0)), both, both, both,
                  qspec, qspec, pl.BlockSpec((1, nblk, 8, GROUP * BLOCK), lambda k, b: (k, b, 0, 0)),
                  pl.BlockSpec((1, 3, BLOCK, GROUP * BLOCK), lambda k, b: (k, 0, 0, 0))],
        out_specs=[qspec, grad, grad,
                   pl.BlockSpec((1, 3, BLOCK, GROUP * BLOCK), lambda k, b: (k, 0, 0, 0)),
                   pl.BlockSpec((1, 8, GROUP * BLOCK), lambda k, b: (k, 0, 0))],
        out_shape=[jax.ShapeDtypeStruct((tokens, Q_WIDTH), F32),
                   jax.ShapeDtypeStruct((N_KV, nblk_all, BLOCK, KV_WIDTH), F32),
                   jax.ShapeDtypeStruct((N_KV, nblk_all, BLOCK, KV_WIDTH), F32),
                   jax.ShapeDtypeStruct((N_KV, 3, BLOCK, GROUP * BLOCK), F32),
                   jax.ShapeDtypeStruct((N_KV, 8, GROUP * BLOCK), F32)],
        params=_cparams(("arbitrary", "arbitrary"), 48))


def _wo_post(oa, ob, w_o, x, g2, g3, *, tm, sub):
    tokens = x.shape[0]

    def body(oa_ref, ob_ref, w_ref, x_ref, g2_ref, g3_ref, mix_ref, x1_ref, h2_ref, o_ref):
        for r in range(tm // sub):
            rows = slice(r * sub, (r + 1) * sub)
            o = jnp.concatenate([oa_ref[rows, :].astype(MM), ob_ref[rows, :].astype(MM)], axis=-1)
            o_ref[rows, :] = o
            mix = _dot(o, w_ref[...])
            mix_ref[rows, :] = mix
            x1 = x_ref[rows, :] + mix * _rms_r(mix) * g2_ref[...]
            x1_ref[rows, :] = x1
            h2_ref[rows, :] = (x1 * _rms_r(x1) * g3_ref[...]).astype(MM)

    tok = lambda w: pl.BlockSpec((tm, w), lambda i: (i, 0))
    row = pl.BlockSpec((1, D_MODEL), lambda i: (0, 0))
    return pl.pallas_call(
        body, name="wo_post",
        grid=(tokens // tm,),
        in_specs=[tok(Q_WIDTH), tok(Q_WIDTH), pl.BlockSpec((D_MODEL, D_MODEL), lambda i: (0, 0)),
                  tok(D_MODEL), row, row],
        out_specs=[tok(D_MODEL), tok(D_MODEL), tok(D_MODEL), tok(D_MODEL)],
        out_shape=[jax.ShapeDtypeStruct((tokens, D_MODEL), F32),
                   jax.ShapeDtypeStruct((tokens, D_MODEL), F32),
                   jax.ShapeDtypeStruct((tokens, D_MODEL), MM),
                   jax.ShapeDtypeStruct((tokens, D_MODEL), MM)],
        compiler_params=_cparams(("parallel",), 40),
    )(oa, ob, w_o, x, g2, g3)


def _resident(shape):
    return pl.BlockSpec(shape, lambda i: (0,) * len(shape), pipeline_mode=pl.Buffered(1))


def _ffn_fwd_loss(h2, w_up, w_down, x1, target, g4, *, tm):
    tokens = h2.shape[0]
    nt = tokens // tm

    def body(h2_ref, wu_ref, wd_ref, x1_ref, t_ref, g4_ref, u_ref, df_ref, dy_ref, loss_ref, dg4_ref):
        h2v = h2_ref[...]
        f = jnp.zeros((tm, D_MODEL), F32)
        for c in range(N_CHIPS):
            u = jnp.maximum(_dot(h2v, wu_ref[c]), 0.0)
            u_ref[:, c * FF_CHUNK:(c + 1) * FF_CHUNK] = u.astype(MM)
            f = f + _dot((u * u).astype(MM), wd_ref[c * FF_CHUNK:(c + 1) * FF_CHUNK, :])
        r = _rms_r(f)
        g4v = g4_ref[...]
        err = x1_ref[...] + f * r * g4v - t_ref[...]
        sq = jnp.sum(err * err, axis=-1, keepdims=True)
        loss_ref[0] = jnp.broadcast_to(jnp.sum(sq, axis=0, keepdims=True) * (0.5 / D_MODEL), (8, LANES))
        dy = err * (1.0 / D_MODEL)
        dy_ref[...] = dy
        dfv, dgv = _rms_bwd(f, r, g4v, dy)
        df_ref[...] = dfv.astype(MM)
        dg4_ref[0] = jnp.sum(dgv, axis=0, keepdims=True)

    tok = pl.BlockSpec((tm, D_MODEL), lambda i: (i, 0))
    return pl.pallas_call(
        body, name="ffn_fwd_loss",
        grid=(nt,),
        in_specs=[tok, _resident((N_CHIPS, D_MODEL, FF_CHUNK)), _resident((D_FF, D_MODEL)),
                  tok, tok, pl.BlockSpec((1, D_MODEL), lambda i: (0, 0))],
        out_specs=[pl.BlockSpec((tm, D_FF), lambda i: (i, 0)), tok, tok,
                   pl.BlockSpec((1, 8, LANES), lambda i: (i, 0, 0)),
                   pl.BlockSpec((1, 1, D_MODEL), lambda i: (i, 0, 0))],
        out_shape=[jax.ShapeDtypeStruct((tokens, D_FF), MM),
                   jax.ShapeDtypeStruct((tokens, D_MODEL), MM),
                   jax.ShapeDtypeStruct((tokens, D_MODEL), F32),
                   jax.ShapeDtypeStruct((nt, 8, LANES), F32),
                   jax.ShapeDtypeStruct((nt, 1, D_MODEL), F32)],
        compiler_params=_cparams(("parallel",), 56),
    )(h2, w_up, w_down, x1, target, g4)


def _ffn_bwd_act(df, w_down, u, w_up, x1, dy, mix, g3, g2, *, tm):
    tokens = df.shape[0]
    nt = tokens // tm

    def body(df_ref, wd_ref, u_ref, wu_ref, x1_ref, dy_ref, mix_ref, g3_ref, g2_ref,
             dz_ref, dx1_ref, dmix_ref, dg3_ref, dg2_ref):
        dfv = df_ref[...]
        dh2 = jnp.zeros((tm, D_MODEL), F32)
        for c in range(N_CHIPS):
            cols = slice(c * FF_CHUNK, (c + 1) * FF_CHUNK)
            da = _dot_nt(dfv, wd_ref[cols, :])
            dz = (da * (2.0 * u_ref[:, cols].astype(F32))).astype(MM)
            dz_ref[:, cols] = dz
            dh2 = dh2 + _dot_nt(dz, wu_ref[c])
        x1 = x1_ref[...]
        dxn, dg3v = _rms_bwd(x1, _rms_r(x1), g3_ref[...], dh2)
        dx1 = dy_ref[...] + dxn
        dx1_ref[...] = dx1
        dg3_ref[0] = jnp.sum(dg3v, axis=0, keepdims=True)
        mix = mix_ref[...]
        dmix, dg2v = _rms_bwd(mix, _rms_r(mix), g2_ref[...], dx1)
        dmix_ref[...] = dmix.astype(MM)
        dg2_ref[0] = jnp.sum(dg2v, axis=0, keepdims=True)

    tok = pl.BlockSpec((tm, D_MODEL), lambda i: (i, 0))
    wide = pl.BlockSpec((tm, D_FF), lambda i: (i, 0))
    row = pl.BlockSpec((1, D_MODEL), lambda i: (0, 0))
    part = pl.BlockSpec((1, 1, D_MODEL), lambda i: (i, 0, 0))
    return pl.pallas_call(
        body, name="ffn_bwd_act",
        grid=(nt,),
        in_specs=[tok, _resident((D_FF, D_MODEL)), wide, _resident((N_CHIPS, D_MODEL, FF_CHUNK)),
                  tok, tok, tok, row, row],
        out_specs=[wide, tok, tok, part, part],
        out_shape=[jax.ShapeDtypeStruct((tokens, D_FF), MM),
                   jax.ShapeDtypeStruct((tokens, D_MODEL), F32),
                   jax.ShapeDtypeStruct((tokens, D_MODEL), MM),
                   jax.ShapeDtypeStruct((nt, 1, D_MODEL), F32),
                   jax.ShapeDtypeStruct((nt, 1, D_MODEL), F32)],
        compiler_params=_cparams(("parallel",), 56),
    )(df, w_down, u, w_up, x1, dy, mix, g3, g2)


def _tn_matmul(a, b, *, name, tm, tn, tk, chunked=False, square_a=False, vmem_mb=48, jobs=()):
    tokens, m_dim = a.shape
    n_dim = b.shape[1]
    if chunked:
        assert tm == m_dim

    def body(a_ref, b_ref, o_ref):
        av = a_ref[...]
        if square_a:
            av = av.astype(F32)
            av = av * av
        part = _dot_tn(av.astype(MM), b_ref[...].astype(MM))
        part = part[None] if chunked else part

        @pl.when(pl.program_id(2) == 0)
        def _():
            o_ref[...] = part

        @pl.when(pl.program_id(2) > 0)
        def _():
            o_ref[...] += part

    if chunked:
        out_spec = pl.BlockSpec((1, tm, tn), lambda i, j, k: (j, 0, 0))
        out_shape = jax.ShapeDtypeStruct((n_dim // tn, m_dim, tn), F32)
    else:
        out_spec = pl.BlockSpec((tm, tn), lambda i, j, k: (i, j))
        out_shape = jax.ShapeDtypeStruct((m_dim, n_dim), F32)
    (out,), job_res = _call(
        body, (a, b), name=name, jobs=jobs,
        grid=(m_dim // tm, n_dim // tn, tokens // tk),
        in_specs=[pl.BlockSpec((tk, tm), lambda i, j, k: (k, i)),
                  pl.BlockSpec((tk, tn), lambda i, j, k: (k, j))],
        out_specs=[out_spec], out_shape=[_in_hbm(out_shape)],
        params=_cparams(("arbitrary", "arbitrary", "arbitrary"), vmem_mb))
    return out, job_res


def _wo_bwd(dmix, w_o, *, tm):
    tokens = dmix.shape[0]

    def body(dm_ref, w_ref, doa_ref, dob_ref):
        dm = dm_ref[...]
        doa_ref[...] = _dot_nt(dm, w_ref[0:Q_WIDTH, :])
        dob_ref[...] = _dot_nt(dm, w_ref[Q_WIDTH:D_MODEL, :])

    tok = lambda w: pl.BlockSpec((tm, w), lambda i: (i, 0))
    return pl.pallas_call(
        body, name="wo_bwd",
        grid=(tokens // tm,),
        in_specs=[tok(D_MODEL), pl.BlockSpec((D_MODEL, D_MODEL), lambda i: (0, 0))],
        out_specs=[tok(Q_WIDTH), tok(Q_WIDTH)],
        out_shape=[jax.ShapeDtypeStruct((tokens, Q_WIDTH), F32)] * 2,
        compiler_params=_cparams(("parallel",), 40),
    )(dmix, w_o)


def _proj_bwd(dqa, dkta, dvta, dqb, dktb, dvtb, raw, x, dx1, g1, w_in, gq, gk, cq, sq, ck, sk, *, seq, tm, sub,
              jobs=()):
    tokens = x.shape[0]
    nt = tokens // tm
    n_seq = seq // tm
    nblk = tm // BLOCK

    def body(dqa_ref, dkta_ref, dvta_ref, dqb_ref, dkb_ref, dvb_ref, raw_ref, x_ref, dx1_ref, g1_ref, w_ref,
             gq_ref, gk_ref, cq_ref, sq_ref, ck_ref, sk_ref,
             gx_ref, dproj_ref, dg1_ref, dgq_ref, dgk_ref, dp):
        parts = []
        for r in range(tm // sub):
            rows = slice(r * sub, (r + 1) * sub)
            qa = raw_ref[rows, 0:Q_WIDTH]
            dqn = _rope_t(dqa_ref[rows, :], cq_ref[rows, :], sq_ref[rows, :])
            rq = _head_r(qa)
            nq = qa * rq
            dnq = dqn * gq_ref[...]
            dp[rows, 0:Q_WIDTH] = rq * (dnq - nq * (_seg64_sum(dnq * nq) * (1.0 / HEAD_DIM)))

            ka = raw_ref[rows, Q_WIDTH:QK_RAW]
            dkn = _rope_t(dkta_ref[0, :, rows].T, ck_ref[rows, :], sk_ref[rows, :])
            rk = _head_r(ka)
            nk = ka * rk
            dnk = dkn * gk_ref[...]
            dp[rows, 512:640] = rk * (dnk - nk * (_seg64_sum(dnk * nk) * (1.0 / HEAD_DIM)))

            dp[rows, 640:768] = dvta_ref[0, :, rows].T
            dp[rows, 768:1280] = dqb_ref[rows, :] * SCALE
            for j in range(r * sub // BLOCK, (r + 1) * sub // BLOCK):
                dp[j * BLOCK:(j + 1) * BLOCK, 1280:1408] = dkb_ref[0, j] + dkb_ref[1, j]
                dp[j * BLOCK:(j + 1) * BLOCK, 1408:1536] = dvb_ref[0, j] + dvb_ref[1, j]

            dproj = dp[rows, :].astype(MM)
            dproj_ref[rows, :] = dproj
            dh1 = _dot_nt(dproj[:, 0:IN_CHUNK], w_ref[0])
            for j in range(1, N_CHIPS):
                dh1 = dh1 + _dot_nt(dproj[:, j * IN_CHUNK:(j + 1) * IN_CHUNK], w_ref[j])
            xv = x_ref[rows, :]
            dxn, dg1v = _rms_bwd(xv, _rms_r(xv), g1_ref[...], dh1)
            gx_ref[rows, :] = dx1_ref[rows, :] + dxn
            parts.append((jnp.sum(dqn * nq, axis=0, keepdims=True), jnp.sum(dkn * nk, axis=0, keepdims=True),
                          jnp.sum(dg1v, axis=0, keepdims=True)))
        dgq_ref[0] = functools.reduce(jnp.add, [p[0] for p in parts])
        dgk_ref[0] = functools.reduce(jnp.add, [p[1] for p in parts])
        dg1_ref[0] = functools.reduce(jnp.add, [p[2] for p in parts])

    tok = lambda w: pl.BlockSpec((tm, w), lambda i: (i, 0))
    tab = lambda w: pl.BlockSpec((tm, w), lambda i: (i % n_seq, 0))
    row = lambda w: pl.BlockSpec((1, w), lambda i: (0, 0))
    tposed = pl.BlockSpec((1, KV_WIDTH, tm), lambda i: (i // n_seq, 0, i % n_seq))
    blocks = pl.BlockSpec((N_KV, nblk, BLOCK, KV_WIDTH), lambda i: (0, i, 0, 0))
    part = lambda w: pl.BlockSpec((1, 1, w), lambda i: (i, 0, 0))
    return _call(
        body, (dqa, dkta, dvta, dqb, dktb, dvtb, raw, x, dx1, g1, w_in, gq, gk, cq, sq, ck, sk),
        name="proj_bwd", jobs=jobs,
        grid=(nt,),
        in_specs=[tok(Q_WIDTH), tposed, tposed, tok(Q_WIDTH), blocks, blocks, tok(QK_RAW), tok(D_MODEL),
                  tok(D_MODEL), row(D_MODEL),
                  pl.BlockSpec((N_CHIPS, D_MODEL, IN_CHUNK), lambda i: (0, 0, 0)),
                  row(Q_WIDTH), row(KV_WIDTH), tab(KV_WIDTH), tab(KV_WIDTH), tab(KV_WIDTH), tab(KV_WIDTH)],
        out_specs=[tok(D_MODEL), tok(IN_TOTAL), part(D_MODEL), part(Q_WIDTH), part(KV_WIDTH)],
        out_shape=[jax.ShapeDtypeStruct((tokens, D_MODEL), F32),
                   jax.ShapeDtypeStruct((tokens, IN_TOTAL), MM),
                   jax.ShapeDtypeStruct((nt, 1, D_MODEL), F32),
                   jax.ShapeDtypeStruct((nt, 1, Q_WIDTH), F32),
                   jax.ShapeDtypeStruct((nt, 1, KV_WIDTH), F32)],
        scratch_shapes=[pltpu.VMEM((tm, IN_TOTAL), F32)],
        params=_cparams(("arbitrary",), 56))


def _pack_small(dg1, dg2, dg3, dg4, dgq, dgk, dsink, dbias, bucket, loss):
    def body(dg1_ref, dg2_ref, dg3_ref, dg4_ref, dgq_ref, dgk_ref, dsink_ref, dbias_ref, bucket_ref, loss_ref,
             out_ref, rel_ref):
        out_ref[...] = jnp.zeros_like(out_ref)
        for r, ref in ((ROW_G1, dg1_ref), (ROW_G2, dg2_ref), (ROW_G3, dg3_ref), (ROW_G4, dg4_ref)):
            acc = ref[0]
            for t in range(1, ref.shape[0]):
                acc = acc + ref[t]
            out_ref[r:r + 1, :] = acc

        def fold(ref, heads):
            acc = ref[0]
            for t in range(1, ref.shape[0]):
                acc = acc + ref[t]
            tot = acc[:, 0:HEAD_DIM]
            for h in range(1, heads):
                tot = tot + acc[:, h * HEAD_DIM:(h + 1) * HEAD_DIM]
            return tot

        out_ref[ROW_MISC:ROW_MISC + 1, MISC_GQ:MISC_GQ + HEAD_DIM] = fold(dgq_ref, GROUP * N_KV)
        out_ref[ROW_MISC:ROW_MISC + 1, MISC_GK:MISC_GK + HEAD_DIM] = fold(dgk_ref, N_KV)
        for h in range(GROUP * N_KV):
            g = h % GROUP
            out_ref[ROW_MISC:ROW_MISC + 1, MISC_SINK + h:MISC_SINK + h + 1] = jnp.sum(
                dsink_ref[h // GROUP, 0:1, g * BLOCK:(g + 1) * BLOCK], axis=-1, keepdims=True)
        lacc = loss_ref[0, 0:1, 0:1]
        for t in range(1, loss_ref.shape[0]):
            lacc = lacc + loss_ref[t, 0:1, 0:1]
        out_ref[ROW_MISC:ROW_MISC + 1, MISC_LOSS:MISC_LOSS + 1] = lacc
        lane = lax.broadcasted_iota(jnp.int32, (N_BUCKETS, LANES), 1)
        row = lax.broadcasted_iota(jnp.int32, (N_BUCKETS, LANES), 0)

        def per_bucket(b, acc):
            for h in range(GROUP * N_KV):
                g = h % GROUP
                sel = jnp.zeros((BLOCK, BLOCK), F32)
                for piece in range(3):
                    sel = sel + jnp.where(bucket_ref[piece] == b,
                                          dbias_ref[h // GROUP, piece, :, g * BLOCK:(g + 1) * BLOCK], 0.0)
                tot = jnp.sum(jnp.sum(sel, axis=0, keepdims=True), axis=-1, keepdims=True)
                acc = jnp.where((row == b) & (lane == h), tot, acc)
            return acc

        rel_ref[...] = lax.fori_loop(0, N_BUCKETS, per_bucket, jnp.zeros((N_BUCKETS, LANES), F32))

    return pl.pallas_call(
        body, name="pack_small",
        in_specs=[VMEM] * 10, out_specs=[VMEM, VMEM],
        out_shape=[jax.ShapeDtypeStruct((8, D_MODEL), F32), jax.ShapeDtypeStruct((N_BUCKETS, LANES), F32)],
        compiler_params=pltpu.CompilerParams(vmem_limit_bytes=32 * 1024 * 1024),
    )(dg1, dg2, dg3, dg4, dgq, dgk, dsink, dbias, bucket, loss)


def _gather_weights(shards, whole):
    n = len(shards)
    full = [t for t in range(n) if whole[t]]

    def body(*refs):
        ins, outs = refs[:n], refs[n:2 * n]
        stage = refs[2 * n:3 * n]
        local_sem, ici_send, ici_recv, d2d_send, d2d_recv = refs[3 * n:]
        x, y, c = _place()
        k = 2 * x + y
        sibling = (x, y, 1 - c)
        copies = []
        for t in range(n):
            stage[t][...] = ins[t][...].astype(MM)
            mine = pltpu.make_async_copy(stage[t], outs[t].at[k], local_sem.at[t])
            mine.start()
            copies.append(mine)
        sends = []
        for t in full:
            half = ins[t].shape[0] // 2
            rows = pl.ds(c * half, half)
            for r, (fx, fy) in enumerate(_CHIP_FLIPS):
                cp = _remote(stage[t].at[rows], outs[t].at[k, rows], ici_send.at[t, r], ici_recv.at[t, r],
                             (_flip(x, fx), _flip(y, fy), c))
                cp.start()
                sends.append(cp)
        for t in full:
            half = ins[t].shape[0] // 2
            rows = pl.ds(c * half, half)
            for r, (fx, fy) in enumerate(_CHIP_FLIPS):
                kk = 2 * _flip(x, fx) + _flip(y, fy)
                landed = outs[t].at[kk, rows]
                _remote(landed, landed, ici_send.at[t, r], ici_recv.at[t, r], sibling).wait_recv()
                fwd = _remote(landed, landed, d2d_send.at[t, r], d2d_recv.at[t, r], sibling)
                fwd.start()
                sends.append(fwd)
        for t in full:
            half = ins[t].shape[0] // 2
            other = pl.ds((1 - c) * half, half)
            for r, (fx, fy) in enumerate(_CHIP_FLIPS):
                kk = 2 * _flip(x, fx) + _flip(y, fy)
                theirs = outs[t].at[kk, other]
                _remote(theirs, theirs, d2d_send.at[t, r], d2d_recv.at[t, r], sibling).wait_recv()
        for cp in sends:
            cp.wait_send()
        for cp in copies:
            cp.wait()

    return pl.pallas_call(
        body, name="gather_weights",
        in_specs=[VMEM] * n, out_specs=[HBM] * n,
        out_shape=[pltpu.HBM((N_CHIPS,) + s.shape, MM) for s in shards],
        scratch_shapes=[pltpu.VMEM(s.shape, MM) for s in shards] + [
            pltpu.SemaphoreType.DMA((n,)),
            pltpu.SemaphoreType.DMA((n, 3)), pltpu.SemaphoreType.DMA((n, 3)),
            pltpu.SemaphoreType.DMA((n, 3)), pltpu.SemaphoreType.DMA((n, 3))],
        compiler_params=pltpu.CompilerParams(vmem_limit_bytes=40 * 1024 * 1024),
    )(*shards)


def _add_half(grad, got, where, *, name, tr):
    nch, half, cols = got.shape
    nblk = half // tr

    def body(where_ref, g_ref, r_ref, o_ref):
        o_ref[...] = g_ref[...] + r_ref[...]

    return pl.pallas_call(
        body, name=name,
        grid_spec=pltpu.PrefetchScalarGridSpec(
            num_scalar_prefetch=1, grid=(nch, nblk),
            in_specs=[pl.BlockSpec((1, tr, cols), lambda j, i, where_ref: (j, where_ref[1] * nblk + i, 0)),
                      pl.BlockSpec((1, tr, cols), lambda j, i, where_ref: (j, i, 0))],
            out_specs=pl.BlockSpec((1, tr, cols), lambda j, i, where_ref: (j, i, 0))),
        out_shape=pltpu.HBM(got.shape, F32),
        compiler_params=_cparams(("parallel", "parallel"), 32),
    )(where, grad, got)


def _add_chips(own, got, where, *, name, tr):
    _, half, cols = own.shape
    nblk = half // tr

    def body(where_ref, o_ref, g_ref, out_ref):
        out_ref[...] = ((o_ref[0] + g_ref[0]) + g_ref[1]) + g_ref[2]

    return pl.pallas_call(
        body, name=name,
        grid_spec=pltpu.PrefetchScalarGridSpec(
            num_scalar_prefetch=1, grid=(nblk,),
            in_specs=[pl.BlockSpec((1, tr, cols), lambda i, where_ref: (where_ref[0], i, 0)),
                      pl.BlockSpec((3, tr, cols), lambda i, where_ref: (0, i, 0))],
            out_specs=pl.BlockSpec((tr, cols), lambda i, where_ref: (where_ref[1] * nblk + i, 0))),
        out_shape=pltpu.HBM((2 * half, cols), F32),
        compiler_params=_cparams(("parallel",), 32),
    )(where, own, got)


def _small_job(tiles):
    n = len(tiles)

    def copies(ins, outs, sems):
        x, y, c = _place()
        me = 4 * x + 2 * y + c
        local, send, recv = sems
        cps = []
        for t in range(n):
            cps.append(pltpu.make_async_copy(ins[t], outs[t].at[me], local.at[t]))
            for r in range(1, N_DEV):
                fx, fy, fc = (r >> 2) & 1, (r >> 1) & 1, r & 1
                cps.append(_remote(ins[t], outs[t].at[me], send.at[t, r - 1], recv.at[t, r - 1],
                                   (_flip(x, fx), _flip(y, fy), _flip(c, fc))))
        return cps

    return _Job(tiles, [jax.ShapeDtypeStruct((N_DEV,) + t.shape, F32) for t in tiles],
                [pltpu.SemaphoreType.DMA((n,)), pltpu.SemaphoreType.DMA((n, N_DEV - 1)),
                 pltpu.SemaphoreType.DMA((n, N_DEV - 1))], copies)


def _adamw_math(w, g, m, v):
    m = ADAM_B1 * m + (1.0 - ADAM_B1) * g
    v = ADAM_B2 * v + (1.0 - ADAM_B2) * (g * g)
    m_hat = m / (1.0 - ADAM_B1 ** ADAM_STEP)
    v_hat = v / (1.0 - ADAM_B2 ** ADAM_STEP)
    delta = -ADAM_LR * (m_hat / (jnp.sqrt(v_hat) + ADAM_EPS) + ADAM_WD * w)
    return delta, m, v


def _adamw(w, g, m, v, *, name, tr, jobs=()):
    rows, cols = w.shape

    def body(w_ref, g_ref, m_ref, v_ref, d_ref, nm_ref, nv_ref):
        d_ref[...], nm_ref[...], nv_ref[...] = _adamw_math(w_ref[...], g_ref[...], m_ref[...], v_ref[...])

    spec = pl.BlockSpec((tr, cols), lambda i: (i, 0))
    return _call(
        body, (w, g, m, v), name=name, jobs=jobs,
        grid=(rows // tr,),
        in_specs=[spec] * 4, out_specs=[spec] * 3,
        out_shape=[jax.ShapeDtypeStruct(w.shape, F32)] * 3,
        params=_cparams(("arbitrary",), 32))


def _small_adamw(gathered, gathered_rel, params, moments_m, moments_v):
    n = len(params)

    def body(all_ref, rel_all_ref, *refs):
        w_refs, m_refs, v_refs = refs[:n], refs[n:2 * n], refs[2 * n:3 * n]
        loss_ref = refs[3 * n]
        out_refs = refs[3 * n + 1:]
        g = all_ref[0]
        rel = rel_all_ref[0]
        for d in range(1, N_DEV):
            g = g + all_ref[d]
            rel = rel + rel_all_ref[d]
        misc = g[ROW_MISC:ROW_MISC + 1]
        loss_ref[...] = misc[:, MISC_LOSS:MISC_LOSS + 1]
        grads = (g[ROW_G1:ROW_G1 + 1], g[ROW_G2:ROW_G2 + 1], g[ROW_G3:ROW_G3 + 1], g[ROW_G4:ROW_G4 + 1],
                 misc[:, MISC_GQ:MISC_GQ + HEAD_DIM], misc[:, MISC_GK:MISC_GK + HEAD_DIM],
                 misc[:, MISC_SINK:MISC_SINK + GROUP * N_KV], rel[:, 0:GROUP * N_KV])
        for i in range(n):
            d, nm, nv = _adamw_math(w_refs[i][...], grads[i], m_refs[i][...], v_refs[i][...])
            for j, val in enumerate((grads[i], d, nm, nv)):
                out_refs[4 * i + j][...] = val

    outs = pl.pallas_call(
        body, name="small_adamw",
        in_specs=[VMEM] * (2 + 3 * n), out_specs=[VMEM] * (1 + 4 * n),
        out_shape=[jax.ShapeDtypeStruct((1, 1), F32)] + [jax.ShapeDtypeStruct(p.shape, F32) for p in params
                                                          for _ in range(4)],
    )(gathered, gathered_rel, *params, *moments_m, *moments_v)
    return outs[0], [outs[1 + 4 * i:5 + 4 * i] for i in range(n)]


def kernel(x, w_in, w_o, g_pre_mix, g_post_mix, q_norm_a, k_norm_a, sink_b, rel_bias, g_pre_ffn, w_ffn_up, w_ffn_down, g_post_ffn, loss_target, m_w_in, m_w_o, m_g_pre_mix, m_g_post_mix, m_q_norm_a, m_k_norm_a, m_sink_b, m_rel_bias, m_g_pre_ffn, m_w_ffn_up, m_w_ffn_down, m_g_post_ffn, v_w_in, v_w_o, v_g_pre_mix, v_g_post_mix, v_q_norm_a, v_k_norm_a, v_sink_b, v_rel_bias, v_g_pre_ffn, v_w_ffn_up, v_w_ffn_down, v_g_post_ffn):
    batch, seq, _ = x.shape
    tokens = batch * seq
    where = jnp.stack([2 * lax.axis_index("x") + lax.axis_index("y"), lax.axis_index("c")]).astype(jnp.int32)
    x2 = x.reshape(tokens, D_MODEL)
    g1, g2, g3, g4 = g_pre_mix, g_post_mix, g_pre_ffn, g_post_ffn

    cos, sin = _rope_tables(seq)
    ck, sk = jnp.tile(cos, (1, 2)), jnp.tile(sin, (1, 2))
    cq, sq = ck * SCALE, sk * SCALE
    gq8, gk2 = jnp.tile(q_norm_a, (1, 8)), jnp.tile(k_norm_a, (1, 2))
    bucket, band = _window_tables()
    bias = _bias_build(rel_bias.T, bucket, band)

    w_in_g, w_o_p, w_up_p, w_down_p = _gather_weights(
        (w_in[0], w_o[0], w_ffn_up[0], w_ffn_down[0]), whole=(True, False, False, False))
    (h1, raw, qa, ka, kta, va, vta, qtb, kb, ktb, vb, vtb) = _pre_proj(
        x2, g1, w_in_g, gq8, gk2, cq, sq, ck, sk, seq=seq, tm=min(512, seq), sub=256)
    (oa, p_a, linv_a), (w_part,) = _attn_a_fwd(
        qa, kta, va, seq=seq, bq=min(256, seq), jobs=[_gather_job([w_o_p, w_up_p, w_down_p], forward=False)])
    kb3 = kb.reshape(tokens // BLOCK, BLOCK, KV_WIDTH)
    vb3 = vb.reshape(tokens // BLOCK, BLOCK, KV_WIDTH)
    (ob, lse_b), ((w_o_g, w_up_g, w_down_g),) = _attn_b_fwd(
        qtb, kb3, vtb, bias, sink_b, seq=seq, jobs=[_gather_job(w_part, forward=True)])
    w_o2 = w_o_g.reshape(D_MODEL, D_MODEL)
    w_down2 = w_down_g.reshape(D_FF, D_MODEL)
    mix, x1, h2, o_cat = _wo_post(oa, ob, w_o2, x2, g2, g3, tm=512, sub=256)
    u, df, dy, loss_t, dg4 = _ffn_fwd_loss(h2, w_up_g, w_down2, x1, loss_target.reshape(tokens, D_MODEL), g4, tm=256)

    dz, dx1, dmix, dg3, dg2 = _ffn_bwd_act(df, w_down2, u, w_up_g, x1, dy, mix, g3, g2, tm=256)
    gw_down, _ = _tn_matmul(u, df, name="grad_w_down", tm=1024, tn=1024, tk=min(2048, tokens), square_a=True)
    gw_down = gw_down.reshape(N_CHIPS, FF_CHUNK, D_MODEL)
    gw_up, ((got_down,),) = _tn_matmul(h2, dz, name="grad_w_up", tm=1024, tn=1024, tk=min(2048, tokens), chunked=True,
                                        jobs=[_swap_job([gw_down])])
    doa, dob = _wo_bwd(dmix, w_o2, tm=512)
    gw_o, _ = _tn_matmul(o_cat, dmix, name="grad_w_o", tm=1024, tn=1024, tk=min(2048, tokens))
    gw_o = gw_o.reshape(N_CHIPS, O_CHUNK, D_MODEL)
    sum_down = _add_half(gw_down, got_down, where, name="add_half_w_down", tr=128)
    (dqa, dkta, dvta), ((ex_down,), (got_up,)) = _attn_a_bwd(
        qa, ka, vta, doa, oa, p_a, linv_a, seq=seq, bq=min(256, seq),
        jobs=[_exchange_job([sum_down]), _swap_job([gw_up])])
    full_down = _add_chips(sum_down, ex_down, where, name="add_chips_w_down", tr=128)
    sum_up = _add_half(gw_up, got_up, where, name="add_half_w_up", tr=128)
    (dqb, dkb, dvb, dbias, dsink), ((ex_up,), (g_down,), (got_o,)) = _attn_b_bwd(
        qtb, kb3, ktb, vb3, dob, ob, lse_b, bias, sink_b, seq=seq,
        jobs=[_exchange_job([sum_up]), _join_job([full_down]), _swap_job([gw_o])])
    full_up = _add_chips(sum_up, ex_up, where, name="add_chips_w_up", tr=128)
    sum_o = _add_half(gw_o, got_o, where, name="add_half_w_o", tr=128)
    (grad_x, dproj, dg1, dgq, dgk), _ = _proj_bwd(
        dqa, dkta, dvta, dqb, dkb, dvb, raw, x2, dx1, g1, w_in_g, gq8, gk2, cq, sq, ck, sk,
        seq=seq, tm=min(512, seq), sub=128)
    packed, packed_rel = _pack_small(dg1, dg2, dg3, dg4, dgq, dgk, dsink, dbias, bucket, loss_t)
    gw_in, ((ex_o,), (g_up,), (gathered, gathered_rel)) = _tn_matmul(
        h1, dproj, name="grad_w_in", tm=1024, tn=IN_CHUNK, tk=min(2048, tokens), chunked=True,
        jobs=[_exchange_job([sum_o]), _join_job([full_up]), _small_job([packed, packed_rel])])
    full_o = _add_chips(sum_o, ex_o, where, name="add_chips_w_o", tr=128)

    upd_down, ((g_o,), (got_in,)) = _adamw(
        w_ffn_down[0], g_down, m_w_ffn_down[0], v_w_ffn_down[0], name="adamw_w_down", tr=128,
        jobs=[_join_job([full_o]), _swap_job([gw_in])])
    sum_in = _add_half(gw_in, got_in, where, name="add_half_w_in", tr=128)
    upd_up, ((ex_in,),) = _adamw(w_ffn_up[0], g_up, m_w_ffn_up[0], v_w_ffn_up[0], name="adamw_w_up", tr=128,
                                 jobs=[_exchange_job([sum_in])])
    full_in = _add_chips(sum_in, ex_in, where, name="add_chips_w_in", tr=128)
    upd_o, ((g_in,),) = _adamw(w_o[0], g_o, m_w_o[0], v_w_o[0], name="adamw_w_o", tr=128,
                               jobs=[_join_job([full_in])])
    upd_in, _ = _adamw(w_in[0], g_in, m_w_in[0], v_w_in[0], name="adamw_w_in", tr=128)
    big = [[t[None] for t in (g, *upd)] for g, upd in
           ((g_in, upd_in), (g_o, upd_o), (g_up, upd_up), (g_down, upd_down))]

    loss, small = _small_adamw(
        gathered, gathered_rel,
        (g1, g2, g3, g4, q_norm_a, k_norm_a, sink_b, rel_bias),
        (m_g_pre_mix, m_g_post_mix, m_g_pre_ffn, m_g_post_ffn, m_q_norm_a, m_k_norm_a, m_sink_b, m_rel_bias),
        (v_g_pre_mix, v_g_post_mix, v_g_pre_ffn, v_g_post_ffn, v_q_norm_a, v_k_norm_a, v_sink_b, v_rel_bias))
    s_g1, s_g2, s_g3, s_g4, s_gq, s_gk, s_sink, s_rel = small

    def leaves(i):
        return (big[0][i], big[1][i], s_g1[i], s_g2[i], s_gq[i], s_gk[i], s_sink[i], s_rel[i], s_g3[i],
                big[2][i], big[3][i], s_g4[i])

    loss = loss.reshape(())
    return (loss, grad_x.reshape(batch, seq, D_MODEL), *leaves(0), *leaves(1), *leaves(2), *leaves(3))
```

```python
import functools

import jax
import jax.numpy as jnp
import numpy as np
from jax import lax
from jax.experimental import pallas as pl
from jax.experimental.pallas import tpu as pltpu

F32 = jnp.float32
MM = jnp.bfloat16

D_MODEL = 1024
HEAD_DIM = 64
N_KV = 2
GROUP = 4
Q_WIDTH = 512
KV_WIDTH = 128
D_FF = 4096
GRID_W = 64
BLOCK = 128
N_BUCKETS = 32
MAX_DISTANCE = 128
ROPE_THETA = 10000.0
EPS = 1e-6
NEG_INF = -1e30
SCALE = HEAD_DIM ** -0.5
IN_TOTAL = 1536
N_CHIPS = 4
N_DEV = 8
IN_CHUNK = IN_TOTAL // N_CHIPS
FF_CHUNK = D_FF // N_CHIPS
O_CHUNK = D_MODEL // N_CHIPS
QK_RAW = 640

ADAM_LR = 0.001
ADAM_B1 = 0.9
ADAM_B2 = 0.999
ADAM_EPS = 1e-08
ADAM_WD = 0.01
ADAM_STEP = 10

LANES = 128
MESH = pl.DeviceIdType.MESH
HBM = pl.BlockSpec(memory_space=pl.ANY)
VMEM = pl.BlockSpec(memory_space=pltpu.VMEM)
SMEM = pl.BlockSpec(memory_space=pltpu.SMEM)

ROW_G1, ROW_G2, ROW_G3, ROW_G4, ROW_MISC = 0, 1, 2, 3, 4
MISC_GQ, MISC_GK, MISC_SINK, MISC_LOSS = 0, 64, 128, 512


def _cparams(sem, vmem_mb):
    return pltpu.CompilerParams(dimension_semantics=sem, vmem_limit_bytes=vmem_mb * 1024 * 1024)


def _whole(a):
    return pl.BlockSpec(a.shape, lambda i: (0,) * len(a.shape))


def _in_hbm(s):
    return pltpu.HBM(s.shape, s.dtype)


class _Job:
    def __init__(self, operands, out_shapes, sems, copies, alias=None):
        self.operands, self.out_shapes, self.sems, self.copies = list(operands), list(out_shapes), list(sems), copies
        self.alias = dict(alias or {})


def _place():
    return lax.axis_index("x"), lax.axis_index("y"), lax.axis_index("c")


_CHIP_FLIPS = ((1, 0), (0, 1), (1, 1))


def _flip(v, bit):
    return 1 - v if bit else v


def _remote(src, dst, send, recv, dev):
    return pltpu.make_async_remote_copy(src_ref=src, dst_ref=dst, send_sem=send, recv_sem=recv,
                                        device_id=dev, device_id_type=MESH)


def _swap_job(grads):
    n = len(grads)

    def copies(ins, outs, sems):
        x, y, c = _place()
        send, recv = sems
        cps = []
        for t in range(n):
            half = ins[t].shape[1] // 2
            cps.append(_remote(ins[t].at[:, pl.ds((1 - c) * half, half), :], outs[t], send.at[t], recv.at[t],
                               (x, y, 1 - c)))
        return cps

    shapes = [jax.ShapeDtypeStruct((g.shape[0], g.shape[1] // 2, g.shape[2]), F32) for g in grads]
    return _Job(grads, shapes, [pltpu.SemaphoreType.DMA((n,)), pltpu.SemaphoreType.DMA((n,))], copies)


def _exchange_job(sums):
    n = len(sums)

    def copies(ins, outs, sems):
        x, y, c = _place()
        send, recv = sems
        cps = []
        for t in range(n):
            for r, (fx, fy) in enumerate(_CHIP_FLIPS):
                kk = 2 * _flip(x, fx) + _flip(y, fy)
                cps.append(_remote(ins[t].at[kk], outs[t].at[r], send.at[t, r], recv.at[t, r],
                                   (_flip(x, fx), _flip(y, fy), c)))
        return cps

    shapes = [jax.ShapeDtypeStruct((3,) + s.shape[1:], F32) for s in sums]
    return _Job(sums, shapes, [pltpu.SemaphoreType.DMA((n, 3)), pltpu.SemaphoreType.DMA((n, 3))], copies)


def _join_job(fulls):
    n = len(fulls)

    def copies(ins, outs, sems):
        x, y, c = _place()
        send, recv = sems
        cps = []
        for t in range(n):
            half = ins[t].shape[0] // 2
            rows = pl.ds(c * half, half)
            cps.append(_remote(ins[t].at[rows], outs[t].at[rows], send.at[t], recv.at[t], (x, y, 1 - c)))
        return cps

    shapes = [jax.ShapeDtypeStruct(f.shape, f.dtype) for f in fulls]
    return _Job(fulls, shapes, [pltpu.SemaphoreType.DMA((n,)), pltpu.SemaphoreType.DMA((n,))], copies,
                alias={t: t for t in range(n)})


def _gather_job(bufs, forward):
    n = len(bufs)

    def copies(ins, outs, sems):
        x, y, c = _place()
        send, recv = sems
        cps = []
        for t in range(n):
            half = ins[t].shape[1] // 2
            rows = pl.ds(c * half, half)
            for r, (fx, fy) in enumerate(_CHIP_FLIPS):
                if forward:
                    kk = 2 * _flip(x, fx) + _flip(y, fy)
                    dev = (x, y, 1 - c)
                else:
                    kk = 2 * x + y
                    dev = (_flip(x, fx), _flip(y, fy), c)
                cps.append(_remote(ins[t].at[kk, rows], outs[t].at[kk, rows], send.at[t, r], recv.at[t, r], dev))
        return cps

    shapes = [jax.ShapeDtypeStruct(b.shape, b.dtype) for b in bufs]
    return _Job(bufs, shapes, [pltpu.SemaphoreType.DMA((n, 3)), pltpu.SemaphoreType.DMA((n, 3))], copies,
                alias={t: t for t in range(n)})


def _call(body, args, *, name, grid, in_specs, out_specs, out_shape, scratch_shapes=(), params=None, jobs=()):
    n_in, n_out, n_scr = len(in_specs), len(out_specs), len(scratch_shapes)
    job_in = [len(j.operands) for j in jobs]
    job_out = [len(j.out_shapes) for j in jobs]
    job_sem = [len(j.sems) for j in jobs]

    def wrapped(*refs):
        pos = 0
        ins = refs[pos:pos + n_in]; pos += n_in
        jins = []
        for k in job_in:
            jins.append(refs[pos:pos + k]); pos += k
        outs = refs[pos:pos + n_out]; pos += n_out
        jouts = []
        for k in job_out:
            jouts.append(refs[pos:pos + k]); pos += k
        scr = refs[pos:pos + n_scr]; pos += n_scr
        jsems = []
        for k in job_sem:
            jsems.append(refs[pos:pos + k]); pos += k
        if jobs:
            ids = [pl.program_id(d) for d in range(len(grid))]
            first = functools.reduce(jnp.logical_and, [i == 0 for i in ids])
            last = functools.reduce(jnp.logical_and, [i == g - 1 for i, g in zip(ids, grid)])

            @pl.when(first)
            def _():
                for j, ji, jo, js in zip(jobs, jins, jouts, jsems):
                    for cp in j.copies(ji, jo, js):
                        cp.start()

        body(*ins, *outs, *scr)
        if jobs:
            @pl.when(last)
            def _():
                for j, ji, jo, js in zip(jobs, jins, jouts, jsems):
                    for cp in j.copies(ji, jo, js):
                        cp.wait()

    aliases = {}
    in_pos, out_pos = n_in, n_out
    for j in jobs:
        for i, o in j.alias.items():
            aliases[in_pos + i] = out_pos + o
        in_pos += len(j.operands)
        out_pos += len(j.out_shapes)
    res = pl.pallas_call(
        wrapped, name=name, grid=grid,
        in_specs=list(in_specs) + [HBM] * sum(job_in),
        out_specs=list(out_specs) + [HBM] * sum(job_out),
        out_shape=list(out_shape) + [_in_hbm(s) for j in jobs for s in j.out_shapes],
        scratch_shapes=list(scratch_shapes) + [s for j in jobs for s in j.sems],
        input_output_aliases=aliases,
        compiler_params=params,
    )(*args, *[a for j in jobs for a in j.operands])
    own, rest = list(res[:n_out]), list(res[n_out:])
    job_res = []
    for k in job_out:
        job_res.append(rest[:k])
        rest = rest[k:]
    return own, job_res


def _dot(a, b):
    return jnp.dot(a, b, preferred_element_type=F32)


def _dot_nt(a, b):
    return lax.dot_general(a, b, (((1,), (1,)), ((), ())), preferred_element_type=F32)


def _dot_tn(a, b):
    return lax.dot_general(a, b, (((0,), (0,)), ((), ())), preferred_element_type=F32)


def _rms_r(x):
    return lax.rsqrt(jnp.mean(x * x, axis=-1, keepdims=True) + EPS)


def _rms_bwd(x, r, g, dy):
    n = x * r
    dn = dy * g
    dx = r * (dn - n * jnp.mean(dn * n, axis=-1, keepdims=True))
    return dx, dy * n


def _seg64_sum(v):
    rows, width = v.shape
    lane = lax.broadcasted_iota(jnp.int32, (rows, LANES), 1)
    lo = lane < HEAD_DIM
    outs = []
    for c in range(width // LANES):
        ch = v[:, c * LANES:(c + 1) * LANES]
        s_lo = jnp.sum(jnp.where(lo, ch, 0.0), axis=-1, keepdims=True)
        s_hi = jnp.sum(jnp.where(lo, 0.0, ch), axis=-1, keepdims=True)
        outs.append(jnp.where(lo, s_lo, s_hi))
    return outs[0] if len(outs) == 1 else jnp.concatenate(outs, axis=-1)


def _head_r(v):
    return lax.rsqrt(_seg64_sum(v * v) * (1.0 / HEAD_DIM) + EPS)


def _swap16(ch):
    lane = lax.broadcasted_iota(jnp.int32, ch.shape, 1)
    return jnp.where((lane % 32) < 16, pltpu.roll(ch, LANES - 16, 1), pltpu.roll(ch, 16, 1))


def _by_chunk(fn, v):
    outs = [fn(v[:, c * LANES:(c + 1) * LANES]) for c in range(v.shape[1] // LANES)]
    return outs[0] if len(outs) == 1 else jnp.concatenate(outs, axis=-1)


def _rope(v, cos, sin_signed):
    return _by_chunk(lambda ch: ch * cos + _swap16(ch) * sin_signed, v)


def _rope_t(g, cos, sin_signed):
    return _by_chunk(lambda ch: ch * cos + _swap16(ch * sin_signed), g)


def _rope_tables(seq):
    nf = HEAD_DIM // 4
    freqs = ROPE_THETA ** (-jnp.arange(nf, dtype=F32) / nf)
    pos = jnp.arange(seq, dtype=jnp.int32)
    row = (pos // GRID_W).astype(F32)
    col = (pos % GRID_W).astype(F32)
    ang_r = row[:, None] * freqs[None, :]
    ang_c = col[:, None] * freqs[None, :]
    cr, sr, cc, sc = jnp.cos(ang_r), jnp.sin(ang_r), jnp.cos(ang_c), jnp.sin(ang_c)
    cos = jnp.concatenate([cr, cr, cc, cc], axis=1)
    sin = jnp.concatenate([-sr, sr, -sc, sc], axis=1)
    return cos, sin


def _t5_bucket(rel):
    nb = N_BUCKETS // 2
    ret = (rel > 0).astype(jnp.int32) * nb
    n = jnp.abs(rel)
    max_exact = nb // 2
    nf = jnp.maximum(n, 1).astype(jnp.float32)
    large = max_exact + (jnp.log(nf / max_exact) / np.float32(np.log(MAX_DISTANCE / max_exact))
                         * (nb - max_exact)).astype(jnp.int32)
    large = jnp.minimum(large, nb - 1)
    return ret + jnp.where(n < max_exact, n, large)


def _window_tables():
    a = jnp.arange(BLOCK, dtype=jnp.int32)
    c = jnp.arange(3 * BLOCK, dtype=jnp.int32)
    rel = c[None, :] - BLOCK - a[:, None]
    bucket = _t5_bucket(rel)
    band = (jnp.abs(rel) <= BLOCK).astype(jnp.int32)
    to3 = lambda t: t.reshape(BLOCK, 3, BLOCK).transpose(1, 2, 0)
    return to3(bucket), to3(band)


def _pre_proj(x, g1, w_in, gq, gk, cq, sq, ck, sk, *, seq, tm, sub):
    tokens = x.shape[0]
    n_seq = seq // tm
    nblk = tm // BLOCK
    batch = tokens // seq

    def body(x_ref, g1_ref, w_ref, gq_ref, gk_ref, cq_ref, sq_ref, ck_ref, sk_ref,
             h1_ref, raw_ref, qa_ref, ka_ref, kta_ref, va_ref, vta_ref,
             qtb_ref, kb_ref, ktb_ref, vb_ref, vtb_ref, proj):
        for r in range(tm // sub):
            rows = slice(r * sub, (r + 1) * sub)
            xv = x_ref[rows, :]
            h = (xv * _rms_r(xv) * g1_ref[...]).astype(MM)
            h1_ref[rows, :] = h
            for j in range(N_CHIPS):
                proj[rows, j * IN_CHUNK:(j + 1) * IN_CHUNK] = _dot(h, w_ref[j])
            qa = proj[rows, 0:Q_WIDTH]
            ka = proj[rows, Q_WIDTH:QK_RAW]
            raw_ref[rows, :] = proj[rows, 0:QK_RAW]
            qn = qa * _head_r(qa) * gq_ref[...]
            qa_ref[rows, :] = _rope(qn, cq_ref[rows, :], sq_ref[rows, :]).astype(MM)
            kn = ka * _head_r(ka) * gk_ref[...]
            kr = _rope(kn, ck_ref[rows, :], sk_ref[rows, :])
            ka_ref[rows, :] = kr.astype(MM)
            kta_ref[0, :, rows] = kr.T.astype(MM)
            va = proj[rows, 640:768]
            va_ref[rows, :] = va.astype(MM)
            vta_ref[0, :, rows] = va.T.astype(MM)
            qb = proj[rows, 768:1280] * SCALE
            kb = proj[rows, 1280:1408]
            vb = proj[rows, 1408:1536]
            kb_ref[rows, :] = kb.astype(MM)
            vb_ref[rows, :] = vb.astype(MM)
            for j in range(sub // BLOCK):
                blk = slice(j * BLOCK, (j + 1) * BLOCK)
                qtb_ref[r * (sub // BLOCK) + j] = qb[blk, :].T.astype(MM)
                ktb_ref[r * (sub // BLOCK) + j] = kb[blk, :].T.astype(MM)
                vtb_ref[r * (sub // BLOCK) + j] = vb[blk, :].T.astype(MM)

    tok = lambda w: pl.BlockSpec((tm, w), lambda i: (i, 0))
    tab = lambda w: pl.BlockSpec((tm, w), lambda i: (i % n_seq, 0))
    row = lambda w: pl.BlockSpec((1, w), lambda i: (0, 0))
    tposed = pl.BlockSpec((1, LANES, tm), lambda i: (i // n_seq, 0, i % n_seq))
    blocks = pl.BlockSpec((nblk, BLOCK, LANES), lambda i: (i, 0, 0))
    qblocks = pl.BlockSpec((nblk, Q_WIDTH, BLOCK), lambda i: (i, 0, 0))
    tok_mm = lambda w: jax.ShapeDtypeStruct((tokens, w), MM)
    return pl.pallas_call(
        body, name="pre_proj",
        grid=(tokens // tm,),
        in_specs=[tok(D_MODEL), row(D_MODEL),
                  pl.BlockSpec((N_CHIPS, D_MODEL, IN_CHUNK), lambda i: (0, 0, 0)),
                  row(Q_WIDTH), row(KV_WIDTH), tab(KV_WIDTH), tab(KV_WIDTH), tab(KV_WIDTH), tab(KV_WIDTH)],
        out_specs=[tok(D_MODEL), tok(QK_RAW), tok(Q_WIDTH), tok(KV_WIDTH), tposed, tok(KV_WIDTH), tposed,
                   qblocks, tok(KV_WIDTH), blocks, tok(KV_WIDTH), blocks],
        out_shape=[
            tok_mm(D_MODEL),
            jax.ShapeDtypeStruct((tokens, QK_RAW), F32),
            tok_mm(Q_WIDTH),
            tok_mm(KV_WIDTH),
            jax.ShapeDtypeStruct((batch, KV_WIDTH, seq), MM),
            tok_mm(KV_WIDTH),
            jax.ShapeDtypeStruct((batch, KV_WIDTH, seq), MM),
            jax.ShapeDtypeStruct((tokens // BLOCK, Q_WIDTH, BLOCK), MM),
            tok_mm(KV_WIDTH),
            jax.ShapeDtypeStruct((tokens // BLOCK, KV_WIDTH, BLOCK), MM),
            tok_mm(KV_WIDTH),
            jax.ShapeDtypeStruct((tokens // BLOCK, KV_WIDTH, BLOCK), MM),
        ],
        scratch_shapes=[pltpu.VMEM((tm, IN_TOTAL), F32)],
        compiler_params=_cparams(("parallel",), 48),
    )(x, g1, w_in, gq, gk, cq, sq, ck, sk)


def _kv_half(v2, kv):
    return jnp.where(kv == 0, v2[:, :HEAD_DIM], v2[:, HEAD_DIM:])


def _attn_a_fwd(qa, kta, va, *, seq, bq, jobs=()):
    tokens = qa.shape[0]
    batch = tokens // seq
    nq = seq // bq

    def body(q_ref, kt_ref, v_ref, o_ref, p_ref, linv_ref):
        kv = pl.program_id(1)
        kt = kt_ref[0]
        lane = lax.broadcasted_iota(jnp.int32, (seq, KV_WIDTH), 1)
        v = jnp.where((lane < HEAD_DIM) == (kv == 0), v_ref[...], jnp.ones((), MM))
        for g in range(GROUP):
            sl = slice(g * HEAD_DIM, (g + 1) * HEAD_DIM)
            s = _dot(q_ref[:, sl], kt)
            pb = jnp.exp((s - jnp.max(s, axis=-1, keepdims=True)).astype(MM))
            p_ref[0, g] = pb
            o2 = _dot(pb, v)
            linv = 1.0 / _kv_half(o2, 1 - kv)[:, 0:1]
            o_ref[:, sl] = _kv_half(o2, kv) * linv
            linv_ref[0, :, g:g + 1] = linv

    return _call(
        body, (qa, kta, va), name="attn_a_fwd", jobs=jobs,
        grid=(batch, N_KV, nq),
        in_specs=[pl.BlockSpec((bq, GROUP * HEAD_DIM), lambda b, k, i: (b * nq + i, k)),
                  pl.BlockSpec((1, HEAD_DIM, seq), lambda b, k, i: (b, k, 0)),
                  pl.BlockSpec((seq, KV_WIDTH), lambda b, k, i: (b, 0))],
        out_specs=[pl.BlockSpec((bq, GROUP * HEAD_DIM), lambda b, k, i: (b * nq + i, k)),
                   pl.BlockSpec((1, GROUP, bq, seq), lambda b, k, i: (k, 0, b * nq + i, 0)),
                   pl.BlockSpec((1, bq, GROUP), lambda b, k, i: (k, b * nq + i, 0))],
        out_shape=[jax.ShapeDtypeStruct((tokens, Q_WIDTH), F32),
                   jax.ShapeDtypeStruct((N_KV, GROUP, tokens, seq), MM),
                   jax.ShapeDtypeStruct((N_KV, tokens, GROUP), F32)],
        params=_cparams(("arbitrary", "arbitrary", "arbitrary"), 56))


def _attn_a_bwd(qa, ka, vta, do, o, p, linv, *, seq, bq, jobs=()):
    tokens = qa.shape[0]
    batch = tokens // seq
    nq = seq // bq

    def body(q_ref, k_ref, vt_ref, do_ref, o_ref, p_ref, linv_ref, dq_ref, dkt_ref, dvt_ref):
        kv = pl.program_id(1)

        @pl.when(pl.program_id(2) == 0)
        def _():
            dkt_ref[...] = jnp.zeros_like(dkt_ref)
            dvt_ref[...] = jnp.zeros_like(dvt_ref)

        vt = vt_ref[0]
        k2 = k_ref[...]
        for g in range(GROUP):
            sl = slice(g * HEAD_DIM, (g + 1) * HEAD_DIM)
            dof = do_ref[:, sl]
            delta = jnp.sum(dof * o_ref[:, sl], axis=-1, keepdims=True)
            linv_g = linv_ref[0, :, g:g + 1]
            pb = p_ref[0, g]
            dp = _dot(dof.astype(MM), vt)
            ds = pb * ((dp - delta) * linv_g).astype(MM)
            dq_ref[:, sl] = _kv_half(_dot(ds, k2), kv)
            dkt_ref[0] += _dot_tn(q_ref[:, sl], ds)
            dvt_ref[0] += _dot_tn((dof * linv_g).astype(MM), pb)

    qspec = pl.BlockSpec((bq, GROUP * HEAD_DIM), lambda b, k, i: (b * nq + i, k))
    tspec = pl.BlockSpec((1, HEAD_DIM, seq), lambda b, k, i: (b, k, 0))
    return _call(
        body, (qa, ka, vta, do, o, p, linv), name="attn_a_bwd", jobs=jobs,
        grid=(batch, N_KV, nq),
        in_specs=[qspec, pl.BlockSpec((seq, KV_WIDTH), lambda b, k, i: (b, 0)), tspec, qspec, qspec,
                  pl.BlockSpec((1, GROUP, bq, seq), lambda b, k, i: (k, 0, b * nq + i, 0)),
                  pl.BlockSpec((1, bq, GROUP), lambda b, k, i: (k, b * nq + i, 0))],
        out_specs=[qspec, tspec, tspec],
        out_shape=[jax.ShapeDtypeStruct((tokens, Q_WIDTH), F32),
                   jax.ShapeDtypeStruct((batch, KV_WIDTH, seq), F32),
                   jax.ShapeDtypeStruct((batch, KV_WIDTH, seq), F32)],
        params=_cparams(("arbitrary", "arbitrary", "arbitrary"), 56))


def _bias_build(rel_bias_t, bucket_t, band_t):
    def body(tab_ref, bucket_ref, band_ref, bias_ref):
        for h in range(GROUP * N_KV):
            for piece in range(3):
                bk = bucket_ref[piece]
                acc = jnp.zeros((BLOCK, BLOCK), F32)
                for b in range(N_BUCKETS):
                    acc = jnp.where(bk == b, tab_ref[h, b], acc)
                g = h % GROUP
                bias_ref[h // GROUP, piece, :, g * BLOCK:(g + 1) * BLOCK] = jnp.where(band_ref[piece] != 0, acc, NEG_INF)

    out = jax.ShapeDtypeStruct((N_KV, 3, BLOCK, GROUP * BLOCK), F32)
    return pl.pallas_call(
        body, name="bias_build", grid=(1,),
        in_specs=[SMEM, _whole(bucket_t), _whole(band_t)], out_specs=_whole(out), out_shape=out,
    )(rel_bias_t, bucket_t, band_t)


def _pad_heads(t, kv):
    outs = []
    for g in range(GROUP):
        tg = t[g * HEAD_DIM:(g + 1) * HEAD_DIM, :]
        zero = jnp.zeros_like(tg)
        outs.append(jnp.concatenate([jnp.where(kv == 0, tg, zero), jnp.where(kv == 0, zero, tg)], axis=0))
    return jnp.concatenate(outs, axis=-1)


def _unpad_heads(t, kv):
    outs = [_kv_half(t[:, g * BLOCK:(g + 1) * BLOCK].T, kv) for g in range(GROUP)]
    return jnp.concatenate(outs, axis=-1)


def _sink_row(sink_ref, kv):
    lane_head = lax.broadcasted_iota(jnp.int32, (1, GROUP * BLOCK), 1) // BLOCK
    row = jnp.zeros((1, GROUP * BLOCK), F32)
    for g in range(GROUP):
        row = jnp.where(lane_head == g, sink_ref[0, kv * GROUP + g], row)
    return row


def _window_scores_t(k_ref, idx, qpad, bias_ref, n, nblk):
    pieces = []
    for piece in range(3):
        s = _dot(k_ref[idx[piece]], qpad) + bias_ref[0, piece]
        if piece == 0:
            s = jnp.where(n > 0, s, NEG_INF)
        if piece == 2:
            s = jnp.where(n < nblk - 1, s, NEG_INF)
        pieces.append(s)
    return pieces


def _attn_b_fwd(qtb, kb3, vtb, bias, sink, *, seq, jobs=()):
    nblk_all = qtb.shape[0]
    tokens = nblk_all * BLOCK
    batch = tokens // seq
    nblk = seq // BLOCK

    def body(sink_ref, q_ref, k_ref, vt_ref, bias_ref, o_ref, lse_ref):
        kv = pl.program_id(0)
        sink_row = _sink_row(sink_ref, kv)

        def block(n, carry):
            idx = (jnp.maximum(n - 1, 0), n, jnp.minimum(n + 1, nblk - 1))
            rows = pl.ds(pl.multiple_of(n * BLOCK, BLOCK), BLOCK)
            qpad = _pad_heads(q_ref[n], kv)
            ss = _window_scores_t(k_ref, idx, qpad, bias_ref, n, nblk)
            m = jnp.maximum(jnp.maximum(jnp.max(ss[0], axis=0, keepdims=True),
                                        jnp.max(ss[1], axis=0, keepdims=True)),
                            jnp.maximum(jnp.max(ss[2], axis=0, keepdims=True), sink_row))
            ps = [jnp.exp(s - m) for s in ss]
            denom = (jnp.sum(ps[0], axis=0, keepdims=True) + jnp.sum(ps[1], axis=0, keepdims=True)
                     + jnp.sum(ps[2], axis=0, keepdims=True) + jnp.exp(sink_row - m))
            ot = (_dot(vt_ref[idx[0]], ps[0].astype(MM)) + _dot(vt_ref[idx[1]], ps[1].astype(MM))
                  + _dot(vt_ref[idx[2]], ps[2].astype(MM)))
            o_ref[rows, :] = _unpad_heads(ot * (1.0 / denom), kv)
            lse_ref[0, n] = jnp.broadcast_to(m + jnp.log(denom), (8, GROUP * BLOCK))
            return carry

        lax.fori_loop(0, nblk, block, 0, unroll=8)

    both = pl.BlockSpec((nblk, BLOCK, KV_WIDTH), lambda k, b: (b, 0, 0))
    return _call(
        body, (sink, qtb, kb3, vtb, bias), name="attn_b_fwd", jobs=jobs,
        grid=(N_KV, batch),
        in_specs=[SMEM, pl.BlockSpec((nblk, GROUP * HEAD_DIM, BLOCK), lambda k, b: (b, k, 0)), both, both,
                  pl.BlockSpec((1, 3, BLOCK, GROUP * BLOCK), lambda k, b: (k, 0, 0, 0))],
        out_specs=[pl.BlockSpec((seq, GROUP * HEAD_DIM), lambda k, b: (b, k)),
                   pl.BlockSpec((1, nblk, 8, GROUP * BLOCK), lambda k, b: (k, b, 0, 0))],
        out_shape=[jax.ShapeDtypeStruct((tokens, Q_WIDTH), F32),
                   jax.ShapeDtypeStruct((N_KV, nblk_all, 8, GROUP * BLOCK), F32)],
        params=_cparams(("arbitrary", "arbitrary"), 48))


def _attn_b_bwd(qtb, kb3, ktb, vb3, do, o, lse, bias, sink, *, seq, jobs=()):
    nblk_all = qtb.shape[0]
    tokens = nblk_all * BLOCK
    batch = tokens // seq
    nblk = seq // BLOCK

    def body(sink_ref, q_ref, k_ref, kt_ref, v_ref, do_ref, o_ref, lse_ref, bias_ref,
             dq_ref, dk_ref, dv_ref, dbias_ref, dsink_ref):
        kv = pl.program_id(0)
        sink_row = _sink_row(sink_ref, kv)

        @pl.when(pl.program_id(1) == 0)
        def _():
            dbias_ref[...] = jnp.zeros_like(dbias_ref)
            dsink_ref[...] = jnp.zeros_like(dsink_ref)

        dk_ref[...] = jnp.zeros_like(dk_ref)
        dv_ref[...] = jnp.zeros_like(dv_ref)

        def block(n, dsink):
            idx = (jnp.maximum(n - 1, 0), n, jnp.minimum(n + 1, nblk - 1))
            rows = pl.ds(pl.multiple_of(n * BLOCK, BLOCK), BLOCK)
            qpad = _pad_heads(q_ref[n], kv)
            dot_t = do_ref[rows, :].T
            prod = dot_t * o_ref[rows, :].T
            delta = jnp.concatenate(
                [jnp.sum(prod[g * HEAD_DIM:(g + 1) * HEAD_DIM, :], axis=0, keepdims=True) for g in range(GROUP)],
                axis=-1)
            dopad = _pad_heads(dot_t.astype(MM), kv)
            lse_row = lse_ref[0, n][0:1, :]
            ss = _window_scores_t(k_ref, idx, qpad, bias_ref, n, nblk)
            dqt = jnp.zeros((KV_WIDTH, GROUP * BLOCK), F32)
            for piece in range(3):
                pt = jnp.exp(ss[piece] - lse_row)
                dst = pt * (_dot(v_ref[idx[piece]], dopad) - delta)
                dsb = dst.astype(MM)
                dbias_ref[0, piece] += dst
                dqt = dqt + _dot(kt_ref[idx[piece]], dsb)
                dk_ref[0, idx[piece]] += _dot_nt(dsb, qpad)
                dv_ref[0, idx[piece]] += _dot_nt(pt.astype(MM), dopad)
            dq_ref[rows, :] = _unpad_heads(dqt, kv)
            return dsink - jnp.exp(sink_row - lse_row) * delta

        dsink = lax.fori_loop(
            0, nblk // 4, lambda i, c: block(4 * i + 3, block(4 * i + 2, block(4 * i + 1, block(4 * i, c)))),
            jnp.zeros((1, GROUP * BLOCK), F32))
        dsink_ref[0] += jnp.broadcast_to(dsink, (8, GROUP * BLOCK))

    qspec = pl.BlockSpec((seq, GROUP * HEAD_DIM), lambda k, b: (b, k))
    both = pl.BlockSpec((nblk, BLOCK, KV_WIDTH), lambda k, b: (b, 0, 0))
    grad = pl.BlockSpec((1, nblk, BLOCK, KV_WIDTH), lambda k, b: (k, b, 0, 0))
    return _call(
        body, (sink, qtb, kb3, ktb, vb3, do, o, lse, bias), name="attn_b_bwd", jobs=jobs,
        grid=(N_KV, batch),
        in_specs=[SMEM, pl.BlockSpec((nblk, GROUP * HEAD_DIM, BLOCK), lambda k, b: (b, k, 0)), both, both, both,
                  qspec, qspec, pl.BlockSpec((1, nblk, 8, GROUP * BLOCK), lambda k, b: (k, b, 0, 0)),
                  pl.BlockSpec((1, 3, BLOCK, GROUP * BLOCK), lambda k, b: (k, 0, 0, 0))],
        out_specs=[qspec, grad, grad,
                   pl.BlockSpec((1, 3, BLOCK, GROUP * BLOCK), lambda k, b: (k, 0, 0, 0)),
                   pl.BlockSpec((1, 8, GROUP * BLOCK), lambda k, b: (k, 0, 0))],
        out_shape=[jax.ShapeDtypeStruct((tokens, Q_WIDTH), F32),
                   jax.ShapeDtypeStruct((N_KV, nblk_all, BLOCK, KV_WIDTH), F32),
                   jax.ShapeDtypeStruct((N_KV, nblk_all, BLOCK, KV_WIDTH), F32),
                   jax.ShapeDtypeStruct((N_KV, 3, BLOCK, GROUP * BLOCK), F32),
                   jax.ShapeDtypeStruct((N_KV, 8, GROUP * BLOCK), F32)],
        params=_cparams(("arbitrary", "arbitrary"), 48))


def _wo_post(oa, ob, w_o, x, g2, g3, *, tm, sub):
    tokens = x.shape[0]

    def body(oa_ref, ob_ref, w_ref, x_ref, g2_ref, g3_ref, mix_ref, x1_ref, h2_ref, o_ref):
        for r in range(tm // sub):
            rows = slice(r * sub, (r + 1) * sub)
            o = jnp.concatenate([oa_ref[rows, :].astype(MM), ob_ref[rows, :].astype(MM)], axis=-1)
            o_ref[rows, :] = o
            mix = _dot(o, w_ref[...])
            mix_ref[rows, :] = mix
            x1 = x_ref[rows, :] + mix * _rms_r(mix) * g2_ref[...]
            x1_ref[rows, :] = x1
            h2_ref[rows, :] = (x1 * _rms_r(x1) * g3_ref[...]).astype(MM)

    tok = lambda w: pl.BlockSpec((tm, w), lambda i: (i, 0))
    row = pl.BlockSpec((1, D_MODEL), lambda i: (0, 0))
    return pl.pallas_call(
        body, name="wo_post",
        grid=(tokens // tm,),
        in_specs=[tok(Q_WIDTH), tok(Q_WIDTH), pl.BlockSpec((D_MODEL, D_MODEL), lambda i: (0, 0)),
                  tok(D_MODEL), row, row],
        out_specs=[tok(D_MODEL), tok(D_MODEL), tok(D_MODEL), tok(D_MODEL)],
        out_shape=[jax.ShapeDtypeStruct((tokens, D_MODEL), F32),
                   jax.ShapeDtypeStruct((tokens, D_MODEL), F32),
                   jax.ShapeDtypeStruct((tokens, D_MODEL), MM),
                   jax.ShapeDtypeStruct((tokens, D_MODEL), MM)],
        compiler_params=_cparams(("parallel",), 40),
    )(oa, ob, w_o, x, g2, g3)


def _resident(shape):
    return pl.BlockSpec(shape, lambda i: (0,) * len(shape), pipeline_mode=pl.Buffered(1))


def _ffn_fwd_loss(h2, w_up, w_down, x1, target, g4, *, tm):
    tokens = h2.shape[0]
    nt = tokens // tm

    def body(h2_ref, wu_ref, wd_ref, x1_ref, t_ref, g4_ref, u_ref, df_ref, dy_ref, loss_ref, dg4_ref):
        h2v = h2_ref[...]
        f = jnp.zeros((tm, D_MODEL), F32)
        for c in range(N_CHIPS):
            u = jnp.maximum(_dot(h2v, wu_ref[c]), 0.0)
            u_ref[:, c * FF_CHUNK:(c + 1) * FF_CHUNK] = u.astype(MM)
            f = f + _dot((u * u).astype(MM), wd_ref[c * FF_CHUNK:(c + 1) * FF_CHUNK, :])
        r = _rms_r(f)
        g4v = g4_ref[...]
        err = x1_ref[...] + f * r * g4v - t_ref[...]
        sq = jnp.sum(err * err, axis=-1, keepdims=True)
        loss_ref[0] = jnp.broadcast_to(jnp.sum(sq, axis=0, keepdims=True) * (0.5 / D_MODEL), (8, LANES))
        dy = err * (1.0 / D_MODEL)
        dy_ref[...] = dy
        dfv, dgv = _rms_bwd(f, r, g4v, dy)
        df_ref[...] = dfv.astype(MM)
        dg4_ref[0] = jnp.sum(dgv, axis=0, keepdims=True)

    tok = pl.BlockSpec((tm, D_MODEL), lambda i: (i, 0))
    return pl.pallas_call(
        body, name="ffn_fwd_loss",
        grid=(nt,),
        in_specs=[tok, _resident((N_CHIPS, D_MODEL, FF_CHUNK)), _resident((D_FF, D_MODEL)),
                  tok, tok, pl.BlockSpec((1, D_MODEL), lambda i: (0, 0))],
        out_specs=[pl.BlockSpec((tm, D_FF), lambda i: (i, 0)), tok, tok,
                   pl.BlockSpec((1, 8, LANES), lambda i: (i, 0, 0)),
                   pl.BlockSpec((1, 1, D_MODEL), lambda i: (i, 0, 0))],
        out_shape=[jax.ShapeDtypeStruct((tokens, D_FF), MM),
                   jax.ShapeDtypeStruct((tokens, D_MODEL), MM),
                   jax.ShapeDtypeStruct((tokens, D_MODEL), F32),
                   jax.ShapeDtypeStruct((nt, 8, LANES), F32),
                   jax.ShapeDtypeStruct((nt, 1, D_MODEL), F32)],
        compiler_params=_cparams(("parallel",), 56),
    )(h2, w_up, w_down, x1, target, g4)


def _ffn_bwd_act(df, w_down, u, w_up, x1, dy, mix, g3, g2, *, tm):
    tokens = df.shape[0]
    nt = tokens // tm

    def body(df_ref, wd_ref, u_ref, wu_ref, x1_ref, dy_ref, mix_ref, g3_ref, g2_ref,
             dz_ref, dx1_ref, dmix_ref, dg3_ref, dg2_ref):
        dfv = df_ref[...]
        dh2 = jnp.zeros((tm, D_MODEL), F32)
        for c in range(N_CHIPS):
            cols = slice(c * FF_CHUNK, (c + 1) * FF_CHUNK)
            da = _dot_nt(dfv, wd_ref[cols, :])
            dz = (da * (2.0 * u_ref[:, cols].astype(F32))).astype(MM)
            dz_ref[:, cols] = dz
            dh2 = dh2 + _dot_nt(dz, wu_ref[c])
        x1 = x1_ref[...]
        dxn, dg3v = _rms_bwd(x1, _rms_r(x1), g3_ref[...], dh2)
        dx1 = dy_ref[...] + dxn
        dx1_ref[...] = dx1
        dg3_ref[0] = jnp.sum(dg3v, axis=0, keepdims=True)
        mix = mix_ref[...]
        dmix, dg2v = _rms_bwd(mix, _rms_r(mix), g2_ref[...], dx1)
        dmix_ref[...] = dmix.astype(MM)
        dg2_ref[0] = jnp.sum(dg2v, axis=0, keepdims=True)

    tok = pl.BlockSpec((tm, D_MODEL), lambda i: (i, 0))
    wide = pl.BlockSpec((tm, D_FF), lambda i: (i, 0))
    row = pl.BlockSpec((1, D_MODEL), lambda i: (0, 0))
    part = pl.BlockSpec((1, 1, D_MODEL), lambda i: (i, 0, 0))
    return pl.pallas_call(
        body, name="ffn_bwd_act",
        grid=(nt,),
        in_specs=[tok, _resident((D_FF, D_MODEL)), wide, _resident((N_CHIPS, D_MODEL, FF_CHUNK)),
                  tok, tok, tok, row, row],
        out_specs=[wide, tok, tok, part, part],
        out_shape=[jax.ShapeDtypeStruct((tokens, D_FF), MM),
                   jax.ShapeDtypeStruct((tokens, D_MODEL), F32),
                   jax.ShapeDtypeStruct((tokens, D_MODEL), MM),
                   jax.ShapeDtypeStruct((nt, 1, D_MODEL), F32),
                   jax.ShapeDtypeStruct((nt, 1, D_MODEL), F32)],
        compiler_params=_cparams(("parallel",), 56),
    )(df, w_down, u, w_up, x1, dy, mix, g3, g2)


def _tn_matmul(a, b, *, name, tm, tn, tk, chunked=False, square_a=False, vmem_mb=48, jobs=()):
    tokens, m_dim = a.shape
    n_dim = b.shape[1]
    if chunked:
        assert tm == m_dim

    def body(a_ref, b_ref, o_ref):
        av = a_ref[...]
        if square_a:
            av = av.astype(F32)
            av = av * av
        part = _dot_tn(av.astype(MM), b_ref[...].astype(MM))
        part = part[None] if chunked else part

        @pl.when(pl.program_id(2) == 0)
        def _():
            o_ref[...] = part

        @pl.when(pl.program_id(2) > 0)
        def _():
            o_ref[...] += part

    if chunked:
        out_spec = pl.BlockSpec((1, tm, tn), lambda i, j, k: (j, 0, 0))
        out_shape = jax.ShapeDtypeStruct((n_dim // tn, m_dim, tn), F32)
    else:
        out_spec = pl.BlockSpec((tm, tn), lambda i, j, k: (i, j))
        out_shape = jax.ShapeDtypeStruct((m_dim, n_dim), F32)
    (out,), job_res = _call(
        body, (a, b), name=name, jobs=jobs,
        grid=(m_dim // tm, n_dim // tn, tokens // tk),
        in_specs=[pl.BlockSpec((tk, tm), lambda i, j, k: (k, i)),
                  pl.BlockSpec((tk, tn), lambda i, j, k: (k, j))],
        out_specs=[out_spec], out_shape=[_in_hbm(out_shape)],
        params=_cparams(("arbitrary", "arbitrary", "arbitrary"), vmem_mb))
    return out, job_res


def _wo_bwd(dmix, w_o, *, tm):
    tokens = dmix.shape[0]

    def body(dm_ref, w_ref, doa_ref, dob_ref):
        dm = dm_ref[...]
        doa_ref[...] = _dot_nt(dm, w_ref[0:Q_WIDTH, :])
        dob_ref[...] = _dot_nt(dm, w_ref[Q_WIDTH:D_MODEL, :])

    tok = lambda w: pl.BlockSpec((tm, w), lambda i: (i, 0))
    return pl.pallas_call(
        body, name="wo_bwd",
        grid=(tokens // tm,),
        in_specs=[tok(D_MODEL), pl.BlockSpec((D_MODEL, D_MODEL), lambda i: (0, 0))],
        out_specs=[tok(Q_WIDTH), tok(Q_WIDTH)],
        out_shape=[jax.ShapeDtypeStruct((tokens, Q_WIDTH), F32)] * 2,
        compiler_params=_cparams(("parallel",), 40),
    )(dmix, w_o)


def _proj_bwd(dqa, dkta, dvta, dqb, dktb, dvtb, raw, x, dx1, g1, w_in, gq, gk, cq, sq, ck, sk, *, seq, tm, sub,
              jobs=()):
    tokens = x.shape[0]
    nt = tokens // tm
    n_seq = seq // tm
    nblk = tm // BLOCK

    def body(dqa_ref, dkta_ref, dvta_ref, dqb_ref, dkb_ref, dvb_ref, raw_ref, x_ref, dx1_ref, g1_ref, w_ref,
             gq_ref, gk_ref, cq_ref, sq_ref, ck_ref, sk_ref,
             gx_ref, dproj_ref, dg1_ref, dgq_ref, dgk_ref, dp):
        parts = []
        for r in range(tm // sub):
            rows = slice(r * sub, (r + 1) * sub)
            qa = raw_ref[rows, 0:Q_WIDTH]
            dqn = _rope_t(dqa_ref[rows, :], cq_ref[rows, :], sq_ref[rows, :])
            rq = _head_r(qa)
            nq = qa * rq
            dnq = dqn * gq_ref[...]
            dp[rows, 0:Q_WIDTH] = rq * (dnq - nq * (_seg64_sum(dnq * nq) * (1.0 / HEAD_DIM)))

            ka = raw_ref[rows, Q_WIDTH:QK_RAW]
            dkn = _rope_t(dkta_ref[0, :, rows].T, ck_ref[rows, :], sk_ref[rows, :])
            rk = _head_r(ka)
            nk = ka * rk
            dnk = dkn * gk_ref[...]
            dp[rows, 512:640] = rk * (dnk - nk * (_seg64_sum(dnk * nk) * (1.0 / HEAD_DIM)))

            dp[rows, 640:768] = dvta_ref[0, :, rows].T
            dp[rows, 768:1280] = dqb_ref[rows, :] * SCALE
            for j in range(r * sub // BLOCK, (r + 1) * sub // BLOCK):
                dp[j * BLOCK:(j + 1) * BLOCK, 1280:1408] = dkb_ref[0, j] + dkb_ref[1, j]
                dp[j * BLOCK:(j + 1) * BLOCK, 1408:1536] = dvb_ref[0, j] + dvb_ref[1, j]

            dproj = dp[rows, :].astype(MM)
            dproj_ref[rows, :] = dproj
            dh1 = _dot_nt(dproj[:, 0:IN_CHUNK], w_ref[0])
            for j in range(1, N_CHIPS):
                dh1 = dh1 + _dot_nt(dproj[:, j * IN_CHUNK:(j + 1) * IN_CHUNK], w_ref[j])
            xv = x_ref[rows, :]
            dxn, dg1v = _rms_bwd(xv, _rms_r(xv), g1_ref[...], dh1)
            gx_ref[rows, :] = dx1_ref[rows, :] + dxn
            parts.append((jnp.sum(dqn * nq, axis=0, keepdims=True), jnp.sum(dkn * nk, axis=0, keepdims=True),
                          jnp.sum(dg1v, axis=0, keepdims=True)))
        dgq_ref[0] = functools.reduce(jnp.add, [p[0] for p in parts])
        dgk_ref[0] = functools.reduce(jnp.add, [p[1] for p in parts])
        dg1_ref[0] = functools.reduce(jnp.add, [p[2] for p in parts])

    tok = lambda w: pl.BlockSpec((tm, w), lambda i: (i, 0))
    tab = lambda w: pl.BlockSpec((tm, w), lambda i: (i % n_seq, 0))
    row = lambda w: pl.BlockSpec((1, w), lambda i: (0, 0))
    tposed = pl.BlockSpec((1, KV_WIDTH, tm), lambda i: (i // n_seq, 0, i % n_seq))
    blocks = pl.BlockSpec((N_KV, nblk, BLOCK, KV_WIDTH), lambda i: (0, i, 0, 0))
    part = lambda w: pl.BlockSpec((1, 1, w), lambda i: (i, 0, 0))
    return _call(
        body, (dqa, dkta, dvta, dqb, dktb, dvtb, raw, x, dx1, g1, w_in, gq, gk, cq, sq, ck, sk),
        name="proj_bwd", jobs=jobs,
        grid=(nt,),
        in_specs=[tok(Q_WIDTH), tposed, tposed, tok(Q_WIDTH), blocks, blocks, tok(QK_RAW), tok(D_MODEL),
                  tok(D_MODEL), row(D_MODEL),
                  pl.BlockSpec((N_CHIPS, D_MODEL, IN_CHUNK), lambda i: (0, 0, 0)),
                  row(Q_WIDTH), row(KV_WIDTH), tab(KV_WIDTH), tab(KV_WIDTH), tab(KV_WIDTH), tab(KV_WIDTH)],
        out_specs=[tok(D_MODEL), tok(IN_TOTAL), part(D_MODEL), part(Q_WIDTH), part(KV_WIDTH)],
        out_shape=[jax.ShapeDtypeStruct((tokens, D_MODEL), F32),
                   jax.ShapeDtypeStruct((tokens, IN_TOTAL), MM),
                   jax.ShapeDtypeStruct((nt, 1, D_MODEL), F32),
                   jax.ShapeDtypeStruct((nt, 1, Q_WIDTH), F32),
                   jax.ShapeDtypeStruct((nt, 1, KV_WIDTH), F32)],
        scratch_shapes=[pltpu.VMEM((tm, IN_TOTAL), F32)],
        params=_cparams(("arbitrary",), 56))


def _pack_small(dg1, dg2, dg3, dg4, dgq, dgk, dsink, dbias, bucket, loss):
    def body(dg1_ref, dg2_ref, dg3_ref, dg4_ref, dgq_ref, dgk_ref, dsink_ref, dbias_ref, bucket_ref, loss_ref,
             out_ref, rel_ref):
        out_ref[...] = jnp.zeros_like(out_ref)
        for r, ref in ((ROW_G1, dg1_ref), (ROW_G2, dg2_ref), (ROW_G3, dg3_ref), (ROW_G4, dg4_ref)):
            acc = ref[0]
            for t in range(1, ref.shape[0]):
                acc = acc + ref[t]
            out_ref[r:r + 1, :] = acc

        def fold(ref, heads):
            acc = ref[0]
            for t in range(1, ref.shape[0]):
                acc = acc + ref[t]
            tot = acc[:, 0:HEAD_DIM]
            for h in range(1, heads):
                tot = tot + acc[:, h * HEAD_DIM:(h + 1) * HEAD_DIM]
            return tot

        out_ref[ROW_MISC:ROW_MISC + 1, MISC_GQ:MISC_GQ + HEAD_DIM] = fold(dgq_ref, GROUP * N_KV)
        out_ref[ROW_MISC:ROW_MISC + 1, MISC_GK:MISC_GK + HEAD_DIM] = fold(dgk_ref, N_KV)
        for h in range(GROUP * N_KV):
            g = h % GROUP
            out_ref[ROW_MISC:ROW_MISC + 1, MISC_SINK + h:MISC_SINK + h + 1] = jnp.sum(
                dsink_ref[h // GROUP, 0:1, g * BLOCK:(g + 1) * BLOCK], axis=-1, keepdims=True)
        lacc = loss_ref[0, 0:1, 0:1]
        for t in range(1, loss_ref.shape[0]):
            lacc = lacc + loss_ref[t, 0:1, 0:1]
        out_ref[ROW_MISC:ROW_MISC + 1, MISC_LOSS:MISC_LOSS + 1] = lacc
        lane = lax.broadcasted_iota(jnp.int32, (N_BUCKETS, LANES), 1)
        row = lax.broadcasted_iota(jnp.int32, (N_BUCKETS, LANES), 0)

        def per_bucket(b, acc):
            for h in range(GROUP * N_KV):
                g = h % GROUP
                sel = jnp.zeros((BLOCK, BLOCK), F32)
                for piece in range(3):
                    sel = sel + jnp.where(bucket_ref[piece] == b,
                                          dbias_ref[h // GROUP, piece, :, g * BLOCK:(g + 1) * BLOCK], 0.0)
                tot = jnp.sum(jnp.sum(sel, axis=0, keepdims=True), axis=-1, keepdims=True)
                acc = jnp.where((row == b) & (lane == h), tot, acc)
            return acc

        rel_ref[...] = lax.fori_loop(0, N_BUCKETS, per_bucket, jnp.zeros((N_BUCKETS, LANES), F32))

    args = (dg1, dg2, dg3, dg4, dgq, dgk, dsink, dbias, bucket, loss)
    outs = [jax.ShapeDtypeStruct((8, D_MODEL), F32), jax.ShapeDtypeStruct((N_BUCKETS, LANES), F32)]
    return pl.pallas_call(
        body, name="pack_small", grid=(1,),
        in_specs=[_whole(a) for a in args], out_specs=[_whole(o) for o in outs], out_shape=outs,
        compiler_params=pltpu.CompilerParams(vmem_limit_bytes=32 * 1024 * 1024),
    )(*args)


def _gather_weights(shards, whole):
    n = len(shards)
    full = [t for t in range(n) if whole[t]]

    def body(*refs):
        ins, outs = refs[:n], refs[n:2 * n]
        raw, stage = refs[2 * n:3 * n], refs[3 * n:4 * n]
        load_sem, local_sem, ici_send, ici_recv, d2d_send, d2d_recv = refs[4 * n:]
        x, y, c = _place()
        k = 2 * x + y
        sibling = (x, y, 1 - c)
        order = full + [t for t in range(n) if t not in full]
        loads = {t: pltpu.make_async_copy(ins[t], raw[t], load_sem.at[t]) for t in order}
        for t in order:
            loads[t].start()
        copies, sends = [], []
        for t in order:
            loads[t].wait()
            stage[t][...] = raw[t][...].astype(MM)
            mine = pltpu.make_async_copy(stage[t], outs[t].at[k], local_sem.at[t])
            mine.start()
            copies.append(mine)
            if t in full:
                half = ins[t].shape[0] // 2
                rows = pl.ds(c * half, half)
                for r, (fx, fy) in enumerate(_CHIP_FLIPS):
                    cp = _remote(stage[t].at[rows], outs[t].at[k, rows], ici_send.at[t, r], ici_recv.at[t, r],
                                 (_flip(x, fx), _flip(y, fy), c))
                    cp.start()
                    sends.append(cp)
        for t in full:
            half = ins[t].shape[0] // 2
            rows = pl.ds(c * half, half)
            for r, (fx, fy) in enumerate(_CHIP_FLIPS):
                kk = 2 * _flip(x, fx) + _flip(y, fy)
                landed = outs[t].at[kk, rows]
                _remote(landed, landed, ici_send.at[t, r], ici_recv.at[t, r], sibling).wait_recv()
                fwd = _remote(landed, landed, d2d_send.at[t, r], d2d_recv.at[t, r], sibling)
                fwd.start()
                sends.append(fwd)
        for t in full:
            half = ins[t].shape[0] // 2
            other = pl.ds((1 - c) * half, half)
            for r, (fx, fy) in enumerate(_CHIP_FLIPS):
                kk = 2 * _flip(x, fx) + _flip(y, fy)
                theirs = outs[t].at[kk, other]
                _remote(theirs, theirs, d2d_send.at[t, r], d2d_recv.at[t, r], sibling).wait_recv()
        for cp in sends:
            cp.wait_send()
        for cp in copies:
            cp.wait()

    return pl.pallas_call(
        body, name="gather_weights",
        in_specs=[HBM] * n, out_specs=[HBM] * n,
        out_shape=[pltpu.HBM((N_CHIPS,) + s.shape, MM) for s in shards],
        scratch_shapes=[pltpu.VMEM(s.shape, F32) for s in shards] + [pltpu.VMEM(s.shape, MM) for s in shards] + [
            pltpu.SemaphoreType.DMA((n,)), pltpu.SemaphoreType.DMA((n,)),
            pltpu.SemaphoreType.DMA((n, 3)), pltpu.SemaphoreType.DMA((n, 3)),
            pltpu.SemaphoreType.DMA((n, 3)), pltpu.SemaphoreType.DMA((n, 3))],
        compiler_params=pltpu.CompilerParams(vmem_limit_bytes=40 * 1024 * 1024),
    )(*shards)


def _add_half(grad, got, where, *, name, tr):
    nch, half, cols = got.shape
    nblk = half // tr

    def body(where_ref, g_ref, r_ref, o_ref):
        o_ref[...] = g_ref[...] + r_ref[...]

    return pl.pallas_call(
        body, name=name,
        grid_spec=pltpu.PrefetchScalarGridSpec(
            num_scalar_prefetch=1, grid=(nch, nblk),
            in_specs=[pl.BlockSpec((1, tr, cols), lambda j, i, where_ref: (j, where_ref[1] * nblk + i, 0)),
                      pl.BlockSpec((1, tr, cols), lambda j, i, where_ref: (j, i, 0))],
            out_specs=pl.BlockSpec((1, tr, cols), lambda j, i, where_ref: (j, i, 0))),
        out_shape=jax.ShapeDtypeStruct(got.shape, F32),
        compiler_params=_cparams(("parallel", "parallel"), 32),
    )(where, grad, got)


def _add_chips(own, got, where, *, name, tr):
    _, half, cols = own.shape
    nblk = half // tr

    def body(where_ref, o_ref, g_ref, out_ref):
        out_ref[...] = ((o_ref[0] + g_ref[0]) + g_ref[1]) + g_ref[2]

    return pl.pallas_call(
        body, name=name,
        grid_spec=pltpu.PrefetchScalarGridSpec(
            num_scalar_prefetch=1, grid=(nblk,),
            in_specs=[pl.BlockSpec((1, tr, cols), lambda i, where_ref: (where_ref[0], i, 0)),
                      pl.BlockSpec((3, tr, cols), lambda i, where_ref: (0, i, 0))],
            out_specs=pl.BlockSpec((tr, cols), lambda i, where_ref: (where_ref[1] * nblk + i, 0))),
        out_shape=pltpu.HBM((2 * half, cols), F32),
        compiler_params=_cparams(("parallel",), 32),
    )(where, own, got)


def _small_job(tiles):
    n = len(tiles)

    def copies(ins, outs, sems):
        x, y, c = _place()
        me = 4 * x + 2 * y + c
        local, send, recv = sems
        cps = []
        for t in range(n):
            cps.append(pltpu.make_async_copy(ins[t], outs[t].at[me], local.at[t]))
            for r in range(1, N_DEV):
                fx, fy, fc = (r >> 2) & 1, (r >> 1) & 1, r & 1
                cps.append(_remote(ins[t], outs[t].at[me], send.at[t, r - 1], recv.at[t, r - 1],
                                   (_flip(x, fx), _flip(y, fy), _flip(c, fc))))
        return cps

    return _Job(tiles, [jax.ShapeDtypeStruct((N_DEV,) + t.shape, F32) for t in tiles],
                [pltpu.SemaphoreType.DMA((n,)), pltpu.SemaphoreType.DMA((n, N_DEV - 1)),
                 pltpu.SemaphoreType.DMA((n, N_DEV - 1))], copies)


def _adamw_math(w, g, m, v):
    m = ADAM_B1 * m + (1.0 - ADAM_B1) * g
    v = ADAM_B2 * v + (1.0 - ADAM_B2) * (g * g)
    m_hat = m / (1.0 - ADAM_B1 ** ADAM_STEP)
    v_hat = v / (1.0 - ADAM_B2 ** ADAM_STEP)
    delta = -ADAM_LR * (m_hat / (jnp.sqrt(v_hat) + ADAM_EPS) + ADAM_WD * w)
    return delta, m, v


def _adamw(w, g, m, v, *, name, tr, jobs=()):
    rows, cols = w.shape

    def body(w_ref, g_ref, m_ref, v_ref, d_ref, nm_ref, nv_ref):
        d_ref[...], nm_ref[...], nv_ref[...] = _adamw_math(w_ref[...], g_ref[...], m_ref[...], v_ref[...])

    spec = pl.BlockSpec((tr, cols), lambda i: (i, 0))
    return _call(
        body, (w, g, m, v), name=name, jobs=jobs,
        grid=(rows // tr,),
        in_specs=[spec] * 4, out_specs=[spec] * 3,
        out_shape=[jax.ShapeDtypeStruct(w.shape, F32)] * 3,
        params=_cparams(("arbitrary",), 32))


def _small_adamw(gathered, gathered_rel, params, moments_m, moments_v):
    n = len(params)

    def body(all_ref, rel_all_ref, *refs):
        w_refs, m_refs, v_refs = refs[:n], refs[n:2 * n], refs[2 * n:3 * n]
        loss_ref = refs[3 * n]
        out_refs = refs[3 * n + 1:]
        g = all_ref[0]
        rel = rel_all_ref[0]
        for d in range(1, N_DEV):
            g = g + all_ref[d]
            rel = rel + rel_all_ref[d]
        misc = g[ROW_MISC:ROW_MISC + 1]
        loss_ref[...] = misc[:, MISC_LOSS:MISC_LOSS + 1]
        grads = (g[ROW_G1:ROW_G1 + 1], g[ROW_G2:ROW_G2 + 1], g[ROW_G3:ROW_G3 + 1], g[ROW_G4:ROW_G4 + 1],
                 misc[:, MISC_GQ:MISC_GQ + HEAD_DIM], misc[:, MISC_GK:MISC_GK + HEAD_DIM],
                 misc[:, MISC_SINK:MISC_SINK + GROUP * N_KV], rel[:, 0:GROUP * N_KV])
        for i in range(n):
            d, nm, nv = _adamw_math(w_refs[i][...], grads[i], m_refs[i][...], v_refs[i][...])
            for j, val in enumerate((grads[i], d, nm, nv)):
                out_refs[4 * i + j][...] = val

    args = (gathered, gathered_rel, *params, *moments_m, *moments_v)
    out_shape = [jax.ShapeDtypeStruct((1, 1), F32)] + [jax.ShapeDtypeStruct(p.shape, F32) for p in params
                                                       for _ in range(4)]
    outs = pl.pallas_call(
        body, name="small_adamw", grid=(1,),
        in_specs=[_whole(a) for a in args], out_specs=[_whole(o) for o in out_shape], out_shape=out_shape,
    )(*args)
    return outs[0], [outs[1 + 4 * i:5 + 4 * i] for i in range(n)]


def kernel(x, w_in, w_o, g_pre_mix, g_post_mix, q_norm_a, k_norm_a, sink_b, rel_bias, g_pre_ffn, w_ffn_up, w_ffn_down, g_post_ffn, loss_target, m_w_in, m_w_o, m_g_pre_mix, m_g_post_mix, m_q_norm_a, m_k_norm_a, m_sink_b, m_rel_bias, m_g_pre_ffn, m_w_ffn_up, m_w_ffn_down, m_g_post_ffn, v_w_in, v_w_o, v_g_pre_mix, v_g_post_mix, v_q_norm_a, v_k_norm_a, v_sink_b, v_rel_bias, v_g_pre_ffn, v_w_ffn_up, v_w_ffn_down, v_g_post_ffn):
    batch, seq, _ = x.shape
    tokens = batch * seq
    where = jnp.stack([2 * lax.axis_index("x") + lax.axis_index("y"), lax.axis_index("c")]).astype(jnp.int32)
    x2 = x.reshape(tokens, D_MODEL)
    g1, g2, g3, g4 = g_pre_mix, g_post_mix, g_pre_ffn, g_post_ffn

    cos, sin = _rope_tables(seq)
    ck, sk = jnp.tile(cos, (1, 2)), jnp.tile(sin, (1, 2))
    cq, sq = ck * SCALE, sk * SCALE
    gq8, gk2 = jnp.tile(q_norm_a, (1, 8)), jnp.tile(k_norm_a, (1, 2))
    bucket, band = _window_tables()
    bias = _bias_build(rel_bias.T, bucket, band)

    w_in_g, w_o_p, w_up_p, w_down_p = _gather_weights(
        (w_in[0], w_o[0], w_ffn_up[0], w_ffn_down[0]), whole=(True, False, False, False))
    (h1, raw, qa, ka, kta, va, vta, qtb, kb, ktb, vb, vtb) = _pre_proj(
        x2, g1, w_in_g, gq8, gk2, cq, sq, ck, sk, seq=seq, tm=min(512, seq), sub=256)
    (oa, p_a, linv_a), (w_part,) = _attn_a_fwd(
        qa, kta, va, seq=seq, bq=min(256, seq), jobs=[_gather_job([w_o_p, w_up_p, w_down_p], forward=False)])
    kb3 = kb.reshape(tokens // BLOCK, BLOCK, KV_WIDTH)
    vb3 = vb.reshape(tokens // BLOCK, BLOCK, KV_WIDTH)
    (ob, lse_b), ((w_o_g, w_up_g, w_down_g),) = _attn_b_fwd(
        qtb, kb3, vtb, bias, sink_b, seq=seq, jobs=[_gather_job(w_part, forward=True)])
    w_o2 = w_o_g.reshape(D_MODEL, D_MODEL)
    w_down2 = w_down_g.reshape(D_FF, D_MODEL)
    mix, x1, h2, o_cat = _wo_post(oa, ob, w_o2, x2, g2, g3, tm=512, sub=256)
    u, df, dy, loss_t, dg4 = _ffn_fwd_loss(h2, w_up_g, w_down2, x1, loss_target.reshape(tokens, D_MODEL), g4, tm=256)

    dz, dx1, dmix, dg3, dg2 = _ffn_bwd_act(df, w_down2, u, w_up_g, x1, dy, mix, g3, g2, tm=256)
    gw_down, _ = _tn_matmul(u, df, name="grad_w_down", tm=1024, tn=1024, tk=min(2048, tokens), square_a=True)
    gw_down = gw_down.reshape(N_CHIPS, FF_CHUNK, D_MODEL)
    gw_up, ((got_down,),) = _tn_matmul(h2, dz, name="grad_w_up", tm=1024, tn=1024, tk=min(2048, tokens), chunked=True,
                                        jobs=[_swap_job([gw_down])])
    doa, dob = _wo_bwd(dmix, w_o2, tm=512)
    gw_o, _ = _tn_matmul(o_cat, dmix, name="grad_w_o", tm=1024, tn=1024, tk=min(2048, tokens))
    gw_o = gw_o.reshape(N_CHIPS, O_CHUNK, D_MODEL)
    sum_down = _add_half(gw_down, got_down, where, name="add_half_w_down", tr=128)
    (dqa, dkta, dvta), ((ex_down,), (got_up,)) = _attn_a_bwd(
        qa, ka, vta, doa, oa, p_a, linv_a, seq=seq, bq=min(256, seq),
        jobs=[_exchange_job([sum_down]), _swap_job([gw_up])])
    full_down = _add_chips(sum_down, ex_down, where, name="add_chips_w_down", tr=128)
    sum_up = _add_half(gw_up, got_up, where, name="add_half_w_up", tr=128)
    (dqb, dkb, dvb, dbias, dsink), ((ex_up,), (g_down,), (got_o,)) = _attn_b_bwd(
        qtb, kb3, ktb, vb3, dob, ob, lse_b, bias, sink_b, seq=seq,
        jobs=[_exchange_job([sum_up]), _join_job([full_down]), _swap_job([gw_o])])
    full_up = _add_chips(sum_up, ex_up, where, name="add_chips_w_up", tr=128)
    sum_o = _add_half(gw_o, got_o, where, name="add_half_w_o", tr=128)
    (grad_x, dproj, dg1, dgq, dgk), _ = _proj_bwd(
        dqa, dkta, dvta, dqb, dkb, dvb, raw, x2, dx1, g1, w_in_g, gq8, gk2, cq, sq, ck, sk,
        seq=seq, tm=min(512, seq), sub=128)
    packed, packed_rel = _pack_small(dg1, dg2, dg3, dg4, dgq, dgk, dsink, dbias, bucket, loss_t)
    gw_in, ((ex_o,), (g_up,), (gathered, gathered_rel)) = _tn_matmul(
        h1, dproj, name="grad_w_in", tm=1024, tn=IN_CHUNK, tk=min(2048, tokens), chunked=True,
        jobs=[_exchange_job([sum_o]), _join_job([full_up]), _small_job([packed, packed_rel])])
    full_o = _add_chips(sum_o, ex_o, where, name="add_chips_w_o", tr=128)

    upd_down, ((g_o,), (got_in,)) = _adamw(
        w_ffn_down[0], g_down, m_w_ffn_down[0], v_w_ffn_down[0], name="adamw_w_down", tr=128,
        jobs=[_join_job([full_o]), _swap_job([gw_in])])
    sum_in = _add_half(gw_in, got_in, where, name="add_half_w_in", tr=128)
    upd_up, ((ex_in,),) = _adamw(w_ffn_up[0], g_up, m_w_ffn_up[0], v_w_ffn_up[0], name="adamw_w_up", tr=128,
                                 jobs=[_exchange_job([sum_in])])
    full_in = _add_chips(sum_in, ex_in, where, name="add_chips_w_in", tr=128)
    upd_o, ((g_in,),) = _adamw(w_o[0], g_o, m_w_o[0], v_w_o[0], name="adamw_w_o", tr=128,
                               jobs=[_join_job([full_in])])
    upd_in, _ = _adamw(w_in[0], g_in, m_w_in[0], v_w_in[0], name="adamw_w_in", tr=128)
    big = [[t[None] for t in (g, *upd)] for g, upd in
           ((g_in, upd_in), (g_o, upd_o), (g_up, upd_up), (g_down, upd_down))]

    loss, small = _small_adamw(
        gathered, gathered_rel,
        (g1, g2, g3, g4, q_norm_a, k_norm_a, sink_b, rel_bias),
        (m_g_pre_mix, m_g_post_mix, m_g_pre_ffn, m_g_post_ffn, m_q_norm_a, m_k_norm_a, m_sink_b, m_rel_bias),
        (v_g_pre_mix, v_g_post_mix, v_g_pre_ffn, v_g_post_ffn, v_q_norm_a, v_k_norm_a, v_sink_b, v_rel_bias))
    s_g1, s_g2, s_g3, s_g4, s_gq, s_gk, s_sink, s_rel = small

    def leaves(i):
        return (big[0][i], big[1][i], s_g1[i], s_g2[i], s_gq[i], s_gk[i], s_sink[i], s_rel[i], s_g3[i],
                big[2][i], big[3][i], s_g4[i])

    loss = loss.reshape(())
    return (loss, grad_x.reshape(batch, seq, D_MODEL), *leaves(0), *leaves(1), *leaves(2), *leaves(3))
```

```python
import functools

import jax
import jax.numpy as jnp
import numpy as np
from jax import lax
from jax.experimental import pallas as pl
from jax.experimental.pallas import tpu as pltpu

F32 = jnp.float32
MM = jnp.bfloat16

D_MODEL = 1024
HEAD_DIM = 64
N_KV = 2
GROUP = 4
Q_WIDTH = 512
KV_WIDTH = 128
D_FF = 4096
GRID_W = 64
BLOCK = 128
N_BUCKETS = 32
MAX_DISTANCE = 128
ROPE_THETA = 10000.0
EPS = 1e-6
NEG_INF = -1e30
SCALE = HEAD_DIM ** -0.5
IN_TOTAL = 1536
N_CHIPS = 4
N_DEV = 8
IN_CHUNK = IN_TOTAL // N_CHIPS
FF_CHUNK = D_FF // N_CHIPS
O_CHUNK = D_MODEL // N_CHIPS
QK_RAW = 640

ADAM_LR = 0.001
ADAM_B1 = 0.9
ADAM_B2 = 0.999
ADAM_EPS = 1e-08
ADAM_WD = 0.01
ADAM_STEP = 10

LANES = 128
MESH = pl.DeviceIdType.MESH
HBM = pl.BlockSpec(memory_space=pl.ANY)
VMEM = pl.BlockSpec(memory_space=pltpu.VMEM)
SMEM = pl.BlockSpec(memory_space=pltpu.SMEM)

ROW_G1, ROW_G2, ROW_G3, ROW_G4, ROW_MISC = 0, 1, 2, 3, 4
MISC_GQ, MISC_GK, MISC_SINK, MISC_LOSS = 0, 64, 128, 512


def _cparams(sem, vmem_mb):
    return pltpu.CompilerParams(dimension_semantics=sem, vmem_limit_bytes=vmem_mb * 1024 * 1024)


def _whole(a):
    return pl.BlockSpec(a.shape, lambda i: (0,) * len(a.shape))


def _in_hbm(s):
    return pltpu.HBM(s.shape, s.dtype)


class _Job:
    def __init__(self, operands, out_shapes, sems, copies, alias=None):
        self.operands, self.out_shapes, self.sems, self.copies = list(operands), list(out_shapes), list(sems), copies
        self.alias = dict(alias or {})


def _place():
    return lax.axis_index("x"), lax.axis_index("y"), lax.axis_index("c")


_CHIP_FLIPS = ((1, 0), (0, 1), (1, 1))


def _flip(v, bit):
    return 1 - v if bit else v


def _remote(src, dst, send, recv, dev):
    return pltpu.make_async_remote_copy(src_ref=src, dst_ref=dst, send_sem=send, recv_sem=recv,
                                        device_id=dev, device_id_type=MESH)


def _swap_job(grads):
    n = len(grads)

    def copies(ins, outs, sems):
        x, y, c = _place()
        send, recv = sems
        cps = []
        for t in range(n):
            half = ins[t].shape[1] // 2
            cps.append(_remote(ins[t].at[:, pl.ds((1 - c) * half, half), :], outs[t], send.at[t], recv.at[t],
                               (x, y, 1 - c)))
        return cps

    shapes = [jax.ShapeDtypeStruct((g.shape[0], g.shape[1] // 2, g.shape[2]), F32) for g in grads]
    return _Job(grads, shapes, [pltpu.SemaphoreType.DMA((n,)), pltpu.SemaphoreType.DMA((n,))], copies)


def _exchange_job(sums):
    n = len(sums)

    def copies(ins, outs, sems):
        x, y, c = _place()
        send, recv = sems
        cps = []
        for t in range(n):
            for r, (fx, fy) in enumerate(_CHIP_FLIPS):
                kk = 2 * _flip(x, fx) + _flip(y, fy)
                cps.append(_remote(ins[t].at[kk], outs[t].at[r], send.at[t, r], recv.at[t, r],
                                   (_flip(x, fx), _flip(y, fy), c)))
        return cps

    shapes = [jax.ShapeDtypeStruct((3,) + s.shape[1:], s.dtype) for s in sums]
    return _Job(sums, shapes, [pltpu.SemaphoreType.DMA((n, 3)), pltpu.SemaphoreType.DMA((n, 3))], copies)


def _join_job(fulls):
    n = len(fulls)

    def copies(ins, outs, sems):
        x, y, c = _place()
        send, recv = sems
        cps = []
        for t in range(n):
            half = ins[t].shape[0] // 2
            rows = pl.ds(c * half, half)
            cps.append(_remote(ins[t].at[rows], outs[t].at[rows], send.at[t], recv.at[t], (x, y, 1 - c)))
        return cps

    shapes = [jax.ShapeDtypeStruct(f.shape, f.dtype) for f in fulls]
    return _Job(fulls, shapes, [pltpu.SemaphoreType.DMA((n,)), pltpu.SemaphoreType.DMA((n,))], copies,
                alias={t: t for t in range(n)})


def _gather_job(bufs, forward):
    n = len(bufs)

    def copies(ins, outs, sems):
        x, y, c = _place()
        send, recv = sems
        cps = []
        for t in range(n):
            half = ins[t].shape[1] // 2
            rows = pl.ds(c * half, half)
            for r, (fx, fy) in enumerate(_CHIP_FLIPS):
                if forward:
                    kk = 2 * _flip(x, fx) + _flip(y, fy)
                    dev = (x, y, 1 - c)
                else:
                    kk = 2 * x + y
                    dev = (_flip(x, fx), _flip(y, fy), c)
                cps.append(_remote(ins[t].at[kk, rows], outs[t].at[kk, rows], send.at[t, r], recv.at[t, r], dev))
        return cps

    shapes = [jax.ShapeDtypeStruct(b.shape, b.dtype) for b in bufs]
    return _Job(bufs, shapes, [pltpu.SemaphoreType.DMA((n, 3)), pltpu.SemaphoreType.DMA((n, 3))], copies,
                alias={t: t for t in range(n)})


def _call(body, args, *, name, grid, in_specs, out_specs, out_shape, scratch_shapes=(), params=None, jobs=()):
    n_in, n_out, n_scr = len(in_specs), len(out_specs), len(scratch_shapes)
    job_in = [len(j.operands) for j in jobs]
    job_out = [len(j.out_shapes) for j in jobs]
    job_sem = [len(j.sems) for j in jobs]

    def wrapped(*refs):
        pos = 0
        ins = refs[pos:pos + n_in]; pos += n_in
        jins = []
        for k in job_in:
            jins.append(refs[pos:pos + k]); pos += k
        outs = refs[pos:pos + n_out]; pos += n_out
        jouts = []
        for k in job_out:
            jouts.append(refs[pos:pos + k]); pos += k
        scr = refs[pos:pos + n_scr]; pos += n_scr
        jsems = []
        for k in job_sem:
            jsems.append(refs[pos:pos + k]); pos += k
        if jobs:
            ids = [pl.program_id(d) for d in range(len(grid))]
            first = functools.reduce(jnp.logical_and, [i == 0 for i in ids])
            last = functools.reduce(jnp.logical_and, [i == g - 1 for i, g in zip(ids, grid)])

            @pl.when(first)
            def _():
                for j, ji, jo, js in zip(jobs, jins, jouts, jsems):
                    for cp in j.copies(ji, jo, js):
                        cp.start()

        body(*ins, *outs, *scr)
        if jobs:
            @pl.when(last)
            def _():
                for j, ji, jo, js in zip(jobs, jins, jouts, jsems):
                    for cp in j.copies(ji, jo, js):
                        cp.wait()

    aliases = {}
    in_pos, out_pos = n_in, n_out
    for j in jobs:
        for i, o in j.alias.items():
            aliases[in_pos + i] = out_pos + o
        in_pos += len(j.operands)
        out_pos += len(j.out_shapes)
    res = pl.pallas_call(
        wrapped, name=name, grid=grid,
        in_specs=list(in_specs) + [HBM] * sum(job_in),
        out_specs=list(out_specs) + [HBM] * sum(job_out),
        out_shape=list(out_shape) + [_in_hbm(s) for j in jobs for s in j.out_shapes],
        scratch_shapes=list(scratch_shapes) + [s for j in jobs for s in j.sems],
        input_output_aliases=aliases,
        compiler_params=params,
    )(*args, *[a for j in jobs for a in j.operands])
    own, rest = list(res[:n_out]), list(res[n_out:])
    job_res = []
    for k in job_out:
        job_res.append(rest[:k])
        rest = rest[k:]
    return own, job_res


def _run_jobs(name, jobs):
    def body():
        pass

    return _call(body, (), name=name, grid=(1,), in_specs=[], out_specs=[], out_shape=[], jobs=jobs)[1]


def _dot(a, b):
    return jnp.dot(a, b, preferred_element_type=F32)


def _dot_nt(a, b):
    return lax.dot_general(a, b, (((1,), (1,)), ((), ())), preferred_element_type=F32)


def _dot_tn(a, b):
    return lax.dot_general(a, b, (((0,), (0,)), ((), ())), preferred_element_type=F32)


def _rms_r(x):
    return lax.rsqrt(jnp.mean(x * x, axis=-1, keepdims=True) + EPS)


def _rms_bwd(x, r, g, dy):
    n = x * r
    dn = dy * g
    dx = r * (dn - n * jnp.mean(dn * n, axis=-1, keepdims=True))
    return dx, dy * n


def _seg64_sum(v):
    rows, width = v.shape
    lane = lax.broadcasted_iota(jnp.int32, (rows, LANES), 1)
    lo = lane < HEAD_DIM
    outs = []
    for c in range(width // LANES):
        ch = v[:, c * LANES:(c + 1) * LANES]
        s_lo = jnp.sum(jnp.where(lo, ch, 0.0), axis=-1, keepdims=True)
        s_hi = jnp.sum(jnp.where(lo, 0.0, ch), axis=-1, keepdims=True)
        outs.append(jnp.where(lo, s_lo, s_hi))
    return outs[0] if len(outs) == 1 else jnp.concatenate(outs, axis=-1)


def _head_r(v):
    return lax.rsqrt(_seg64_sum(v * v) * (1.0 / HEAD_DIM) + EPS)


def _swap16(ch):
    lane = lax.broadcasted_iota(jnp.int32, ch.shape, 1)
    return jnp.where((lane % 32) < 16, pltpu.roll(ch, LANES - 16, 1), pltpu.roll(ch, 16, 1))


def _by_chunk(fn, v):
    outs = [fn(v[:, c * LANES:(c + 1) * LANES]) for c in range(v.shape[1] // LANES)]
    return outs[0] if len(outs) == 1 else jnp.concatenate(outs, axis=-1)


def _rope(v, cos, sin_signed):
    return _by_chunk(lambda ch: ch * cos + _swap16(ch) * sin_signed, v)


def _rope_t(g, cos, sin_signed):
    return _by_chunk(lambda ch: ch * cos + _swap16(ch * sin_signed), g)


def _rope_tables(seq):
    nf = HEAD_DIM // 4
    freqs = ROPE_THETA ** (-jnp.arange(nf, dtype=F32) / nf)
    pos = jnp.arange(seq, dtype=jnp.int32)
    row = (pos // GRID_W).astype(F32)
    col = (pos % GRID_W).astype(F32)
    ang_r = row[:, None] * freqs[None, :]
    ang_c = col[:, None] * freqs[None, :]
    cr, sr, cc, sc = jnp.cos(ang_r), jnp.sin(ang_r), jnp.cos(ang_c), jnp.sin(ang_c)
    cos = jnp.concatenate([cr, cr, cc, cc], axis=1)
    sin = jnp.concatenate([-sr, sr, -sc, sc], axis=1)
    return cos, sin


def _t5_bucket(rel):
    nb = N_BUCKETS // 2
    ret = (rel > 0).astype(jnp.int32) * nb
    n = jnp.abs(rel)
    max_exact = nb // 2
    nf = jnp.maximum(n, 1).astype(jnp.float32)
    large = max_exact + (jnp.log(nf / max_exact) / np.float32(np.log(MAX_DISTANCE / max_exact))
                         * (nb - max_exact)).astype(jnp.int32)
    large = jnp.minimum(large, nb - 1)
    return ret + jnp.where(n < max_exact, n, large)


def _window_tables():
    a = jnp.arange(BLOCK, dtype=jnp.int32)
    c = jnp.arange(3 * BLOCK, dtype=jnp.int32)
    rel = c[None, :] - BLOCK - a[:, None]
    bucket = _t5_bucket(rel)
    band = (jnp.abs(rel) <= BLOCK).astype(jnp.int32)
    to3 = lambda t: t.reshape(BLOCK, 3, BLOCK).transpose(1, 2, 0)
    return to3(bucket), to3(band)


def _pre_proj(x, g1, w_in, gq, gk, cq, sq, ck, sk, *, seq, tm, sub):
    tokens = x.shape[0]
    n_seq = seq // tm
    nblk = tm // BLOCK
    batch = tokens // seq

    def body(x_ref, g1_ref, w_ref, gq_ref, gk_ref, cq_ref, sq_ref, ck_ref, sk_ref,
             h1_ref, raw_ref, qa_ref, ka_ref, kta_ref, va_ref, vta_ref,
             qtb_ref, kb_ref, ktb_ref, vb_ref, vtb_ref, proj):
        for r in range(tm // sub):
            rows = slice(r * sub, (r + 1) * sub)
            xv = x_ref[rows, :]
            h = (xv * _rms_r(xv) * g1_ref[...]).astype(MM)
            h1_ref[rows, :] = h
            for j in range(N_CHIPS):
                proj[rows, j * IN_CHUNK:(j + 1) * IN_CHUNK] = _dot(h, w_ref[j])
            qa = proj[rows, 0:Q_WIDTH]
            ka = proj[rows, Q_WIDTH:QK_RAW]
            raw_ref[rows, :] = proj[rows, 0:QK_RAW]
            qn = qa * _head_r(qa) * gq_ref[...]
            qa_ref[rows, :] = _rope(qn, cq_ref[rows, :], sq_ref[rows, :]).astype(MM)
            kn = ka * _head_r(ka) * gk_ref[...]
            kr = _rope(kn, ck_ref[rows, :], sk_ref[rows, :])
            ka_ref[rows, :] = kr.astype(MM)
            kta_ref[0, :, rows] = kr.T.astype(MM)
            va = proj[rows, 640:768]
            va_ref[rows, :] = va.astype(MM)
            vta_ref[0, :, rows] = va.T.astype(MM)
            qb = proj[rows, 768:1280] * SCALE
            kb = proj[rows, 1280:1408]
            vb = proj[rows, 1408:1536]
            kb_ref[rows, :] = kb.astype(MM)
            vb_ref[rows, :] = vb.astype(MM)
            for j in range(sub // BLOCK):
                blk = slice(j * BLOCK, (j + 1) * BLOCK)
                qtb_ref[r * (sub // BLOCK) + j] = qb[blk, :].T.astype(MM)
                ktb_ref[r * (sub // BLOCK) + j] = kb[blk, :].T.astype(MM)
                vtb_ref[r * (sub // BLOCK) + j] = vb[blk, :].T.astype(MM)

    tok = lambda w: pl.BlockSpec((tm, w), lambda i: (i, 0))
    tab = lambda w: pl.BlockSpec((tm, w), lambda i: (i % n_seq, 0))
    row = lambda w: pl.BlockSpec((1, w), lambda i: (0, 0))
    tposed = pl.BlockSpec((1, LANES, tm), lambda i: (i // n_seq, 0, i % n_seq))
    blocks = pl.BlockSpec((nblk, BLOCK, LANES), lambda i: (i, 0, 0))
    qblocks = pl.BlockSpec((nblk, Q_WIDTH, BLOCK), lambda i: (i, 0, 0))
    tok_mm = lambda w: jax.ShapeDtypeStruct((tokens, w), MM)
    return pl.pallas_call(
        body, name="pre_proj",
        grid=(tokens // tm,),
        in_specs=[tok(D_MODEL), row(D_MODEL),
                  pl.BlockSpec((N_CHIPS, D_MODEL, IN_CHUNK), lambda i: (0, 0, 0)),
                  row(Q_WIDTH), row(KV_WIDTH), tab(KV_WIDTH), tab(KV_WIDTH), tab(KV_WIDTH), tab(KV_WIDTH)],
        out_specs=[tok(D_MODEL), tok(QK_RAW), tok(Q_WIDTH), tok(KV_WIDTH), tposed, tok(KV_WIDTH), tposed,
                   qblocks, tok(KV_WIDTH), blocks, tok(KV_WIDTH), blocks],
        out_shape=[
            tok_mm(D_MODEL),
            jax.ShapeDtypeStruct((tokens, QK_RAW), F32),
            tok_mm(Q_WIDTH),
            tok_mm(KV_WIDTH),
            jax.ShapeDtypeStruct((batch, KV_WIDTH, seq), MM),
            tok_mm(KV_WIDTH),
            jax.ShapeDtypeStruct((batch, KV_WIDTH, seq), MM),
            jax.ShapeDtypeStruct((tokens // BLOCK, Q_WIDTH, BLOCK), MM),
            tok_mm(KV_WIDTH),
            jax.ShapeDtypeStruct((tokens // BLOCK, KV_WIDTH, BLOCK), MM),
            tok_mm(KV_WIDTH),
            jax.ShapeDtypeStruct((tokens // BLOCK, KV_WIDTH, BLOCK), MM),
        ],
        scratch_shapes=[pltpu.VMEM((tm, IN_TOTAL), F32)],
        compiler_params=_cparams(("parallel",), 48),
    )(x, g1, w_in, gq, gk, cq, sq, ck, sk)


def _kv_half(v2, kv):
    return jnp.where(kv == 0, v2[:, :HEAD_DIM], v2[:, HEAD_DIM:])


def _attn_a_fwd(qa, kta, va, *, seq, bq, jobs=()):
    tokens = qa.shape[0]
    batch = tokens // seq
    nq = seq // bq

    def body(q_ref, kt_ref, v_ref, o_ref, p_ref, linv_ref):
        kv = pl.program_id(1)
        kt = kt_ref[0]
        lane = lax.broadcasted_iota(jnp.int32, (seq, KV_WIDTH), 1)
        v = jnp.where((lane < HEAD_DIM) == (kv == 0), v_ref[...], jnp.ones((), MM))
        for g in range(GROUP):
            sl = slice(g * HEAD_DIM, (g + 1) * HEAD_DIM)
            s = _dot(q_ref[:, sl], kt)
            pb = jnp.exp((s - jnp.max(s, axis=-1, keepdims=True)).astype(MM))
            p_ref[0, g] = pb
            o2 = _dot(pb, v)
            linv = 1.0 / _kv_half(o2, 1 - kv)[:, 0:1]
            o_ref[:, sl] = _kv_half(o2, kv) * linv
            linv_ref[0, :, g:g + 1] = linv

    return _call(
        body, (qa, kta, va), name="attn_a_fwd", jobs=jobs,
        grid=(batch, N_KV, nq),
        in_specs=[pl.BlockSpec((bq, GROUP * HEAD_DIM), lambda b, k, i: (b * nq + i, k)),
                  pl.BlockSpec((1, HEAD_DIM, seq), lambda b, k, i: (b, k, 0)),
                  pl.BlockSpec((seq, KV_WIDTH), lambda b, k, i: (b, 0))],
        out_specs=[pl.BlockSpec((bq, GROUP * HEAD_DIM), lambda b, k, i: (b * nq + i, k)),
                   pl.BlockSpec((1, GROUP, bq, seq), lambda b, k, i: (k, 0, b * nq + i, 0)),
                   pl.BlockSpec((1, bq, GROUP), lambda b, k, i: (k, b * nq + i, 0))],
        out_shape=[jax.ShapeDtypeStruct((tokens, Q_WIDTH), F32),
                   jax.ShapeDtypeStruct((N_KV, GROUP, tokens, seq), MM),
                   jax.ShapeDtypeStruct((N_KV, tokens, GROUP), F32)],
        params=_cparams(("arbitrary", "arbitrary", "arbitrary"), 56))


def _attn_a_bwd(qa, ka, vta, do, o, p, linv, *, seq, bq, jobs=()):
    tokens = qa.shape[0]
    batch = tokens // seq
    nq = seq // bq

    def body(q_ref, k_ref, vt_ref, do_ref, o_ref, p_ref, linv_ref, dq_ref, dkt_ref, dvt_ref):
        kv = pl.program_id(1)

        @pl.when(pl.program_id(2) == 0)
        def _():
            dkt_ref[...] = jnp.zeros_like(dkt_ref)
            dvt_ref[...] = jnp.zeros_like(dvt_ref)

        vt = vt_ref[0]
        k2 = k_ref[...]
        for g in range(GROUP):
            sl = slice(g * HEAD_DIM, (g + 1) * HEAD_DIM)
            dof = do_ref[:, sl]
            delta = jnp.sum(dof * o_ref[:, sl], axis=-1, keepdims=True)
            linv_g = linv_ref[0, :, g:g + 1]
            pb = p_ref[0, g]
            dp = _dot(dof.astype(MM), vt)
            ds = pb * ((dp - delta) * linv_g).astype(MM)
            dq_ref[:, sl] = _kv_half(_dot(ds, k2), kv)
            dkt_ref[0] += _dot_tn(q_ref[:, sl], ds)
            dvt_ref[0] += _dot_tn((dof * linv_g).astype(MM), pb)

    qspec = pl.BlockSpec((bq, GROUP * HEAD_DIM), lambda b, k, i: (b * nq + i, k))
    tspec = pl.BlockSpec((1, HEAD_DIM, seq), lambda b, k, i: (b, k, 0))
    return _call(
        body, (qa, ka, vta, do, o, p, linv), name="attn_a_bwd", jobs=jobs,
        grid=(batch, N_KV, nq),
        in_specs=[qspec, pl.BlockSpec((seq, KV_WIDTH), lambda b, k, i: (b, 0)), tspec, qspec, qspec,
                  pl.BlockSpec((1, GROUP, bq, seq), lambda b, k, i: (k, 0, b * nq + i, 0)),
                  pl.BlockSpec((1, bq, GROUP), lambda b, k, i: (k, b * nq + i, 0))],
        out_specs=[qspec, tspec, tspec],
        out_shape=[jax.ShapeDtypeStruct((tokens, Q_WIDTH), F32),
                   jax.ShapeDtypeStruct((batch, KV_WIDTH, seq), F32),
                   jax.ShapeDtypeStruct((batch, KV_WIDTH, seq), F32)],
        params=_cparams(("arbitrary", "arbitrary", "arbitrary"), 56))


def _bias_build(rel_bias_t, bucket_t, band_t):
    def body(tab_ref, bucket_ref, band_ref, bias_ref):
        for h in range(GROUP * N_KV):
            for piece in range(3):
                bk = bucket_ref[piece]
                acc = jnp.zeros((BLOCK, BLOCK), F32)
                for b in range(N_BUCKETS):
                    acc = jnp.where(bk == b, tab_ref[h, b], acc)
                g = h % GROUP
                bias_ref[h // GROUP, piece, :, g * BLOCK:(g + 1) * BLOCK] = jnp.where(band_ref[piece] != 0, acc, NEG_INF)

    out = jax.ShapeDtypeStruct((N_KV, 3, BLOCK, GROUP * BLOCK), F32)
    return pl.pallas_call(
        body, name="bias_build", grid=(1,),
        in_specs=[SMEM, _whole(bucket_t), _whole(band_t)], out_specs=_whole(out), out_shape=out,
    )(rel_bias_t, bucket_t, band_t)


def _pad_heads(t, kv):
    outs = []
    for g in range(GROUP):
        tg = t[g * HEAD_DIM:(g + 1) * HEAD_DIM, :]
        zero = jnp.zeros_like(tg)
        outs.append(jnp.concatenate([jnp.where(kv == 0, tg, zero), jnp.where(kv == 0, zero, tg)], axis=0))
    return jnp.concatenate(outs, axis=-1)


def _unpad_heads(t, kv):
    outs = [_kv_half(t[:, g * BLOCK:(g + 1) * BLOCK].T, kv) for g in range(GROUP)]
    return jnp.concatenate(outs, axis=-1)


def _sink_row(sink_ref, kv):
    lane_head = lax.broadcasted_iota(jnp.int32, (1, GROUP * BLOCK), 1) // BLOCK
    row = jnp.zeros((1, GROUP * BLOCK), F32)
    for g in range(GROUP):
        row = jnp.where(lane_head == g, sink_ref[0, kv * GROUP + g], row)
    return row


def _window_scores_t(k_ref, idx, qpad, bias_ref, n, nblk):
    pieces = []
    for piece in range(3):
        s = _dot(k_ref[idx[piece]], qpad) + bias_ref[0, piece]
        if piece == 0:
            s = jnp.where(n > 0, s, NEG_INF)
        if piece == 2:
            s = jnp.where(n < nblk - 1, s, NEG_INF)
        pieces.append(s)
    return pieces


def _attn_b_fwd(qtb, kb3, vtb, bias, sink, *, seq, jobs=()):
    nblk_all = qtb.shape[0]
    tokens = nblk_all * BLOCK
    batch = tokens // seq
    nblk = seq // BLOCK

    def body(sink_ref, q_ref, k_ref, vt_ref, bias_ref, o_ref, lse_ref):
        kv = pl.program_id(0)
        sink_row = _sink_row(sink_ref, kv)

        def block(n, carry):
            idx = (jnp.maximum(n - 1, 0), n, jnp.minimum(n + 1, nblk - 1))
            rows = pl.ds(pl.multiple_of(n * BLOCK, BLOCK), BLOCK)
            qpad = _pad_heads(q_ref[n], kv)
            ss = _window_scores_t(k_ref, idx, qpad, bias_ref, n, nblk)
            m = jnp.maximum(jnp.maximum(jnp.max(ss[0], axis=0, keepdims=True),
                                        jnp.max(ss[1], axis=0, keepdims=True)),
                            jnp.maximum(jnp.max(ss[2], axis=0, keepdims=True), sink_row))
            ps = [jnp.exp(s - m) for s in ss]
            denom = (jnp.sum(ps[0], axis=0, keepdims=True) + jnp.sum(ps[1], axis=0, keepdims=True)
                     + jnp.sum(ps[2], axis=0, keepdims=True) + jnp.exp(sink_row - m))
            ot = (_dot(vt_ref[idx[0]], ps[0].astype(MM)) + _dot(vt_ref[idx[1]], ps[1].astype(MM))
                  + _dot(vt_ref[idx[2]], ps[2].astype(MM)))
            o_ref[rows, :] = _unpad_heads(ot * (1.0 / denom), kv)
            lse_ref[0, n] = jnp.broadcast_to(m + jnp.log(denom), (8, GROUP * BLOCK))
            return carry

        lax.fori_loop(0, nblk, block, 0, unroll=8)

    both = pl.BlockSpec((nblk, BLOCK, KV_WIDTH), lambda k, b: (b, 0, 0))
    return _call(
        body, (sink, qtb, kb3, vtb, bias), name="attn_b_fwd", jobs=jobs,
        grid=(N_KV, batch),
        in_specs=[SMEM, pl.BlockSpec((nblk, GROUP * HEAD_DIM, BLOCK), lambda k, b: (b, k, 0)), both, both,
                  pl.BlockSpec((1, 3, BLOCK, GROUP * BLOCK), lambda k, b: (k, 0, 0, 0))],
        out_specs=[pl.BlockSpec((seq, GROUP * HEAD_DIM), lambda k, b: (b, k)),
                   pl.BlockSpec((1, nblk, 8, GROUP * BLOCK), lambda k, b: (k, b, 0, 0))],
        out_shape=[jax.ShapeDtypeStruct((tokens, Q_WIDTH), F32),
                   jax.ShapeDtypeStruct((N_KV, nblk_all, 8, GROUP * BLOCK), F32)],
        params=_cparams(("arbitrary", "arbitrary"), 48))


def _attn_b_bwd(qtb, kb3, ktb, vb3, do, o, lse, bias, sink, *, seq, jobs=()):
    nblk_all = qtb.shape[0]
    tokens = nblk_all * BLOCK
    batch = tokens // seq
    nblk = seq // BLOCK

    def body(sink_ref, q_ref, k_ref, kt_ref, v_ref, do_ref, o_ref, lse_ref, bias_ref,
             dq_ref, dk_ref, dv_ref, dbias_ref, dsink_ref):
        kv = pl.program_id(0)
        sink_row = _sink_row(sink_ref, kv)

        @pl.when(pl.program_id(1) == 0)
        def _():
            dbias_ref[...] = jnp.zeros_like(dbias_ref)
            dsink_ref[...] = jnp.zeros_like(dsink_ref)

        dk_ref[...] = jnp.zeros_like(dk_ref)
        dv_ref[...] = jnp.zeros_like(dv_ref)

        def block(n, dsink):
            idx = (jnp.maximum(n - 1, 0), n, jnp.minimum(n + 1, nblk - 1))
            rows = pl.ds(pl.multiple_of(n * BLOCK, BLOCK), BLOCK)
            qpad = _pad_heads(q_ref[n], kv)
            dot_t = do_ref[rows, :].T
            prod = dot_t * o_ref[rows, :].T
            delta = jnp.concatenate(
                [jnp.sum(prod[g * HEAD_DIM:(g + 1) * HEAD_DIM, :], axis=0, keepdims=True) for g in range(GROUP)],
                axis=-1)
            dopad = _pad_heads(dot_t.astype(MM), kv)
            lse_row = lse_ref[0, n][0:1, :]
            ss = _window_scores_t(k_ref, idx, qpad, bias_ref, n, nblk)
            dqt = jnp.zeros((KV_WIDTH, GROUP * BLOCK), F32)
            for piece in range(3):
                pt = jnp.exp(ss[piece] - lse_row)
                dst = pt * (_dot(v_ref[idx[piece]], dopad) - delta)
                dsb = dst.astype(MM)
                dbias_ref[0, piece] += dst
                dqt = dqt + _dot(kt_ref[idx[piece]], dsb)
                dk_ref[0, idx[piece]] += _dot_nt(dsb, qpad)
                dv_ref[0, idx[piece]] += _dot_nt(pt.astype(MM), dopad)
            dq_ref[rows, :] = _unpad_heads(dqt, kv)
            return dsink - jnp.exp(sink_row - lse_row) * delta

        dsink = lax.fori_loop(
            0, nblk // 4, lambda i, c: block(4 * i + 3, block(4 * i + 2, block(4 * i + 1, block(4 * i, c)))),
            jnp.zeros((1, GROUP * BLOCK), F32))
        dsink_ref[0] += jnp.broadcast_to(dsink, (8, GROUP * BLOCK))

    qspec = pl.BlockSpec((seq, GROUP * HEAD_DIM), lambda k, b: (b, k))
    both = pl.BlockSpec((nblk, BLOCK, KV_WIDTH), lambda k, b: (b, 0, 0))
    grad = pl.BlockSpec((1, nblk, BLOCK, KV_WIDTH), lambda k, b: (k, b, 0, 0))
    return _call(
        body, (sink, qtb, kb3, ktb, vb3, do, o, lse, bias), name="attn_b_bwd", jobs=jobs,
        grid=(N_KV, batch),
        in_specs=[SMEM, pl.BlockSpec((nblk, GROUP * HEAD_DIM, BLOCK), lambda k, b: (b, k, 0)), both, both, both,
                  qspec, qspec, pl.BlockSpec((1, nblk, 8, GROUP * BLOCK), lambda k, b: (k, b, 0, 0)),
                  pl.BlockSpec((1, 3, BLOCK, GROUP * BLOCK), lambda k, b: (k, 0, 0, 0))],
        out_specs=[qspec, grad, grad,
                   pl.BlockSpec((1, 3, BLOCK, GROUP * BLOCK), lambda k, b: (k, 0, 0, 0)),
                   pl.BlockSpec((1, 8, GROUP * BLOCK), lambda k, b: (k, 0, 0))],
        out_shape=[jax.ShapeDtypeStruct((tokens, Q_WIDTH), F32),
                   jax.ShapeDtypeStruct((N_KV, nblk_all, BLOCK, KV_WIDTH), F32),
                   jax.ShapeDtypeStruct((N_KV, nblk_all, BLOCK, KV_WIDTH), F32),
                   jax.ShapeDtypeStruct((N_KV, 3, BLOCK, GROUP * BLOCK), F32),
                   jax.ShapeDtypeStruct((N_KV, 8, GROUP * BLOCK), F32)],
        params=_cparams(("arbitrary", "arbitrary"), 48))


def _wo_post(oa, ob, w_o, x, g2, g3, *, tm, sub):
    tokens = x.shape[0]

    def body(oa_ref, ob_ref, w_ref, x_ref, g2_ref, g3_ref, mix_ref, x1_ref, h2_ref, o_ref):
        for r in range(tm // sub):
            rows = slice(r * sub, (r + 1) * sub)
            o = jnp.concatenate([oa_ref[rows, :].astype(MM), ob_ref[rows, :].astype(MM)], axis=-1)
            o_ref[rows, :] = o
            mix = _dot(o, w_ref[...])
            mix_ref[rows, :] = mix
            x1 = x_ref[rows, :] + mix * _rms_r(mix) * g2_ref[...]
            x1_ref[rows, :] = x1
            h2_ref[rows, :] = (x1 * _rms_r(x1) * g3_ref[...]).astype(MM)

    tok = lambda w: pl.BlockSpec((tm, w), lambda i: (i, 0))
    row = pl.BlockSpec((1, D_MODEL), lambda i: (0, 0))
    return pl.pallas_call(
        body, name="wo_post",
        grid=(tokens // tm,),
        in_specs=[tok(Q_WIDTH), tok(Q_WIDTH), pl.BlockSpec((D_MODEL, D_MODEL), lambda i: (0, 0)),
                  tok(D_MODEL), row, row],
        out_specs=[tok(D_MODEL), tok(D_MODEL), tok(D_MODEL), tok(D_MODEL)],
        out_shape=[jax.ShapeDtypeStruct((tokens, D_MODEL), F32),
                   jax.ShapeDtypeStruct((tokens, D_MODEL), F32),
                   jax.ShapeDtypeStruct((tokens, D_MODEL), MM),
                   jax.ShapeDtypeStruct((tokens, D_MODEL), MM)],
        compiler_params=_cparams(("parallel",), 40),
    )(oa, ob, w_o, x, g2, g3)


def _resident(shape):
    return pl.BlockSpec(shape, lambda i: (0,) * len(shape), pipeline_mode=pl.Buffered(1))


def _ffn_fwd_loss(h2, w_up, w_down, x1, target, g4, *, tm):
    tokens = h2.shape[0]
    nt = tokens // tm

    def body(h2_ref, wu_ref, wd_ref, x1_ref, t_ref, g4_ref, u_ref, df_ref, dy_ref, loss_ref, dg4_ref):
        h2v = h2_ref[...]
        f = jnp.zeros((tm, D_MODEL), F32)
        for c in range(N_CHIPS):
            u = jnp.maximum(_dot(h2v, wu_ref[c]), 0.0)
            u_ref[:, c * FF_CHUNK:(c + 1) * FF_CHUNK] = u.astype(MM)
            f = f + _dot((u * u).astype(MM), wd_ref[c * FF_CHUNK:(c + 1) * FF_CHUNK, :])
        r = _rms_r(f)
        g4v = g4_ref[...]
        err = x1_ref[...] + f * r * g4v - t_ref[...]
        sq = jnp.sum(err * err, axis=-1, keepdims=True)
        loss_ref[0] = jnp.broadcast_to(jnp.sum(sq, axis=0, keepdims=True) * (0.5 / D_MODEL), (8, LANES))
        dy = err * (1.0 / D_MODEL)
        dy_ref[...] = dy
        dfv, dgv = _rms_bwd(f, r, g4v, dy)
        df_ref[...] = dfv.astype(MM)
        dg4_ref[0] = jnp.sum(dgv, axis=0, keepdims=True)

    tok = pl.BlockSpec((tm, D_MODEL), lambda i: (i, 0))
    return pl.pallas_call(
        body, name="ffn_fwd_loss",
        grid=(nt,),
        in_specs=[tok, _resident((N_CHIPS, D_MODEL, FF_CHUNK)), _resident((D_FF, D_MODEL)),
                  tok, tok, pl.BlockSpec((1, D_MODEL), lambda i: (0, 0))],
        out_specs=[pl.BlockSpec((tm, D_FF), lambda i: (i, 0)), tok, tok,
                   pl.BlockSpec((1, 8, LANES), lambda i: (i, 0, 0)),
                   pl.BlockSpec((1, 1, D_MODEL), lambda i: (i, 0, 0))],
        out_shape=[jax.ShapeDtypeStruct((tokens, D_FF), MM),
                   jax.ShapeDtypeStruct((tokens, D_MODEL), MM),
                   jax.ShapeDtypeStruct((tokens, D_MODEL), F32),
                   jax.ShapeDtypeStruct((nt, 8, LANES), F32),
                   jax.ShapeDtypeStruct((nt, 1, D_MODEL), F32)],
        compiler_params=_cparams(("parallel",), 56),
    )(h2, w_up, w_down, x1, target, g4)


def _ffn_bwd_act(df, w_down, u, w_up, x1, dy, mix, g3, g2, *, tm):
    tokens = df.shape[0]
    nt = tokens // tm

    def body(df_ref, wd_ref, u_ref, wu_ref, x1_ref, dy_ref, mix_ref, g3_ref, g2_ref,
             dz_ref, dx1_ref, dmix_ref, dg3_ref, dg2_ref):
        dfv = df_ref[...]
        dh2 = jnp.zeros((tm, D_MODEL), F32)
        for c in range(N_CHIPS):
            cols = slice(c * FF_CHUNK, (c + 1) * FF_CHUNK)
            da = _dot_nt(dfv, wd_ref[cols, :])
            dz = (da * (2.0 * u_ref[:, cols].astype(F32))).astype(MM)
            dz_ref[:, cols] = dz
            dh2 = dh2 + _dot_nt(dz, wu_ref[c])
        x1 = x1_ref[...]
        dxn, dg3v = _rms_bwd(x1, _rms_r(x1), g3_ref[...], dh2)
        dx1 = dy_ref[...] + dxn
        dx1_ref[...] = dx1
        dg3_ref[0] = jnp.sum(dg3v, axis=0, keepdims=True)
        mix = mix_ref[...]
        dmix, dg2v = _rms_bwd(mix, _rms_r(mix), g2_ref[...], dx1)
        dmix_ref[...] = dmix.astype(MM)
        dg2_ref[0] = jnp.sum(dg2v, axis=0, keepdims=True)

    tok = pl.BlockSpec((tm, D_MODEL), lambda i: (i, 0))
    wide = pl.BlockSpec((tm, D_FF), lambda i: (i, 0))
    row = pl.BlockSpec((1, D_MODEL), lambda i: (0, 0))
    part = pl.BlockSpec((1, 1, D_MODEL), lambda i: (i, 0, 0))
    return pl.pallas_call(
        body, name="ffn_bwd_act",
        grid=(nt,),
        in_specs=[tok, _resident((D_FF, D_MODEL)), wide, _resident((N_CHIPS, D_MODEL, FF_CHUNK)),
                  tok, tok, tok, row, row],
        out_specs=[wide, tok, tok, part, part],
        out_shape=[jax.ShapeDtypeStruct((tokens, D_FF), MM),
                   jax.ShapeDtypeStruct((tokens, D_MODEL), F32),
                   jax.ShapeDtypeStruct((tokens, D_MODEL), MM),
                   jax.ShapeDtypeStruct((nt, 1, D_MODEL), F32),
                   jax.ShapeDtypeStruct((nt, 1, D_MODEL), F32)],
        compiler_params=_cparams(("parallel",), 56),
    )(df, w_down, u, w_up, x1, dy, mix, g3, g2)


def _tn_matmul(a, b, *, name, tm, tn, tk, chunked=False, square_a=False, vmem_mb=48, jobs=()):
    tokens, m_dim = a.shape
    n_dim = b.shape[1]
    if chunked:
        assert tm == m_dim

    def body(a_ref, b_ref, o_ref):
        av = a_ref[...]
        if square_a:
            av = av.astype(F32)
            av = av * av
        part = _dot_tn(av.astype(MM), b_ref[...].astype(MM))
        part = part[None] if chunked else part

        @pl.when(pl.program_id(2) == 0)
        def _():
            o_ref[...] = part

        @pl.when(pl.program_id(2) > 0)
        def _():
            o_ref[...] += part

    if chunked:
        out_spec = pl.BlockSpec((1, tm, tn), lambda i, j, k: (j, 0, 0))
        out_shape = jax.ShapeDtypeStruct((n_dim // tn, m_dim, tn), F32)
    else:
        out_spec = pl.BlockSpec((tm, tn), lambda i, j, k: (i, j))
        out_shape = jax.ShapeDtypeStruct((m_dim, n_dim), F32)
    (out,), job_res = _call(
        body, (a, b), name=name, jobs=jobs,
        grid=(m_dim // tm, n_dim // tn, tokens // tk),
        in_specs=[pl.BlockSpec((tk, tm), lambda i, j, k: (k, i)),
                  pl.BlockSpec((tk, tn), lambda i, j, k: (k, j))],
        out_specs=[out_spec], out_shape=[_in_hbm(out_shape)],
        params=_cparams(("arbitrary", "arbitrary", "arbitrary"), vmem_mb))
    return out, job_res


def _wo_bwd(dmix, w_o, *, tm):
    tokens = dmix.shape[0]

    def body(dm_ref, w_ref, doa_ref, dob_ref):
        dm = dm_ref[...]
        doa_ref[...] = _dot_nt(dm, w_ref[0:Q_WIDTH, :])
        dob_ref[...] = _dot_nt(dm, w_ref[Q_WIDTH:D_MODEL, :])

    tok = lambda w: pl.BlockSpec((tm, w), lambda i: (i, 0))
    return pl.pallas_call(
        body, name="wo_bwd",
        grid=(tokens // tm,),
        in_specs=[tok(D_MODEL), pl.BlockSpec((D_MODEL, D_MODEL), lambda i: (0, 0))],
        out_specs=[tok(Q_WIDTH), tok(Q_WIDTH)],
        out_shape=[jax.ShapeDtypeStruct((tokens, Q_WIDTH), F32)] * 2,
        compiler_params=_cparams(("parallel",), 40),
    )(dmix, w_o)


def _proj_bwd(dqa, dkta, dvta, dqb, dktb, dvtb, raw, x, dx1, g1, w_in, gq, gk, cq, sq, ck, sk, *, seq, tm, sub,
              jobs=()):
    tokens = x.shape[0]
    nt = tokens // tm
    n_seq = seq // tm
    nblk = tm // BLOCK

    def body(dqa_ref, dkta_ref, dvta_ref, dqb_ref, dkb_ref, dvb_ref, raw_ref, x_ref, dx1_ref, g1_ref, w_ref,
             gq_ref, gk_ref, cq_ref, sq_ref, ck_ref, sk_ref,
             gx_ref, dproj_ref, dg1_ref, dgq_ref, dgk_ref, dp):
        parts = []
        for r in range(tm // sub):
            rows = slice(r * sub, (r + 1) * sub)
            qa = raw_ref[rows, 0:Q_WIDTH]
            dqn = _rope_t(dqa_ref[rows, :], cq_ref[rows, :], sq_ref[rows, :])
            rq = _head_r(qa)
            nq = qa * rq
            dnq = dqn * gq_ref[...]
            dp[rows, 0:Q_WIDTH] = rq * (dnq - nq * (_seg64_sum(dnq * nq) * (1.0 / HEAD_DIM)))

            ka = raw_ref[rows, Q_WIDTH:QK_RAW]
            dkn = _rope_t(dkta_ref[0, :, rows].T, ck_ref[rows, :], sk_ref[rows, :])
            rk = _head_r(ka)
            nk = ka * rk
            dnk = dkn * gk_ref[...]
            dp[rows, 512:640] = rk * (dnk - nk * (_seg64_sum(dnk * nk) * (1.0 / HEAD_DIM)))

            dp[rows, 640:768] = dvta_ref[0, :, rows].T
            dp[rows, 768:1280] = dqb_ref[rows, :] * SCALE
            for j in range(r * sub // BLOCK, (r + 1) * sub // BLOCK):
                dp[j * BLOCK:(j + 1) * BLOCK, 1280:1408] = dkb_ref[0, j] + dkb_ref[1, j]
                dp[j * BLOCK:(j + 1) * BLOCK, 1408:1536] = dvb_ref[0, j] + dvb_ref[1, j]

            dproj = dp[rows, :].astype(MM)
            dproj_ref[rows, :] = dproj
            dh1 = _dot_nt(dproj[:, 0:IN_CHUNK], w_ref[0])
            for j in range(1, N_CHIPS):
                dh1 = dh1 + _dot_nt(dproj[:, j * IN_CHUNK:(j + 1) * IN_CHUNK], w_ref[j])
            xv = x_ref[rows, :]
            dxn, dg1v = _rms_bwd(xv, _rms_r(xv), g1_ref[...], dh1)
            gx_ref[rows, :] = dx1_ref[rows, :] + dxn
            parts.append((jnp.sum(dqn * nq, axis=0, keepdims=True), jnp.sum(dkn * nk, axis=0, keepdims=True),
                          jnp.sum(dg1v, axis=0, keepdims=True)))
        dgq_ref[0] = functools.reduce(jnp.add, [p[0] for p in parts])
        dgk_ref[0] = functools.reduce(jnp.add, [p[1] for p in parts])
        dg1_ref[0] = functools.reduce(jnp.add, [p[2] for p in parts])

    tok = lambda w: pl.BlockSpec((tm, w), lambda i: (i, 0))
    tab = lambda w: pl.BlockSpec((tm, w), lambda i: (i % n_seq, 0))
    row = lambda w: pl.BlockSpec((1, w), lambda i: (0, 0))
    tposed = pl.BlockSpec((1, KV_WIDTH, tm), lambda i: (i // n_seq, 0, i % n_seq))
    blocks = pl.BlockSpec((N_KV, nblk, BLOCK, KV_WIDTH), lambda i: (0, i, 0, 0))
    part = lambda w: pl.BlockSpec((1, 1, w), lambda i: (i, 0, 0))
    return _call(
        body, (dqa, dkta, dvta, dqb, dktb, dvtb, raw, x, dx1, g1, w_in, gq, gk, cq, sq, ck, sk),
        name="proj_bwd", jobs=jobs,
        grid=(nt,),
        in_specs=[tok(Q_WIDTH), tposed, tposed, tok(Q_WIDTH), blocks, blocks, tok(QK_RAW), tok(D_MODEL),
                  tok(D_MODEL), row(D_MODEL),
                  pl.BlockSpec((N_CHIPS, D_MODEL, IN_CHUNK), lambda i: (0, 0, 0)),
                  row(Q_WIDTH), row(KV_WIDTH), tab(KV_WIDTH), tab(KV_WIDTH), tab(KV_WIDTH), tab(KV_WIDTH)],
        out_specs=[tok(D_MODEL), tok(IN_TOTAL), part(D_MODEL), part(Q_WIDTH), part(KV_WIDTH)],
        out_shape=[jax.ShapeDtypeStruct((tokens, D_MODEL), F32),
                   jax.ShapeDtypeStruct((tokens, IN_TOTAL), MM),
                   jax.ShapeDtypeStruct((nt, 1, D_MODEL), F32),
                   jax.ShapeDtypeStruct((nt, 1, Q_WIDTH), F32),
                   jax.ShapeDtypeStruct((nt, 1, KV_WIDTH), F32)],
        scratch_shapes=[pltpu.VMEM((tm, IN_TOTAL), F32)],
        params=_cparams(("arbitrary",), 56))


def _pack_small(dg1, dg2, dg3, dg4, dgq, dgk, dsink, dbias, bucket, loss):
    def body(dg1_ref, dg2_ref, dg3_ref, dg4_ref, dgq_ref, dgk_ref, dsink_ref, dbias_ref, bucket_ref, loss_ref,
             out_ref, rel_ref):
        out_ref[...] = jnp.zeros_like(out_ref)
        for r, ref in ((ROW_G1, dg1_ref), (ROW_G2, dg2_ref), (ROW_G3, dg3_ref), (ROW_G4, dg4_ref)):
            acc = ref[0]
            for t in range(1, ref.shape[0]):
                acc = acc + ref[t]
            out_ref[r:r + 1, :] = acc

        def fold(ref, heads):
            acc = ref[0]
            for t in range(1, ref.shape[0]):
                acc = acc + ref[t]
            tot = acc[:, 0:HEAD_DIM]
            for h in range(1, heads):
                tot = tot + acc[:, h * HEAD_DIM:(h + 1) * HEAD_DIM]
            return tot

        out_ref[ROW_MISC:ROW_MISC + 1, MISC_GQ:MISC_GQ + HEAD_DIM] = fold(dgq_ref, GROUP * N_KV)
        out_ref[ROW_MISC:ROW_MISC + 1, MISC_GK:MISC_GK + HEAD_DIM] = fold(dgk_ref, N_KV)
        for h in range(GROUP * N_KV):
            g = h % GROUP
            out_ref[ROW_MISC:ROW_MISC + 1, MISC_SINK + h:MISC_SINK + h + 1] = jnp.sum(
                dsink_ref[h // GROUP, 0:1, g * BLOCK:(g + 1) * BLOCK], axis=-1, keepdims=True)
        lacc = loss_ref[0, 0:1, 0:1]
        for t in range(1, loss_ref.shape[0]):
            lacc = lacc + loss_ref[t, 0:1, 0:1]
        out_ref[ROW_MISC:ROW_MISC + 1, MISC_LOSS:MISC_LOSS + 1] = lacc
        lane = lax.broadcasted_iota(jnp.int32, (N_BUCKETS, LANES), 1)
        row = lax.broadcasted_iota(jnp.int32, (N_BUCKETS, LANES), 0)

        def per_bucket(b, acc):
            for h in range(GROUP * N_KV):
                g = h % GROUP
                sel = jnp.zeros((BLOCK, BLOCK), F32)
                for piece in range(3):
                    sel = sel + jnp.where(bucket_ref[piece] == b,
                                          dbias_ref[h // GROUP, piece, :, g * BLOCK:(g + 1) * BLOCK], 0.0)
                tot = jnp.sum(jnp.sum(sel, axis=0, keepdims=True), axis=-1, keepdims=True)
                acc = jnp.where((row == b) & (lane == h), tot, acc)
            return acc

        rel_ref[...] = lax.fori_loop(0, N_BUCKETS, per_bucket, jnp.zeros((N_BUCKETS, LANES), F32))

    args = (dg1, dg2, dg3, dg4, dgq, dgk, dsink, dbias, bucket, loss)
    outs = [jax.ShapeDtypeStruct((8, D_MODEL), F32), jax.ShapeDtypeStruct((N_BUCKETS, LANES), F32)]
    return pl.pallas_call(
        body, name="pack_small", grid=(1,),
        in_specs=[_whole(a) for a in args], out_specs=[_whole(o) for o in outs], out_shape=outs,
        compiler_params=pltpu.CompilerParams(vmem_limit_bytes=32 * 1024 * 1024),
    )(*args)


def _gather_weights(shards, whole):
    n = len(shards)
    full = [t for t in range(n) if whole[t]]

    def body(*refs):
        ins, outs = refs[:n], refs[n:2 * n]
        raw, stage = refs[2 * n:3 * n], refs[3 * n:4 * n]
        load_sem, local_sem, ici_send, ici_recv, d2d_send, d2d_recv = refs[4 * n:]
        x, y, c = _place()
        k = 2 * x + y
        sibling = (x, y, 1 - c)
        order = full + [t for t in range(n) if t not in full]
        loads = {t: pltpu.make_async_copy(ins[t], raw[t], load_sem.at[t]) for t in order}
        for t in order:
            loads[t].start()
        copies, sends = [], []
        for t in order:
            loads[t].wait()
            stage[t][...] = raw[t][...].astype(MM)
            mine = pltpu.make_async_copy(stage[t], outs[t].at[k], local_sem.at[t])
            mine.start()
            copies.append(mine)
            if t in full:
                half = ins[t].shape[0] // 2
                rows = pl.ds(c * half, half)
                for r, (fx, fy) in enumerate(_CHIP_FLIPS):
                    cp = _remote(stage[t].at[rows], outs[t].at[k, rows], ici_send.at[t, r], ici_recv.at[t, r],
                                 (_flip(x, fx), _flip(y, fy), c))
                    cp.start()
                    sends.append(cp)
        for t in full:
            half = ins[t].shape[0] // 2
            rows = pl.ds(c * half, half)
            for r, (fx, fy) in enumerate(_CHIP_FLIPS):
                kk = 2 * _flip(x, fx) + _flip(y, fy)
                landed = outs[t].at[kk, rows]
                _remote(landed, landed, ici_send.at[t, r], ici_recv.at[t, r], sibling).wait_recv()
                fwd = _remote(landed, landed, d2d_send.at[t, r], d2d_recv.at[t, r], sibling)
                fwd.start()
                sends.append(fwd)
        for t in full:
            half = ins[t].shape[0] // 2
            other = pl.ds((1 - c) * half, half)
            for r, (fx, fy) in enumerate(_CHIP_FLIPS):
                kk = 2 * _flip(x, fx) + _flip(y, fy)
                theirs = outs[t].at[kk, other]
                _remote(theirs, theirs, d2d_send.at[t, r], d2d_recv.at[t, r], sibling).wait_recv()
        for cp in sends:
            cp.wait_send()
        for cp in copies:
            cp.wait()

    return pl.pallas_call(
        body, name="gather_weights",
        in_specs=[HBM] * n, out_specs=[HBM] * n,
        out_shape=[pltpu.HBM((N_CHIPS,) + s.shape, MM) for s in shards],
        scratch_shapes=[pltpu.VMEM(s.shape, F32) for s in shards] + [pltpu.VMEM(s.shape, MM) for s in shards] + [
            pltpu.SemaphoreType.DMA((n,)), pltpu.SemaphoreType.DMA((n,)),
            pltpu.SemaphoreType.DMA((n, 3)), pltpu.SemaphoreType.DMA((n, 3)),
            pltpu.SemaphoreType.DMA((n, 3)), pltpu.SemaphoreType.DMA((n, 3))],
        compiler_params=pltpu.CompilerParams(vmem_limit_bytes=40 * 1024 * 1024),
    )(*shards)


def _add_half(grad, got, where, *, name, tr):
    nch, half, cols = got.shape
    nblk = half // tr

    def body(where_ref, g_ref, r_ref, o_ref):
        o_ref[...] = (g_ref[...] + r_ref[...]).astype(MM)

    return pl.pallas_call(
        body, name=name,
        grid_spec=pltpu.PrefetchScalarGridSpec(
            num_scalar_prefetch=1, grid=(nch, nblk),
            in_specs=[pl.BlockSpec((1, tr, cols), lambda j, i, where_ref: (j, where_ref[1] * nblk + i, 0)),
                      pl.BlockSpec((1, tr, cols), lambda j, i, where_ref: (j, i, 0))],
            out_specs=pl.BlockSpec((1, tr, cols), lambda j, i, where_ref: (j, i, 0))),
        out_shape=jax.ShapeDtypeStruct(got.shape, MM),
        compiler_params=_cparams(("parallel", "parallel"), 32),
    )(where, grad, got)


def _add_chips(own, got, where, *, name, tr):
    _, half, cols = own.shape
    nblk = half // tr

    def body(where_ref, o_ref, g_ref, out_ref):
        f = lambda v: v.astype(F32)
        out_ref[...] = ((f(o_ref[0]) + f(g_ref[0])) + f(g_ref[1])) + f(g_ref[2])

    return pl.pallas_call(
        body, name=name,
        grid_spec=pltpu.PrefetchScalarGridSpec(
            num_scalar_prefetch=1, grid=(nblk,),
            in_specs=[pl.BlockSpec((1, tr, cols), lambda i, where_ref: (where_ref[0], i, 0)),
                      pl.BlockSpec((3, tr, cols), lambda i, where_ref: (0, i, 0))],
            out_specs=pl.BlockSpec((tr, cols), lambda i, where_ref: (where_ref[1] * nblk + i, 0))),
        out_shape=pltpu.HBM((2 * half, cols), F32),
        compiler_params=_cparams(("parallel",), 32),
    )(where, own, got)


def _small_job(tiles):
    n = len(tiles)

    def copies(ins, outs, sems):
        x, y, c = _place()
        me = 4 * x + 2 * y + c
        local, send, recv = sems
        cps = []
        for t in range(n):
            cps.append(pltpu.make_async_copy(ins[t], outs[t].at[me], local.at[t]))
            for r in range(1, N_DEV):
                fx, fy, fc = (r >> 2) & 1, (r >> 1) & 1, r & 1
                cps.append(_remote(ins[t], outs[t].at[me], send.at[t, r - 1], recv.at[t, r - 1],
                                   (_flip(x, fx), _flip(y, fy), _flip(c, fc))))
        return cps

    return _Job(tiles, [jax.ShapeDtypeStruct((N_DEV,) + t.shape, F32) for t in tiles],
                [pltpu.SemaphoreType.DMA((n,)), pltpu.SemaphoreType.DMA((n, N_DEV - 1)),
                 pltpu.SemaphoreType.DMA((n, N_DEV - 1))], copies)


def _adamw_math(w, g, m, v):
    m = ADAM_B1 * m + (1.0 - ADAM_B1) * g
    v = ADAM_B2 * v + (1.0 - ADAM_B2) * (g * g)
    m_hat = m / (1.0 - ADAM_B1 ** ADAM_STEP)
    v_hat = v / (1.0 - ADAM_B2 ** ADAM_STEP)
    delta = -ADAM_LR * (m_hat / (jnp.sqrt(v_hat) + ADAM_EPS) + ADAM_WD * w)
    return delta, m, v


def _adamw(w, g, m, v, *, name, tr):
    rows, cols = w.shape

    def body(w_ref, g_ref, m_ref, v_ref, go_ref, d_ref, nm_ref, nv_ref):
        g = g_ref[...]
        go_ref[...] = g
        d_ref[...], nm_ref[...], nv_ref[...] = _adamw_math(w_ref[...], g, m_ref[...], v_ref[...])

    spec = pl.BlockSpec((tr, cols), lambda i: (i, 0))
    return pl.pallas_call(
        body, name=name,
        grid=(rows // tr,),
        in_specs=[spec] * 4, out_specs=[spec] * 4,
        out_shape=[jax.ShapeDtypeStruct(w.shape, F32)] * 4,
        compiler_params=_cparams(("parallel",), 32),
    )(w, g, m, v)


def _small_adamw(gathered, gathered_rel, params, moments_m, moments_v):
    n = len(params)

    def body(all_ref, rel_all_ref, *refs):
        w_refs, m_refs, v_refs = refs[:n], refs[n:2 * n], refs[2 * n:3 * n]
        loss_ref = refs[3 * n]
        out_refs = refs[3 * n + 1:]
        g = all_ref[0]
        rel = rel_all_ref[0]
        for d in range(1, N_DEV):
            g = g + all_ref[d]
            rel = rel + rel_all_ref[d]
        misc = g[ROW_MISC:ROW_MISC + 1]
        loss_ref[...] = misc[:, MISC_LOSS:MISC_LOSS + 1]
        grads = (g[ROW_G1:ROW_G1 + 1], g[ROW_G2:ROW_G2 + 1], g[ROW_G3:ROW_G3 + 1], g[ROW_G4:ROW_G4 + 1],
                 misc[:, MISC_GQ:MISC_GQ + HEAD_DIM], misc[:, MISC_GK:MISC_GK + HEAD_DIM],
                 misc[:, MISC_SINK:MISC_SINK + GROUP * N_KV], rel[:, 0:GROUP * N_KV])
        for i in range(n):
            d, nm, nv = _adamw_math(w_refs[i][...], grads[i], m_refs[i][...], v_refs[i][...])
            for j, val in enumerate((grads[i], d, nm, nv)):
                out_refs[4 * i + j][...] = val

    args = (gathered, gathered_rel, *params, *moments_m, *moments_v)
    out_shape = [jax.ShapeDtypeStruct((1, 1), F32)] + [jax.ShapeDtypeStruct(p.shape, F32) for p in params
                                                       for _ in range(4)]
    outs = pl.pallas_call(
        body, name="small_adamw", grid=(1,),
        in_specs=[_whole(a) for a in args], out_specs=[_whole(o) for o in out_shape], out_shape=out_shape,
    )(*args)
    return outs[0], [outs[1 + 4 * i:5 + 4 * i] for i in range(n)]


def kernel(x, w_in, w_o, g_pre_mix, g_post_mix, q_norm_a, k_norm_a, sink_b, rel_bias, g_pre_ffn, w_ffn_up, w_ffn_down, g_post_ffn, loss_target, m_w_in, m_w_o, m_g_pre_mix, m_g_post_mix, m_q_norm_a, m_k_norm_a, m_sink_b, m_rel_bias, m_g_pre_ffn, m_w_ffn_up, m_w_ffn_down, m_g_post_ffn, v_w_in, v_w_o, v_g_pre_mix, v_g_post_mix, v_q_norm_a, v_k_norm_a, v_sink_b, v_rel_bias, v_g_pre_ffn, v_w_ffn_up, v_w_ffn_down, v_g_post_ffn):
    batch, seq, _ = x.shape
    tokens = batch * seq
    where = jnp.stack([2 * lax.axis_index("x") + lax.axis_index("y"), lax.axis_index("c")]).astype(jnp.int32)
    x2 = x.reshape(tokens, D_MODEL)
    g1, g2, g3, g4 = g_pre_mix, g_post_mix, g_pre_ffn, g_post_ffn

    cos, sin = _rope_tables(seq)
    ck, sk = jnp.tile(cos, (1, 2)), jnp.tile(sin, (1, 2))
    cq, sq = ck * SCALE, sk * SCALE
    gq8, gk2 = jnp.tile(q_norm_a, (1, 8)), jnp.tile(k_norm_a, (1, 2))
    bucket, band = _window_tables()
    bias = _bias_build(rel_bias.T, bucket, band)

    w_in_g, w_o_p, w_up_p, w_down_p = _gather_weights(
        (w_in[0], w_o[0], w_ffn_up[0], w_ffn_down[0]), whole=(True, False, False, False))
    (h1, raw, qa, ka, kta, va, vta, qtb, kb, ktb, vb, vtb) = _pre_proj(
        x2, g1, w_in_g, gq8, gk2, cq, sq, ck, sk, seq=seq, tm=min(512, seq), sub=256)
    (oa, p_a, linv_a), (w_part,) = _attn_a_fwd(
        qa, kta, va, seq=seq, bq=min(256, seq), jobs=[_gather_job([w_o_p, w_up_p, w_down_p], forward=False)])
    kb3 = kb.reshape(tokens // BLOCK, BLOCK, KV_WIDTH)
    vb3 = vb.reshape(tokens // BLOCK, BLOCK, KV_WIDTH)
    (ob, lse_b), ((w_o_g, w_up_g, w_down_g),) = _attn_b_fwd(
        qtb, kb3, vtb, bias, sink_b, seq=seq, jobs=[_gather_job(w_part, forward=True)])
    w_o2 = w_o_g.reshape(D_MODEL, D_MODEL)
    w_down2 = w_down_g.reshape(D_FF, D_MODEL)
    mix, x1, h2, o_cat = _wo_post(oa, ob, w_o2, x2, g2, g3, tm=512, sub=256)
    u, df, dy, loss_t, dg4 = _ffn_fwd_loss(h2, w_up_g, w_down2, x1, loss_target.reshape(tokens, D_MODEL), g4, tm=256)

    dz, dx1, dmix, dg3, dg2 = _ffn_bwd_act(df, w_down2, u, w_up_g, x1, dy, mix, g3, g2, tm=256)
    gw_down, _ = _tn_matmul(u, df, name="grad_w_down", tm=1024, tn=1024, tk=min(2048, tokens), square_a=True)
    gw_down = gw_down.reshape(N_CHIPS, FF_CHUNK, D_MODEL)
    gw_up, ((got_down,),) = _tn_matmul(h2, dz, name="grad_w_up", tm=1024, tn=1024, tk=min(2048, tokens), chunked=True,
                                        jobs=[_swap_job([gw_down])])
    doa, dob = _wo_bwd(dmix, w_o2, tm=512)
    gw_o, _ = _tn_matmul(o_cat, dmix, name="grad_w_o", tm=1024, tn=1024, tk=min(2048, tokens))
    gw_o = gw_o.reshape(N_CHIPS, O_CHUNK, D_MODEL)
    sum_down = _add_half(gw_down, got_down, where, name="add_half_w_down", tr=128)
    (dqa, dkta, dvta), ((ex_down,), (got_up,)) = _attn_a_bwd(
        qa, ka, vta, doa, oa, p_a, linv_a, seq=seq, bq=min(256, seq),
        jobs=[_exchange_job([sum_down]), _swap_job([gw_up])])
    full_down = _add_chips(sum_down, ex_down, where, name="add_chips_w_down", tr=128)
    sum_up = _add_half(gw_up, got_up, where, name="add_half_w_up", tr=128)
    (dqb, dkb, dvb, dbias, dsink), ((ex_up,), (g_down,), (got_o,)) = _attn_b_bwd(
        qtb, kb3, ktb, vb3, dob, ob, lse_b, bias, sink_b, seq=seq,
        jobs=[_exchange_job([sum_up]), _join_job([full_down]), _swap_job([gw_o])])
    full_up = _add_chips(sum_up, ex_up, where, name="add_chips_w_up", tr=128)
    sum_o = _add_half(gw_o, got_o, where, name="add_half_w_o", tr=128)
    (grad_x, dproj, dg1, dgq, dgk), _ = _proj_bwd(
        dqa, dkta, dvta, dqb, dkb, dvb, raw, x2, dx1, g1, w_in_g, gq8, gk2, cq, sq, ck, sk,
        seq=seq, tm=min(512, seq), sub=128)
    packed, packed_rel = _pack_small(dg1, dg2, dg3, dg4, dgq, dgk, dsink, dbias, bucket, loss_t)
    gw_in, ((ex_o,), (g_up,), (gathered, gathered_rel)) = _tn_matmul(
        h1, dproj, name="grad_w_in", tm=1024, tn=IN_CHUNK, tk=min(2048, tokens), chunked=True,
        jobs=[_exchange_job([sum_o]), _join_job([full_up]), _small_job([packed, packed_rel])])
    full_o = _add_chips(sum_o, ex_o, where, name="add_chips_w_o", tr=128)

    (g_o,), (got_in,) = _run_jobs("tail_swap", [_join_job([full_o]), _swap_job([gw_in])])
    sum_in = _add_half(gw_in, got_in, where, name="add_half_w_in", tr=128)
    ((ex_in,),) = _run_jobs("tail_exchange", [_exchange_job([sum_in])])
    full_in = _add_chips(sum_in, ex_in, where, name="add_chips_w_in", tr=128)
    ((g_in,),) = _run_jobs("tail_join", [_join_job([full_in])])

    big = [[t[None] for t in _adamw(w[0], g, m[0], v[0], name="adamw_" + nm, tr=128)] for nm, w, g, m, v in (
        ("w_in", w_in, g_in, m_w_in, v_w_in), ("w_o", w_o, g_o, m_w_o, v_w_o),
        ("w_up", w_ffn_up, g_up, m_w_ffn_up, v_w_ffn_up), ("w_down", w_ffn_down, g_down, m_w_ffn_down, v_w_ffn_down))]

    loss, small = _small_adamw(
        gathered, gathered_rel,
        (g1, g2, g3, g4, q_norm_a, k_norm_a, sink_b, rel_bias),
        (m_g_pre_mix, m_g_post_mix, m_g_pre_ffn, m_g_post_ffn, m_q_norm_a, m_k_norm_a, m_sink_b, m_rel_bias),
        (v_g_pre_mix, v_g_post_mix, v_g_pre_ffn, v_g_post_ffn, v_q_norm_a, v_k_norm_a, v_sink_b, v_rel_bias))
    s_g1, s_g2, s_g3, s_g4, s_gq, s_gk, s_sink, s_rel = small

    def leaves(i):
        return (big[0][i], big[1][i], s_g1[i], s_g2[i], s_gq[i], s_gk[i], s_sink[i], s_rel[i], s_g3[i],
                big[2][i], big[3][i], s_g4[i])

    loss = loss.reshape(())
    return (loss, grad_x.reshape(batch, seq, D_MODEL), *leaves(0), *leaves(1), *leaves(2), *leaves(3))
```

```python
import functools

import jax
import jax.numpy as jnp
import numpy as np
from jax import lax
from jax.experimental import pallas as pl
from jax.experimental.pallas import tpu as pltpu

F32 = jnp.float32
MM = jnp.bfloat16

D_MODEL = 1024
HEAD_DIM = 64
N_KV = 2
GROUP = 4
Q_WIDTH = 512
KV_WIDTH = 128
D_FF = 4096
GRID_W = 64
BLOCK = 128
N_BUCKETS = 32
MAX_DISTANCE = 128
ROPE_THETA = 10000.0
EPS = 1e-6
NEG_INF = -1e30
SCALE = HEAD_DIM ** -0.5
IN_TOTAL = 1536
N_CHIPS = 4
N_DEV = 8
IN_CHUNK = IN_TOTAL // N_CHIPS
FF_CHUNK = D_FF // N_CHIPS
O_CHUNK = D_MODEL // N_CHIPS
QK_RAW = 640

ADAM_LR = 0.001
ADAM_B1 = 0.9
ADAM_B2 = 0.999
ADAM_EPS = 1e-08
ADAM_WD = 0.01
ADAM_STEP = 10

LANES = 128
MESH = pl.DeviceIdType.MESH
HBM = pl.BlockSpec(memory_space=pl.ANY)
VMEM = pl.BlockSpec(memory_space=pltpu.VMEM)
SMEM = pl.BlockSpec(memory_space=pltpu.SMEM)

ROW_G1, ROW_G2, ROW_G3, ROW_G4, ROW_MISC = 0, 1, 2, 3, 4
MISC_GQ, MISC_GK, MISC_SINK, MISC_LOSS = 0, 64, 128, 512


def _cparams(sem, vmem_mb):
    return pltpu.CompilerParams(dimension_semantics=sem, vmem_limit_bytes=vmem_mb * 1024 * 1024)


def _whole(a):
    return pl.BlockSpec(a.shape, lambda i: (0,) * len(a.shape))


def _from_hbm(a):
    return pltpu.with_memory_space_constraint(a, pltpu.HBM)


def _in_hbm(s):
    return pltpu.HBM(s.shape, s.dtype)


class _Job:
    def __init__(self, operands, out_shapes, sems, copies, alias=None):
        self.operands, self.out_shapes, self.sems, self.copies = list(operands), list(out_shapes), list(sems), copies
        self.alias = dict(alias or {})


def _place():
    return lax.axis_index("x"), lax.axis_index("y"), lax.axis_index("c")


_CHIP_FLIPS = ((1, 0), (0, 1), (1, 1))


def _flip(v, bit):
    return 1 - v if bit else v


def _remote(src, dst, send, recv, dev):
    return pltpu.make_async_remote_copy(src_ref=src, dst_ref=dst, send_sem=send, recv_sem=recv,
                                        device_id=dev, device_id_type=MESH)


def _swap_job(grads):
    n = len(grads)

    def copies(ins, outs, sems):
        x, y, c = _place()
        send, recv = sems
        cps = []
        for t in range(n):
            half = ins[t].shape[1] // 2
            cps.append(_remote(ins[t].at[:, pl.ds((1 - c) * half, half), :], outs[t], send.at[t], recv.at[t],
                               (x, y, 1 - c)))
        return cps

    shapes = [jax.ShapeDtypeStruct((g.shape[0], g.shape[1] // 2, g.shape[2]), F32) for g in grads]
    return _Job(grads, shapes, [pltpu.SemaphoreType.DMA((n,)), pltpu.SemaphoreType.DMA((n,))], copies)


def _exchange_job(sums):
    n = len(sums)

    def copies(ins, outs, sems):
        x, y, c = _place()
        send, recv = sems
        cps = []
        for t in range(n):
            for r, (fx, fy) in enumerate(_CHIP_FLIPS):
                kk = 2 * _flip(x, fx) + _flip(y, fy)
                cps.append(_remote(ins[t].at[kk], outs[t].at[r], send.at[t, r], recv.at[t, r],
                                   (_flip(x, fx), _flip(y, fy), c)))
        return cps

    shapes = [jax.ShapeDtypeStruct((3,) + s.shape[1:], s.dtype) for s in sums]
    return _Job(sums, shapes, [pltpu.SemaphoreType.DMA((n, 3)), pltpu.SemaphoreType.DMA((n, 3))], copies)


def _join_job(fulls):
    n = len(fulls)

    def copies(ins, outs, sems):
        x, y, c = _place()
        send, recv = sems
        cps = []
        for t in range(n):
            half = ins[t].shape[0] // 2
            rows = pl.ds(c * half, half)
            cps.append(_remote(ins[t].at[rows], outs[t].at[rows], send.at[t], recv.at[t], (x, y, 1 - c)))
        return cps

    shapes = [jax.ShapeDtypeStruct(f.shape, f.dtype) for f in fulls]
    return _Job(fulls, shapes, [pltpu.SemaphoreType.DMA((n,)), pltpu.SemaphoreType.DMA((n,))], copies,
                alias={t: t for t in range(n)})


def _gather_job(bufs, forward):
    n = len(bufs)

    def copies(ins, outs, sems):
        x, y, c = _place()
        send, recv = sems
        cps = []
        for t in range(n):
            half = ins[t].shape[1] // 2
            rows = pl.ds(c * half, half)
            for r, (fx, fy) in enumerate(_CHIP_FLIPS):
                if forward:
                    kk = 2 * _flip(x, fx) + _flip(y, fy)
                    dev = (x, y, 1 - c)
                else:
                    kk = 2 * x + y
                    dev = (_flip(x, fx), _flip(y, fy), c)
                cps.append(_remote(ins[t].at[kk, rows], outs[t].at[kk, rows], send.at[t, r], recv.at[t, r], dev))
        return cps

    shapes = [jax.ShapeDtypeStruct(b.shape, b.dtype) for b in bufs]
    return _Job(bufs, shapes, [pltpu.SemaphoreType.DMA((n, 3)), pltpu.SemaphoreType.DMA((n, 3))], copies,
                alias={t: t for t in range(n)})


def _call(body, args, *, name, grid, in_specs, out_specs, out_shape, scratch_shapes=(), params=None, jobs=()):
    n_in, n_out, n_scr = len(in_specs), len(out_specs), len(scratch_shapes)
    job_in = [len(j.operands) for j in jobs]
    job_out = [len(j.out_shapes) for j in jobs]
    job_sem = [len(j.sems) for j in jobs]

    def wrapped(*refs):
        pos = 0
        ins = refs[pos:pos + n_in]; pos += n_in
        jins = []
        for k in job_in:
            jins.append(refs[pos:pos + k]); pos += k
        outs = refs[pos:pos + n_out]; pos += n_out
        jouts = []
        for k in job_out:
            jouts.append(refs[pos:pos + k]); pos += k
        scr = refs[pos:pos + n_scr]; pos += n_scr
        jsems = []
        for k in job_sem:
            jsems.append(refs[pos:pos + k]); pos += k
        if jobs:
            ids = [pl.program_id(d) for d in range(len(grid))]
            first = functools.reduce(jnp.logical_and, [i == 0 for i in ids])
            last = functools.reduce(jnp.logical_and, [i == g - 1 for i, g in zip(ids, grid)])

            @pl.when(first)
            def _():
                for j, ji, jo, js in zip(jobs, jins, jouts, jsems):
                    for cp in j.copies(ji, jo, js):
                        cp.start()

        body(*ins, *outs, *scr)
        if jobs:
            @pl.when(last)
            def _():
                for j, ji, jo, js in zip(jobs, jins, jouts, jsems):
                    for cp in j.copies(ji, jo, js):
                        cp.wait()

    aliases = {}
    in_pos, out_pos = n_in, n_out
    for j in jobs:
        for i, o in j.alias.items():
            aliases[in_pos + i] = out_pos + o
        in_pos += len(j.operands)
        out_pos += len(j.out_shapes)
    res = pl.pallas_call(
        wrapped, name=name, grid=grid,
        in_specs=list(in_specs) + [HBM] * sum(job_in),
        out_specs=list(out_specs) + [HBM] * sum(job_out),
        out_shape=list(out_shape) + [_in_hbm(s) for j in jobs for s in j.out_shapes],
        scratch_shapes=list(scratch_shapes) + [s for j in jobs for s in j.sems],
        input_output_aliases=aliases,
        compiler_params=params,
    )(*[a if spec is SMEM else _from_hbm(a) for a, spec in zip(args, in_specs)],
      *[a for j in jobs for a in j.operands])
    own, rest = list(res[:n_out]), list(res[n_out:])
    job_res = []
    for k in job_out:
        job_res.append(rest[:k])
        rest = rest[k:]
    return own, job_res


def _run_jobs(name, jobs):
    def body():
        pass

    return _call(body, (), name=name, grid=(1,), in_specs=[], out_specs=[], out_shape=[], jobs=jobs)[1]


def _dot(a, b):
    return jnp.dot(a, b, preferred_element_type=F32)


def _dot_nt(a, b):
    return lax.dot_general(a, b, (((1,), (1,)), ((), ())), preferred_element_type=F32)


def _dot_tn(a, b):
    return lax.dot_general(a, b, (((0,), (0,)), ((), ())), preferred_element_type=F32)


def _rms_r(x):
    return lax.rsqrt(jnp.mean(x * x, axis=-1, keepdims=True) + EPS)


def _rms_bwd(x, r, g, dy):
    n = x * r
    dn = dy * g
    dx = r * (dn - n * jnp.mean(dn * n, axis=-1, keepdims=True))
    return dx, dy * n


def _seg64_sum(v):
    rows, width = v.shape
    lane = lax.broadcasted_iota(jnp.int32, (rows, LANES), 1)
    lo = lane < HEAD_DIM
    outs = []
    for c in range(width // LANES):
        ch = v[:, c * LANES:(c + 1) * LANES]
        s_lo = jnp.sum(jnp.where(lo, ch, 0.0), axis=-1, keepdims=True)
        s_hi = jnp.sum(jnp.where(lo, 0.0, ch), axis=-1, keepdims=True)
        outs.append(jnp.where(lo, s_lo, s_hi))
    return outs[0] if len(outs) == 1 else jnp.concatenate(outs, axis=-1)


def _head_r(v):
    return lax.rsqrt(_seg64_sum(v * v) * (1.0 / HEAD_DIM) + EPS)


def _swap16(ch):
    lane = lax.broadcasted_iota(jnp.int32, ch.shape, 1)
    return jnp.where((lane % 32) < 16, pltpu.roll(ch, LANES - 16, 1), pltpu.roll(ch, 16, 1))


def _by_chunk(fn, v):
    outs = [fn(v[:, c * LANES:(c + 1) * LANES]) for c in range(v.shape[1] // LANES)]
    return outs[0] if len(outs) == 1 else jnp.concatenate(outs, axis=-1)


def _rope(v, cos, sin_signed):
    return _by_chunk(lambda ch: ch * cos + _swap16(ch) * sin_signed, v)


def _rope_t(g, cos, sin_signed):
    return _by_chunk(lambda ch: ch * cos + _swap16(ch * sin_signed), g)


def _rope_tables(seq):
    nf = HEAD_DIM // 4
    freqs = ROPE_THETA ** (-jnp.arange(nf, dtype=F32) / nf)
    pos = jnp.arange(seq, dtype=jnp.int32)
    row = (pos // GRID_W).astype(F32)
    col = (pos % GRID_W).astype(F32)
    ang_r = row[:, None] * freqs[None, :]
    ang_c = col[:, None] * freqs[None, :]
    cr, sr, cc, sc = jnp.cos(ang_r), jnp.sin(ang_r), jnp.cos(ang_c), jnp.sin(ang_c)
    cos = jnp.concatenate([cr, cr, cc, cc], axis=1)
    sin = jnp.concatenate([-sr, sr, -sc, sc], axis=1)
    return cos, sin


def _t5_bucket(rel):
    nb = N_BUCKETS // 2
    ret = (rel > 0).astype(jnp.int32) * nb
    n = jnp.abs(rel)
    max_exact = nb // 2
    nf = jnp.maximum(n, 1).astype(jnp.float32)
    large = max_exact + (jnp.log(nf / max_exact) / np.float32(np.log(MAX_DISTANCE / max_exact))
                         * (nb - max_exact)).astype(jnp.int32)
    large = jnp.minimum(large, nb - 1)
    return ret + jnp.where(n < max_exact, n, large)


def _window_tables():
    a = jnp.arange(BLOCK, dtype=jnp.int32)
    c = jnp.arange(3 * BLOCK, dtype=jnp.int32)
    rel = c[None, :] - BLOCK - a[:, None]
    bucket = _t5_bucket(rel)
    band = (jnp.abs(rel) <= BLOCK).astype(jnp.int32)
    to3 = lambda t: t.reshape(BLOCK, 3, BLOCK).transpose(1, 2, 0)
    return to3(bucket), to3(band)


def _pre_proj(x, g1, w_in, gq, gk, cq, sq, ck, sk, *, seq, tm, sub):
    tokens = x.shape[0]
    n_seq = seq // tm
    nblk = tm // BLOCK
    batch = tokens // seq

    def body(x_ref, g1_ref, w_ref, gq_ref, gk_ref, cq_ref, sq_ref, ck_ref, sk_ref,
             h1_ref, raw_ref, qa_ref, ka_ref, kta_ref, va_ref, vta_ref,
             qtb_ref, kb_ref, ktb_ref, vb_ref, vtb_ref, proj):
        for r in range(tm // sub):
            rows = slice(r * sub, (r + 1) * sub)
            xv = x_ref[rows, :]
            h = (xv * _rms_r(xv) * g1_ref[...]).astype(MM)
            h1_ref[rows, :] = h
            for j in range(N_CHIPS):
                proj[rows, j * IN_CHUNK:(j + 1) * IN_CHUNK] = _dot(h, w_ref[j])
            qa = proj[rows, 0:Q_WIDTH]
            ka = proj[rows, Q_WIDTH:QK_RAW]
            raw_ref[rows, :] = proj[rows, 0:QK_RAW]
            qn = qa * _head_r(qa) * gq_ref[...]
            qa_ref[rows, :] = _rope(qn, cq_ref[rows, :], sq_ref[rows, :]).astype(MM)
            kn = ka * _head_r(ka) * gk_ref[...]
            kr = _rope(kn, ck_ref[rows, :], sk_ref[rows, :])
            ka_ref[rows, :] = kr.astype(MM)
            kta_ref[0, :, rows] = kr.T.astype(MM)
            va = proj[rows, 640:768]
            va_ref[rows, :] = va.astype(MM)
            vta_ref[0, :, rows] = va.T.astype(MM)
            qb = proj[rows, 768:1280] * SCALE
            kb = proj[rows, 1280:1408]
            vb = proj[rows, 1408:1536]
            kb_ref[rows, :] = kb.astype(MM)
            vb_ref[rows, :] = vb.astype(MM)
            for j in range(sub // BLOCK):
                blk = slice(j * BLOCK, (j + 1) * BLOCK)
                qtb_ref[r * (sub // BLOCK) + j] = qb[blk, :].T.astype(MM)
                ktb_ref[r * (sub // BLOCK) + j] = kb[blk, :].T.astype(MM)
                vtb_ref[r * (sub // BLOCK) + j] = vb[blk, :].T.astype(MM)

    tok = lambda w: pl.BlockSpec((tm, w), lambda i: (i, 0))
    tab = lambda w: pl.BlockSpec((tm, w), lambda i: (i % n_seq, 0))
    row = lambda w: pl.BlockSpec((1, w), lambda i: (0, 0))
    tposed = pl.BlockSpec((1, LANES, tm), lambda i: (i // n_seq, 0, i % n_seq))
    blocks = pl.BlockSpec((nblk, BLOCK, LANES), lambda i: (i, 0, 0))
    qblocks = pl.BlockSpec((nblk, Q_WIDTH, BLOCK), lambda i: (i, 0, 0))
    tok_mm = lambda w: jax.ShapeDtypeStruct((tokens, w), MM)
    return pl.pallas_call(
        body, name="pre_proj",
        grid=(tokens // tm,),
        in_specs=[tok(D_MODEL), row(D_MODEL),
                  pl.BlockSpec((N_CHIPS, D_MODEL, IN_CHUNK), lambda i: (0, 0, 0)),
                  row(Q_WIDTH), row(KV_WIDTH), tab(KV_WIDTH), tab(KV_WIDTH), tab(KV_WIDTH), tab(KV_WIDTH)],
        out_specs=[tok(D_MODEL), tok(QK_RAW), tok(Q_WIDTH), tok(KV_WIDTH), tposed, tok(KV_WIDTH), tposed,
                   qblocks, tok(KV_WIDTH), blocks, tok(KV_WIDTH), blocks],
        out_shape=[
            tok_mm(D_MODEL),
            jax.ShapeDtypeStruct((tokens, QK_RAW), F32),
            tok_mm(Q_WIDTH),
            tok_mm(KV_WIDTH),
            jax.ShapeDtypeStruct((batch, KV_WIDTH, seq), MM),
            tok_mm(KV_WIDTH),
            jax.ShapeDtypeStruct((batch, KV_WIDTH, seq), MM),
            jax.ShapeDtypeStruct((tokens // BLOCK, Q_WIDTH, BLOCK), MM),
            tok_mm(KV_WIDTH),
            jax.ShapeDtypeStruct((tokens // BLOCK, KV_WIDTH, BLOCK), MM),
            tok_mm(KV_WIDTH),
            jax.ShapeDtypeStruct((tokens // BLOCK, KV_WIDTH, BLOCK), MM),
        ],
        scratch_shapes=[pltpu.VMEM((tm, IN_TOTAL), F32)],
        compiler_params=_cparams(("parallel",), 48),
    )(*map(_from_hbm, (x, g1, w_in, gq, gk, cq, sq, ck, sk)))


def _kv_half(v2, kv):
    return jnp.where(kv == 0, v2[:, :HEAD_DIM], v2[:, HEAD_DIM:])


def _attn_a_fwd(qa, kta, va, *, seq, bq, jobs=()):
    tokens = qa.shape[0]
    batch = tokens // seq
    nq = seq // bq

    def body(q_ref, kt_ref, v_ref, o_ref, p_ref, linv_ref):
        kv = pl.program_id(1)
        kt = kt_ref[0]
        lane = lax.broadcasted_iota(jnp.int32, (seq, KV_WIDTH), 1)
        v = jnp.where((lane < HEAD_DIM) == (kv == 0), v_ref[...], jnp.ones((), MM))
        for g in range(GROUP):
            sl = slice(g * HEAD_DIM, (g + 1) * HEAD_DIM)
            s = _dot(q_ref[:, sl], kt)
            pb = jnp.exp((s - jnp.max(s, axis=-1, keepdims=True)).astype(MM))
            p_ref[0, g] = pb
            o2 = _dot(pb, v)
            linv = 1.0 / _kv_half(o2, 1 - kv)[:, 0:1]
            o_ref[:, sl] = _kv_half(o2, kv) * linv
            linv_ref[0, :, g:g + 1] = linv

    return _call(
        body, (qa, kta, va), name="attn_a_fwd", jobs=jobs,
        grid=(batch, N_KV, nq),
        in_specs=[pl.BlockSpec((bq, GROUP * HEAD_DIM), lambda b, k, i: (b * nq + i, k)),
                  pl.BlockSpec((1, HEAD_DIM, seq), lambda b, k, i: (b, k, 0)),
                  pl.BlockSpec((seq, KV_WIDTH), lambda b, k, i: (b, 0))],
        out_specs=[pl.BlockSpec((bq, GROUP * HEAD_DIM), lambda b, k, i: (b * nq + i, k)),
                   pl.BlockSpec((1, GROUP, bq, seq), lambda b, k, i: (k, 0, b * nq + i, 0)),
                   pl.BlockSpec((1, bq, GROUP), lambda b, k, i: (k, b * nq + i, 0))],
        out_shape=[jax.ShapeDtypeStruct((tokens, Q_WIDTH), F32),
                   jax.ShapeDtypeStruct((N_KV, GROUP, tokens, seq), MM),
                   jax.ShapeDtypeStruct((N_KV, tokens, GROUP), F32)],
        params=_cparams(("arbitrary", "arbitrary", "arbitrary"), 56))


def _attn_a_bwd(qa, ka, vta, do, o, p, linv, *, seq, bq, jobs=()):
    tokens = qa.shape[0]
    batch = tokens // seq
    nq = seq // bq

    def body(q_ref, k_ref, vt_ref, do_ref, o_ref, p_ref, linv_ref, dq_ref, dkt_ref, dvt_ref):
        kv = pl.program_id(1)

        @pl.when(pl.program_id(2) == 0)
        def _():
            dkt_ref[...] = jnp.zeros_like(dkt_ref)
            dvt_ref[...] = jnp.zeros_like(dvt_ref)

        vt = vt_ref[0]
        k2 = k_ref[...]
        for g in range(GROUP):
            sl = slice(g * HEAD_DIM, (g + 1) * HEAD_DIM)
            dof = do_ref[:, sl]
            delta = jnp.sum(dof * o_ref[:, sl], axis=-1, keepdims=True)
            linv_g = linv_ref[0, :, g:g + 1]
            pb = p_ref[0, g]
            dp = _dot(dof.astype(MM), vt)
            ds = pb * ((dp - delta) * linv_g).astype(MM)
            dq_ref[:, sl] = _kv_half(_dot(ds, k2), kv)
            dkt_ref[0] += _dot_tn(q_ref[:, sl], ds)
            dvt_ref[0] += _dot_tn((dof * linv_g).astype(MM), pb)

    qspec = pl.BlockSpec((bq, GROUP * HEAD_DIM), lambda b, k, i: (b * nq + i, k))
    tspec = pl.BlockSpec((1, HEAD_DIM, seq), lambda b, k, i: (b, k, 0))
    return _call(
        body, (qa, ka, vta, do, o, p, linv), name="attn_a_bwd", jobs=jobs,
        grid=(batch, N_KV, nq),
        in_specs=[qspec, pl.BlockSpec((seq, KV_WIDTH), lambda b, k, i: (b, 0)), tspec, qspec, qspec,
                  pl.BlockSpec((1, GROUP, bq, seq), lambda b, k, i: (k, 0, b * nq + i, 0)),
                  pl.BlockSpec((1, bq, GROUP), lambda b, k, i: (k, b * nq + i, 0))],
        out_specs=[qspec, tspec, tspec],
        out_shape=[jax.ShapeDtypeStruct((tokens, Q_WIDTH), F32),
                   jax.ShapeDtypeStruct((batch, KV_WIDTH, seq), F32),
                   jax.ShapeDtypeStruct((batch, KV_WIDTH, seq), F32)],
        params=_cparams(("arbitrary", "arbitrary", "arbitrary"), 56))


def _bias_build(rel_bias_t, bucket_t, band_t):
    def body(tab_ref, bucket_ref, band_ref, bias_ref):
        for h in range(GROUP * N_KV):
            for piece in range(3):
                bk = bucket_ref[piece]
                acc = jnp.zeros((BLOCK, BLOCK), F32)
                for b in range(N_BUCKETS):
                    acc = jnp.where(bk == b, tab_ref[h, b], acc)
                g = h % GROUP
                bias_ref[h // GROUP, piece, :, g * BLOCK:(g + 1) * BLOCK] = jnp.where(band_ref[piece] != 0, acc, NEG_INF)

    out = jax.ShapeDtypeStruct((N_KV, 3, BLOCK, GROUP * BLOCK), F32)
    return pl.pallas_call(
        body, name="bias_build", grid=(1,),
        in_specs=[SMEM, _whole(bucket_t), _whole(band_t)], out_specs=_whole(out), out_shape=out,
    )(rel_bias_t, bucket_t, band_t)


def _pad_heads(t, kv):
    outs = []
    for g in range(GROUP):
        tg = t[g * HEAD_DIM:(g + 1) * HEAD_DIM, :]
        zero = jnp.zeros_like(tg)
        outs.append(jnp.concatenate([jnp.where(kv == 0, tg, zero), jnp.where(kv == 0, zero, tg)], axis=0))
    return jnp.concatenate(outs, axis=-1)


def _unpad_heads(t, kv):
    outs = [_kv_half(t[:, g * BLOCK:(g + 1) * BLOCK].T, kv) for g in range(GROUP)]
    return jnp.concatenate(outs, axis=-1)


def _sink_row(sink_ref, kv):
    lane_head = lax.broadcasted_iota(jnp.int32, (1, GROUP * BLOCK), 1) // BLOCK
    row = jnp.zeros((1, GROUP * BLOCK), F32)
    for g in range(GROUP):
        row = jnp.where(lane_head == g, sink_ref[0, kv * GROUP + g], row)
    return row


def _window_scores_t(k_ref, idx, qpad, bias_ref, n, nblk):
    pieces = []
    for piece in range(3):
        s = _dot(k_ref[idx[piece]], qpad) + bias_ref[0, piece]
        if piece == 0:
            s = jnp.where(n > 0, s, NEG_INF)
        if piece == 2:
            s = jnp.where(n < nblk - 1, s, NEG_INF)
        pieces.append(s)
    return pieces


def _attn_b_fwd(qtb, kb3, vtb, bias, sink, *, seq, jobs=()):
    nblk_all = qtb.shape[0]
    tokens = nblk_all * BLOCK
    batch = tokens // seq
    nblk = seq // BLOCK

    def body(sink_ref, q_ref, k_ref, vt_ref, bias_ref, o_ref, lse_ref):
        kv = pl.program_id(0)
        sink_row = _sink_row(sink_ref, kv)

        def block(n, carry):
            idx = (jnp.maximum(n - 1, 0), n, jnp.minimum(n + 1, nblk - 1))
            rows = pl.ds(pl.multiple_of(n * BLOCK, BLOCK), BLOCK)
            qpad = _pad_heads(q_ref[n], kv)
            ss = _window_scores_t(k_ref, idx, qpad, bias_ref, n, nblk)
            m = jnp.maximum(jnp.maximum(jnp.max(ss[0], axis=0, keepdims=True),
                                        jnp.max(ss[1], axis=0, keepdims=True)),
                            jnp.maximum(jnp.max(ss[2], axis=0, keepdims=True), sink_row))
            ps = [jnp.exp(s - m) for s in ss]
            denom = (jnp.sum(ps[0], axis=0, keepdims=True) + jnp.sum(ps[1], axis=0, keepdims=True)
                     + jnp.sum(ps[2], axis=0, keepdims=True) + jnp.exp(sink_row - m))
            ot = (_dot(vt_ref[idx[0]], ps[0].astype(MM)) + _dot(vt_ref[idx[1]], ps[1].astype(MM))
                  + _dot(vt_ref[idx[2]], ps[2].astype(MM)))
            o_ref[rows, :] = _unpad_heads(ot * (1.0 / denom), kv)
            lse_ref[0, n] = jnp.broadcast_to(m + jnp.log(denom), (8, GROUP * BLOCK))
            return carry

        lax.fori_loop(0, nblk, block, 0, unroll=8)

    both = pl.BlockSpec((nblk, BLOCK, KV_WIDTH), lambda k, b: (b, 0, 0))
    return _call(
        body, (sink, qtb, kb3, vtb, bias), name="attn_b_fwd", jobs=jobs,
        grid=(N_KV, batch),
        in_specs=[SMEM, pl.BlockSpec((nblk, GROUP * HEAD_DIM, BLOCK), lambda k, b: (b, k, 0)), both, both,
                  pl.BlockSpec((1, 3, BLOCK, GROUP * BLOCK), lambda k, b: (k, 0, 0, 0))],
        out_specs=[pl.BlockSpec((seq, GROUP * HEAD_DIM), lambda k, b: (b, k)),
                   pl.BlockSpec((1, nblk, 8, GROUP * BLOCK), lambda k, b: (k, b, 0, 0))],
        out_shape=[jax.ShapeDtypeStruct((tokens, Q_WIDTH), F32),
                   jax.ShapeDtypeStruct((N_KV, nblk_all, 8, GROUP * BLOCK), F32)],
        params=_cparams(("arbitrary", "arbitrary"), 48))


def _attn_b_bwd(qtb, kb3, ktb, vb3, do, o, lse, bias, sink, *, seq, jobs=()):
    nblk_all = qtb.shape[0]
    tokens = nblk_all * BLOCK
    batch = tokens // seq
    nblk = seq // BLOCK

    def body(sink_ref, q_ref, k_ref, kt_ref, v_ref, do_ref, o_ref, lse_ref, bias_ref,
             dq_ref, dk_ref, dv_ref, dbias_ref, dsink_ref):
        kv = pl.program_id(0)
        sink_row = _sink_row(sink_ref, kv)

        @pl.when(pl.program_id(1) == 0)
        def _():
            dbias_ref[...] = jnp.zeros_like(dbias_ref)
            dsink_ref[...] = jnp.zeros_like(dsink_ref)

        dk_ref[...] = jnp.zeros_like(dk_ref)
        dv_ref[...] = jnp.zeros_like(dv_ref)

        def block(n, dsink):
            idx = (jnp.maximum(n - 1, 0), n, jnp.minimum(n + 1, nblk - 1))
            rows = pl.ds(pl.multiple_of(n * BLOCK, BLOCK), BLOCK)
            qpad = _pad_heads(q_ref[n], kv)
            dot_t = do_ref[rows, :].T
            prod = dot_t * o_ref[rows, :].T
            delta = jnp.concatenate(
                [jnp.sum(prod[g * HEAD_DIM:(g + 1) * HEAD_DIM, :], axis=0, keepdims=True) for g in range(GROUP)],
                axis=-1)
            dopad = _pad_heads(dot_t.astype(MM), kv)
            lse_row = lse_ref[0, n][0:1, :]
            ss = _window_scores_t(k_ref, idx, qpad, bias_ref, n, nblk)
            dqt = jnp.zeros((KV_WIDTH, GROUP * BLOCK), F32)
            for piece in range(3):
                pt = jnp.exp(ss[piece] - lse_row)
                dst = pt * (_dot(v_ref[idx[piece]], dopad) - delta)
                dsb = dst.astype(MM)
                dbias_ref[0, piece] += dst
                dqt = dqt + _dot(kt_ref[idx[piece]], dsb)
                dk_ref[0, idx[piece]] += _dot_nt(dsb, qpad)
                dv_ref[0, idx[piece]] += _dot_nt(pt.astype(MM), dopad)
            dq_ref[rows, :] = _unpad_heads(dqt, kv)
            return dsink - jnp.exp(sink_row - lse_row) * delta

        dsink = lax.fori_loop(
            0, nblk // 4, lambda i, c: block(4 * i + 3, block(4 * i + 2, block(4 * i + 1, block(4 * i, c)))),
            jnp.zeros((1, GROUP * BLOCK), F32))
        dsink_ref[0] += jnp.broadcast_to(dsink, (8, GROUP * BLOCK))

    qspec = pl.BlockSpec((seq, GROUP * HEAD_DIM), lambda k, b: (b, k))
    both = pl.BlockSpec((nblk, BLOCK, KV_WIDTH), lambda k, b: (b, 0, 0))
    grad = pl.BlockSpec((1, nblk, BLOCK, KV_WIDTH), lambda k, b: (k, b, 0, 0))
    return _call(
        body, (sink, qtb, kb3, ktb, vb3, do, o, lse, bias), name="attn_b_bwd", jobs=jobs,
        grid=(N_KV, batch),
        in_specs=[SMEM, pl.BlockSpec((nblk, GROUP * HEAD_DIM, BLOCK), lambda k, b: (b, k, 0)), both, both, both,
                  qspec, qspec, pl.BlockSpec((1, nblk, 8, GROUP * BLOCK), lambda k, b: (k, b, 0, 0)),
                  pl.BlockSpec((1, 3, BLOCK, GROUP * BLOCK), lambda k, b: (k, 0, 0, 0))],
        out_specs=[qspec, grad, grad,
                   pl.BlockSpec((1, 3, BLOCK, GROUP * BLOCK), lambda k, b: (k, 0, 0, 0)),
                   pl.BlockSpec((1, 8, GROUP * BLOCK), lambda k, b: (k, 0, 0))],
        out_shape=[jax.ShapeDtypeStruct((tokens, Q_WIDTH), F32),
                   jax.ShapeDtypeStruct((N_KV, nblk_all, BLOCK, KV_WIDTH), F32),
                   jax.ShapeDtypeStruct((N_KV, nblk_all, BLOCK, KV_WIDTH), F32),
                   jax.ShapeDtypeStruct((N_KV, 3, BLOCK, GROUP * BLOCK), F32),
                   jax.ShapeDtypeStruct((N_KV, 8, GROUP * BLOCK), F32)],
        params=_cparams(("arbitrary", "arbitrary"), 48))


def _wo_post(oa, ob, w_o, x, g2, g3, *, tm, sub):
    tokens = x.shape[0]

    def body(oa_ref, ob_ref, w_ref, x_ref, g2_ref, g3_ref, mix_ref, x1_ref, h2_ref, o_ref):
        for r in range(tm // sub):
            rows = slice(r * sub, (r + 1) * sub)
            o = jnp.concatenate([oa_ref[rows, :].astype(MM), ob_ref[rows, :].astype(MM)], axis=-1)
            o_ref[rows, :] = o
            mix = _dot(o, w_ref[...])
            mix_ref[rows, :] = mix
            x1 = x_ref[rows, :] + mix * _rms_r(mix) * g2_ref[...]
            x1_ref[rows, :] = x1
            h2_ref[rows, :] = (x1 * _rms_r(x1) * g3_ref[...]).astype(MM)

    tok = lambda w: pl.BlockSpec((tm, w), lambda i: (i, 0))
    row = pl.BlockSpec((1, D_MODEL), lambda i: (0, 0))
    return pl.pallas_call(
        body, name="wo_post",
        grid=(tokens // tm,),
        in_specs=[tok(Q_WIDTH), tok(Q_WIDTH), pl.BlockSpec((D_MODEL, D_MODEL), lambda i: (0, 0)),
                  tok(D_MODEL), row, row],
        out_specs=[tok(D_MODEL), tok(D_MODEL), tok(D_MODEL), tok(D_MODEL)],
        out_shape=[jax.ShapeDtypeStruct((tokens, D_MODEL), F32),
                   jax.ShapeDtypeStruct((tokens, D_MODEL), F32),
                   jax.ShapeDtypeStruct((tokens, D_MODEL), MM),
                   jax.ShapeDtypeStruct((tokens, D_MODEL), MM)],
        compiler_params=_cparams(("parallel",), 40),
    )(*map(_from_hbm, (oa, ob, w_o, x, g2, g3)))


def _resident(shape):
    return pl.BlockSpec(shape, lambda i: (0,) * len(shape), pipeline_mode=pl.Buffered(1))


def _ffn_fwd_loss(h2, w_up, w_down, x1, target, g4, *, tm):
    tokens = h2.shape[0]
    nt = tokens // tm

    def body(h2_ref, wu_ref, wd_ref, x1_ref, t_ref, g4_ref, u_ref, df_ref, dy_ref, loss_ref, dg4_ref):
        h2v = h2_ref[...]
        f = jnp.zeros((tm, D_MODEL), F32)
        for c in range(N_CHIPS):
            u = jnp.maximum(_dot(h2v, wu_ref[c]), 0.0)
            u_ref[:, c * FF_CHUNK:(c + 1) * FF_CHUNK] = u.astype(MM)
            f = f + _dot((u * u).astype(MM), wd_ref[c * FF_CHUNK:(c + 1) * FF_CHUNK, :])
        r = _rms_r(f)
        g4v = g4_ref[...]
        err = x1_ref[...] + f * r * g4v - t_ref[...]
        sq = jnp.sum(err * err, axis=-1, keepdims=True)
        loss_ref[0] = jnp.broadcast_to(jnp.sum(sq, axis=0, keepdims=True) * (0.5 / D_MODEL), (8, LANES))
        dy = err * (1.0 / D_MODEL)
        dy_ref[...] = dy
        dfv, dgv = _rms_bwd(f, r, g4v, dy)
        df_ref[...] = dfv.astype(MM)
        dg4_ref[0] = jnp.sum(dgv, axis=0, keepdims=True)

    tok = pl.BlockSpec((tm, D_MODEL), lambda i: (i, 0))
    return pl.pallas_call(
        body, name="ffn_fwd_loss",
        grid=(nt,),
        in_specs=[tok, _resident((N_CHIPS, D_MODEL, FF_CHUNK)), _resident((D_FF, D_MODEL)),
                  tok, tok, pl.BlockSpec((1, D_MODEL), lambda i: (0, 0))],
        out_specs=[pl.BlockSpec((tm, D_FF), lambda i: (i, 0)), tok, tok,
                   pl.BlockSpec((1, 8, LANES), lambda i: (i, 0, 0)),
                   pl.BlockSpec((1, 1, D_MODEL), lambda i: (i, 0, 0))],
        out_shape=[jax.ShapeDtypeStruct((tokens, D_FF), MM),
                   jax.ShapeDtypeStruct((tokens, D_MODEL), MM),
                   jax.ShapeDtypeStruct((tokens, D_MODEL), F32),
                   jax.ShapeDtypeStruct((nt, 8, LANES), F32),
                   jax.ShapeDtypeStruct((nt, 1, D_MODEL), F32)],
        compiler_params=_cparams(("parallel",), 56),
    )(*map(_from_hbm, (h2, w_up, w_down, x1, target, g4)))


def _ffn_bwd_act(df, w_down, u, w_up, x1, dy, mix, g3, g2, *, tm):
    tokens = df.shape[0]
    nt = tokens // tm

    def body(df_ref, wd_ref, u_ref, wu_ref, x1_ref, dy_ref, mix_ref, g3_ref, g2_ref,
             dz_ref, dx1_ref, dmix_ref, dg3_ref, dg2_ref):
        dfv = df_ref[...]
        dh2 = jnp.zeros((tm, D_MODEL), F32)
        for c in range(N_CHIPS):
            cols = slice(c * FF_CHUNK, (c + 1) * FF_CHUNK)
            da = _dot_nt(dfv, wd_ref[cols, :])
            dz = (da * (2.0 * u_ref[:, cols].astype(F32))).astype(MM)
            dz_ref[:, cols] = dz
            dh2 = dh2 + _dot_nt(dz, wu_ref[c])
        x1 = x1_ref[...]
        dxn, dg3v = _rms_bwd(x1, _rms_r(x1), g3_ref[...], dh2)
        dx1 = dy_ref[...] + dxn
        dx1_ref[...] = dx1
        dg3_ref[0] = jnp.sum(dg3v, axis=0, keepdims=True)
        mix = mix_ref[...]
        dmix, dg2v = _rms_bwd(mix, _rms_r(mix), g2_ref[...], dx1)
        dmix_ref[...] = dmix.astype(MM)
        dg2_ref[0] = jnp.sum(dg2v, axis=0, keepdims=True)

    tok = pl.BlockSpec((tm, D_MODEL), lambda i: (i, 0))
    wide = pl.BlockSpec((tm, D_FF), lambda i: (i, 0))
    row = pl.BlockSpec((1, D_MODEL), lambda i: (0, 0))
    part = pl.BlockSpec((1, 1, D_MODEL), lambda i: (i, 0, 0))
    return pl.pallas_call(
        body, name="ffn_bwd_act",
        grid=(nt,),
        in_specs=[tok, _resident((D_FF, D_MODEL)), wide, _resident((N_CHIPS, D_MODEL, FF_CHUNK)),
                  tok, tok, tok, row, row],
        out_specs=[wide, tok, tok, part, part],
        out_shape=[jax.ShapeDtypeStruct((tokens, D_FF), MM),
                   jax.ShapeDtypeStruct((tokens, D_MODEL), F32),
                   jax.ShapeDtypeStruct((tokens, D_MODEL), MM),
                   jax.ShapeDtypeStruct((nt, 1, D_MODEL), F32),
                   jax.ShapeDtypeStruct((nt, 1, D_MODEL), F32)],
        compiler_params=_cparams(("parallel",), 56),
    )(*map(_from_hbm, (df, w_down, u, w_up, x1, dy, mix, g3, g2)))


def _tn_matmul(a, b, *, name, tm, tn, tk, chunked=False, square_a=False, vmem_mb=48, jobs=()):
    tokens, m_dim = a.shape
    n_dim = b.shape[1]
    if chunked:
        assert tm == m_dim

    def body(a_ref, b_ref, o_ref):
        av = a_ref[...]
        if square_a:
            av = av.astype(F32)
            av = av * av
        part = _dot_tn(av.astype(MM), b_ref[...].astype(MM))
        part = part[None] if chunked else part

        @pl.when(pl.program_id(2) == 0)
        def _():
            o_ref[...] = part

        @pl.when(pl.program_id(2) > 0)
        def _():
            o_ref[...] += part

    if chunked:
        out_spec = pl.BlockSpec((1, tm, tn), lambda i, j, k: (j, 0, 0))
        out_shape = jax.ShapeDtypeStruct((n_dim // tn, m_dim, tn), F32)
    else:
        out_spec = pl.BlockSpec((tm, tn), lambda i, j, k: (i, j))
        out_shape = jax.ShapeDtypeStruct((m_dim, n_dim), F32)
    (out,), job_res = _call(
        body, (a, b), name=name, jobs=jobs,
        grid=(m_dim // tm, n_dim // tn, tokens // tk),
        in_specs=[pl.BlockSpec((tk, tm), lambda i, j, k: (k, i)),
                  pl.BlockSpec((tk, tn), lambda i, j, k: (k, j))],
        out_specs=[out_spec], out_shape=[_in_hbm(out_shape)],
        params=_cparams(("arbitrary", "arbitrary", "arbitrary"), vmem_mb))
    return out, job_res


def _wo_bwd(dmix, w_o, *, tm):
    tokens = dmix.shape[0]

    def body(dm_ref, w_ref, doa_ref, dob_ref):
        dm = dm_ref[...]
        doa_ref[...] = _dot_nt(dm, w_ref[0:Q_WIDTH, :])
        dob_ref[...] = _dot_nt(dm, w_ref[Q_WIDTH:D_MODEL, :])

    tok = lambda w: pl.BlockSpec((tm, w), lambda i: (i, 0))
    return pl.pallas_call(
        body, name="wo_bwd",
        grid=(tokens // tm,),
        in_specs=[tok(D_MODEL), pl.BlockSpec((D_MODEL, D_MODEL), lambda i: (0, 0))],
        out_specs=[tok(Q_WIDTH), tok(Q_WIDTH)],
        out_shape=[jax.ShapeDtypeStruct((tokens, Q_WIDTH), F32)] * 2,
        compiler_params=_cparams(("parallel",), 40),
    )(*map(_from_hbm, (dmix, w_o)))


def _proj_bwd(dqa, dkta, dvta, dqb, dktb, dvtb, raw, x, dx1, g1, w_in, gq, gk, cq, sq, ck, sk, *, seq, tm, sub,
              jobs=()):
    tokens = x.shape[0]
    nt = tokens // tm
    n_seq = seq // tm
    nblk = tm // BLOCK

    def body(dqa_ref, dkta_ref, dvta_ref, dqb_ref, dkb_ref, dvb_ref, raw_ref, x_ref, dx1_ref, g1_ref, w_ref,
             gq_ref, gk_ref, cq_ref, sq_ref, ck_ref, sk_ref,
             gx_ref, dproj_ref, dg1_ref, dgq_ref, dgk_ref, dp):
        parts = []
        for r in range(tm // sub):
            rows = slice(r * sub, (r + 1) * sub)
            qa = raw_ref[rows, 0:Q_WIDTH]
            dqn = _rope_t(dqa_ref[rows, :], cq_ref[rows, :], sq_ref[rows, :])
            rq = _head_r(qa)
            nq = qa * rq
            dnq = dqn * gq_ref[...]
            dp[rows, 0:Q_WIDTH] = rq * (dnq - nq * (_seg64_sum(dnq * nq) * (1.0 / HEAD_DIM)))

            ka = raw_ref[rows, Q_WIDTH:QK_RAW]
            dkn = _rope_t(dkta_ref[0, :, rows].T, ck_ref[rows, :], sk_ref[rows, :])
            rk = _head_r(ka)
            nk = ka * rk
            dnk = dkn * gk_ref[...]
            dp[rows, 512:640] = rk * (dnk - nk * (_seg64_sum(dnk * nk) * (1.0 / HEAD_DIM)))

            dp[rows, 640:768] = dvta_ref[0, :, rows].T
            dp[rows, 768:1280] = dqb_ref[rows, :] * SCALE
            for j in range(r * sub // BLOCK, (r + 1) * sub // BLOCK):
                dp[j * BLOCK:(j + 1) * BLOCK, 1280:1408] = dkb_ref[0, j] + dkb_ref[1, j]
                dp[j * BLOCK:(j + 1) * BLOCK, 1408:1536] = dvb_ref[0, j] + dvb_ref[1, j]

            dproj = dp[rows, :].astype(MM)
            dproj_ref[rows, :] = dproj
            dh1 = _dot_nt(dproj[:, 0:IN_CHUNK], w_ref[0])
            for j in range(1, N_CHIPS):
                dh1 = dh1 + _dot_nt(dproj[:, j * IN_CHUNK:(j + 1) * IN_CHUNK], w_ref[j])
            xv = x_ref[rows, :]
            dxn, dg1v = _rms_bwd(xv, _rms_r(xv), g1_ref[...], dh1)
            gx_ref[rows, :] = dx1_ref[rows, :] + dxn
            parts.append((jnp.sum(dqn * nq, axis=0, keepdims=True), jnp.sum(dkn * nk, axis=0, keepdims=True),
                          jnp.sum(dg1v, axis=0, keepdims=True)))
        dgq_ref[0] = functools.reduce(jnp.add, [p[0] for p in parts])
        dgk_ref[0] = functools.reduce(jnp.add, [p[1] for p in parts])
        dg1_ref[0] = functools.reduce(jnp.add, [p[2] for p in parts])

    tok = lambda w: pl.BlockSpec((tm, w), lambda i: (i, 0))
    tab = lambda w: pl.BlockSpec((tm, w), lambda i: (i % n_seq, 0))
    row = lambda w: pl.BlockSpec((1, w), lambda i: (0, 0))
    tposed = pl.BlockSpec((1, KV_WIDTH, tm), lambda i: (i // n_seq, 0, i % n_seq))
    blocks = pl.BlockSpec((N_KV, nblk, BLOCK, KV_WIDTH), lambda i: (0, i, 0, 0))
    part = lambda w: pl.BlockSpec((1, 1, w), lambda i: (i, 0, 0))
    return _call(
        body, (dqa, dkta, dvta, dqb, dktb, dvtb, raw, x, dx1, g1, w_in, gq, gk, cq, sq, ck, sk),
        name="proj_bwd", jobs=jobs,
        grid=(nt,),
        in_specs=[tok(Q_WIDTH), tposed, tposed, tok(Q_WIDTH), blocks, blocks, tok(QK_RAW), tok(D_MODEL),
                  tok(D_MODEL), row(D_MODEL),
                  pl.BlockSpec((N_CHIPS, D_MODEL, IN_CHUNK), lambda i: (0, 0, 0)),
                  row(Q_WIDTH), row(KV_WIDTH), tab(KV_WIDTH), tab(KV_WIDTH), tab(KV_WIDTH), tab(KV_WIDTH)],
        out_specs=[tok(D_MODEL), tok(IN_TOTAL), part(D_MODEL), part(Q_WIDTH), part(KV_WIDTH)],
        out_shape=[jax.ShapeDtypeStruct((tokens, D_MODEL), F32),
                   jax.ShapeDtypeStruct((tokens, IN_TOTAL), MM),
                   jax.ShapeDtypeStruct((nt, 1, D_MODEL), F32),
                   jax.ShapeDtypeStruct((nt, 1, Q_WIDTH), F32),
                   jax.ShapeDtypeStruct((nt, 1, KV_WIDTH), F32)],
        scratch_shapes=[pltpu.VMEM((tm, IN_TOTAL), F32)],
        params=_cparams(("arbitrary",), 56))


def _pack_small(dg1, dg2, dg3, dg4, dgq, dgk, dsink, dbias, bucket, loss):
    def body(dg1_ref, dg2_ref, dg3_ref, dg4_ref, dgq_ref, dgk_ref, dsink_ref, dbias_ref, bucket_ref, loss_ref,
             out_ref, rel_ref):
        out_ref[...] = jnp.zeros_like(out_ref)
        for r, ref in ((ROW_G1, dg1_ref), (ROW_G2, dg2_ref), (ROW_G3, dg3_ref), (ROW_G4, dg4_ref)):
            acc = ref[0]
            for t in range(1, ref.shape[0]):
                acc = acc + ref[t]
            out_ref[r:r + 1, :] = acc

        def fold(ref, heads):
            acc = ref[0]
            for t in range(1, ref.shape[0]):
                acc = acc + ref[t]
            tot = acc[:, 0:HEAD_DIM]
            for h in range(1, heads):
                tot = tot + acc[:, h * HEAD_DIM:(h + 1) * HEAD_DIM]
            return tot

        out_ref[ROW_MISC:ROW_MISC + 1, MISC_GQ:MISC_GQ + HEAD_DIM] = fold(dgq_ref, GROUP * N_KV)
        out_ref[ROW_MISC:ROW_MISC + 1, MISC_GK:MISC_GK + HEAD_DIM] = fold(dgk_ref, N_KV)
        for h in range(GROUP * N_KV):
            g = h % GROUP
            out_ref[ROW_MISC:ROW_MISC + 1, MISC_SINK + h:MISC_SINK + h + 1] = jnp.sum(
                dsink_ref[h // GROUP, 0:1, g * BLOCK:(g + 1) * BLOCK], axis=-1, keepdims=True)
        lacc = loss_ref[0, 0:1, 0:1]
        for t in range(1, loss_ref.shape[0]):
            lacc = lacc + loss_ref[t, 0:1, 0:1]
        out_ref[ROW_MISC:ROW_MISC + 1, MISC_LOSS:MISC_LOSS + 1] = lacc
        lane = lax.broadcasted_iota(jnp.int32, (N_BUCKETS, LANES), 1)
        row = lax.broadcasted_iota(jnp.int32, (N_BUCKETS, LANES), 0)

        def per_bucket(b, acc):
            for h in range(GROUP * N_KV):
                g = h % GROUP
                sel = jnp.zeros((BLOCK, BLOCK), F32)
                for piece in range(3):
                    sel = sel + jnp.where(bucket_ref[piece] == b,
                                          dbias_ref[h // GROUP, piece, :, g * BLOCK:(g + 1) * BLOCK], 0.0)
                tot = jnp.sum(jnp.sum(sel, axis=0, keepdims=True), axis=-1, keepdims=True)
                acc = jnp.where((row == b) & (lane == h), tot, acc)
            return acc

        rel_ref[...] = lax.fori_loop(0, N_BUCKETS, per_bucket, jnp.zeros((N_BUCKETS, LANES), F32))

    args = (dg1, dg2, dg3, dg4, dgq, dgk, dsink, dbias, bucket, loss)
    outs = [jax.ShapeDtypeStruct((8, D_MODEL), F32), jax.ShapeDtypeStruct((N_BUCKETS, LANES), F32)]
    return pl.pallas_call(
        body, name="pack_small", grid=(1,),
        in_specs=[_whole(a) for a in args], out_specs=[_whole(o) for o in outs], out_shape=outs,
        compiler_params=pltpu.CompilerParams(vmem_limit_bytes=32 * 1024 * 1024),
    )(*map(_from_hbm, args))


def _gather_weights(shards, whole):
    n = len(shards)
    full = [t for t in range(n) if whole[t]]

    def body(*refs):
        ins, outs = refs[:n], refs[n:2 * n]
        raw, stage = refs[2 * n:3 * n], refs[3 * n:4 * n]
        load_sem, local_sem, ici_send, ici_recv, d2d_send, d2d_recv = refs[4 * n:]
        x, y, c = _place()
        k = 2 * x + y
        sibling = (x, y, 1 - c)
        order = full + [t for t in range(n) if t not in full]
        loads = {t: pltpu.make_async_copy(ins[t], raw[t], load_sem.at[t]) for t in order}
        for t in order:
            loads[t].start()
        copies, sends = [], []
        for t in order:
            loads[t].wait()
            stage[t][...] = raw[t][...].astype(MM)
            mine = pltpu.make_async_copy(stage[t], outs[t].at[k], local_sem.at[t])
            mine.start()
            copies.append(mine)
            if t in full:
                half = ins[t].shape[0] // 2
                rows = pl.ds(c * half, half)
                for r, (fx, fy) in enumerate(_CHIP_FLIPS):
                    cp = _remote(stage[t].at[rows], outs[t].at[k, rows], ici_send.at[t, r], ici_recv.at[t, r],
                                 (_flip(x, fx), _flip(y, fy), c))
                    cp.start()
                    sends.append(cp)
        for t in full:
            half = ins[t].shape[0] // 2
            rows = pl.ds(c * half, half)
            for r, (fx, fy) in enumerate(_CHIP_FLIPS):
                kk = 2 * _flip(x, fx) + _flip(y, fy)
                landed = outs[t].at[kk, rows]
                _remote(landed, landed, ici_send.at[t, r], ici_recv.at[t, r], sibling).wait_recv()
                fwd = _remote(landed, landed, d2d_send.at[t, r], d2d_recv.at[t, r], sibling)
                fwd.start()
                sends.append(fwd)
        for t in full:
            half = ins[t].shape[0] // 2
            other = pl.ds((1 - c) * half, half)
            for r, (fx, fy) in enumerate(_CHIP_FLIPS):
                kk = 2 * _flip(x, fx) + _flip(y, fy)
                theirs = outs[t].at[kk, other]
                _remote(theirs, theirs, d2d_send.at[t, r], d2d_recv.at[t, r], sibling).wait_recv()
        for cp in sends:
            cp.wait_send()
        for cp in copies:
            cp.wait()

    return pl.pallas_call(
        body, name="gather_weights",
        in_specs=[HBM] * n, out_specs=[HBM] * n,
        out_shape=[pltpu.HBM((N_CHIPS,) + s.shape, MM) for s in shards],
        scratch_shapes=[pltpu.VMEM(s.shape, F32) for s in shards] + [pltpu.VMEM(s.shape, MM) for s in shards] + [
            pltpu.SemaphoreType.DMA((n,)), pltpu.SemaphoreType.DMA((n,)),
            pltpu.SemaphoreType.DMA((n, 3)), pltpu.SemaphoreType.DMA((n, 3)),
            pltpu.SemaphoreType.DMA((n, 3)), pltpu.SemaphoreType.DMA((n, 3))],
        compiler_params=pltpu.CompilerParams(vmem_limit_bytes=40 * 1024 * 1024),
    )(*shards)


def _add_half(grad, got, where, *, name, tr):
    nch, half, cols = got.shape
    nblk = half // tr

    def body(where_ref, g_ref, r_ref, o_ref):
        o_ref[...] = (g_ref[...] + r_ref[...]).astype(MM)

    return pl.pallas_call(
        body, name=name,
        grid_spec=pltpu.PrefetchScalarGridSpec(
            num_scalar_prefetch=1, grid=(nch, nblk),
            in_specs=[pl.BlockSpec((1, tr, cols), lambda j, i, where_ref: (j, where_ref[1] * nblk + i, 0)),
                      pl.BlockSpec((1, tr, cols), lambda j, i, where_ref: (j, i, 0))],
            out_specs=pl.BlockSpec((1, tr, cols), lambda j, i, where_ref: (j, i, 0))),
        out_shape=jax.ShapeDtypeStruct(got.shape, MM),
        compiler_params=_cparams(("parallel", "parallel"), 32),
    )(where, _from_hbm(grad), _from_hbm(got))


def _add_chips(own, got, where, *, name, tr):
    _, half, cols = own.shape
    nblk = half // tr

    def body(where_ref, o_ref, g_ref, out_ref):
        f = lambda v: v.astype(F32)
        out_ref[...] = ((f(o_ref[0]) + f(g_ref[0])) + f(g_ref[1])) + f(g_ref[2])

    return pl.pallas_call(
        body, name=name,
        grid_spec=pltpu.PrefetchScalarGridSpec(
            num_scalar_prefetch=1, grid=(nblk,),
            in_specs=[pl.BlockSpec((1, tr, cols), lambda i, where_ref: (where_ref[0], i, 0)),
                      pl.BlockSpec((3, tr, cols), lambda i, where_ref: (0, i, 0))],
            out_specs=pl.BlockSpec((tr, cols), lambda i, where_ref: (where_ref[1] * nblk + i, 0))),
        out_shape=pltpu.HBM((2 * half, cols), F32),
        compiler_params=_cparams(("parallel",), 32),
    )(where, _from_hbm(own), _from_hbm(got))


def _small_job(tiles):
    n = len(tiles)

    def copies(ins, outs, sems):
        x, y, c = _place()
        me = 4 * x + 2 * y + c
        local, send, recv = sems
        cps = []
        for t in range(n):
            cps.append(pltpu.make_async_copy(ins[t], outs[t].at[me], local.at[t]))
            for r in range(1, N_DEV):
                fx, fy, fc = (r >> 2) & 1, (r >> 1) & 1, r & 1
                cps.append(_remote(ins[t], outs[t].at[me], send.at[t, r - 1], recv.at[t, r - 1],
                                   (_flip(x, fx), _flip(y, fy), _flip(c, fc))))
        return cps

    return _Job(tiles, [jax.ShapeDtypeStruct((N_DEV,) + t.shape, F32) for t in tiles],
                [pltpu.SemaphoreType.DMA((n,)), pltpu.SemaphoreType.DMA((n, N_DEV - 1)),
                 pltpu.SemaphoreType.DMA((n, N_DEV - 1))], copies)


def _adamw_math(w, g, m, v):
    m = ADAM_B1 * m + (1.0 - ADAM_B1) * g
    v = ADAM_B2 * v + (1.0 - ADAM_B2) * (g * g)
    m_hat = m / (1.0 - ADAM_B1 ** ADAM_STEP)
    v_hat = v / (1.0 - ADAM_B2 ** ADAM_STEP)
    delta = -ADAM_LR * (m_hat / (jnp.sqrt(v_hat) + ADAM_EPS) + ADAM_WD * w)
    return delta, m, v


def _adamw(w, g, m, v, *, name, tr):
    rows, cols = w.shape

    def body(w_ref, g_ref, m_ref, v_ref, go_ref, d_ref, nm_ref, nv_ref):
        g = g_ref[...]
        go_ref[...] = g
        d_ref[...], nm_ref[...], nv_ref[...] = _adamw_math(w_ref[...], g, m_ref[...], v_ref[...])

    spec = pl.BlockSpec((tr, cols), lambda i: (i, 0))
    return pl.pallas_call(
        body, name=name,
        grid=(rows // tr,),
        in_specs=[spec] * 4, out_specs=[spec] * 4,
        out_shape=[jax.ShapeDtypeStruct(w.shape, F32)] * 4,
        compiler_params=_cparams(("parallel",), 32),
    )(*map(_from_hbm, (w, g, m, v)))


def _small_adamw(gathered, gathered_rel, params, moments_m, moments_v):
    n = len(params)

    def body(all_ref, rel_all_ref, *refs):
        w_refs, m_refs, v_refs = refs[:n], refs[n:2 * n], refs[2 * n:3 * n]
        loss_ref = refs[3 * n]
        out_refs = refs[3 * n + 1:]
        g = all_ref[0]
        rel = rel_all_ref[0]
        for d in range(1, N_DEV):
            g = g + all_ref[d]
            rel = rel + rel_all_ref[d]
        misc = g[ROW_MISC:ROW_MISC + 1]
        loss_ref[...] = misc[:, MISC_LOSS:MISC_LOSS + 1]
        grads = (g[ROW_G1:ROW_G1 + 1], g[ROW_G2:ROW_G2 + 1], g[ROW_G3:ROW_G3 + 1], g[ROW_G4:ROW_G4 + 1],
                 misc[:, MISC_GQ:MISC_GQ + HEAD_DIM], misc[:, MISC_GK:MISC_GK + HEAD_DIM],
                 misc[:, MISC_SINK:MISC_SINK + GROUP * N_KV], rel[:, 0:GROUP * N_KV])
        for i in range(n):
            d, nm, nv = _adamw_math(w_refs[i][...], grads[i], m_refs[i][...], v_refs[i][...])
            for j, val in enumerate((grads[i], d, nm, nv)):
                out_refs[4 * i + j][...] = val

    args = (gathered, gathered_rel, *params, *moments_m, *moments_v)
    out_shape = [jax.ShapeDtypeStruct((1, 1), F32)] + [jax.ShapeDtypeStruct(p.shape, F32) for p in params
                                                       for _ in range(4)]
    outs = pl.pallas_call(
        body, name="small_adamw", grid=(1,),
        in_specs=[_whole(a) for a in args], out_specs=[_whole(o) for o in out_shape], out_shape=out_shape,
    )(*map(_from_hbm, args))
    return outs[0], [outs[1 + 4 * i:5 + 4 * i] for i in range(n)]


def kernel(x, w_in, w_o, g_pre_mix, g_post_mix, q_norm_a, k_norm_a, sink_b, rel_bias, g_pre_ffn, w_ffn_up, w_ffn_down, g_post_ffn, loss_target, m_w_in, m_w_o, m_g_pre_mix, m_g_post_mix, m_q_norm_a, m_k_norm_a, m_sink_b, m_rel_bias, m_g_pre_ffn, m_w_ffn_up, m_w_ffn_down, m_g_post_ffn, v_w_in, v_w_o, v_g_pre_mix, v_g_post_mix, v_q_norm_a, v_k_norm_a, v_sink_b, v_rel_bias, v_g_pre_ffn, v_w_ffn_up, v_w_ffn_down, v_g_post_ffn):
    batch, seq, _ = x.shape
    tokens = batch * seq
    where = jnp.stack([2 * lax.axis_index("x") + lax.axis_index("y"), lax.axis_index("c")]).astype(jnp.int32)
    x2 = x.reshape(tokens, D_MODEL)
    g1, g2, g3, g4 = g_pre_mix, g_post_mix, g_pre_ffn, g_post_ffn

    cos, sin = _rope_tables(seq)
    ck, sk = jnp.tile(cos, (1, 2)), jnp.tile(sin, (1, 2))
    cq, sq = ck * SCALE, sk * SCALE
    gq8, gk2 = jnp.tile(q_norm_a, (1, 8)), jnp.tile(k_norm_a, (1, 2))
    bucket, band = _window_tables()
    bias = _bias_build(rel_bias.T, bucket, band)

    w_in_g, w_o_p, w_up_p, w_down_p = _gather_weights(
        (w_in[0], w_o[0], w_ffn_up[0], w_ffn_down[0]), whole=(True, False, False, False))
    (h1, raw, qa, ka, kta, va, vta, qtb, kb, ktb, vb, vtb) = _pre_proj(
        x2, g1, w_in_g, gq8, gk2, cq, sq, ck, sk, seq=seq, tm=min(512, seq), sub=256)
    (oa, p_a, linv_a), (w_part,) = _attn_a_fwd(
        qa, kta, va, seq=seq, bq=min(256, seq), jobs=[_gather_job([w_o_p, w_up_p, w_down_p], forward=False)])
    kb3 = kb.reshape(tokens // BLOCK, BLOCK, KV_WIDTH)
    vb3 = vb.reshape(tokens // BLOCK, BLOCK, KV_WIDTH)
    (ob, lse_b), ((w_o_g, w_up_g, w_down_g),) = _attn_b_fwd(
        qtb, kb3, vtb, bias, sink_b, seq=seq, jobs=[_gather_job(w_part, forward=True)])
    w_o2 = w_o_g.reshape(D_MODEL, D_MODEL)
    w_down2 = w_down_g.reshape(D_FF, D_MODEL)
    mix, x1, h2, o_cat = _wo_post(oa, ob, w_o2, x2, g2, g3, tm=512, sub=256)
    u, df, dy, loss_t, dg4 = _ffn_fwd_loss(h2, w_up_g, w_down2, x1, loss_target.reshape(tokens, D_MODEL), g4, tm=256)

    dz, dx1, dmix, dg3, dg2 = _ffn_bwd_act(df, w_down2, u, w_up_g, x1, dy, mix, g3, g2, tm=256)
    gw_down, _ = _tn_matmul(u, df, name="grad_w_down", tm=1024, tn=1024, tk=min(2048, tokens), square_a=True)
    gw_down = gw_down.reshape(N_CHIPS, FF_CHUNK, D_MODEL)
    gw_up, ((got_down,),) = _tn_matmul(h2, dz, name="grad_w_up", tm=1024, tn=1024, tk=min(2048, tokens), chunked=True,
                                        jobs=[_swap_job([gw_down])])
    doa, dob = _wo_bwd(dmix, w_o2, tm=512)
    gw_o, _ = _tn_matmul(o_cat, dmix, name="grad_w_o", tm=1024, tn=1024, tk=min(2048, tokens))
    gw_o = gw_o.reshape(N_CHIPS, O_CHUNK, D_MODEL)
    sum_down = _add_half(gw_down, got_down, where, name="add_half_w_down", tr=128)
    (dqa, dkta, dvta), ((ex_down,), (got_up,)) = _attn_a_bwd(
        qa, ka, vta, doa, oa, p_a, linv_a, seq=seq, bq=min(256, seq),
        jobs=[_exchange_job([sum_down]), _swap_job([gw_up])])
    full_down = _add_chips(sum_down, ex_down, where, name="add_chips_w_down", tr=128)
    sum_up = _add_half(gw_up, got_up, where, name="add_half_w_up", tr=128)
    (dqb, dkb, dvb, dbias, dsink), ((ex_up,), (g_down,), (got_o,)) = _attn_b_bwd(
        qtb, kb3, ktb, vb3, dob, ob, lse_b, bias, sink_b, seq=seq,
        jobs=[_exchange_job([sum_up]), _join_job([full_down]), _swap_job([gw_o])])
    full_up = _add_chips(sum_up, ex_up, where, name="add_chips_w_up", tr=128)
    sum_o = _add_half(gw_o, got_o, where, name="add_half_w_o", tr=128)
    (grad_x, dproj, dg1, dgq, dgk), _ = _proj_bwd(
        dqa, dkta, dvta, dqb, dkb, dvb, raw, x2, dx1, g1, w_in_g, gq8, gk2, cq, sq, ck, sk,
        seq=seq, tm=min(512, seq), sub=128)
    packed, packed_rel = _pack_small(dg1, dg2, dg3, dg4, dgq, dgk, dsink, dbias, bucket, loss_t)
    gw_in, ((ex_o,), (g_up,), (gathered, gathered_rel)) = _tn_matmul(
        h1, dproj, name="grad_w_in", tm=1024, tn=IN_CHUNK, tk=min(2048, tokens), chunked=True,
        jobs=[_exchange_job([sum_o]), _join_job([full_up]), _small_job([packed, packed_rel])])
    full_o = _add_chips(sum_o, ex_o, where, name="add_chips_w_o", tr=128)

    (g_o,), (got_in,) = _run_jobs("tail_swap", [_join_job([full_o]), _swap_job([gw_in])])
    sum_in = _add_half(gw_in, got_in, where, name="add_half_w_in", tr=128)
    ((ex_in,),) = _run_jobs("tail_exchange", [_exchange_job([sum_in])])
    full_in = _add_chips(sum_in, ex_in, where, name="add_chips_w_in", tr=128)
    ((g_in,),) = _run_jobs("tail_join", [_join_job([full_in])])

    big = [[t[None] for t in _adamw(w[0], g, m[0], v[0], name="adamw_" + nm, tr=128)] for nm, w, g, m, v in (
        ("w_in", w_in, g_in, m_w_in, v_w_in), ("w_o", w_o, g_o, m_w_o, v_w_o),
        ("w_up", w_ffn_up, g_up, m_w_ffn_up, v_w_ffn_up), ("w_down", w_ffn_down, g_down, m_w_ffn_down, v_w_ffn_down))]

    loss, small = _small_adamw(
        gathered, gathered_rel,
        (g1, g2, g3, g4, q_norm_a, k_norm_a, sink_b, rel_bias),
        (m_g_pre_mix, m_g_post_mix, m_g_pre_ffn, m_g_post_ffn, m_q_norm_a, m_k_norm_a, m_sink_b, m_rel_bias),
        (v_g_pre_mix, v_g_post_mix, v_g_pre_ffn, v_g_post_ffn, v_q_norm_a, v_k_norm_a, v_sink_b, v_rel_bias))
    s_g1, s_g2, s_g3, s_g4, s_gq, s_gk, s_sink, s_rel = small

    def leaves(i):
        return (big[0][i], big[1][i], s_g1[i], s_g2[i], s_gq[i], s_gk[i], s_sink[i], s_rel[i], s_g3[i],
                big[2][i], big[3][i], s_g4[i])

    loss = loss.reshape(())
    return (loss, grad_x.reshape(batch, seq, D_MODEL), *leaves(0), *leaves(1), *leaves(2), *leaves(3))
```

```python
import functools

import jax
import jax.numpy as jnp
import numpy as np
from jax import lax
from jax.experimental import pallas as pl
from jax.experimental.pallas import tpu as pltpu

F32 = jnp.float32
MM = jnp.bfloat16

D_MODEL = 1024
HEAD_DIM = 64
N_KV = 2
GROUP = 4
Q_WIDTH = 512
KV_WIDTH = 128
D_FF = 4096
GRID_W = 64
BLOCK = 128
N_BUCKETS = 32
MAX_DISTANCE = 128
ROPE_THETA = 10000.0
EPS = 1e-6
NEG_INF = -1e30
SCALE = HEAD_DIM ** -0.5
IN_TOTAL = 1536
N_CHIPS = 4
N_DEV = 8
IN_CHUNK = IN_TOTAL // N_CHIPS
FF_CHUNK = D_FF // N_CHIPS
O_CHUNK = D_MODEL // N_CHIPS
QK_RAW = 640

ADAM_LR = 0.001
ADAM_B1 = 0.9
ADAM_B2 = 0.999
ADAM_EPS = 1e-08
ADAM_WD = 0.01
ADAM_STEP = 10

LANES = 128
MESH = pl.DeviceIdType.MESH
HBM = pl.BlockSpec(memory_space=pl.ANY)
VMEM = pl.BlockSpec(memory_space=pltpu.VMEM)
SMEM = pl.BlockSpec(memory_space=pltpu.SMEM)

ROW_G1, ROW_G2, ROW_G3, ROW_G4, ROW_MISC = 0, 1, 2, 3, 4
MISC_GQ, MISC_GK, MISC_SINK, MISC_LOSS = 0, 64, 128, 512


def _cparams(sem, vmem_mb):
    return pltpu.CompilerParams(dimension_semantics=sem, vmem_limit_bytes=vmem_mb * 1024 * 1024)


def _whole(a):
    return pl.BlockSpec(a.shape, lambda i: (0,) * len(a.shape))


def _from_hbm(a):
    return pltpu.with_memory_space_constraint(a, pltpu.HBM)


def _in_hbm(s):
    return pltpu.HBM(s.shape, s.dtype)


class _Job:
    def __init__(self, operands, out_shapes, sems, copies, alias=None):
        self.operands, self.out_shapes, self.sems, self.copies = list(operands), list(out_shapes), list(sems), copies
        self.alias = dict(alias or {})


def _place():
    return lax.axis_index("x"), lax.axis_index("y"), lax.axis_index("c")


_CHIP_FLIPS = ((1, 0), (0, 1), (1, 1))


def _flip(v, bit):
    return 1 - v if bit else v


def _remote(src, dst, send, recv, dev):
    return pltpu.make_async_remote_copy(src_ref=src, dst_ref=dst, send_sem=send, recv_sem=recv,
                                        device_id=dev, device_id_type=MESH)


def _swap_job(grads):
    n = len(grads)

    def copies(ins, outs, sems):
        x, y, c = _place()
        send, recv = sems
        cps = []
        for t in range(n):
            half = ins[t].shape[1] // 2
            cps.append(_remote(ins[t].at[:, pl.ds((1 - c) * half, half), :], outs[t], send.at[t], recv.at[t],
                               (x, y, 1 - c)))
        return cps

    shapes = [jax.ShapeDtypeStruct((g.shape[0], g.shape[1] // 2, g.shape[2]), F32) for g in grads]
    return _Job(grads, shapes, [pltpu.SemaphoreType.DMA((n,)), pltpu.SemaphoreType.DMA((n,))], copies)


def _exchange_job(sums):
    n = len(sums)

    def copies(ins, outs, sems):
        x, y, c = _place()
        send, recv = sems
        cps = []
        for t in range(n):
            for r, (fx, fy) in enumerate(_CHIP_FLIPS):
                kk = 2 * _flip(x, fx) + _flip(y, fy)
                cps.append(_remote(ins[t].at[kk], outs[t].at[r], send.at[t, r], recv.at[t, r],
                                   (_flip(x, fx), _flip(y, fy), c)))
        return cps

    shapes = [jax.ShapeDtypeStruct((3,) + s.shape[1:], s.dtype) for s in sums]
    return _Job(sums, shapes, [pltpu.SemaphoreType.DMA((n, 3)), pltpu.SemaphoreType.DMA((n, 3))], copies)


def _join_job(fulls):
    n = len(fulls)

    def copies(ins, outs, sems):
        x, y, c = _place()
        send, recv = sems
        cps = []
        for t in range(n):
            half = ins[t].shape[0] // 2
            rows = pl.ds(c * half, half)
            cps.append(_remote(ins[t].at[rows], outs[t].at[rows], send.at[t], recv.at[t], (x, y, 1 - c)))
        return cps

    shapes = [jax.ShapeDtypeStruct(f.shape, f.dtype) for f in fulls]
    return _Job(fulls, shapes, [pltpu.SemaphoreType.DMA((n,)), pltpu.SemaphoreType.DMA((n,))], copies,
                alias={t: t for t in range(n)})


def _gather_job(bufs, forward):
    n = len(bufs)

    def copies(ins, outs, sems):
        x, y, c = _place()
        send, recv = sems
        cps = []
        for t in range(n):
            half = ins[t].shape[1] // 2
            rows = pl.ds(c * half, half)
            for r, (fx, fy) in enumerate(_CHIP_FLIPS):
                if forward:
                    kk = 2 * _flip(x, fx) + _flip(y, fy)
                    dev = (x, y, 1 - c)
                else:
                    kk = 2 * x + y
                    dev = (_flip(x, fx), _flip(y, fy), c)
                cps.append(_remote(ins[t].at[kk, rows], outs[t].at[kk, rows], send.at[t, r], recv.at[t, r], dev))
        return cps

    shapes = [jax.ShapeDtypeStruct(b.shape, b.dtype) for b in bufs]
    return _Job(bufs, shapes, [pltpu.SemaphoreType.DMA((n, 3)), pltpu.SemaphoreType.DMA((n, 3))], copies,
                alias={t: t for t in range(n)})


def _call(body, args, *, name, grid, in_specs, out_specs, out_shape, scratch_shapes=(), params=None, jobs=()):
    n_in, n_out, n_scr = len(in_specs), len(out_specs), len(scratch_shapes)
    job_in = [len(j.operands) for j in jobs]
    job_out = [len(j.out_shapes) for j in jobs]
    job_sem = [len(j.sems) for j in jobs]

    def wrapped(*refs):
        pos = 0
        ins = refs[pos:pos + n_in]; pos += n_in
        jins = []
        for k in job_in:
            jins.append(refs[pos:pos + k]); pos += k
        outs = refs[pos:pos + n_out]; pos += n_out
        jouts = []
        for k in job_out:
            jouts.append(refs[pos:pos + k]); pos += k
        scr = refs[pos:pos + n_scr]; pos += n_scr
        jsems = []
        for k in job_sem:
            jsems.append(refs[pos:pos + k]); pos += k
        if jobs:
            ids = [pl.program_id(d) for d in range(len(grid))]
            first = functools.reduce(jnp.logical_and, [i == 0 for i in ids])
            last = functools.reduce(jnp.logical_and, [i == g - 1 for i, g in zip(ids, grid)])

            @pl.when(first)
            def _():
                for j, ji, jo, js in zip(jobs, jins, jouts, jsems):
                    for cp in j.copies(ji, jo, js):
                        cp.start()

        body(*ins, *outs, *scr)
        if jobs:
            @pl.when(last)
            def _():
                for j, ji, jo, js in zip(jobs, jins, jouts, jsems):
                    for cp in j.copies(ji, jo, js):
                        cp.wait()

    aliases = {}
    in_pos, out_pos = n_in, n_out
    for j in jobs:
        for i, o in j.alias.items():
            aliases[in_pos + i] = out_pos + o
        in_pos += len(j.operands)
        out_pos += len(j.out_shapes)
    res = pl.pallas_call(
        wrapped, name=name, grid=grid,
        in_specs=list(in_specs) + [HBM] * sum(job_in),
        out_specs=list(out_specs) + [HBM] * sum(job_out),
        out_shape=list(out_shape) + [_in_hbm(s) for j in jobs for s in j.out_shapes],
        scratch_shapes=list(scratch_shapes) + [s for j in jobs for s in j.sems],
        input_output_aliases=aliases,
        compiler_params=params,
    )(*[a if spec is SMEM else _from_hbm(a) for a, spec in zip(args, in_specs)],
      *[a for j in jobs for a in j.operands])
    own, rest = list(res[:n_out]), list(res[n_out:])
    job_res = []
    for k in job_out:
        job_res.append(rest[:k])
        rest = rest[k:]
    return own, job_res


def _run_jobs(name, jobs):
    def body():
        pass

    return _call(body, (), name=name, grid=(1,), in_specs=[], out_specs=[], out_shape=[], jobs=jobs)[1]


def _dot(a, b):
    return jnp.dot(a, b, preferred_element_type=F32)


def _dot_nt(a, b):
    return lax.dot_general(a, b, (((1,), (1,)), ((), ())), preferred_element_type=F32)


def _dot_tn(a, b):
    return lax.dot_general(a, b, (((0,), (0,)), ((), ())), preferred_element_type=F32)


def _rms_r(x):
    return lax.rsqrt(jnp.mean(x * x, axis=-1, keepdims=True) + EPS)


def _rms_bwd(x, r, g, dy):
    n = x * r
    dn = dy * g
    dx = r * (dn - n * jnp.mean(dn * n, axis=-1, keepdims=True))
    return dx, dy * n


def _seg64_sum(v):
    rows, width = v.shape
    lane = lax.broadcasted_iota(jnp.int32, (rows, LANES), 1)
    lo = lane < HEAD_DIM
    outs = []
    for c in range(width // LANES):
        ch = v[:, c * LANES:(c + 1) * LANES]
        s_lo = jnp.sum(jnp.where(lo, ch, 0.0), axis=-1, keepdims=True)
        s_hi = jnp.sum(jnp.where(lo, 0.0, ch), axis=-1, keepdims=True)
        outs.append(jnp.where(lo, s_lo, s_hi))
    return outs[0] if len(outs) == 1 else jnp.concatenate(outs, axis=-1)


def _head_r(v):
    return lax.rsqrt(_seg64_sum(v * v) * (1.0 / HEAD_DIM) + EPS)


def _swap16(ch):
    lane = lax.broadcasted_iota(jnp.int32, ch.shape, 1)
    return jnp.where((lane % 32) < 16, pltpu.roll(ch, LANES - 16, 1), pltpu.roll(ch, 16, 1))


def _by_chunk(fn, v):
    outs = [fn(v[:, c * LANES:(c + 1) * LANES]) for c in range(v.shape[1] // LANES)]
    return outs[0] if len(outs) == 1 else jnp.concatenate(outs, axis=-1)


def _rope(v, cos, sin_signed):
    return _by_chunk(lambda ch: ch * cos + _swap16(ch) * sin_signed, v)


def _rope_t(g, cos, sin_signed):
    return _by_chunk(lambda ch: ch * cos + _swap16(ch * sin_signed), g)


def _rope_tables(seq):
    nf = HEAD_DIM // 4
    freqs = ROPE_THETA ** (-jnp.arange(nf, dtype=F32) / nf)
    pos = jnp.arange(seq, dtype=jnp.int32)
    row = (pos // GRID_W).astype(F32)
    col = (pos % GRID_W).astype(F32)
    ang_r = row[:, None] * freqs[None, :]
    ang_c = col[:, None] * freqs[None, :]
    cr, sr, cc, sc = jnp.cos(ang_r), jnp.sin(ang_r), jnp.cos(ang_c), jnp.sin(ang_c)
    cos = jnp.concatenate([cr, cr, cc, cc], axis=1)
    sin = jnp.concatenate([-sr, sr, -sc, sc], axis=1)
    return cos, sin


def _t5_bucket(rel):
    nb = N_BUCKETS // 2
    ret = (rel > 0).astype(jnp.int32) * nb
    n = jnp.abs(rel)
    max_exact = nb // 2
    nf = jnp.maximum(n, 1).astype(jnp.float32)
    large = max_exact + (jnp.log(nf / max_exact) / np.float32(np.log(MAX_DISTANCE / max_exact))
                         * (nb - max_exact)).astype(jnp.int32)
    large = jnp.minimum(large, nb - 1)
    return ret + jnp.where(n < max_exact, n, large)


def _window_tables():
    a = jnp.arange(BLOCK, dtype=jnp.int32)
    c = jnp.arange(3 * BLOCK, dtype=jnp.int32)
    rel = c[None, :] - BLOCK - a[:, None]
    bucket = _t5_bucket(rel)
    band = (jnp.abs(rel) <= BLOCK).astype(jnp.int32)
    to3 = lambda t: t.reshape(BLOCK, 3, BLOCK).transpose(1, 2, 0)
    return to3(bucket), to3(band)


def _pre_proj(x, g1, w_in, gq, gk, cq, sq, ck, sk, *, seq, tm, sub):
    tokens = x.shape[0]
    n_seq = seq // tm
    nblk = tm // BLOCK
    batch = tokens // seq

    def body(x_ref, g1_ref, w_ref, gq_ref, gk_ref, cq_ref, sq_ref, ck_ref, sk_ref,
             h1_ref, raw_ref, qa_ref, ka_ref, kta_ref, va_ref, vta_ref,
             qtb_ref, kb_ref, ktb_ref, vb_ref, vtb_ref, proj):
        for r in range(tm // sub):
            rows = slice(r * sub, (r + 1) * sub)
            xv = x_ref[rows, :]
            h = (xv * _rms_r(xv) * g1_ref[...]).astype(MM)
            h1_ref[rows, :] = h
            for j in range(N_CHIPS):
                proj[rows, j * IN_CHUNK:(j + 1) * IN_CHUNK] = _dot(h, w_ref[j])
            qa = proj[rows, 0:Q_WIDTH]
            ka = proj[rows, Q_WIDTH:QK_RAW]
            raw_ref[rows, :] = proj[rows, 0:QK_RAW]
            qn = qa * _head_r(qa) * gq_ref[...]
            qa_ref[rows, :] = _rope(qn, cq_ref[rows, :], sq_ref[rows, :]).astype(MM)
            kn = ka * _head_r(ka) * gk_ref[...]
            kr = _rope(kn, ck_ref[rows, :], sk_ref[rows, :])
            ka_ref[rows, :] = kr.astype(MM)
            kta_ref[0, :, rows] = kr.T.astype(MM)
            va = proj[rows, 640:768]
            va_ref[rows, :] = va.astype(MM)
            vta_ref[0, :, rows] = va.T.astype(MM)
            qb = proj[rows, 768:1280] * SCALE
            kb = proj[rows, 1280:1408]
            vb = proj[rows, 1408:1536]
            kb_ref[rows, :] = kb.astype(MM)
            vb_ref[rows, :] = vb.astype(MM)
            for j in range(sub // BLOCK):
                blk = slice(j * BLOCK, (j + 1) * BLOCK)
                qtb_ref[r * (sub // BLOCK) + j] = qb[blk, :].T.astype(MM)
                ktb_ref[r * (sub // BLOCK) + j] = kb[blk, :].T.astype(MM)
                vtb_ref[r * (sub // BLOCK) + j] = vb[blk, :].T.astype(MM)

    tok = lambda w: pl.BlockSpec((tm, w), lambda i: (i, 0))
    tab = lambda w: pl.BlockSpec((tm, w), lambda i: (i % n_seq, 0))
    row = lambda w: pl.BlockSpec((1, w), lambda i: (0, 0))
    tposed = pl.BlockSpec((1, LANES, tm), lambda i: (i // n_seq, 0, i % n_seq))
    blocks = pl.BlockSpec((nblk, BLOCK, LANES), lambda i: (i, 0, 0))
    qblocks = pl.BlockSpec((nblk, Q_WIDTH, BLOCK), lambda i: (i, 0, 0))
    tok_mm = lambda w: jax.ShapeDtypeStruct((tokens, w), MM)
    return pl.pallas_call(
        body, name="pre_proj",
        grid=(tokens // tm,),
        in_specs=[tok(D_MODEL), row(D_MODEL),
                  pl.BlockSpec((N_CHIPS, D_MODEL, IN_CHUNK), lambda i: (0, 0, 0)),
                  row(Q_WIDTH), row(KV_WIDTH), tab(KV_WIDTH), tab(KV_WIDTH), tab(KV_WIDTH), tab(KV_WIDTH)],
        out_specs=[tok(D_MODEL), tok(QK_RAW), tok(Q_WIDTH), tok(KV_WIDTH), tposed, tok(KV_WIDTH), tposed,
                   qblocks, tok(KV_WIDTH), blocks, tok(KV_WIDTH), blocks],
        out_shape=[
            tok_mm(D_MODEL),
            jax.ShapeDtypeStruct((tokens, QK_RAW), F32),
            tok_mm(Q_WIDTH),
            tok_mm(KV_WIDTH),
            jax.ShapeDtypeStruct((batch, KV_WIDTH, seq), MM),
            tok_mm(KV_WIDTH),
            jax.ShapeDtypeStruct((batch, KV_WIDTH, seq), MM),
            jax.ShapeDtypeStruct((tokens // BLOCK, Q_WIDTH, BLOCK), MM),
            tok_mm(KV_WIDTH),
            jax.ShapeDtypeStruct((tokens // BLOCK, KV_WIDTH, BLOCK), MM),
            tok_mm(KV_WIDTH),
            jax.ShapeDtypeStruct((tokens // BLOCK, KV_WIDTH, BLOCK), MM),
        ],
        scratch_shapes=[pltpu.VMEM((tm, IN_TOTAL), F32)],
        compiler_params=_cparams(("parallel",), 48),
    )(*map(_from_hbm, (x, g1, w_in, gq, gk, cq, sq, ck, sk)))


def _kv_half(v2, kv):
    return jnp.where(kv == 0, v2[:, :HEAD_DIM], v2[:, HEAD_DIM:])


def _attn_a_fwd(qa, kta, va, *, seq, bq, jobs=()):
    tokens = qa.shape[0]
    batch = tokens // seq
    nq = seq // bq

    def body(q_ref, kt_ref, v_ref, o_ref, p_ref, linv_ref):
        kv = pl.program_id(1)
        kt = kt_ref[0]
        lane = lax.broadcasted_iota(jnp.int32, (seq, KV_WIDTH), 1)
        v = jnp.where((lane < HEAD_DIM) == (kv == 0), v_ref[...], jnp.ones((), MM))
        for g in range(GROUP):
            sl = slice(g * HEAD_DIM, (g + 1) * HEAD_DIM)
            s = _dot(q_ref[:, sl], kt)
            pb = jnp.exp((s - jnp.max(s, axis=-1, keepdims=True)).astype(MM))
            p_ref[0, g] = pb
            o2 = _dot(pb, v)
            linv = 1.0 / _kv_half(o2, 1 - kv)[:, 0:1]
            o_ref[:, sl] = _kv_half(o2, kv) * linv
            linv_ref[0, :, g:g + 1] = linv

    return _call(
        body, (qa, kta, va), name="attn_a_fwd", jobs=jobs,
        grid=(batch, N_KV, nq),
        in_specs=[pl.BlockSpec((bq, GROUP * HEAD_DIM), lambda b, k, i: (b * nq + i, k)),
                  pl.BlockSpec((1, HEAD_DIM, seq), lambda b, k, i: (b, k, 0)),
                  pl.BlockSpec((seq, KV_WIDTH), lambda b, k, i: (b, 0))],
        out_specs=[pl.BlockSpec((bq, GROUP * HEAD_DIM), lambda b, k, i: (b * nq + i, k)),
                   pl.BlockSpec((1, GROUP, bq, seq), lambda b, k, i: (k, 0, b * nq + i, 0)),
                   pl.BlockSpec((1, bq, GROUP), lambda b, k, i: (k, b * nq + i, 0))],
        out_shape=[jax.ShapeDtypeStruct((tokens, Q_WIDTH), F32),
                   jax.ShapeDtypeStruct((N_KV, GROUP, tokens, seq), MM),
                   jax.ShapeDtypeStruct((N_KV, tokens, GROUP), F32)],
        params=_cparams(("arbitrary", "arbitrary", "arbitrary"), 56))


def _attn_a_bwd(qa, ka, vta, do, o, p, linv, *, seq, bq, jobs=()):
    tokens = qa.shape[0]
    batch = tokens // seq
    nq = seq // bq

    def body(q_ref, k_ref, vt_ref, do_ref, o_ref, p_ref, linv_ref, dq_ref, dkt_ref, dvt_ref):
        kv = pl.program_id(1)

        @pl.when(pl.program_id(2) == 0)
        def _():
            dkt_ref[...] = jnp.zeros_like(dkt_ref)
            dvt_ref[...] = jnp.zeros_like(dvt_ref)

        vt = vt_ref[0]
        k2 = k_ref[...]
        for g in range(GROUP):
            sl = slice(g * HEAD_DIM, (g + 1) * HEAD_DIM)
            dof = do_ref[:, sl]
            delta = jnp.sum(dof * o_ref[:, sl], axis=-1, keepdims=True)
            linv_g = linv_ref[0, :, g:g + 1]
            pb = p_ref[0, g]
            dp = _dot(dof.astype(MM), vt)
            ds = pb * ((dp - delta) * linv_g).astype(MM)
            dq_ref[:, sl] = _kv_half(_dot(ds, k2), kv)
            dkt_ref[0] += _dot_tn(q_ref[:, sl], ds)
            dvt_ref[0] += _dot_tn((dof * linv_g).astype(MM), pb)

    qspec = pl.BlockSpec((bq, GROUP * HEAD_DIM), lambda b, k, i: (b * nq + i, k))
    tspec = pl.BlockSpec((1, HEAD_DIM, seq), lambda b, k, i: (b, k, 0))
    return _call(
        body, (qa, ka, vta, do, o, p, linv), name="attn_a_bwd", jobs=jobs,
        grid=(batch, N_KV, nq),
        in_specs=[qspec, pl.BlockSpec((seq, KV_WIDTH), lambda b, k, i: (b, 0)), tspec, qspec, qspec,
                  pl.BlockSpec((1, GROUP, bq, seq), lambda b, k, i: (k, 0, b * nq + i, 0)),
                  pl.BlockSpec((1, bq, GROUP), lambda b, k, i: (k, b * nq + i, 0))],
        out_specs=[qspec, tspec, tspec],
        out_shape=[jax.ShapeDtypeStruct((tokens, Q_WIDTH), F32),
                   jax.ShapeDtypeStruct((batch, KV_WIDTH, seq), F32),
                   jax.ShapeDtypeStruct((batch, KV_WIDTH, seq), F32)],
        params=_cparams(("arbitrary", "arbitrary", "arbitrary"), 56))


def _bias_build(rel_bias_t, bucket_t, band_t):
    def body(tab_ref, bucket_ref, band_ref, bias_ref):
        for h in range(GROUP * N_KV):
            for piece in range(3):
                bk = bucket_ref[piece]
                acc = jnp.zeros((BLOCK, BLOCK), F32)
                for b in range(N_BUCKETS):
                    acc = jnp.where(bk == b, tab_ref[h, b], acc)
                g = h % GROUP
                bias_ref[h // GROUP, piece, :, g * BLOCK:(g + 1) * BLOCK] = jnp.where(band_ref[piece] != 0, acc, NEG_INF)

    out = jax.ShapeDtypeStruct((N_KV, 3, BLOCK, GROUP * BLOCK), F32)
    return pl.pallas_call(
        body, name="bias_build", grid=(1,),
        in_specs=[SMEM, _whole(bucket_t), _whole(band_t)], out_specs=_whole(out), out_shape=out,
    )(rel_bias_t, bucket_t, band_t)


def _pad_heads(t, kv):
    outs = []
    for g in range(GROUP):
        tg = t[g * HEAD_DIM:(g + 1) * HEAD_DIM, :]
        zero = jnp.zeros_like(tg)
        outs.append(jnp.concatenate([jnp.where(kv == 0, tg, zero), jnp.where(kv == 0, zero, tg)], axis=0))
    return jnp.concatenate(outs, axis=-1)


def _unpad_heads(t, kv):
    outs = [_kv_half(t[:, g * BLOCK:(g + 1) * BLOCK].T, kv) for g in range(GROUP)]
    return jnp.concatenate(outs, axis=-1)


def _sink_row(sink_ref, kv):
    lane_head = lax.broadcasted_iota(jnp.int32, (1, GROUP * BLOCK), 1) // BLOCK
    row = jnp.zeros((1, GROUP * BLOCK), F32)
    for g in range(GROUP):
        row = jnp.where(lane_head == g, sink_ref[0, kv * GROUP + g], row)
    return row


def _window_scores_t(k_ref, idx, qpad, bias_ref, n, nblk):
    pieces = []
    for piece in range(3):
        s = _dot(k_ref[idx[piece]], qpad) + bias_ref[0, piece]
        if piece == 0:
            s = jnp.where(n > 0, s, NEG_INF)
        if piece == 2:
            s = jnp.where(n < nblk - 1, s, NEG_INF)
        pieces.append(s)
    return pieces


def _attn_b_fwd(qtb, kb3, vtb, bias, sink, *, seq, jobs=()):
    nblk_all = qtb.shape[0]
    tokens = nblk_all * BLOCK
    batch = tokens // seq
    nblk = seq // BLOCK

    def body(sink_ref, q_ref, k_ref, vt_ref, bias_ref, o_ref, lse_ref):
        kv = pl.program_id(0)
        sink_row = _sink_row(sink_ref, kv)

        def block(n, carry):
            idx = (jnp.maximum(n - 1, 0), n, jnp.minimum(n + 1, nblk - 1))
            rows = pl.ds(pl.multiple_of(n * BLOCK, BLOCK), BLOCK)
            qpad = _pad_heads(q_ref[n], kv)
            ss = _window_scores_t(k_ref, idx, qpad, bias_ref, n, nblk)
            m = jnp.maximum(jnp.maximum(jnp.max(ss[0], axis=0, keepdims=True),
                                        jnp.max(ss[1], axis=0, keepdims=True)),
                            jnp.maximum(jnp.max(ss[2], axis=0, keepdims=True), sink_row))
            ps = [jnp.exp(s - m) for s in ss]
            denom = (jnp.sum(ps[0], axis=0, keepdims=True) + jnp.sum(ps[1], axis=0, keepdims=True)
                     + jnp.sum(ps[2], axis=0, keepdims=True) + jnp.exp(sink_row - m))
            ot = (_dot(vt_ref[idx[0]], ps[0].astype(MM)) + _dot(vt_ref[idx[1]], ps[1].astype(MM))
                  + _dot(vt_ref[idx[2]], ps[2].astype(MM)))
            o_ref[rows, :] = _unpad_heads(ot * (1.0 / denom), kv)
            lse_ref[0, n] = jnp.broadcast_to(m + jnp.log(denom), (8, GROUP * BLOCK))
            return carry

        lax.fori_loop(0, nblk, block, 0, unroll=8)

    both = pl.BlockSpec((nblk, BLOCK, KV_WIDTH), lambda k, b: (b, 0, 0))
    return _call(
        body, (sink, qtb, kb3, vtb, bias), name="attn_b_fwd", jobs=jobs,
        grid=(N_KV, batch),
        in_specs=[SMEM, pl.BlockSpec((nblk, GROUP * HEAD_DIM, BLOCK), lambda k, b: (b, k, 0)), both, both,
                  pl.BlockSpec((1, 3, BLOCK, GROUP * BLOCK), lambda k, b: (k, 0, 0, 0))],
        out_specs=[pl.BlockSpec((seq, GROUP * HEAD_DIM), lambda k, b: (b, k)),
                   pl.BlockSpec((1, nblk, 8, GROUP * BLOCK), lambda k, b: (k, b, 0, 0))],
        out_shape=[jax.ShapeDtypeStruct((tokens, Q_WIDTH), F32),
                   jax.ShapeDtypeStruct((N_KV, nblk_all, 8, GROUP * BLOCK), F32)],
        params=_cparams(("arbitrary", "arbitrary"), 48))


def _attn_b_bwd(qtb, kb3, ktb, vb3, do, o, lse, bias, sink, *, seq, jobs=()):
    nblk_all = qtb.shape[0]
    tokens = nblk_all * BLOCK
    batch = tokens // seq
    nblk = seq // BLOCK

    def body(sink_ref, q_ref, k_ref, kt_ref, v_ref, do_ref, o_ref, lse_ref, bias_ref,
             dq_ref, dk_ref, dv_ref, dbias_ref, dsink_ref):
        kv = pl.program_id(0)
        sink_row = _sink_row(sink_ref, kv)

        @pl.when(pl.program_id(1) == 0)
        def _():
            dbias_ref[...] = jnp.zeros_like(dbias_ref)
            dsink_ref[...] = jnp.zeros_like(dsink_ref)

        dk_ref[...] = jnp.zeros_like(dk_ref)
        dv_ref[...] = jnp.zeros_like(dv_ref)

        def block(n, dsink):
            idx = (jnp.maximum(n - 1, 0), n, jnp.minimum(n + 1, nblk - 1))
            rows = pl.ds(pl.multiple_of(n * BLOCK, BLOCK), BLOCK)
            qpad = _pad_heads(q_ref[n], kv)
            dot_t = do_ref[rows, :].T
            prod = dot_t * o_ref[rows, :].T
            delta = jnp.concatenate(
                [jnp.sum(prod[g * HEAD_DIM:(g + 1) * HEAD_DIM, :], axis=0, keepdims=True) for g in range(GROUP)],
                axis=-1)
            dopad = _pad_heads(dot_t.astype(MM), kv)
            lse_row = lse_ref[0, n][0:1, :]
            ss = _window_scores_t(k_ref, idx, qpad, bias_ref, n, nblk)
            dqt = jnp.zeros((KV_WIDTH, GROUP * BLOCK), F32)
            for piece in range(3):
                pt = jnp.exp(ss[piece] - lse_row)
                dst = pt * (_dot(v_ref[idx[piece]], dopad) - delta)
                dsb = dst.astype(MM)
                dbias_ref[0, piece] += dst
                dqt = dqt + _dot(kt_ref[idx[piece]], dsb)
                dk_ref[0, idx[piece]] += _dot_nt(dsb, qpad)
                dv_ref[0, idx[piece]] += _dot_nt(pt.astype(MM), dopad)
            dq_ref[rows, :] = _unpad_heads(dqt, kv)
            return dsink - jnp.exp(sink_row - lse_row) * delta

        dsink = lax.fori_loop(
            0, nblk // 4, lambda i, c: block(4 * i + 3, block(4 * i + 2, block(4 * i + 1, block(4 * i, c)))),
            jnp.zeros((1, GROUP * BLOCK), F32))
        dsink_ref[0] += jnp.broadcast_to(dsink, (8, GROUP * BLOCK))

    qspec = pl.BlockSpec((seq, GROUP * HEAD_DIM), lambda k, b: (b, k))
    both = pl.BlockSpec((nblk, BLOCK, KV_WIDTH), lambda k, b: (b, 0, 0))
    grad = pl.BlockSpec((1, nblk, BLOCK, KV_WIDTH), lambda k, b: (k, b, 0, 0))
    return _call(
        body, (sink, qtb, kb3, ktb, vb3, do, o, lse, bias), name="attn_b_bwd", jobs=jobs,
        grid=(N_KV, batch),
        in_specs=[SMEM, pl.BlockSpec((nblk, GROUP * HEAD_DIM, BLOCK), lambda k, b: (b, k, 0)), both, both, both,
                  qspec, qspec, pl.BlockSpec((1, nblk, 8, GROUP * BLOCK), lambda k, b: (k, b, 0, 0)),
                  pl.BlockSpec((1, 3, BLOCK, GROUP * BLOCK), lambda k, b: (k, 0, 0, 0))],
        out_specs=[qspec, grad, grad,
                   pl.BlockSpec((1, 3, BLOCK, GROUP * BLOCK), lambda k, b: (k, 0, 0, 0)),
                   pl.BlockSpec((1, 8, GROUP * BLOCK), lambda k, b: (k, 0, 0))],
        out_shape=[jax.ShapeDtypeStruct((tokens, Q_WIDTH), F32),
                   jax.ShapeDtypeStruct((N_KV, nblk_all, BLOCK, KV_WIDTH), F32),
                   jax.ShapeDtypeStruct((N_KV, nblk_all, BLOCK, KV_WIDTH), F32),
                   jax.ShapeDtypeStruct((N_KV, 3, BLOCK, GROUP * BLOCK), F32),
                   jax.ShapeDtypeStruct((N_KV, 8, GROUP * BLOCK), F32)],
        params=_cparams(("arbitrary", "arbitrary"), 48))


def _resident(shape):
    return pl.BlockSpec(shape, lambda i: (0,) * len(shape), pipeline_mode=pl.Buffered(1))


def _mix_ffn_fwd(oa, ob, w_o, x, g2, g3, w_up, w_down, target, g4, *, tm):
    tokens = x.shape[0]
    nt = tokens // tm

    def body(oa_ref, ob_ref, wo_ref, x_ref, g2_ref, g3_ref, wu_ref, wd_ref, t_ref, g4_ref,
             mix_ref, x1_ref, h2_ref, o_ref, u_ref, df_ref, dy_ref, loss_ref, dg4_ref):
        o = jnp.concatenate([oa_ref[...].astype(MM), ob_ref[...].astype(MM)], axis=-1)
        o_ref[...] = o
        mix = _dot(o, wo_ref[...])
        mix_ref[...] = mix
        x1 = x_ref[...] + mix * _rms_r(mix) * g2_ref[...]
        x1_ref[...] = x1
        h2v = (x1 * _rms_r(x1) * g3_ref[...]).astype(MM)
        h2_ref[...] = h2v
        f = jnp.zeros((tm, D_MODEL), F32)
        for c in range(N_CHIPS):
            u = jnp.maximum(_dot(h2v, wu_ref[c]), 0.0)
            u_ref[:, c * FF_CHUNK:(c + 1) * FF_CHUNK] = u.astype(MM)
            f = f + _dot((u * u).astype(MM), wd_ref[c * FF_CHUNK:(c + 1) * FF_CHUNK, :])
        r = _rms_r(f)
        g4v = g4_ref[...]
        err = x1 + f * r * g4v - t_ref[...]
        sq = jnp.sum(err * err, axis=-1, keepdims=True)
        loss_ref[0] = jnp.broadcast_to(jnp.sum(sq, axis=0, keepdims=True) * (0.5 / D_MODEL), (8, LANES))
        dy = err * (1.0 / D_MODEL)
        dy_ref[...] = dy
        dfv, dgv = _rms_bwd(f, r, g4v, dy)
        df_ref[...] = dfv.astype(MM)
        dg4_ref[0] = jnp.sum(dgv, axis=0, keepdims=True)

    tok = pl.BlockSpec((tm, D_MODEL), lambda i: (i, 0))
    half = pl.BlockSpec((tm, Q_WIDTH), lambda i: (i, 0))
    row = pl.BlockSpec((1, D_MODEL), lambda i: (0, 0))
    tok_f32 = jax.ShapeDtypeStruct((tokens, D_MODEL), F32)
    tok_mm = jax.ShapeDtypeStruct((tokens, D_MODEL), MM)
    return pl.pallas_call(
        body, name="mix_ffn_fwd",
        grid=(nt,),
        in_specs=[half, half, _resident((D_MODEL, D_MODEL)), tok, row, row,
                  _resident((N_CHIPS, D_MODEL, FF_CHUNK)), _resident((D_FF, D_MODEL)), tok, row],
        out_specs=[tok, tok, tok, tok, pl.BlockSpec((tm, D_FF), lambda i: (i, 0)), tok, tok,
                   pl.BlockSpec((1, 8, LANES), lambda i: (i, 0, 0)),
                   pl.BlockSpec((1, 1, D_MODEL), lambda i: (i, 0, 0))],
        out_shape=[tok_f32,
                   tok_f32,
                   tok_mm,
                   tok_mm,
                   jax.ShapeDtypeStruct((tokens, D_FF), MM),
                   tok_mm,
                   tok_f32,
                   jax.ShapeDtypeStruct((nt, 8, LANES), F32),
                   jax.ShapeDtypeStruct((nt, 1, D_MODEL), F32)],
        compiler_params=_cparams(("parallel",), 56),
    )(*map(_from_hbm, (oa, ob, w_o, x, g2, g3, w_up, w_down, target, g4)))


def _ffn_bwd_act(df, w_down, u, w_up, x1, dy, mix, g3, g2, w_o, *, tm):
    tokens = df.shape[0]
    nt = tokens // tm

    def body(df_ref, wd_ref, u_ref, wu_ref, x1_ref, dy_ref, mix_ref, g3_ref, g2_ref, wo_ref,
             dz_ref, dx1_ref, dmix_ref, dg3_ref, dg2_ref, doa_ref, dob_ref):
        dfv = df_ref[...]
        dh2 = jnp.zeros((tm, D_MODEL), F32)
        for c in range(N_CHIPS):
            cols = slice(c * FF_CHUNK, (c + 1) * FF_CHUNK)
            da = _dot_nt(dfv, wd_ref[cols, :])
            dz = (da * (2.0 * u_ref[:, cols].astype(F32))).astype(MM)
            dz_ref[:, cols] = dz
            dh2 = dh2 + _dot_nt(dz, wu_ref[c])
        x1 = x1_ref[...]
        dxn, dg3v = _rms_bwd(x1, _rms_r(x1), g3_ref[...], dh2)
        dx1 = dy_ref[...] + dxn
        dx1_ref[...] = dx1
        dg3_ref[0] = jnp.sum(dg3v, axis=0, keepdims=True)
        mix = mix_ref[...]
        dmix, dg2v = _rms_bwd(mix, _rms_r(mix), g2_ref[...], dx1)
        dmb = dmix.astype(MM)
        dmix_ref[...] = dmb
        dg2_ref[0] = jnp.sum(dg2v, axis=0, keepdims=True)
        doa_ref[...] = _dot_nt(dmb, wo_ref[0:Q_WIDTH, :])
        dob_ref[...] = _dot_nt(dmb, wo_ref[Q_WIDTH:D_MODEL, :])

    tok = pl.BlockSpec((tm, D_MODEL), lambda i: (i, 0))
    half = pl.BlockSpec((tm, Q_WIDTH), lambda i: (i, 0))
    wide = pl.BlockSpec((tm, D_FF), lambda i: (i, 0))
    row = pl.BlockSpec((1, D_MODEL), lambda i: (0, 0))
    part = pl.BlockSpec((1, 1, D_MODEL), lambda i: (i, 0, 0))
    return pl.pallas_call(
        body, name="ffn_bwd_act",
        grid=(nt,),
        in_specs=[tok, _resident((D_FF, D_MODEL)), wide, _resident((N_CHIPS, D_MODEL, FF_CHUNK)),
                  tok, tok, tok, row, row, _resident((D_MODEL, D_MODEL))],
        out_specs=[wide, tok, tok, part, part, half, half],
        out_shape=[jax.ShapeDtypeStruct((tokens, D_FF), MM),
                   jax.ShapeDtypeStruct((tokens, D_MODEL), F32),
                   jax.ShapeDtypeStruct((tokens, D_MODEL), MM),
                   jax.ShapeDtypeStruct((nt, 1, D_MODEL), F32),
                   jax.ShapeDtypeStruct((nt, 1, D_MODEL), F32),
                   jax.ShapeDtypeStruct((tokens, Q_WIDTH), F32),
                   jax.ShapeDtypeStruct((tokens, Q_WIDTH), F32)],
        compiler_params=_cparams(("parallel",), 56),
    )(*map(_from_hbm, (df, w_down, u, w_up, x1, dy, mix, g3, g2, w_o)))


def _tn_matmul(a, b, *, name, tm, tn, tk, chunked=False, square_a=False, vmem_mb=48, jobs=()):
    tokens, m_dim = a.shape
    n_dim = b.shape[1]
    if chunked:
        assert tm == m_dim

    def body(a_ref, b_ref, o_ref):
        av = a_ref[...]
        if square_a:
            av = av.astype(F32)
            av = av * av
        part = _dot_tn(av.astype(MM), b_ref[...].astype(MM))
        part = part[None] if chunked else part

        @pl.when(pl.program_id(2) == 0)
        def _():
            o_ref[...] = part

        @pl.when(pl.program_id(2) > 0)
        def _():
            o_ref[...] += part

    if chunked:
        out_spec = pl.BlockSpec((1, tm, tn), lambda i, j, k: (j, 0, 0))
        out_shape = jax.ShapeDtypeStruct((n_dim // tn, m_dim, tn), F32)
    else:
        out_spec = pl.BlockSpec((tm, tn), lambda i, j, k: (i, j))
        out_shape = jax.ShapeDtypeStruct((m_dim, n_dim), F32)
    (out,), job_res = _call(
        body, (a, b), name=name, jobs=jobs,
        grid=(m_dim // tm, n_dim // tn, tokens // tk),
        in_specs=[pl.BlockSpec((tk, tm), lambda i, j, k: (k, i)),
                  pl.BlockSpec((tk, tn), lambda i, j, k: (k, j))],
        out_specs=[out_spec], out_shape=[_in_hbm(out_shape)],
        params=_cparams(("arbitrary", "arbitrary", "arbitrary"), vmem_mb))
    return out, job_res


def _proj_bwd(dqa, dkta, dvta, dqb, dktb, dvtb, raw, x, dx1, g1, w_in, gq, gk, cq, sq, ck, sk, *, seq, tm, sub,
              jobs=()):
    tokens = x.shape[0]
    nt = tokens // tm
    n_seq = seq // tm
    nblk = tm // BLOCK

    def body(dqa_ref, dkta_ref, dvta_ref, dqb_ref, dkb_ref, dvb_ref, raw_ref, x_ref, dx1_ref, g1_ref, w_ref,
             gq_ref, gk_ref, cq_ref, sq_ref, ck_ref, sk_ref,
             gx_ref, dproj_ref, dg1_ref, dgq_ref, dgk_ref, dp):
        parts = []
        for r in range(tm // sub):
            rows = slice(r * sub, (r + 1) * sub)
            qa = raw_ref[rows, 0:Q_WIDTH]
            dqn = _rope_t(dqa_ref[rows, :], cq_ref[rows, :], sq_ref[rows, :])
            rq = _head_r(qa)
            nq = qa * rq
            dnq = dqn * gq_ref[...]
            dp[rows, 0:Q_WIDTH] = rq * (dnq - nq * (_seg64_sum(dnq * nq) * (1.0 / HEAD_DIM)))

            ka = raw_ref[rows, Q_WIDTH:QK_RAW]
            dkn = _rope_t(dkta_ref[0, :, rows].T, ck_ref[rows, :], sk_ref[rows, :])
            rk = _head_r(ka)
            nk = ka * rk
            dnk = dkn * gk_ref[...]
            dp[rows, 512:640] = rk * (dnk - nk * (_seg64_sum(dnk * nk) * (1.0 / HEAD_DIM)))

            dp[rows, 640:768] = dvta_ref[0, :, rows].T
            dp[rows, 768:1280] = dqb_ref[rows, :] * SCALE
            for j in range(r * sub // BLOCK, (r + 1) * sub // BLOCK):
                dp[j * BLOCK:(j + 1) * BLOCK, 1280:1408] = dkb_ref[0, j] + dkb_ref[1, j]
                dp[j * BLOCK:(j + 1) * BLOCK, 1408:1536] = dvb_ref[0, j] + dvb_ref[1, j]

            dproj = dp[rows, :].astype(MM)
            dproj_ref[rows, :] = dproj
            dh1 = _dot_nt(dproj[:, 0:IN_CHUNK], w_ref[0])
            for j in range(1, N_CHIPS):
                dh1 = dh1 + _dot_nt(dproj[:, j * IN_CHUNK:(j + 1) * IN_CHUNK], w_ref[j])
            xv = x_ref[rows, :]
            dxn, dg1v = _rms_bwd(xv, _rms_r(xv), g1_ref[...], dh1)
            gx_ref[rows, :] = dx1_ref[rows, :] + dxn
            parts.append((jnp.sum(dqn * nq, axis=0, keepdims=True), jnp.sum(dkn * nk, axis=0, keepdims=True),
                          jnp.sum(dg1v, axis=0, keepdims=True)))
        dgq_ref[0] = functools.reduce(jnp.add, [p[0] for p in parts])
        dgk_ref[0] = functools.reduce(jnp.add, [p[1] for p in parts])
        dg1_ref[0] = functools.reduce(jnp.add, [p[2] for p in parts])

    tok = lambda w: pl.BlockSpec((tm, w), lambda i: (i, 0))
    tab = lambda w: pl.BlockSpec((tm, w), lambda i: (i % n_seq, 0))
    row = lambda w: pl.BlockSpec((1, w), lambda i: (0, 0))
    tposed = pl.BlockSpec((1, KV_WIDTH, tm), lambda i: (i // n_seq, 0, i % n_seq))
    blocks = pl.BlockSpec((N_KV, nblk, BLOCK, KV_WIDTH), lambda i: (0, i, 0, 0))
    part = lambda w: pl.BlockSpec((1, 1, w), lambda i: (i, 0, 0))
    return _call(
        body, (dqa, dkta, dvta, dqb, dktb, dvtb, raw, x, dx1, g1, w_in, gq, gk, cq, sq, ck, sk),
        name="proj_bwd", jobs=jobs,
        grid=(nt,),
        in_specs=[tok(Q_WIDTH), tposed, tposed, tok(Q_WIDTH), blocks, blocks, tok(QK_RAW), tok(D_MODEL),
                  tok(D_MODEL), row(D_MODEL),
                  pl.BlockSpec((N_CHIPS, D_MODEL, IN_CHUNK), lambda i: (0, 0, 0)),
                  row(Q_WIDTH), row(KV_WIDTH), tab(KV_WIDTH), tab(KV_WIDTH), tab(KV_WIDTH), tab(KV_WIDTH)],
        out_specs=[tok(D_MODEL), tok(IN_TOTAL), part(D_MODEL), part(Q_WIDTH), part(KV_WIDTH)],
        out_shape=[jax.ShapeDtypeStruct((tokens, D_MODEL), F32),
                   jax.ShapeDtypeStruct((tokens, IN_TOTAL), MM),
                   jax.ShapeDtypeStruct((nt, 1, D_MODEL), F32),
                   jax.ShapeDtypeStruct((nt, 1, Q_WIDTH), F32),
                   jax.ShapeDtypeStruct((nt, 1, KV_WIDTH), F32)],
        scratch_shapes=[pltpu.VMEM((tm, IN_TOTAL), F32)],
        params=_cparams(("arbitrary",), 56))


def _pack_small(dg1, dg2, dg3, dg4, dgq, dgk, dsink, dbias, bucket, loss):
    def body(dg1_ref, dg2_ref, dg3_ref, dg4_ref, dgq_ref, dgk_ref, dsink_ref, dbias_ref, bucket_ref, loss_ref,
             out_ref, rel_ref):
        out_ref[...] = jnp.zeros_like(out_ref)
        for r, ref in ((ROW_G1, dg1_ref), (ROW_G2, dg2_ref), (ROW_G3, dg3_ref), (ROW_G4, dg4_ref)):
            acc = ref[0]
            for t in range(1, ref.shape[0]):
                acc = acc + ref[t]
            out_ref[r:r + 1, :] = acc

        def fold(ref, heads):
            acc = ref[0]
            for t in range(1, ref.shape[0]):
                acc = acc + ref[t]
            tot = acc[:, 0:HEAD_DIM]
            for h in range(1, heads):
                tot = tot + acc[:, h * HEAD_DIM:(h + 1) * HEAD_DIM]
            return tot

        out_ref[ROW_MISC:ROW_MISC + 1, MISC_GQ:MISC_GQ + HEAD_DIM] = fold(dgq_ref, GROUP * N_KV)
        out_ref[ROW_MISC:ROW_MISC + 1, MISC_GK:MISC_GK + HEAD_DIM] = fold(dgk_ref, N_KV)
        for h in range(GROUP * N_KV):
            g = h % GROUP
            out_ref[ROW_MISC:ROW_MISC + 1, MISC_SINK + h:MISC_SINK + h + 1] = jnp.sum(
                dsink_ref[h // GROUP, 0:1, g * BLOCK:(g + 1) * BLOCK], axis=-1, keepdims=True)
        lacc = loss_ref[0, 0:1, 0:1]
        for t in range(1, loss_ref.shape[0]):
            lacc = lacc + loss_ref[t, 0:1, 0:1]
        out_ref[ROW_MISC:ROW_MISC + 1, MISC_LOSS:MISC_LOSS + 1] = lacc
        lane = lax.broadcasted_iota(jnp.int32, (N_BUCKETS, LANES), 1)
        row = lax.broadcasted_iota(jnp.int32, (N_BUCKETS, LANES), 0)

        def per_bucket(b, acc):
            for h in range(GROUP * N_KV):
                g = h % GROUP
                sel = jnp.zeros((BLOCK, BLOCK), F32)
                for piece in range(3):
                    sel = sel + jnp.where(bucket_ref[piece] == b,
                                          dbias_ref[h // GROUP, piece, :, g * BLOCK:(g + 1) * BLOCK], 0.0)
                tot = jnp.sum(jnp.sum(sel, axis=0, keepdims=True), axis=-1, keepdims=True)
                acc = jnp.where((row == b) & (lane == h), tot, acc)
            return acc

        rel_ref[...] = lax.fori_loop(0, N_BUCKETS, per_bucket, jnp.zeros((N_BUCKETS, LANES), F32))

    args = (dg1, dg2, dg3, dg4, dgq, dgk, dsink, dbias, bucket, loss)
    outs = [jax.ShapeDtypeStruct((8, D_MODEL), F32), jax.ShapeDtypeStruct((N_BUCKETS, LANES), F32)]
    return pl.pallas_call(
        body, name="pack_small", grid=(1,),
        in_specs=[_whole(a) for a in args], out_specs=[_whole(o) for o in outs], out_shape=outs,
        compiler_params=pltpu.CompilerParams(vmem_limit_bytes=32 * 1024 * 1024),
    )(*map(_from_hbm, args))


def _gather_weights(shards, whole):
    n = len(shards)
    full = [t for t in range(n) if whole[t]]

    def body(*refs):
        ins, outs = refs[:n], refs[n:2 * n]
        raw, stage = refs[2 * n:3 * n], refs[3 * n:4 * n]
        load_sem, local_sem, ici_send, ici_recv, d2d_send, d2d_recv = refs[4 * n:]
        x, y, c = _place()
        k = 2 * x + y
        sibling = (x, y, 1 - c)
        order = full + [t for t in range(n) if t not in full]
        loads = {t: pltpu.make_async_copy(ins[t], raw[t], load_sem.at[t]) for t in order}
        for t in order:
            loads[t].start()
        copies, sends = [], []
        for t in order:
            loads[t].wait()
            stage[t][...] = raw[t][...].astype(MM)
            mine = pltpu.make_async_copy(stage[t], outs[t].at[k], local_sem.at[t])
            mine.start()
            copies.append(mine)
            if t in full:
                half = ins[t].shape[0] // 2
                rows = pl.ds(c * half, half)
                for r, (fx, fy) in enumerate(_CHIP_FLIPS):
                    cp = _remote(stage[t].at[rows], outs[t].at[k, rows], ici_send.at[t, r], ici_recv.at[t, r],
                                 (_flip(x, fx), _flip(y, fy), c))
                    cp.start()
                    sends.append(cp)
        for t in full:
            half = ins[t].shape[0] // 2
            rows = pl.ds(c * half, half)
            for r, (fx, fy) in enumerate(_CHIP_FLIPS):
                kk = 2 * _flip(x, fx) + _flip(y, fy)
                landed = outs[t].at[kk, rows]
                _remote(landed, landed, ici_send.at[t, r], ici_recv.at[t, r], sibling).wait_recv()
                fwd = _remote(landed, landed, d2d_send.at[t, r], d2d_recv.at[t, r], sibling)
                fwd.start()
                sends.append(fwd)
        for t in full:
            half = ins[t].shape[0] // 2
            other = pl.ds((1 - c) * half, half)
            for r, (fx, fy) in enumerate(_CHIP_FLIPS):
                kk = 2 * _flip(x, fx) + _flip(y, fy)
                theirs = outs[t].at[kk, other]
                _remote(theirs, theirs, d2d_send.at[t, r], d2d_recv.at[t, r], sibling).wait_recv()
        for cp in sends:
            cp.wait_send()
        for cp in copies:
            cp.wait()

    return pl.pallas_call(
        body, name="gather_weights",
        in_specs=[HBM] * n, out_specs=[HBM] * n,
        out_shape=[pltpu.HBM((N_CHIPS,) + s.shape, MM) for s in shards],
        scratch_shapes=[pltpu.VMEM(s.shape, F32) for s in shards] + [pltpu.VMEM(s.shape, MM) for s in shards] + [
            pltpu.SemaphoreType.DMA((n,)), pltpu.SemaphoreType.DMA((n,)),
            pltpu.SemaphoreType.DMA((n, 3)), pltpu.SemaphoreType.DMA((n, 3)),
            pltpu.SemaphoreType.DMA((n, 3)), pltpu.SemaphoreType.DMA((n, 3))],
        compiler_params=pltpu.CompilerParams(vmem_limit_bytes=40 * 1024 * 1024),
    )(*shards)


def _add_half(grad, got, where, *, name, tr):
    nch, half, cols = got.shape
    nblk = half // tr

    def body(where_ref, g_ref, r_ref, o_ref):
        o_ref[...] = (g_ref[...] + r_ref[...]).astype(MM)

    return pl.pallas_call(
        body, name=name,
        grid_spec=pltpu.PrefetchScalarGridSpec(
            num_scalar_prefetch=1, grid=(nch, nblk),
            in_specs=[pl.BlockSpec((1, tr, cols), lambda j, i, where_ref: (j, where_ref[1] * nblk + i, 0)),
                      pl.BlockSpec((1, tr, cols), lambda j, i, where_ref: (j, i, 0))],
            out_specs=pl.BlockSpec((1, tr, cols), lambda j, i, where_ref: (j, i, 0))),
        out_shape=jax.ShapeDtypeStruct(got.shape, MM),
        compiler_params=_cparams(("parallel", "parallel"), 32),
    )(where, grad, got)


def _add_chips(own, got, where, *, name, tr):
    _, half, cols = own.shape
    nblk = half // tr

    def body(where_ref, o_ref, g_ref, out_ref):
        f = lambda v: v.astype(F32)
        out_ref[...] = ((f(o_ref[0]) + f(g_ref[0])) + f(g_ref[1])) + f(g_ref[2])

    return pl.pallas_call(
        body, name=name,
        grid_spec=pltpu.PrefetchScalarGridSpec(
            num_scalar_prefetch=1, grid=(nblk,),
            in_specs=[pl.BlockSpec((1, tr, cols), lambda i, where_ref: (where_ref[0], i, 0)),
                      pl.BlockSpec((3, tr, cols), lambda i, where_ref: (0, i, 0))],
            out_specs=pl.BlockSpec((tr, cols), lambda i, where_ref: (where_ref[1] * nblk + i, 0))),
        out_shape=pltpu.HBM((2 * half, cols), F32),
        compiler_params=_cparams(("parallel",), 32),
    )(where, own, got)


def _small_job(tiles):
    n = len(tiles)

    def copies(ins, outs, sems):
        x, y, c = _place()
        me = 4 * x + 2 * y + c
        local, send, recv = sems
        cps = []
        for t in range(n):
            cps.append(pltpu.make_async_copy(ins[t], outs[t].at[me], local.at[t]))
            for r in range(1, N_DEV):
                fx, fy, fc = (r >> 2) & 1, (r >> 1) & 1, r & 1
                cps.append(_remote(ins[t], outs[t].at[me], send.at[t, r - 1], recv.at[t, r - 1],
                                   (_flip(x, fx), _flip(y, fy), _flip(c, fc))))
        return cps

    return _Job(tiles, [jax.ShapeDtypeStruct((N_DEV,) + t.shape, F32) for t in tiles],
                [pltpu.SemaphoreType.DMA((n,)), pltpu.SemaphoreType.DMA((n, N_DEV - 1)),
                 pltpu.SemaphoreType.DMA((n, N_DEV - 1))], copies)


def _adamw_math(w, g, m, v):
    m = ADAM_B1 * m + (1.0 - ADAM_B1) * g
    v = ADAM_B2 * v + (1.0 - ADAM_B2) * (g * g)
    m_hat = m / (1.0 - ADAM_B1 ** ADAM_STEP)
    v_hat = v / (1.0 - ADAM_B2 ** ADAM_STEP)
    delta = -ADAM_LR * (m_hat / (jnp.sqrt(v_hat) + ADAM_EPS) + ADAM_WD * w)
    return delta, m, v


def _adamw(w, g, m, v, *, name, tr):
    rows, cols = w.shape

    def body(w_ref, g_ref, m_ref, v_ref, go_ref, d_ref, nm_ref, nv_ref):
        g = g_ref[...]
        go_ref[...] = g
        d_ref[...], nm_ref[...], nv_ref[...] = _adamw_math(w_ref[...], g, m_ref[...], v_ref[...])

    spec = pl.BlockSpec((tr, cols), lambda i: (i, 0))
    return pl.pallas_call(
        body, name=name,
        grid=(rows // tr,),
        in_specs=[spec] * 4, out_specs=[spec] * 4,
        out_shape=[jax.ShapeDtypeStruct(w.shape, F32)] * 4,
        compiler_params=_cparams(("parallel",), 32),
    )(w, g, m, v)


def _small_adamw(gathered, gathered_rel, params, moments_m, moments_v):
    n = len(params)

    def body(all_ref, rel_all_ref, *refs):
        w_refs, m_refs, v_refs = refs[:n], refs[n:2 * n], refs[2 * n:3 * n]
        loss_ref = refs[3 * n]
        out_refs = refs[3 * n + 1:]
        g = all_ref[0]
        rel = rel_all_ref[0]
        for d in range(1, N_DEV):
            g = g + all_ref[d]
            rel = rel + rel_all_ref[d]
        misc = g[ROW_MISC:ROW_MISC + 1]
        loss_ref[...] = misc[:, MISC_LOSS:MISC_LOSS + 1]
        grads = (g[ROW_G1:ROW_G1 + 1], g[ROW_G2:ROW_G2 + 1], g[ROW_G3:ROW_G3 + 1], g[ROW_G4:ROW_G4 + 1],
                 misc[:, MISC_GQ:MISC_GQ + HEAD_DIM], misc[:, MISC_GK:MISC_GK + HEAD_DIM],
                 misc[:, MISC_SINK:MISC_SINK + GROUP * N_KV], rel[:, 0:GROUP * N_KV])
        for i in range(n):
            d, nm, nv = _adamw_math(w_refs[i][...], grads[i], m_refs[i][...], v_refs[i][...])
            for j, val in enumerate((grads[i], d, nm, nv)):
                out_refs[4 * i + j][...] = val

    args = (gathered, gathered_rel, *params, *moments_m, *moments_v)
    out_shape = [jax.ShapeDtypeStruct((1, 1), F32)] + [jax.ShapeDtypeStruct(p.shape, F32) for p in params
                                                       for _ in range(4)]
    outs = pl.pallas_call(
        body, name="small_adamw", grid=(1,),
        in_specs=[_whole(a) for a in args], out_specs=[_whole(o) for o in out_shape], out_shape=out_shape,
    )(*map(_from_hbm, args))
    return outs[0], [outs[1 + 4 * i:5 + 4 * i] for i in range(n)]


def kernel(x, w_in, w_o, g_pre_mix, g_post_mix, q_norm_a, k_norm_a, sink_b, rel_bias, g_pre_ffn, w_ffn_up, w_ffn_down, g_post_ffn, loss_target, m_w_in, m_w_o, m_g_pre_mix, m_g_post_mix, m_q_norm_a, m_k_norm_a, m_sink_b, m_rel_bias, m_g_pre_ffn, m_w_ffn_up, m_w_ffn_down, m_g_post_ffn, v_w_in, v_w_o, v_g_pre_mix, v_g_post_mix, v_q_norm_a, v_k_norm_a, v_sink_b, v_rel_bias, v_g_pre_ffn, v_w_ffn_up, v_w_ffn_down, v_g_post_ffn):
    batch, seq, _ = x.shape
    tokens = batch * seq
    where = jnp.stack([2 * lax.axis_index("x") + lax.axis_index("y"), lax.axis_index("c")]).astype(jnp.int32)
    x2 = x.reshape(tokens, D_MODEL)
    g1, g2, g3, g4 = g_pre_mix, g_post_mix, g_pre_ffn, g_post_ffn

    cos, sin = _rope_tables(seq)
    ck, sk = jnp.tile(cos, (1, 2)), jnp.tile(sin, (1, 2))
    cq, sq = ck * SCALE, sk * SCALE
    gq8, gk2 = jnp.tile(q_norm_a, (1, 8)), jnp.tile(k_norm_a, (1, 2))
    bucket, band = _window_tables()
    bias = _bias_build(rel_bias.T, bucket, band)

    w_in_g, w_o_p, w_up_p, w_down_p = _gather_weights(
        (w_in[0], w_o[0], w_ffn_up[0], w_ffn_down[0]), whole=(True, False, False, False))
    (h1, raw, qa, ka, kta, va, vta, qtb, kb, ktb, vb, vtb) = _pre_proj(
        x2, g1, w_in_g, gq8, gk2, cq, sq, ck, sk, seq=seq, tm=min(512, seq), sub=256)
    (oa, p_a, linv_a), (w_part,) = _attn_a_fwd(
        qa, kta, va, seq=seq, bq=min(256, seq), jobs=[_gather_job([w_o_p, w_up_p, w_down_p], forward=False)])
    kb3 = kb.reshape(tokens // BLOCK, BLOCK, KV_WIDTH)
    vb3 = vb.reshape(tokens // BLOCK, BLOCK, KV_WIDTH)
    (ob, lse_b), ((w_o_g, w_up_g, w_down_g),) = _attn_b_fwd(
        qtb, kb3, vtb, bias, sink_b, seq=seq, jobs=[_gather_job(w_part, forward=True)])
    w_o2 = w_o_g.reshape(D_MODEL, D_MODEL)
    w_down2 = w_down_g.reshape(D_FF, D_MODEL)
    mix, x1, h2, o_cat, u, df, dy, loss_t, dg4 = _mix_ffn_fwd(
        oa, ob, w_o2, x2, g2, g3, w_up_g, w_down2, loss_target.reshape(tokens, D_MODEL), g4, tm=256)

    dz, dx1, dmix, dg3, dg2, doa, dob = _ffn_bwd_act(df, w_down2, u, w_up_g, x1, dy, mix, g3, g2, w_o2, tm=256)
    gw_down, _ = _tn_matmul(u, df, name="grad_w_down", tm=1024, tn=1024, tk=min(2048, tokens), square_a=True)
    gw_down = gw_down.reshape(N_CHIPS, FF_CHUNK, D_MODEL)
    gw_up, ((got_down,),) = _tn_matmul(h2, dz, name="grad_w_up", tm=1024, tn=1024, tk=min(2048, tokens), chunked=True,
                                        jobs=[_swap_job([gw_down])])
    gw_o, _ = _tn_matmul(o_cat, dmix, name="grad_w_o", tm=1024, tn=1024, tk=min(2048, tokens))
    gw_o = gw_o.reshape(N_CHIPS, O_CHUNK, D_MODEL)
    sum_down = _add_half(gw_down, got_down, where, name="add_half_w_down", tr=128)
    (dqa, dkta, dvta), ((ex_down,), (got_up,)) = _attn_a_bwd(
        qa, ka, vta, doa, oa, p_a, linv_a, seq=seq, bq=min(256, seq),
        jobs=[_exchange_job([sum_down]), _swap_job([gw_up])])
    full_down = _add_chips(sum_down, ex_down, where, name="add_chips_w_down", tr=128)
    sum_up = _add_half(gw_up, got_up, where, name="add_half_w_up", tr=128)
    (dqb, dkb, dvb, dbias, dsink), ((ex_up,), (g_down,), (got_o,)) = _attn_b_bwd(
        qtb, kb3, ktb, vb3, dob, ob, lse_b, bias, sink_b, seq=seq,
        jobs=[_exchange_job([sum_up]), _join_job([full_down]), _swap_job([gw_o])])
    full_up = _add_chips(sum_up, ex_up, where, name="add_chips_w_up", tr=128)
    sum_o = _add_half(gw_o, got_o, where, name="add_half_w_o", tr=128)
    (grad_x, dproj, dg1, dgq, dgk), _ = _proj_bwd(
        dqa, dkta, dvta, dqb, dkb, dvb, raw, x2, dx1, g1, w_in_g, gq8, gk2, cq, sq, ck, sk,
        seq=seq, tm=min(512, seq), sub=128)
    packed, packed_rel = _pack_small(dg1, dg2, dg3, dg4, dgq, dgk, dsink, dbias, bucket, loss_t)
    gw_in, ((ex_o,), (g_up,), (gathered, gathered_rel)) = _tn_matmul(
        h1, dproj, name="grad_w_in", tm=1024, tn=IN_CHUNK, tk=min(2048, tokens), chunked=True,
        jobs=[_exchange_job([sum_o]), _join_job([full_up]), _small_job([packed, packed_rel])])
    full_o = _add_chips(sum_o, ex_o, where, name="add_chips_w_o", tr=128)

    (g_o,), (got_in,) = _run_jobs("tail_swap", [_join_job([full_o]), _swap_job([gw_in])])
    sum_in = _add_half(gw_in, got_in, where, name="add_half_w_in", tr=128)
    ((ex_in,),) = _run_jobs("tail_exchange", [_exchange_job([sum_in])])
    full_in = _add_chips(sum_in, ex_in, where, name="add_chips_w_in", tr=128)
    ((g_in,),) = _run_jobs("tail_join", [_join_job([full_in])])

    big = [[t[None] for t in _adamw(w[0], g, m[0], v[0], name="adamw_" + nm, tr=128)] for nm, w, g, m, v in (
        ("w_in", w_in, g_in, m_w_in, v_w_in), ("w_o", w_o, g_o, m_w_o, v_w_o),
        ("w_up", w_ffn_up, g_up, m_w_ffn_up, v_w_ffn_up), ("w_down", w_ffn_down, g_down, m_w_ffn_down, v_w_ffn_down))]

    loss, small = _small_adamw(
        gathered, gathered_rel,
        (g1, g2, g3, g4, q_norm_a, k_norm_a, sink_b, rel_bias),
        (m_g_pre_mix, m_g_post_mix, m_g_pre_ffn, m_g_post_ffn, m_q_norm_a, m_k_norm_a, m_sink_b, m_rel_bias),
        (v_g_pre_mix, v_g_post_mix, v_g_pre_ffn, v_g_post_ffn, v_q_norm_a, v_k_norm_a, v_sink_b, v_rel_bias))
    s_g1, s_g2, s_g3, s_g4, s_gq, s_gk, s_sink, s_rel = small

    def leaves(i):
        return (big[0][i], big[1][i], s_g1[i], s_g2[i], s_gq[i], s_gk[i], s_sink[i], s_rel[i], s_g3[i],
                big[2][i], big[3][i], s_g4[i])

    loss = loss.reshape(())
    return (loss, grad_x.reshape(batch, seq, D_MODEL), *leaves(0), *leaves(1), *leaves(2), *leaves(3))
```

```python
import functools

import jax
import jax.numpy as jnp
import numpy as np
from jax import lax
from jax.experimental import pallas as pl
from jax.experimental.pallas import tpu as pltpu

F32 = jnp.float32
MM = jnp.bfloat16

D_MODEL = 1024
HEAD_DIM = 64
N_KV = 2
GROUP = 4
Q_WIDTH = 512
KV_WIDTH = 128
D_FF = 4096
GRID_W = 64
BLOCK = 128
N_BUCKETS = 32
MAX_DISTANCE = 128
ROPE_THETA = 10000.0
EPS = 1e-6
NEG_INF = -1e30
SCALE = HEAD_DIM ** -0.5
IN_TOTAL = 1536
N_CHIPS = 4
N_DEV = 8
IN_CHUNK = IN_TOTAL // N_CHIPS
FF_CHUNK = D_FF // N_CHIPS
O_CHUNK = D_MODEL // N_CHIPS
QK_RAW = 640

ADAM_LR = 0.001
ADAM_B1 = 0.9
ADAM_B2 = 0.999
ADAM_EPS = 1e-08
ADAM_WD = 0.01
ADAM_STEP = 10

LANES = 128
MESH = pl.DeviceIdType.MESH
HBM = pl.BlockSpec(memory_space=pl.ANY)
VMEM = pl.BlockSpec(memory_space=pltpu.VMEM)
SMEM = pl.BlockSpec(memory_space=pltpu.SMEM)

ROW_G1, ROW_G2, ROW_G3, ROW_G4, ROW_MISC = 0, 1, 2, 3, 4
MISC_GQ, MISC_GK, MISC_SINK, MISC_LOSS = 0, 64, 128, 512


def _cparams(sem, vmem_mb):
    return pltpu.CompilerParams(dimension_semantics=sem, vmem_limit_bytes=vmem_mb * 1024 * 1024)


def _whole(a):
    return pl.BlockSpec(a.shape, lambda i: (0,) * len(a.shape))


def _from_hbm(a):
    return pltpu.with_memory_space_constraint(a, pltpu.HBM)


def _in_hbm(s):
    return pltpu.HBM(s.shape, s.dtype)


class _Job:
    def __init__(self, operands, out_shapes, sems, copies, alias=None):
        self.operands, self.out_shapes, self.sems, self.copies = list(operands), list(out_shapes), list(sems), copies
        self.alias = dict(alias or {})


def _place():
    return lax.axis_index("x"), lax.axis_index("y"), lax.axis_index("c")


_CHIP_FLIPS = ((1, 0), (0, 1), (1, 1))


def _flip(v, bit):
    return 1 - v if bit else v


def _remote(src, dst, send, recv, dev):
    return pltpu.make_async_remote_copy(src_ref=src, dst_ref=dst, send_sem=send, recv_sem=recv,
                                        device_id=dev, device_id_type=MESH)


def _swap_job(grads):
    n = len(grads)

    def copies(ins, outs, sems):
        x, y, c = _place()
        send, recv = sems
        cps = []
        for t in range(n):
            half = ins[t].shape[1] // 2
            cps.append(_remote(ins[t].at[:, pl.ds((1 - c) * half, half), :], outs[t], send.at[t], recv.at[t],
                               (x, y, 1 - c)))
        return cps

    shapes = [jax.ShapeDtypeStruct((g.shape[0], g.shape[1] // 2, g.shape[2]), F32) for g in grads]
    return _Job(grads, shapes, [pltpu.SemaphoreType.DMA((n,)), pltpu.SemaphoreType.DMA((n,))], copies)


def _exchange_job(sums):
    n = len(sums)

    def copies(ins, outs, sems):
        x, y, c = _place()
        send, recv = sems
        cps = []
        for t in range(n):
            for r, (fx, fy) in enumerate(_CHIP_FLIPS):
                kk = 2 * _flip(x, fx) + _flip(y, fy)
                cps.append(_remote(ins[t].at[kk], outs[t].at[r], send.at[t, r], recv.at[t, r],
                                   (_flip(x, fx), _flip(y, fy), c)))
        return cps

    shapes = [jax.ShapeDtypeStruct((3,) + s.shape[1:], s.dtype) for s in sums]
    return _Job(sums, shapes, [pltpu.SemaphoreType.DMA((n, 3)), pltpu.SemaphoreType.DMA((n, 3))], copies)


def _join_job(fulls):
    n = len(fulls)

    def copies(ins, outs, sems):
        x, y, c = _place()
        send, recv = sems
        cps = []
        for t in range(n):
            half = ins[t].shape[0] // 2
            rows = pl.ds(c * half, half)
            cps.append(_remote(ins[t].at[rows], outs[t].at[rows], send.at[t], recv.at[t], (x, y, 1 - c)))
        return cps

    shapes = [jax.ShapeDtypeStruct(f.shape, f.dtype) for f in fulls]
    return _Job(fulls, shapes, [pltpu.SemaphoreType.DMA((n,)), pltpu.SemaphoreType.DMA((n,))], copies,
                alias={t: t for t in range(n)})


def _gather_job(bufs, forward):
    n = len(bufs)

    def copies(ins, outs, sems):
        x, y, c = _place()
        send, recv = sems
        cps = []
        for t in range(n):
            half = ins[t].shape[1] // 2
            rows = pl.ds(c * half, half)
            for r, (fx, fy) in enumerate(_CHIP_FLIPS):
                if forward:
                    kk = 2 * _flip(x, fx) + _flip(y, fy)
                    dev = (x, y, 1 - c)
                else:
                    kk = 2 * x + y
                    dev = (_flip(x, fx), _flip(y, fy), c)
                cps.append(_remote(ins[t].at[kk, rows], outs[t].at[kk, rows], send.at[t, r], recv.at[t, r], dev))
        return cps

    shapes = [jax.ShapeDtypeStruct(b.shape, b.dtype) for b in bufs]
    return _Job(bufs, shapes, [pltpu.SemaphoreType.DMA((n, 3)), pltpu.SemaphoreType.DMA((n, 3))], copies,
                alias={t: t for t in range(n)})


def _call(body, args, *, name, grid, in_specs, out_specs, out_shape, scratch_shapes=(), params=None, jobs=()):
    n_in, n_out, n_scr = len(in_specs), len(out_specs), len(scratch_shapes)
    job_in = [len(j.operands) for j in jobs]
    job_out = [len(j.out_shapes) for j in jobs]
    job_sem = [len(j.sems) for j in jobs]

    def wrapped(*refs):
        pos = 0
        ins = refs[pos:pos + n_in]; pos += n_in
        jins = []
        for k in job_in:
            jins.append(refs[pos:pos + k]); pos += k
        outs = refs[pos:pos + n_out]; pos += n_out
        jouts = []
        for k in job_out:
            jouts.append(refs[pos:pos + k]); pos += k
        scr = refs[pos:pos + n_scr]; pos += n_scr
        jsems = []
        for k in job_sem:
            jsems.append(refs[pos:pos + k]); pos += k
        if jobs:
            ids = [pl.program_id(d) for d in range(len(grid))]
            first = functools.reduce(jnp.logical_and, [i == 0 for i in ids])
            last = functools.reduce(jnp.logical_and, [i == g - 1 for i, g in zip(ids, grid)])

            @pl.when(first)
            def _():
                for j, ji, jo, js in zip(jobs, jins, jouts, jsems):
                    for cp in j.copies(ji, jo, js):
                        cp.start()

        body(*ins, *outs, *scr)
        if jobs:
            @pl.when(last)
            def _():
                for j, ji, jo, js in zip(jobs, jins, jouts, jsems):
                    for cp in j.copies(ji, jo, js):
                        cp.wait()

    aliases = {}
    in_pos, out_pos = n_in, n_out
    for j in jobs:
        for i, o in j.alias.items():
            aliases[in_pos + i] = out_pos + o
        in_pos += len(j.operands)
        out_pos += len(j.out_shapes)
    res = pl.pallas_call(
        wrapped, name=name, grid=grid,
        in_specs=list(in_specs) + [HBM] * sum(job_in),
        out_specs=list(out_specs) + [HBM] * sum(job_out),
        out_shape=list(out_shape) + [_in_hbm(s) for j in jobs for s in j.out_shapes],
        scratch_shapes=list(scratch_shapes) + [s for j in jobs for s in j.sems],
        input_output_aliases=aliases,
        compiler_params=params,
    )(*[a if spec is SMEM else _from_hbm(a) for a, spec in zip(args, in_specs)],
      *[a for j in jobs for a in j.operands])
    own, rest = list(res[:n_out]), list(res[n_out:])
    job_res = []
    for k in job_out:
        job_res.append(rest[:k])
        rest = rest[k:]
    return own, job_res


def _run_jobs(name, jobs):
    def body():
        pass

    return _call(body, (), name=name, grid=(1,), in_specs=[], out_specs=[], out_shape=[], jobs=jobs)[1]


def _dot(a, b):
    return jnp.dot(a, b, preferred_element_type=F32)


def _dot_nt(a, b):
    return lax.dot_general(a, b, (((1,), (1,)), ((), ())), preferred_element_type=F32)


def _dot_tn(a, b):
    return lax.dot_general(a, b, (((0,), (0,)), ((), ())), preferred_element_type=F32)


def _rms_r(x):
    return lax.rsqrt(jnp.mean(x * x, axis=-1, keepdims=True) + EPS)


def _rms_bwd(x, r, g, dy):
    n = x * r
    dn = dy * g
    dx = r * (dn - n * jnp.mean(dn * n, axis=-1, keepdims=True))
    return dx, dy * n


def _seg64_sum(v):
    rows, width = v.shape
    lane = lax.broadcasted_iota(jnp.int32, (rows, LANES), 1)
    lo = lane < HEAD_DIM
    outs = []
    for c in range(width // LANES):
        ch = v[:, c * LANES:(c + 1) * LANES]
        s_lo = jnp.sum(jnp.where(lo, ch, 0.0), axis=-1, keepdims=True)
        s_hi = jnp.sum(jnp.where(lo, 0.0, ch), axis=-1, keepdims=True)
        outs.append(jnp.where(lo, s_lo, s_hi))
    return outs[0] if len(outs) == 1 else jnp.concatenate(outs, axis=-1)


def _head_r(v):
    return lax.rsqrt(_seg64_sum(v * v) * (1.0 / HEAD_DIM) + EPS)


def _swap16(ch):
    lane = lax.broadcasted_iota(jnp.int32, ch.shape, 1)
    return jnp.where((lane % 32) < 16, pltpu.roll(ch, LANES - 16, 1), pltpu.roll(ch, 16, 1))


def _by_chunk(fn, v):
    outs = [fn(v[:, c * LANES:(c + 1) * LANES]) for c in range(v.shape[1] // LANES)]
    return outs[0] if len(outs) == 1 else jnp.concatenate(outs, axis=-1)


def _rope(v, cos, sin_signed):
    return _by_chunk(lambda ch: ch * cos + _swap16(ch) * sin_signed, v)


def _rope_t(g, cos, sin_signed):
    return _by_chunk(lambda ch: ch * cos + _swap16(ch * sin_signed), g)


def _rope_tables(seq):
    nf = HEAD_DIM // 4
    freqs = ROPE_THETA ** (-jnp.arange(nf, dtype=F32) / nf)
    pos = jnp.arange(seq, dtype=jnp.int32)
    row = (pos // GRID_W).astype(F32)
    col = (pos % GRID_W).astype(F32)
    ang_r = row[:, None] * freqs[None, :]
    ang_c = col[:, None] * freqs[None, :]
    cr, sr, cc, sc = jnp.cos(ang_r), jnp.sin(ang_r), jnp.cos(ang_c), jnp.sin(ang_c)
    cos = jnp.concatenate([cr, cr, cc, cc], axis=1)
    sin = jnp.concatenate([-sr, sr, -sc, sc], axis=1)
    return cos, sin


def _t5_bucket(rel):
    nb = N_BUCKETS // 2
    ret = (rel > 0).astype(jnp.int32) * nb
    n = jnp.abs(rel)
    max_exact = nb // 2
    nf = jnp.maximum(n, 1).astype(jnp.float32)
    large = max_exact + (jnp.log(nf / max_exact) / np.float32(np.log(MAX_DISTANCE / max_exact))
                         * (nb - max_exact)).astype(jnp.int32)
    large = jnp.minimum(large, nb - 1)
    return ret + jnp.where(n < max_exact, n, large)


def _window_tables():
    a = jnp.arange(BLOCK, dtype=jnp.int32)
    c = jnp.arange(3 * BLOCK, dtype=jnp.int32)
    rel = c[None, :] - BLOCK - a[:, None]
    bucket = _t5_bucket(rel)
    band = (jnp.abs(rel) <= BLOCK).astype(jnp.int32)
    to3 = lambda t: t.reshape(BLOCK, 3, BLOCK).transpose(1, 2, 0)
    return to3(bucket), to3(band)


def _pre_proj(x, g1, w_in, gq, gk, ck, sk, *, seq, tm, sub):
    tokens = x.shape[0]
    n_seq = seq // tm
    nblk = tm // BLOCK
    batch = tokens // seq

    def body(x_ref, g1_ref, w_ref, gq_ref, gk_ref, ck_ref, sk_ref,
             h1_ref, raw_ref, qa_ref, ka_ref, kta_ref, va_ref, vta_ref,
             qtb_ref, kb_ref, ktb_ref, vb_ref, vtb_ref, proj):
        for r in range(tm // sub):
            rows = slice(r * sub, (r + 1) * sub)
            xv = x_ref[rows, :]
            h = (xv * _rms_r(xv) * g1_ref[...]).astype(MM)
            h1_ref[rows, :] = h
            for j in range(N_CHIPS):
                proj[rows, j * IN_CHUNK:(j + 1) * IN_CHUNK] = _dot(h, w_ref[j])
            qa = proj[rows, 0:Q_WIDTH]
            ka = proj[rows, Q_WIDTH:QK_RAW]
            raw_ref[rows, :] = proj[rows, 0:QK_RAW]
            qn = qa * _head_r(qa) * gq_ref[...]
            qa_ref[rows, :] = (_rope(qn, ck_ref[rows, :], sk_ref[rows, :]) * SCALE).astype(MM)
            kn = ka * _head_r(ka) * gk_ref[...]
            kr = _rope(kn, ck_ref[rows, :], sk_ref[rows, :])
            ka_ref[rows, :] = kr.astype(MM)
            kta_ref[0, :, rows] = kr.T.astype(MM)
            va = proj[rows, 640:768]
            va_ref[rows, :] = va.astype(MM)
            vta_ref[0, :, rows] = va.T.astype(MM)
            qb = proj[rows, 768:1280] * SCALE
            kb = proj[rows, 1280:1408]
            vb = proj[rows, 1408:1536]
            kb_ref[rows, :] = kb.astype(MM)
            vb_ref[rows, :] = vb.astype(MM)
            for j in range(sub // BLOCK):
                blk = slice(j * BLOCK, (j + 1) * BLOCK)
                qtb_ref[r * (sub // BLOCK) + j] = qb[blk, :].T.astype(MM)
                ktb_ref[r * (sub // BLOCK) + j] = kb[blk, :].T.astype(MM)
                vtb_ref[r * (sub // BLOCK) + j] = vb[blk, :].T.astype(MM)

    tok = lambda w: pl.BlockSpec((tm, w), lambda i: (i, 0))
    tab = lambda w: pl.BlockSpec((tm, w), lambda i: (i % n_seq, 0))
    row = lambda w: pl.BlockSpec((1, w), lambda i: (0, 0))
    tposed = pl.BlockSpec((1, LANES, tm), lambda i: (i // n_seq, 0, i % n_seq))
    blocks = pl.BlockSpec((nblk, BLOCK, LANES), lambda i: (i, 0, 0))
    qblocks = pl.BlockSpec((nblk, Q_WIDTH, BLOCK), lambda i: (i, 0, 0))
    tok_mm = lambda w: jax.ShapeDtypeStruct((tokens, w), MM)
    return pl.pallas_call(
        body, name="pre_proj",
        grid=(tokens // tm,),
        in_specs=[tok(D_MODEL), row(D_MODEL),
                  pl.BlockSpec((N_CHIPS, D_MODEL, IN_CHUNK), lambda i: (0, 0, 0)),
                  row(Q_WIDTH), row(KV_WIDTH), tab(KV_WIDTH), tab(KV_WIDTH)],
        out_specs=[tok(D_MODEL), tok(QK_RAW), tok(Q_WIDTH), tok(KV_WIDTH), tposed, tok(KV_WIDTH), tposed,
                   qblocks, tok(KV_WIDTH), blocks, tok(KV_WIDTH), blocks],
        out_shape=[
            tok_mm(D_MODEL),
            jax.ShapeDtypeStruct((tokens, QK_RAW), F32),
            tok_mm(Q_WIDTH),
            tok_mm(KV_WIDTH),
            jax.ShapeDtypeStruct((batch, KV_WIDTH, seq), MM),
            tok_mm(KV_WIDTH),
            jax.ShapeDtypeStruct((batch, KV_WIDTH, seq), MM),
            jax.ShapeDtypeStruct((tokens // BLOCK, Q_WIDTH, BLOCK), MM),
            tok_mm(KV_WIDTH),
            jax.ShapeDtypeStruct((tokens // BLOCK, KV_WIDTH, BLOCK), MM),
            tok_mm(KV_WIDTH),
            jax.ShapeDtypeStruct((tokens // BLOCK, KV_WIDTH, BLOCK), MM),
        ],
        scratch_shapes=[pltpu.VMEM((tm, IN_TOTAL), F32)],
        compiler_params=_cparams(("parallel",), 48),
    )(*map(_from_hbm, (x, g1, w_in, gq, gk, ck, sk)))


def _kv_half(v2, kv):
    return jnp.where(kv == 0, v2[:, :HEAD_DIM], v2[:, HEAD_DIM:])


def _attn_a_fwd(qa, kta, va, *, seq, bq, jobs=()):
    tokens = qa.shape[0]
    batch = tokens // seq
    nq = seq // bq

    def body(q_ref, kt_ref, v_ref, o_ref, p_ref, linv_ref):
        kv = pl.program_id(1)
        kt = kt_ref[0]
        lane = lax.broadcasted_iota(jnp.int32, (seq, KV_WIDTH), 1)
        v = jnp.where((lane < HEAD_DIM) == (kv == 0), v_ref[...], jnp.ones((), MM))
        for g in range(GROUP):
            sl = slice(g * HEAD_DIM, (g + 1) * HEAD_DIM)
            s = _dot(q_ref[:, sl], kt)
            pb = jnp.exp((s - jnp.max(s, axis=-1, keepdims=True)).astype(MM))
            p_ref[0, g] = pb
            o2 = _dot(pb, v)
            linv = 1.0 / _kv_half(o2, 1 - kv)[:, 0:1]
            o_ref[:, sl] = _kv_half(o2, kv) * linv
            linv_ref[0, :, g:g + 1] = linv

    return _call(
        body, (qa, kta, va), name="attn_a_fwd", jobs=jobs,
        grid=(batch, N_KV, nq),
        in_specs=[pl.BlockSpec((bq, GROUP * HEAD_DIM), lambda b, k, i: (b * nq + i, k)),
                  pl.BlockSpec((1, HEAD_DIM, seq), lambda b, k, i: (b, k, 0)),
                  pl.BlockSpec((seq, KV_WIDTH), lambda b, k, i: (b, 0))],
        out_specs=[pl.BlockSpec((bq, GROUP * HEAD_DIM), lambda b, k, i: (b * nq + i, k)),
                   pl.BlockSpec((1, GROUP, bq, seq), lambda b, k, i: (k, 0, b * nq + i, 0)),
                   pl.BlockSpec((1, bq, GROUP), lambda b, k, i: (k, b * nq + i, 0))],
        out_shape=[jax.ShapeDtypeStruct((tokens, Q_WIDTH), F32),
                   jax.ShapeDtypeStruct((N_KV, GROUP, tokens, seq), MM),
                   jax.ShapeDtypeStruct((N_KV, tokens, GROUP), F32)],
        params=_cparams(("arbitrary", "arbitrary", "arbitrary"), 56))


def _attn_a_bwd(qa, ka, vta, do, o, p, linv, *, seq, bq, jobs=()):
    tokens = qa.shape[0]
    batch = tokens // seq
    nq = seq // bq

    def body(q_ref, k_ref, vt_ref, do_ref, o_ref, p_ref, linv_ref, dq_ref, dkt_ref, dvt_ref):
        kv = pl.program_id(1)

        @pl.when(pl.program_id(2) == 0)
        def _():
            dkt_ref[...] = jnp.zeros_like(dkt_ref)
            dvt_ref[...] = jnp.zeros_like(dvt_ref)

        vt = vt_ref[0]
        k2 = k_ref[...]
        for g in range(GROUP):
            sl = slice(g * HEAD_DIM, (g + 1) * HEAD_DIM)
            dof = do_ref[:, sl]
            delta = jnp.sum(dof * o_ref[:, sl], axis=-1, keepdims=True)
            linv_g = linv_ref[0, :, g:g + 1]
            pb = p_ref[0, g]
            dp = _dot(dof.astype(MM), vt)
            ds = pb * ((dp - delta) * linv_g).astype(MM)
            dq_ref[:, sl] = _kv_half(_dot(ds, k2), kv)
            dkt_ref[0] += _dot_tn(q_ref[:, sl], ds)
            dvt_ref[0] += _dot_tn((dof * linv_g).astype(MM), pb)

    qspec = pl.BlockSpec((bq, GROUP * HEAD_DIM), lambda b, k, i: (b * nq + i, k))
    tspec = pl.BlockSpec((1, HEAD_DIM, seq), lambda b, k, i: (b, k, 0))
    return _call(
        body, (qa, ka, vta, do, o, p, linv), name="attn_a_bwd", jobs=jobs,
        grid=(batch, N_KV, nq),
        in_specs=[qspec, pl.BlockSpec((seq, KV_WIDTH), lambda b, k, i: (b, 0)), tspec, qspec, qspec,
                  pl.BlockSpec((1, GROUP, bq, seq), lambda b, k, i: (k, 0, b * nq + i, 0)),
                  pl.BlockSpec((1, bq, GROUP), lambda b, k, i: (k, b * nq + i, 0))],
        out_specs=[qspec, tspec, tspec],
        out_shape=[jax.ShapeDtypeStruct((tokens, Q_WIDTH), F32),
                   jax.ShapeDtypeStruct((batch, KV_WIDTH, seq), F32),
                   jax.ShapeDtypeStruct((batch, KV_WIDTH, seq), F32)],
        params=_cparams(("arbitrary", "arbitrary", "arbitrary"), 56))


def _bias_build(rel_bias_t, bucket_t, band_t):
    def body(tab_ref, bucket_ref, band_ref, bias_ref):
        for h in range(GROUP * N_KV):
            for piece in range(3):
                bk = bucket_ref[piece]
                acc = jnp.zeros((BLOCK, BLOCK), F32)
                for b in range(N_BUCKETS):
                    acc = jnp.where(bk == b, tab_ref[h, b], acc)
                g = h % GROUP
                bias_ref[h // GROUP, piece, :, g * BLOCK:(g + 1) * BLOCK] = jnp.where(band_ref[piece] != 0, acc, NEG_INF)

    out = jax.ShapeDtypeStruct((N_KV, 3, BLOCK, GROUP * BLOCK), F32)
    return pl.pallas_call(
        body, name="bias_build", grid=(1,),
        in_specs=[SMEM, _whole(bucket_t), _whole(band_t)], out_specs=_whole(out), out_shape=out,
    )(rel_bias_t, bucket_t, band_t)


def _pad_heads(t, kv):
    outs = []
    for g in range(GROUP):
        tg = t[g * HEAD_DIM:(g + 1) * HEAD_DIM, :]
        zero = jnp.zeros_like(tg)
        outs.append(jnp.concatenate([jnp.where(kv == 0, tg, zero), jnp.where(kv == 0, zero, tg)], axis=0))
    return jnp.concatenate(outs, axis=-1)


def _unpad_heads(t, kv):
    outs = [_kv_half(t[:, g * BLOCK:(g + 1) * BLOCK].T, kv) for g in range(GROUP)]
    return jnp.concatenate(outs, axis=-1)


def _sink_row(sink_ref, kv):
    lane_head = lax.broadcasted_iota(jnp.int32, (1, GROUP * BLOCK), 1) // BLOCK
    row = jnp.zeros((1, GROUP * BLOCK), F32)
    for g in range(GROUP):
        row = jnp.where(lane_head == g, sink_ref[0, kv * GROUP + g], row)
    return row


def _window_scores_t(k_ref, idx, qpad, bias_ref, n, nblk):
    pieces = []
    for piece in range(3):
        s = _dot(k_ref[idx[piece]], qpad) + bias_ref[0, piece]
        if piece == 0:
            s = jnp.where(n > 0, s, NEG_INF)
        if piece == 2:
            s = jnp.where(n < nblk - 1, s, NEG_INF)
        pieces.append(s)
    return pieces


def _attn_b_fwd(qtb, kb3, vtb, bias, sink, *, seq, jobs=()):
    nblk_all = qtb.shape[0]
    tokens = nblk_all * BLOCK
    batch = tokens // seq
    nblk = seq // BLOCK

    def body(sink_ref, q_ref, k_ref, vt_ref, bias_ref, o_ref, lse_ref):
        kv = pl.program_id(0)
        sink_row = _sink_row(sink_ref, kv)

        def block(n, carry):
            idx = (jnp.maximum(n - 1, 0), n, jnp.minimum(n + 1, nblk - 1))
            rows = pl.ds(pl.multiple_of(n * BLOCK, BLOCK), BLOCK)
            qpad = _pad_heads(q_ref[n], kv)
            ss = _window_scores_t(k_ref, idx, qpad, bias_ref, n, nblk)
            m = jnp.maximum(jnp.maximum(jnp.max(ss[0], axis=0, keepdims=True),
                                        jnp.max(ss[1], axis=0, keepdims=True)),
                            jnp.maximum(jnp.max(ss[2], axis=0, keepdims=True), sink_row))
            ps = [jnp.exp(s - m) for s in ss]
            denom = (jnp.sum(ps[0], axis=0, keepdims=True) + jnp.sum(ps[1], axis=0, keepdims=True)
                     + jnp.sum(ps[2], axis=0, keepdims=True) + jnp.exp(sink_row - m))
            ot = (_dot(vt_ref[idx[0]], ps[0].astype(MM)) + _dot(vt_ref[idx[1]], ps[1].astype(MM))
                  + _dot(vt_ref[idx[2]], ps[2].astype(MM)))
            o_ref[rows, :] = _unpad_heads(ot * (1.0 / denom), kv)
            lse_ref[0, n] = jnp.broadcast_to(m + jnp.log(denom), (8, GROUP * BLOCK))
            return carry

        lax.fori_loop(0, nblk, block, 0, unroll=8)

    both = pl.BlockSpec((nblk, BLOCK, KV_WIDTH), lambda k, b: (b, 0, 0))
    return _call(
        body, (sink, qtb, kb3, vtb, bias), name="attn_b_fwd", jobs=jobs,
        grid=(N_KV, batch),
        in_specs=[SMEM, pl.BlockSpec((nblk, GROUP * HEAD_DIM, BLOCK), lambda k, b: (b, k, 0)), both, both,
                  pl.BlockSpec((1, 3, BLOCK, GROUP * BLOCK), lambda k, b: (k, 0, 0, 0))],
        out_specs=[pl.BlockSpec((seq, GROUP * HEAD_DIM), lambda k, b: (b, k)),
                   pl.BlockSpec((1, nblk, 8, GROUP * BLOCK), lambda k, b: (k, b, 0, 0))],
        out_shape=[jax.ShapeDtypeStruct((tokens, Q_WIDTH), F32),
                   jax.ShapeDtypeStruct((N_KV, nblk_all, 8, GROUP * BLOCK), F32)],
        params=_cparams(("arbitrary", "arbitrary"), 48))


def _attn_b_bwd(qtb, kb3, ktb, vb3, do, o, lse, bias, sink, *, seq, jobs=()):
    nblk_all = qtb.shape[0]
    tokens = nblk_all * BLOCK
    batch = tokens // seq
    nblk = seq // BLOCK

    def body(sink_ref, q_ref, k_ref, kt_ref, v_ref, do_ref, o_ref, lse_ref, bias_ref,
             dq_ref, dk_ref, dv_ref, dbias_ref, dsink_ref):
        kv = pl.program_id(0)
        sink_row = _sink_row(sink_ref, kv)

        @pl.when(pl.program_id(1) == 0)
        def _():
            dbias_ref[...] = jnp.zeros_like(dbias_ref)
            dsink_ref[...] = jnp.zeros_like(dsink_ref)

        dk_ref[...] = jnp.zeros_like(dk_ref)
        dv_ref[...] = jnp.zeros_like(dv_ref)

        def block(n, dsink):
            idx = (jnp.maximum(n - 1, 0), n, jnp.minimum(n + 1, nblk - 1))
            rows = pl.ds(pl.multiple_of(n * BLOCK, BLOCK), BLOCK)
            qpad = _pad_heads(q_ref[n], kv)
            dot_t = do_ref[rows, :].T
            prod = dot_t * o_ref[rows, :].T
            delta = jnp.concatenate(
                [jnp.sum(prod[g * HEAD_DIM:(g + 1) * HEAD_DIM, :], axis=0, keepdims=True) for g in range(GROUP)],
                axis=-1)
            dopad = _pad_heads(dot_t.astype(MM), kv)
            lse_row = lse_ref[0, n][0:1, :]
            ss = _window_scores_t(k_ref, idx, qpad, bias_ref, n, nblk)
            dqt = jnp.zeros((KV_WIDTH, GROUP * BLOCK), F32)
            for piece in range(3):
                pt = jnp.exp(ss[piece] - lse_row)
                dst = pt * (_dot(v_ref[idx[piece]], dopad) - delta)
                dsb = dst.astype(MM)
                dbias_ref[0, piece] += dst
                dqt = dqt + _dot(kt_ref[idx[piece]], dsb)
                dk_ref[0, idx[piece]] += _dot_nt(dsb, qpad)
                dv_ref[0, idx[piece]] += _dot_nt(pt.astype(MM), dopad)
            dq_ref[rows, :] = _unpad_heads(dqt, kv)
            return dsink - jnp.exp(sink_row - lse_row) * delta

        dsink = lax.fori_loop(
            0, nblk // 4, lambda i, c: block(4 * i + 3, block(4 * i + 2, block(4 * i + 1, block(4 * i, c)))),
            jnp.zeros((1, GROUP * BLOCK), F32))
        dsink_ref[0] += jnp.broadcast_to(dsink, (8, GROUP * BLOCK))

    qspec = pl.BlockSpec((seq, GROUP * HEAD_DIM), lambda k, b: (b, k))
    both = pl.BlockSpec((nblk, BLOCK, KV_WIDTH), lambda k, b: (b, 0, 0))
    grad = pl.BlockSpec((1, nblk, BLOCK, KV_WIDTH), lambda k, b: (k, b, 0, 0))
    return _call(
        body, (sink, qtb, kb3, ktb, vb3, do, o, lse, bias), name="attn_b_bwd", jobs=jobs,
        grid=(N_KV, batch),
        in_specs=[SMEM, pl.BlockSpec((nblk, GROUP * HEAD_DIM, BLOCK), lambda k, b: (b, k, 0)), both, both, both,
                  qspec, qspec, pl.BlockSpec((1, nblk, 8, GROUP * BLOCK), lambda k, b: (k, b, 0, 0)),
                  pl.BlockSpec((1, 3, BLOCK, GROUP * BLOCK), lambda k, b: (k, 0, 0, 0))],
        out_specs=[qspec, grad, grad,
                   pl.BlockSpec((1, 3, BLOCK, GROUP * BLOCK), lambda k, b: (k, 0, 0, 0)),
                   pl.BlockSpec((1, 8, GROUP * BLOCK), lambda k, b: (k, 0, 0))],
        out_shape=[jax.ShapeDtypeStruct((tokens, Q_WIDTH), F32),
                   jax.ShapeDtypeStruct((N_KV, nblk_all, BLOCK, KV_WIDTH), F32),
                   jax.ShapeDtypeStruct((N_KV, nblk_all, BLOCK, KV_WIDTH), F32),
                   jax.ShapeDtypeStruct((N_KV, 3, BLOCK, GROUP * BLOCK), F32),
                   jax.ShapeDtypeStruct((N_KV, 8, GROUP * BLOCK), F32)],
        params=_cparams(("arbitrary", "arbitrary"), 48))


def _resident(shape):
    return pl.BlockSpec(shape, lambda i: (0,) * len(shape), pipeline_mode=pl.Buffered(1))


def _mix_ffn_fwd(oa, ob, w_o, x, g2, g3, w_up, w_down, target, g4, *, tm):
    tokens = x.shape[0]
    nt = tokens // tm

    def body(oa_ref, ob_ref, wo_ref, x_ref, g2_ref, g3_ref, wu_ref, wd_ref, t_ref, g4_ref,
             mix_ref, x1_ref, h2_ref, o_ref, u_ref, df_ref, dy_ref, loss_ref, dg4_ref):
        o = jnp.concatenate([oa_ref[...].astype(MM), ob_ref[...].astype(MM)], axis=-1)
        o_ref[...] = o
        mix = _dot(o, wo_ref[...])
        mix_ref[...] = mix
        x1 = x_ref[...] + mix * _rms_r(mix) * g2_ref[...]
        x1_ref[...] = x1
        h2v = (x1 * _rms_r(x1) * g3_ref[...]).astype(MM)
        h2_ref[...] = h2v
        f = jnp.zeros((tm, D_MODEL), F32)
        for c in range(N_CHIPS):
            u = jnp.maximum(_dot(h2v, wu_ref[c]), 0.0)
            u_ref[:, c * FF_CHUNK:(c + 1) * FF_CHUNK] = u.astype(MM)
            f = f + _dot((u * u).astype(MM), wd_ref[c * FF_CHUNK:(c + 1) * FF_CHUNK, :])
        r = _rms_r(f)
        g4v = g4_ref[...]
        err = x1 + f * r * g4v - t_ref[...]
        sq = jnp.sum(err * err, axis=-1, keepdims=True)
        loss_ref[0] = jnp.broadcast_to(jnp.sum(sq, axis=0, keepdims=True) * (0.5 / D_MODEL), (8, LANES))
        dy = err * (1.0 / D_MODEL)
        dy_ref[...] = dy
        dfv, dgv = _rms_bwd(f, r, g4v, dy)
        df_ref[...] = dfv.astype(MM)
        dg4_ref[0] = jnp.sum(dgv, axis=0, keepdims=True)

    tok = pl.BlockSpec((tm, D_MODEL), lambda i: (i, 0))
    half = pl.BlockSpec((tm, Q_WIDTH), lambda i: (i, 0))
    row = pl.BlockSpec((1, D_MODEL), lambda i: (0, 0))
    tok_f32 = jax.ShapeDtypeStruct((tokens, D_MODEL), F32)
    tok_mm = jax.ShapeDtypeStruct((tokens, D_MODEL), MM)
    return pl.pallas_call(
        body, name="mix_ffn_fwd",
        grid=(nt,),
        in_specs=[half, half, _resident((D_MODEL, D_MODEL)), tok, row, row,
                  _resident((N_CHIPS, D_MODEL, FF_CHUNK)), _resident((D_FF, D_MODEL)), tok, row],
        out_specs=[tok, tok, tok, tok, pl.BlockSpec((tm, D_FF), lambda i: (i, 0)), tok, tok,
                   pl.BlockSpec((1, 8, LANES), lambda i: (i, 0, 0)),
                   pl.BlockSpec((1, 1, D_MODEL), lambda i: (i, 0, 0))],
        out_shape=[tok_f32,
                   tok_f32,
                   tok_mm,
                   tok_mm,
                   jax.ShapeDtypeStruct((tokens, D_FF), MM),
                   tok_mm,
                   tok_f32,
                   jax.ShapeDtypeStruct((nt, 8, LANES), F32),
                   jax.ShapeDtypeStruct((nt, 1, D_MODEL), F32)],
        compiler_params=_cparams(("parallel",), 56),
    )(*map(_from_hbm, (oa, ob, w_o, x, g2, g3, w_up, w_down, target, g4)))


def _ffn_bwd_act(df, w_down, u, w_up, x1, dy, mix, g3, g2, w_o, *, tm):
    tokens = df.shape[0]
    nt = tokens // tm

    def body(df_ref, wd_ref, u_ref, wu_ref, x1_ref, dy_ref, mix_ref, g3_ref, g2_ref, wo_ref,
             dz_ref, dx1_ref, dmix_ref, dg3_ref, dg2_ref, doa_ref, dob_ref):
        dfv = df_ref[...]
        dh2 = jnp.zeros((tm, D_MODEL), F32)
        for c in range(N_CHIPS):
            cols = slice(c * FF_CHUNK, (c + 1) * FF_CHUNK)
            da = _dot_nt(dfv, wd_ref[cols, :])
            dz = (da * (2.0 * u_ref[:, cols].astype(F32))).astype(MM)
            dz_ref[:, cols] = dz
            dh2 = dh2 + _dot_nt(dz, wu_ref[c])
        x1 = x1_ref[...]
        dxn, dg3v = _rms_bwd(x1, _rms_r(x1), g3_ref[...], dh2)
        dx1 = dy_ref[...] + dxn
        dx1_ref[...] = dx1
        dg3_ref[0] = jnp.sum(dg3v, axis=0, keepdims=True)
        mix = mix_ref[...]
        dmix, dg2v = _rms_bwd(mix, _rms_r(mix), g2_ref[...], dx1)
        dmb = dmix.astype(MM)
        dmix_ref[...] = dmb
        dg2_ref[0] = jnp.sum(dg2v, axis=0, keepdims=True)
        doa_ref[...] = _dot_nt(dmb, wo_ref[0:Q_WIDTH, :])
        dob_ref[...] = _dot_nt(dmb, wo_ref[Q_WIDTH:D_MODEL, :])

    tok = pl.BlockSpec((tm, D_MODEL), lambda i: (i, 0))
    half = pl.BlockSpec((tm, Q_WIDTH), lambda i: (i, 0))
    wide = pl.BlockSpec((tm, D_FF), lambda i: (i, 0))
    row = pl.BlockSpec((1, D_MODEL), lambda i: (0, 0))
    part = pl.BlockSpec((1, 1, D_MODEL), lambda i: (i, 0, 0))
    return pl.pallas_call(
        body, name="ffn_bwd_act",
        grid=(nt,),
        in_specs=[tok, _resident((D_FF, D_MODEL)), wide, _resident((N_CHIPS, D_MODEL, FF_CHUNK)),
                  tok, tok, tok, row, row, _resident((D_MODEL, D_MODEL))],
        out_specs=[wide, tok, tok, part, part, half, half],
        out_shape=[jax.ShapeDtypeStruct((tokens, D_FF), MM),
                   jax.ShapeDtypeStruct((tokens, D_MODEL), F32),
                   jax.ShapeDtypeStruct((tokens, D_MODEL), MM),
                   jax.ShapeDtypeStruct((nt, 1, D_MODEL), F32),
                   jax.ShapeDtypeStruct((nt, 1, D_MODEL), F32),
                   jax.ShapeDtypeStruct((tokens, Q_WIDTH), F32),
                   jax.ShapeDtypeStruct((tokens, Q_WIDTH), F32)],
        compiler_params=_cparams(("parallel",), 56),
    )(*map(_from_hbm, (df, w_down, u, w_up, x1, dy, mix, g3, g2, w_o)))


def _tn_matmul(a, b, *, name, tm, tn, tk, chunk=None, square_a=False, vmem_mb=48, jobs=()):
    tokens, m_dim = a.shape
    n_dim = b.shape[1]
    chunked = chunk is not None
    if chunked:
        assert tm == m_dim and tn % chunk == 0

    def body(a_ref, b_ref, o_ref):
        av = a_ref[...]
        if square_a:
            av = av.astype(F32)
            av = av * av
        part = _dot_tn(av.astype(MM), b_ref[...].astype(MM))
        if chunked:
            part = jnp.stack([part[:, c * chunk:(c + 1) * chunk] for c in range(tn // chunk)])

        @pl.when(pl.program_id(2) == 0)
        def _():
            o_ref[...] = part

        @pl.when(pl.program_id(2) > 0)
        def _():
            o_ref[...] += part

    if chunked:
        out_spec = pl.BlockSpec((tn // chunk, tm, chunk), lambda i, j, k: (j, 0, 0))
        out_shape = jax.ShapeDtypeStruct((n_dim // chunk, m_dim, chunk), F32)
    else:
        out_spec = pl.BlockSpec((tm, tn), lambda i, j, k: (i, j))
        out_shape = jax.ShapeDtypeStruct((m_dim, n_dim), F32)
    (out,), job_res = _call(
        body, (a, b), name=name, jobs=jobs,
        grid=(m_dim // tm, n_dim // tn, tokens // tk),
        in_specs=[pl.BlockSpec((tk, tm), lambda i, j, k: (k, i)),
                  pl.BlockSpec((tk, tn), lambda i, j, k: (k, j))],
        out_specs=[out_spec], out_shape=[_in_hbm(out_shape)],
        params=_cparams(("arbitrary", "arbitrary", "arbitrary"), vmem_mb))
    return out, job_res


def _proj_bwd(dqa, dkta, dvta, dqb, dktb, dvtb, raw, x, dx1, g1, w_in, gq, gk, ck, sk, *, seq, tm, sub, jobs=()):
    tokens = x.shape[0]
    nt = tokens // tm
    n_seq = seq // tm
    nblk = tm // BLOCK

    def body(dqa_ref, dkta_ref, dvta_ref, dqb_ref, dkb_ref, dvb_ref, raw_ref, x_ref, dx1_ref, g1_ref, w_ref,
             gq_ref, gk_ref, ck_ref, sk_ref,
             gx_ref, dproj_ref, dg1_ref, dgq_ref, dgk_ref, dp):
        parts = []
        for r in range(tm // sub):
            rows = slice(r * sub, (r + 1) * sub)
            qa = raw_ref[rows, 0:Q_WIDTH]
            dqn = _rope_t(dqa_ref[rows, :], ck_ref[rows, :], sk_ref[rows, :]) * SCALE
            rq = _head_r(qa)
            nq = qa * rq
            dnq = dqn * gq_ref[...]
            dp[rows, 0:Q_WIDTH] = rq * (dnq - nq * (_seg64_sum(dnq * nq) * (1.0 / HEAD_DIM)))

            ka = raw_ref[rows, Q_WIDTH:QK_RAW]
            dkn = _rope_t(dkta_ref[0, :, rows].T, ck_ref[rows, :], sk_ref[rows, :])
            rk = _head_r(ka)
            nk = ka * rk
            dnk = dkn * gk_ref[...]
            dp[rows, 512:640] = rk * (dnk - nk * (_seg64_sum(dnk * nk) * (1.0 / HEAD_DIM)))

            dp[rows, 640:768] = dvta_ref[0, :, rows].T
            dp[rows, 768:1280] = dqb_ref[rows, :] * SCALE
            for j in range(r * sub // BLOCK, (r + 1) * sub // BLOCK):
                dp[j * BLOCK:(j + 1) * BLOCK, 1280:1408] = dkb_ref[0, j] + dkb_ref[1, j]
                dp[j * BLOCK:(j + 1) * BLOCK, 1408:1536] = dvb_ref[0, j] + dvb_ref[1, j]

            dproj = dp[rows, :].astype(MM)
            dproj_ref[rows, :] = dproj
            dh1 = _dot_nt(dproj[:, 0:IN_CHUNK], w_ref[0])
            for j in range(1, N_CHIPS):
                dh1 = dh1 + _dot_nt(dproj[:, j * IN_CHUNK:(j + 1) * IN_CHUNK], w_ref[j])
            xv = x_ref[rows, :]
            dxn, dg1v = _rms_bwd(xv, _rms_r(xv), g1_ref[...], dh1)
            gx_ref[rows, :] = dx1_ref[rows, :] + dxn
            parts.append((jnp.sum(dqn * nq, axis=0, keepdims=True), jnp.sum(dkn * nk, axis=0, keepdims=True),
                          jnp.sum(dg1v, axis=0, keepdims=True)))
        dgq_ref[0] = functools.reduce(jnp.add, [p[0] for p in parts])
        dgk_ref[0] = functools.reduce(jnp.add, [p[1] for p in parts])
        dg1_ref[0] = functools.reduce(jnp.add, [p[2] for p in parts])

    tok = lambda w: pl.BlockSpec((tm, w), lambda i: (i, 0))
    tab = lambda w: pl.BlockSpec((tm, w), lambda i: (i % n_seq, 0))
    row = lambda w: pl.BlockSpec((1, w), lambda i: (0, 0))
    tposed = pl.BlockSpec((1, KV_WIDTH, tm), lambda i: (i // n_seq, 0, i % n_seq))
    blocks = pl.BlockSpec((N_KV, nblk, BLOCK, KV_WIDTH), lambda i: (0, i, 0, 0))
    part = lambda w: pl.BlockSpec((1, 1, w), lambda i: (i, 0, 0))
    return _call(
        body, (dqa, dkta, dvta, dqb, dktb, dvtb, raw, x, dx1, g1, w_in, gq, gk, ck, sk),
        name="proj_bwd", jobs=jobs,
        grid=(nt,),
        in_specs=[tok(Q_WIDTH), tposed, tposed, tok(Q_WIDTH), blocks, blocks, tok(QK_RAW), tok(D_MODEL),
                  tok(D_MODEL), row(D_MODEL),
                  pl.BlockSpec((N_CHIPS, D_MODEL, IN_CHUNK), lambda i: (0, 0, 0)),
                  row(Q_WIDTH), row(KV_WIDTH), tab(KV_WIDTH), tab(KV_WIDTH)],
        out_specs=[tok(D_MODEL), tok(IN_TOTAL), part(D_MODEL), part(Q_WIDTH), part(KV_WIDTH)],
        out_shape=[jax.ShapeDtypeStruct((tokens, D_MODEL), F32),
                   jax.ShapeDtypeStruct((tokens, IN_TOTAL), MM),
                   jax.ShapeDtypeStruct((nt, 1, D_MODEL), F32),
                   jax.ShapeDtypeStruct((nt, 1, Q_WIDTH), F32),
                   jax.ShapeDtypeStruct((nt, 1, KV_WIDTH), F32)],
        scratch_shapes=[pltpu.VMEM((tm, IN_TOTAL), F32)],
        params=_cparams(("arbitrary",), 56))


def _pack_small(dg1, dg2, dg3, dg4, dgq, dgk, dsink, dbias, bucket, loss):
    def body(dg1_ref, dg2_ref, dg3_ref, dg4_ref, dgq_ref, dgk_ref, dsink_ref, dbias_ref, bucket_ref, loss_ref,
             out_ref, rel_ref):
        out_ref[...] = jnp.zeros_like(out_ref)
        for r, ref in ((ROW_G1, dg1_ref), (ROW_G2, dg2_ref), (ROW_G3, dg3_ref), (ROW_G4, dg4_ref)):
            acc = ref[0]
            for t in range(1, ref.shape[0]):
                acc = acc + ref[t]
            out_ref[r:r + 1, :] = acc

        def fold(ref, heads):
            acc = ref[0]
            for t in range(1, ref.shape[0]):
                acc = acc + ref[t]
            tot = acc[:, 0:HEAD_DIM]
            for h in range(1, heads):
                tot = tot + acc[:, h * HEAD_DIM:(h + 1) * HEAD_DIM]
            return tot

        out_ref[ROW_MISC:ROW_MISC + 1, MISC_GQ:MISC_GQ + HEAD_DIM] = fold(dgq_ref, GROUP * N_KV)
        out_ref[ROW_MISC:ROW_MISC + 1, MISC_GK:MISC_GK + HEAD_DIM] = fold(dgk_ref, N_KV)
        for h in range(GROUP * N_KV):
            g = h % GROUP
            out_ref[ROW_MISC:ROW_MISC + 1, MISC_SINK + h:MISC_SINK + h + 1] = jnp.sum(
                dsink_ref[h // GROUP, 0:1, g * BLOCK:(g + 1) * BLOCK], axis=-1, keepdims=True)
        lacc = loss_ref[0, 0:1, 0:1]
        for t in range(1, loss_ref.shape[0]):
            lacc = lacc + loss_ref[t, 0:1, 0:1]
        out_ref[ROW_MISC:ROW_MISC + 1, MISC_LOSS:MISC_LOSS + 1] = lacc
        lane = lax.broadcasted_iota(jnp.int32, (N_BUCKETS, LANES), 1)
        row = lax.broadcasted_iota(jnp.int32, (N_BUCKETS, LANES), 0)

        def per_bucket(b, acc):
            for h in range(GROUP * N_KV):
                g = h % GROUP
                sel = jnp.zeros((BLOCK, BLOCK), F32)
                for piece in range(3):
                    sel = sel + jnp.where(bucket_ref[piece] == b,
                                          dbias_ref[h // GROUP, piece, :, g * BLOCK:(g + 1) * BLOCK], 0.0)
                tot = jnp.sum(jnp.sum(sel, axis=0, keepdims=True), axis=-1, keepdims=True)
                acc = jnp.where((row == b) & (lane == h), tot, acc)
            return acc

        rel_ref[...] = lax.fori_loop(0, N_BUCKETS, per_bucket, jnp.zeros((N_BUCKETS, LANES), F32))

    args = (dg1, dg2, dg3, dg4, dgq, dgk, dsink, dbias, bucket, loss)
    outs = [jax.ShapeDtypeStruct((8, D_MODEL), F32), jax.ShapeDtypeStruct((N_BUCKETS, LANES), F32)]
    return pl.pallas_call(
        body, name="pack_small", grid=(1,),
        in_specs=[_whole(a) for a in args], out_specs=[_whole(o) for o in outs], out_shape=outs,
        compiler_params=pltpu.CompilerParams(vmem_limit_bytes=32 * 1024 * 1024),
    )(*map(_from_hbm, args))


def _gather_weights(shards, whole):
    n = len(shards)
    full = [t for t in range(n) if whole[t]]

    def body(*refs):
        ins, outs = refs[:n], refs[n:2 * n]
        raw, stage = refs[2 * n:3 * n], refs[3 * n:4 * n]
        load_sem, local_sem, ici_send, ici_recv, d2d_send, d2d_recv = refs[4 * n:]
        x, y, c = _place()
        k = 2 * x + y
        sibling = (x, y, 1 - c)
        order = full + [t for t in range(n) if t not in full]
        loads = {t: pltpu.make_async_copy(ins[t], raw[t], load_sem.at[t]) for t in order}
        for t in order:
            loads[t].start()
        copies, sends = [], []
        for t in order:
            loads[t].wait()
            stage[t][...] = raw[t][...].astype(MM)
            mine = pltpu.make_async_copy(stage[t], outs[t].at[k], local_sem.at[t])
            mine.start()
            copies.append(mine)
            if t in full:
                half = ins[t].shape[0] // 2
                rows = pl.ds(c * half, half)
                for r, (fx, fy) in enumerate(_CHIP_FLIPS):
                    cp = _remote(stage[t].at[rows], outs[t].at[k, rows], ici_send.at[t, r], ici_recv.at[t, r],
                                 (_flip(x, fx), _flip(y, fy), c))
                    cp.start()
                    sends.append(cp)
        for t in full:
            half = ins[t].shape[0] // 2
            rows = pl.ds(c * half, half)
            for r, (fx, fy) in enumerate(_CHIP_FLIPS):
                kk = 2 * _flip(x, fx) + _flip(y, fy)
                landed = outs[t].at[kk, rows]
                _remote(landed, landed, ici_send.at[t, r], ici_recv.at[t, r], sibling).wait_recv()
                fwd = _remote(landed, landed, d2d_send.at[t, r], d2d_recv.at[t, r], sibling)
                fwd.start()
                sends.append(fwd)
        for t in full:
            half = ins[t].shape[0] // 2
            other = pl.ds((1 - c) * half, half)
            for r, (fx, fy) in enumerate(_CHIP_FLIPS):
                kk = 2 * _flip(x, fx) + _flip(y, fy)
                theirs = outs[t].at[kk, other]
                _remote(theirs, theirs, d2d_send.at[t, r], d2d_recv.at[t, r], sibling).wait_recv()
        for cp in sends:
            cp.wait_send()
        for cp in copies:
            cp.wait()

    return pl.pallas_call(
        body, name="gather_weights",
        in_specs=[HBM] * n, out_specs=[HBM] * n,
        out_shape=[pltpu.HBM((N_CHIPS,) + s.shape, MM) for s in shards],
        scratch_shapes=[pltpu.VMEM(s.shape, F32) for s in shards] + [pltpu.VMEM(s.shape, MM) for s in shards] + [
            pltpu.SemaphoreType.DMA((n,)), pltpu.SemaphoreType.DMA((n,)),
            pltpu.SemaphoreType.DMA((n, 3)), pltpu.SemaphoreType.DMA((n, 3)),
            pltpu.SemaphoreType.DMA((n, 3)), pltpu.SemaphoreType.DMA((n, 3))],
        compiler_params=pltpu.CompilerParams(vmem_limit_bytes=40 * 1024 * 1024),
    )(*shards)


def _add_half(grad, got, where, *, name, tr):
    nch, half, cols = got.shape
    nblk = half // tr

    def body(where_ref, g_ref, r_ref, o_ref):
        o_ref[...] = (g_ref[...] + r_ref[...]).astype(MM)

    return pl.pallas_call(
        body, name=name,
        grid_spec=pltpu.PrefetchScalarGridSpec(
            num_scalar_prefetch=1, grid=(nch, nblk),
            in_specs=[pl.BlockSpec((1, tr, cols), lambda j, i, where_ref: (j, where_ref[1] * nblk + i, 0)),
                      pl.BlockSpec((1, tr, cols), lambda j, i, where_ref: (j, i, 0))],
            out_specs=pl.BlockSpec((1, tr, cols), lambda j, i, where_ref: (j, i, 0))),
        out_shape=jax.ShapeDtypeStruct(got.shape, MM),
        compiler_params=_cparams(("parallel", "parallel"), 32),
    )(where, grad, got)


def _add_chips(own, got, where, *, name, tr):
    _, half, cols = own.shape
    nblk = half // tr

    def body(where_ref, o_ref, g_ref, out_ref):
        f = lambda v: v.astype(F32)
        out_ref[...] = ((f(o_ref[0]) + f(g_ref[0])) + f(g_ref[1])) + f(g_ref[2])

    return pl.pallas_call(
        body, name=name,
        grid_spec=pltpu.PrefetchScalarGridSpec(
            num_scalar_prefetch=1, grid=(nblk,),
            in_specs=[pl.BlockSpec((1, tr, cols), lambda i, where_ref: (where_ref[0], i, 0)),
                      pl.BlockSpec((3, tr, cols), lambda i, where_ref: (0, i, 0))],
            out_specs=pl.BlockSpec((tr, cols), lambda i, where_ref: (where_ref[1] * nblk + i, 0))),
        out_shape=pltpu.HBM((2 * half, cols), F32),
        compiler_params=_cparams(("parallel",), 32),
    )(where, own, got)


def _small_job(tiles):
    n = len(tiles)

    def copies(ins, outs, sems):
        x, y, c = _place()
        me = 4 * x + 2 * y + c
        local, send, recv = sems
        cps = []
        for t in range(n):
            cps.append(pltpu.make_async_copy(ins[t], outs[t].at[me], local.at[t]))
            for r in range(1, N_DEV):
                fx, fy, fc = (r >> 2) & 1, (r >> 1) & 1, r & 1
                cps.append(_remote(ins[t], outs[t].at[me], send.at[t, r - 1], recv.at[t, r - 1],
                                   (_flip(x, fx), _flip(y, fy), _flip(c, fc))))
        return cps

    return _Job(tiles, [jax.ShapeDtypeStruct((N_DEV,) + t.shape, F32) for t in tiles],
                [pltpu.SemaphoreType.DMA((n,)), pltpu.SemaphoreType.DMA((n, N_DEV - 1)),
                 pltpu.SemaphoreType.DMA((n, N_DEV - 1))], copies)


def _adamw_math(w, g, m, v):
    m = ADAM_B1 * m + (1.0 - ADAM_B1) * g
    v = ADAM_B2 * v + (1.0 - ADAM_B2) * (g * g)
    m_hat = m / (1.0 - ADAM_B1 ** ADAM_STEP)
    v_hat = v / (1.0 - ADAM_B2 ** ADAM_STEP)
    delta = -ADAM_LR * (m_hat / (jnp.sqrt(v_hat) + ADAM_EPS) + ADAM_WD * w)
    return delta, m, v


def _adamw(w, g, m, v, *, name, tr):
    rows, cols = w.shape

    def body(w_ref, g_ref, m_ref, v_ref, go_ref, d_ref, nm_ref, nv_ref):
        g = g_ref[...]
        go_ref[...] = g
        d_ref[...], nm_ref[...], nv_ref[...] = _adamw_math(w_ref[...], g, m_ref[...], v_ref[...])

    spec = pl.BlockSpec((tr, cols), lambda i: (i, 0))
    return pl.pallas_call(
        body, name=name,
        grid=(rows // tr,),
        in_specs=[spec] * 4, out_specs=[spec] * 4,
        out_shape=[jax.ShapeDtypeStruct(w.shape, F32)] * 4,
        compiler_params=_cparams(("parallel",), 32),
    )(w, g, m, v)


def _small_adamw(gathered, gathered_rel, params, moments_m, moments_v):
    n = len(params)

    def body(all_ref, rel_all_ref, *refs):
        w_refs, m_refs, v_refs = refs[:n], refs[n:2 * n], refs[2 * n:3 * n]
        loss_ref = refs[3 * n]
        out_refs = refs[3 * n + 1:]
        g = all_ref[0]
        rel = rel_all_ref[0]
        for d in range(1, N_DEV):
            g = g + all_ref[d]
            rel = rel + rel_all_ref[d]
        misc = g[ROW_MISC:ROW_MISC + 1]
        loss_ref[...] = misc[:, MISC_LOSS:MISC_LOSS + 1]
        grads = (g[ROW_G1:ROW_G1 + 1], g[ROW_G2:ROW_G2 + 1], g[ROW_G3:ROW_G3 + 1], g[ROW_G4:ROW_G4 + 1],
                 misc[:, MISC_GQ:MISC_GQ + HEAD_DIM], misc[:, MISC_GK:MISC_GK + HEAD_DIM],
                 misc[:, MISC_SINK:MISC_SINK + GROUP * N_KV], rel[:, 0:GROUP * N_KV])
        for i in range(n):
            d, nm, nv = _adamw_math(w_refs[i][...], grads[i], m_refs[i][...], v_refs[i][...])
            for j, val in enumerate((grads[i], d, nm, nv)):
                out_refs[4 * i + j][...] = val

    args = (gathered, gathered_rel, *params, *moments_m, *moments_v)
    out_shape = [jax.ShapeDtypeStruct((1, 1), F32)] + [jax.ShapeDtypeStruct(p.shape, F32) for p in params
                                                       for _ in range(4)]
    outs = pl.pallas_call(
        body, name="small_adamw", grid=(1,),
        in_specs=[_whole(a) for a in args], out_specs=[_whole(o) for o in out_shape], out_shape=out_shape,
    )(*map(_from_hbm, args))
    return outs[0], [outs[1 + 4 * i:5 + 4 * i] for i in range(n)]


def kernel(x, w_in, w_o, g_pre_mix, g_post_mix, q_norm_a, k_norm_a, sink_b, rel_bias, g_pre_ffn, w_ffn_up, w_ffn_down, g_post_ffn, loss_target, m_w_in, m_w_o, m_g_pre_mix, m_g_post_mix, m_q_norm_a, m_k_norm_a, m_sink_b, m_rel_bias, m_g_pre_ffn, m_w_ffn_up, m_w_ffn_down, m_g_post_ffn, v_w_in, v_w_o, v_g_pre_mix, v_g_post_mix, v_q_norm_a, v_k_norm_a, v_sink_b, v_rel_bias, v_g_pre_ffn, v_w_ffn_up, v_w_ffn_down, v_g_post_ffn):
    batch, seq, _ = x.shape
    tokens = batch * seq
    where = jnp.stack([2 * lax.axis_index("x") + lax.axis_index("y"), lax.axis_index("c")]).astype(jnp.int32)
    x2 = x.reshape(tokens, D_MODEL)
    g1, g2, g3, g4 = g_pre_mix, g_post_mix, g_pre_ffn, g_post_ffn

    cos, sin = _rope_tables(seq)
    ck, sk = jnp.tile(cos, (1, 2)), jnp.tile(sin, (1, 2))
    gq8, gk2 = jnp.tile(q_norm_a, (1, 8)), jnp.tile(k_norm_a, (1, 2))
    bucket, band = _window_tables()
    bias = _bias_build(rel_bias.T, bucket, band)

    w_in_g, w_o_p, w_up_p, w_down_p = _gather_weights(
        (w_in[0], w_o[0], w_ffn_up[0], w_ffn_down[0]), whole=(True, False, False, False))
    (h1, raw, qa, ka, kta, va, vta, qtb, kb, ktb, vb, vtb) = _pre_proj(
        x2, g1, w_in_g, gq8, gk2, ck, sk, seq=seq, tm=min(512, seq), sub=256)
    (oa, p_a, linv_a), (w_part,) = _attn_a_fwd(
        qa, kta, va, seq=seq, bq=min(256, seq), jobs=[_gather_job([w_o_p, w_up_p, w_down_p], forward=False)])
    kb3 = kb.reshape(tokens // BLOCK, BLOCK, KV_WIDTH)
    vb3 = vb.reshape(tokens // BLOCK, BLOCK, KV_WIDTH)
    (ob, lse_b), ((w_o_g, w_up_g, w_down_g),) = _attn_b_fwd(
        qtb, kb3, vtb, bias, sink_b, seq=seq, jobs=[_gather_job(w_part, forward=True)])
    w_o2 = w_o_g.reshape(D_MODEL, D_MODEL)
    w_down2 = w_down_g.reshape(D_FF, D_MODEL)
    mix, x1, h2, o_cat, u, df, dy, loss_t, dg4 = _mix_ffn_fwd(
        oa, ob, w_o2, x2, g2, g3, w_up_g, w_down2, loss_target.reshape(tokens, D_MODEL), g4, tm=256)

    dz, dx1, dmix, dg3, dg2, doa, dob = _ffn_bwd_act(df, w_down2, u, w_up_g, x1, dy, mix, g3, g2, w_o2, tm=256)
    gw_down, _ = _tn_matmul(u, df, name="grad_w_down", tm=1024, tn=1024, tk=min(2048, tokens), square_a=True)
    gw_down = gw_down.reshape(N_CHIPS, FF_CHUNK, D_MODEL)
    gw_up, ((got_down,),) = _tn_matmul(h2, dz, name="grad_w_up", tm=1024, tn=1024, tk=min(2048, tokens), chunk=FF_CHUNK,
                                        jobs=[_swap_job([gw_down])])
    gw_o, _ = _tn_matmul(o_cat, dmix, name="grad_w_o", tm=1024, tn=1024, tk=min(2048, tokens))
    gw_o = gw_o.reshape(N_CHIPS, O_CHUNK, D_MODEL)
    sum_down = _add_half(gw_down, got_down, where, name="add_half_w_down", tr=128)
    (dqa, dkta, dvta), ((ex_down,), (got_up,)) = _attn_a_bwd(
        qa, ka, vta, doa, oa, p_a, linv_a, seq=seq, bq=min(256, seq),
        jobs=[_exchange_job([sum_down]), _swap_job([gw_up])])
    full_down = _add_chips(sum_down, ex_down, where, name="add_chips_w_down", tr=128)
    sum_up = _add_half(gw_up, got_up, where, name="add_half_w_up", tr=128)
    (dqb, dkb, dvb, dbias, dsink), ((ex_up,), (g_down,), (got_o,)) = _attn_b_bwd(
        qtb, kb3, ktb, vb3, dob, ob, lse_b, bias, sink_b, seq=seq,
        jobs=[_exchange_job([sum_up]), _join_job([full_down]), _swap_job([gw_o])])
    full_up = _add_chips(sum_up, ex_up, where, name="add_chips_w_up", tr=128)
    sum_o = _add_half(gw_o, got_o, where, name="add_half_w_o", tr=128)
    (grad_x, dproj, dg1, dgq, dgk), _ = _proj_bwd(
        dqa, dkta, dvta, dqb, dkb, dvb, raw, x2, dx1, g1, w_in_g, gq8, gk2, ck, sk,
        seq=seq, tm=min(512, seq), sub=128)
    packed, packed_rel = _pack_small(dg1, dg2, dg3, dg4, dgq, dgk, dsink, dbias, bucket, loss_t)
    gw_in, ((ex_o,), (g_up,), (gathered, gathered_rel)) = _tn_matmul(
        h1, dproj, name="grad_w_in", tm=1024, tn=2 * IN_CHUNK, tk=min(2048, tokens), chunk=IN_CHUNK,
        jobs=[_exchange_job([sum_o]), _join_job([full_up]), _small_job([packed, packed_rel])])
    full_o = _add_chips(sum_o, ex_o, where, name="add_chips_w_o", tr=128)

    (g_o,), (got_in,) = _run_jobs("tail_swap", [_join_job([full_o]), _swap_job([gw_in])])
    sum_in = _add_half(gw_in, got_in, where, name="add_half_w_in", tr=128)
    ((ex_in,),) = _run_jobs("tail_exchange", [_exchange_job([sum_in])])
    full_in = _add_chips(sum_in, ex_in, where, name="add_chips_w_in", tr=128)
    ((g_in,),) = _run_jobs("tail_join", [_join_job([full_in])])

    big = [[t[None] for t in _adamw(w[0], g, m[0], v[0], name="adamw_" + nm, tr=128)] for nm, w, g, m, v in (
        ("w_in", w_in, g_in, m_w_in, v_w_in), ("w_o", w_o, g_o, m_w_o, v_w_o),
        ("w_up", w_ffn_up, g_up, m_w_ffn_up, v_w_ffn_up), ("w_down", w_ffn_down, g_down, m_w_ffn_down, v_w_ffn_down))]

    loss, small = _small_adamw(
        gathered, gathered_rel,
        (g1, g2, g3, g4, q_norm_a, k_norm_a, sink_b, rel_bias),
        (m_g_pre_mix, m_g_post_mix, m_g_pre_ffn, m_g_post_ffn, m_q_norm_a, m_k_norm_a, m_sink_b, m_rel_bias),
        (v_g_pre_mix, v_g_post_mix, v_g_pre_ffn, v_g_post_ffn, v_q_norm_a, v_k_norm_a, v_sink_b, v_rel_bias))
    s_g1, s_g2, s_g3, s_g4, s_gq, s_gk, s_sink, s_rel = small

    def leaves(i):
        return (big[0][i], big[1][i], s_g1[i], s_g2[i], s_gq[i], s_gk[i], s_sink[i], s_rel[i], s_g3[i],
                big[2][i], big[3][i], s_g4[i])

    loss = loss.reshape(())
    return (loss, grad_x.reshape(batch, seq, D_MODEL), *leaves(0), *leaves(1), *leaves(2), *leaves(3))
```

```python
import functools

import jax
import jax.numpy as jnp
import numpy as np
from jax import lax
from jax.experimental import pallas as pl
from jax.experimental.pallas import tpu as pltpu

F32 = jnp.float32
MM = jnp.bfloat16

D_MODEL = 1024
HEAD_DIM = 64
N_KV = 2
GROUP = 4
Q_WIDTH = 512
KV_WIDTH = 128
D_FF = 4096
GRID_W = 64
BLOCK = 128
N_BUCKETS = 32
MAX_DISTANCE = 128
ROPE_THETA = 10000.0
EPS = 1e-6
NEG_INF = -1e30
SCALE = HEAD_DIM ** -0.5
IN_TOTAL = 1536
N_CHIPS = 4
N_DEV = 8
IN_CHUNK = IN_TOTAL // N_CHIPS
FF_CHUNK = D_FF // N_CHIPS
O_CHUNK = D_MODEL // N_CHIPS
QK_RAW = 640

ADAM_LR = 0.001
ADAM_B1 = 0.9
ADAM_B2 = 0.999
ADAM_EPS = 1e-08
ADAM_WD = 0.01
ADAM_STEP = 10

LANES = 128
MESH = pl.DeviceIdType.MESH
HBM = pl.BlockSpec(memory_space=pl.ANY)
VMEM = pl.BlockSpec(memory_space=pltpu.VMEM)
SMEM = pl.BlockSpec(memory_space=pltpu.SMEM)

ROW_G1, ROW_G2, ROW_G3, ROW_G4, ROW_MISC = 0, 1, 2, 3, 4
MISC_GQ, MISC_GK, MISC_SINK, MISC_LOSS = 0, 64, 128, 512


def _cparams(sem, vmem_mb):
    return pltpu.CompilerParams(dimension_semantics=sem, vmem_limit_bytes=vmem_mb * 1024 * 1024)


def _whole(a):
    return pl.BlockSpec(a.shape, lambda i: (0,) * len(a.shape))


def _from_hbm(a):
    return pltpu.with_memory_space_constraint(a, pltpu.HBM)


def _in_hbm(s):
    return pltpu.HBM(s.shape, s.dtype)


class _Job:
    def __init__(self, operands, out_shapes, sems, copies, alias=None):
        self.operands, self.out_shapes, self.sems, self.copies = list(operands), list(out_shapes), list(sems), copies
        self.alias = dict(alias or {})


def _place():
    return lax.axis_index("x"), lax.axis_index("y"), lax.axis_index("c")


_CHIP_FLIPS = ((1, 0), (0, 1), (1, 1))


def _flip(v, bit):
    return 1 - v if bit else v


def _remote(src, dst, send, recv, dev):
    return pltpu.make_async_remote_copy(src_ref=src, dst_ref=dst, send_sem=send, recv_sem=recv,
                                        device_id=dev, device_id_type=MESH)


def _swap_job(grads):
    n = len(grads)

    def copies(ins, outs, sems):
        x, y, c = _place()
        send, recv = sems
        cps = []
        for t in range(n):
            half = ins[t].shape[1] // 2
            cps.append(_remote(ins[t].at[:, pl.ds((1 - c) * half, half), :], outs[t], send.at[t], recv.at[t],
                               (x, y, 1 - c)))
        return cps

    shapes = [jax.ShapeDtypeStruct((g.shape[0], g.shape[1] // 2, g.shape[2]), F32) for g in grads]
    return _Job(grads, shapes, [pltpu.SemaphoreType.DMA((n,)), pltpu.SemaphoreType.DMA((n,))], copies)


def _exchange_job(sums):
    n = len(sums)

    def copies(ins, outs, sems):
        x, y, c = _place()
        send, recv = sems
        cps = []
        for t in range(n):
            for r, (fx, fy) in enumerate(_CHIP_FLIPS):
                kk = 2 * _flip(x, fx) + _flip(y, fy)
                cps.append(_remote(ins[t].at[kk], outs[t].at[r], send.at[t, r], recv.at[t, r],
                                   (_flip(x, fx), _flip(y, fy), c)))
        return cps

    shapes = [jax.ShapeDtypeStruct((3,) + s.shape[1:], s.dtype) for s in sums]
    return _Job(sums, shapes, [pltpu.SemaphoreType.DMA((n, 3)), pltpu.SemaphoreType.DMA((n, 3))], copies)


def _join_job(fulls):
    n = len(fulls)

    def copies(ins, outs, sems):
        x, y, c = _place()
        send, recv = sems
        cps = []
        for t in range(n):
            half = ins[t].shape[0] // 2
            rows = pl.ds(c * half, half)
            cps.append(_remote(ins[t].at[rows], outs[t].at[rows], send.at[t], recv.at[t], (x, y, 1 - c)))
        return cps

    shapes = [jax.ShapeDtypeStruct(f.shape, f.dtype) for f in fulls]
    return _Job(fulls, shapes, [pltpu.SemaphoreType.DMA((n,)), pltpu.SemaphoreType.DMA((n,))], copies,
                alias={t: t for t in range(n)})


def _gather_job(bufs, forward):
    n = len(bufs)

    def copies(ins, outs, sems):
        x, y, c = _place()
        send, recv = sems
        cps = []
        for t in range(n):
            half = ins[t].shape[1] // 2
            rows = pl.ds(c * half, half)
            for r, (fx, fy) in enumerate(_CHIP_FLIPS):
                if forward:
                    kk = 2 * _flip(x, fx) + _flip(y, fy)
                    dev = (x, y, 1 - c)
                else:
                    kk = 2 * x + y
                    dev = (_flip(x, fx), _flip(y, fy), c)
                cps.append(_remote(ins[t].at[kk, rows], outs[t].at[kk, rows], send.at[t, r], recv.at[t, r], dev))
        return cps

    shapes = [jax.ShapeDtypeStruct(b.shape, b.dtype) for b in bufs]
    return _Job(bufs, shapes, [pltpu.SemaphoreType.DMA((n, 3)), pltpu.SemaphoreType.DMA((n, 3))], copies,
                alias={t: t for t in range(n)})


def _call(body, args, *, name, grid, in_specs, out_specs, out_shape, scratch_shapes=(), params=None, jobs=()):
    n_in, n_out, n_scr = len(in_specs), len(out_specs), len(scratch_shapes)
    job_in = [len(j.operands) for j in jobs]
    job_out = [len(j.out_shapes) for j in jobs]
    job_sem = [len(j.sems) for j in jobs]

    def wrapped(*refs):
        pos = 0
        ins = refs[pos:pos + n_in]; pos += n_in
        jins = []
        for k in job_in:
            jins.append(refs[pos:pos + k]); pos += k
        outs = refs[pos:pos + n_out]; pos += n_out
        jouts = []
        for k in job_out:
            jouts.append(refs[pos:pos + k]); pos += k
        scr = refs[pos:pos + n_scr]; pos += n_scr
        jsems = []
        for k in job_sem:
            jsems.append(refs[pos:pos + k]); pos += k
        if jobs:
            ids = [pl.program_id(d) for d in range(len(grid))]
            first = functools.reduce(jnp.logical_and, [i == 0 for i in ids])
            last = functools.reduce(jnp.logical_and, [i == g - 1 for i, g in zip(ids, grid)])

            @pl.when(first)
            def _():
                for j, ji, jo, js in zip(jobs, jins, jouts, jsems):
                    for cp in j.copies(ji, jo, js):
                        cp.start()

        body(*ins, *outs, *scr)
        if jobs:
            @pl.when(last)
            def _():
                for j, ji, jo, js in zip(jobs, jins, jouts, jsems):
                    for cp in j.copies(ji, jo, js):
                        cp.wait()

    aliases = {}
    in_pos, out_pos = n_in, n_out
    for j in jobs:
        for i, o in j.alias.items():
            aliases[in_pos + i] = out_pos + o
        in_pos += len(j.operands)
        out_pos += len(j.out_shapes)
    res = pl.pallas_call(
        wrapped, name=name, grid=grid,
        in_specs=list(in_specs) + [HBM] * sum(job_in),
        out_specs=list(out_specs) + [HBM] * sum(job_out),
        out_shape=list(out_shape) + [_in_hbm(s) for j in jobs for s in j.out_shapes],
        scratch_shapes=list(scratch_shapes) + [s for j in jobs for s in j.sems],
        input_output_aliases=aliases,
        compiler_params=params,
    )(*[a if spec is SMEM else _from_hbm(a) for a, spec in zip(args, in_specs)],
      *[a for j in jobs for a in j.operands])
    own, rest = list(res[:n_out]), list(res[n_out:])
    job_res = []
    for k in job_out:
        job_res.append(rest[:k])
        rest = rest[k:]
    return own, job_res


def _run_jobs(name, jobs):
    def body():
        pass

    return _call(body, (), name=name, grid=(1,), in_specs=[], out_specs=[], out_shape=[], jobs=jobs)[1]


def _dot(a, b):
    return jnp.dot(a, b, preferred_element_type=F32)


def _dot_nt(a, b):
    return lax.dot_general(a, b, (((1,), (1,)), ((), ())), preferred_element_type=F32)


def _dot_tn(a, b):
    return lax.dot_general(a, b, (((0,), (0,)), ((), ())), preferred_element_type=F32)


def _rms_r(x):
    return lax.rsqrt(jnp.mean(x * x, axis=-1, keepdims=True) + EPS)


def _rms_bwd(x, r, g, dy):
    n = x * r
    dn = dy * g
    dx = r * (dn - n * jnp.mean(dn * n, axis=-1, keepdims=True))
    return dx, dy * n


def _seg64_sum(v):
    rows, width = v.shape
    lane = lax.broadcasted_iota(jnp.int32, (rows, LANES), 1)
    lo = lane < HEAD_DIM
    outs = []
    for c in range(width // LANES):
        ch = v[:, c * LANES:(c + 1) * LANES]
        s_lo = jnp.sum(jnp.where(lo, ch, 0.0), axis=-1, keepdims=True)
        s_hi = jnp.sum(jnp.where(lo, 0.0, ch), axis=-1, keepdims=True)
        outs.append(jnp.where(lo, s_lo, s_hi))
    return outs[0] if len(outs) == 1 else jnp.concatenate(outs, axis=-1)


def _head_r(v):
    return lax.rsqrt(_seg64_sum(v * v) * (1.0 / HEAD_DIM) + EPS)


def _swap16(ch):
    lane = lax.broadcasted_iota(jnp.int32, ch.shape, 1)
    return jnp.where((lane % 32) < 16, pltpu.roll(ch, LANES - 16, 1), pltpu.roll(ch, 16, 1))


def _by_chunk(fn, v):
    outs = [fn(v[:, c * LANES:(c + 1) * LANES]) for c in range(v.shape[1] // LANES)]
    return outs[0] if len(outs) == 1 else jnp.concatenate(outs, axis=-1)


def _rope(v, cos, sin_signed):
    return _by_chunk(lambda ch: ch * cos + _swap16(ch) * sin_signed, v)


def _rope_t(g, cos, sin_signed):
    return _by_chunk(lambda ch: ch * cos + _swap16(ch * sin_signed), g)


def _rope_tables(seq):
    nf = HEAD_DIM // 4
    freqs = ROPE_THETA ** (-jnp.arange(nf, dtype=F32) / nf)
    pos = jnp.arange(seq, dtype=jnp.int32)
    row = (pos // GRID_W).astype(F32)
    col = (pos % GRID_W).astype(F32)
    ang_r = row[:, None] * freqs[None, :]
    ang_c = col[:, None] * freqs[None, :]
    cr, sr, cc, sc = jnp.cos(ang_r), jnp.sin(ang_r), jnp.cos(ang_c), jnp.sin(ang_c)
    cos = jnp.concatenate([cr, cr, cc, cc], axis=1)
    sin = jnp.concatenate([-sr, sr, -sc, sc], axis=1)
    return cos, sin


def _t5_bucket(rel):
    nb = N_BUCKETS // 2
    ret = (rel > 0).astype(jnp.int32) * nb
    n = jnp.abs(rel)
    max_exact = nb // 2
    nf = jnp.maximum(n, 1).astype(jnp.float32)
    large = max_exact + (jnp.log(nf / max_exact) / np.float32(np.log(MAX_DISTANCE / max_exact))
                         * (nb - max_exact)).astype(jnp.int32)
    large = jnp.minimum(large, nb - 1)
    return ret + jnp.where(n < max_exact, n, large)


def _window_tables():
    a = jnp.arange(BLOCK, dtype=jnp.int32)
    c = jnp.arange(3 * BLOCK, dtype=jnp.int32)
    rel = c[None, :] - BLOCK - a[:, None]
    bucket = _t5_bucket(rel)
    band = (jnp.abs(rel) <= BLOCK).astype(jnp.int32)
    to3 = lambda t: t.reshape(BLOCK, 3, BLOCK).transpose(1, 2, 0)
    return to3(bucket), to3(band)


def _pre_proj(x, g1, w_in, gq, gk, ck, sk, *, seq, tm, sub):
    tokens = x.shape[0]
    n_seq = seq // tm
    nblk = tm // BLOCK
    batch = tokens // seq

    def body(x_ref, g1_ref, w_ref, gq_ref, gk_ref, ck_ref, sk_ref,
             h1_ref, raw_ref, qa_ref, ka_ref, kta_ref, va_ref, vta_ref,
             qtb_ref, kb_ref, ktb_ref, vb_ref, vtb_ref, proj):
        for r in range(tm // sub):
            rows = slice(r * sub, (r + 1) * sub)
            xv = x_ref[rows, :]
            h = (xv * _rms_r(xv) * g1_ref[...]).astype(MM)
            h1_ref[rows, :] = h
            for j in range(N_CHIPS):
                proj[rows, j * IN_CHUNK:(j + 1) * IN_CHUNK] = _dot(h, w_ref[j])
            qa = proj[rows, 0:Q_WIDTH]
            ka = proj[rows, Q_WIDTH:QK_RAW]
            raw_ref[rows, :] = proj[rows, 0:QK_RAW]
            qn = qa * _head_r(qa) * gq_ref[...]
            qa_ref[rows, :] = (_rope(qn, ck_ref[rows, :], sk_ref[rows, :]) * SCALE).astype(MM)
            kn = ka * _head_r(ka) * gk_ref[...]
            kr = _rope(kn, ck_ref[rows, :], sk_ref[rows, :])
            ka_ref[rows, :] = kr.astype(MM)
            kta_ref[0, :, rows] = kr.T.astype(MM)
            va = proj[rows, 640:768]
            va_ref[rows, :] = va.astype(MM)
            vta_ref[0, :, rows] = va.T.astype(MM)
            qb = proj[rows, 768:1280] * SCALE
            kb = proj[rows, 1280:1408]
            vb = proj[rows, 1408:1536]
            kb_ref[rows, :] = kb.astype(MM)
            vb_ref[rows, :] = vb.astype(MM)
            for j in range(sub // BLOCK):
                blk = slice(j * BLOCK, (j + 1) * BLOCK)
                qtb_ref[r * (sub // BLOCK) + j] = qb[blk, :].T.astype(MM)
                ktb_ref[r * (sub // BLOCK) + j] = kb[blk, :].T.astype(MM)
                vtb_ref[r * (sub // BLOCK) + j] = vb[blk, :].T.astype(MM)

    tok = lambda w: pl.BlockSpec((tm, w), lambda i: (i, 0))
    tab = lambda w: pl.BlockSpec((tm, w), lambda i: (i % n_seq, 0))
    row = lambda w: pl.BlockSpec((1, w), lambda i: (0, 0))
    tposed = pl.BlockSpec((1, LANES, tm), lambda i: (i // n_seq, 0, i % n_seq))
    blocks = pl.BlockSpec((nblk, BLOCK, LANES), lambda i: (i, 0, 0))
    qblocks = pl.BlockSpec((nblk, Q_WIDTH, BLOCK), lambda i: (i, 0, 0))
    tok_mm = lambda w: jax.ShapeDtypeStruct((tokens, w), MM)
    return pl.pallas_call(
        body, name="pre_proj",
        grid=(tokens // tm,),
        in_specs=[tok(D_MODEL), row(D_MODEL),
                  pl.BlockSpec((N_CHIPS, D_MODEL, IN_CHUNK), lambda i: (0, 0, 0)),
                  row(Q_WIDTH), row(KV_WIDTH), tab(KV_WIDTH), tab(KV_WIDTH)],
        out_specs=[tok(D_MODEL), tok(QK_RAW), tok(Q_WIDTH), tok(KV_WIDTH), tposed, tok(KV_WIDTH), tposed,
                   qblocks, tok(KV_WIDTH), blocks, tok(KV_WIDTH), blocks],
        out_shape=[
            tok_mm(D_MODEL),
            jax.ShapeDtypeStruct((tokens, QK_RAW), F32),
            tok_mm(Q_WIDTH),
            tok_mm(KV_WIDTH),
            jax.ShapeDtypeStruct((batch, KV_WIDTH, seq), MM),
            tok_mm(KV_WIDTH),
            jax.ShapeDtypeStruct((batch, KV_WIDTH, seq), MM),
            jax.ShapeDtypeStruct((tokens // BLOCK, Q_WIDTH, BLOCK), MM),
            tok_mm(KV_WIDTH),
            jax.ShapeDtypeStruct((tokens // BLOCK, KV_WIDTH, BLOCK), MM),
            tok_mm(KV_WIDTH),
            jax.ShapeDtypeStruct((tokens // BLOCK, KV_WIDTH, BLOCK), MM),
        ],
        scratch_shapes=[pltpu.VMEM((tm, IN_TOTAL), F32)],
        compiler_params=_cparams(("parallel",), 48),
    )(*map(_from_hbm, (x, g1, w_in, gq, gk, ck, sk)))


def _kv_half(v2, kv):
    return jnp.where(kv == 0, v2[:, :HEAD_DIM], v2[:, HEAD_DIM:])


def _attn_a_fwd(qa, kta, va, *, seq, bq, jobs=()):
    tokens = qa.shape[0]
    batch = tokens // seq
    nq = seq // bq

    def body(q_ref, kt_ref, v_ref, o_ref, p_ref, linv_ref):
        kv = pl.program_id(1)
        kt = kt_ref[0]
        lane = lax.broadcasted_iota(jnp.int32, (seq, KV_WIDTH), 1)
        v = jnp.where((lane < HEAD_DIM) == (kv == 0), v_ref[...], jnp.ones((), MM))
        for g in range(GROUP):
            sl = slice(g * HEAD_DIM, (g + 1) * HEAD_DIM)
            s = _dot(q_ref[:, sl], kt)
            pb = jnp.exp((s - jnp.max(s, axis=-1, keepdims=True)).astype(MM))
            p_ref[0, g] = pb
            o2 = _dot(pb, v)
            linv = 1.0 / _kv_half(o2, 1 - kv)[:, 0:1]
            o_ref[:, sl] = _kv_half(o2, kv) * linv
            linv_ref[0, :, g:g + 1] = linv

    return _call(
        body, (qa, kta, va), name="attn_a_fwd", jobs=jobs,
        grid=(batch, N_KV, nq),
        in_specs=[pl.BlockSpec((bq, GROUP * HEAD_DIM), lambda b, k, i: (b * nq + i, k)),
                  pl.BlockSpec((1, HEAD_DIM, seq), lambda b, k, i: (b, k, 0)),
                  pl.BlockSpec((seq, KV_WIDTH), lambda b, k, i: (b, 0))],
        out_specs=[pl.BlockSpec((bq, GROUP * HEAD_DIM), lambda b, k, i: (b * nq + i, k)),
                   pl.BlockSpec((1, GROUP, bq, seq), lambda b, k, i: (k, 0, b * nq + i, 0)),
                   pl.BlockSpec((1, bq, GROUP), lambda b, k, i: (k, b * nq + i, 0))],
        out_shape=[jax.ShapeDtypeStruct((tokens, Q_WIDTH), F32),
                   jax.ShapeDtypeStruct((N_KV, GROUP, tokens, seq), MM),
                   jax.ShapeDtypeStruct((N_KV, tokens, GROUP), F32)],
        params=_cparams(("arbitrary", "arbitrary", "arbitrary"), 56))


def _attn_a_bwd(qa, ka, vta, do, o, p, linv, *, seq, bq, jobs=()):
    tokens = qa.shape[0]
    batch = tokens // seq
    nq = seq // bq

    def body(q_ref, k_ref, vt_ref, do_ref, o_ref, p_ref, linv_ref, dq_ref, dkt_ref, dvt_ref):
        kv = pl.program_id(1)

        @pl.when(pl.program_id(2) == 0)
        def _():
            dkt_ref[...] = jnp.zeros_like(dkt_ref)
            dvt_ref[...] = jnp.zeros_like(dvt_ref)

        vt = vt_ref[0]
        k2 = k_ref[...]
        for g in range(GROUP):
            sl = slice(g * HEAD_DIM, (g + 1) * HEAD_DIM)
            dof = do_ref[:, sl]
            delta = jnp.sum(dof * o_ref[:, sl], axis=-1, keepdims=True)
            linv_g = linv_ref[0, :, g:g + 1]
            pb = p_ref[0, g]
            dp = _dot(dof.astype(MM), vt)
            ds = pb * ((dp - delta) * linv_g).astype(MM)
            dq_ref[:, sl] = _kv_half(_dot(ds, k2), kv)
            dkt_ref[0] += _dot_tn(q_ref[:, sl], ds)
            dvt_ref[0] += _dot_tn((dof * linv_g).astype(MM), pb)

    qspec = pl.BlockSpec((bq, GROUP * HEAD_DIM), lambda b, k, i: (b * nq + i, k))
    tspec = pl.BlockSpec((1, HEAD_DIM, seq), lambda b, k, i: (b, k, 0))
    return _call(
        body, (qa, ka, vta, do, o, p, linv), name="attn_a_bwd", jobs=jobs,
        grid=(batch, N_KV, nq),
        in_specs=[qspec, pl.BlockSpec((seq, KV_WIDTH), lambda b, k, i: (b, 0)), tspec, qspec, qspec,
                  pl.BlockSpec((1, GROUP, bq, seq), lambda b, k, i: (k, 0, b * nq + i, 0)),
                  pl.BlockSpec((1, bq, GROUP), lambda b, k, i: (k, b * nq + i, 0))],
        out_specs=[qspec, tspec, tspec],
        out_shape=[jax.ShapeDtypeStruct((tokens, Q_WIDTH), F32),
                   jax.ShapeDtypeStruct((batch, KV_WIDTH, seq), F32),
                   jax.ShapeDtypeStruct((batch, KV_WIDTH, seq), F32)],
        params=_cparams(("arbitrary", "arbitrary", "arbitrary"), 56))


def _bias_build(rel_bias_t, bucket_t, band_t):
    def body(tab_ref, bucket_ref, band_ref, bias_ref):
        for h in range(GROUP * N_KV):
            for piece in range(3):
                bk = bucket_ref[piece]
                acc = jnp.zeros((BLOCK, BLOCK), F32)
                for b in range(N_BUCKETS):
                    acc = jnp.where(bk == b, tab_ref[h, b], acc)
                g = h % GROUP
                bias_ref[h // GROUP, piece, :, g * BLOCK:(g + 1) * BLOCK] = jnp.where(band_ref[piece] != 0, acc, NEG_INF)

    out = jax.ShapeDtypeStruct((N_KV, 3, BLOCK, GROUP * BLOCK), F32)
    return pl.pallas_call(
        body, name="bias_build", grid=(1,),
        in_specs=[SMEM, _whole(bucket_t), _whole(band_t)], out_specs=_whole(out), out_shape=out,
    )(rel_bias_t, bucket_t, band_t)


def _pad_heads(t, kv):
    outs = []
    for g in range(GROUP):
        tg = t[g * HEAD_DIM:(g + 1) * HEAD_DIM, :]
        zero = jnp.zeros_like(tg)
        outs.append(jnp.concatenate([jnp.where(kv == 0, tg, zero), jnp.where(kv == 0, zero, tg)], axis=0))
    return jnp.concatenate(outs, axis=-1)


def _unpad_heads(t, kv):
    outs = [_kv_half(t[:, g * BLOCK:(g + 1) * BLOCK].T, kv) for g in range(GROUP)]
    return jnp.concatenate(outs, axis=-1)


def _sink_row(sink_ref, kv):
    lane_head = lax.broadcasted_iota(jnp.int32, (1, GROUP * BLOCK), 1) // BLOCK
    row = jnp.zeros((1, GROUP * BLOCK), F32)
    for g in range(GROUP):
        row = jnp.where(lane_head == g, sink_ref[0, kv * GROUP + g], row)
    return row


def _window_scores_t(k_ref, idx, qpad, bias_ref, n, nblk):
    pieces = []
    for piece in range(3):
        s = _dot(k_ref[idx[piece]], qpad) + bias_ref[0, piece]
        if piece == 0:
            s = jnp.where(n > 0, s, NEG_INF)
        if piece == 2:
            s = jnp.where(n < nblk - 1, s, NEG_INF)
        pieces.append(s)
    return pieces


def _attn_b_fwd(qtb, kb3, vtb, bias, sink, *, seq, per_step, jobs=()):
    nblk_all = qtb.shape[0]
    tokens = nblk_all * BLOCK
    batch = tokens // seq
    nblk = seq // BLOCK
    nstep = nblk // per_step

    def body(sink_ref, q_ref, k_ref, vt_ref, bias_ref, o_ref, p_ref, stat_ref):
        kv = pl.program_id(0)
        first = pl.program_id(2) * per_step
        sink_row = _sink_row(sink_ref, kv)
        stat_row = lax.broadcasted_iota(jnp.int32, (8, GROUP * BLOCK), 0)

        def block(i, carry):
            n = first + i
            idx = (jnp.maximum(n - 1, 0), n, jnp.minimum(n + 1, nblk - 1))
            rows = pl.ds(pl.multiple_of(n * BLOCK, BLOCK), BLOCK)
            qpad = _pad_heads(q_ref[n], kv)
            ss = _window_scores_t(k_ref, idx, qpad, bias_ref, n, nblk)
            m = jnp.maximum(jnp.maximum(jnp.max(ss[0], axis=0, keepdims=True),
                                        jnp.max(ss[1], axis=0, keepdims=True)),
                            jnp.maximum(jnp.max(ss[2], axis=0, keepdims=True), sink_row))
            ps = [jnp.exp(s - m) for s in ss]
            e_sink = jnp.exp(sink_row - m)
            rinv = 1.0 / (jnp.sum(ps[0], axis=0, keepdims=True) + jnp.sum(ps[1], axis=0, keepdims=True)
                          + jnp.sum(ps[2], axis=0, keepdims=True) + e_sink)
            ot = jnp.zeros((KV_WIDTH, GROUP * BLOCK), F32)
            for piece in range(3):
                pb = ps[piece].astype(MM)
                p_ref[0, i, piece] = pb
                ot = ot + _dot(vt_ref[idx[piece]], pb)
            o_ref[rows, :] = _unpad_heads(ot * rinv, kv)
            stat_ref[0, i] = jnp.where(stat_row == 0, rinv, e_sink * rinv)
            return carry

        lax.fori_loop(0, per_step, block, 0, unroll=True)

    both = pl.BlockSpec((nblk, BLOCK, KV_WIDTH), lambda k, b, j: (b, 0, 0))
    return _call(
        body, (sink, qtb, kb3, vtb, bias), name="attn_b_fwd", jobs=jobs,
        grid=(N_KV, batch, nstep),
        in_specs=[SMEM, pl.BlockSpec((nblk, GROUP * HEAD_DIM, BLOCK), lambda k, b, j: (b, k, 0)), both, both,
                  pl.BlockSpec((1, 3, BLOCK, GROUP * BLOCK), lambda k, b, j: (k, 0, 0, 0))],
        out_specs=[pl.BlockSpec((seq, GROUP * HEAD_DIM), lambda k, b, j: (b, k)),
                   pl.BlockSpec((1, per_step, 3, BLOCK, GROUP * BLOCK), lambda k, b, j: (k, b * nstep + j, 0, 0, 0)),
                   pl.BlockSpec((1, per_step, 8, GROUP * BLOCK), lambda k, b, j: (k, b * nstep + j, 0, 0))],
        out_shape=[jax.ShapeDtypeStruct((tokens, Q_WIDTH), F32),
                   jax.ShapeDtypeStruct((N_KV, nblk_all, 3, BLOCK, GROUP * BLOCK), MM),
                   jax.ShapeDtypeStruct((N_KV, nblk_all, 8, GROUP * BLOCK), F32)],
        params=_cparams(("arbitrary", "arbitrary", "arbitrary"), 48))


def _attn_b_bwd(qtb, ktb, vb3, do, o, p, stat, *, seq, per_step, jobs=()):
    nblk_all = qtb.shape[0]
    tokens = nblk_all * BLOCK
    batch = tokens // seq
    nblk = seq // BLOCK
    nstep = nblk // per_step

    def body(q_ref, kt_ref, v_ref, do_ref, o_ref, p_ref, stat_ref,
             dq_ref, dk_ref, dv_ref, dbias_ref, dsink_ref):
        kv = pl.program_id(0)
        step = pl.program_id(2)
        first = step * per_step

        @pl.when(jnp.logical_and(pl.program_id(1) == 0, step == 0))
        def _():
            dbias_ref[...] = jnp.zeros_like(dbias_ref)
            dsink_ref[...] = jnp.zeros_like(dsink_ref)

        @pl.when(step == 0)
        def _():
            dk_ref[...] = jnp.zeros_like(dk_ref)
            dv_ref[...] = jnp.zeros_like(dv_ref)

        def block(i, dsink):
            n = first + i
            idx = (jnp.maximum(n - 1, 0), n, jnp.minimum(n + 1, nblk - 1))
            rows = pl.ds(pl.multiple_of(n * BLOCK, BLOCK), BLOCK)
            qpad = _pad_heads(q_ref[n], kv)
            dot_t = do_ref[rows, :].T
            prod = dot_t * o_ref[rows, :].T
            delta = jnp.concatenate(
                [jnp.sum(prod[g * HEAD_DIM:(g + 1) * HEAD_DIM, :], axis=0, keepdims=True) for g in range(GROUP)],
                axis=-1)
            stats = stat_ref[0, i]
            rinv, p_sink = stats[0:1, :], stats[1:2, :]
            dopad32 = _pad_heads(dot_t, kv)
            dopad = dopad32.astype(MM)
            dopad_n = (dopad32 * rinv).astype(MM)
            dqt = jnp.zeros((KV_WIDTH, GROUP * BLOCK), F32)
            for piece in range(3):
                pb = p_ref[0, i, piece]
                dst = pb.astype(F32) * ((_dot(v_ref[idx[piece]], dopad) - delta) * rinv)
                dsb = dst.astype(MM)
                dbias_ref[0, piece] += dst
                dqt = dqt + _dot(kt_ref[idx[piece]], dsb)
                dk_ref[0, idx[piece]] += _dot_nt(dsb, qpad)
                dv_ref[0, idx[piece]] += _dot_nt(pb, dopad_n)
            dq_ref[rows, :] = _unpad_heads(dqt, kv)
            return dsink - p_sink * delta

        dsink = lax.fori_loop(0, per_step, block, jnp.zeros((1, GROUP * BLOCK), F32), unroll=True)
        dsink_ref[0] += jnp.broadcast_to(dsink, (8, GROUP * BLOCK))

    qspec = pl.BlockSpec((seq, GROUP * HEAD_DIM), lambda k, b, j: (b, k))
    both = pl.BlockSpec((nblk, BLOCK, KV_WIDTH), lambda k, b, j: (b, 0, 0))
    grad = pl.BlockSpec((1, nblk, BLOCK, KV_WIDTH), lambda k, b, j: (k, b, 0, 0))
    return _call(
        body, (qtb, ktb, vb3, do, o, p, stat), name="attn_b_bwd", jobs=jobs,
        grid=(N_KV, batch, nstep),
        in_specs=[pl.BlockSpec((nblk, GROUP * HEAD_DIM, BLOCK), lambda k, b, j: (b, k, 0)), both, both,
                  qspec, qspec,
                  pl.BlockSpec((1, per_step, 3, BLOCK, GROUP * BLOCK), lambda k, b, j: (k, b * nstep + j, 0, 0, 0)),
                  pl.BlockSpec((1, per_step, 8, GROUP * BLOCK), lambda k, b, j: (k, b * nstep + j, 0, 0))],
        out_specs=[qspec, grad, grad,
                   pl.BlockSpec((1, 3, BLOCK, GROUP * BLOCK), lambda k, b, j: (k, 0, 0, 0)),
                   pl.BlockSpec((1, 8, GROUP * BLOCK), lambda k, b, j: (k, 0, 0))],
        out_shape=[jax.ShapeDtypeStruct((tokens, Q_WIDTH), F32),
                   jax.ShapeDtypeStruct((N_KV, nblk_all, BLOCK, KV_WIDTH), F32),
                   jax.ShapeDtypeStruct((N_KV, nblk_all, BLOCK, KV_WIDTH), F32),
                   jax.ShapeDtypeStruct((N_KV, 3, BLOCK, GROUP * BLOCK), F32),
                   jax.ShapeDtypeStruct((N_KV, 8, GROUP * BLOCK), F32)],
        params=_cparams(("arbitrary", "arbitrary", "arbitrary"), 48))


def _resident(shape):
    return pl.BlockSpec(shape, lambda i: (0,) * len(shape), pipeline_mode=pl.Buffered(1))


def _mix_ffn_fwd(oa, ob, w_o, x, g2, g3, w_up, w_down, target, g4, *, tm):
    tokens = x.shape[0]
    nt = tokens // tm

    def body(oa_ref, ob_ref, wo_ref, x_ref, g2_ref, g3_ref, wu_ref, wd_ref, t_ref, g4_ref,
             mix_ref, x1_ref, h2_ref, o_ref, u_ref, df_ref, dy_ref, loss_ref, dg4_ref):
        o = jnp.concatenate([oa_ref[...].astype(MM), ob_ref[...].astype(MM)], axis=-1)
        o_ref[...] = o
        mix = _dot(o, wo_ref[...])
        mix_ref[...] = mix
        x1 = x_ref[...] + mix * _rms_r(mix) * g2_ref[...]
        x1_ref[...] = x1
        h2v = (x1 * _rms_r(x1) * g3_ref[...]).astype(MM)
        h2_ref[...] = h2v
        f = jnp.zeros((tm, D_MODEL), F32)
        for c in range(N_CHIPS):
            u = jnp.maximum(_dot(h2v, wu_ref[c]), 0.0)
            u_ref[:, c * FF_CHUNK:(c + 1) * FF_CHUNK] = u.astype(MM)
            f = f + _dot((u * u).astype(MM), wd_ref[c * FF_CHUNK:(c + 1) * FF_CHUNK, :])
        r = _rms_r(f)
        g4v = g4_ref[...]
        err = x1 + f * r * g4v - t_ref[...]
        sq = jnp.sum(err * err, axis=-1, keepdims=True)
        loss_ref[0] = jnp.broadcast_to(jnp.sum(sq, axis=0, keepdims=True) * (0.5 / D_MODEL), (8, LANES))
        dy = err * (1.0 / D_MODEL)
        dy_ref[...] = dy
        dfv, dgv = _rms_bwd(f, r, g4v, dy)
        df_ref[...] = dfv.astype(MM)
        dg4_ref[0] = jnp.sum(dgv, axis=0, keepdims=True)

    tok = pl.BlockSpec((tm, D_MODEL), lambda i: (i, 0))
    half = pl.BlockSpec((tm, Q_WIDTH), lambda i: (i, 0))
    row = pl.BlockSpec((1, D_MODEL), lambda i: (0, 0))
    tok_f32 = jax.ShapeDtypeStruct((tokens, D_MODEL), F32)
    tok_mm = jax.ShapeDtypeStruct((tokens, D_MODEL), MM)
    return pl.pallas_call(
        body, name="mix_ffn_fwd",
        grid=(nt,),
        in_specs=[half, half, _resident((D_MODEL, D_MODEL)), tok, row, row,
                  _resident((N_CHIPS, D_MODEL, FF_CHUNK)), _resident((D_FF, D_MODEL)), tok, row],
        out_specs=[tok, tok, tok, tok, pl.BlockSpec((tm, D_FF), lambda i: (i, 0)), tok, tok,
                   pl.BlockSpec((1, 8, LANES), lambda i: (i, 0, 0)),
                   pl.BlockSpec((1, 1, D_MODEL), lambda i: (i, 0, 0))],
        out_shape=[tok_f32,
                   tok_f32,
                   tok_mm,
                   tok_mm,
                   jax.ShapeDtypeStruct((tokens, D_FF), MM),
                   tok_mm,
                   tok_f32,
                   jax.ShapeDtypeStruct((nt, 8, LANES), F32),
                   jax.ShapeDtypeStruct((nt, 1, D_MODEL), F32)],
        compiler_params=_cparams(("parallel",), 56),
    )(*map(_from_hbm, (oa, ob, w_o, x, g2, g3, w_up, w_down, target, g4)))


def _ffn_bwd_act(df, w_down, u, w_up, x1, dy, mix, g3, g2, w_o, *, tm):
    tokens = df.shape[0]
    nt = tokens // tm

    def body(df_ref, wd_ref, u_ref, wu_ref, x1_ref, dy_ref, mix_ref, g3_ref, g2_ref, wo_ref,
             dz_ref, dx1_ref, dmix_ref, dg3_ref, dg2_ref, doa_ref, dob_ref):
        dfv = df_ref[...]
        dh2 = jnp.zeros((tm, D_MODEL), F32)
        for c in range(N_CHIPS):
            cols = slice(c * FF_CHUNK, (c + 1) * FF_CHUNK)
            da = _dot_nt(dfv, wd_ref[cols, :])
            dz = (da * (2.0 * u_ref[:, cols].astype(F32))).astype(MM)
            dz_ref[:, cols] = dz
            dh2 = dh2 + _dot_nt(dz, wu_ref[c])
        x1 = x1_ref[...]
        dxn, dg3v = _rms_bwd(x1, _rms_r(x1), g3_ref[...], dh2)
        dx1 = dy_ref[...] + dxn
        dx1_ref[...] = dx1
        dg3_ref[0] = jnp.sum(dg3v, axis=0, keepdims=True)
        mix = mix_ref[...]
        dmix, dg2v = _rms_bwd(mix, _rms_r(mix), g2_ref[...], dx1)
        dmb = dmix.astype(MM)
        dmix_ref[...] = dmb
        dg2_ref[0] = jnp.sum(dg2v, axis=0, keepdims=True)
        doa_ref[...] = _dot_nt(dmb, wo_ref[0:Q_WIDTH, :])
        dob_ref[...] = _dot_nt(dmb, wo_ref[Q_WIDTH:D_MODEL, :])

    tok = pl.BlockSpec((tm, D_MODEL), lambda i: (i, 0))
    half = pl.BlockSpec((tm, Q_WIDTH), lambda i: (i, 0))
    wide = pl.BlockSpec((tm, D_FF), lambda i: (i, 0))
    row = pl.BlockSpec((1, D_MODEL), lambda i: (0, 0))
    part = pl.BlockSpec((1, 1, D_MODEL), lambda i: (i, 0, 0))
    return pl.pallas_call(
        body, name="ffn_bwd_act",
        grid=(nt,),
        in_specs=[tok, _resident((D_FF, D_MODEL)), wide, _resident((N_CHIPS, D_MODEL, FF_CHUNK)),
                  tok, tok, tok, row, row, _resident((D_MODEL, D_MODEL))],
        out_specs=[wide, tok, tok, part, part, half, half],
        out_shape=[jax.ShapeDtypeStruct((tokens, D_FF), MM),
                   jax.ShapeDtypeStruct((tokens, D_MODEL), F32),
                   jax.ShapeDtypeStruct((tokens, D_MODEL), MM),
                   jax.ShapeDtypeStruct((nt, 1, D_MODEL), F32),
                   jax.ShapeDtypeStruct((nt, 1, D_MODEL), F32),
                   jax.ShapeDtypeStruct((tokens, Q_WIDTH), F32),
                   jax.ShapeDtypeStruct((tokens, Q_WIDTH), F32)],
        compiler_params=_cparams(("parallel",), 56),
    )(*map(_from_hbm, (df, w_down, u, w_up, x1, dy, mix, g3, g2, w_o)))


def _tn_matmul(a, b, *, name, tm, tn, tk, chunk=None, square_a=False, vmem_mb=48, jobs=()):
    tokens, m_dim = a.shape
    n_dim = b.shape[1]
    chunked = chunk is not None
    if chunked:
        assert tm == m_dim and tn % chunk == 0

    def body(a_ref, b_ref, o_ref):
        av = a_ref[...]
        if square_a:
            av = av.astype(F32)
            av = av * av
        part = _dot_tn(av.astype(MM), b_ref[...].astype(MM))
        if chunked:
            part = jnp.stack([part[:, c * chunk:(c + 1) * chunk] for c in range(tn // chunk)])

        @pl.when(pl.program_id(2) == 0)
        def _():
            o_ref[...] = part

        @pl.when(pl.program_id(2) > 0)
        def _():
            o_ref[...] += part

    if chunked:
        out_spec = pl.BlockSpec((tn // chunk, tm, chunk), lambda i, j, k: (j, 0, 0))
        out_shape = jax.ShapeDtypeStruct((n_dim // chunk, m_dim, chunk), F32)
    else:
        out_spec = pl.BlockSpec((tm, tn), lambda i, j, k: (i, j))
        out_shape = jax.ShapeDtypeStruct((m_dim, n_dim), F32)
    (out,), job_res = _call(
        body, (a, b), name=name, jobs=jobs,
        grid=(m_dim // tm, n_dim // tn, tokens // tk),
        in_specs=[pl.BlockSpec((tk, tm), lambda i, j, k: (k, i)),
                  pl.BlockSpec((tk, tn), lambda i, j, k: (k, j))],
        out_specs=[out_spec], out_shape=[_in_hbm(out_shape)],
        params=_cparams(("arbitrary", "arbitrary", "arbitrary"), vmem_mb))
    return out, job_res


def _proj_bwd(dqa, dkta, dvta, dqb, dktb, dvtb, raw, x, dx1, g1, w_in, gq, gk, ck, sk, *, seq, tm, sub, jobs=()):
    tokens = x.shape[0]
    nt = tokens // tm
    n_seq = seq // tm
    nblk = tm // BLOCK

    def body(dqa_ref, dkta_ref, dvta_ref, dqb_ref, dkb_ref, dvb_ref, raw_ref, x_ref, dx1_ref, g1_ref, w_ref,
             gq_ref, gk_ref, ck_ref, sk_ref,
             gx_ref, dproj_ref, dg1_ref, dgq_ref, dgk_ref, dp):
        parts = []
        for r in range(tm // sub):
            rows = slice(r * sub, (r + 1) * sub)
            qa = raw_ref[rows, 0:Q_WIDTH]
            dqn = _rope_t(dqa_ref[rows, :], ck_ref[rows, :], sk_ref[rows, :]) * SCALE
            rq = _head_r(qa)
            nq = qa * rq
            dnq = dqn * gq_ref[...]
            dp[rows, 0:Q_WIDTH] = rq * (dnq - nq * (_seg64_sum(dnq * nq) * (1.0 / HEAD_DIM)))

            ka = raw_ref[rows, Q_WIDTH:QK_RAW]
            dkn = _rope_t(dkta_ref[0, :, rows].T, ck_ref[rows, :], sk_ref[rows, :])
            rk = _head_r(ka)
            nk = ka * rk
            dnk = dkn * gk_ref[...]
            dp[rows, 512:640] = rk * (dnk - nk * (_seg64_sum(dnk * nk) * (1.0 / HEAD_DIM)))

            dp[rows, 640:768] = dvta_ref[0, :, rows].T
            dp[rows, 768:1280] = dqb_ref[rows, :] * SCALE
            for j in range(r * sub // BLOCK, (r + 1) * sub // BLOCK):
                dp[j * BLOCK:(j + 1) * BLOCK, 1280:1408] = dkb_ref[0, j] + dkb_ref[1, j]
                dp[j * BLOCK:(j + 1) * BLOCK, 1408:1536] = dvb_ref[0, j] + dvb_ref[1, j]

            dproj = dp[rows, :].astype(MM)
            dproj_ref[rows, :] = dproj
            dh1 = _dot_nt(dproj[:, 0:IN_CHUNK], w_ref[0])
            for j in range(1, N_CHIPS):
                dh1 = dh1 + _dot_nt(dproj[:, j * IN_CHUNK:(j + 1) * IN_CHUNK], w_ref[j])
            xv = x_ref[rows, :]
            dxn, dg1v = _rms_bwd(xv, _rms_r(xv), g1_ref[...], dh1)
            gx_ref[rows, :] = dx1_ref[rows, :] + dxn
            parts.append((jnp.sum(dqn * nq, axis=0, keepdims=True), jnp.sum(dkn * nk, axis=0, keepdims=True),
                          jnp.sum(dg1v, axis=0, keepdims=True)))
        dgq_ref[0] = functools.reduce(jnp.add, [p[0] for p in parts])
        dgk_ref[0] = functools.reduce(jnp.add, [p[1] for p in parts])
        dg1_ref[0] = functools.reduce(jnp.add, [p[2] for p in parts])

    tok = lambda w: pl.BlockSpec((tm, w), lambda i: (i, 0))
    tab = lambda w: pl.BlockSpec((tm, w), lambda i: (i % n_seq, 0))
    row = lambda w: pl.BlockSpec((1, w), lambda i: (0, 0))
    tposed = pl.BlockSpec((1, KV_WIDTH, tm), lambda i: (i // n_seq, 0, i % n_seq))
    blocks = pl.BlockSpec((N_KV, nblk, BLOCK, KV_WIDTH), lambda i: (0, i, 0, 0))
    part = lambda w: pl.BlockSpec((1, 1, w), lambda i: (i, 0, 0))
    return _call(
        body, (dqa, dkta, dvta, dqb, dktb, dvtb, raw, x, dx1, g1, w_in, gq, gk, ck, sk),
        name="proj_bwd", jobs=jobs,
        grid=(nt,),
        in_specs=[tok(Q_WIDTH), tposed, tposed, tok(Q_WIDTH), blocks, blocks, tok(QK_RAW), tok(D_MODEL),
                  tok(D_MODEL), row(D_MODEL),
                  pl.BlockSpec((N_CHIPS, D_MODEL, IN_CHUNK), lambda i: (0, 0, 0)),
                  row(Q_WIDTH), row(KV_WIDTH), tab(KV_WIDTH), tab(KV_WIDTH)],
        out_specs=[tok(D_MODEL), tok(IN_TOTAL), part(D_MODEL), part(Q_WIDTH), part(KV_WIDTH)],
        out_shape=[jax.ShapeDtypeStruct((tokens, D_MODEL), F32),
                   jax.ShapeDtypeStruct((tokens, IN_TOTAL), MM),
                   jax.ShapeDtypeStruct((nt, 1, D_MODEL), F32),
                   jax.ShapeDtypeStruct((nt, 1, Q_WIDTH), F32),
                   jax.ShapeDtypeStruct((nt, 1, KV_WIDTH), F32)],
        scratch_shapes=[pltpu.VMEM((tm, IN_TOTAL), F32)],
        params=_cparams(("arbitrary",), 56))


def _pack_small(dg1, dg2, dg3, dg4, dgq, dgk, dsink, dbias, bucket, loss):
    def body(dg1_ref, dg2_ref, dg3_ref, dg4_ref, dgq_ref, dgk_ref, dsink_ref, dbias_ref, bucket_ref, loss_ref,
             out_ref, rel_ref):
        out_ref[...] = jnp.zeros_like(out_ref)
        for r, ref in ((ROW_G1, dg1_ref), (ROW_G2, dg2_ref), (ROW_G3, dg3_ref), (ROW_G4, dg4_ref)):
            acc = ref[0]
            for t in range(1, ref.shape[0]):
                acc = acc + ref[t]
            out_ref[r:r + 1, :] = acc

        def fold(ref, heads):
            acc = ref[0]
            for t in range(1, ref.shape[0]):
                acc = acc + ref[t]
            tot = acc[:, 0:HEAD_DIM]
            for h in range(1, heads):
                tot = tot + acc[:, h * HEAD_DIM:(h + 1) * HEAD_DIM]
            return tot

        out_ref[ROW_MISC:ROW_MISC + 1, MISC_GQ:MISC_GQ + HEAD_DIM] = fold(dgq_ref, GROUP * N_KV)
        out_ref[ROW_MISC:ROW_MISC + 1, MISC_GK:MISC_GK + HEAD_DIM] = fold(dgk_ref, N_KV)
        for h in range(GROUP * N_KV):
            g = h % GROUP
            out_ref[ROW_MISC:ROW_MISC + 1, MISC_SINK + h:MISC_SINK + h + 1] = jnp.sum(
                dsink_ref[h // GROUP, 0:1, g * BLOCK:(g + 1) * BLOCK], axis=-1, keepdims=True)
        lacc = loss_ref[0, 0:1, 0:1]
        for t in range(1, loss_ref.shape[0]):
            lacc = lacc + loss_ref[t, 0:1, 0:1]
        out_ref[ROW_MISC:ROW_MISC + 1, MISC_LOSS:MISC_LOSS + 1] = lacc
        lane = lax.broadcasted_iota(jnp.int32, (N_BUCKETS, LANES), 1)
        row = lax.broadcasted_iota(jnp.int32, (N_BUCKETS, LANES), 0)

        def per_bucket(b, acc):
            for h in range(GROUP * N_KV):
                g = h % GROUP
                sel = jnp.zeros((BLOCK, BLOCK), F32)
                for piece in range(3):
                    sel = sel + jnp.where(bucket_ref[piece] == b,
                                          dbias_ref[h // GROUP, piece, :, g * BLOCK:(g + 1) * BLOCK], 0.0)
                tot = jnp.sum(jnp.sum(sel, axis=0, keepdims=True), axis=-1, keepdims=True)
                acc = jnp.where((row == b) & (lane == h), tot, acc)
            return acc

        rel_ref[...] = lax.fori_loop(0, N_BUCKETS, per_bucket, jnp.zeros((N_BUCKETS, LANES), F32))

    args = (dg1, dg2, dg3, dg4, dgq, dgk, dsink, dbias, bucket, loss)
    outs = [jax.ShapeDtypeStruct((8, D_MODEL), F32), jax.ShapeDtypeStruct((N_BUCKETS, LANES), F32)]
    return pl.pallas_call(
        body, name="pack_small", grid=(1,),
        in_specs=[_whole(a) for a in args], out_specs=[_whole(o) for o in outs], out_shape=outs,
        compiler_params=pltpu.CompilerParams(vmem_limit_bytes=32 * 1024 * 1024),
    )(*map(_from_hbm, args))


def _gather_weights(shards, whole):
    n = len(shards)
    full = [t for t in range(n) if whole[t]]

    def body(*refs):
        ins, outs = refs[:n], refs[n:2 * n]
        raw, stage = refs[2 * n:3 * n], refs[3 * n:4 * n]
        load_sem, local_sem, ici_send, ici_recv, d2d_send, d2d_recv = refs[4 * n:]
        x, y, c = _place()
        k = 2 * x + y
        sibling = (x, y, 1 - c)
        order = full + [t for t in range(n) if t not in full]
        loads = {t: pltpu.make_async_copy(ins[t], raw[t], load_sem.at[t]) for t in order}
        for t in order:
            loads[t].start()
        copies, sends = [], []
        for t in order:
            loads[t].wait()
            stage[t][...] = raw[t][...].astype(MM)
            mine = pltpu.make_async_copy(stage[t], outs[t].at[k], local_sem.at[t])
            mine.start()
            copies.append(mine)
            if t in full:
                half = ins[t].shape[0] // 2
                rows = pl.ds(c * half, half)
                for r, (fx, fy) in enumerate(_CHIP_FLIPS):
                    cp = _remote(stage[t].at[rows], outs[t].at[k, rows], ici_send.at[t, r], ici_recv.at[t, r],
                                 (_flip(x, fx), _flip(y, fy), c))
                    cp.start()
                    sends.append(cp)
        for t in full:
            half = ins[t].shape[0] // 2
            rows = pl.ds(c * half, half)
            for r, (fx, fy) in enumerate(_CHIP_FLIPS):
                kk = 2 * _flip(x, fx) + _flip(y, fy)
                landed = outs[t].at[kk, rows]
                _remote(landed, landed, ici_send.at[t, r], ici_recv.at[t, r], sibling).wait_recv()
                fwd = _remote(landed, landed, d2d_send.at[t, r], d2d_recv.at[t, r], sibling)
                fwd.start()
                sends.append(fwd)
        for t in full:
            half = ins[t].shape[0] // 2
            other = pl.ds((1 - c) * half, half)
            for r, (fx, fy) in enumerate(_CHIP_FLIPS):
                kk = 2 * _flip(x, fx) + _flip(y, fy)
                theirs = outs[t].at[kk, other]
                _remote(theirs, theirs, d2d_send.at[t, r], d2d_recv.at[t, r], sibling).wait_recv()
        for cp in sends:
            cp.wait_send()
        for cp in copies:
            cp.wait()

    return pl.pallas_call(
        body, name="gather_weights",
        in_specs=[HBM] * n, out_specs=[HBM] * n,
        out_shape=[pltpu.HBM((N_CHIPS,) + s.shape, MM) for s in shards],
        scratch_shapes=[pltpu.VMEM(s.shape, F32) for s in shards] + [pltpu.VMEM(s.shape, MM) for s in shards] + [
            pltpu.SemaphoreType.DMA((n,)), pltpu.SemaphoreType.DMA((n,)),
            pltpu.SemaphoreType.DMA((n, 3)), pltpu.SemaphoreType.DMA((n, 3)),
            pltpu.SemaphoreType.DMA((n, 3)), pltpu.SemaphoreType.DMA((n, 3))],
        compiler_params=pltpu.CompilerParams(vmem_limit_bytes=40 * 1024 * 1024),
    )(*shards)


def _add_half(grad, got, where, *, name, tr):
    nch, half, cols = got.shape
    nblk = half // tr

    def body(where_ref, g_ref, r_ref, o_ref):
        o_ref[...] = (g_ref[...] + r_ref[...]).astype(MM)

    return pl.pallas_call(
        body, name=name,
        grid_spec=pltpu.PrefetchScalarGridSpec(
            num_scalar_prefetch=1, grid=(nch, nblk),
            in_specs=[pl.BlockSpec((1, tr, cols), lambda j, i, where_ref: (j, where_ref[1] * nblk + i, 0)),
                      pl.BlockSpec((1, tr, cols), lambda j, i, where_ref: (j, i, 0))],
            out_specs=pl.BlockSpec((1, tr, cols), lambda j, i, where_ref: (j, i, 0))),
        out_shape=jax.ShapeDtypeStruct(got.shape, MM),
        compiler_params=_cparams(("parallel", "parallel"), 32),
    )(where, grad, got)


def _add_chips(own, got, where, *, name, tr):
    _, half, cols = own.shape
    nblk = half // tr

    def body(where_ref, o_ref, g_ref, out_ref):
        f = lambda v: v.astype(F32)
        out_ref[...] = ((f(o_ref[0]) + f(g_ref[0])) + f(g_ref[1])) + f(g_ref[2])

    return pl.pallas_call(
        body, name=name,
        grid_spec=pltpu.PrefetchScalarGridSpec(
            num_scalar_prefetch=1, grid=(nblk,),
            in_specs=[pl.BlockSpec((1, tr, cols), lambda i, where_ref: (where_ref[0], i, 0)),
                      pl.BlockSpec((3, tr, cols), lambda i, where_ref: (0, i, 0))],
            out_specs=pl.BlockSpec((tr, cols), lambda i, where_ref: (where_ref[1] * nblk + i, 0))),
        out_shape=pltpu.HBM((2 * half, cols), F32),
        compiler_params=_cparams(("parallel",), 32),
    )(where, own, got)


def _small_job(tiles):
    n = len(tiles)

    def copies(ins, outs, sems):
        x, y, c = _place()
        me = 4 * x + 2 * y + c
        local, send, recv = sems
        cps = []
        for t in range(n):
            cps.append(pltpu.make_async_copy(ins[t], outs[t].at[me], local.at[t]))
            for r in range(1, N_DEV):
                fx, fy, fc = (r >> 2) & 1, (r >> 1) & 1, r & 1
                cps.append(_remote(ins[t], outs[t].at[me], send.at[t, r - 1], recv.at[t, r - 1],
                                   (_flip(x, fx), _flip(y, fy), _flip(c, fc))))
        return cps

    return _Job(tiles, [jax.ShapeDtypeStruct((N_DEV,) + t.shape, F32) for t in tiles],
                [pltpu.SemaphoreType.DMA((n,)), pltpu.SemaphoreType.DMA((n, N_DEV - 1)),
                 pltpu.SemaphoreType.DMA((n, N_DEV - 1))], copies)


def _adamw_math(w, g, m, v):
    m = ADAM_B1 * m + (1.0 - ADAM_B1) * g
    v = ADAM_B2 * v + (1.0 - ADAM_B2) * (g * g)
    m_hat = m / (1.0 - ADAM_B1 ** ADAM_STEP)
    v_hat = v / (1.0 - ADAM_B2 ** ADAM_STEP)
    delta = -ADAM_LR * (m_hat / (jnp.sqrt(v_hat) + ADAM_EPS) + ADAM_WD * w)
    return delta, m, v


def _adamw(w, g, m, v, *, name, tr):
    rows, cols = w.shape

    def body(w_ref, g_ref, m_ref, v_ref, go_ref, d_ref, nm_ref, nv_ref):
        g = g_ref[...]
        go_ref[...] = g
        d_ref[...], nm_ref[...], nv_ref[...] = _adamw_math(w_ref[...], g, m_ref[...], v_ref[...])

    spec = pl.BlockSpec((tr, cols), lambda i: (i, 0))
    return pl.pallas_call(
        body, name=name,
        grid=(rows // tr,),
        in_specs=[spec] * 4, out_specs=[spec] * 4,
        out_shape=[jax.ShapeDtypeStruct(w.shape, F32)] * 4,
        compiler_params=_cparams(("parallel",), 32),
    )(w, g, m, v)


def _small_adamw(gathered, gathered_rel, params, moments_m, moments_v):
    n = len(params)

    def body(all_ref, rel_all_ref, *refs):
        w_refs, m_refs, v_refs = refs[:n], refs[n:2 * n], refs[2 * n:3 * n]
        loss_ref = refs[3 * n]
        out_refs = refs[3 * n + 1:]
        g = all_ref[0]
        rel = rel_all_ref[0]
        for d in range(1, N_DEV):
            g = g + all_ref[d]
            rel = rel + rel_all_ref[d]
        misc = g[ROW_MISC:ROW_MISC + 1]
        loss_ref[...] = misc[:, MISC_LOSS:MISC_LOSS + 1]
        grads = (g[ROW_G1:ROW_G1 + 1], g[ROW_G2:ROW_G2 + 1], g[ROW_G3:ROW_G3 + 1], g[ROW_G4:ROW_G4 + 1],
                 misc[:, MISC_GQ:MISC_GQ + HEAD_DIM], misc[:, MISC_GK:MISC_GK + HEAD_DIM],
                 misc[:, MISC_SINK:MISC_SINK + GROUP * N_KV], rel[:, 0:GROUP * N_KV])
        for i in range(n):
            d, nm, nv = _adamw_math(w_refs[i][...], grads[i], m_refs[i][...], v_refs[i][...])
            for j, val in enumerate((grads[i], d, nm, nv)):
                out_refs[4 * i + j][...] = val

    args = (gathered, gathered_rel, *params, *moments_m, *moments_v)
    out_shape = [jax.ShapeDtypeStruct((1, 1), F32)] + [jax.ShapeDtypeStruct(p.shape, F32) for p in params
                                                       for _ in range(4)]
    outs = pl.pallas_call(
        body, name="small_adamw", grid=(1,),
        in_specs=[_whole(a) for a in args], out_specs=[_whole(o) for o in out_shape], out_shape=out_shape,
    )(*map(_from_hbm, args))
    return outs[0], [outs[1 + 4 * i:5 + 4 * i] for i in range(n)]


def kernel(x, w_in, w_o, g_pre_mix, g_post_mix, q_norm_a, k_norm_a, sink_b, rel_bias, g_pre_ffn, w_ffn_up, w_ffn_down, g_post_ffn, loss_target, m_w_in, m_w_o, m_g_pre_mix, m_g_post_mix, m_q_norm_a, m_k_norm_a, m_sink_b, m_rel_bias, m_g_pre_ffn, m_w_ffn_up, m_w_ffn_down, m_g_post_ffn, v_w_in, v_w_o, v_g_pre_mix, v_g_post_mix, v_q_norm_a, v_k_norm_a, v_sink_b, v_rel_bias, v_g_pre_ffn, v_w_ffn_up, v_w_ffn_down, v_g_post_ffn):
    batch, seq, _ = x.shape
    tokens = batch * seq
    where = jnp.stack([2 * lax.axis_index("x") + lax.axis_index("y"), lax.axis_index("c")]).astype(jnp.int32)
    x2 = x.reshape(tokens, D_MODEL)
    g1, g2, g3, g4 = g_pre_mix, g_post_mix, g_pre_ffn, g_post_ffn

    cos, sin = _rope_tables(seq)
    ck, sk = jnp.tile(cos, (1, 2)), jnp.tile(sin, (1, 2))
    gq8, gk2 = jnp.tile(q_norm_a, (1, 8)), jnp.tile(k_norm_a, (1, 2))
    bucket, band = _window_tables()
    bias = _bias_build(rel_bias.T, bucket, band)

    w_in_g, w_o_p, w_up_p, w_down_p = _gather_weights(
        (w_in[0], w_o[0], w_ffn_up[0], w_ffn_down[0]), whole=(True, False, False, False))
    (h1, raw, qa, ka, kta, va, vta, qtb, kb, ktb, vb, vtb) = _pre_proj(
        x2, g1, w_in_g, gq8, gk2, ck, sk, seq=seq, tm=min(512, seq), sub=256)
    (oa, p_a, linv_a), (w_part,) = _attn_a_fwd(
        qa, kta, va, seq=seq, bq=min(256, seq), jobs=[_gather_job([w_o_p, w_up_p, w_down_p], forward=False)])
    kb3 = kb.reshape(tokens // BLOCK, BLOCK, KV_WIDTH)
    vb3 = vb.reshape(tokens // BLOCK, BLOCK, KV_WIDTH)
    (ob, p_b, stat_b), ((w_o_g, w_up_g, w_down_g),) = _attn_b_fwd(
        qtb, kb3, vtb, bias, sink_b, seq=seq, per_step=min(8, seq // BLOCK),
        jobs=[_gather_job(w_part, forward=True)])
    w_o2 = w_o_g.reshape(D_MODEL, D_MODEL)
    w_down2 = w_down_g.reshape(D_FF, D_MODEL)
    mix, x1, h2, o_cat, u, df, dy, loss_t, dg4 = _mix_ffn_fwd(
        oa, ob, w_o2, x2, g2, g3, w_up_g, w_down2, loss_target.reshape(tokens, D_MODEL), g4, tm=256)

    dz, dx1, dmix, dg3, dg2, doa, dob = _ffn_bwd_act(df, w_down2, u, w_up_g, x1, dy, mix, g3, g2, w_o2, tm=256)
    gw_down, _ = _tn_matmul(u, df, name="grad_w_down", tm=1024, tn=1024, tk=min(2048, tokens), square_a=True)
    gw_down = gw_down.reshape(N_CHIPS, FF_CHUNK, D_MODEL)
    gw_up, ((got_down,),) = _tn_matmul(h2, dz, name="grad_w_up", tm=1024, tn=1024, tk=min(2048, tokens), chunk=FF_CHUNK,
                                        jobs=[_swap_job([gw_down])])
    gw_o, _ = _tn_matmul(o_cat, dmix, name="grad_w_o", tm=1024, tn=1024, tk=min(2048, tokens))
    gw_o = gw_o.reshape(N_CHIPS, O_CHUNK, D_MODEL)
    sum_down = _add_half(gw_down, got_down, where, name="add_half_w_down", tr=128)
    (dqa, dkta, dvta), ((ex_down,), (got_up,)) = _attn_a_bwd(
        qa, ka, vta, doa, oa, p_a, linv_a, seq=seq, bq=min(256, seq),
        jobs=[_exchange_job([sum_down]), _swap_job([gw_up])])
    full_down = _add_chips(sum_down, ex_down, where, name="add_chips_w_down", tr=128)
    sum_up = _add_half(gw_up, got_up, where, name="add_half_w_up", tr=128)
    (dqb, dkb, dvb, dbias, dsink), ((ex_up,), (g_down,), (got_o,)) = _attn_b_bwd(
        qtb, ktb, vb3, dob, ob, p_b, stat_b, seq=seq, per_step=min(4, seq // BLOCK),
        jobs=[_exchange_job([sum_up]), _join_job([full_down]), _swap_job([gw_o])])
    full_up = _add_chips(sum_up, ex_up, where, name="add_chips_w_up", tr=128)
    sum_o = _add_half(gw_o, got_o, where, name="add_half_w_o", tr=128)
    (grad_x, dproj, dg1, dgq, dgk), _ = _proj_bwd(
        dqa, dkta, dvta, dqb, dkb, dvb, raw, x2, dx1, g1, w_in_g, gq8, gk2, ck, sk,
        seq=seq, tm=min(512, seq), sub=128)
    packed, packed_rel = _pack_small(dg1, dg2, dg3, dg4, dgq, dgk, dsink, dbias, bucket, loss_t)
    gw_in, ((ex_o,), (g_up,), (gathered, gathered_rel)) = _tn_matmul(
        h1, dproj, name="grad_w_in", tm=1024, tn=2 * IN_CHUNK, tk=min(2048, tokens), chunk=IN_CHUNK,
        jobs=[_exchange_job([sum_o]), _join_job([full_up]), _small_job([packed, packed_rel])])
    full_o = _add_chips(sum_o, ex_o, where, name="add_chips_w_o", tr=128)

    (g_o,), (got_in,) = _run_jobs("tail_swap", [_join_job([full_o]), _swap_job([gw_in])])
    sum_in = _add_half(gw_in, got_in, where, name="add_half_w_in", tr=128)
    ((ex_in,),) = _run_jobs("tail_exchange", [_exchange_job([sum_in])])
    full_in = _add_chips(sum_in, ex_in, where, name="add_chips_w_in", tr=128)
    ((g_in,),) = _run_jobs("tail_join", [_join_job([full_in])])

    big = [[t[None] for t in _adamw(w[0], g, m[0], v[0], name="adamw_" + nm, tr=128)] for nm, w, g, m, v in (
        ("w_in", w_in, g_in, m_w_in, v_w_in), ("w_o", w_o, g_o, m_w_o, v_w_o),
        ("w_up", w_ffn_up, g_up, m_w_ffn_up, v_w_ffn_up), ("w_down", w_ffn_down, g_down, m_w_ffn_down, v_w_ffn_down))]

    loss, small = _small_adamw(
        gathered, gathered_rel,
        (g1, g2, g3, g4, q_norm_a, k_norm_a, sink_b, rel_bias),
        (m_g_pre_mix, m_g_post_mix, m_g_pre_ffn, m_g_post_ffn, m_q_norm_a, m_k_norm_a, m_sink_b, m_rel_bias),
        (v_g_pre_mix, v_g_post_mix, v_g_pre_ffn, v_g_post_ffn, v_q_norm_a, v_k_norm_a, v_sink_b, v_rel_bias))
    s_g1, s_g2, s_g3, s_g4, s_gq, s_gk, s_sink, s_rel = small

    def leaves(i):
        return (big[0][i], big[1][i], s_g1[i], s_g2[i], s_gq[i], s_gk[i], s_sink[i], s_rel[i], s_g3[i],
                big[2][i], big[3][i], s_g4[i])

    loss = loss.reshape(())
    return (loss, grad_x.reshape(batch, seq, D_MODEL), *leaves(0), *leaves(1), *leaves(2), *leaves(3))
```

```python
import functools

import jax
import jax.numpy as jnp
import numpy as np
from jax import lax
from jax.experimental import pallas as pl
from jax.experimental.pallas import tpu as pltpu

F32 = jnp.float32
MM = jnp.bfloat16

D_MODEL = 1024
HEAD_DIM = 64
N_KV = 2
GROUP = 4
Q_WIDTH = 512
KV_WIDTH = 128
D_FF = 4096
GRID_W = 64
BLOCK = 128
N_BUCKETS = 32
MAX_DISTANCE = 128
ROPE_THETA = 10000.0
EPS = 1e-6
NEG_INF = -1e30
SCALE = HEAD_DIM ** -0.5
IN_TOTAL = 1536
N_CHIPS = 4
N_DEV = 8
IN_CHUNK = IN_TOTAL // N_CHIPS
FF_CHUNK = D_FF // N_CHIPS
O_CHUNK = D_MODEL // N_CHIPS
QK_RAW = 640

ADAM_LR = 0.001
ADAM_B1 = 0.9
ADAM_B2 = 0.999
ADAM_EPS = 1e-08
ADAM_WD = 0.01
ADAM_STEP = 10

LANES = 128
MESH = pl.DeviceIdType.MESH
HBM = pl.BlockSpec(memory_space=pl.ANY)
SMEM = pl.BlockSpec(memory_space=pltpu.SMEM)

ROW_G1, ROW_G2, ROW_G3, ROW_G4, ROW_MISC = 0, 1, 2, 3, 4
MISC_GQ, MISC_GK, MISC_SINK, MISC_LOSS = 0, 64, 128, 512


def _cparams(sem, vmem_mb):
    return pltpu.CompilerParams(dimension_semantics=sem, vmem_limit_bytes=vmem_mb * 1024 * 1024)


def _whole(a):
    return pl.BlockSpec(a.shape, lambda i: (0,) * len(a.shape))


def _from_hbm(a):
    return pltpu.with_memory_space_constraint(a, pltpu.HBM)


def _in_hbm(s):
    return pltpu.HBM(s.shape, s.dtype)


class _Job:
    def __init__(self, operands, out_shapes, sems, copies, alias=None):
        self.operands, self.out_shapes, self.sems, self.copies = list(operands), list(out_shapes), list(sems), copies
        self.alias = dict(alias or {})


def _place():
    return lax.axis_index("x"), lax.axis_index("y"), lax.axis_index("c")


_CHIP_FLIPS = ((1, 0), (0, 1), (1, 1))


def _flip(v, bit):
    return 1 - v if bit else v


def _remote(src, dst, send, recv, dev):
    return pltpu.make_async_remote_copy(src_ref=src, dst_ref=dst, send_sem=send, recv_sem=recv,
                                        device_id=dev, device_id_type=MESH)


def _swap_job(grads):
    n = len(grads)

    def copies(ins, outs, sems):
        x, y, c = _place()
        send, recv = sems
        cps = []
        for t in range(n):
            half = ins[t].shape[1] // 2
            cps.append(_remote(ins[t].at[:, pl.ds((1 - c) * half, half), :], outs[t], send.at[t], recv.at[t],
                               (x, y, 1 - c)))
        return cps

    shapes = [jax.ShapeDtypeStruct((g.shape[0], g.shape[1] // 2, g.shape[2]), F32) for g in grads]
    return _Job(grads, shapes, [pltpu.SemaphoreType.DMA((n,)), pltpu.SemaphoreType.DMA((n,))], copies)


def _exchange_job(sums):
    n = len(sums)

    def copies(ins, outs, sems):
        x, y, c = _place()
        send, recv = sems
        cps = []
        for t in range(n):
            for r, (fx, fy) in enumerate(_CHIP_FLIPS):
                kk = 2 * _flip(x, fx) + _flip(y, fy)
                cps.append(_remote(ins[t].at[kk], outs[t].at[r], send.at[t, r], recv.at[t, r],
                                   (_flip(x, fx), _flip(y, fy), c)))
        return cps

    shapes = [jax.ShapeDtypeStruct((3,) + s.shape[1:], s.dtype) for s in sums]
    return _Job(sums, shapes, [pltpu.SemaphoreType.DMA((n, 3)), pltpu.SemaphoreType.DMA((n, 3))], copies)


def _join_job(fulls):
    n = len(fulls)

    def copies(ins, outs, sems):
        x, y, c = _place()
        send, recv = sems
        cps = []
        for t in range(n):
            half = ins[t].shape[0] // 2
            rows = pl.ds(c * half, half)
            cps.append(_remote(ins[t].at[rows], outs[t].at[rows], send.at[t], recv.at[t], (x, y, 1 - c)))
        return cps

    shapes = [jax.ShapeDtypeStruct(f.shape, f.dtype) for f in fulls]
    return _Job(fulls, shapes, [pltpu.SemaphoreType.DMA((n,)), pltpu.SemaphoreType.DMA((n,))], copies,
                alias={t: t for t in range(n)})


def _gather_job(bufs, forward):
    n = len(bufs)

    def copies(ins, outs, sems):
        x, y, c = _place()
        send, recv = sems
        cps = []
        for t in range(n):
            half = ins[t].shape[1] // 2
            rows = pl.ds(c * half, half)
            for r, (fx, fy) in enumerate(_CHIP_FLIPS):
                if forward:
                    kk = 2 * _flip(x, fx) + _flip(y, fy)
                    dev = (x, y, 1 - c)
                else:
                    kk = 2 * x + y
                    dev = (_flip(x, fx), _flip(y, fy), c)
                cps.append(_remote(ins[t].at[kk, rows], outs[t].at[kk, rows], send.at[t, r], recv.at[t, r], dev))
        return cps

    shapes = [jax.ShapeDtypeStruct(b.shape, b.dtype) for b in bufs]
    return _Job(bufs, shapes, [pltpu.SemaphoreType.DMA((n, 3)), pltpu.SemaphoreType.DMA((n, 3))], copies,
                alias={t: t for t in range(n)})


def _call(body, args, *, name, grid, in_specs, out_specs, out_shape, scratch_shapes=(), params=None, jobs=()):
    n_in, n_out, n_scr = len(in_specs), len(out_specs), len(scratch_shapes)
    job_in = [len(j.operands) for j in jobs]
    job_out = [len(j.out_shapes) for j in jobs]
    job_sem = [len(j.sems) for j in jobs]

    def wrapped(*refs):
        pos = 0
        ins = refs[pos:pos + n_in]; pos += n_in
        jins = []
        for k in job_in:
            jins.append(refs[pos:pos + k]); pos += k
        outs = refs[pos:pos + n_out]; pos += n_out
        jouts = []
        for k in job_out:
            jouts.append(refs[pos:pos + k]); pos += k
        scr = refs[pos:pos + n_scr]; pos += n_scr
        jsems = []
        for k in job_sem:
            jsems.append(refs[pos:pos + k]); pos += k
        if jobs:
            ids = [pl.program_id(d) for d in range(len(grid))]
            first = functools.reduce(jnp.logical_and, [i == 0 for i in ids])
            last = functools.reduce(jnp.logical_and, [i == g - 1 for i, g in zip(ids, grid)])

            @pl.when(first)
            def _():
                for j, ji, jo, js in zip(jobs, jins, jouts, jsems):
                    for cp in j.copies(ji, jo, js):
                        cp.start()

        body(*ins, *outs, *scr)
        if jobs:
            @pl.when(last)
            def _():
                for j, ji, jo, js in zip(jobs, jins, jouts, jsems):
                    for cp in j.copies(ji, jo, js):
                        cp.wait()

    aliases = {}
    in_pos, out_pos = n_in, n_out
    for j in jobs:
        for i, o in j.alias.items():
            aliases[in_pos + i] = out_pos + o
        in_pos += len(j.operands)
        out_pos += len(j.out_shapes)
    res = pl.pallas_call(
        wrapped, name=name, grid=grid,
        in_specs=list(in_specs) + [HBM] * sum(job_in),
        out_specs=list(out_specs) + [HBM] * sum(job_out),
        out_shape=list(out_shape) + [_in_hbm(s) for j in jobs for s in j.out_shapes],
        scratch_shapes=list(scratch_shapes) + [s for j in jobs for s in j.sems],
        input_output_aliases=aliases,
        compiler_params=params,
    )(*[a if spec is SMEM else _from_hbm(a) for a, spec in zip(args, in_specs)],
      *[a for j in jobs for a in j.operands])
    own, rest = list(res[:n_out]), list(res[n_out:])
    job_res = []
    for k in job_out:
        job_res.append(rest[:k])
        rest = rest[k:]
    return own, job_res


def _run_jobs(name, jobs):
    def body():
        pass

    return _call(body, (), name=name, grid=(1,), in_specs=[], out_specs=[], out_shape=[], jobs=jobs)[1]


def _dot(a, b):
    return jnp.dot(a, b, preferred_element_type=F32)


def _dot_nt(a, b):
    return lax.dot_general(a, b, (((1,), (1,)), ((), ())), preferred_element_type=F32)


def _dot_tn(a, b):
    return lax.dot_general(a, b, (((0,), (0,)), ((), ())), preferred_element_type=F32)


def _rms_r(x):
    return lax.rsqrt(jnp.mean(x * x, axis=-1, keepdims=True) + EPS)


def _rms_bwd(x, r, g, dy):
    n = x * r
    dn = dy * g
    dx = r * (dn - n * jnp.mean(dn * n, axis=-1, keepdims=True))
    return dx, dy * n


def _seg64_sum(v):
    rows, width = v.shape
    lane = lax.broadcasted_iota(jnp.int32, (rows, LANES), 1)
    lo = lane < HEAD_DIM
    outs = []
    for c in range(width // LANES):
        ch = v[:, c * LANES:(c + 1) * LANES]
        s_lo = jnp.sum(jnp.where(lo, ch, 0.0), axis=-1, keepdims=True)
        s_hi = jnp.sum(jnp.where(lo, 0.0, ch), axis=-1, keepdims=True)
        outs.append(jnp.where(lo, s_lo, s_hi))
    return outs[0] if len(outs) == 1 else jnp.concatenate(outs, axis=-1)


def _head_r(v):
    return lax.rsqrt(_seg64_sum(v * v) * (1.0 / HEAD_DIM) + EPS)


def _swap16(ch):
    lane = lax.broadcasted_iota(jnp.int32, ch.shape, 1)
    return jnp.where((lane % 32) < 16, pltpu.roll(ch, LANES - 16, 1), pltpu.roll(ch, 16, 1))


def _by_chunk(fn, v):
    outs = [fn(v[:, c * LANES:(c + 1) * LANES]) for c in range(v.shape[1] // LANES)]
    return outs[0] if len(outs) == 1 else jnp.concatenate(outs, axis=-1)


def _rope(v, cos, sin_signed):
    return _by_chunk(lambda ch: ch * cos + _swap16(ch) * sin_signed, v)


def _rope_t(g, cos, sin_signed):
    return _by_chunk(lambda ch: ch * cos + _swap16(ch * sin_signed), g)


def _rope_tables(seq):
    nf = HEAD_DIM // 4
    freqs = ROPE_THETA ** (-jnp.arange(nf, dtype=F32) / nf)
    pos = jnp.arange(seq, dtype=jnp.int32)
    row = (pos // GRID_W).astype(F32)
    col = (pos % GRID_W).astype(F32)
    ang_r = row[:, None] * freqs[None, :]
    ang_c = col[:, None] * freqs[None, :]
    cr, sr, cc, sc = jnp.cos(ang_r), jnp.sin(ang_r), jnp.cos(ang_c), jnp.sin(ang_c)
    cos = jnp.concatenate([cr, cr, cc, cc], axis=1)
    sin = jnp.concatenate([-sr, sr, -sc, sc], axis=1)
    return cos, sin


def _t5_bucket(rel):
    nb = N_BUCKETS // 2
    ret = (rel > 0).astype(jnp.int32) * nb
    n = jnp.abs(rel)
    max_exact = nb // 2
    nf = jnp.maximum(n, 1).astype(jnp.float32)
    large = max_exact + (jnp.log(nf / max_exact) / np.float32(np.log(MAX_DISTANCE / max_exact))
                         * (nb - max_exact)).astype(jnp.int32)
    large = jnp.minimum(large, nb - 1)
    return ret + jnp.where(n < max_exact, n, large)


def _window_tables():
    a = jnp.arange(BLOCK, dtype=jnp.int32)
    c = jnp.arange(3 * BLOCK, dtype=jnp.int32)
    rel = c[None, :] - BLOCK - a[:, None]
    bucket = _t5_bucket(rel)
    band = (jnp.abs(rel) <= BLOCK).astype(jnp.int32)
    to3 = lambda t: t.reshape(BLOCK, 3, BLOCK).transpose(1, 2, 0)
    return to3(bucket), to3(band)


def _pre_proj(x, g1, w_in, gq, gk, ck, sk, *, seq, tm, sub):
    tokens = x.shape[0]
    n_seq = seq // tm
    nblk = tm // BLOCK
    batch = tokens // seq

    def body(x_ref, g1_ref, w_ref, gq_ref, gk_ref, ck_ref, sk_ref,
             h1_ref, raw_ref, qa_ref, ka_ref, kta_ref, va_ref, vta_ref,
             qtb_ref, kb_ref, ktb_ref, vb_ref, vtb_ref, proj):
        for r in range(tm // sub):
            rows = slice(r * sub, (r + 1) * sub)
            xv = x_ref[rows, :]
            h = (xv * _rms_r(xv) * g1_ref[...]).astype(MM)
            h1_ref[rows, :] = h
            for j in range(N_CHIPS):
                proj[rows, j * IN_CHUNK:(j + 1) * IN_CHUNK] = _dot(h, w_ref[j])
            qa = proj[rows, 0:Q_WIDTH]
            ka = proj[rows, Q_WIDTH:QK_RAW]
            raw_ref[rows, :] = proj[rows, 0:QK_RAW]
            qn = qa * _head_r(qa) * gq_ref[...]
            qa_ref[rows, :] = (_rope(qn, ck_ref[rows, :], sk_ref[rows, :]) * SCALE).astype(MM)
            kn = ka * _head_r(ka) * gk_ref[...]
            kr = _rope(kn, ck_ref[rows, :], sk_ref[rows, :])
            ka_ref[rows, :] = kr.astype(MM)
            kta_ref[0, :, rows] = kr.T.astype(MM)
            va = proj[rows, 640:768]
            va_ref[rows, :] = va.astype(MM)
            vta_ref[0, :, rows] = va.T.astype(MM)
            qb = proj[rows, 768:1280] * SCALE
            kb = proj[rows, 1280:1408]
            vb = proj[rows, 1408:1536]
            kb_ref[rows, :] = kb.astype(MM)
            vb_ref[rows, :] = vb.astype(MM)
            for j in range(sub // BLOCK):
                blk = slice(j * BLOCK, (j + 1) * BLOCK)
                qtb_ref[r * (sub // BLOCK) + j] = qb[blk, :].T.astype(MM)
                ktb_ref[r * (sub // BLOCK) + j] = kb[blk, :].T.astype(MM)
                vtb_ref[r * (sub // BLOCK) + j] = vb[blk, :].T.astype(MM)

    tok = lambda w: pl.BlockSpec((tm, w), lambda i: (i, 0))
    tab = lambda w: pl.BlockSpec((tm, w), lambda i: (i % n_seq, 0))
    row = lambda w: pl.BlockSpec((1, w), lambda i: (0, 0))
    tposed = pl.BlockSpec((1, LANES, tm), lambda i: (i // n_seq, 0, i % n_seq))
    blocks = pl.BlockSpec((nblk, BLOCK, LANES), lambda i: (i, 0, 0))
    qblocks = pl.BlockSpec((nblk, Q_WIDTH, BLOCK), lambda i: (i, 0, 0))
    tok_mm = lambda w: jax.ShapeDtypeStruct((tokens, w), MM)
    return pl.pallas_call(
        body, name="pre_proj",
        grid=(tokens // tm,),
        in_specs=[tok(D_MODEL), row(D_MODEL),
                  pl.BlockSpec((N_CHIPS, D_MODEL, IN_CHUNK), lambda i: (0, 0, 0)),
                  row(Q_WIDTH), row(KV_WIDTH), tab(KV_WIDTH), tab(KV_WIDTH)],
        out_specs=[tok(D_MODEL), tok(QK_RAW), tok(Q_WIDTH), tok(KV_WIDTH), tposed, tok(KV_WIDTH), tposed,
                   qblocks, tok(KV_WIDTH), blocks, tok(KV_WIDTH), blocks],
        out_shape=[
            tok_mm(D_MODEL),
            jax.ShapeDtypeStruct((tokens, QK_RAW), F32),
            tok_mm(Q_WIDTH),
            tok_mm(KV_WIDTH),
            jax.ShapeDtypeStruct((batch, KV_WIDTH, seq), MM),
            tok_mm(KV_WIDTH),
            jax.ShapeDtypeStruct((batch, KV_WIDTH, seq), MM),
            jax.ShapeDtypeStruct((tokens // BLOCK, Q_WIDTH, BLOCK), MM),
            tok_mm(KV_WIDTH),
            jax.ShapeDtypeStruct((tokens // BLOCK, KV_WIDTH, BLOCK), MM),
            tok_mm(KV_WIDTH),
            jax.ShapeDtypeStruct((tokens // BLOCK, KV_WIDTH, BLOCK), MM),
        ],
        scratch_shapes=[pltpu.VMEM((tm, IN_TOTAL), F32)],
        compiler_params=_cparams(("parallel",), 48),
    )(*map(_from_hbm, (x, g1, w_in, gq, gk, ck, sk)))


def _kv_half(v2, kv):
    return jnp.where(kv == 0, v2[:, :HEAD_DIM], v2[:, HEAD_DIM:])


def _attn_a_fwd(qa, kta, va, *, seq, bq, jobs=()):
    tokens = qa.shape[0]
    batch = tokens // seq
    nq = seq // bq

    def body(q_ref, kt_ref, v_ref, o_ref, p_ref, linv_ref):
        kv = pl.program_id(1)
        kt = kt_ref[0]
        lane = lax.broadcasted_iota(jnp.int32, (seq, KV_WIDTH), 1)
        v = jnp.where((lane < HEAD_DIM) == (kv == 0), v_ref[...], jnp.ones((), MM))
        for g in range(GROUP):
            sl = slice(g * HEAD_DIM, (g + 1) * HEAD_DIM)
            s = _dot(q_ref[:, sl], kt)
            pb = jnp.exp((s - jnp.max(s, axis=-1, keepdims=True)).astype(MM))
            p_ref[0, g] = pb
            o2 = _dot(pb, v)
            linv = 1.0 / _kv_half(o2, 1 - kv)[:, 0:1]
            o_ref[:, sl] = _kv_half(o2, kv) * linv
            linv_ref[0, :, g:g + 1] = linv

    return _call(
        body, (qa, kta, va), name="attn_a_fwd", jobs=jobs,
        grid=(batch, N_KV, nq),
        in_specs=[pl.BlockSpec((bq, GROUP * HEAD_DIM), lambda b, k, i: (b * nq + i, k)),
                  pl.BlockSpec((1, HEAD_DIM, seq), lambda b, k, i: (b, k, 0)),
                  pl.BlockSpec((seq, KV_WIDTH), lambda b, k, i: (b, 0))],
        out_specs=[pl.BlockSpec((bq, GROUP * HEAD_DIM), lambda b, k, i: (b * nq + i, k)),
                   pl.BlockSpec((1, GROUP, bq, seq), lambda b, k, i: (k, 0, b * nq + i, 0)),
                   pl.BlockSpec((1, bq, GROUP), lambda b, k, i: (k, b * nq + i, 0))],
        out_shape=[jax.ShapeDtypeStruct((tokens, Q_WIDTH), F32),
                   jax.ShapeDtypeStruct((N_KV, GROUP, tokens, seq), MM),
                   jax.ShapeDtypeStruct((N_KV, tokens, GROUP), F32)],
        params=_cparams(("arbitrary", "arbitrary", "arbitrary"), 56))


def _attn_a_bwd(qa, ka, vta, do, o, p, linv, *, seq, bq, jobs=()):
    tokens = qa.shape[0]
    batch = tokens // seq
    nq = seq // bq

    def body(q_ref, k_ref, vt_ref, do_ref, o_ref, p_ref, linv_ref, dq_ref, dkt_ref, dvt_ref):
        kv = pl.program_id(1)

        @pl.when(pl.program_id(2) == 0)
        def _():
            dkt_ref[...] = jnp.zeros_like(dkt_ref)
            dvt_ref[...] = jnp.zeros_like(dvt_ref)

        vt = vt_ref[0]
        k2 = k_ref[...]
        for g in range(GROUP):
            sl = slice(g * HEAD_DIM, (g + 1) * HEAD_DIM)
            dof = do_ref[:, sl]
            delta = jnp.sum(dof * o_ref[:, sl], axis=-1, keepdims=True)
            linv_g = linv_ref[0, :, g:g + 1]
            pb = p_ref[0, g]
            dp = _dot(dof.astype(MM), vt)
            ds = pb * ((dp - delta) * linv_g).astype(MM)
            dq_ref[:, sl] = _kv_half(_dot(ds, k2), kv)
            dkt_ref[0] += _dot_tn(q_ref[:, sl], ds)
            dvt_ref[0] += _dot_tn((dof * linv_g).astype(MM), pb)

    qspec = pl.BlockSpec((bq, GROUP * HEAD_DIM), lambda b, k, i: (b * nq + i, k))
    tspec = pl.BlockSpec((1, HEAD_DIM, seq), lambda b, k, i: (b, k, 0))
    return _call(
        body, (qa, ka, vta, do, o, p, linv), name="attn_a_bwd", jobs=jobs,
        grid=(batch, N_KV, nq),
        in_specs=[qspec, pl.BlockSpec((seq, KV_WIDTH), lambda b, k, i: (b, 0)), tspec, qspec, qspec,
                  pl.BlockSpec((1, GROUP, bq, seq), lambda b, k, i: (k, 0, b * nq + i, 0)),
                  pl.BlockSpec((1, bq, GROUP), lambda b, k, i: (k, b * nq + i, 0))],
        out_specs=[qspec, tspec, tspec],
        out_shape=[jax.ShapeDtypeStruct((tokens, Q_WIDTH), F32),
                   jax.ShapeDtypeStruct((batch, KV_WIDTH, seq), F32),
                   jax.ShapeDtypeStruct((batch, KV_WIDTH, seq), F32)],
        params=_cparams(("arbitrary", "arbitrary", "arbitrary"), 56))


def _bias_build(rel_bias_t, bucket_t, band_t):
    def body(tab_ref, bucket_ref, band_ref, bias_ref):
        for h in range(GROUP * N_KV):
            for piece in range(3):
                bk = bucket_ref[piece]
                acc = jnp.zeros((BLOCK, BLOCK), F32)
                for b in range(N_BUCKETS):
                    acc = jnp.where(bk == b, tab_ref[h, b], acc)
                g = h % GROUP
                bias_ref[h // GROUP, piece, :, g * BLOCK:(g + 1) * BLOCK] = jnp.where(band_ref[piece] != 0, acc, NEG_INF)

    out = jax.ShapeDtypeStruct((N_KV, 3, BLOCK, GROUP * BLOCK), F32)
    return pl.pallas_call(
        body, name="bias_build", grid=(1,),
        in_specs=[SMEM, _whole(bucket_t), _whole(band_t)], out_specs=_whole(out), out_shape=out,
    )(rel_bias_t, bucket_t, band_t)


def _pad_heads(t, kv):
    outs = []
    for g in range(GROUP):
        tg = t[g * HEAD_DIM:(g + 1) * HEAD_DIM, :]
        zero = jnp.zeros_like(tg)
        outs.append(jnp.concatenate([jnp.where(kv == 0, tg, zero), jnp.where(kv == 0, zero, tg)], axis=0))
    return jnp.concatenate(outs, axis=-1)


def _unpad_heads(t, kv):
    outs = [_kv_half(t[:, g * BLOCK:(g + 1) * BLOCK].T, kv) for g in range(GROUP)]
    return jnp.concatenate(outs, axis=-1)


def _sink_row(sink_ref, kv):
    lane_head = lax.broadcasted_iota(jnp.int32, (1, GROUP * BLOCK), 1) // BLOCK
    row = jnp.zeros((1, GROUP * BLOCK), F32)
    for g in range(GROUP):
        row = jnp.where(lane_head == g, sink_ref[0, kv * GROUP + g], row)
    return row


def _window_scores_t(k_ref, idx, qpad, bias_ref, n, nblk):
    pieces = []
    for piece in range(3):
        s = _dot(k_ref[idx[piece]], qpad) + bias_ref[0, piece]
        if piece == 0:
            s = jnp.where(n > 0, s, NEG_INF)
        if piece == 2:
            s = jnp.where(n < nblk - 1, s, NEG_INF)
        pieces.append(s)
    return pieces


def _attn_b_fwd(qtb, kb3, vtb, bias, sink, *, seq, per_step, jobs=()):
    nblk_all = qtb.shape[0]
    tokens = nblk_all * BLOCK
    batch = tokens // seq
    nblk = seq // BLOCK
    nstep = nblk // per_step

    def body(sink_ref, q_ref, k_ref, vt_ref, bias_ref, o_ref, p_ref, stat_ref):
        kv = pl.program_id(0)
        first = pl.program_id(2) * per_step
        sink_row = _sink_row(sink_ref, kv)
        stat_row = lax.broadcasted_iota(jnp.int32, (8, GROUP * BLOCK), 0)

        def block(i, carry):
            n = first + i
            idx = (jnp.maximum(n - 1, 0), n, jnp.minimum(n + 1, nblk - 1))
            rows = pl.ds(pl.multiple_of(n * BLOCK, BLOCK), BLOCK)
            qpad = _pad_heads(q_ref[n], kv)
            ss = _window_scores_t(k_ref, idx, qpad, bias_ref, n, nblk)
            m = jnp.maximum(jnp.maximum(jnp.max(ss[0], axis=0, keepdims=True),
                                        jnp.max(ss[1], axis=0, keepdims=True)),
                            jnp.maximum(jnp.max(ss[2], axis=0, keepdims=True), sink_row))
            ps = [jnp.exp(s - m) for s in ss]
            e_sink = jnp.exp(sink_row - m)
            rinv = 1.0 / (jnp.sum(ps[0], axis=0, keepdims=True) + jnp.sum(ps[1], axis=0, keepdims=True)
                          + jnp.sum(ps[2], axis=0, keepdims=True) + e_sink)
            ot = jnp.zeros((KV_WIDTH, GROUP * BLOCK), F32)
            for piece in range(3):
                pb = ps[piece].astype(MM)
                p_ref[0, i, piece] = pb
                ot = ot + _dot(vt_ref[idx[piece]], pb)
            o_ref[rows, :] = _unpad_heads(ot * rinv, kv)
            stat_ref[0, i] = jnp.where(stat_row == 0, rinv, e_sink * rinv)
            return carry

        lax.fori_loop(0, per_step, block, 0, unroll=True)

    both = pl.BlockSpec((nblk, BLOCK, KV_WIDTH), lambda k, b, j: (b, 0, 0))
    return _call(
        body, (sink, qtb, kb3, vtb, bias), name="attn_b_fwd", jobs=jobs,
        grid=(N_KV, batch, nstep),
        in_specs=[SMEM, pl.BlockSpec((nblk, GROUP * HEAD_DIM, BLOCK), lambda k, b, j: (b, k, 0)), both, both,
                  pl.BlockSpec((1, 3, BLOCK, GROUP * BLOCK), lambda k, b, j: (k, 0, 0, 0))],
        out_specs=[pl.BlockSpec((seq, GROUP * HEAD_DIM), lambda k, b, j: (b, k)),
                   pl.BlockSpec((1, per_step, 3, BLOCK, GROUP * BLOCK), lambda k, b, j: (k, b * nstep + j, 0, 0, 0)),
                   pl.BlockSpec((1, per_step, 8, GROUP * BLOCK), lambda k, b, j: (k, b * nstep + j, 0, 0))],
        out_shape=[jax.ShapeDtypeStruct((tokens, Q_WIDTH), F32),
                   jax.ShapeDtypeStruct((N_KV, nblk_all, 3, BLOCK, GROUP * BLOCK), MM),
                   jax.ShapeDtypeStruct((N_KV, nblk_all, 8, GROUP * BLOCK), F32)],
        params=_cparams(("arbitrary", "arbitrary", "arbitrary"), 48))


def _attn_b_bwd(qtb, ktb, vb3, do, o, p, stat, *, seq, per_step, jobs=()):
    nblk_all = qtb.shape[0]
    tokens = nblk_all * BLOCK
    batch = tokens // seq
    nblk = seq // BLOCK
    nstep = nblk // per_step

    def body(q_ref, kt_ref, v_ref, do_ref, o_ref, p_ref, stat_ref,
             dq_ref, dk_ref, dv_ref, dbias_ref, dsink_ref):
        kv = pl.program_id(0)
        step = pl.program_id(2)
        first = step * per_step

        @pl.when(jnp.logical_and(pl.program_id(1) == 0, step == 0))
        def _():
            dbias_ref[...] = jnp.zeros_like(dbias_ref)
            dsink_ref[...] = jnp.zeros_like(dsink_ref)

        @pl.when(step == 0)
        def _():
            dk_ref[...] = jnp.zeros_like(dk_ref)
            dv_ref[...] = jnp.zeros_like(dv_ref)

        def block(i, dsink):
            n = first + i
            idx = (jnp.maximum(n - 1, 0), n, jnp.minimum(n + 1, nblk - 1))
            rows = pl.ds(pl.multiple_of(n * BLOCK, BLOCK), BLOCK)
            qpad = _pad_heads(q_ref[n], kv)
            dot_t = do_ref[rows, :].T
            prod = dot_t * o_ref[rows, :].T
            delta = jnp.concatenate(
                [jnp.sum(prod[g * HEAD_DIM:(g + 1) * HEAD_DIM, :], axis=0, keepdims=True) for g in range(GROUP)],
                axis=-1)
            stats = stat_ref[0, i]
            rinv, p_sink = stats[0:1, :], stats[1:2, :]
            dopad32 = _pad_heads(dot_t, kv)
            dopad = dopad32.astype(MM)
            dopad_n = (dopad32 * rinv).astype(MM)
            dqt = jnp.zeros((KV_WIDTH, GROUP * BLOCK), F32)
            for piece in range(3):
                pb = p_ref[0, i, piece]
                dst = pb.astype(F32) * ((_dot(v_ref[idx[piece]], dopad) - delta) * rinv)
                dsb = dst.astype(MM)
                dbias_ref[0, piece] += dst
                dqt = dqt + _dot(kt_ref[idx[piece]], dsb)
                dk_ref[0, idx[piece]] += _dot_nt(dsb, qpad)
                dv_ref[0, idx[piece]] += _dot_nt(pb, dopad_n)
            dq_ref[rows, :] = _unpad_heads(dqt, kv)
            return dsink - p_sink * delta

        dsink = lax.fori_loop(0, per_step, block, jnp.zeros((1, GROUP * BLOCK), F32), unroll=True)
        dsink_ref[0] += jnp.broadcast_to(dsink, (8, GROUP * BLOCK))

    qspec = pl.BlockSpec((seq, GROUP * HEAD_DIM), lambda k, b, j: (b, k))
    both = pl.BlockSpec((nblk, BLOCK, KV_WIDTH), lambda k, b, j: (b, 0, 0))
    grad = pl.BlockSpec((1, nblk, BLOCK, KV_WIDTH), lambda k, b, j: (k, b, 0, 0))
    return _call(
        body, (qtb, ktb, vb3, do, o, p, stat), name="attn_b_bwd", jobs=jobs,
        grid=(N_KV, batch, nstep),
        in_specs=[pl.BlockSpec((nblk, GROUP * HEAD_DIM, BLOCK), lambda k, b, j: (b, k, 0)), both, both,
                  qspec, qspec,
                  pl.BlockSpec((1, per_step, 3, BLOCK, GROUP * BLOCK), lambda k, b, j: (k, b * nstep + j, 0, 0, 0)),
                  pl.BlockSpec((1, per_step, 8, GROUP * BLOCK), lambda k, b, j: (k, b * nstep + j, 0, 0))],
        out_specs=[qspec, grad, grad,
                   pl.BlockSpec((1, 3, BLOCK, GROUP * BLOCK), lambda k, b, j: (k, 0, 0, 0)),
                   pl.BlockSpec((1, 8, GROUP * BLOCK), lambda k, b, j: (k, 0, 0))],
        out_shape=[jax.ShapeDtypeStruct((tokens, Q_WIDTH), F32),
                   jax.ShapeDtypeStruct((N_KV, nblk_all, BLOCK, KV_WIDTH), F32),
                   jax.ShapeDtypeStruct((N_KV, nblk_all, BLOCK, KV_WIDTH), F32),
                   jax.ShapeDtypeStruct((N_KV, 3, BLOCK, GROUP * BLOCK), F32),
                   jax.ShapeDtypeStruct((N_KV, 8, GROUP * BLOCK), F32)],
        params=_cparams(("arbitrary", "arbitrary", "arbitrary"), 56))


def _resident(shape):
    return pl.BlockSpec(shape, lambda i: (0,) * len(shape), pipeline_mode=pl.Buffered(1))


def _mix_ffn_fwd(oa, ob, w_o, x, g2, g3, w_up, w_down, target, g4, *, tm):
    tokens = x.shape[0]
    nt = tokens // tm

    def body(oa_ref, ob_ref, wo_ref, x_ref, g2_ref, g3_ref, wu_ref, wd_ref, t_ref, g4_ref,
             mix_ref, x1_ref, h2_ref, o_ref, u_ref, df_ref, dy_ref, loss_ref, dg4_ref):
        o = jnp.concatenate([oa_ref[...].astype(MM), ob_ref[...].astype(MM)], axis=-1)
        o_ref[...] = o
        mix = _dot(o, wo_ref[...])
        mix_ref[...] = mix
        x1 = x_ref[...] + mix * _rms_r(mix) * g2_ref[...]
        x1_ref[...] = x1
        h2v = (x1 * _rms_r(x1) * g3_ref[...]).astype(MM)
        h2_ref[...] = h2v
        f = jnp.zeros((tm, D_MODEL), F32)
        for c in range(N_CHIPS):
            u = jnp.maximum(_dot(h2v, wu_ref[c]), 0.0)
            u_ref[:, c * FF_CHUNK:(c + 1) * FF_CHUNK] = u.astype(MM)
            f = f + _dot((u * u).astype(MM), wd_ref[c * FF_CHUNK:(c + 1) * FF_CHUNK, :])
        r = _rms_r(f)
        g4v = g4_ref[...]
        err = x1 + f * r * g4v - t_ref[...]
        sq = jnp.sum(err * err, axis=-1, keepdims=True)
        loss_ref[0] = jnp.broadcast_to(jnp.sum(sq, axis=0, keepdims=True) * (0.5 / D_MODEL), (8, LANES))
        dy = err * (1.0 / D_MODEL)
        dy_ref[...] = dy
        dfv, dgv = _rms_bwd(f, r, g4v, dy)
        df_ref[...] = dfv.astype(MM)
        dg4_ref[0] = jnp.sum(dgv, axis=0, keepdims=True)

    tok = pl.BlockSpec((tm, D_MODEL), lambda i: (i, 0))
    half = pl.BlockSpec((tm, Q_WIDTH), lambda i: (i, 0))
    row = pl.BlockSpec((1, D_MODEL), lambda i: (0, 0))
    tok_f32 = jax.ShapeDtypeStruct((tokens, D_MODEL), F32)
    tok_mm = jax.ShapeDtypeStruct((tokens, D_MODEL), MM)
    return pl.pallas_call(
        body, name="mix_ffn_fwd",
        grid=(nt,),
        in_specs=[half, half, _resident((D_MODEL, D_MODEL)), tok, row, row,
                  _resident((N_CHIPS, D_MODEL, FF_CHUNK)), _resident((D_FF, D_MODEL)), tok, row],
        out_specs=[tok, tok, tok, tok, pl.BlockSpec((tm, D_FF), lambda i: (i, 0)), tok, tok,
                   pl.BlockSpec((1, 8, LANES), lambda i: (i, 0, 0)),
                   pl.BlockSpec((1, 1, D_MODEL), lambda i: (i, 0, 0))],
        out_shape=[tok_f32,
                   tok_f32,
                   tok_mm,
                   tok_mm,
                   jax.ShapeDtypeStruct((tokens, D_FF), MM),
                   tok_mm,
                   tok_f32,
                   jax.ShapeDtypeStruct((nt, 8, LANES), F32),
                   jax.ShapeDtypeStruct((nt, 1, D_MODEL), F32)],
        compiler_params=_cparams(("parallel",), 56),
    )(*map(_from_hbm, (oa, ob, w_o, x, g2, g3, w_up, w_down, target, g4)))


def _ffn_bwd_act(df, w_down, u, w_up, x1, dy, mix, g3, g2, w_o, *, tm):
    tokens = df.shape[0]
    nt = tokens // tm

    def body(df_ref, wd_ref, u_ref, wu_ref, x1_ref, dy_ref, mix_ref, g3_ref, g2_ref, wo_ref,
             dz_ref, dx1_ref, dmix_ref, dg3_ref, dg2_ref, doa_ref, dob_ref):
        dfv = df_ref[...]
        dh2 = jnp.zeros((tm, D_MODEL), F32)
        for c in range(N_CHIPS):
            cols = slice(c * FF_CHUNK, (c + 1) * FF_CHUNK)
            da = _dot_nt(dfv, wd_ref[cols, :])
            dz = (da * (2.0 * u_ref[:, cols].astype(F32))).astype(MM)
            dz_ref[:, cols] = dz
            dh2 = dh2 + _dot_nt(dz, wu_ref[c])
        x1 = x1_ref[...]
        dxn, dg3v = _rms_bwd(x1, _rms_r(x1), g3_ref[...], dh2)
        dx1 = dy_ref[...] + dxn
        dx1_ref[...] = dx1
        dg3_ref[0] = jnp.sum(dg3v, axis=0, keepdims=True)
        mix = mix_ref[...]
        dmix, dg2v = _rms_bwd(mix, _rms_r(mix), g2_ref[...], dx1)
        dmb = dmix.astype(MM)
        dmix_ref[...] = dmb
        dg2_ref[0] = jnp.sum(dg2v, axis=0, keepdims=True)
        doa_ref[...] = _dot_nt(dmb, wo_ref[0:Q_WIDTH, :])
        dob_ref[...] = _dot_nt(dmb, wo_ref[Q_WIDTH:D_MODEL, :])

    tok = pl.BlockSpec((tm, D_MODEL), lambda i: (i, 0))
    half = pl.BlockSpec((tm, Q_WIDTH), lambda i: (i, 0))
    wide = pl.BlockSpec((tm, D_FF), lambda i: (i, 0))
    row = pl.BlockSpec((1, D_MODEL), lambda i: (0, 0))
    part = pl.BlockSpec((1, 1, D_MODEL), lambda i: (i, 0, 0))
    return pl.pallas_call(
        body, name="ffn_bwd_act",
        grid=(nt,),
        in_specs=[tok, _resident((D_FF, D_MODEL)), wide, _resident((N_CHIPS, D_MODEL, FF_CHUNK)),
                  tok, tok, tok, row, row, _resident((D_MODEL, D_MODEL))],
        out_specs=[wide, tok, tok, part, part, half, half],
        out_shape=[jax.ShapeDtypeStruct((tokens, D_FF), MM),
                   jax.ShapeDtypeStruct((tokens, D_MODEL), F32),
                   jax.ShapeDtypeStruct((tokens, D_MODEL), MM),
                   jax.ShapeDtypeStruct((nt, 1, D_MODEL), F32),
                   jax.ShapeDtypeStruct((nt, 1, D_MODEL), F32),
                   jax.ShapeDtypeStruct((tokens, Q_WIDTH), F32),
                   jax.ShapeDtypeStruct((tokens, Q_WIDTH), F32)],
        compiler_params=_cparams(("parallel",), 56),
    )(*map(_from_hbm, (df, w_down, u, w_up, x1, dy, mix, g3, g2, w_o)))


def _tn_matmul(a, b, *, name, tm, tn, tk, chunk=None, square_a=False, vmem_mb=48, jobs=()):
    tokens, m_dim = a.shape
    n_dim = b.shape[1]
    chunked = chunk is not None
    if chunked:
        assert tm == m_dim and tn % chunk == 0

    def body(a_ref, b_ref, o_ref):
        av = a_ref[...]
        if square_a:
            av = av.astype(F32)
            av = av * av
        part = _dot_tn(av.astype(MM), b_ref[...].astype(MM))
        if chunked:
            part = jnp.stack([part[:, c * chunk:(c + 1) * chunk] for c in range(tn // chunk)])

        @pl.when(pl.program_id(2) == 0)
        def _():
            o_ref[...] = part

        @pl.when(pl.program_id(2) > 0)
        def _():
            o_ref[...] += part

    if chunked:
        out_spec = pl.BlockSpec((tn // chunk, tm, chunk), lambda i, j, k: (j, 0, 0))
        out_shape = jax.ShapeDtypeStruct((n_dim // chunk, m_dim, chunk), F32)
    else:
        out_spec = pl.BlockSpec((tm, tn), lambda i, j, k: (i, j))
        out_shape = jax.ShapeDtypeStruct((m_dim, n_dim), F32)
    (out,), job_res = _call(
        body, (a, b), name=name, jobs=jobs,
        grid=(m_dim // tm, n_dim // tn, tokens // tk),
        in_specs=[pl.BlockSpec((tk, tm), lambda i, j, k: (k, i)),
                  pl.BlockSpec((tk, tn), lambda i, j, k: (k, j))],
        out_specs=[out_spec], out_shape=[_in_hbm(out_shape)],
        params=_cparams(("arbitrary", "arbitrary", "arbitrary"), vmem_mb))
    return out, job_res


def _proj_bwd(dqa, dkta, dvta, dqb, dktb, dvtb, raw, x, dx1, g1, w_in, gq, gk, ck, sk, *, seq, tm, sub, jobs=()):
    tokens = x.shape[0]
    nt = tokens // tm
    n_seq = seq // tm
    nblk = tm // BLOCK

    def body(dqa_ref, dkta_ref, dvta_ref, dqb_ref, dkb_ref, dvb_ref, raw_ref, x_ref, dx1_ref, g1_ref, w_ref,
             gq_ref, gk_ref, ck_ref, sk_ref,
             gx_ref, dproj_ref, dg1_ref, dgq_ref, dgk_ref, dp):
        parts = []
        for r in range(tm // sub):
            rows = slice(r * sub, (r + 1) * sub)
            qa = raw_ref[rows, 0:Q_WIDTH]
            dqn = _rope_t(dqa_ref[rows, :], ck_ref[rows, :], sk_ref[rows, :]) * SCALE
            rq = _head_r(qa)
            nq = qa * rq
            dnq = dqn * gq_ref[...]
            dp[rows, 0:Q_WIDTH] = rq * (dnq - nq * (_seg64_sum(dnq * nq) * (1.0 / HEAD_DIM)))

            ka = raw_ref[rows, Q_WIDTH:QK_RAW]
            dkn = _rope_t(dkta_ref[0, :, rows].T, ck_ref[rows, :], sk_ref[rows, :])
            rk = _head_r(ka)
            nk = ka * rk
            dnk = dkn * gk_ref[...]
            dp[rows, 512:640] = rk * (dnk - nk * (_seg64_sum(dnk * nk) * (1.0 / HEAD_DIM)))

            dp[rows, 640:768] = dvta_ref[0, :, rows].T
            dp[rows, 768:1280] = dqb_ref[rows, :] * SCALE
            for j in range(r * sub // BLOCK, (r + 1) * sub // BLOCK):
                dp[j * BLOCK:(j + 1) * BLOCK, 1280:1408] = dkb_ref[0, j] + dkb_ref[1, j]
                dp[j * BLOCK:(j + 1) * BLOCK, 1408:1536] = dvb_ref[0, j] + dvb_ref[1, j]

            dproj = dp[rows, :].astype(MM)
            dproj_ref[rows, :] = dproj
            dh1 = _dot_nt(dproj[:, 0:IN_CHUNK], w_ref[0])
            for j in range(1, N_CHIPS):
                dh1 = dh1 + _dot_nt(dproj[:, j * IN_CHUNK:(j + 1) * IN_CHUNK], w_ref[j])
            xv = x_ref[rows, :]
            dxn, dg1v = _rms_bwd(xv, _rms_r(xv), g1_ref[...], dh1)
            gx_ref[rows, :] = dx1_ref[rows, :] + dxn
            parts.append((jnp.sum(dqn * nq, axis=0, keepdims=True), jnp.sum(dkn * nk, axis=0, keepdims=True),
                          jnp.sum(dg1v, axis=0, keepdims=True)))
        dgq_ref[0] = functools.reduce(jnp.add, [p[0] for p in parts])
        dgk_ref[0] = functools.reduce(jnp.add, [p[1] for p in parts])
        dg1_ref[0] = functools.reduce(jnp.add, [p[2] for p in parts])

    tok = lambda w: pl.BlockSpec((tm, w), lambda i: (i, 0))
    tab = lambda w: pl.BlockSpec((tm, w), lambda i: (i % n_seq, 0))
    row = lambda w: pl.BlockSpec((1, w), lambda i: (0, 0))
    tposed = pl.BlockSpec((1, KV_WIDTH, tm), lambda i: (i // n_seq, 0, i % n_seq))
    blocks = pl.BlockSpec((N_KV, nblk, BLOCK, KV_WIDTH), lambda i: (0, i, 0, 0))
    part = lambda w: pl.BlockSpec((1, 1, w), lambda i: (i, 0, 0))
    return _call(
        body, (dqa, dkta, dvta, dqb, dktb, dvtb, raw, x, dx1, g1, w_in, gq, gk, ck, sk),
        name="proj_bwd", jobs=jobs,
        grid=(nt,),
        in_specs=[tok(Q_WIDTH), tposed, tposed, tok(Q_WIDTH), blocks, blocks, tok(QK_RAW), tok(D_MODEL),
                  tok(D_MODEL), row(D_MODEL),
                  pl.BlockSpec((N_CHIPS, D_MODEL, IN_CHUNK), lambda i: (0, 0, 0)),
                  row(Q_WIDTH), row(KV_WIDTH), tab(KV_WIDTH), tab(KV_WIDTH)],
        out_specs=[tok(D_MODEL), tok(IN_TOTAL), part(D_MODEL), part(Q_WIDTH), part(KV_WIDTH)],
        out_shape=[jax.ShapeDtypeStruct((tokens, D_MODEL), F32),
                   jax.ShapeDtypeStruct((tokens, IN_TOTAL), MM),
                   jax.ShapeDtypeStruct((nt, 1, D_MODEL), F32),
                   jax.ShapeDtypeStruct((nt, 1, Q_WIDTH), F32),
                   jax.ShapeDtypeStruct((nt, 1, KV_WIDTH), F32)],
        scratch_shapes=[pltpu.VMEM((tm, IN_TOTAL), F32)],
        params=_cparams(("arbitrary",), 56))


def _pack_small(dg1, dg2, dg3, dg4, dgq, dgk, dsink, dbias, bucket, loss):
    def body(dg1_ref, dg2_ref, dg3_ref, dg4_ref, dgq_ref, dgk_ref, dsink_ref, dbias_ref, bucket_ref, loss_ref,
             out_ref, rel_ref):
        out_ref[...] = jnp.zeros_like(out_ref)
        for r, ref in ((ROW_G1, dg1_ref), (ROW_G2, dg2_ref), (ROW_G3, dg3_ref), (ROW_G4, dg4_ref)):
            acc = ref[0]
            for t in range(1, ref.shape[0]):
                acc = acc + ref[t]
            out_ref[r:r + 1, :] = acc

        def fold(ref, heads):
            acc = ref[0]
            for t in range(1, ref.shape[0]):
                acc = acc + ref[t]
            tot = acc[:, 0:HEAD_DIM]
            for h in range(1, heads):
                tot = tot + acc[:, h * HEAD_DIM:(h + 1) * HEAD_DIM]
            return tot

        out_ref[ROW_MISC:ROW_MISC + 1, MISC_GQ:MISC_GQ + HEAD_DIM] = fold(dgq_ref, GROUP * N_KV)
        out_ref[ROW_MISC:ROW_MISC + 1, MISC_GK:MISC_GK + HEAD_DIM] = fold(dgk_ref, N_KV)
        for h in range(GROUP * N_KV):
            g = h % GROUP
            out_ref[ROW_MISC:ROW_MISC + 1, MISC_SINK + h:MISC_SINK + h + 1] = jnp.sum(
                dsink_ref[h // GROUP, 0:1, g * BLOCK:(g + 1) * BLOCK], axis=-1, keepdims=True)
        lacc = loss_ref[0, 0:1, 0:1]
        for t in range(1, loss_ref.shape[0]):
            lacc = lacc + loss_ref[t, 0:1, 0:1]
        out_ref[ROW_MISC:ROW_MISC + 1, MISC_LOSS:MISC_LOSS + 1] = lacc
        lane = lax.broadcasted_iota(jnp.int32, (N_BUCKETS, LANES), 1)
        row = lax.broadcasted_iota(jnp.int32, (N_BUCKETS, LANES), 0)

        def per_bucket(b, acc):
            for h in range(GROUP * N_KV):
                g = h % GROUP
                sel = jnp.zeros((BLOCK, BLOCK), F32)
                for piece in range(3):
                    sel = sel + jnp.where(bucket_ref[piece] == b,
                                          dbias_ref[h // GROUP, piece, :, g * BLOCK:(g + 1) * BLOCK], 0.0)
                tot = jnp.sum(jnp.sum(sel, axis=0, keepdims=True), axis=-1, keepdims=True)
                acc = jnp.where((row == b) & (lane == h), tot, acc)
            return acc

        rel_ref[...] = lax.fori_loop(0, N_BUCKETS, per_bucket, jnp.zeros((N_BUCKETS, LANES), F32))

    args = (dg1, dg2, dg3, dg4, dgq, dgk, dsink, dbias, bucket, loss)
    outs = [jax.ShapeDtypeStruct((8, D_MODEL), F32), jax.ShapeDtypeStruct((N_BUCKETS, LANES), F32)]
    return pl.pallas_call(
        body, name="pack_small", grid=(1,),
        in_specs=[_whole(a) for a in args], out_specs=[_whole(o) for o in outs], out_shape=outs,
        compiler_params=pltpu.CompilerParams(vmem_limit_bytes=32 * 1024 * 1024),
    )(*map(_from_hbm, args))


def _gather_weights(shards, whole):
    n = len(shards)
    full = [t for t in range(n) if whole[t]]

    def body(*refs):
        ins, outs = refs[:n], refs[n:2 * n]
        raw, stage = refs[2 * n:3 * n], refs[3 * n:4 * n]
        load_sem, local_sem, ici_send, ici_recv, d2d_send, d2d_recv = refs[4 * n:]
        x, y, c = _place()
        k = 2 * x + y
        sibling = (x, y, 1 - c)
        order = full + [t for t in range(n) if t not in full]
        loads = {t: pltpu.make_async_copy(ins[t], raw[t], load_sem.at[t]) for t in order}
        for t in order:
            loads[t].start()
        copies, sends = [], []
        for t in order:
            loads[t].wait()
            stage[t][...] = raw[t][...].astype(MM)
            mine = pltpu.make_async_copy(stage[t], outs[t].at[k], local_sem.at[t])
            mine.start()
            copies.append(mine)
            if t in full:
                half = ins[t].shape[0] // 2
                rows = pl.ds(c * half, half)
                for r, (fx, fy) in enumerate(_CHIP_FLIPS):
                    cp = _remote(stage[t].at[rows], outs[t].at[k, rows], ici_send.at[t, r], ici_recv.at[t, r],
                                 (_flip(x, fx), _flip(y, fy), c))
                    cp.start()
                    sends.append(cp)
        for t in full:
            half = ins[t].shape[0] // 2
            rows = pl.ds(c * half, half)
            for r, (fx, fy) in enumerate(_CHIP_FLIPS):
                kk = 2 * _flip(x, fx) + _flip(y, fy)
                landed = outs[t].at[kk, rows]
                _remote(landed, landed, ici_send.at[t, r], ici_recv.at[t, r], sibling).wait_recv()
                fwd = _remote(landed, landed, d2d_send.at[t, r], d2d_recv.at[t, r], sibling)
                fwd.start()
                sends.append(fwd)
        for t in full:
            half = ins[t].shape[0] // 2
            other = pl.ds((1 - c) * half, half)
            for r, (fx, fy) in enumerate(_CHIP_FLIPS):
                kk = 2 * _flip(x, fx) + _flip(y, fy)
                theirs = outs[t].at[kk, other]
                _remote(theirs, theirs, d2d_send.at[t, r], d2d_recv.at[t, r], sibling).wait_recv()
        for cp in sends:
            cp.wait_send()
        for cp in copies:
            cp.wait()

    return pl.pallas_call(
        body, name="gather_weights",
        in_specs=[HBM] * n, out_specs=[HBM] * n,
        out_shape=[pltpu.HBM((N_CHIPS,) + s.shape, MM) for s in shards],
        scratch_shapes=[pltpu.VMEM(s.shape, F32) for s in shards] + [pltpu.VMEM(s.shape, MM) for s in shards] + [
            pltpu.SemaphoreType.DMA((n,)), pltpu.SemaphoreType.DMA((n,)),
            pltpu.SemaphoreType.DMA((n, 3)), pltpu.SemaphoreType.DMA((n, 3)),
            pltpu.SemaphoreType.DMA((n, 3)), pltpu.SemaphoreType.DMA((n, 3))],
        compiler_params=pltpu.CompilerParams(vmem_limit_bytes=40 * 1024 * 1024),
    )(*shards)


def _add_half(grad, got, where, *, name, tr):
    nch, half, cols = got.shape
    nblk = half // tr

    def body(where_ref, g_ref, r_ref, o_ref):
        o_ref[...] = (g_ref[...] + r_ref[...]).astype(MM)

    return pl.pallas_call(
        body, name=name,
        grid_spec=pltpu.PrefetchScalarGridSpec(
            num_scalar_prefetch=1, grid=(nch, nblk),
            in_specs=[pl.BlockSpec((1, tr, cols), lambda j, i, where_ref: (j, where_ref[1] * nblk + i, 0)),
                      pl.BlockSpec((1, tr, cols), lambda j, i, where_ref: (j, i, 0))],
            out_specs=pl.BlockSpec((1, tr, cols), lambda j, i, where_ref: (j, i, 0))),
        out_shape=jax.ShapeDtypeStruct(got.shape, MM),
        compiler_params=_cparams(("parallel", "parallel"), 32),
    )(where, grad, got)


def _add_chips(own, got, where, *, name, tr):
    _, half, cols = own.shape
    nblk = half // tr

    def body(where_ref, o_ref, g_ref, out_ref):
        f = lambda v: v.astype(F32)
        out_ref[...] = ((f(o_ref[0]) + f(g_ref[0])) + f(g_ref[1])) + f(g_ref[2])

    return pl.pallas_call(
        body, name=name,
        grid_spec=pltpu.PrefetchScalarGridSpec(
            num_scalar_prefetch=1, grid=(nblk,),
            in_specs=[pl.BlockSpec((1, tr, cols), lambda i, where_ref: (where_ref[0], i, 0)),
                      pl.BlockSpec((3, tr, cols), lambda i, where_ref: (0, i, 0))],
            out_specs=pl.BlockSpec((tr, cols), lambda i, where_ref: (where_ref[1] * nblk + i, 0))),
        out_shape=pltpu.HBM((2 * half, cols), F32),
        compiler_params=_cparams(("parallel",), 32),
    )(where, own, got)


def _small_job(tiles):
    n = len(tiles)

    def copies(ins, outs, sems):
        x, y, c = _place()
        me = 4 * x + 2 * y + c
        local, send, recv = sems
        cps = []
        for t in range(n):
            cps.append(pltpu.make_async_copy(ins[t], outs[t].at[me], local.at[t]))
            for r in range(1, N_DEV):
                fx, fy, fc = (r >> 2) & 1, (r >> 1) & 1, r & 1
                cps.append(_remote(ins[t], outs[t].at[me], send.at[t, r - 1], recv.at[t, r - 1],
                                   (_flip(x, fx), _flip(y, fy), _flip(c, fc))))
        return cps

    return _Job(tiles, [jax.ShapeDtypeStruct((N_DEV,) + t.shape, F32) for t in tiles],
                [pltpu.SemaphoreType.DMA((n,)), pltpu.SemaphoreType.DMA((n, N_DEV - 1)),
                 pltpu.SemaphoreType.DMA((n, N_DEV - 1))], copies)


def _adamw_math(w, g, m, v):
    m = ADAM_B1 * m + (1.0 - ADAM_B1) * g
    v = ADAM_B2 * v + (1.0 - ADAM_B2) * (g * g)
    m_hat = m / (1.0 - ADAM_B1 ** ADAM_STEP)
    v_hat = v / (1.0 - ADAM_B2 ** ADAM_STEP)
    delta = -ADAM_LR * (m_hat / (jnp.sqrt(v_hat) + ADAM_EPS) + ADAM_WD * w)
    return delta, m, v


def _adamw(w, g, m, v, *, name, tr):
    rows, cols = w.shape

    def body(w_ref, g_ref, m_ref, v_ref, go_ref, d_ref, nm_ref, nv_ref):
        g = g_ref[...]
        go_ref[...] = g
        d_ref[...], nm_ref[...], nv_ref[...] = _adamw_math(w_ref[...], g, m_ref[...], v_ref[...])

    spec = pl.BlockSpec((tr, cols), lambda i: (i, 0))
    return pl.pallas_call(
        body, name=name,
        grid=(rows // tr,),
        in_specs=[spec] * 4, out_specs=[spec] * 4,
        out_shape=[jax.ShapeDtypeStruct(w.shape, F32)] * 4,
        compiler_params=_cparams(("parallel",), 32),
    )(w, g, m, v)


def _small_adamw(gathered, gathered_rel, params, moments_m, moments_v):
    n = len(params)

    def body(all_ref, rel_all_ref, *refs):
        w_refs, m_refs, v_refs = refs[:n], refs[n:2 * n], refs[2 * n:3 * n]
        loss_ref = refs[3 * n]
        out_refs = refs[3 * n + 1:]
        g = all_ref[0]
        rel = rel_all_ref[0]
        for d in range(1, N_DEV):
            g = g + all_ref[d]
            rel = rel + rel_all_ref[d]
        misc = g[ROW_MISC:ROW_MISC + 1]
        loss_ref[...] = misc[:, MISC_LOSS:MISC_LOSS + 1]
        grads = (g[ROW_G1:ROW_G1 + 1], g[ROW_G2:ROW_G2 + 1], g[ROW_G3:ROW_G3 + 1], g[ROW_G4:ROW_G4 + 1],
                 misc[:, MISC_GQ:MISC_GQ + HEAD_DIM], misc[:, MISC_GK:MISC_GK + HEAD_DIM],
                 misc[:, MISC_SINK:MISC_SINK + GROUP * N_KV], rel[:, 0:GROUP * N_KV])
        for i in range(n):
            d, nm, nv = _adamw_math(w_refs[i][...], grads[i], m_refs[i][...], v_refs[i][...])
            for j, val in enumerate((grads[i], d, nm, nv)):
                out_refs[4 * i + j][...] = val

    args = (gathered, gathered_rel, *params, *moments_m, *moments_v)
    out_shape = [jax.ShapeDtypeStruct((1, 1), F32)] + [jax.ShapeDtypeStruct(p.shape, F32) for p in params
                                                       for _ in range(4)]
    outs = pl.pallas_call(
        body, name="small_adamw", grid=(1,),
        in_specs=[_whole(a) for a in args], out_specs=[_whole(o) for o in out_shape], out_shape=out_shape,
    )(*map(_from_hbm, args))
    return outs[0], [outs[1 + 4 * i:5 + 4 * i] for i in range(n)]


def kernel(x, w_in, w_o, g_pre_mix, g_post_mix, q_norm_a, k_norm_a, sink_b, rel_bias, g_pre_ffn, w_ffn_up, w_ffn_down, g_post_ffn, loss_target, m_w_in, m_w_o, m_g_pre_mix, m_g_post_mix, m_q_norm_a, m_k_norm_a, m_sink_b, m_rel_bias, m_g_pre_ffn, m_w_ffn_up, m_w_ffn_down, m_g_post_ffn, v_w_in, v_w_o, v_g_pre_mix, v_g_post_mix, v_q_norm_a, v_k_norm_a, v_sink_b, v_rel_bias, v_g_pre_ffn, v_w_ffn_up, v_w_ffn_down, v_g_post_ffn):
    batch, seq, _ = x.shape
    tokens = batch * seq
    where = jnp.stack([2 * lax.axis_index("x") + lax.axis_index("y"), lax.axis_index("c")]).astype(jnp.int32)
    x2 = x.reshape(tokens, D_MODEL)
    g1, g2, g3, g4 = g_pre_mix, g_post_mix, g_pre_ffn, g_post_ffn

    cos, sin = _rope_tables(seq)
    ck, sk = jnp.tile(cos, (1, 2)), jnp.tile(sin, (1, 2))
    gq8, gk2 = jnp.tile(q_norm_a, (1, 8)), jnp.tile(k_norm_a, (1, 2))
    bucket, band = _window_tables()
    bias = _bias_build(rel_bias.T, bucket, band)

    w_in_g, w_o_p, w_up_p, w_down_p = _gather_weights(
        (w_in[0], w_o[0], w_ffn_up[0], w_ffn_down[0]), whole=(True, False, False, False))
    (h1, raw, qa, ka, kta, va, vta, qtb, kb, ktb, vb, vtb) = _pre_proj(
        x2, g1, w_in_g, gq8, gk2, ck, sk, seq=seq, tm=min(512, seq), sub=256)
    (oa, p_a, linv_a), (w_part,) = _attn_a_fwd(
        qa, kta, va, seq=seq, bq=min(256, seq), jobs=[_gather_job([w_o_p, w_up_p, w_down_p], forward=False)])
    kb3 = kb.reshape(tokens // BLOCK, BLOCK, KV_WIDTH)
    vb3 = vb.reshape(tokens // BLOCK, BLOCK, KV_WIDTH)
    (ob, p_b, stat_b), ((w_o_g, w_up_g, w_down_g),) = _attn_b_fwd(
        qtb, kb3, vtb, bias, sink_b, seq=seq, per_step=min(16, seq // BLOCK),
        jobs=[_gather_job(w_part, forward=True)])
    w_o2 = w_o_g.reshape(D_MODEL, D_MODEL)
    w_down2 = w_down_g.reshape(D_FF, D_MODEL)
    mix, x1, h2, o_cat, u, df, dy, loss_t, dg4 = _mix_ffn_fwd(
        oa, ob, w_o2, x2, g2, g3, w_up_g, w_down2, loss_target.reshape(tokens, D_MODEL), g4, tm=256)

    dz, dx1, dmix, dg3, dg2, doa, dob = _ffn_bwd_act(df, w_down2, u, w_up_g, x1, dy, mix, g3, g2, w_o2, tm=256)
    gw_down, _ = _tn_matmul(u, df, name="grad_w_down", tm=1024, tn=1024, tk=min(2048, tokens), square_a=True)
    gw_down = gw_down.reshape(N_CHIPS, FF_CHUNK, D_MODEL)
    gw_up, ((got_down,),) = _tn_matmul(h2, dz, name="grad_w_up", tm=1024, tn=1024, tk=min(2048, tokens), chunk=FF_CHUNK,
                                        jobs=[_swap_job([gw_down])])
    gw_o, _ = _tn_matmul(o_cat, dmix, name="grad_w_o", tm=1024, tn=1024, tk=min(2048, tokens))
    gw_o = gw_o.reshape(N_CHIPS, O_CHUNK, D_MODEL)
    sum_down = _add_half(gw_down, got_down, where, name="add_half_w_down", tr=128)
    (dqa, dkta, dvta), ((ex_down,), (got_up,)) = _attn_a_bwd(
        qa, ka, vta, doa, oa, p_a, linv_a, seq=seq, bq=min(256, seq),
        jobs=[_exchange_job([sum_down]), _swap_job([gw_up])])
    full_down = _add_chips(sum_down, ex_down, where, name="add_chips_w_down", tr=128)
    sum_up = _add_half(gw_up, got_up, where, name="add_half_w_up", tr=128)
    (dqb, dkb, dvb, dbias, dsink), ((ex_up,), (g_down,), (got_o,)) = _attn_b_bwd(
        qtb, ktb, vb3, dob, ob, p_b, stat_b, seq=seq, per_step=min(8, seq // BLOCK),
        jobs=[_exchange_job([sum_up]), _join_job([full_down]), _swap_job([gw_o])])
    full_up = _add_chips(sum_up, ex_up, where, name="add_chips_w_up", tr=128)
    sum_o = _add_half(gw_o, got_o, where, name="add_half_w_o", tr=128)
    (grad_x, dproj, dg1, dgq, dgk), _ = _proj_bwd(
        dqa, dkta, dvta, dqb, dkb, dvb, raw, x2, dx1, g1, w_in_g, gq8, gk2, ck, sk,
        seq=seq, tm=min(512, seq), sub=128)
    packed, packed_rel = _pack_small(dg1, dg2, dg3, dg4, dgq, dgk, dsink, dbias, bucket, loss_t)
    gw_in, ((ex_o,), (g_up,), (gathered, gathered_rel)) = _tn_matmul(
        h1, dproj, name="grad_w_in", tm=1024, tn=2 * IN_CHUNK, tk=min(2048, tokens), chunk=IN_CHUNK,
        jobs=[_exchange_job([sum_o]), _join_job([full_up]), _small_job([packed, packed_rel])])
    full_o = _add_chips(sum_o, ex_o, where, name="add_chips_w_o", tr=128)

    (g_o,), (got_in,) = _run_jobs("tail_swap", [_join_job([full_o]), _swap_job([gw_in])])
    sum_in = _add_half(gw_in, got_in, where, name="add_half_w_in", tr=128)
    ((ex_in,),) = _run_jobs("tail_exchange", [_exchange_job([sum_in])])
    full_in = _add_chips(sum_in, ex_in, where, name="add_chips_w_in", tr=128)
    ((g_in,),) = _run_jobs("tail_join", [_join_job([full_in])])

    big = [[t[None] for t in _adamw(w[0], g, m[0], v[0], name="adamw_" + nm, tr=128)] for nm, w, g, m, v in (
        ("w_in", w_in, g_in, m_w_in, v_w_in), ("w_o", w_o, g_o, m_w_o, v_w_o),
        ("w_up", w_ffn_up, g_up, m_w_ffn_up, v_w_ffn_up), ("w_down", w_ffn_down, g_down, m_w_ffn_down, v_w_ffn_down))]

    loss, small = _small_adamw(
        gathered, gathered_rel,
        (g1, g2, g3, g4, q_norm_a, k_norm_a, sink_b, rel_bias),
        (m_g_pre_mix, m_g_post_mix, m_g_pre_ffn, m_g_post_ffn, m_q_norm_a, m_k_norm_a, m_sink_b, m_rel_bias),
        (v_g_pre_mix, v_g_post_mix, v_g_pre_ffn, v_g_post_ffn, v_q_norm_a, v_k_norm_a, v_sink_b, v_rel_bias))
    s_g1, s_g2, s_g3, s_g4, s_gq, s_gk, s_sink, s_rel = small

    def leaves(i):
        return (big[0][i], big[1][i], s_g1[i], s_g2[i], s_gq[i], s_gk[i], s_sink[i], s_rel[i], s_g3[i],
                big[2][i], big[3][i], s_g4[i])

    loss = loss.reshape(())
    return (loss, grad_x.reshape(batch, seq, D_MODEL), *leaves(0), *leaves(1), *leaves(2), *leaves(3))
```

```python
import functools

import jax
import jax.numpy as jnp
import numpy as np
from jax import lax
from jax.experimental import pallas as pl
from jax.experimental.pallas import tpu as pltpu

F32 = jnp.float32
MM = jnp.bfloat16

D_MODEL = 1024
HEAD_DIM = 64
N_KV = 2
GROUP = 4
Q_WIDTH = 512
KV_WIDTH = 128
D_FF = 4096
GRID_W = 64
BLOCK = 128
N_BUCKETS = 32
MAX_DISTANCE = 128
ROPE_THETA = 10000.0
EPS = 1e-6
NEG_INF = -1e30
SCALE = HEAD_DIM ** -0.5
IN_TOTAL = 1536
N_CHIPS = 4
N_DEV = 8
IN_CHUNK = IN_TOTAL // N_CHIPS
FF_CHUNK = D_FF // N_CHIPS
O_CHUNK = D_MODEL // N_CHIPS
QK_RAW = 640

ADAM_LR = 0.001
ADAM_B1 = 0.9
ADAM_B2 = 0.999
ADAM_EPS = 1e-08
ADAM_WD = 0.01
ADAM_STEP = 10

LANES = 128
MESH = pl.DeviceIdType.MESH
HBM = pl.BlockSpec(memory_space=pl.ANY)
SMEM = pl.BlockSpec(memory_space=pltpu.SMEM)

ROW_G1, ROW_G2, ROW_G3, ROW_G4, ROW_MISC = 0, 1, 2, 3, 4
MISC_GQ, MISC_GK, MISC_SINK, MISC_LOSS = 0, 64, 128, 512


def _cparams(sem, vmem_mb):
    return pltpu.CompilerParams(dimension_semantics=sem, vmem_limit_bytes=vmem_mb * 1024 * 1024)


def _whole(a):
    return pl.BlockSpec(a.shape, lambda i: (0,) * len(a.shape))


def _from_hbm(a):
    return pltpu.with_memory_space_constraint(a, pltpu.HBM)


def _in_hbm(s):
    return pltpu.HBM(s.shape, s.dtype)


class _Job:
    def __init__(self, operands, out_shapes, sems, copies, alias=None):
        self.operands, self.out_shapes, self.sems, self.copies = list(operands), list(out_shapes), list(sems), copies
        self.alias = dict(alias or {})


def _place():
    return lax.axis_index("x"), lax.axis_index("y"), lax.axis_index("c")


_CHIP_FLIPS = ((1, 0), (0, 1), (1, 1))


def _flip(v, bit):
    return 1 - v if bit else v


def _remote(src, dst, send, recv, dev):
    return pltpu.make_async_remote_copy(src_ref=src, dst_ref=dst, send_sem=send, recv_sem=recv,
                                        device_id=dev, device_id_type=MESH)


def _swap_job(grads):
    n = len(grads)

    def copies(ins, outs, sems):
        x, y, c = _place()
        send, recv = sems
        cps = []
        for t in range(n):
            half = ins[t].shape[1] // 2
            cps.append(_remote(ins[t].at[:, pl.ds((1 - c) * half, half), :], outs[t], send.at[t], recv.at[t],
                               (x, y, 1 - c)))
        return cps

    shapes = [jax.ShapeDtypeStruct((g.shape[0], g.shape[1] // 2, g.shape[2]), F32) for g in grads]
    return _Job(grads, shapes, [pltpu.SemaphoreType.DMA((n,)), pltpu.SemaphoreType.DMA((n,))], copies)


def _exchange_job(sums):
    n = len(sums)

    def copies(ins, outs, sems):
        x, y, c = _place()
        send, recv = sems
        cps = []
        for t in range(n):
            for r, (fx, fy) in enumerate(_CHIP_FLIPS):
                kk = 2 * _flip(x, fx) + _flip(y, fy)
                cps.append(_remote(ins[t].at[kk], outs[t].at[r], send.at[t, r], recv.at[t, r],
                                   (_flip(x, fx), _flip(y, fy), c)))
        return cps

    shapes = [jax.ShapeDtypeStruct((3,) + s.shape[1:], s.dtype) for s in sums]
    return _Job(sums, shapes, [pltpu.SemaphoreType.DMA((n, 3)), pltpu.SemaphoreType.DMA((n, 3))], copies)


def _join_job(fulls):
    n = len(fulls)

    def copies(ins, outs, sems):
        x, y, c = _place()
        send, recv = sems
        cps = []
        for t in range(n):
            half = ins[t].shape[0] // 2
            rows = pl.ds(c * half, half)
            cps.append(_remote(ins[t].at[rows], outs[t].at[rows], send.at[t], recv.at[t], (x, y, 1 - c)))
        return cps

    shapes = [jax.ShapeDtypeStruct(f.shape, f.dtype) for f in fulls]
    return _Job(fulls, shapes, [pltpu.SemaphoreType.DMA((n,)), pltpu.SemaphoreType.DMA((n,))], copies,
                alias={t: t for t in range(n)})


def _gather_job(bufs, forward):
    n = len(bufs)

    def copies(ins, outs, sems):
        x, y, c = _place()
        send, recv = sems
        cps = []
        for t in range(n):
            half = ins[t].shape[1] // 2
            rows = pl.ds(c * half, half)
            for r, (fx, fy) in enumerate(_CHIP_FLIPS):
                if forward:
                    kk = 2 * _flip(x, fx) + _flip(y, fy)
                    dev = (x, y, 1 - c)
                else:
                    kk = 2 * x + y
                    dev = (_flip(x, fx), _flip(y, fy), c)
                cps.append(_remote(ins[t].at[kk, rows], outs[t].at[kk, rows], send.at[t, r], recv.at[t, r], dev))
        return cps

    shapes = [jax.ShapeDtypeStruct(b.shape, b.dtype) for b in bufs]
    return _Job(bufs, shapes, [pltpu.SemaphoreType.DMA((n, 3)), pltpu.SemaphoreType.DMA((n, 3))], copies,
                alias={t: t for t in range(n)})


def _call(body, args, *, name, grid, in_specs, out_specs, out_shape, scratch_shapes=(), params=None, jobs=()):
    n_in, n_out, n_scr = len(in_specs), len(out_specs), len(scratch_shapes)
    job_in = [len(j.operands) for j in jobs]
    job_out = [len(j.out_shapes) for j in jobs]
    job_sem = [len(j.sems) for j in jobs]

    def wrapped(*refs):
        pos = 0
        ins = refs[pos:pos + n_in]; pos += n_in
        jins = []
        for k in job_in:
            jins.append(refs[pos:pos + k]); pos += k
        outs = refs[pos:pos + n_out]; pos += n_out
        jouts = []
        for k in job_out:
            jouts.append(refs[pos:pos + k]); pos += k
        scr = refs[pos:pos + n_scr]; pos += n_scr
        jsems = []
        for k in job_sem:
            jsems.append(refs[pos:pos + k]); pos += k
        if jobs:
            ids = [pl.program_id(d) for d in range(len(grid))]
            first = functools.reduce(jnp.logical_and, [i == 0 for i in ids])
            last = functools.reduce(jnp.logical_and, [i == g - 1 for i, g in zip(ids, grid)])

            @pl.when(first)
            def _():
                for j, ji, jo, js in zip(jobs, jins, jouts, jsems):
                    for cp in j.copies(ji, jo, js):
                        cp.start()

        body(*ins, *outs, *scr)
        if jobs:
            @pl.when(last)
            def _():
                for j, ji, jo, js in zip(jobs, jins, jouts, jsems):
                    for cp in j.copies(ji, jo, js):
                        cp.wait()

    aliases = {}
    in_pos, out_pos = n_in, n_out
    for j in jobs:
        for i, o in j.alias.items():
            aliases[in_pos + i] = out_pos + o
        in_pos += len(j.operands)
        out_pos += len(j.out_shapes)
    res = pl.pallas_call(
        wrapped, name=name, grid=grid,
        in_specs=list(in_specs) + [HBM] * sum(job_in),
        out_specs=list(out_specs) + [HBM] * sum(job_out),
        out_shape=list(out_shape) + [_in_hbm(s) for j in jobs for s in j.out_shapes],
        scratch_shapes=list(scratch_shapes) + [s for j in jobs for s in j.sems],
        input_output_aliases=aliases,
        compiler_params=params,
    )(*[a if spec is SMEM else _from_hbm(a) for a, spec in zip(args, in_specs)],
      *[a for j in jobs for a in j.operands])
    own, rest = list(res[:n_out]), list(res[n_out:])
    job_res = []
    for k in job_out:
        job_res.append(rest[:k])
        rest = rest[k:]
    return own, job_res


def _run_jobs(name, jobs):
    def body():
        pass

    return _call(body, (), name=name, grid=(1,), in_specs=[], out_specs=[], out_shape=[], jobs=jobs)[1]


def _dot(a, b):
    return jnp.dot(a, b, preferred_element_type=F32)


def _dot_nt(a, b):
    return lax.dot_general(a, b, (((1,), (1,)), ((), ())), preferred_element_type=F32)


def _dot_tn(a, b):
    return lax.dot_general(a, b, (((0,), (0,)), ((), ())), preferred_element_type=F32)


def _rms_r(x):
    return lax.rsqrt(jnp.mean(x * x, axis=-1, keepdims=True) + EPS)


def _rms_bwd(x, r, g, dy):
    n = x * r
    dn = dy * g
    dx = r * (dn - n * jnp.mean(dn * n, axis=-1, keepdims=True))
    return dx, dy * n


def _seg64_sum(v):
    rows, width = v.shape
    lane = lax.broadcasted_iota(jnp.int32, (rows, LANES), 1)
    lo = lane < HEAD_DIM
    outs = []
    for c in range(width // LANES):
        ch = v[:, c * LANES:(c + 1) * LANES]
        s_lo = jnp.sum(jnp.where(lo, ch, 0.0), axis=-1, keepdims=True)
        s_hi = jnp.sum(jnp.where(lo, 0.0, ch), axis=-1, keepdims=True)
        outs.append(jnp.where(lo, s_lo, s_hi))
    return outs[0] if len(outs) == 1 else jnp.concatenate(outs, axis=-1)


def _head_r(v):
    return lax.rsqrt(_seg64_sum(v * v) * (1.0 / HEAD_DIM) + EPS)


def _swap16(ch):
    lane = lax.broadcasted_iota(jnp.int32, ch.shape, 1)
    return jnp.where((lane % 32) < 16, pltpu.roll(ch, LANES - 16, 1), pltpu.roll(ch, 16, 1))


def _by_chunk(fn, v):
    outs = [fn(v[:, c * LANES:(c + 1) * LANES]) for c in range(v.shape[1] // LANES)]
    return outs[0] if len(outs) == 1 else jnp.concatenate(outs, axis=-1)


def _rope(v, cos, sin_signed):
    return _by_chunk(lambda ch: ch * cos + _swap16(ch) * sin_signed, v)


def _rope_t(g, cos, sin_signed):
    return _by_chunk(lambda ch: ch * cos + _swap16(ch * sin_signed), g)


def _rope_tables(seq):
    nf = HEAD_DIM // 4
    freqs = ROPE_THETA ** (-jnp.arange(nf, dtype=F32) / nf)
    pos = jnp.arange(seq, dtype=jnp.int32)
    row = (pos // GRID_W).astype(F32)
    col = (pos % GRID_W).astype(F32)
    ang_r = row[:, None] * freqs[None, :]
    ang_c = col[:, None] * freqs[None, :]
    cr, sr, cc, sc = jnp.cos(ang_r), jnp.sin(ang_r), jnp.cos(ang_c), jnp.sin(ang_c)
    cos = jnp.concatenate([cr, cr, cc, cc], axis=1)
    sin = jnp.concatenate([-sr, sr, -sc, sc], axis=1)
    return cos, sin


def _t5_bucket(rel):
    nb = N_BUCKETS // 2
    ret = (rel > 0).astype(jnp.int32) * nb
    n = jnp.abs(rel)
    max_exact = nb // 2
    nf = jnp.maximum(n, 1).astype(jnp.float32)
    large = max_exact + (jnp.log(nf / max_exact) / np.float32(np.log(MAX_DISTANCE / max_exact))
                         * (nb - max_exact)).astype(jnp.int32)
    large = jnp.minimum(large, nb - 1)
    return ret + jnp.where(n < max_exact, n, large)


def _window_tables():
    a = jnp.arange(BLOCK, dtype=jnp.int32)
    c = jnp.arange(3 * BLOCK, dtype=jnp.int32)
    rel = c[None, :] - BLOCK - a[:, None]
    bucket = _t5_bucket(rel)
    band = (jnp.abs(rel) <= BLOCK).astype(jnp.int32)
    to3 = lambda t: t.reshape(BLOCK, 3, BLOCK).transpose(1, 2, 0)
    return to3(bucket), to3(band)


def _pre_proj(x, g1, w_in, gq, gk, ck, sk, *, seq, tm, sub):
    tokens = x.shape[0]
    n_seq = seq // tm
    nblk = tm // BLOCK
    batch = tokens // seq

    def body(x_ref, g1_ref, w_ref, gq_ref, gk_ref, ck_ref, sk_ref,
             h1_ref, raw_ref, qa_ref, ka_ref, kta_ref, va_ref, vta_ref,
             qtb_ref, kb_ref, ktb_ref, vb_ref, vtb_ref, proj):
        for r in range(tm // sub):
            rows = slice(r * sub, (r + 1) * sub)
            xv = x_ref[rows, :]
            h = (xv * _rms_r(xv) * g1_ref[...]).astype(MM)
            h1_ref[rows, :] = h
            for j in range(N_CHIPS):
                proj[rows, j * IN_CHUNK:(j + 1) * IN_CHUNK] = _dot(h, w_ref[j])
            qa = proj[rows, 0:Q_WIDTH]
            ka = proj[rows, Q_WIDTH:QK_RAW]
            raw_ref[rows, :] = proj[rows, 0:QK_RAW]
            qn = qa * _head_r(qa) * gq_ref[...]
            qa_ref[rows, :] = (_rope(qn, ck_ref[rows, :], sk_ref[rows, :]) * SCALE).astype(MM)
            kn = ka * _head_r(ka) * gk_ref[...]
            kr = _rope(kn, ck_ref[rows, :], sk_ref[rows, :])
            ka_ref[rows, :] = kr.astype(MM)
            kta_ref[0, :, rows] = kr.T.astype(MM)
            va = proj[rows, 640:768]
            va_ref[rows, :] = va.astype(MM)
            vta_ref[0, :, rows] = va.T.astype(MM)
            qb = proj[rows, 768:1280] * SCALE
            kb = proj[rows, 1280:1408]
            vb = proj[rows, 1408:1536]
            kb_ref[rows, :] = kb.astype(MM)
            vb_ref[rows, :] = vb.astype(MM)
            for j in range(sub // BLOCK):
                blk = slice(j * BLOCK, (j + 1) * BLOCK)
                qtb_ref[r * (sub // BLOCK) + j] = qb[blk, :].T.astype(MM)
                ktb_ref[r * (sub // BLOCK) + j] = kb[blk, :].T.astype(MM)
                vtb_ref[r * (sub // BLOCK) + j] = vb[blk, :].T.astype(MM)

    tok = lambda w: pl.BlockSpec((tm, w), lambda i: (i, 0))
    tab = lambda w: pl.BlockSpec((tm, w), lambda i: (i % n_seq, 0))
    row = lambda w: pl.BlockSpec((1, w), lambda i: (0, 0))
    tposed = pl.BlockSpec((1, LANES, tm), lambda i: (i // n_seq, 0, i % n_seq))
    blocks = pl.BlockSpec((nblk, BLOCK, LANES), lambda i: (i, 0, 0))
    qblocks = pl.BlockSpec((nblk, Q_WIDTH, BLOCK), lambda i: (i, 0, 0))
    tok_mm = lambda w: jax.ShapeDtypeStruct((tokens, w), MM)
    return pl.pallas_call(
        body, name="pre_proj",
        grid=(tokens // tm,),
        in_specs=[tok(D_MODEL), row(D_MODEL),
                  pl.BlockSpec((N_CHIPS, D_MODEL, IN_CHUNK), lambda i: (0, 0, 0)),
                  row(Q_WIDTH), row(KV_WIDTH), tab(KV_WIDTH), tab(KV_WIDTH)],
        out_specs=[tok(D_MODEL), tok(QK_RAW), tok(Q_WIDTH), tok(KV_WIDTH), tposed, tok(KV_WIDTH), tposed,
                   qblocks, tok(KV_WIDTH), blocks, tok(KV_WIDTH), blocks],
        out_shape=[
            tok_mm(D_MODEL),
            jax.ShapeDtypeStruct((tokens, QK_RAW), F32),
            tok_mm(Q_WIDTH),
            tok_mm(KV_WIDTH),
            jax.ShapeDtypeStruct((batch, KV_WIDTH, seq), MM),
            tok_mm(KV_WIDTH),
            jax.ShapeDtypeStruct((batch, KV_WIDTH, seq), MM),
            jax.ShapeDtypeStruct((tokens // BLOCK, Q_WIDTH, BLOCK), MM),
            tok_mm(KV_WIDTH),
            jax.ShapeDtypeStruct((tokens // BLOCK, KV_WIDTH, BLOCK), MM),
            tok_mm(KV_WIDTH),
            jax.ShapeDtypeStruct((tokens // BLOCK, KV_WIDTH, BLOCK), MM),
        ],
        scratch_shapes=[pltpu.VMEM((tm, IN_TOTAL), F32)],
        compiler_params=_cparams(("parallel",), 48),
    )(*map(_from_hbm, (x, g1, w_in, gq, gk, ck, sk)))


def _kv_half(v2, kv):
    return jnp.where(kv == 0, v2[:, :HEAD_DIM], v2[:, HEAD_DIM:])


def _attn_a_fwd(qa, kta, va, *, seq, bq, jobs=()):
    tokens = qa.shape[0]
    batch = tokens // seq
    nq = seq // bq

    def body(q_ref, kt_ref, v_ref, o_ref, p_ref, linv_ref):
        kv = pl.program_id(1)
        kt = kt_ref[0]
        lane = lax.broadcasted_iota(jnp.int32, (seq, KV_WIDTH), 1)
        v = jnp.where((lane < HEAD_DIM) == (kv == 0), v_ref[...], jnp.ones((), MM))
        for g in range(GROUP):
            sl = slice(g * HEAD_DIM, (g + 1) * HEAD_DIM)
            s = _dot(q_ref[:, sl], kt)
            pb = jnp.exp((s - jnp.max(s, axis=-1, keepdims=True)).astype(MM))
            p_ref[0, g] = pb
            o2 = _dot(pb, v)
            linv = 1.0 / _kv_half(o2, 1 - kv)[:, 0:1]
            o_ref[:, sl] = _kv_half(o2, kv) * linv
            linv_ref[0, :, g:g + 1] = linv

    return _call(
        body, (qa, kta, va), name="attn_a_fwd", jobs=jobs,
        grid=(batch, N_KV, nq),
        in_specs=[pl.BlockSpec((bq, GROUP * HEAD_DIM), lambda b, k, i: (b * nq + i, k)),
                  pl.BlockSpec((1, HEAD_DIM, seq), lambda b, k, i: (b, k, 0)),
                  pl.BlockSpec((seq, KV_WIDTH), lambda b, k, i: (b, 0))],
        out_specs=[pl.BlockSpec((bq, GROUP * HEAD_DIM), lambda b, k, i: (b * nq + i, k)),
                   pl.BlockSpec((1, GROUP, bq, seq), lambda b, k, i: (k, 0, b * nq + i, 0)),
                   pl.BlockSpec((1, bq, GROUP), lambda b, k, i: (k, b * nq + i, 0))],
        out_shape=[jax.ShapeDtypeStruct((tokens, Q_WIDTH), F32),
                   jax.ShapeDtypeStruct((N_KV, GROUP, tokens, seq), MM),
                   jax.ShapeDtypeStruct((N_KV, tokens, GROUP), F32)],
        params=_cparams(("arbitrary", "arbitrary", "arbitrary"), 56))


def _attn_a_bwd(qa, ka, vta, do, o, p, linv, *, seq, bq, jobs=()):
    tokens = qa.shape[0]
    batch = tokens // seq
    nq = seq // bq

    def body(q_ref, k_ref, vt_ref, do_ref, o_ref, p_ref, linv_ref, dq_ref, dkt_ref, dvt_ref):
        kv = pl.program_id(1)

        @pl.when(pl.program_id(2) == 0)
        def _():
            dkt_ref[...] = jnp.zeros_like(dkt_ref)
            dvt_ref[...] = jnp.zeros_like(dvt_ref)

        vt = vt_ref[0]
        k2 = k_ref[...]
        for g in range(GROUP):
            sl = slice(g * HEAD_DIM, (g + 1) * HEAD_DIM)
            dof = do_ref[:, sl]
            delta = jnp.sum(dof * o_ref[:, sl], axis=-1, keepdims=True)
            linv_g = linv_ref[0, :, g:g + 1]
            pb = p_ref[0, g]
            dp = _dot(dof.astype(MM), vt)
            ds = pb * ((dp - delta) * linv_g).astype(MM)
            dq_ref[:, sl] = _kv_half(_dot(ds, k2), kv)
            dkt_ref[0] += _dot_tn(q_ref[:, sl], ds)
            dvt_ref[0] += _dot_tn((dof * linv_g).astype(MM), pb)

    qspec = pl.BlockSpec((bq, GROUP * HEAD_DIM), lambda b, k, i: (b * nq + i, k))
    tspec = pl.BlockSpec((1, HEAD_DIM, seq), lambda b, k, i: (b, k, 0))
    return _call(
        body, (qa, ka, vta, do, o, p, linv), name="attn_a_bwd", jobs=jobs,
        grid=(batch, N_KV, nq),
        in_specs=[qspec, pl.BlockSpec((seq, KV_WIDTH), lambda b, k, i: (b, 0)), tspec, qspec, qspec,
                  pl.BlockSpec((1, GROUP, bq, seq), lambda b, k, i: (k, 0, b * nq + i, 0)),
                  pl.BlockSpec((1, bq, GROUP), lambda b, k, i: (k, b * nq + i, 0))],
        out_specs=[qspec, tspec, tspec],
        out_shape=[jax.ShapeDtypeStruct((tokens, Q_WIDTH), F32),
                   jax.ShapeDtypeStruct((batch, KV_WIDTH, seq), F32),
                   jax.ShapeDtypeStruct((batch, KV_WIDTH, seq), F32)],
        params=_cparams(("arbitrary", "arbitrary", "arbitrary"), 56))


def _bias_build(rel_bias_t, bucket_t, band_t):
    def body(tab_ref, bucket_ref, band_ref, bias_ref):
        for h in range(GROUP * N_KV):
            for piece in range(3):
                bk = bucket_ref[piece]
                acc = jnp.zeros((BLOCK, BLOCK), F32)
                for b in range(N_BUCKETS):
                    acc = jnp.where(bk == b, tab_ref[h, b], acc)
                g = h % GROUP
                bias_ref[h // GROUP, piece, :, g * BLOCK:(g + 1) * BLOCK] = jnp.where(band_ref[piece] != 0, acc, NEG_INF)

    out = jax.ShapeDtypeStruct((N_KV, 3, BLOCK, GROUP * BLOCK), F32)
    return pl.pallas_call(
        body, name="bias_build", grid=(1,),
        in_specs=[SMEM, _whole(bucket_t), _whole(band_t)], out_specs=_whole(out), out_shape=out,
    )(rel_bias_t, bucket_t, band_t)


def _pad_heads(t, kv):
    outs = []
    for g in range(GROUP):
        tg = t[g * HEAD_DIM:(g + 1) * HEAD_DIM, :]
        zero = jnp.zeros_like(tg)
        outs.append(jnp.concatenate([jnp.where(kv == 0, tg, zero), jnp.where(kv == 0, zero, tg)], axis=0))
    return jnp.concatenate(outs, axis=-1)


def _unpad_heads(t, kv):
    outs = [_kv_half(t[:, g * BLOCK:(g + 1) * BLOCK].T, kv) for g in range(GROUP)]
    return jnp.concatenate(outs, axis=-1)


def _sink_row(sink_ref, kv):
    lane_head = lax.broadcasted_iota(jnp.int32, (1, GROUP * BLOCK), 1) // BLOCK
    row = jnp.zeros((1, GROUP * BLOCK), F32)
    for g in range(GROUP):
        row = jnp.where(lane_head == g, sink_ref[0, kv * GROUP + g], row)
    return row


def _window_scores_t(k_ref, idx, qpad, bias_ref, n, nblk):
    pieces = []
    for piece in range(3):
        s = _dot(k_ref[idx[piece]], qpad) + bias_ref[0, piece]
        if piece == 0:
            s = jnp.where(n > 0, s, NEG_INF)
        if piece == 2:
            s = jnp.where(n < nblk - 1, s, NEG_INF)
        pieces.append(s)
    return pieces


def _attn_b_fwd(qtb, kb3, vtb, bias, sink, *, seq, per_step, jobs=()):
    nblk_all = qtb.shape[0]
    tokens = nblk_all * BLOCK
    batch = tokens // seq
    nblk = seq // BLOCK
    nstep = nblk // per_step

    def body(sink_ref, q_ref, k_ref, vt_ref, bias_ref, o_ref, p_ref, stat_ref):
        kv = pl.program_id(0)
        first = pl.program_id(2) * per_step
        sink_row = _sink_row(sink_ref, kv)
        stat_row = lax.broadcasted_iota(jnp.int32, (8, GROUP * BLOCK), 0)

        def block(i, carry):
            n = first + i
            idx = (jnp.maximum(n - 1, 0), n, jnp.minimum(n + 1, nblk - 1))
            rows = slice(i * BLOCK, (i + 1) * BLOCK)
            qpad = _pad_heads(q_ref[n], kv)
            ss = _window_scores_t(k_ref, idx, qpad, bias_ref, n, nblk)
            m = jnp.maximum(jnp.maximum(jnp.max(ss[0], axis=0, keepdims=True),
                                        jnp.max(ss[1], axis=0, keepdims=True)),
                            jnp.maximum(jnp.max(ss[2], axis=0, keepdims=True), sink_row))
            ps = [jnp.exp(s - m) for s in ss]
            e_sink = jnp.exp(sink_row - m)
            rinv = 1.0 / (jnp.sum(ps[0], axis=0, keepdims=True) + jnp.sum(ps[1], axis=0, keepdims=True)
                          + jnp.sum(ps[2], axis=0, keepdims=True) + e_sink)
            ot = jnp.zeros((KV_WIDTH, GROUP * BLOCK), F32)
            for piece in range(3):
                pb = ps[piece].astype(MM)
                p_ref[0, i, piece] = pb
                ot = ot + _dot(vt_ref[idx[piece]], pb)
            o_ref[rows, :] = _unpad_heads(ot * rinv, kv)
            stat_ref[0, i] = jnp.where(stat_row == 0, rinv, e_sink * rinv)
            return carry

        for i in range(per_step):
            block(i, 0)

    both = pl.BlockSpec((nblk, BLOCK, KV_WIDTH), lambda k, b, j: (b, 0, 0))
    return _call(
        body, (sink, qtb, kb3, vtb, bias), name="attn_b_fwd", jobs=jobs,
        grid=(N_KV, batch, nstep),
        in_specs=[SMEM, pl.BlockSpec((nblk, GROUP * HEAD_DIM, BLOCK), lambda k, b, j: (b, k, 0)), both, both,
                  pl.BlockSpec((1, 3, BLOCK, GROUP * BLOCK), lambda k, b, j: (k, 0, 0, 0))],
        out_specs=[pl.BlockSpec((per_step * BLOCK, GROUP * HEAD_DIM), lambda k, b, j: (b * nstep + j, k)),
                   pl.BlockSpec((1, per_step, 3, BLOCK, GROUP * BLOCK), lambda k, b, j: (k, b * nstep + j, 0, 0, 0)),
                   pl.BlockSpec((1, per_step, 8, GROUP * BLOCK), lambda k, b, j: (k, b * nstep + j, 0, 0))],
        out_shape=[jax.ShapeDtypeStruct((tokens, Q_WIDTH), F32),
                   jax.ShapeDtypeStruct((N_KV, nblk_all, 3, BLOCK, GROUP * BLOCK), MM),
                   jax.ShapeDtypeStruct((N_KV, nblk_all, 8, GROUP * BLOCK), F32)],
        params=_cparams(("arbitrary", "arbitrary", "arbitrary"), 48))


def _attn_b_bwd(qtb, ktb, vb3, do, o, p, stat, *, seq, per_step, jobs=()):
    nblk_all = qtb.shape[0]
    tokens = nblk_all * BLOCK
    batch = tokens // seq
    nblk = seq // BLOCK
    nstep = nblk // per_step

    def body(q_ref, kt_ref, v_ref, do_ref, o_ref, p_ref, stat_ref,
             dq_ref, dk_ref, dv_ref, dbias_ref, dsink_ref):
        kv = pl.program_id(0)
        step = pl.program_id(2)
        first = step * per_step

        @pl.when(jnp.logical_and(pl.program_id(1) == 0, step == 0))
        def _():
            dbias_ref[...] = jnp.zeros_like(dbias_ref)
            dsink_ref[...] = jnp.zeros_like(dsink_ref)

        @pl.when(step == 0)
        def _():
            dk_ref[...] = jnp.zeros_like(dk_ref)
            dv_ref[...] = jnp.zeros_like(dv_ref)

        def block(i, dsink):
            n = first + i
            idx = (jnp.maximum(n - 1, 0), n, jnp.minimum(n + 1, nblk - 1))
            rows = slice(i * BLOCK, (i + 1) * BLOCK)
            qpad = _pad_heads(q_ref[n], kv)
            dot_t = do_ref[rows, :].T
            prod = dot_t * o_ref[rows, :].T
            delta = jnp.concatenate(
                [jnp.sum(prod[g * HEAD_DIM:(g + 1) * HEAD_DIM, :], axis=0, keepdims=True) for g in range(GROUP)],
                axis=-1)
            stats = stat_ref[0, i]
            rinv, p_sink = stats[0:1, :], stats[1:2, :]
            dopad32 = _pad_heads(dot_t, kv)
            dopad = dopad32.astype(MM)
            dopad_n = (dopad32 * rinv).astype(MM)
            dqt = jnp.zeros((KV_WIDTH, GROUP * BLOCK), F32)
            for piece in range(3):
                pb = p_ref[0, i, piece]
                dst = pb.astype(F32) * ((_dot(v_ref[idx[piece]], dopad) - delta) * rinv)
                dsb = dst.astype(MM)
                dbias_ref[0, piece] += dst
                dqt = dqt + _dot(kt_ref[idx[piece]], dsb)
                dk_ref[0, idx[piece]] += _dot_nt(dsb, qpad)
                dv_ref[0, idx[piece]] += _dot_nt(pb, dopad_n)
            dq_ref[rows, :] = _unpad_heads(dqt, kv)
            return dsink - p_sink * delta

        dsink = jnp.zeros((1, GROUP * BLOCK), F32)
        for i in range(per_step):
            dsink = block(i, dsink)
        dsink_ref[0] += jnp.broadcast_to(dsink, (8, GROUP * BLOCK))

    qspec = pl.BlockSpec((per_step * BLOCK, GROUP * HEAD_DIM), lambda k, b, j: (b * nstep + j, k))
    both = pl.BlockSpec((nblk, BLOCK, KV_WIDTH), lambda k, b, j: (b, 0, 0))
    grad =pl.BlockSpec((1, nblk, BLOCK, KV_WIDTH), lambda k, b, j: (k, b, 0, 0))
    return _call(
        body, (qtb, ktb, vb3, do, o, p, stat), name="attn_b_bwd", jobs=jobs,
        grid=(N_KV, batch, nstep),
        in_specs=[pl.BlockSpec((nblk, GROUP * HEAD_DIM, BLOCK), lambda k, b, j: (b, k, 0)), both, both,
                  qspec, qspec,
                  pl.BlockSpec((1, per_step, 3, BLOCK, GROUP * BLOCK), lambda k, b, j: (k, b * nstep + j, 0, 0, 0)),
                  pl.BlockSpec((1, per_step, 8, GROUP * BLOCK), lambda k, b, j: (k, b * nstep + j, 0, 0))],
        out_specs=[qspec, grad, grad,
                   pl.BlockSpec((1, 3, BLOCK, GROUP * BLOCK), lambda k, b, j: (k, 0, 0, 0)),
                   pl.BlockSpec((1, 8, GROUP * BLOCK), lambda k, b, j: (k, 0, 0))],
        out_shape=[jax.ShapeDtypeStruct((tokens, Q_WIDTH), F32),
                   jax.ShapeDtypeStruct((N_KV, nblk_all, BLOCK, KV_WIDTH), F32),
                   jax.ShapeDtypeStruct((N_KV, nblk_all, BLOCK, KV_WIDTH), F32),
                   jax.ShapeDtypeStruct((N_KV, 3, BLOCK, GROUP * BLOCK), F32),
                   jax.ShapeDtypeStruct((N_KV, 8, GROUP * BLOCK), F32)],
        params=_cparams(("arbitrary", "arbitrary", "arbitrary"), 56))


def _resident(shape):
    return pl.BlockSpec(shape, lambda i: (0,) * len(shape), pipeline_mode=pl.Buffered(1))


def _mix_ffn_fwd(oa, ob, w_o, x, g2, g3, w_up, w_down, target, g4, *, tm):
    tokens = x.shape[0]
    nt = tokens // tm

    def body(oa_ref, ob_ref, wo_ref, x_ref, g2_ref, g3_ref, wu_ref, wd_ref, t_ref, g4_ref,
             mix_ref, x1_ref, h2_ref, o_ref, u_ref, df_ref, dy_ref, loss_ref, dg4_ref):
        o = jnp.concatenate([oa_ref[...].astype(MM), ob_ref[...].astype(MM)], axis=-1)
        o_ref[...] = o
        mix = _dot(o, wo_ref[...])
        mix_ref[...] = mix
        x1 = x_ref[...] + mix * _rms_r(mix) * g2_ref[...]
        x1_ref[...] = x1
        h2v = (x1 * _rms_r(x1) * g3_ref[...]).astype(MM)
        h2_ref[...] = h2v
        f = jnp.zeros((tm, D_MODEL), F32)
        for c in range(N_CHIPS):
            u = jnp.maximum(_dot(h2v, wu_ref[c]), 0.0)
            u_ref[:, c * FF_CHUNK:(c + 1) * FF_CHUNK] = u.astype(MM)
            f = f + _dot((u * u).astype(MM), wd_ref[c * FF_CHUNK:(c + 1) * FF_CHUNK, :])
        r = _rms_r(f)
        g4v = g4_ref[...]
        err = x1 + f * r * g4v - t_ref[...]
        sq = jnp.sum(err * err, axis=-1, keepdims=True)
        loss_ref[0] = jnp.broadcast_to(jnp.sum(sq, axis=0, keepdims=True) * (0.5 / D_MODEL), (8, LANES))
        dy = err * (1.0 / D_MODEL)
        dy_ref[...] = dy
        dfv, dgv = _rms_bwd(f, r, g4v, dy)
        df_ref[...] = dfv.astype(MM)
        dg4_ref[0] = jnp.sum(dgv, axis=0, keepdims=True)

    tok = pl.BlockSpec((tm, D_MODEL), lambda i: (i, 0))
    half = pl.BlockSpec((tm, Q_WIDTH), lambda i: (i, 0))
    row = pl.BlockSpec((1, D_MODEL), lambda i: (0, 0))
    tok_f32 = jax.ShapeDtypeStruct((tokens, D_MODEL), F32)
    tok_mm = jax.ShapeDtypeStruct((tokens, D_MODEL), MM)
    return pl.pallas_call(
        body, name="mix_ffn_fwd",
        grid=(nt,),
        in_specs=[half, half, _resident((D_MODEL, D_MODEL)), tok, row, row,
                  _resident((N_CHIPS, D_MODEL, FF_CHUNK)), _resident((D_FF, D_MODEL)), tok, row],
        out_specs=[tok, tok, tok, tok, pl.BlockSpec((tm, D_FF), lambda i: (i, 0)), tok, tok,
                   pl.BlockSpec((1, 8, LANES), lambda i: (i, 0, 0)),
                   pl.BlockSpec((1, 1, D_MODEL), lambda i: (i, 0, 0))],
        out_shape=[tok_f32,
                   tok_f32,
                   tok_mm,
                   tok_mm,
                   jax.ShapeDtypeStruct((tokens, D_FF), MM),
                   tok_mm,
                   tok_f32,
                   jax.ShapeDtypeStruct((nt, 8, LANES), F32),
                   jax.ShapeDtypeStruct((nt, 1, D_MODEL), F32)],
        compiler_params=_cparams(("parallel",), 56),
    )(*map(_from_hbm, (oa, ob, w_o, x, g2, g3, w_up, w_down, target, g4)))


def _ffn_bwd_act(df, w_down, u, w_up, x1, dy, mix, g3, g2, w_o, *, tm):
    tokens = df.shape[0]
    nt = tokens // tm

    def body(df_ref, wd_ref, u_ref, wu_ref, x1_ref, dy_ref, mix_ref, g3_ref, g2_ref, wo_ref,
             dz_ref, dx1_ref, dmix_ref, dg3_ref, dg2_ref, doa_ref, dob_ref):
        dfv = df_ref[...]
        dh2 = jnp.zeros((tm, D_MODEL), F32)
        for c in range(N_CHIPS):
            cols = slice(c * FF_CHUNK, (c + 1) * FF_CHUNK)
            da = _dot_nt(dfv, wd_ref[cols, :])
            dz = (da * (2.0 * u_ref[:, cols].astype(F32))).astype(MM)
            dz_ref[:, cols] = dz
            dh2 = dh2 + _dot_nt(dz, wu_ref[c])
        x1 = x1_ref[...]
        dxn, dg3v = _rms_bwd(x1, _rms_r(x1), g3_ref[...], dh2)
        dx1 = dy_ref[...] + dxn
        dx1_ref[...] = dx1
        dg3_ref[0] = jnp.sum(dg3v, axis=0, keepdims=True)
        mix = mix_ref[...]
        dmix, dg2v = _rms_bwd(mix, _rms_r(mix), g2_ref[...], dx1)
        dmb = dmix.astype(MM)
        dmix_ref[...] = dmb
        dg2_ref[0] = jnp.sum(dg2v, axis=0, keepdims=True)
        doa_ref[...] = _dot_nt(dmb, wo_ref[0:Q_WIDTH, :])
        dob_ref[...] = _dot_nt(dmb, wo_ref[Q_WIDTH:D_MODEL, :])

    tok = pl.BlockSpec((tm, D_MODEL), lambda i: (i, 0))
    half = pl.BlockSpec((tm, Q_WIDTH), lambda i: (i, 0))
    wide = pl.BlockSpec((tm, D_FF), lambda i: (i, 0))
    row = pl.BlockSpec((1, D_MODEL), lambda i: (0, 0))
    part = pl.BlockSpec((1, 1, D_MODEL), lambda i: (i, 0, 0))
    return pl.pallas_call(
        body, name="ffn_bwd_act",
        grid=(nt,),
        in_specs=[tok, _resident((D_FF, D_MODEL)), wide, _resident((N_CHIPS, D_MODEL, FF_CHUNK)),
                  tok, tok, tok, row, row, _resident((D_MODEL, D_MODEL))],
        out_specs=[wide, tok, tok, part, part, half, half],
        out_shape=[jax.ShapeDtypeStruct((tokens, D_FF), MM),
                   jax.ShapeDtypeStruct((tokens, D_MODEL), F32),
                   jax.ShapeDtypeStruct((tokens, D_MODEL), MM),
                   jax.ShapeDtypeStruct((nt, 1, D_MODEL), F32),
                   jax.ShapeDtypeStruct((nt, 1, D_MODEL), F32),
                   jax.ShapeDtypeStruct((tokens, Q_WIDTH), F32),
                   jax.ShapeDtypeStruct((tokens, Q_WIDTH), F32)],
        compiler_params=_cparams(("parallel",), 56),
    )(*map(_from_hbm, (df, w_down, u, w_up, x1, dy, mix, g3, g2, w_o)))


def _tn_matmul(a, b, *, name, tm, tn, tk, chunk=None, square_a=False, vmem_mb=48, jobs=()):
    tokens, m_dim = a.shape
    n_dim = b.shape[1]
    chunked = chunk is not None
    if chunked:
        assert tm == m_dim and tn % chunk == 0

    def body(a_ref, b_ref, o_ref):
        av = a_ref[...]
        if square_a:
            av = av.astype(F32)
            av = av * av
        part = _dot_tn(av.astype(MM), b_ref[...].astype(MM))
        if chunked:
            part = jnp.stack([part[:, c * chunk:(c + 1) * chunk] for c in range(tn // chunk)])

        @pl.when(pl.program_id(2) == 0)
        def _():
            o_ref[...] = part

        @pl.when(pl.program_id(2) > 0)
        def _():
            o_ref[...] += part

    if chunked:
        out_spec = pl.BlockSpec((tn // chunk, tm, chunk), lambda i, j, k: (j, 0, 0))
        out_shape = jax.ShapeDtypeStruct((n_dim // chunk, m_dim, chunk), F32)
    else:
        out_spec = pl.BlockSpec((tm, tn), lambda i, j, k: (i, j))
        out_shape = jax.ShapeDtypeStruct((m_dim, n_dim), F32)
    (out,), job_res = _call(
        body, (a, b), name=name, jobs=jobs,
        grid=(m_dim // tm, n_dim // tn, tokens // tk),
        in_specs=[pl.BlockSpec((tk, tm), lambda i, j, k: (k, i)),
                  pl.BlockSpec((tk, tn), lambda i, j, k: (k, j))],
        out_specs=[out_spec], out_shape=[_in_hbm(out_shape)],
        params=_cparams(("arbitrary", "arbitrary", "arbitrary"), vmem_mb))
    return out, job_res


def _proj_bwd(dqa, dkta, dvta, dqb, dktb, dvtb, raw, x, dx1, g1, w_in, gq, gk, ck, sk, *, seq, tm, sub, jobs=()):
    tokens = x.shape[0]
    nt = tokens // tm
    n_seq = seq // tm
    nblk = tm // BLOCK

    def body(dqa_ref, dkta_ref, dvta_ref, dqb_ref, dkb_ref, dvb_ref, raw_ref, x_ref, dx1_ref, g1_ref, w_ref,
             gq_ref, gk_ref, ck_ref, sk_ref,
             gx_ref, dproj_ref, dg1_ref, dgq_ref, dgk_ref, dp):
        parts = []
        for r in range(tm // sub):
            rows = slice(r * sub, (r + 1) * sub)
            qa = raw_ref[rows, 0:Q_WIDTH]
            dqn = _rope_t(dqa_ref[rows, :], ck_ref[rows, :], sk_ref[rows, :]) * SCALE
            rq = _head_r(qa)
            nq = qa * rq
            dnq = dqn * gq_ref[...]
            dp[rows, 0:Q_WIDTH] = rq * (dnq - nq * (_seg64_sum(dnq * nq) * (1.0 / HEAD_DIM)))

            ka = raw_ref[rows, Q_WIDTH:QK_RAW]
            dkn = _rope_t(dkta_ref[0, :, rows].T, ck_ref[rows, :], sk_ref[rows, :])
            rk = _head_r(ka)
            nk = ka * rk
            dnk = dkn * gk_ref[...]
            dp[rows, 512:640] = rk * (dnk - nk * (_seg64_sum(dnk * nk) * (1.0 / HEAD_DIM)))

            dp[rows, 640:768] = dvta_ref[0, :, rows].T
            dp[rows, 768:1280] = dqb_ref[rows, :] * SCALE
            for j in range(r * sub // BLOCK, (r + 1) * sub // BLOCK):
                dp[j * BLOCK:(j + 1) * BLOCK, 1280:1408] = dkb_ref[0, j] + dkb_ref[1, j]
                dp[j * BLOCK:(j + 1) * BLOCK, 1408:1536] = dvb_ref[0, j] + dvb_ref[1, j]

            dproj = dp[rows, :].astype(MM)
            dproj_ref[rows, :] = dproj
            dh1 = _dot_nt(dproj[:, 0:IN_CHUNK], w_ref[0])
            for j in range(1, N_CHIPS):
                dh1 = dh1 + _dot_nt(dproj[:, j * IN_CHUNK:(j + 1) * IN_CHUNK], w_ref[j])
            xv = x_ref[rows, :]
            dxn, dg1v = _rms_bwd(xv, _rms_r(xv), g1_ref[...], dh1)
            gx_ref[rows, :] = dx1_ref[rows, :] + dxn
            parts.append((jnp.sum(dqn * nq, axis=0, keepdims=True), jnp.sum(dkn * nk, axis=0, keepdims=True),
                          jnp.sum(dg1v, axis=0, keepdims=True)))
        dgq_ref[0] = functools.reduce(jnp.add, [p[0] for p in parts])
        dgk_ref[0] = functools.reduce(jnp.add, [p[1] for p in parts])
        dg1_ref[0] = functools.reduce(jnp.add, [p[2] for p in parts])

    tok = lambda w: pl.BlockSpec((tm, w), lambda i: (i, 0))
    tab = lambda w: pl.BlockSpec((tm, w), lambda i: (i % n_seq, 0))
    row = lambda w: pl.BlockSpec((1, w), lambda i: (0, 0))
    tposed = pl.BlockSpec((1, KV_WIDTH, tm), lambda i: (i // n_seq, 0, i % n_seq))
    blocks = pl.BlockSpec((N_KV, nblk, BLOCK, KV_WIDTH), lambda i: (0, i, 0, 0))
    part = lambda w: pl.BlockSpec((1, 1, w), lambda i: (i, 0, 0))
    return _call(
        body, (dqa, dkta, dvta, dqb, dktb, dvtb, raw, x, dx1, g1, w_in, gq, gk, ck, sk),
        name="proj_bwd", jobs=jobs,
        grid=(nt,),
        in_specs=[tok(Q_WIDTH), tposed, tposed, tok(Q_WIDTH), blocks, blocks, tok(QK_RAW), tok(D_MODEL),
                  tok(D_MODEL), row(D_MODEL),
                  pl.BlockSpec((N_CHIPS, D_MODEL, IN_CHUNK), lambda i: (0, 0, 0)),
                  row(Q_WIDTH), row(KV_WIDTH), tab(KV_WIDTH), tab(KV_WIDTH)],
        out_specs=[tok(D_MODEL), tok(IN_TOTAL), part(D_MODEL), part(Q_WIDTH), part(KV_WIDTH)],
        out_shape=[jax.ShapeDtypeStruct((tokens, D_MODEL), F32),
                   jax.ShapeDtypeStruct((tokens, IN_TOTAL), MM),
                   jax.ShapeDtypeStruct((nt, 1, D_MODEL), F32),
                   jax.ShapeDtypeStruct((nt, 1, Q_WIDTH), F32),
                   jax.ShapeDtypeStruct((nt, 1, KV_WIDTH), F32)],
        scratch_shapes=[pltpu.VMEM((tm, IN_TOTAL), F32)],
        params=_cparams(("arbitrary",), 56))


def _pack_small(dg1, dg2, dg3, dg4, dgq, dgk, dsink, dbias, bucket, loss):
    def body(dg1_ref, dg2_ref, dg3_ref, dg4_ref, dgq_ref, dgk_ref, dsink_ref, dbias_ref, bucket_ref, loss_ref,
             out_ref, rel_ref):
        out_ref[...] = jnp.zeros_like(out_ref)
        for r, ref in ((ROW_G1, dg1_ref), (ROW_G2, dg2_ref), (ROW_G3, dg3_ref), (ROW_G4, dg4_ref)):
            acc = ref[0]
            for t in range(1, ref.shape[0]):
                acc = acc + ref[t]
            out_ref[r:r + 1, :] = acc

        def fold(ref, heads):
            acc = ref[0]
            for t in range(1, ref.shape[0]):
                acc = acc + ref[t]
            tot = acc[:, 0:HEAD_DIM]
            for h in range(1, heads):
                tot = tot + acc[:, h * HEAD_DIM:(h + 1) * HEAD_DIM]
            return tot

        out_ref[ROW_MISC:ROW_MISC + 1, MISC_GQ:MISC_GQ + HEAD_DIM] = fold(dgq_ref, GROUP * N_KV)
        out_ref[ROW_MISC:ROW_MISC + 1, MISC_GK:MISC_GK + HEAD_DIM] = fold(dgk_ref, N_KV)
        for h in range(GROUP * N_KV):
            g = h % GROUP
            out_ref[ROW_MISC:ROW_MISC + 1, MISC_SINK + h:MISC_SINK + h + 1] = jnp.sum(
                dsink_ref[h // GROUP, 0:1, g * BLOCK:(g + 1) * BLOCK], axis=-1, keepdims=True)
        lacc = loss_ref[0, 0:1, 0:1]
        for t in range(1, loss_ref.shape[0]):
            lacc = lacc + loss_ref[t, 0:1, 0:1]
        out_ref[ROW_MISC:ROW_MISC + 1, MISC_LOSS:MISC_LOSS + 1] = lacc
        lane = lax.broadcasted_iota(jnp.int32, (N_BUCKETS, LANES), 1)
        row = lax.broadcasted_iota(jnp.int32, (N_BUCKETS, LANES), 0)

        def per_bucket(b, acc):
            for h in range(GROUP * N_KV):
                g = h % GROUP
                sel = jnp.zeros((BLOCK, BLOCK), F32)
                for piece in range(3):
                    sel = sel + jnp.where(bucket_ref[piece] == b,
                                          dbias_ref[h // GROUP, piece, :, g * BLOCK:(g + 1) * BLOCK], 0.0)
                tot = jnp.sum(jnp.sum(sel, axis=0, keepdims=True), axis=-1, keepdims=True)
                acc = jnp.where((row == b) & (lane == h), tot, acc)
            return acc

        rel_ref[...] = lax.fori_loop(0, N_BUCKETS, per_bucket, jnp.zeros((N_BUCKETS, LANES), F32))

    args = (dg1, dg2, dg3, dg4, dgq, dgk, dsink, dbias, bucket, loss)
    outs = [jax.ShapeDtypeStruct((8, D_MODEL), F32), jax.ShapeDtypeStruct((N_BUCKETS, LANES), F32)]
    return pl.pallas_call(
        body, name="pack_small", grid=(1,),
        in_specs=[_whole(a) for a in args], out_specs=[_whole(o) for o in outs], out_shape=outs,
        compiler_params=pltpu.CompilerParams(vmem_limit_bytes=32 * 1024 * 1024),
    )(*map(_from_hbm, args))


def _gather_weights(shards, whole):
    n = len(shards)
    full = [t for t in range(n) if whole[t]]

    def body(*refs):
        ins, outs = refs[:n], refs[n:2 * n]
        raw, stage = refs[2 * n:3 * n], refs[3 * n:4 * n]
        load_sem, local_sem, ici_send, ici_recv, d2d_send, d2d_recv = refs[4 * n:]
        x, y, c = _place()
        k = 2 * x + y
        sibling = (x, y, 1 - c)
        order = full + [t for t in range(n) if t not in full]
        loads = {t: pltpu.make_async_copy(ins[t], raw[t], load_sem.at[t]) for t in order}
        for t in order:
            loads[t].start()
        copies, sends = [], []
        for t in order:
            loads[t].wait()
            stage[t][...] = raw[t][...].astype(MM)
            mine = pltpu.make_async_copy(stage[t], outs[t].at[k], local_sem.at[t])
            mine.start()
            copies.append(mine)
            if t in full:
                half = ins[t].shape[0] // 2
                rows = pl.ds(c * half, half)
                for r, (fx, fy) in enumerate(_CHIP_FLIPS):
                    cp = _remote(stage[t].at[rows], outs[t].at[k, rows], ici_send.at[t, r], ici_recv.at[t, r],
                                 (_flip(x, fx), _flip(y, fy), c))
                    cp.start()
                    sends.append(cp)
        for t in full:
            half = ins[t].shape[0] // 2
            rows = pl.ds(c * half, half)
            for r, (fx, fy) in enumerate(_CHIP_FLIPS):
                kk = 2 * _flip(x, fx) + _flip(y, fy)
                landed = outs[t].at[kk, rows]
                _remote(landed, landed, ici_send.at[t, r], ici_recv.at[t, r], sibling).wait_recv()
                fwd = _remote(landed, landed, d2d_send.at[t, r], d2d_recv.at[t, r], sibling)
                fwd.start()
                sends.append(fwd)
        for t in full:
            half = ins[t].shape[0] // 2
            other = pl.ds((1 - c) * half, half)
            for r, (fx, fy) in enumerate(_CHIP_FLIPS):
                kk = 2 * _flip(x, fx) + _flip(y, fy)
                theirs = outs[t].at[kk, other]
                _remote(theirs, theirs, d2d_send.at[t, r], d2d_recv.at[t, r], sibling).wait_recv()
        for cp in sends:
            cp.wait_send()
        for cp in copies:
            cp.wait()

    return pl.pallas_call(
        body, name="gather_weights",
        in_specs=[HBM] * n, out_specs=[HBM] * n,
        out_shape=[pltpu.HBM((N_CHIPS,) + s.shape, MM) for s in shards],
        scratch_shapes=[pltpu.VMEM(s.shape, F32) for s in shards] + [pltpu.VMEM(s.shape, MM) for s in shards] + [
            pltpu.SemaphoreType.DMA((n,)), pltpu.SemaphoreType.DMA((n,)),
            pltpu.SemaphoreType.DMA((n, 3)), pltpu.SemaphoreType.DMA((n, 3)),
            pltpu.SemaphoreType.DMA((n, 3)), pltpu.SemaphoreType.DMA((n, 3))],
        compiler_params=pltpu.CompilerParams(vmem_limit_bytes=40 * 1024 * 1024),
    )(*shards)


def _add_half(grad, got, where, *, name, tr):
    nch, half, cols = got.shape
    nblk = half // tr

    def body(where_ref, g_ref, r_ref, o_ref):
        o_ref[...] = (g_ref[...] + r_ref[...]).astype(MM)

    return pl.pallas_call(
        body, name=name,
        grid_spec=pltpu.PrefetchScalarGridSpec(
            num_scalar_prefetch=1, grid=(nch, nblk),
            in_specs=[pl.BlockSpec((1, tr, cols), lambda j, i, where_ref: (j, where_ref[1] * nblk + i, 0)),
                      pl.BlockSpec((1, tr, cols), lambda j, i, where_ref: (j, i, 0))],
            out_specs=pl.BlockSpec((1, tr, cols), lambda j, i, where_ref: (j, i, 0))),
        out_shape=jax.ShapeDtypeStruct(got.shape, MM),
        compiler_params=_cparams(("parallel", "parallel"), 32),
    )(where, grad, got)


def _add_chips(own, got, where, *, name, tr):
    _, half, cols = own.shape
    nblk = half // tr

    def body(where_ref, o_ref, g_ref, out_ref):
        f = lambda v: v.astype(F32)
        out_ref[...] = ((f(o_ref[0]) + f(g_ref[0])) + f(g_ref[1])) + f(g_ref[2])

    return pl.pallas_call(
        body, name=name,
        grid_spec=pltpu.PrefetchScalarGridSpec(
            num_scalar_prefetch=1, grid=(nblk,),
            in_specs=[pl.BlockSpec((1, tr, cols), lambda i, where_ref: (where_ref[0], i, 0)),
                      pl.BlockSpec((3, tr, cols), lambda i, where_ref: (0, i, 0))],
            out_specs=pl.BlockSpec((tr, cols), lambda i, where_ref: (where_ref[1] * nblk + i, 0))),
        out_shape=pltpu.HBM((2 * half, cols), F32),
        compiler_params=_cparams(("parallel",), 32),
    )(where, own, got)


def _small_job(tiles):
    n = len(tiles)

    def copies(ins, outs, sems):
        x, y, c = _place()
        me = 4 * x + 2 * y + c
        local, send, recv = sems
        cps = []
        for t in range(n):
            cps.append(pltpu.make_async_copy(ins[t], outs[t].at[me], local.at[t]))
            for r in range(1, N_DEV):
                fx, fy, fc = (r >> 2) & 1, (r >> 1) & 1, r & 1
                cps.append(_remote(ins[t], outs[t].at[me], send.at[t, r - 1], recv.at[t, r - 1],
                                   (_flip(x, fx), _flip(y, fy), _flip(c, fc))))
        return cps

    return _Job(tiles, [jax.ShapeDtypeStruct((N_DEV,) + t.shape, F32) for t in tiles],
                [pltpu.SemaphoreType.DMA((n,)), pltpu.SemaphoreType.DMA((n, N_DEV - 1)),
                 pltpu.SemaphoreType.DMA((n, N_DEV - 1))], copies)


def _adamw_math(w, g, m, v):
    m = ADAM_B1 * m + (1.0 - ADAM_B1) * g
    v = ADAM_B2 * v + (1.0 - ADAM_B2) * (g * g)
    m_hat = m / (1.0 - ADAM_B1 ** ADAM_STEP)
    v_hat = v / (1.0 - ADAM_B2 ** ADAM_STEP)
    delta = -ADAM_LR * (m_hat / (jnp.sqrt(v_hat) + ADAM_EPS) + ADAM_WD * w)
    return delta, m, v


def _adamw(w, g, m, v, *, name, tr):
    rows, cols = w.shape

    def body(w_ref, g_ref, m_ref, v_ref, go_ref, d_ref, nm_ref, nv_ref):
        g = g_ref[...]
        go_ref[...] = g
        d_ref[...], nm_ref[...], nv_ref[...] = _adamw_math(w_ref[...], g, m_ref[...], v_ref[...])

    spec = pl.BlockSpec((tr, cols), lambda i: (i, 0))
    return pl.pallas_call(
        body, name=name,
        grid=(rows // tr,),
        in_specs=[spec] * 4, out_specs=[spec] * 4,
        out_shape=[jax.ShapeDtypeStruct(w.shape, F32)] * 4,
        compiler_params=_cparams(("parallel",), 32),
    )(w, g, m, v)


def _small_adamw(gathered, gathered_rel, params, moments_m, moments_v):
    n = len(params)

    def body(all_ref, rel_all_ref, *refs):
        w_refs, m_refs, v_refs = refs[:n], refs[n:2 * n], refs[2 * n:3 * n]
        loss_ref = refs[3 * n]
        out_refs = refs[3 * n + 1:]
        g = all_ref[0]
        rel = rel_all_ref[0]
        for d in range(1, N_DEV):
            g = g + all_ref[d]
            rel = rel + rel_all_ref[d]
        misc = g[ROW_MISC:ROW_MISC + 1]
        loss_ref[...] = misc[:, MISC_LOSS:MISC_LOSS + 1]
        grads = (g[ROW_G1:ROW_G1 + 1], g[ROW_G2:ROW_G2 + 1], g[ROW_G3:ROW_G3 + 1], g[ROW_G4:ROW_G4 + 1],
                 misc[:, MISC_GQ:MISC_GQ + HEAD_DIM], misc[:, MISC_GK:MISC_GK + HEAD_DIM],
                 misc[:, MISC_SINK:MISC_SINK + GROUP * N_KV], rel[:, 0:GROUP * N_KV])
        for i in range(n):
            d, nm, nv = _adamw_math(w_refs[i][...], grads[i], m_refs[i][...], v_refs[i][...])
            for j, val in enumerate((grads[i], d, nm, nv)):
                out_refs[4 * i + j][...] = val

    args = (gathered, gathered_rel, *params, *moments_m, *moments_v)
    out_shape = [jax.ShapeDtypeStruct((1, 1), F32)] + [jax.ShapeDtypeStruct(p.shape, F32) for p in params
                                                       for _ in range(4)]
    outs = pl.pallas_call(
        body, name="small_adamw", grid=(1,),
        in_specs=[_whole(a) for a in args], out_specs=[_whole(o) for o in out_shape], out_shape=out_shape,
    )(*map(_from_hbm, args))
    return outs[0], [outs[1 + 4 * i:5 + 4 * i] for i in range(n)]


def kernel(x, w_in, w_o, g_pre_mix, g_post_mix, q_norm_a, k_norm_a, sink_b, rel_bias, g_pre_ffn, w_ffn_up, w_ffn_down, g_post_ffn, loss_target, m_w_in, m_w_o, m_g_pre_mix, m_g_post_mix, m_q_norm_a, m_k_norm_a, m_sink_b, m_rel_bias, m_g_pre_ffn, m_w_ffn_up, m_w_ffn_down, m_g_post_ffn, v_w_in, v_w_o, v_g_pre_mix, v_g_post_mix, v_q_norm_a, v_k_norm_a, v_sink_b, v_rel_bias, v_g_pre_ffn, v_w_ffn_up, v_w_ffn_down, v_g_post_ffn):
    batch, seq, _ = x.shape
    tokens = batch * seq
    where = jnp.stack([2 * lax.axis_index("x") + lax.axis_index("y"), lax.axis_index("c")]).astype(jnp.int32)
    x2 = x.reshape(tokens, D_MODEL)
    g1, g2, g3, g4 = g_pre_mix, g_post_mix, g_pre_ffn, g_post_ffn

    cos, sin = _rope_tables(seq)
    ck, sk = jnp.tile(cos, (1, 2)), jnp.tile(sin, (1, 2))
    gq8, gk2 = jnp.tile(q_norm_a, (1, 8)), jnp.tile(k_norm_a, (1, 2))
    bucket, band = _window_tables()
    bias = _bias_build(rel_bias.T, bucket, band)

    w_in_g, w_o_p, w_up_p, w_down_p = _gather_weights(
        (w_in[0], w_o[0], w_ffn_up[0], w_ffn_down[0]), whole=(True, False, False, False))
    (h1, raw, qa, ka, kta, va, vta, qtb, kb, ktb, vb, vtb) = _pre_proj(
        x2, g1, w_in_g, gq8, gk2, ck, sk, seq=seq, tm=min(512, seq), sub=256)
    (oa, p_a, linv_a), (w_part,) = _attn_a_fwd(
        qa, kta, va, seq=seq, bq=min(256, seq), jobs=[_gather_job([w_o_p, w_up_p, w_down_p], forward=False)])
    kb3 = kb.reshape(tokens // BLOCK, BLOCK, KV_WIDTH)
    vb3 = vb.reshape(tokens // BLOCK, BLOCK, KV_WIDTH)
    (ob, p_b, stat_b), ((w_o_g, w_up_g, w_down_g),) = _attn_b_fwd(
        qtb, kb3, vtb, bias, sink_b, seq=seq, per_step=min(16, seq // BLOCK),
        jobs=[_gather_job(w_part, forward=True)])
    w_o2 = w_o_g.reshape(D_MODEL, D_MODEL)
    w_down2 = w_down_g.reshape(D_FF, D_MODEL)
    mix, x1, h2, o_cat, u, df, dy, loss_t, dg4 = _mix_ffn_fwd(
        oa, ob, w_o2, x2, g2, g3, w_up_g, w_down2, loss_target.reshape(tokens, D_MODEL), g4, tm=256)

    dz, dx1, dmix, dg3, dg2, doa, dob = _ffn_bwd_act(df, w_down2, u, w_up_g, x1, dy, mix, g3, g2, w_o2, tm=256)
    gw_down, _ = _tn_matmul(u, df, name="grad_w_down", tm=1024, tn=1024, tk=min(2048, tokens), square_a=True)
    gw_down = gw_down.reshape(N_CHIPS, FF_CHUNK, D_MODEL)
    gw_up, ((got_down,),) = _tn_matmul(h2, dz, name="grad_w_up", tm=1024, tn=1024, tk=min(2048, tokens), chunk=FF_CHUNK,
                                        jobs=[_swap_job([gw_down])])
    gw_o, _ = _tn_matmul(o_cat, dmix, name="grad_w_o", tm=1024, tn=1024, tk=min(2048, tokens))
    gw_o = gw_o.reshape(N_CHIPS, O_CHUNK, D_MODEL)
    sum_down = _add_half(gw_down, got_down, where, name="add_half_w_down", tr=128)
    (dqa, dkta, dvta), ((ex_down,), (got_up,)) = _attn_a_bwd(
        qa, ka, vta, doa, oa, p_a, linv_a, seq=seq, bq=min(256, seq),
        jobs=[_exchange_job([sum_down]), _swap_job([gw_up])])
    full_down = _add_chips(sum_down, ex_down, where, name="add_chips_w_down", tr=128)
    sum_up = _add_half(gw_up, got_up, where, name="add_half_w_up", tr=128)
    (dqb, dkb, dvb, dbias, dsink), ((ex_up,), (g_down,), (got_o,)) = _attn_b_bwd(
        qtb, ktb, vb3, dob, ob, p_b, stat_b, seq=seq, per_step=min(16, seq // BLOCK),
        jobs=[_exchange_job([sum_up]), _join_job([full_down]), _swap_job([gw_o])])
    full_up = _add_chips(sum_up, ex_up, where, name="add_chips_w_up", tr=128)
    sum_o = _add_half(gw_o, got_o, where, name="add_half_w_o", tr=128)
    (grad_x, dproj, dg1, dgq, dgk), _ = _proj_bwd(
        dqa, dkta, dvta, dqb, dkb, dvb, raw, x2, dx1, g1, w_in_g, gq8, gk2, ck, sk,
        seq=seq, tm=min(512, seq), sub=128)
    packed, packed_rel = _pack_small(dg1, dg2, dg3, dg4, dgq, dgk, dsink, dbias, bucket, loss_t)
    gw_in, ((ex_o,), (g_up,), (gathered, gathered_rel)) = _tn_matmul(
        h1, dproj, name="grad_w_in", tm=1024, tn=2 * IN_CHUNK, tk=min(2048, tokens), chunk=IN_CHUNK,
        jobs=[_exchange_job([sum_o]), _join_job([full_up]), _small_job([packed, packed_rel])])
    full_o = _add_chips(sum_o, ex_o, where, name="add_chips_w_o", tr=128)

    (g_o,), (got_in,) = _run_jobs("tail_swap", [_join_job([full_o]), _swap_job([gw_in])])
    sum_in = _add_half(gw_in, got_in, where, name="add_half_w_in", tr=128)
    ((ex_in,),) = _run_jobs("tail_exchange", [_exchange_job([sum_in])])
    full_in = _add_chips(sum_in, ex_in, where, name="add_chips_w_in", tr=128)
    ((g_in,),) = _run_jobs("tail_join", [_join_job([full_in])])

    big = [[t[None] for t in _adamw(w[0], g, m[0], v[0], name="adamw_" + nm, tr=128)] for nm, w, g, m, v in (
        ("w_in", w_in, g_in, m_w_in, v_w_in), ("w_o", w_o, g_o, m_w_o, v_w_o),
        ("w_up", w_ffn_up, g_up, m_w_ffn_up, v_w_ffn_up), ("w_down", w_ffn_down, g_down, m_w_ffn_down, v_w_ffn_down))]

    loss, small = _small_adamw(
        gathered, gathered_rel,
        (g1, g2, g3, g4, q_norm_a, k_norm_a, sink_b, rel_bias),
        (m_g_pre_mix, m_g_post_mix, m_g_pre_ffn, m_g_post_ffn, m_q_norm_a, m_k_norm_a, m_sink_b, m_rel_bias),
        (v_g_pre_mix, v_g_post_mix, v_g_pre_ffn, v_g_post_ffn, v_q_norm_a, v_k_norm_a, v_sink_b, v_rel_bias))
    s_g1, s_g2, s_g3, s_g4, s_gq, s_gk, s_sink, s_rel = small

    def leaves(i):
        return (big[0][i], big[1][i], s_g1[i], s_g2[i], s_gq[i], s_gk[i], s_sink[i], s_rel[i], s_g3[i],
                big[2][i], big[3][i], s_g4[i])

    loss = loss.reshape(())
    return (loss, grad_x.reshape(batch, seq, D_MODEL), *leaves(0), *leaves(1), *leaves(2), *leaves(3))
```

```python
import functools

import jax
import jax.numpy as jnp
import numpy as np
from jax import lax
from jax.experimental import pallas as pl
from jax.experimental.pallas import tpu as pltpu

F32 = jnp.float32
MM = jnp.bfloat16

D_MODEL = 1024
HEAD_DIM = 64
N_KV = 2
GROUP = 4
Q_WIDTH = 512
KV_WIDTH = 128
D_FF = 4096
GRID_W = 64
BLOCK = 128
N_BUCKETS = 32
MAX_DISTANCE = 128
ROPE_THETA = 10000.0
EPS = 1e-6
NEG_INF = -1e30
SCALE = HEAD_DIM ** -0.5
IN_TOTAL = 1536
N_CHIPS = 4
N_DEV = 8
IN_CHUNK = IN_TOTAL // N_CHIPS
FF_CHUNK = D_FF // N_CHIPS
O_CHUNK = D_MODEL // N_CHIPS
QK_RAW = 640

ADAM_LR = 0.001
ADAM_B1 = 0.9
ADAM_B2 = 0.999
ADAM_EPS = 1e-08
ADAM_WD = 0.01
ADAM_STEP = 10

LANES = 128
MESH = pl.DeviceIdType.MESH
HBM = pl.BlockSpec(memory_space=pl.ANY)
SMEM = pl.BlockSpec(memory_space=pltpu.SMEM)

ROW_G1, ROW_G2, ROW_G3, ROW_G4, ROW_MISC = 0, 1, 2, 3, 4
MISC_GQ, MISC_GK, MISC_SINK, MISC_LOSS = 0, 64, 128, 512


def _cparams(sem, vmem_mb):
    return pltpu.CompilerParams(dimension_semantics=sem, vmem_limit_bytes=vmem_mb * 1024 * 1024)


def _whole(a):
    return pl.BlockSpec(a.shape, lambda i: (0,) * len(a.shape))


def _from_hbm(a):
    return pltpu.with_memory_space_constraint(a, pltpu.HBM)


def _in_hbm(s):
    return pltpu.HBM(s.shape, s.dtype)


class _Job:
    def __init__(self, operands, out_shapes, sems, copies, alias=None):
        self.operands, self.out_shapes, self.sems, self.copies = list(operands), list(out_shapes), list(sems), copies
        self.alias = dict(alias or {})


def _place():
    return lax.axis_index("x"), lax.axis_index("y"), lax.axis_index("c")


_CHIP_FLIPS = ((1, 0), (0, 1), (1, 1))


def _flip(v, bit):
    return 1 - v if bit else v


def _remote(src, dst, send, recv, dev):
    return pltpu.make_async_remote_copy(src_ref=src, dst_ref=dst, send_sem=send, recv_sem=recv,
                                        device_id=dev, device_id_type=MESH)


def _swap_job(grads):
    n = len(grads)

    def copies(ins, outs, sems):
        x, y, c = _place()
        send, recv = sems
        cps = []
        for t in range(n):
            half = ins[t].shape[1] // 2
            cps.append(_remote(ins[t].at[:, pl.ds((1 - c) * half, half), :], outs[t], send.at[t], recv.at[t],
                               (x, y, 1 - c)))
        return cps

    shapes = [jax.ShapeDtypeStruct((g.shape[0], g.shape[1] // 2, g.shape[2]), F32) for g in grads]
    return _Job(grads, shapes, [pltpu.SemaphoreType.DMA((n,)), pltpu.SemaphoreType.DMA((n,))], copies)


def _exchange_job(sums):
    n = len(sums)

    def copies(ins, outs, sems):
        x, y, c = _place()
        send, recv = sems
        cps = []
        for t in range(n):
            for r, (fx, fy) in enumerate(_CHIP_FLIPS):
                kk = 2 * _flip(x, fx) + _flip(y, fy)
                cps.append(_remote(ins[t].at[kk], outs[t].at[r], send.at[t, r], recv.at[t, r],
                                   (_flip(x, fx), _flip(y, fy), c)))
        return cps

    shapes = [jax.ShapeDtypeStruct((3,) + s.shape[1:], s.dtype) for s in sums]
    return _Job(sums, shapes, [pltpu.SemaphoreType.DMA((n, 3)), pltpu.SemaphoreType.DMA((n, 3))], copies)


def _join_job(fulls):
    n = len(fulls)

    def copies(ins, outs, sems):
        x, y, c = _place()
        send, recv = sems
        cps = []
        for t in range(n):
            half = ins[t].shape[0] // 2
            rows = pl.ds(c * half, half)
            cps.append(_remote(ins[t].at[rows], outs[t].at[rows], send.at[t], recv.at[t], (x, y, 1 - c)))
        return cps

    shapes = [jax.ShapeDtypeStruct(f.shape, f.dtype) for f in fulls]
    return _Job(fulls, shapes, [pltpu.SemaphoreType.DMA((n,)), pltpu.SemaphoreType.DMA((n,))], copies,
                alias={t: t for t in range(n)})


def _gather_job(bufs, forward):
    n = len(bufs)

    def copies(ins, outs, sems):
        x, y, c = _place()
        send, recv = sems
        cps = []
        for t in range(n):
            half = ins[t].shape[1] // 2
            rows = pl.ds(c * half, half)
            for r, (fx, fy) in enumerate(_CHIP_FLIPS):
                if forward:
                    kk = 2 * _flip(x, fx) + _flip(y, fy)
                    dev = (x, y, 1 - c)
                else:
                    kk = 2 * x + y
                    dev = (_flip(x, fx), _flip(y, fy), c)
                cps.append(_remote(ins[t].at[kk, rows], outs[t].at[kk, rows], send.at[t, r], recv.at[t, r], dev))
        return cps

    shapes = [jax.ShapeDtypeStruct(b.shape, b.dtype) for b in bufs]
    return _Job(bufs, shapes, [pltpu.SemaphoreType.DMA((n, 3)), pltpu.SemaphoreType.DMA((n, 3))], copies,
                alias={t: t for t in range(n)})


def _call(body, args, *, name, grid, in_specs, out_specs, out_shape, scratch_shapes=(), params=None, jobs=()):
    n_in, n_out, n_scr = len(in_specs), len(out_specs), len(scratch_shapes)
    job_in = [len(j.operands) for j in jobs]
    job_out = [len(j.out_shapes) for j in jobs]
    job_sem = [len(j.sems) for j in jobs]

    def wrapped(*refs):
        pos = 0
        ins = refs[pos:pos + n_in]; pos += n_in
        jins = []
        for k in job_in:
            jins.append(refs[pos:pos + k]); pos += k
        outs = refs[pos:pos + n_out]; pos += n_out
        jouts = []
        for k in job_out:
            jouts.append(refs[pos:pos + k]); pos += k
        scr = refs[pos:pos + n_scr]; pos += n_scr
        jsems = []
        for k in job_sem:
            jsems.append(refs[pos:pos + k]); pos += k
        if jobs:
            ids = [pl.program_id(d) for d in range(len(grid))]
            first = functools.reduce(jnp.logical_and, [i == 0 for i in ids])
            last = functools.reduce(jnp.logical_and, [i == g - 1 for i, g in zip(ids, grid)])

            @pl.when(first)
            def _():
                for j, ji, jo, js in zip(jobs, jins, jouts, jsems):
                    for cp in j.copies(ji, jo, js):
                        cp.start()

        body(*ins, *outs, *scr)
        if jobs:
            @pl.when(last)
            def _():
                for j, ji, jo, js in zip(jobs, jins, jouts, jsems):
                    for cp in j.copies(ji, jo, js):
                        cp.wait()

    aliases = {}
    in_pos, out_pos = n_in, n_out
    for j in jobs:
        for i, o in j.alias.items():
            aliases[in_pos + i] = out_pos + o
        in_pos += len(j.operands)
        out_pos += len(j.out_shapes)
    res = pl.pallas_call(
        wrapped, name=name, grid=grid,
        in_specs=list(in_specs) + [HBM] * sum(job_in),
        out_specs=list(out_specs) + [HBM] * sum(job_out),
        out_shape=list(out_shape) + [_in_hbm(s) for j in jobs for s in j.out_shapes],
        scratch_shapes=list(scratch_shapes) + [s for j in jobs for s in j.sems],
        input_output_aliases=aliases,
        compiler_params=params,
    )(*[a if spec is SMEM else _from_hbm(a) for a, spec in zip(args, in_specs)],
      *[a for j in jobs for a in j.operands])
    own, rest = list(res[:n_out]), list(res[n_out:])
    job_res = []
    for k in job_out:
        job_res.append(rest[:k])
        rest = rest[k:]
    return own, job_res


def _run_jobs(name, jobs):
    def body():
        pass

    return _call(body, (), name=name, grid=(1,), in_specs=[], out_specs=[], out_shape=[], jobs=jobs)[1]


def _dot(a, b):
    return jnp.dot(a, b, preferred_element_type=F32)


def _dot_nt(a, b):
    return lax.dot_general(a, b, (((1,), (1,)), ((), ())), preferred_element_type=F32)


def _dot_tn(a, b):
    return lax.dot_general(a, b, (((0,), (0,)), ((), ())), preferred_element_type=F32)


def _rms_r(x):
    return lax.rsqrt(jnp.mean(x * x, axis=-1, keepdims=True) + EPS)


def _rms_bwd(x, r, g, dy):
    n = x * r
    dn = dy * g
    dx = r * (dn - n * jnp.mean(dn * n, axis=-1, keepdims=True))
    return dx, dy * n


def _seg64_sum(v):
    rows, width = v.shape
    lane = lax.broadcasted_iota(jnp.int32, (rows, LANES), 1)
    lo = lane < HEAD_DIM
    outs = []
    for c in range(width // LANES):
        ch = v[:, c * LANES:(c + 1) * LANES]
        s_lo = jnp.sum(jnp.where(lo, ch, 0.0), axis=-1, keepdims=True)
        s_hi = jnp.sum(jnp.where(lo, 0.0, ch), axis=-1, keepdims=True)
        outs.append(jnp.where(lo, s_lo, s_hi))
    return outs[0] if len(outs) == 1 else jnp.concatenate(outs, axis=-1)


def _head_r(v):
    return lax.rsqrt(_seg64_sum(v * v) * (1.0 / HEAD_DIM) + EPS)


def _swap16(ch):
    lane = lax.broadcasted_iota(jnp.int32, ch.shape, 1)
    return jnp.where((lane % 32) < 16, pltpu.roll(ch, LANES - 16, 1), pltpu.roll(ch, 16, 1))


def _by_chunk(fn, v):
    outs = [fn(v[:, c * LANES:(c + 1) * LANES]) for c in range(v.shape[1] // LANES)]
    return outs[0] if len(outs) == 1 else jnp.concatenate(outs, axis=-1)


def _rope(v, cos, sin_signed):
    return _by_chunk(lambda ch: ch * cos + _swap16(ch) * sin_signed, v)


def _rope_t(g, cos, sin_signed):
    return _by_chunk(lambda ch: ch * cos + _swap16(ch * sin_signed), g)


def _rope_tables(seq):
    nf = HEAD_DIM // 4
    freqs = ROPE_THETA ** (-jnp.arange(nf, dtype=F32) / nf)
    pos = jnp.arange(seq, dtype=jnp.int32)
    row = (pos // GRID_W).astype(F32)
    col = (pos % GRID_W).astype(F32)
    ang_r = row[:, None] * freqs[None, :]
    ang_c = col[:, None] * freqs[None, :]
    cr, sr, cc, sc = jnp.cos(ang_r), jnp.sin(ang_r), jnp.cos(ang_c), jnp.sin(ang_c)
    cos = jnp.concatenate([cr, cr, cc, cc], axis=1)
    sin = jnp.concatenate([-sr, sr, -sc, sc], axis=1)
    return cos, sin


def _t5_bucket(rel):
    nb = N_BUCKETS // 2
    ret = (rel > 0).astype(jnp.int32) * nb
    n = jnp.abs(rel)
    max_exact = nb // 2
    nf = jnp.maximum(n, 1).astype(jnp.float32)
    large = max_exact + (jnp.log(nf / max_exact) / np.float32(np.log(MAX_DISTANCE / max_exact))
                         * (nb - max_exact)).astype(jnp.int32)
    large = jnp.minimum(large, nb - 1)
    return ret + jnp.where(n < max_exact, n, large)


def _window_tables():
    a = jnp.arange(BLOCK, dtype=jnp.int32)
    c = jnp.arange(3 * BLOCK, dtype=jnp.int32)
    rel = c[None, :] - BLOCK - a[:, None]
    bucket = _t5_bucket(rel)
    band = (jnp.abs(rel) <= BLOCK).astype(jnp.int32)
    to3 = lambda t: t.reshape(BLOCK, 3, BLOCK).transpose(1, 2, 0)
    return to3(bucket), to3(band)


def _pre_proj(x, g1, w_in, gq, gk, ck, sk, *, seq, tm, sub):
    tokens = x.shape[0]
    n_seq = seq // tm
    nblk = tm // BLOCK
    batch = tokens // seq

    def body(x_ref, g1_ref, w_ref, gq_ref, gk_ref, ck_ref, sk_ref,
             h1_ref, raw_ref, qa_ref, ka_ref, kta_ref, va_ref, vta_ref,
             qtb_ref, kb_ref, ktb_ref, vb_ref, vtb_ref, proj):
        for r in range(tm // sub):
            rows = slice(r * sub, (r + 1) * sub)
            xv = x_ref[rows, :]
            h = (xv * _rms_r(xv) * g1_ref[...]).astype(MM)
            h1_ref[rows, :] = h
            for j in range(N_CHIPS):
                proj[rows, j * IN_CHUNK:(j + 1) * IN_CHUNK] = _dot(h, w_ref[j])
            qa = proj[rows, 0:Q_WIDTH]
            ka = proj[rows, Q_WIDTH:QK_RAW]
            raw_ref[rows, :] = proj[rows, 0:QK_RAW]
            qn = qa * _head_r(qa) * gq_ref[...]
            qa_ref[rows, :] = (_rope(qn, ck_ref[rows, :], sk_ref[rows, :]) * SCALE).astype(MM)
            kn = ka * _head_r(ka) * gk_ref[...]
            kr = _rope(kn, ck_ref[rows, :], sk_ref[rows, :])
            ka_ref[rows, :] = kr.astype(MM)
            kta_ref[0, :, rows] = kr.T.astype(MM)
            va = proj[rows, 640:768]
            va_ref[rows, :] = va.astype(MM)
            vta_ref[0, :, rows] = va.T.astype(MM)
            qb = proj[rows, 768:1280] * SCALE
            kb = proj[rows, 1280:1408]
            vb = proj[rows, 1408:1536]
            kb_ref[rows, :] = kb.astype(MM)
            vb_ref[rows, :] = vb.astype(MM)
            for j in range(sub // BLOCK):
                blk = slice(j * BLOCK, (j + 1) * BLOCK)
                qtb_ref[r * (sub // BLOCK) + j] = qb[blk, :].T.astype(MM)
                ktb_ref[r * (sub // BLOCK) + j] = kb[blk, :].T.astype(MM)
                vtb_ref[r * (sub // BLOCK) + j] = vb[blk, :].T.astype(MM)

    tok = lambda w: pl.BlockSpec((tm, w), lambda i: (i, 0))
    tab = lambda w: pl.BlockSpec((tm, w), lambda i: (i % n_seq, 0))
    row = lambda w: pl.BlockSpec((1, w), lambda i: (0, 0))
    tposed = pl.BlockSpec((1, LANES, tm), lambda i: (i // n_seq, 0, i % n_seq))
    blocks = pl.BlockSpec((nblk, BLOCK, LANES), lambda i: (i, 0, 0))
    qblocks = pl.BlockSpec((nblk, Q_WIDTH, BLOCK), lambda i: (i, 0, 0))
    tok_mm = lambda w: jax.ShapeDtypeStruct((tokens, w), MM)
    return pl.pallas_call(
        body, name="pre_proj",
        grid=(tokens // tm,),
        in_specs=[tok(D_MODEL), row(D_MODEL),
                  pl.BlockSpec((N_CHIPS, D_MODEL, IN_CHUNK), lambda i: (0, 0, 0)),
                  row(Q_WIDTH), row(KV_WIDTH), tab(KV_WIDTH), tab(KV_WIDTH)],
        out_specs=[tok(D_MODEL), tok(QK_RAW), tok(Q_WIDTH), tok(KV_WIDTH), tposed, tok(KV_WIDTH), tposed,
                   qblocks, tok(KV_WIDTH), blocks, tok(KV_WIDTH), blocks],
        out_shape=[
            tok_mm(D_MODEL),
            jax.ShapeDtypeStruct((tokens, QK_RAW), F32),
            tok_mm(Q_WIDTH),
            tok_mm(KV_WIDTH),
            jax.ShapeDtypeStruct((batch, KV_WIDTH, seq), MM),
            tok_mm(KV_WIDTH),
            jax.ShapeDtypeStruct((batch, KV_WIDTH, seq), MM),
            jax.ShapeDtypeStruct((tokens // BLOCK, Q_WIDTH, BLOCK), MM),
            tok_mm(KV_WIDTH),
            jax.ShapeDtypeStruct((tokens // BLOCK, KV_WIDTH, BLOCK), MM),
            tok_mm(KV_WIDTH),
            jax.ShapeDtypeStruct((tokens // BLOCK, KV_WIDTH, BLOCK), MM),
        ],
        scratch_shapes=[pltpu.VMEM((tm, IN_TOTAL), F32)],
        compiler_params=_cparams(("parallel",), 48),
    )(*map(_from_hbm, (x, g1, w_in, gq, gk, ck, sk)))


def _kv_half(v2, kv):
    return jnp.where(kv == 0, v2[:, :HEAD_DIM], v2[:, HEAD_DIM:])


def _attn_a_fwd(qa, kta, va, *, seq, bq, jobs=()):
    tokens = qa.shape[0]
    batch = tokens // seq
    nq = seq // bq

    def body(q_ref, kt_ref, v_ref, o_ref, p_ref, linv_ref):
        kv = pl.program_id(1)
        kt = kt_ref[0]
        lane = lax.broadcasted_iota(jnp.int32, (seq, KV_WIDTH), 1)
        v = jnp.where((lane < HEAD_DIM) == (kv == 0), v_ref[...], jnp.ones((), MM))
        for g in range(GROUP):
            sl = slice(g * HEAD_DIM, (g + 1) * HEAD_DIM)
            s = _dot(q_ref[:, sl], kt)
            pb = jnp.exp((s - jnp.max(s, axis=-1, keepdims=True)).astype(MM))
            p_ref[0, g] = pb
            o2 = _dot(pb, v)
            linv = 1.0 / _kv_half(o2, 1 - kv)[:, 0:1]
            o_ref[:, sl] = _kv_half(o2, kv) * linv
            linv_ref[0, :, g:g + 1] = linv

    return _call(
        body, (qa, kta, va), name="attn_a_fwd", jobs=jobs,
        grid=(batch, N_KV, nq),
        in_specs=[pl.BlockSpec((bq, GROUP * HEAD_DIM), lambda b, k, i: (b * nq + i, k)),
                  pl.BlockSpec((1, HEAD_DIM, seq), lambda b, k, i: (b, k, 0)),
                  pl.BlockSpec((seq, KV_WIDTH), lambda b, k, i: (b, 0))],
        out_specs=[pl.BlockSpec((bq, GROUP * HEAD_DIM), lambda b, k, i: (b * nq + i, k)),
                   pl.BlockSpec((1, GROUP, bq, seq), lambda b, k, i: (k, 0, b * nq + i, 0)),
                   pl.BlockSpec((1, bq, GROUP), lambda b, k, i: (k, b * nq + i, 0))],
        out_shape=[jax.ShapeDtypeStruct((tokens, Q_WIDTH), F32),
                   jax.ShapeDtypeStruct((N_KV, GROUP, tokens, seq), MM),
                   jax.ShapeDtypeStruct((N_KV, tokens, GROUP), F32)],
        params=_cparams(("arbitrary", "arbitrary", "arbitrary"), 56))


def _attn_a_bwd(qa, ka, vta, do, o, p, linv, *, seq, bq, jobs=()):
    tokens = qa.shape[0]
    batch = tokens // seq
    nq = seq // bq

    def body(q_ref, k_ref, vt_ref, do_ref, o_ref, p_ref, linv_ref, dq_ref, dkt_ref, dvt_ref):
        kv = pl.program_id(1)

        @pl.when(pl.program_id(2) == 0)
        def _():
            dkt_ref[...] = jnp.zeros_like(dkt_ref)
            dvt_ref[...] = jnp.zeros_like(dvt_ref)

        vt = vt_ref[0]
        k2 = k_ref[...]
        for g in range(GROUP):
            sl = slice(g * HEAD_DIM, (g + 1) * HEAD_DIM)
            dof = do_ref[:, sl]
            delta = jnp.sum(dof * o_ref[:, sl], axis=-1, keepdims=True)
            linv_g = linv_ref[0, :, g:g + 1]
            pb = p_ref[0, g]
            dp = _dot(dof.astype(MM), vt)
            ds = pb * ((dp - delta) * linv_g).astype(MM)
            dq_ref[:, sl] = _kv_half(_dot(ds, k2), kv)
            dkt_ref[0] += _dot_tn(q_ref[:, sl], ds)
            dvt_ref[0] += _dot_tn((dof * linv_g).astype(MM), pb)

    qspec = pl.BlockSpec((bq, GROUP * HEAD_DIM), lambda b, k, i: (b * nq + i, k))
    tspec = pl.BlockSpec((1, HEAD_DIM, seq), lambda b, k, i: (b, k, 0))
    return _call(
        body, (qa, ka, vta, do, o, p, linv), name="attn_a_bwd", jobs=jobs,
        grid=(batch, N_KV, nq),
        in_specs=[qspec, pl.BlockSpec((seq, KV_WIDTH), lambda b, k, i: (b, 0)), tspec, qspec, qspec,
                  pl.BlockSpec((1, GROUP, bq, seq), lambda b, k, i: (k, 0, b * nq + i, 0)),
                  pl.BlockSpec((1, bq, GROUP), lambda b, k, i: (k, b * nq + i, 0))],
        out_specs=[qspec, tspec, tspec],
        out_shape=[jax.ShapeDtypeStruct((tokens, Q_WIDTH), F32),
                   jax.ShapeDtypeStruct((batch, KV_WIDTH, seq), F32),
                   jax.ShapeDtypeStruct((batch, KV_WIDTH, seq), F32)],
        params=_cparams(("arbitrary", "arbitrary", "arbitrary"), 56))


def _bias_build(rel_bias_t, bucket_t, band_t):
    def body(tab_ref, bucket_ref, band_ref, bias_ref):
        for h in range(GROUP * N_KV):
            for piece in range(3):
                bk = bucket_ref[piece]
                acc = jnp.zeros((BLOCK, BLOCK), F32)
                for b in range(N_BUCKETS):
                    acc = jnp.where(bk == b, tab_ref[h, b], acc)
                g = h % GROUP
                bias_ref[h // GROUP, piece, :, g * BLOCK:(g + 1) * BLOCK] = jnp.where(band_ref[piece] != 0, acc, NEG_INF)

    out = jax.ShapeDtypeStruct((N_KV, 3, BLOCK, GROUP * BLOCK), F32)
    return pl.pallas_call(
        body, name="bias_build", grid=(1,),
        in_specs=[SMEM, _whole(bucket_t), _whole(band_t)], out_specs=_whole(out), out_shape=out,
    )(rel_bias_t, bucket_t, band_t)


def _pad_heads(t, kv):
    outs = []
    for g in range(GROUP):
        tg = t[g * HEAD_DIM:(g + 1) * HEAD_DIM, :]
        zero = jnp.zeros_like(tg)
        outs.append(jnp.concatenate([jnp.where(kv == 0, tg, zero), jnp.where(kv == 0, zero, tg)], axis=0))
    return jnp.concatenate(outs, axis=-1)


def _unpad_heads(t, kv):
    outs = [_kv_half(t[:, g * BLOCK:(g + 1) * BLOCK].T, kv) for g in range(GROUP)]
    return jnp.concatenate(outs, axis=-1)


def _sink_row(sink_ref, kv):
    lane_head = lax.broadcasted_iota(jnp.int32, (1, GROUP * BLOCK), 1) // BLOCK
    row = jnp.zeros((1, GROUP * BLOCK), F32)
    for g in range(GROUP):
        row = jnp.where(lane_head == g, sink_ref[0, kv * GROUP + g], row)
    return row


def _window_scores_t(k_ref, idx, qpad, bias_ref, n, nblk):
    pieces = []
    for piece in range(3):
        s = _dot(k_ref[idx[piece]], qpad) + bias_ref[0, piece]
        if piece == 0:
            s = jnp.where(n > 0, s, NEG_INF)
        if piece == 2:
            s = jnp.where(n < nblk - 1, s, NEG_INF)
        pieces.append(s)
    return pieces


def _attn_b_fwd(qtb, kb3, vtb, bias, sink, *, seq, per_step, jobs=()):
    nblk_all = qtb.shape[0]
    tokens = nblk_all * BLOCK
    batch = tokens // seq
    nblk = seq // BLOCK
    nstep = nblk // per_step

    def body(sink_ref, q_ref, k_ref, vt_ref, bias_ref, o_ref, p_ref, stat_ref):
        kv = pl.program_id(0)
        first = pl.program_id(2) * per_step
        sink_row = _sink_row(sink_ref, kv)
        stat_row = lax.broadcasted_iota(jnp.int32, (8, GROUP * BLOCK), 0)

        def block(i, carry):
            n = first + i
            idx = (jnp.maximum(n - 1, 0), n, jnp.minimum(n + 1, nblk - 1))
            rows = slice(i * BLOCK, (i + 1) * BLOCK)
            qpad = _pad_heads(q_ref[n], kv)
            ss = _window_scores_t(k_ref, idx, qpad, bias_ref, n, nblk)
            m = jnp.maximum(jnp.maximum(jnp.max(ss[0], axis=0, keepdims=True),
                                        jnp.max(ss[1], axis=0, keepdims=True)),
                            jnp.maximum(jnp.max(ss[2], axis=0, keepdims=True), sink_row))
            ps = [jnp.exp(s - m) for s in ss]
            e_sink = jnp.exp(sink_row - m)
            rinv = 1.0 / (jnp.sum(ps[0], axis=0, keepdims=True) + jnp.sum(ps[1], axis=0, keepdims=True)
                          + jnp.sum(ps[2], axis=0, keepdims=True) + e_sink)
            ot = jnp.zeros((KV_WIDTH, GROUP * BLOCK), F32)
            for piece in range(3):
                pb = ps[piece].astype(MM)
                p_ref[0, i, piece] = pb
                ot = ot + _dot(vt_ref[idx[piece]], pb)
            o_ref[rows, :] = _unpad_heads(ot * rinv, kv)
            stat_ref[0, i] = jnp.where(stat_row == 0, rinv, e_sink * rinv)
            return carry

        for i in range(per_step):
            block(i, 0)

    both = pl.BlockSpec((nblk, BLOCK, KV_WIDTH), lambda k, b, j: (b, 0, 0))
    return _call(
        body, (sink, qtb, kb3, vtb, bias), name="attn_b_fwd", jobs=jobs,
        grid=(N_KV, batch, nstep),
        in_specs=[SMEM, pl.BlockSpec((nblk, GROUP * HEAD_DIM, BLOCK), lambda k, b, j: (b, k, 0)), both, both,
                  pl.BlockSpec((1, 3, BLOCK, GROUP * BLOCK), lambda k, b, j: (k, 0, 0, 0))],
        out_specs=[pl.BlockSpec((per_step * BLOCK, GROUP * HEAD_DIM), lambda k, b, j: (b * nstep + j, k)),
                   pl.BlockSpec((1, per_step, 3, BLOCK, GROUP * BLOCK), lambda k, b, j: (k, b * nstep + j, 0, 0, 0)),
                   pl.BlockSpec((1, per_step, 8, GROUP * BLOCK), lambda k, b, j: (k, b * nstep + j, 0, 0))],
        out_shape=[jax.ShapeDtypeStruct((tokens, Q_WIDTH), F32),
                   jax.ShapeDtypeStruct((N_KV, nblk_all, 3, BLOCK, GROUP * BLOCK), MM),
                   jax.ShapeDtypeStruct((N_KV, nblk_all, 8, GROUP * BLOCK), F32)],
        params=_cparams(("arbitrary", "arbitrary", "arbitrary"), 48))


def _attn_b_bwd(qtb, ktb, vb3, do, o, p, stat, *, seq, per_step, jobs=()):
    nblk_all = qtb.shape[0]
    tokens = nblk_all * BLOCK
    batch = tokens // seq
    nblk = seq // BLOCK
    nstep = nblk // per_step

    def body(q_ref, kt_ref, v_ref, do_ref, o_ref, p_ref, stat_ref,
             dq_ref, dk_ref, dv_ref, dbias_ref, dsink_ref):
        kv = pl.program_id(0)
        step = pl.program_id(2)
        first = step * per_step

        @pl.when(jnp.logical_and(pl.program_id(1) == 0, step == 0))
        def _():
            dbias_ref[...] = jnp.zeros_like(dbias_ref)
            dsink_ref[...] = jnp.zeros_like(dsink_ref)

        @pl.when(step == 0)
        def _():
            dk_ref[...] = jnp.zeros_like(dk_ref)
            dv_ref[...] = jnp.zeros_like(dv_ref)

        def block(i, dsink):
            n = first + i
            idx = (jnp.maximum(n - 1, 0), n, jnp.minimum(n + 1, nblk - 1))
            rows = slice(i * BLOCK, (i + 1) * BLOCK)
            qpad = _pad_heads(q_ref[n], kv)
            dot_t = do_ref[rows, :].T
            prod = dot_t * o_ref[rows, :].T
            delta = jnp.concatenate(
                [jnp.sum(prod[g * HEAD_DIM:(g + 1) * HEAD_DIM, :], axis=0, keepdims=True) for g in range(GROUP)],
                axis=-1)
            stats = stat_ref[0, i]
            rinv, p_sink = stats[0:1, :], stats[1:2, :]
            dopad32 = _pad_heads(dot_t, kv)
            dopad = dopad32.astype(MM)
            dopad_n = (dopad32 * rinv).astype(MM)
            dqt = jnp.zeros((KV_WIDTH, GROUP * BLOCK), F32)
            for piece in range(3):
                pb = p_ref[0, i, piece]
                dst = pb.astype(F32) * ((_dot(v_ref[idx[piece]], dopad) - delta) * rinv)
                dsb = dst.astype(MM)
                dbias_ref[0, piece] += dst
                dqt = dqt + _dot(kt_ref[idx[piece]], dsb)
                dk_ref[0, idx[piece]] += _dot_nt(dsb, qpad)
                dv_ref[0, idx[piece]] += _dot_nt(pb, dopad_n)
            dq_ref[rows, :] = _unpad_heads(dqt, kv)
            return dsink - p_sink * delta

        dsink = jnp.zeros((1, GROUP * BLOCK), F32)
        for i in range(per_step):
            dsink = block(i, dsink)
        dsink_ref[0] += jnp.broadcast_to(dsink, (8, GROUP * BLOCK))

    qspec = pl.BlockSpec((per_step * BLOCK, GROUP * HEAD_DIM), lambda k, b, j: (b * nstep + j, k))
    both = pl.BlockSpec((nblk, BLOCK, KV_WIDTH), lambda k, b, j: (b, 0, 0))
    grad =pl.BlockSpec((1, nblk, BLOCK, KV_WIDTH), lambda k, b, j: (k, b, 0, 0))
    return _call(
        body, (qtb, ktb, vb3, do, o, p, stat), name="attn_b_bwd", jobs=jobs,
        grid=(N_KV, batch, nstep),
        in_specs=[pl.BlockSpec((nblk, GROUP * HEAD_DIM, BLOCK), lambda k, b, j: (b, k, 0)), both, both,
                  qspec, qspec,
                  pl.BlockSpec((1, per_step, 3, BLOCK, GROUP * BLOCK), lambda k, b, j: (k, b * nstep + j, 0, 0, 0)),
                  pl.BlockSpec((1, per_step, 8, GROUP * BLOCK), lambda k, b, j: (k, b * nstep + j, 0, 0))],
        out_specs=[qspec, grad, grad,
                   pl.BlockSpec((1, 3, BLOCK, GROUP * BLOCK), lambda k, b, j: (k, 0, 0, 0)),
                   pl.BlockSpec((1, 8, GROUP * BLOCK), lambda k, b, j: (k, 0, 0))],
        out_shape=[jax.ShapeDtypeStruct((tokens, Q_WIDTH), F32),
                   jax.ShapeDtypeStruct((N_KV, nblk_all, BLOCK, KV_WIDTH), F32),
                   jax.ShapeDtypeStruct((N_KV, nblk_all, BLOCK, KV_WIDTH), F32),
                   jax.ShapeDtypeStruct((N_KV, 3, BLOCK, GROUP * BLOCK), F32),
                   jax.ShapeDtypeStruct((N_KV, 8, GROUP * BLOCK), F32)],
        params=_cparams(("arbitrary", "arbitrary", "arbitrary"), 56))


def _resident(shape):
    return pl.BlockSpec(shape, lambda i: (0,) * len(shape), pipeline_mode=pl.Buffered(1))


def _mix_ffn_fwd(oa, ob, w_o, x, g2, g3, w_up, w_down, target, g4, *, tm):
    tokens = x.shape[0]
    nt = tokens // tm

    def body(oa_ref, ob_ref, wo_ref, x_ref, g2_ref, g3_ref, wu_ref, wd_ref, t_ref, g4_ref,
             mix_ref, x1_ref, h2_ref, o_ref, u_ref, df_ref, dy_ref, loss_ref, dg4_ref):
        o = jnp.concatenate([oa_ref[...].astype(MM), ob_ref[...].astype(MM)], axis=-1)
        o_ref[...] = o
        mix = _dot(o, wo_ref[...])
        mix_ref[...] = mix
        x1 = x_ref[...] + mix * _rms_r(mix) * g2_ref[...]
        x1_ref[...] = x1
        h2v = (x1 * _rms_r(x1) * g3_ref[...]).astype(MM)
        h2_ref[...] = h2v
        f = jnp.zeros((tm, D_MODEL), F32)
        for c in range(N_CHIPS):
            u = jnp.maximum(_dot(h2v, wu_ref[c]), 0.0)
            u_ref[:, c * FF_CHUNK:(c + 1) * FF_CHUNK] = u.astype(MM)
            f = f + _dot((u * u).astype(MM), wd_ref[c * FF_CHUNK:(c + 1) * FF_CHUNK, :])
        r = _rms_r(f)
        g4v = g4_ref[...]
        err = x1 + f * r * g4v - t_ref[...]
        sq = jnp.sum(err * err, axis=-1, keepdims=True)
        loss_ref[0] = jnp.broadcast_to(jnp.sum(sq, axis=0, keepdims=True) * (0.5 / D_MODEL), (8, LANES))
        dy = err * (1.0 / D_MODEL)
        dy_ref[...] = dy
        dfv, dgv = _rms_bwd(f, r, g4v, dy)
        df_ref[...] = dfv.astype(MM)
        dg4_ref[0] = jnp.sum(dgv, axis=0, keepdims=True)

    tok = pl.BlockSpec((tm, D_MODEL), lambda i: (i, 0))
    half = pl.BlockSpec((tm, Q_WIDTH), lambda i: (i, 0))
    row = pl.BlockSpec((1, D_MODEL), lambda i: (0, 0))
    tok_f32 = jax.ShapeDtypeStruct((tokens, D_MODEL), F32)
    tok_mm = jax.ShapeDtypeStruct((tokens, D_MODEL), MM)
    return pl.pallas_call(
        body, name="mix_ffn_fwd",
        grid=(nt,),
        in_specs=[half, half, _resident((D_MODEL, D_MODEL)), tok, row, row,
                  _resident((N_CHIPS, D_MODEL, FF_CHUNK)), _resident((D_FF, D_MODEL)), tok, row],
        out_specs=[tok, tok, tok, tok, pl.BlockSpec((tm, D_FF), lambda i: (i, 0)), tok, tok,
                   pl.BlockSpec((1, 8, LANES), lambda i: (i, 0, 0)),
                   pl.BlockSpec((1, 1, D_MODEL), lambda i: (i, 0, 0))],
        out_shape=[tok_f32,
                   tok_f32,
                   tok_mm,
                   tok_mm,
                   jax.ShapeDtypeStruct((tokens, D_FF), MM),
                   tok_mm,
                   tok_f32,
                   jax.ShapeDtypeStruct((nt, 8, LANES), F32),
                   jax.ShapeDtypeStruct((nt, 1, D_MODEL), F32)],
        compiler_params=_cparams(("parallel",), 56),
    )(*map(_from_hbm, (oa, ob, w_o, x, g2, g3, w_up, w_down, target, g4)))


def _ffn_bwd_act(df, w_down, u, w_up, x1, dy, mix, g3, g2, w_o, *, tm):
    tokens = df.shape[0]
    nt = tokens // tm

    def body(df_ref, wd_ref, u_ref, wu_ref, x1_ref, dy_ref, mix_ref, g3_ref, g2_ref, wo_ref,
             dz_ref, dx1_ref, dmix_ref, dg3_ref, dg2_ref, doa_ref, dob_ref):
        dfv = df_ref[...]
        dh2 = jnp.zeros((tm, D_MODEL), F32)
        for c in range(N_CHIPS):
            cols = slice(c * FF_CHUNK, (c + 1) * FF_CHUNK)
            da = _dot_nt(dfv, wd_ref[cols, :])
            dz = (da * (2.0 * u_ref[:, cols].astype(F32))).astype(MM)
            dz_ref[:, cols] = dz
            dh2 = dh2 + _dot_nt(dz, wu_ref[c])
        x1 = x1_ref[...]
        dxn, dg3v = _rms_bwd(x1, _rms_r(x1), g3_ref[...], dh2)
        dx1 = dy_ref[...] + dxn
        dx1_ref[...] = dx1
        dg3_ref[0] = jnp.sum(dg3v, axis=0, keepdims=True)
        mix = mix_ref[...]
        dmix, dg2v = _rms_bwd(mix, _rms_r(mix), g2_ref[...], dx1)
        dmb = dmix.astype(MM)
        dmix_ref[...] = dmb
        dg2_ref[0] = jnp.sum(dg2v, axis=0, keepdims=True)
        doa_ref[...] = _dot_nt(dmb, wo_ref[0:Q_WIDTH, :])
        dob_ref[...] = _dot_nt(dmb, wo_ref[Q_WIDTH:D_MODEL, :])

    tok = pl.BlockSpec((tm, D_MODEL), lambda i: (i, 0))
    half = pl.BlockSpec((tm, Q_WIDTH), lambda i: (i, 0))
    wide = pl.BlockSpec((tm, D_FF), lambda i: (i, 0))
    row = pl.BlockSpec((1, D_MODEL), lambda i: (0, 0))
    part = pl.BlockSpec((1, 1, D_MODEL), lambda i: (i, 0, 0))
    return pl.pallas_call(
        body, name="ffn_bwd_act",
        grid=(nt,),
        in_specs=[tok, _resident((D_FF, D_MODEL)), wide, _resident((N_CHIPS, D_MODEL, FF_CHUNK)),
                  tok, tok, tok, row, row, _resident((D_MODEL, D_MODEL))],
        out_specs=[wide, tok, tok, part, part, half, half],
        out_shape=[jax.ShapeDtypeStruct((tokens, D_FF), MM),
                   jax.ShapeDtypeStruct((tokens, D_MODEL), F32),
                   jax.ShapeDtypeStruct((tokens, D_MODEL), MM),
                   jax.ShapeDtypeStruct((nt, 1, D_MODEL), F32),
                   jax.ShapeDtypeStruct((nt, 1, D_MODEL), F32),
                   jax.ShapeDtypeStruct((tokens, Q_WIDTH), F32),
                   jax.ShapeDtypeStruct((tokens, Q_WIDTH), F32)],
        compiler_params=_cparams(("parallel",), 56),
    )(*map(_from_hbm, (df, w_down, u, w_up, x1, dy, mix, g3, g2, w_o)))


def _tn_matmul(a, b, *, name, tm, tn, tk, chunk=None, square_a=False, vmem_mb=48, jobs=()):
    tokens, m_dim = a.shape
    n_dim = b.shape[1]
    chunked = chunk is not None
    if chunked:
        assert tm == m_dim and tn % chunk == 0

    def body(a_ref, b_ref, o_ref):
        av = a_ref[...]
        if square_a:
            av = av.astype(F32)
            av = av * av
        part = _dot_tn(av.astype(MM), b_ref[...].astype(MM))
        if chunked:
            part = jnp.stack([part[:, c * chunk:(c + 1) * chunk] for c in range(tn // chunk)])

        @pl.when(pl.program_id(2) == 0)
        def _():
            o_ref[...] = part

        @pl.when(pl.program_id(2) > 0)
        def _():
            o_ref[...] += part

    if chunked:
        out_spec = pl.BlockSpec((tn // chunk, tm, chunk), lambda i, j, k: (j, 0, 0))
        out_shape = jax.ShapeDtypeStruct((n_dim // chunk, m_dim, chunk), F32)
    else:
        out_spec = pl.BlockSpec((tm, tn), lambda i, j, k: (i, j))
        out_shape = jax.ShapeDtypeStruct((m_dim, n_dim), F32)
    (out,), job_res = _call(
        body, (a, b), name=name, jobs=jobs,
        grid=(m_dim // tm, n_dim // tn, tokens // tk),
        in_specs=[pl.BlockSpec((tk, tm), lambda i, j, k: (k, i)),
                  pl.BlockSpec((tk, tn), lambda i, j, k: (k, j))],
        out_specs=[out_spec], out_shape=[_in_hbm(out_shape)],
        params=_cparams(("arbitrary", "arbitrary", "arbitrary"), vmem_mb))
    return out, job_res


def _proj_bwd(dqa, dkta, dvta, dqb, dktb, dvtb, raw, x, dx1, g1, w_in, gq, gk, ck, sk, *, seq, tm, sub, jobs=()):
    tokens = x.shape[0]
    nt = tokens // tm
    n_seq = seq // tm
    nblk = tm // BLOCK

    def body(dqa_ref, dkta_ref, dvta_ref, dqb_ref, dkb_ref, dvb_ref, raw_ref, x_ref, dx1_ref, g1_ref, w_ref,
             gq_ref, gk_ref, ck_ref, sk_ref,
             gx_ref, dproj_ref, dg1_ref, dgq_ref, dgk_ref, dp):
        parts = []
        for r in range(tm // sub):
            rows = slice(r * sub, (r + 1) * sub)
            qa = raw_ref[rows, 0:Q_WIDTH]
            dqn = _rope_t(dqa_ref[rows, :], ck_ref[rows, :], sk_ref[rows, :]) * SCALE
            rq = _head_r(qa)
            nq = qa * rq
            dnq = dqn * gq_ref[...]
            dp[rows, 0:Q_WIDTH] = rq * (dnq - nq * (_seg64_sum(dnq * nq) * (1.0 / HEAD_DIM)))

            ka = raw_ref[rows, Q_WIDTH:QK_RAW]
            dkn = _rope_t(dkta_ref[0, :, rows].T, ck_ref[rows, :], sk_ref[rows, :])
            rk = _head_r(ka)
            nk = ka * rk
            dnk = dkn * gk_ref[...]
            dp[rows, 512:640] = rk * (dnk - nk * (_seg64_sum(dnk * nk) * (1.0 / HEAD_DIM)))

            dp[rows, 640:768] = dvta_ref[0, :, rows].T
            dp[rows, 768:1280] = dqb_ref[rows, :] * SCALE
            for j in range(r * sub // BLOCK, (r + 1) * sub // BLOCK):
                dp[j * BLOCK:(j + 1) * BLOCK, 1280:1408] = dkb_ref[0, j] + dkb_ref[1, j]
                dp[j * BLOCK:(j + 1) * BLOCK, 1408:1536] = dvb_ref[0, j] + dvb_ref[1, j]

            dproj = dp[rows, :].astype(MM)
            dproj_ref[rows, :] = dproj
            dh1 = _dot_nt(dproj[:, 0:IN_CHUNK], w_ref[0])
            for j in range(1, N_CHIPS):
                dh1 = dh1 + _dot_nt(dproj[:, j * IN_CHUNK:(j + 1) * IN_CHUNK], w_ref[j])
            xv = x_ref[rows, :]
            dxn, dg1v = _rms_bwd(xv, _rms_r(xv), g1_ref[...], dh1)
            gx_ref[rows, :] = dx1_ref[rows, :] + dxn
            parts.append((jnp.sum(dqn * nq, axis=0, keepdims=True), jnp.sum(dkn * nk, axis=0, keepdims=True),
                          jnp.sum(dg1v, axis=0, keepdims=True)))
        dgq_ref[0] = functools.reduce(jnp.add, [p[0] for p in parts])
        dgk_ref[0] = functools.reduce(jnp.add, [p[1] for p in parts])
        dg1_ref[0] = functools.reduce(jnp.add, [p[2] for p in parts])

    tok = lambda w: pl.BlockSpec((tm, w), lambda i: (i, 0))
    tab = lambda w: pl.BlockSpec((tm, w), lambda i: (i % n_seq, 0))
    row = lambda w: pl.BlockSpec((1, w), lambda i: (0, 0))
    tposed = pl.BlockSpec((1, KV_WIDTH, tm), lambda i: (i // n_seq, 0, i % n_seq))
    blocks = pl.BlockSpec((N_KV, nblk, BLOCK, KV_WIDTH), lambda i: (0, i, 0, 0))
    part = lambda w: pl.BlockSpec((1, 1, w), lambda i: (i, 0, 0))
    return _call(
        body, (dqa, dkta, dvta, dqb, dktb, dvtb, raw, x, dx1, g1, w_in, gq, gk, ck, sk),
        name="proj_bwd", jobs=jobs,
        grid=(nt,),
        in_specs=[tok(Q_WIDTH), tposed, tposed, tok(Q_WIDTH), blocks, blocks, tok(QK_RAW), tok(D_MODEL),
                  tok(D_MODEL), row(D_MODEL),
                  pl.BlockSpec((N_CHIPS, D_MODEL, IN_CHUNK), lambda i: (0, 0, 0)),
                  row(Q_WIDTH), row(KV_WIDTH), tab(KV_WIDTH), tab(KV_WIDTH)],
        out_specs=[tok(D_MODEL), tok(IN_TOTAL), part(D_MODEL), part(Q_WIDTH), part(KV_WIDTH)],
        out_shape=[jax.ShapeDtypeStruct((tokens, D_MODEL), F32),
                   jax.ShapeDtypeStruct((tokens, IN_TOTAL), MM),
                   jax.ShapeDtypeStruct((nt, 1, D_MODEL), F32),
                   jax.ShapeDtypeStruct((nt, 1, Q_WIDTH), F32),
                   jax.ShapeDtypeStruct((nt, 1, KV_WIDTH), F32)],
        scratch_shapes=[pltpu.VMEM((tm, IN_TOTAL), F32)],
        params=_cparams(("arbitrary",), 56))


def _pack_small(dg1, dg2, dg3, dg4, dgq, dgk, dsink, dbias, bucket, loss):
    def body(dg1_ref, dg2_ref, dg3_ref, dg4_ref, dgq_ref, dgk_ref, dsink_ref, dbias_ref, bucket_ref, loss_ref,
             out_ref, rel_ref):
        out_ref[...] = jnp.zeros_like(out_ref)
        for r, ref in ((ROW_G1, dg1_ref), (ROW_G2, dg2_ref), (ROW_G3, dg3_ref), (ROW_G4, dg4_ref)):
            acc = ref[0]
            for t in range(1, ref.shape[0]):
                acc = acc + ref[t]
            out_ref[r:r + 1, :] = acc

        def fold(ref, heads):
            acc = ref[0]
            for t in range(1, ref.shape[0]):
                acc = acc + ref[t]
            tot = acc[:, 0:HEAD_DIM]
            for h in range(1, heads):
                tot = tot + acc[:, h * HEAD_DIM:(h + 1) * HEAD_DIM]
            return tot

        out_ref[ROW_MISC:ROW_MISC + 1, MISC_GQ:MISC_GQ + HEAD_DIM] = fold(dgq_ref, GROUP * N_KV)
        out_ref[ROW_MISC:ROW_MISC + 1, MISC_GK:MISC_GK + HEAD_DIM] = fold(dgk_ref, N_KV)
        for h in range(GROUP * N_KV):
            g = h % GROUP
            out_ref[ROW_MISC:ROW_MISC + 1, MISC_SINK + h:MISC_SINK + h + 1] = jnp.sum(
                dsink_ref[h // GROUP, 0:1, g * BLOCK:(g + 1) * BLOCK], axis=-1, keepdims=True)
        lacc = loss_ref[0, 0:1, 0:1]
        for t in range(1, loss_ref.shape[0]):
            lacc = lacc + loss_ref[t, 0:1, 0:1]
        out_ref[ROW_MISC:ROW_MISC + 1, MISC_LOSS:MISC_LOSS + 1] = lacc
        lane = lax.broadcasted_iota(jnp.int32, (N_BUCKETS, LANES), 1)
        row = lax.broadcasted_iota(jnp.int32, (N_BUCKETS, LANES), 0)

        def per_bucket(b, acc):
            for h in range(GROUP * N_KV):
                g = h % GROUP
                sel = jnp.zeros((BLOCK, BLOCK), F32)
                for piece in range(3):
                    sel = sel + jnp.where(bucket_ref[piece] == b,
                                          dbias_ref[h // GROUP, piece, :, g * BLOCK:(g + 1) * BLOCK], 0.0)
                tot = jnp.sum(jnp.sum(sel, axis=0, keepdims=True), axis=-1, keepdims=True)
                acc = jnp.where((row == b) & (lane == h), tot, acc)
            return acc

        rel_ref[...] = lax.fori_loop(0, N_BUCKETS, per_bucket, jnp.zeros((N_BUCKETS, LANES), F32))

    args = (dg1, dg2, dg3, dg4, dgq, dgk, dsink, dbias, bucket, loss)
    outs = [jax.ShapeDtypeStruct((8, D_MODEL), F32), jax.ShapeDtypeStruct((N_BUCKETS, LANES), F32)]
    return pl.pallas_call(
        body, name="pack_small", grid=(1,),
        in_specs=[_whole(a) for a in args], out_specs=[_whole(o) for o in outs], out_shape=outs,
        compiler_params=pltpu.CompilerParams(vmem_limit_bytes=32 * 1024 * 1024),
    )(*map(_from_hbm, args))


def _gather_weights(shards, whole):
    n = len(shards)
    full = [t for t in range(n) if whole[t]]

    def body(*refs):
        ins, outs = refs[:n], refs[n:2 * n]
        raw, stage = refs[2 * n:3 * n], refs[3 * n:4 * n]
        load_sem, local_sem, ici_send, ici_recv, d2d_send, d2d_recv = refs[4 * n:]
        x, y, c = _place()
        k = 2 * x + y
        sibling = (x, y, 1 - c)
        order = full + [t for t in range(n) if t not in full]
        loads = {t: pltpu.make_async_copy(ins[t], raw[t], load_sem.at[t]) for t in order}
        for t in order:
            loads[t].start()
        copies, sends = [], []
        for t in order:
            loads[t].wait()
            stage[t][...] = raw[t][...].astype(MM)
            mine = pltpu.make_async_copy(stage[t], outs[t].at[k], local_sem.at[t])
            mine.start()
            copies.append(mine)
            if t in full:
                half = ins[t].shape[0] // 2
                rows = pl.ds(c * half, half)
                for r, (fx, fy) in enumerate(_CHIP_FLIPS):
                    cp = _remote(stage[t].at[rows], outs[t].at[k, rows], ici_send.at[t, r], ici_recv.at[t, r],
                                 (_flip(x, fx), _flip(y, fy), c))
                    cp.start()
                    sends.append(cp)
        for t in full:
            half = ins[t].shape[0] // 2
            rows = pl.ds(c * half, half)
            for r, (fx, fy) in enumerate(_CHIP_FLIPS):
                kk = 2 * _flip(x, fx) + _flip(y, fy)
                landed = outs[t].at[kk, rows]
                _remote(landed, landed, ici_send.at[t, r], ici_recv.at[t, r], sibling).wait_recv()
                fwd = _remote(landed, landed, d2d_send.at[t, r], d2d_recv.at[t, r], sibling)
                fwd.start()
                sends.append(fwd)
        for t in full:
            half = ins[t].shape[0] // 2
            other = pl.ds((1 - c) * half, half)
            for r, (fx, fy) in enumerate(_CHIP_FLIPS):
                kk = 2 * _flip(x, fx) + _flip(y, fy)
                theirs = outs[t].at[kk, other]
                _remote(theirs, theirs, d2d_send.at[t, r], d2d_recv.at[t, r], sibling).wait_recv()
        for cp in sends:
            cp.wait_send()
        for cp in copies:
            cp.wait()

    return pl.pallas_call(
        body, name="gather_weights",
        in_specs=[HBM] * n, out_specs=[HBM] * n,
        out_shape=[pltpu.HBM((N_CHIPS,) + s.shape, MM) for s in shards],
        scratch_shapes=[pltpu.VMEM(s.shape, F32) for s in shards] + [pltpu.VMEM(s.shape, MM) for s in shards] + [
            pltpu.SemaphoreType.DMA((n,)), pltpu.SemaphoreType.DMA((n,)),
            pltpu.SemaphoreType.DMA((n, 3)), pltpu.SemaphoreType.DMA((n, 3)),
            pltpu.SemaphoreType.DMA((n, 3)), pltpu.SemaphoreType.DMA((n, 3))],
        compiler_params=pltpu.CompilerParams(vmem_limit_bytes=40 * 1024 * 1024),
    )(*shards)


def _add_half(grad, got, where, *, name, tr):
    nch, half, cols = got.shape
    nblk = half // tr

    def body(where_ref, g_ref, r_ref, o_ref):
        o_ref[...] = (g_ref[...] + r_ref[...]).astype(MM)

    return pl.pallas_call(
        body, name=name,
        grid_spec=pltpu.PrefetchScalarGridSpec(
            num_scalar_prefetch=1, grid=(nch, nblk),
            in_specs=[pl.BlockSpec((1, tr, cols), lambda j, i, where_ref: (j, where_ref[1] * nblk + i, 0)),
                      pl.BlockSpec((1, tr, cols), lambda j, i, where_ref: (j, i, 0))],
            out_specs=pl.BlockSpec((1, tr, cols), lambda j, i, where_ref: (j, i, 0))),
        out_shape=jax.ShapeDtypeStruct(got.shape, MM),
        compiler_params=_cparams(("parallel", "parallel"), 32),
    )(where, grad, got)


def _add_chips(own, got, where, *, name, tr):
    _, half, cols = own.shape
    nblk = half // tr

    def body(where_ref, o_ref, g_ref, out_ref):
        f = lambda v: v.astype(F32)
        out_ref[...] = ((f(o_ref[0]) + f(g_ref[0])) + f(g_ref[1])) + f(g_ref[2])

    return pl.pallas_call(
        body, name=name,
        grid_spec=pltpu.PrefetchScalarGridSpec(
            num_scalar_prefetch=1, grid=(nblk,),
            in_specs=[pl.BlockSpec((1, tr, cols), lambda i, where_ref: (where_ref[0], i, 0)),
                      pl.BlockSpec((3, tr, cols), lambda i, where_ref: (0, i, 0))],
            out_specs=pl.BlockSpec((tr, cols), lambda i, where_ref: (where_ref[1] * nblk + i, 0))),
        out_shape=pltpu.HBM((2 * half, cols), F32),
        compiler_params=_cparams(("parallel",), 32),
    )(where, own, got)


def _small_job(tiles):
    n = len(tiles)

    def copies(ins, outs, sems):
        x, y, c = _place()
        me = 4 * x + 2 * y + c
        local, send, recv = sems
        cps = []
        for t in range(n):
            cps.append(pltpu.make_async_copy(ins[t], outs[t].at[me], local.at[t]))
            for r in range(1, N_DEV):
                fx, fy, fc = (r >> 2) & 1, (r >> 1) & 1, r & 1
                cps.append(_remote(ins[t], outs[t].at[me], send.at[t, r - 1], recv.at[t, r - 1],
                                   (_flip(x, fx), _flip(y, fy), _flip(c, fc))))
        return cps

    return _Job(tiles, [jax.ShapeDtypeStruct((N_DEV,) + t.shape, F32) for t in tiles],
                [pltpu.SemaphoreType.DMA((n,)), pltpu.SemaphoreType.DMA((n, N_DEV - 1)),
                 pltpu.SemaphoreType.DMA((n, N_DEV - 1))], copies)


def _adamw_math(w, g, m, v):
    m = ADAM_B1 * m + (1.0 - ADAM_B1) * g
    v = ADAM_B2 * v + (1.0 - ADAM_B2) * (g * g)
    m_hat = m / (1.0 - ADAM_B1 ** ADAM_STEP)
    v_hat = v / (1.0 - ADAM_B2 ** ADAM_STEP)
    delta = -ADAM_LR * (m_hat / (jnp.sqrt(v_hat) + ADAM_EPS) + ADAM_WD * w)
    return delta, m, v


def _adamw(w, g, m, v, *, name, tr):
    rows, cols = w.shape

    def body(w_ref, g_ref, m_ref, v_ref, go_ref, d_ref, nm_ref, nv_ref):
        g = g_ref[...]
        go_ref[...] = g
        d_ref[...], nm_ref[...], nv_ref[...] = _adamw_math(w_ref[...], g, m_ref[...], v_ref[...])

    spec = pl.BlockSpec((tr, cols), lambda i: (i, 0))
    return pl.pallas_call(
        body, name=name,
        grid=(rows // tr,),
        in_specs=[spec] * 4, out_specs=[spec] * 4,
        out_shape=[jax.ShapeDtypeStruct(w.shape, F32)] * 4,
        compiler_params=_cparams(("parallel",), 32),
    )(w, g, m, v)


def _small_adamw(gathered, gathered_rel, params, moments_m, moments_v):
    n = len(params)

    def body(all_ref, rel_all_ref, *refs):
        w_refs, m_refs, v_refs = refs[:n], refs[n:2 * n], refs[2 * n:3 * n]
        loss_ref = refs[3 * n]
        out_refs = refs[3 * n + 1:]
        g = all_ref[0]
        rel = rel_all_ref[0]
        for d in range(1, N_DEV):
            g = g + all_ref[d]
            rel = rel + rel_all_ref[d]
        misc = g[ROW_MISC:ROW_MISC + 1]
        loss_ref[...] = misc[:, MISC_LOSS:MISC_LOSS + 1]
        grads = (g[ROW_G1:ROW_G1 + 1], g[ROW_G2:ROW_G2 + 1], g[ROW_G3:ROW_G3 + 1], g[ROW_G4:ROW_G4 + 1],
                 misc[:, MISC_GQ:MISC_GQ + HEAD_DIM], misc[:, MISC_GK:MISC_GK + HEAD_DIM],
                 misc[:, MISC_SINK:MISC_SINK + GROUP * N_KV], rel[:, 0:GROUP * N_KV])
        for i in range(n):
            d, nm, nv = _adamw_math(w_refs[i][...], grads[i], m_refs[i][...], v_refs[i][...])
            for j, val in enumerate((grads[i], d, nm, nv)):
                out_refs[4 * i + j][...] = val

    args = (gathered, gathered_rel, *params, *moments_m, *moments_v)
    out_shape = [jax.ShapeDtypeStruct((1, 1), F32)] + [jax.ShapeDtypeStruct(p.shape, F32) for p in params
                                                       for _ in range(4)]
    outs = pl.pallas_call(
        body, name="small_adamw", grid=(1,),
        in_specs=[_whole(a) for a in args], out_specs=[_whole(o) for o in out_shape], out_shape=out_shape,
    )(*map(_from_hbm, args))
    return outs[0], [outs[1 + 4 * i:5 + 4 * i] for i in range(n)]


def kernel(x, w_in, w_o, g_pre_mix, g_post_mix, q_norm_a, k_norm_a, sink_b, rel_bias, g_pre_ffn, w_ffn_up, w_ffn_down, g_post_ffn, loss_target, m_w_in, m_w_o, m_g_pre_mix, m_g_post_mix, m_q_norm_a, m_k_norm_a, m_sink_b, m_rel_bias, m_g_pre_ffn, m_w_ffn_up, m_w_ffn_down, m_g_post_ffn, v_w_in, v_w_o, v_g_pre_mix, v_g_post_mix, v_q_norm_a, v_k_norm_a, v_sink_b, v_rel_bias, v_g_pre_ffn, v_w_ffn_up, v_w_ffn_down, v_g_post_ffn):
    batch, seq, _ = x.shape
    tokens = batch * seq
    where = jnp.stack([2 * lax.axis_index("x") + lax.axis_index("y"), lax.axis_index("c")]).astype(jnp.int32)
    x2 = x.reshape(tokens, D_MODEL)
    g1, g2, g3, g4 = g_pre_mix, g_post_mix, g_pre_ffn, g_post_ffn

    cos, sin = _rope_tables(seq)
    ck, sk = jnp.tile(cos, (1, 2)), jnp.tile(sin, (1, 2))
    gq8, gk2 = jnp.tile(q_norm_a, (1, 8)), jnp.tile(k_norm_a, (1, 2))
    bucket, band = _window_tables()
    bias = _bias_build(rel_bias.T, bucket, band)

    w_in_g, w_o_p, w_up_p, w_down_p = _gather_weights(
        (w_in[0], w_o[0], w_ffn_up[0], w_ffn_down[0]), whole=(True, False, False, False))
    (h1, raw, qa, ka, kta, va, vta, qtb, kb, ktb, vb, vtb) = _pre_proj(
        x2, g1, w_in_g, gq8, gk2, ck, sk, seq=seq, tm=min(512, seq), sub=256)
    (oa, p_a, linv_a), (w_part,) = _attn_a_fwd(
        qa, kta, va, seq=seq, bq=min(256, seq), jobs=[_gather_job([w_o_p, w_up_p, w_down_p], forward=False)])
    kb3 = kb.reshape(tokens // BLOCK, BLOCK, KV_WIDTH)
    vb3 = vb.reshape(tokens // BLOCK, BLOCK, KV_WIDTH)
    (ob, p_b, stat_b), ((w_o_g, w_up_g, w_down_g),) = _attn_b_fwd(
        qtb, kb3, vtb, bias, sink_b, seq=seq, per_step=min(16, seq // BLOCK),
        jobs=[_gather_job(w_part, forward=True)])
    w_o2 = w_o_g.reshape(D_MODEL, D_MODEL)
    w_down2 = w_down_g.reshape(D_FF, D_MODEL)
    mix, x1, h2, o_cat, u, df, dy, loss_t, dg4 = _mix_ffn_fwd(
        oa, ob, w_o2, x2, g2, g3, w_up_g, w_down2, loss_target.reshape(tokens, D_MODEL), g4, tm=256)

    dz, dx1, dmix, dg3, dg2, doa, dob = _ffn_bwd_act(df, w_down2, u, w_up_g, x1, dy, mix, g3, g2, w_o2, tm=256)
    gw_down, _ = _tn_matmul(u, df, name="grad_w_down", tm=1024, tn=1024, tk=min(4096, tokens), square_a=True,
                            vmem_mb=56)
    gw_down = gw_down.reshape(N_CHIPS, FF_CHUNK, D_MODEL)
    gw_up, ((got_down,),) = _tn_matmul(h2, dz, name="grad_w_up", tm=1024, tn=1024, tk=min(4096, tokens), chunk=FF_CHUNK,
                                        vmem_mb=56, jobs=[_swap_job([gw_down])])
    gw_o, _ = _tn_matmul(o_cat, dmix, name="grad_w_o", tm=1024, tn=1024, tk=min(2048, tokens))
    gw_o = gw_o.reshape(N_CHIPS, O_CHUNK, D_MODEL)
    sum_down = _add_half(gw_down, got_down, where, name="add_half_w_down", tr=128)
    (dqa, dkta, dvta), ((ex_down,), (got_up,)) = _attn_a_bwd(
        qa, ka, vta, doa, oa, p_a, linv_a, seq=seq, bq=min(256, seq),
        jobs=[_exchange_job([sum_down]), _swap_job([gw_up])])
    full_down = _add_chips(sum_down, ex_down, where, name="add_chips_w_down", tr=128)
    sum_up = _add_half(gw_up, got_up, where, name="add_half_w_up", tr=128)
    (dqb, dkb, dvb, dbias, dsink), ((ex_up,), (g_down,), (got_o,)) = _attn_b_bwd(
        qtb, ktb, vb3, dob, ob, p_b, stat_b, seq=seq, per_step=min(16, seq // BLOCK),
        jobs=[_exchange_job([sum_up]), _join_job([full_down]), _swap_job([gw_o])])
    full_up = _add_chips(sum_up, ex_up, where, name="add_chips_w_up", tr=128)
    sum_o = _add_half(gw_o, got_o, where, name="add_half_w_o", tr=128)
    (grad_x, dproj, dg1, dgq, dgk), _ = _proj_bwd(
        dqa, dkta, dvta, dqb, dkb, dvb, raw, x2, dx1, g1, w_in_g, gq8, gk2, ck, sk,
        seq=seq, tm=min(512, seq), sub=128)
    packed, packed_rel = _pack_small(dg1, dg2, dg3, dg4, dgq, dgk, dsink, dbias, bucket, loss_t)
    gw_in, ((ex_o,), (g_up,), (gathered, gathered_rel)) = _tn_matmul(
        h1, dproj, name="grad_w_in", tm=1024, tn=2 * IN_CHUNK, tk=min(4096, tokens), chunk=IN_CHUNK,
        vmem_mb=56, jobs=[_exchange_job([sum_o]), _join_job([full_up]), _small_job([packed, packed_rel])])
    full_o = _add_chips(sum_o, ex_o, where, name="add_chips_w_o", tr=128)

    (g_o,), (got_in,) = _run_jobs("tail_swap", [_join_job([full_o]), _swap_job([gw_in])])
    sum_in = _add_half(gw_in, got_in, where, name="add_half_w_in", tr=128)
    ((ex_in,),) = _run_jobs("tail_exchange", [_exchange_job([sum_in])])
    full_in = _add_chips(sum_in, ex_in, where, name="add_chips_w_in", tr=128)
    ((g_in,),) = _run_jobs("tail_join", [_join_job([full_in])])

    big = [[t[None] for t in _adamw(w[0], g, m[0], v[0], name="adamw_" + nm, tr=128)] for nm, w, g, m, v in (
        ("w_in", w_in, g_in, m_w_in, v_w_in), ("w_o", w_o, g_o, m_w_o, v_w_o),
        ("w_up", w_ffn_up, g_up, m_w_ffn_up, v_w_ffn_up), ("w_down", w_ffn_down, g_down, m_w_ffn_down, v_w_ffn_down))]

    loss, small = _small_adamw(
        gathered, gathered_rel,
        (g1, g2, g3, g4, q_norm_a, k_norm_a, sink_b, rel_bias),
        (m_g_pre_mix, m_g_post_mix, m_g_pre_ffn, m_g_post_ffn, m_q_norm_a, m_k_norm_a, m_sink_b, m_rel_bias),
        (v_g_pre_mix, v_g_post_mix, v_g_pre_ffn, v_g_post_ffn, v_q_norm_a, v_k_norm_a, v_sink_b, v_rel_bias))
    s_g1, s_g2, s_g3, s_g4, s_gq, s_gk, s_sink, s_rel = small

    def leaves(i):
        return (big[0][i], big[1][i], s_g1[i], s_g2[i], s_gq[i], s_gk[i], s_sink[i], s_rel[i], s_g3[i],
                big[2][i], big[3][i], s_g4[i])

    loss = loss.reshape(())
    return (loss, grad_x.reshape(batch, seq, D_MODEL), *leaves(0), *leaves(1), *leaves(2), *leaves(3))
```

```python
import functools

import jax
import jax.numpy as jnp
import numpy as np
from jax import lax
from jax.experimental import pallas as pl
from jax.experimental.pallas import tpu as pltpu

F32 = jnp.float32
MM = jnp.bfloat16

D_MODEL = 1024
HEAD_DIM = 64
N_KV = 2
GROUP = 4
Q_WIDTH = 512
KV_WIDTH = 128
D_FF = 4096
GRID_W = 64
BLOCK = 128
N_BUCKETS = 32
MAX_DISTANCE = 128
ROPE_THETA = 10000.0
EPS = 1e-6
NEG_INF = -1e30
SCALE = HEAD_DIM ** -0.5
IN_TOTAL = 1536
N_CHIPS = 4
N_DEV = 8
IN_CHUNK = IN_TOTAL // N_CHIPS
FF_CHUNK = D_FF // N_CHIPS
O_CHUNK = D_MODEL // N_CHIPS
QK_RAW = 640

ADAM_LR = 0.001
ADAM_B1 = 0.9
ADAM_B2 = 0.999
ADAM_EPS = 1e-08
ADAM_WD = 0.01
ADAM_STEP = 10

LANES = 128
MESH = pl.DeviceIdType.MESH
HBM = pl.BlockSpec(memory_space=pl.ANY)
SMEM = pl.BlockSpec(memory_space=pltpu.SMEM)

ROW_G1, ROW_G2, ROW_G3, ROW_G4, ROW_MISC = 0, 1, 2, 3, 4
MISC_GQ, MISC_GK, MISC_SINK, MISC_LOSS = 0, 64, 128, 512


def _cparams(sem, vmem_mb):
    return pltpu.CompilerParams(dimension_semantics=sem, vmem_limit_bytes=vmem_mb * 1024 * 1024)


def _whole(a):
    return pl.BlockSpec(a.shape, lambda i: (0,) * len(a.shape))


def _from_hbm(a):
    return pltpu.with_memory_space_constraint(a, pltpu.HBM)


def _in_hbm(s):
    return pltpu.HBM(s.shape, s.dtype)


class _Job:
    def __init__(self, operands, out_shapes, sems, copies, alias=None):
        self.operands, self.out_shapes, self.sems, self.copies = list(operands), list(out_shapes), list(sems), copies
        self.alias = dict(alias or {})


def _place():
    return lax.axis_index("x"), lax.axis_index("y"), lax.axis_index("c")


_CHIP_FLIPS = ((1, 0), (0, 1), (1, 1))


def _flip(v, bit):
    return 1 - v if bit else v


def _remote(src, dst, send, recv, dev):
    return pltpu.make_async_remote_copy(src_ref=src, dst_ref=dst, send_sem=send, recv_sem=recv,
                                        device_id=dev, device_id_type=MESH)


def _swap_job(grads):
    n = len(grads)

    def copies(ins, outs, sems):
        x, y, c = _place()
        send, recv = sems
        cps = []
        for t in range(n):
            half = ins[t].shape[1] // 2
            cps.append(_remote(ins[t].at[:, pl.ds((1 - c) * half, half), :], outs[t], send.at[t], recv.at[t],
                               (x, y, 1 - c)))
        return cps

    shapes = [jax.ShapeDtypeStruct((g.shape[0], g.shape[1] // 2, g.shape[2]), F32) for g in grads]
    return _Job(grads, shapes, [pltpu.SemaphoreType.DMA((n,)), pltpu.SemaphoreType.DMA((n,))], copies)


def _exchange_job(sums):
    n = len(sums)

    def copies(ins, outs, sems):
        x, y, c = _place()
        send, recv = sems
        cps = []
        for t in range(n):
            for r, (fx, fy) in enumerate(_CHIP_FLIPS):
                kk = 2 * _flip(x, fx) + _flip(y, fy)
                cps.append(_remote(ins[t].at[kk], outs[t].at[r], send.at[t, r], recv.at[t, r],
                                   (_flip(x, fx), _flip(y, fy), c)))
        return cps

    shapes = [jax.ShapeDtypeStruct((3,) + s.shape[1:], s.dtype) for s in sums]
    return _Job(sums, shapes, [pltpu.SemaphoreType.DMA((n, 3)), pltpu.SemaphoreType.DMA((n, 3))], copies)


def _join_job(fulls):
    n = len(fulls)

    def copies(ins, outs, sems):
        x, y, c = _place()
        send, recv = sems
        cps = []
        for t in range(n):
            half = ins[t].shape[0] // 2
            rows = pl.ds(c * half, half)
            cps.append(_remote(ins[t].at[rows], outs[t].at[rows], send.at[t], recv.at[t], (x, y, 1 - c)))
        return cps

    shapes = [jax.ShapeDtypeStruct(f.shape, f.dtype) for f in fulls]
    return _Job(fulls, shapes, [pltpu.SemaphoreType.DMA((n,)), pltpu.SemaphoreType.DMA((n,))], copies,
                alias={t: t for t in range(n)})


def _gather_job(bufs, forward):
    n = len(bufs)

    def copies(ins, outs, sems):
        x, y, c = _place()
        send, recv = sems
        cps = []
        for t in range(n):
            half = ins[t].shape[1] // 2
            rows = pl.ds(c * half, half)
            for r, (fx, fy) in enumerate(_CHIP_FLIPS):
                if forward:
                    kk = 2 * _flip(x, fx) + _flip(y, fy)
                    dev = (x, y, 1 - c)
                else:
                    kk = 2 * x + y
                    dev = (_flip(x, fx), _flip(y, fy), c)
                cps.append(_remote(ins[t].at[kk, rows], outs[t].at[kk, rows], send.at[t, r], recv.at[t, r], dev))
        return cps

    shapes = [jax.ShapeDtypeStruct(b.shape, b.dtype) for b in bufs]
    return _Job(bufs, shapes, [pltpu.SemaphoreType.DMA((n, 3)), pltpu.SemaphoreType.DMA((n, 3))], copies,
                alias={t: t for t in range(n)})


def _call(body, args, *, name, grid, in_specs, out_specs, out_shape, scratch_shapes=(), params=None, jobs=()):
    n_in, n_out, n_scr = len(in_specs), len(out_specs), len(scratch_shapes)
    job_in = [len(j.operands) for j in jobs]
    job_out = [len(j.out_shapes) for j in jobs]
    job_sem = [len(j.sems) for j in jobs]

    def wrapped(*refs):
        pos = 0
        ins = refs[pos:pos + n_in]; pos += n_in
        jins = []
        for k in job_in:
            jins.append(refs[pos:pos + k]); pos += k
        outs = refs[pos:pos + n_out]; pos += n_out
        jouts = []
        for k in job_out:
            jouts.append(refs[pos:pos + k]); pos += k
        scr = refs[pos:pos + n_scr]; pos += n_scr
        jsems = []
        for k in job_sem:
            jsems.append(refs[pos:pos + k]); pos += k
        if jobs:
            ids = [pl.program_id(d) for d in range(len(grid))]
            first = functools.reduce(jnp.logical_and, [i == 0 for i in ids])
            last = functools.reduce(jnp.logical_and, [i == g - 1 for i, g in zip(ids, grid)])

            @pl.when(first)
            def _():
                for j, ji, jo, js in zip(jobs, jins, jouts, jsems):
                    for cp in j.copies(ji, jo, js):
                        cp.start()

        body(*ins, *outs, *scr)
        if jobs:
            @pl.when(last)
            def _():
                for j, ji, jo, js in zip(jobs, jins, jouts, jsems):
                    for cp in j.copies(ji, jo, js):
                        cp.wait()

    aliases = {}
    in_pos, out_pos = n_in, n_out
    for j in jobs:
        for i, o in j.alias.items():
            aliases[in_pos + i] = out_pos + o
        in_pos += len(j.operands)
        out_pos += len(j.out_shapes)
    res = pl.pallas_call(
        wrapped, name=name, grid=grid,
        in_specs=list(in_specs) + [HBM] * sum(job_in),
        out_specs=list(out_specs) + [HBM] * sum(job_out),
        out_shape=list(out_shape) + [_in_hbm(s) for j in jobs for s in j.out_shapes],
        scratch_shapes=list(scratch_shapes) + [s for j in jobs for s in j.sems],
        input_output_aliases=aliases,
        compiler_params=params,
    )(*[a if spec is SMEM else _from_hbm(a) for a, spec in zip(args, in_specs)],
      *[a for j in jobs for a in j.operands])
    own, rest = list(res[:n_out]), list(res[n_out:])
    job_res = []
    for k in job_out:
        job_res.append(rest[:k])
        rest = rest[k:]
    return own, job_res


def _run_jobs(name, jobs):
    def body():
        pass

    return _call(body, (), name=name, grid=(1,), in_specs=[], out_specs=[], out_shape=[], jobs=jobs)[1]


def _dot(a, b):
    return jnp.dot(a, b, preferred_element_type=F32)


def _dot_nt(a, b):
    return lax.dot_general(a, b, (((1,), (1,)), ((), ())), preferred_element_type=F32)


def _dot_tn(a, b):
    return lax.dot_general(a, b, (((0,), (0,)), ((), ())), preferred_element_type=F32)


def _rms_r(x):
    return lax.rsqrt(jnp.mean(x * x, axis=-1, keepdims=True) + EPS)


def _rms_bwd(x, r, g, dy):
    n = x * r
    dn = dy * g
    dx = r * (dn - n * jnp.mean(dn * n, axis=-1, keepdims=True))
    return dx, dy * n


def _seg64_sum(v):
    rows, width = v.shape
    lane = lax.broadcasted_iota(jnp.int32, (rows, LANES), 1)
    lo = lane < HEAD_DIM
    outs = []
    for c in range(width // LANES):
        ch = v[:, c * LANES:(c + 1) * LANES]
        s_lo = jnp.sum(jnp.where(lo, ch, 0.0), axis=-1, keepdims=True)
        s_hi = jnp.sum(jnp.where(lo, 0.0, ch), axis=-1, keepdims=True)
        outs.append(jnp.where(lo, s_lo, s_hi))
    return outs[0] if len(outs) == 1 else jnp.concatenate(outs, axis=-1)


def _head_r(v):
    return lax.rsqrt(_seg64_sum(v * v) * (1.0 / HEAD_DIM) + EPS)


def _swap16(ch):
    lane = lax.broadcasted_iota(jnp.int32, ch.shape, 1)
    return jnp.where((lane % 32) < 16, pltpu.roll(ch, LANES - 16, 1), pltpu.roll(ch, 16, 1))


def _by_chunk(fn, v):
    outs = [fn(v[:, c * LANES:(c + 1) * LANES]) for c in range(v.shape[1] // LANES)]
    return outs[0] if len(outs) == 1 else jnp.concatenate(outs, axis=-1)


def _rope(v, cos, sin_signed):
    return _by_chunk(lambda ch: ch * cos + _swap16(ch) * sin_signed, v)


def _rope_t(g, cos, sin_signed):
    return _by_chunk(lambda ch: ch * cos + _swap16(ch * sin_signed), g)


def _rope_tables(seq):
    nf = HEAD_DIM // 4
    freqs = ROPE_THETA ** (-jnp.arange(nf, dtype=F32) / nf)
    pos = jnp.arange(seq, dtype=jnp.int32)
    row = (pos // GRID_W).astype(F32)
    col = (pos % GRID_W).astype(F32)
    ang_r = row[:, None] * freqs[None, :]
    ang_c = col[:, None] * freqs[None, :]
    cr, sr, cc, sc = jnp.cos(ang_r), jnp.sin(ang_r), jnp.cos(ang_c), jnp.sin(ang_c)
    cos = jnp.concatenate([cr, cr, cc, cc], axis=1)
    sin = jnp.concatenate([-sr, sr, -sc, sc], axis=1)
    return cos, sin


def _t5_bucket(rel):
    nb = N_BUCKETS // 2
    ret = (rel > 0).astype(jnp.int32) * nb
    n = jnp.abs(rel)
    max_exact = nb // 2
    nf = jnp.maximum(n, 1).astype(jnp.float32)
    large = max_exact + (jnp.log(nf / max_exact) / np.float32(np.log(MAX_DISTANCE / max_exact))
                         * (nb - max_exact)).astype(jnp.int32)
    large = jnp.minimum(large, nb - 1)
    return ret + jnp.where(n < max_exact, n, large)


def _window_tables():
    a = jnp.arange(BLOCK, dtype=jnp.int32)
    c = jnp.arange(3 * BLOCK, dtype=jnp.int32)
    rel = c[None, :] - BLOCK - a[:, None]
    bucket = _t5_bucket(rel)
    band = (jnp.abs(rel) <= BLOCK).astype(jnp.int32)
    to3 = lambda t: t.reshape(BLOCK, 3, BLOCK).transpose(1, 2, 0)
    return to3(bucket), to3(band)


def _pre_proj(x, g1, w_in, gq, gk, ck, sk, *, seq, tm, sub):
    tokens = x.shape[0]
    n_seq = seq // tm
    nblk = tm // BLOCK
    batch = tokens // seq

    def body(x_ref, g1_ref, w_ref, gq_ref, gk_ref, ck_ref, sk_ref,
             h1_ref, raw_ref, qa_ref, ka_ref, kta_ref, va_ref, vta_ref,
             qtb_ref, kb_ref, ktb_ref, vb_ref, vtb_ref, proj):
        for r in range(tm // sub):
            rows = slice(r * sub, (r + 1) * sub)
            xv = x_ref[rows, :]
            h = (xv * _rms_r(xv) * g1_ref[...]).astype(MM)
            h1_ref[rows, :] = h
            for j in range(N_CHIPS):
                proj[rows, j * IN_CHUNK:(j + 1) * IN_CHUNK] = _dot(h, w_ref[j])
            qa = proj[rows, 0:Q_WIDTH]
            ka = proj[rows, Q_WIDTH:QK_RAW]
            raw_ref[rows, :] = proj[rows, 0:QK_RAW]
            qn = qa * _head_r(qa) * gq_ref[...]
            qa_ref[rows, :] = (_rope(qn, ck_ref[rows, :], sk_ref[rows, :]) * SCALE).astype(MM)
            kn = ka * _head_r(ka) * gk_ref[...]
            kr = _rope(kn, ck_ref[rows, :], sk_ref[rows, :])
            ka_ref[rows, :] = kr.astype(MM)
            kta_ref[0, :, rows] = kr.T.astype(MM)
            va = proj[rows, 640:768]
            va_ref[rows, :] = va.astype(MM)
            vta_ref[0, :, rows] = va.T.astype(MM)
            qb = proj[rows, 768:1280] * SCALE
            kb = proj[rows, 1280:1408]
            vb = proj[rows, 1408:1536]
            kb_ref[rows, :] = kb.astype(MM)
            vb_ref[rows, :] = vb.astype(MM)
            for j in range(sub // BLOCK):
                blk = slice(j * BLOCK, (j + 1) * BLOCK)
                qtb_ref[r * (sub // BLOCK) + j] = qb[blk, :].T.astype(MM)
                ktb_ref[r * (sub // BLOCK) + j] = kb[blk, :].T.astype(MM)
                vtb_ref[r * (sub // BLOCK) + j] = vb[blk, :].T.astype(MM)

    tok = lambda w: pl.BlockSpec((tm, w), lambda i: (i, 0))
    tab = lambda w: pl.BlockSpec((tm, w), lambda i: (i % n_seq, 0))
    row = lambda w: pl.BlockSpec((1, w), lambda i: (0, 0))
    tposed = pl.BlockSpec((1, LANES, tm), lambda i: (i // n_seq, 0, i % n_seq))
    blocks = pl.BlockSpec((nblk, BLOCK, LANES), lambda i: (i, 0, 0))
    qblocks = pl.BlockSpec((nblk, Q_WIDTH, BLOCK), lambda i: (i, 0, 0))
    tok_mm = lambda w: jax.ShapeDtypeStruct((tokens, w), MM)
    return pl.pallas_call(
        body, name="pre_proj",
        grid=(tokens // tm,),
        in_specs=[tok(D_MODEL), row(D_MODEL),
                  pl.BlockSpec((N_CHIPS, D_MODEL, IN_CHUNK), lambda i: (0, 0, 0)),
                  row(Q_WIDTH), row(KV_WIDTH), tab(KV_WIDTH), tab(KV_WIDTH)],
        out_specs=[tok(D_MODEL), tok(QK_RAW), tok(Q_WIDTH), tok(KV_WIDTH), tposed, tok(KV_WIDTH), tposed,
                   qblocks, tok(KV_WIDTH), blocks, tok(KV_WIDTH), blocks],
        out_shape=[
            tok_mm(D_MODEL),
            jax.ShapeDtypeStruct((tokens, QK_RAW), F32),
            tok_mm(Q_WIDTH),
            tok_mm(KV_WIDTH),
            jax.ShapeDtypeStruct((batch, KV_WIDTH, seq), MM),
            tok_mm(KV_WIDTH),
            jax.ShapeDtypeStruct((batch, KV_WIDTH, seq), MM),
            jax.ShapeDtypeStruct((tokens // BLOCK, Q_WIDTH, BLOCK), MM),
            tok_mm(KV_WIDTH),
            jax.ShapeDtypeStruct((tokens // BLOCK, KV_WIDTH, BLOCK), MM),
            tok_mm(KV_WIDTH),
            jax.ShapeDtypeStruct((tokens // BLOCK, KV_WIDTH, BLOCK), MM),
        ],
        scratch_shapes=[pltpu.VMEM((tm, IN_TOTAL), F32)],
        compiler_params=_cparams(("parallel",), 48),
    )(*map(_from_hbm, (x, g1, w_in, gq, gk, ck, sk)))


def _kv_half(v2, kv):
    return jnp.where(kv == 0, v2[:, :HEAD_DIM], v2[:, HEAD_DIM:])


def _attn_a_fwd(qa, kta, va, *, seq, bq, jobs=()):
    tokens = qa.shape[0]
    batch = tokens // seq
    nq = seq // bq

    def body(q_ref, kt_ref, v_ref, o_ref, p_ref, linv_ref):
        kv = pl.program_id(1)
        kt = kt_ref[0]
        lane = lax.broadcasted_iota(jnp.int32, (seq, KV_WIDTH), 1)
        v = jnp.where((lane < HEAD_DIM) == (kv == 0), v_ref[...], jnp.ones((), MM))
        for g in range(GROUP):
            sl = slice(g * HEAD_DIM, (g + 1) * HEAD_DIM)
            s = _dot(q_ref[:, sl], kt)
            pb = jnp.exp((s - jnp.max(s, axis=-1, keepdims=True)).astype(MM))
            p_ref[0, g] = pb
            o2 = _dot(pb, v)
            linv = 1.0 / _kv_half(o2, 1 - kv)[:, 0:1]
            o_ref[:, sl] = _kv_half(o2, kv) * linv
            linv_ref[0, :, g:g + 1] = linv

    return _call(
        body, (qa, kta, va), name="attn_a_fwd", jobs=jobs,
        grid=(batch, N_KV, nq),
        in_specs=[pl.BlockSpec((bq, GROUP * HEAD_DIM), lambda b, k, i: (b * nq + i, k)),
                  pl.BlockSpec((1, HEAD_DIM, seq), lambda b, k, i: (b, k, 0)),
                  pl.BlockSpec((seq, KV_WIDTH), lambda b, k, i: (b, 0))],
        out_specs=[pl.BlockSpec((bq, GROUP * HEAD_DIM), lambda b, k, i: (b * nq + i, k)),
                   pl.BlockSpec((1, GROUP, bq, seq), lambda b, k, i: (k, 0, b * nq + i, 0)),
                   pl.BlockSpec((1, bq, GROUP), lambda b, k, i: (k, b * nq + i, 0))],
        out_shape=[jax.ShapeDtypeStruct((tokens, Q_WIDTH), F32),
                   jax.ShapeDtypeStruct((N_KV, GROUP, tokens, seq), MM),
                   jax.ShapeDtypeStruct((N_KV, tokens, GROUP), F32)],
        params=_cparams(("arbitrary", "arbitrary", "arbitrary"), 56))


def _attn_a_bwd(qa, ka, vta, do, o, p, linv, *, seq, bq, jobs=()):
    tokens = qa.shape[0]
    batch = tokens // seq
    nq = seq // bq

    def body(q_ref, k_ref, vt_ref, do_ref, o_ref, p_ref, linv_ref, dq_ref, dkt_ref, dvt_ref):
        kv = pl.program_id(1)

        @pl.when(pl.program_id(2) == 0)
        def _():
            dkt_ref[...] = jnp.zeros_like(dkt_ref)
            dvt_ref[...] = jnp.zeros_like(dvt_ref)

        vt = vt_ref[0]
        k2 = k_ref[...]
        for g in range(GROUP):
            sl = slice(g * HEAD_DIM, (g + 1) * HEAD_DIM)
            dof = do_ref[:, sl]
            delta = jnp.sum(dof * o_ref[:, sl], axis=-1, keepdims=True)
            linv_g = linv_ref[0, :, g:g + 1]
            pb = p_ref[0, g]
            dp = _dot(dof.astype(MM), vt)
            ds = pb * ((dp - delta) * linv_g).astype(MM)
            dq_ref[:, sl] = _kv_half(_dot(ds, k2), kv)
            dkt_ref[0] += _dot_tn(q_ref[:, sl], ds)
            dvt_ref[0] += _dot_tn((dof * linv_g).astype(MM), pb)

    qspec = pl.BlockSpec((bq, GROUP * HEAD_DIM), lambda b, k, i: (b * nq + i, k))
    tspec = pl.BlockSpec((1, HEAD_DIM, seq), lambda b, k, i: (b, k, 0))
    return _call(
        body, (qa, ka, vta, do, o, p, linv), name="attn_a_bwd", jobs=jobs,
        grid=(batch, N_KV, nq),
        in_specs=[qspec, pl.BlockSpec((seq, KV_WIDTH), lambda b, k, i: (b, 0)), tspec, qspec, qspec,
                  pl.BlockSpec((1, GROUP, bq, seq), lambda b, k, i: (k, 0, b * nq + i, 0)),
                  pl.BlockSpec((1, bq, GROUP), lambda b, k, i: (k, b * nq + i, 0))],
        out_specs=[qspec, tspec, tspec],
        out_shape=[jax.ShapeDtypeStruct((tokens, Q_WIDTH), F32),
                   jax.ShapeDtypeStruct((batch, KV_WIDTH, seq), F32),
                   jax.ShapeDtypeStruct((batch, KV_WIDTH, seq), F32)],
        params=_cparams(("arbitrary", "arbitrary", "arbitrary"), 56))


def _bias_build(rel_bias_t, bucket_t, band_t):
    def body(tab_ref, bucket_ref, band_ref, bias_ref):
        for h in range(GROUP * N_KV):
            for piece in range(3):
                bk = bucket_ref[piece]
                acc = jnp.zeros((BLOCK, BLOCK), F32)
                for b in range(N_BUCKETS):
                    acc = jnp.where(bk == b, tab_ref[h, b], acc)
                g = h % GROUP
                bias_ref[h // GROUP, piece, :, g * BLOCK:(g + 1) * BLOCK] = jnp.where(band_ref[piece] != 0, acc, NEG_INF)

    out = jax.ShapeDtypeStruct((N_KV, 3, BLOCK, GROUP * BLOCK), F32)
    return pl.pallas_call(
        body, name="bias_build", grid=(1,),
        in_specs=[SMEM, _whole(bucket_t), _whole(band_t)], out_specs=_whole(out), out_shape=out,
    )(rel_bias_t, bucket_t, band_t)


def _pad_heads(t, kv):
    outs = []
    for g in range(GROUP):
        tg = t[g * HEAD_DIM:(g + 1) * HEAD_DIM, :]
        zero = jnp.zeros_like(tg)
        outs.append(jnp.concatenate([jnp.where(kv == 0, tg, zero), jnp.where(kv == 0, zero, tg)], axis=0))
    return jnp.concatenate(outs, axis=-1)


def _unpad_heads(t, kv):
    outs = [_kv_half(t[:, g * BLOCK:(g + 1) * BLOCK].T, kv) for g in range(GROUP)]
    return jnp.concatenate(outs, axis=-1)


def _sink_row(sink_ref, kv):
    lane_head = lax.broadcasted_iota(jnp.int32, (1, GROUP * BLOCK), 1) // BLOCK
    row = jnp.zeros((1, GROUP * BLOCK), F32)
    for g in range(GROUP):
        row = jnp.where(lane_head == g, sink_ref[0, kv * GROUP + g], row)
    return row


def _window_scores_t(k_ref, idx, qpad, bias_ref, n, nblk):
    pieces = []
    for piece in range(3):
        s = _dot(k_ref[idx[piece]], qpad) + bias_ref[0, piece]
        if piece == 0:
            s = jnp.where(n > 0, s, NEG_INF)
        if piece == 2:
            s = jnp.where(n < nblk - 1, s, NEG_INF)
        pieces.append(s)
    return pieces


def _attn_b_fwd(qtb, kb3, vtb, bias, sink, *, seq, per_step, jobs=()):
    nblk_all = qtb.shape[0]
    tokens = nblk_all * BLOCK
    batch = tokens // seq
    nblk = seq // BLOCK
    nstep = nblk // per_step

    def body(sink_ref, q_ref, k_ref, vt_ref, bias_ref, o_ref, p_ref, stat_ref):
        kv = pl.program_id(0)
        first = pl.program_id(2) * per_step
        sink_row = _sink_row(sink_ref, kv)
        stat_row = lax.broadcasted_iota(jnp.int32, (8, GROUP * BLOCK), 0)

        def block(i, carry):
            n = first + i
            idx = (jnp.maximum(n - 1, 0), n, jnp.minimum(n + 1, nblk - 1))
            rows = slice(i * BLOCK, (i + 1) * BLOCK)
            qpad = _pad_heads(q_ref[n], kv)
            ss = _window_scores_t(k_ref, idx, qpad, bias_ref, n, nblk)
            m = jnp.maximum(jnp.maximum(jnp.max(ss[0], axis=0, keepdims=True),
                                        jnp.max(ss[1], axis=0, keepdims=True)),
                            jnp.maximum(jnp.max(ss[2], axis=0, keepdims=True), sink_row))
            ps = [jnp.exp(s - m) for s in ss]
            e_sink = jnp.exp(sink_row - m)
            rinv = 1.0 / (jnp.sum(ps[0], axis=0, keepdims=True) + jnp.sum(ps[1], axis=0, keepdims=True)
                          + jnp.sum(ps[2], axis=0, keepdims=True) + e_sink)
            ot = jnp.zeros((KV_WIDTH, GROUP * BLOCK), F32)
            for piece in range(3):
                pb = ps[piece].astype(MM)
                p_ref[0, i, piece] = pb
                ot = ot + _dot(vt_ref[idx[piece]], pb)
            o_ref[rows, :] = _unpad_heads(ot * rinv, kv)
            stat_ref[0, i] = jnp.where(stat_row == 0, rinv, e_sink * rinv)
            return carry

        for i in range(per_step):
            block(i, 0)

    both = pl.BlockSpec((nblk, BLOCK, KV_WIDTH), lambda k, b, j: (b, 0, 0))
    return _call(
        body, (sink, qtb, kb3, vtb, bias), name="attn_b_fwd", jobs=jobs,
        grid=(N_KV, batch, nstep),
        in_specs=[SMEM, pl.BlockSpec((nblk, GROUP * HEAD_DIM, BLOCK), lambda k, b, j: (b, k, 0)), both, both,
                  pl.BlockSpec((1, 3, BLOCK, GROUP * BLOCK), lambda k, b, j: (k, 0, 0, 0))],
        out_specs=[pl.BlockSpec((per_step * BLOCK, GROUP * HEAD_DIM), lambda k, b, j: (b * nstep + j, k)),
                   pl.BlockSpec((1, per_step, 3, BLOCK, GROUP * BLOCK), lambda k, b, j: (k, b * nstep + j, 0, 0, 0)),
                   pl.BlockSpec((1, per_step, 8, GROUP * BLOCK), lambda k, b, j: (k, b * nstep + j, 0, 0))],
        out_shape=[jax.ShapeDtypeStruct((tokens, Q_WIDTH), F32),
                   jax.ShapeDtypeStruct((N_KV, nblk_all, 3, BLOCK, GROUP * BLOCK), MM),
                   jax.ShapeDtypeStruct((N_KV, nblk_all, 8, GROUP * BLOCK), F32)],
        params=_cparams(("arbitrary", "arbitrary", "arbitrary"), 48))


def _attn_b_bwd(qtb, ktb, vb3, do, o, p, stat, *, seq, per_step, jobs=()):
    nblk_all = qtb.shape[0]
    tokens = nblk_all * BLOCK
    batch = tokens // seq
    nblk = seq // BLOCK
    nstep = nblk // per_step

    def body(q_ref, kt_ref, v_ref, do_ref, o_ref, p_ref, stat_ref,
             dq_ref, dk_ref, dv_ref, dbias_ref, dsink_ref):
        kv = pl.program_id(0)
        step = pl.program_id(2)
        first = step * per_step

        @pl.when(jnp.logical_and(pl.program_id(1) == 0, step == 0))
        def _():
            dbias_ref[...] = jnp.zeros_like(dbias_ref)
            dsink_ref[...] = jnp.zeros_like(dsink_ref)

        @pl.when(step == 0)
        def _():
            dk_ref[...] = jnp.zeros_like(dk_ref)
            dv_ref[...] = jnp.zeros_like(dv_ref)

        def block(i, dsink):
            n = first + i
            idx = (jnp.maximum(n - 1, 0), n, jnp.minimum(n + 1, nblk - 1))
            rows = slice(i * BLOCK, (i + 1) * BLOCK)
            qpad = _pad_heads(q_ref[n], kv)
            dot_t = do_ref[rows, :].T
            prod = dot_t * o_ref[rows, :].T
            delta = jnp.concatenate(
                [jnp.sum(prod[g * HEAD_DIM:(g + 1) * HEAD_DIM, :], axis=0, keepdims=True) for g in range(GROUP)],
                axis=-1)
            stats = stat_ref[0, i]
            rinv, p_sink = stats[0:1, :], stats[1:2, :]
            dopad32 = _pad_heads(dot_t, kv)
            dopad = dopad32.astype(MM)
            dopad_n = (dopad32 * rinv).astype(MM)
            dqt = jnp.zeros((KV_WIDTH, GROUP * BLOCK), F32)
            for piece in range(3):
                pb = p_ref[0, i, piece]
                dst = pb.astype(F32) * ((_dot(v_ref[idx[piece]], dopad) - delta) * rinv)
                dsb = dst.astype(MM)
                dbias_ref[0, piece] += dst
                dqt = dqt + _dot(kt_ref[idx[piece]], dsb)
                dk_ref[0, idx[piece]] += _dot_nt(dsb, qpad)
                dv_ref[0, idx[piece]] += _dot_nt(pb, dopad_n)
            dq_ref[rows, :] = _unpad_heads(dqt, kv)
            return dsink - p_sink * delta

        dsink = jnp.zeros((1, GROUP * BLOCK), F32)
        for i in range(per_step):
            dsink = block(i, dsink)
        dsink_ref[0] += jnp.broadcast_to(dsink, (8, GROUP * BLOCK))

    qspec = pl.BlockSpec((per_step * BLOCK, GROUP * HEAD_DIM), lambda k, b, j: (b * nstep + j, k))
    both = pl.BlockSpec((nblk, BLOCK, KV_WIDTH), lambda k, b, j: (b, 0, 0))
    grad =pl.BlockSpec((1, nblk, BLOCK, KV_WIDTH), lambda k, b, j: (k, b, 0, 0))
    return _call(
        body, (qtb, ktb, vb3, do, o, p, stat), name="attn_b_bwd", jobs=jobs,
        grid=(N_KV, batch, nstep),
        in_specs=[pl.BlockSpec((nblk, GROUP * HEAD_DIM, BLOCK), lambda k, b, j: (b, k, 0)), both, both,
                  qspec, qspec,
                  pl.BlockSpec((1, per_step, 3, BLOCK, GROUP * BLOCK), lambda k, b, j: (k, b * nstep + j, 0, 0, 0)),
                  pl.BlockSpec((1, per_step, 8, GROUP * BLOCK), lambda k, b, j: (k, b * nstep + j, 0, 0))],
        out_specs=[qspec, grad, grad,
                   pl.BlockSpec((1, 3, BLOCK, GROUP * BLOCK), lambda k, b, j: (k, 0, 0, 0)),
                   pl.BlockSpec((1, 8, GROUP * BLOCK), lambda k, b, j: (k, 0, 0))],
        out_shape=[jax.ShapeDtypeStruct((tokens, Q_WIDTH), F32),
                   jax.ShapeDtypeStruct((N_KV, nblk_all, BLOCK, KV_WIDTH), F32),
                   jax.ShapeDtypeStruct((N_KV, nblk_all, BLOCK, KV_WIDTH), F32),
                   jax.ShapeDtypeStruct((N_KV, 3, BLOCK, GROUP * BLOCK), F32),
                   jax.ShapeDtypeStruct((N_KV, 8, GROUP * BLOCK), F32)],
        params=_cparams(("arbitrary", "arbitrary", "arbitrary"), 56))


def _resident(shape):
    return pl.BlockSpec(shape, lambda i: (0,) * len(shape), pipeline_mode=pl.Buffered(1))


def _mix_ffn_fwd(oa, ob, w_o, x, g2, g3, w_up, w_down, target, g4, *, tm):
    tokens = x.shape[0]
    nt = tokens // tm

    def body(oa_ref, ob_ref, wo_ref, x_ref, g2_ref, g3_ref, wu_ref, wd_ref, t_ref, g4_ref,
             mix_ref, x1_ref, h2_ref, o_ref, u_ref, df_ref, dy_ref, loss_ref, dg4_ref):
        o = jnp.concatenate([oa_ref[...].astype(MM), ob_ref[...].astype(MM)], axis=-1)
        o_ref[...] = o
        mix = _dot(o, wo_ref[...])
        mix_ref[...] = mix
        x1 = x_ref[...] + mix * _rms_r(mix) * g2_ref[...]
        x1_ref[...] = x1
        h2v = (x1 * _rms_r(x1) * g3_ref[...]).astype(MM)
        h2_ref[...] = h2v
        f = jnp.zeros((tm, D_MODEL), F32)
        for c in range(N_CHIPS):
            u = jnp.maximum(_dot(h2v, wu_ref[c]), 0.0)
            u_ref[:, c * FF_CHUNK:(c + 1) * FF_CHUNK] = u.astype(MM)
            f = f + _dot((u * u).astype(MM), wd_ref[c * FF_CHUNK:(c + 1) * FF_CHUNK, :])
        r = _rms_r(f)
        g4v = g4_ref[...]
        err = x1 + f * r * g4v - t_ref[...]
        sq = jnp.sum(err * err, axis=-1, keepdims=True)
        loss_ref[0] = jnp.broadcast_to(jnp.sum(sq, axis=0, keepdims=True) * (0.5 / D_MODEL), (8, LANES))
        dy = err * (1.0 / D_MODEL)
        dy_ref[...] = dy
        dfv, dgv = _rms_bwd(f, r, g4v, dy)
        df_ref[...] = dfv.astype(MM)
        dg4_ref[0] = jnp.sum(dgv, axis=0, keepdims=True)

    tok = pl.BlockSpec((tm, D_MODEL), lambda i: (i, 0))
    half = pl.BlockSpec((tm, Q_WIDTH), lambda i: (i, 0))
    row = pl.BlockSpec((1, D_MODEL), lambda i: (0, 0))
    tok_f32 = jax.ShapeDtypeStruct((tokens, D_MODEL), F32)
    tok_mm = jax.ShapeDtypeStruct((tokens, D_MODEL), MM)
    return pl.pallas_call(
        body, name="mix_ffn_fwd",
        grid=(nt,),
        in_specs=[half, half, _resident((D_MODEL, D_MODEL)), tok, row, row,
                  _resident((N_CHIPS, D_MODEL, FF_CHUNK)), _resident((D_FF, D_MODEL)), tok, row],
        out_specs=[tok, tok, tok, tok, pl.BlockSpec((tm, D_FF), lambda i: (i, 0)), tok, tok,
                   pl.BlockSpec((1, 8, LANES), lambda i: (i, 0, 0)),
                   pl.BlockSpec((1, 1, D_MODEL), lambda i: (i, 0, 0))],
        out_shape=[tok_f32,
                   tok_f32,
                   tok_mm,
                   tok_mm,
                   jax.ShapeDtypeStruct((tokens, D_FF), MM),
                   tok_mm,
                   tok_f32,
                   jax.ShapeDtypeStruct((nt, 8, LANES), F32),
                   jax.ShapeDtypeStruct((nt, 1, D_MODEL), F32)],
        compiler_params=_cparams(("parallel",), 56),
    )(*map(_from_hbm, (oa, ob, w_o, x, g2, g3, w_up, w_down, target, g4)))


def _ffn_bwd_act(df, w_down, u, w_up, x1, dy, mix, g3, g2, w_o, *, tm):
    tokens = df.shape[0]
    nt = tokens // tm

    def body(df_ref, wd_ref, u_ref, wu_ref, x1_ref, dy_ref, mix_ref, g3_ref, g2_ref, wo_ref,
             dz_ref, dx1_ref, dmix_ref, dg3_ref, dg2_ref, doa_ref, dob_ref):
        dfv = df_ref[...]
        dh2 = jnp.zeros((tm, D_MODEL), F32)
        for c in range(N_CHIPS):
            cols = slice(c * FF_CHUNK, (c + 1) * FF_CHUNK)
            da = _dot_nt(dfv, wd_ref[cols, :])
            dz = (da * (2.0 * u_ref[:, cols].astype(F32))).astype(MM)
            dz_ref[:, cols] = dz
            dh2 = dh2 + _dot_nt(dz, wu_ref[c])
        x1 = x1_ref[...]
        dxn, dg3v = _rms_bwd(x1, _rms_r(x1), g3_ref[...], dh2)
        dx1 = dy_ref[...] + dxn
        dx1_ref[...] = dx1
        dg3_ref[0] = jnp.sum(dg3v, axis=0, keepdims=True)
        mix = mix_ref[...]
        dmix, dg2v = _rms_bwd(mix, _rms_r(mix), g2_ref[...], dx1)
        dmb = dmix.astype(MM)
        dmix_ref[...] = dmb
        dg2_ref[0] = jnp.sum(dg2v, axis=0, keepdims=True)
        doa_ref[...] = _dot_nt(dmb, wo_ref[0:Q_WIDTH, :])
        dob_ref[...] = _dot_nt(dmb, wo_ref[Q_WIDTH:D_MODEL, :])

    tok = pl.BlockSpec((tm, D_MODEL), lambda i: (i, 0))
    half = pl.BlockSpec((tm, Q_WIDTH), lambda i: (i, 0))
    wide = pl.BlockSpec((tm, D_FF), lambda i: (i, 0))
    row = pl.BlockSpec((1, D_MODEL), lambda i: (0, 0))
    part = pl.BlockSpec((1, 1, D_MODEL), lambda i: (i, 0, 0))
    return pl.pallas_call(
        body, name="ffn_bwd_act",
        grid=(nt,),
        in_specs=[tok, _resident((D_FF, D_MODEL)), wide, _resident((N_CHIPS, D_MODEL, FF_CHUNK)),
                  tok, tok, tok, row, row, _resident((D_MODEL, D_MODEL))],
        out_specs=[wide, tok, tok, part, part, half, half],
        out_shape=[jax.ShapeDtypeStruct((tokens, D_FF), MM),
                   jax.ShapeDtypeStruct((tokens, D_MODEL), F32),
                   jax.ShapeDtypeStruct((tokens, D_MODEL), MM),
                   jax.ShapeDtypeStruct((nt, 1, D_MODEL), F32),
                   jax.ShapeDtypeStruct((nt, 1, D_MODEL), F32),
                   jax.ShapeDtypeStruct((tokens, Q_WIDTH), F32),
                   jax.ShapeDtypeStruct((tokens, Q_WIDTH), F32)],
        compiler_params=_cparams(("parallel",), 56),
    )(*map(_from_hbm, (df, w_down, u, w_up, x1, dy, mix, g3, g2, w_o)))


def _tn_matmul(a, b, *, name, tm, tn, tk, chunk=None, square_a=False, vmem_mb=48, jobs=()):
    tokens, m_dim = a.shape
    n_dim = b.shape[1]
    chunked = chunk is not None
    if chunked:
        assert tm == m_dim and tn % chunk == 0

    def body(a_ref, b_ref, o_ref):
        av = a_ref[...]
        if square_a:
            av = av.astype(F32)
            av = av * av
        part = _dot_tn(av.astype(MM), b_ref[...].astype(MM))
        if chunked:
            part = jnp.stack([part[:, c * chunk:(c + 1) * chunk] for c in range(tn // chunk)])

        @pl.when(pl.program_id(2) == 0)
        def _():
            o_ref[...] = part

        @pl.when(pl.program_id(2) > 0)
        def _():
            o_ref[...] += part

    if chunked:
        out_spec = pl.BlockSpec((tn // chunk, tm, chunk), lambda i, j, k: (j, 0, 0))
        out_shape = jax.ShapeDtypeStruct((n_dim // chunk, m_dim, chunk), F32)
    else:
        out_spec = pl.BlockSpec((tm, tn), lambda i, j, k: (i, j))
        out_shape = jax.ShapeDtypeStruct((m_dim, n_dim), F32)
    (out,), job_res = _call(
        body, (a, b), name=name, jobs=jobs,
        grid=(m_dim // tm, n_dim // tn, tokens // tk),
        in_specs=[pl.BlockSpec((tk, tm), lambda i, j, k: (k, i)),
                  pl.BlockSpec((tk, tn), lambda i, j, k: (k, j))],
        out_specs=[out_spec], out_shape=[_in_hbm(out_shape)],
        params=_cparams(("arbitrary", "arbitrary", "arbitrary"), vmem_mb))
    return out, job_res


def _proj_bwd(dqa, dkta, dvta, dqb, dktb, dvtb, raw, x, dx1, g1, w_in, gq, gk, ck, sk, *, seq, tm, sub, jobs=()):
    tokens = x.shape[0]
    nt = tokens // tm
    n_seq = seq // tm
    nblk = tm // BLOCK

    def body(dqa_ref, dkta_ref, dvta_ref, dqb_ref, dkb_ref, dvb_ref, raw_ref, x_ref, dx1_ref, g1_ref, w_ref,
             gq_ref, gk_ref, ck_ref, sk_ref,
             gx_ref, dproj_ref, dg1_ref, dgq_ref, dgk_ref, dp):
        parts = []
        for r in range(tm // sub):
            rows = slice(r * sub, (r + 1) * sub)
            qa = raw_ref[rows, 0:Q_WIDTH]
            dqn = _rope_t(dqa_ref[rows, :], ck_ref[rows, :], sk_ref[rows, :]) * SCALE
            rq = _head_r(qa)
            nq = qa * rq
            dnq = dqn * gq_ref[...]
            dp[rows, 0:Q_WIDTH] = rq * (dnq - nq * (_seg64_sum(dnq * nq) * (1.0 / HEAD_DIM)))

            ka = raw_ref[rows, Q_WIDTH:QK_RAW]
            dkn = _rope_t(dkta_ref[0, :, rows].T, ck_ref[rows, :], sk_ref[rows, :])
            rk = _head_r(ka)
            nk = ka * rk
            dnk = dkn * gk_ref[...]
            dp[rows, 512:640] = rk * (dnk - nk * (_seg64_sum(dnk * nk) * (1.0 / HEAD_DIM)))

            dp[rows, 640:768] = dvta_ref[0, :, rows].T
            dp[rows, 768:1280] = dqb_ref[rows, :] * SCALE
            for j in range(r * sub // BLOCK, (r + 1) * sub // BLOCK):
                dp[j * BLOCK:(j + 1) * BLOCK, 1280:1408] = dkb_ref[0, j] + dkb_ref[1, j]
                dp[j * BLOCK:(j + 1) * BLOCK, 1408:1536] = dvb_ref[0, j] + dvb_ref[1, j]

            dproj = dp[rows, :].astype(MM)
            dproj_ref[rows, :] = dproj
            dh1 = _dot_nt(dproj[:, 0:IN_CHUNK], w_ref[0])
            for j in range(1, N_CHIPS):
                dh1 = dh1 + _dot_nt(dproj[:, j * IN_CHUNK:(j + 1) * IN_CHUNK], w_ref[j])
            xv = x_ref[rows, :]
            dxn, dg1v = _rms_bwd(xv, _rms_r(xv), g1_ref[...], dh1)
            gx_ref[rows, :] = dx1_ref[rows, :] + dxn
            parts.append((jnp.sum(dqn * nq, axis=0, keepdims=True), jnp.sum(dkn * nk, axis=0, keepdims=True),
                          jnp.sum(dg1v, axis=0, keepdims=True)))
        dgq_ref[0] = functools.reduce(jnp.add, [p[0] for p in parts])
        dgk_ref[0] = functools.reduce(jnp.add, [p[1] for p in parts])
        dg1_ref[0] = functools.reduce(jnp.add, [p[2] for p in parts])

    tok = lambda w: pl.BlockSpec((tm, w), lambda i: (i, 0))
    tab = lambda w: pl.BlockSpec((tm, w), lambda i: (i % n_seq, 0))
    row = lambda w: pl.BlockSpec((1, w), lambda i: (0, 0))
    tposed = pl.BlockSpec((1, KV_WIDTH, tm), lambda i: (i // n_seq, 0, i % n_seq))
    blocks = pl.BlockSpec((N_KV, nblk, BLOCK, KV_WIDTH), lambda i: (0, i, 0, 0))
    part = lambda w: pl.BlockSpec((1, 1, w), lambda i: (i, 0, 0))
    return _call(
        body, (dqa, dkta, dvta, dqb, dktb, dvtb, raw, x, dx1, g1, w_in, gq, gk, ck, sk),
        name="proj_bwd", jobs=jobs,
        grid=(nt,),
        in_specs=[tok(Q_WIDTH), tposed, tposed, tok(Q_WIDTH), blocks, blocks, tok(QK_RAW), tok(D_MODEL),
                  tok(D_MODEL), row(D_MODEL),
                  pl.BlockSpec((N_CHIPS, D_MODEL, IN_CHUNK), lambda i: (0, 0, 0)),
                  row(Q_WIDTH), row(KV_WIDTH), tab(KV_WIDTH), tab(KV_WIDTH)],
        out_specs=[tok(D_MODEL), tok(IN_TOTAL), part(D_MODEL), part(Q_WIDTH), part(KV_WIDTH)],
        out_shape=[jax.ShapeDtypeStruct((tokens, D_MODEL), F32),
                   jax.ShapeDtypeStruct((tokens, IN_TOTAL), MM),
                   jax.ShapeDtypeStruct((nt, 1, D_MODEL), F32),
                   jax.ShapeDtypeStruct((nt, 1, Q_WIDTH), F32),
                   jax.ShapeDtypeStruct((nt, 1, KV_WIDTH), F32)],
        scratch_shapes=[pltpu.VMEM((tm, IN_TOTAL), F32)],
        params=_cparams(("arbitrary",), 56))


def _pack_small(dg1, dg2, dg3, dg4, dgq, dgk, dsink, dbias, bucket, loss):
    def body(dg1_ref, dg2_ref, dg3_ref, dg4_ref, dgq_ref, dgk_ref, dsink_ref, dbias_ref, bucket_ref, loss_ref,
             out_ref, rel_ref):
        out_ref[...] = jnp.zeros_like(out_ref)
        for r, ref in ((ROW_G1, dg1_ref), (ROW_G2, dg2_ref), (ROW_G3, dg3_ref), (ROW_G4, dg4_ref)):
            acc = ref[0]
            for t in range(1, ref.shape[0]):
                acc = acc + ref[t]
            out_ref[r:r + 1, :] = acc

        def fold(ref, heads):
            acc = ref[0]
            for t in range(1, ref.shape[0]):
                acc = acc + ref[t]
            tot = acc[:, 0:HEAD_DIM]
            for h in range(1, heads):
                tot = tot + acc[:, h * HEAD_DIM:(h + 1) * HEAD_DIM]
            return tot

        out_ref[ROW_MISC:ROW_MISC + 1, MISC_GQ:MISC_GQ + HEAD_DIM] = fold(dgq_ref, GROUP * N_KV)
        out_ref[ROW_MISC:ROW_MISC + 1, MISC_GK:MISC_GK + HEAD_DIM] = fold(dgk_ref, N_KV)
        for h in range(GROUP * N_KV):
            g = h % GROUP
            out_ref[ROW_MISC:ROW_MISC + 1, MISC_SINK + h:MISC_SINK + h + 1] = jnp.sum(
                dsink_ref[h // GROUP, 0:1, g * BLOCK:(g + 1) * BLOCK], axis=-1, keepdims=True)
        lacc = loss_ref[0, 0:1, 0:1]
        for t in range(1, loss_ref.shape[0]):
            lacc = lacc + loss_ref[t, 0:1, 0:1]
        out_ref[ROW_MISC:ROW_MISC + 1, MISC_LOSS:MISC_LOSS + 1] = lacc
        lane = lax.broadcasted_iota(jnp.int32, (N_BUCKETS, LANES), 1)
        row = lax.broadcasted_iota(jnp.int32, (N_BUCKETS, LANES), 0)

        def per_bucket(b, acc):
            for h in range(GROUP * N_KV):
                g = h % GROUP
                sel = jnp.zeros((BLOCK, BLOCK), F32)
                for piece in range(3):
                    sel = sel + jnp.where(bucket_ref[piece] == b,
                                          dbias_ref[h // GROUP, piece, :, g * BLOCK:(g + 1) * BLOCK], 0.0)
                tot = jnp.sum(jnp.sum(sel, axis=0, keepdims=True), axis=-1, keepdims=True)
                acc = jnp.where((row == b) & (lane == h), tot, acc)
            return acc

        rel_ref[...] = lax.fori_loop(0, N_BUCKETS, per_bucket, jnp.zeros((N_BUCKETS, LANES), F32))

    args = (dg1, dg2, dg3, dg4, dgq, dgk, dsink, dbias, bucket, loss)
    outs = [jax.ShapeDtypeStruct((8, D_MODEL), F32), jax.ShapeDtypeStruct((N_BUCKETS, LANES), F32)]
    return pl.pallas_call(
        body, name="pack_small", grid=(1,),
        in_specs=[_whole(a) for a in args], out_specs=[_whole(o) for o in outs], out_shape=outs,
        compiler_params=pltpu.CompilerParams(vmem_limit_bytes=32 * 1024 * 1024),
    )(*map(_from_hbm, args))


def _gather_weights(shards, whole):
    n = len(shards)
    full = [t for t in range(n) if whole[t]]

    def body(*refs):
        ins, outs = refs[:n], refs[n:2 * n]
        raw, stage = refs[2 * n:3 * n], refs[3 * n:4 * n]
        load_sem, local_sem, ici_send, ici_recv, d2d_send, d2d_recv = refs[4 * n:]
        x, y, c = _place()
        k = 2 * x + y
        sibling = (x, y, 1 - c)
        order = full + [t for t in range(n) if t not in full]
        loads = {t: pltpu.make_async_copy(ins[t], raw[t], load_sem.at[t]) for t in order}
        for t in order:
            loads[t].start()
        copies, sends = [], []
        for t in order:
            loads[t].wait()
            stage[t][...] = raw[t][...].astype(MM)
            mine = pltpu.make_async_copy(stage[t], outs[t].at[k], local_sem.at[t])
            mine.start()
            copies.append(mine)
            if t in full:
                half = ins[t].shape[0] // 2
                rows = pl.ds(c * half, half)
                for r, (fx, fy) in enumerate(_CHIP_FLIPS):
                    cp = _remote(stage[t].at[rows], outs[t].at[k, rows], ici_send.at[t, r], ici_recv.at[t, r],
                                 (_flip(x, fx), _flip(y, fy), c))
                    cp.start()
                    sends.append(cp)
        for t in full:
            half = ins[t].shape[0] // 2
            rows = pl.ds(c * half, half)
            for r, (fx, fy) in enumerate(_CHIP_FLIPS):
                kk = 2 * _flip(x, fx) + _flip(y, fy)
                landed = outs[t].at[kk, rows]
                _remote(landed, landed, ici_send.at[t, r], ici_recv.at[t, r], sibling).wait_recv()
                fwd = _remote(landed, landed, d2d_send.at[t, r], d2d_recv.at[t, r], sibling)
                fwd.start()
                sends.append(fwd)
        for t in full:
            half = ins[t].shape[0] // 2
            other = pl.ds((1 - c) * half, half)
            for r, (fx, fy) in enumerate(_CHIP_FLIPS):
                kk = 2 * _flip(x, fx) + _flip(y, fy)
                theirs = outs[t].at[kk, other]
                _remote(theirs, theirs, d2d_send.at[t, r], d2d_recv.at[t, r], sibling).wait_recv()
        for cp in sends:
            cp.wait_send()
        for cp in copies:
            cp.wait()

    return pl.pallas_call(
        body, name="gather_weights",
        in_specs=[HBM] * n, out_specs=[HBM] * n,
        out_shape=[pltpu.HBM((N_CHIPS,) + s.shape, MM) for s in shards],
        scratch_shapes=[pltpu.VMEM(s.shape, F32) for s in shards] + [pltpu.VMEM(s.shape, MM) for s in shards] + [
            pltpu.SemaphoreType.DMA((n,)), pltpu.SemaphoreType.DMA((n,)),
            pltpu.SemaphoreType.DMA((n, 3)), pltpu.SemaphoreType.DMA((n, 3)),
            pltpu.SemaphoreType.DMA((n, 3)), pltpu.SemaphoreType.DMA((n, 3))],
        compiler_params=pltpu.CompilerParams(vmem_limit_bytes=40 * 1024 * 1024),
    )(*shards)


def _add_half(grad, got, where, *, name, tr):
    nch, half, cols = got.shape
    tr = min(tr, half)
    nblk = half // tr

    def body(where_ref, g_ref, r_ref, o_ref):
        o_ref[...] = (g_ref[...] + r_ref[...]).astype(MM)

    return pl.pallas_call(
        body, name=name,
        grid_spec=pltpu.PrefetchScalarGridSpec(
            num_scalar_prefetch=1, grid=(nch, nblk),
            in_specs=[pl.BlockSpec((1, tr, cols), lambda j, i, where_ref: (j, where_ref[1] * nblk + i, 0)),
                      pl.BlockSpec((1, tr, cols), lambda j, i, where_ref: (j, i, 0))],
            out_specs=pl.BlockSpec((1, tr, cols), lambda j, i, where_ref: (j, i, 0))),
        out_shape=jax.ShapeDtypeStruct(got.shape, MM),
        compiler_params=_cparams(("parallel", "parallel"), 32),
    )(where, grad, got)


def _add_chips(own, got, where, *, name, tr):
    _, half, cols = own.shape
    tr = min(tr, half)
    nblk = half // tr

    def body(where_ref, o_ref, g_ref, out_ref):
        f = lambda v: v.astype(F32)
        out_ref[...] = ((f(o_ref[0]) + f(g_ref[0])) + f(g_ref[1])) + f(g_ref[2])

    return pl.pallas_call(
        body, name=name,
        grid_spec=pltpu.PrefetchScalarGridSpec(
            num_scalar_prefetch=1, grid=(nblk,),
            in_specs=[pl.BlockSpec((1, tr, cols), lambda i, where_ref: (where_ref[0], i, 0)),
                      pl.BlockSpec((3, tr, cols), lambda i, where_ref: (0, i, 0))],
            out_specs=pl.BlockSpec((tr, cols), lambda i, where_ref: (where_ref[1] * nblk + i, 0))),
        out_shape=pltpu.HBM((2 * half, cols), F32),
        compiler_params=_cparams(("parallel",), 32),
    )(where, own, got)


def _small_job(tiles):
    n = len(tiles)

    def copies(ins, outs, sems):
        x, y, c = _place()
        me = 4 * x + 2 * y + c
        local, send, recv = sems
        cps = []
        for t in range(n):
            cps.append(pltpu.make_async_copy(ins[t], outs[t].at[me], local.at[t]))
            for r in range(1, N_DEV):
                fx, fy, fc = (r >> 2) & 1, (r >> 1) & 1, r & 1
                cps.append(_remote(ins[t], outs[t].at[me], send.at[t, r - 1], recv.at[t, r - 1],
                                   (_flip(x, fx), _flip(y, fy), _flip(c, fc))))
        return cps

    return _Job(tiles, [jax.ShapeDtypeStruct((N_DEV,) + t.shape, F32) for t in tiles],
                [pltpu.SemaphoreType.DMA((n,)), pltpu.SemaphoreType.DMA((n, N_DEV - 1)),
                 pltpu.SemaphoreType.DMA((n, N_DEV - 1))], copies)


def _adamw_math(w, g, m, v):
    m = ADAM_B1 * m + (1.0 - ADAM_B1) * g
    v = ADAM_B2 * v + (1.0 - ADAM_B2) * (g * g)
    m_hat = m / (1.0 - ADAM_B1 ** ADAM_STEP)
    v_hat = v / (1.0 - ADAM_B2 ** ADAM_STEP)
    delta = -ADAM_LR * (m_hat / (jnp.sqrt(v_hat) + ADAM_EPS) + ADAM_WD * w)
    return delta, m, v


def _adamw(w, g, m, v, *, name, tr):
    rows, cols = w.shape
    tr = min(tr, rows)

    def body(w_ref, g_ref, m_ref, v_ref, go_ref, d_ref, nm_ref, nv_ref):
        g = g_ref[...]
        go_ref[...] = g
        d_ref[...], nm_ref[...], nv_ref[...] = _adamw_math(w_ref[...], g, m_ref[...], v_ref[...])

    spec = pl.BlockSpec((tr, cols), lambda i: (i, 0))
    return pl.pallas_call(
        body, name=name,
        grid=(rows // tr,),
        in_specs=[spec] * 4, out_specs=[spec] * 4,
        out_shape=[jax.ShapeDtypeStruct(w.shape, F32)] * 4,
        compiler_params=_cparams(("parallel",), 48),
    )(w, g, m, v)


def _small_adamw(gathered, gathered_rel, params, moments_m, moments_v):
    n = len(params)

    def body(all_ref, rel_all_ref, *refs):
        w_refs, m_refs, v_refs = refs[:n], refs[n:2 * n], refs[2 * n:3 * n]
        loss_ref = refs[3 * n]
        out_refs = refs[3 * n + 1:]
        g = all_ref[0]
        rel = rel_all_ref[0]
        for d in range(1, N_DEV):
            g = g + all_ref[d]
            rel = rel + rel_all_ref[d]
        misc = g[ROW_MISC:ROW_MISC + 1]
        loss_ref[...] = misc[:, MISC_LOSS:MISC_LOSS + 1]
        grads = (g[ROW_G1:ROW_G1 + 1], g[ROW_G2:ROW_G2 + 1], g[ROW_G3:ROW_G3 + 1], g[ROW_G4:ROW_G4 + 1],
                 misc[:, MISC_GQ:MISC_GQ + HEAD_DIM], misc[:, MISC_GK:MISC_GK + HEAD_DIM],
                 misc[:, MISC_SINK:MISC_SINK + GROUP * N_KV], rel[:, 0:GROUP * N_KV])
        for i in range(n):
            d, nm, nv = _adamw_math(w_refs[i][...], grads[i], m_refs[i][...], v_refs[i][...])
            for j, val in enumerate((grads[i], d, nm, nv)):
                out_refs[4 * i + j][...] = val

    args = (gathered, gathered_rel, *params, *moments_m, *moments_v)
    out_shape = [jax.ShapeDtypeStruct((1, 1), F32)] + [jax.ShapeDtypeStruct(p.shape, F32) for p in params
                                                       for _ in range(4)]
    outs = pl.pallas_call(
        body, name="small_adamw", grid=(1,),
        in_specs=[_whole(a) for a in args], out_specs=[_whole(o) for o in out_shape], out_shape=out_shape,
    )(*map(_from_hbm, args))
    return outs[0], [outs[1 + 4 * i:5 + 4 * i] for i in range(n)]


def kernel(x, w_in, w_o, g_pre_mix, g_post_mix, q_norm_a, k_norm_a, sink_b, rel_bias, g_pre_ffn, w_ffn_up, w_ffn_down, g_post_ffn, loss_target, m_w_in, m_w_o, m_g_pre_mix, m_g_post_mix, m_q_norm_a, m_k_norm_a, m_sink_b, m_rel_bias, m_g_pre_ffn, m_w_ffn_up, m_w_ffn_down, m_g_post_ffn, v_w_in, v_w_o, v_g_pre_mix, v_g_post_mix, v_q_norm_a, v_k_norm_a, v_sink_b, v_rel_bias, v_g_pre_ffn, v_w_ffn_up, v_w_ffn_down, v_g_post_ffn):
    batch, seq, _ = x.shape
    tokens = batch * seq
    where = jnp.stack([2 * lax.axis_index("x") + lax.axis_index("y"), lax.axis_index("c")]).astype(jnp.int32)
    x2 = x.reshape(tokens, D_MODEL)
    g1, g2, g3, g4 = g_pre_mix, g_post_mix, g_pre_ffn, g_post_ffn

    cos, sin = _rope_tables(seq)
    ck, sk = jnp.tile(cos, (1, 2)), jnp.tile(sin, (1, 2))
    gq8, gk2 = jnp.tile(q_norm_a, (1, 8)), jnp.tile(k_norm_a, (1, 2))
    bucket, band = _window_tables()
    bias = _bias_build(rel_bias.T, bucket, band)

    w_in_g, w_o_p, w_up_p, w_down_p = _gather_weights(
        (w_in[0], w_o[0], w_ffn_up[0], w_ffn_down[0]), whole=(True, False, False, False))
    (h1, raw, qa, ka, kta, va, vta, qtb, kb, ktb, vb, vtb) = _pre_proj(
        x2, g1, w_in_g, gq8, gk2, ck, sk, seq=seq, tm=min(1024, seq), sub=256)
    (oa, p_a, linv_a), (w_part,) = _attn_a_fwd(
        qa, kta, va, seq=seq, bq=min(256, seq), jobs=[_gather_job([w_o_p, w_up_p, w_down_p], forward=False)])
    kb3 = kb.reshape(tokens // BLOCK, BLOCK, KV_WIDTH)
    vb3 = vb.reshape(tokens // BLOCK, BLOCK, KV_WIDTH)
    (ob, p_b, stat_b), ((w_o_g, w_up_g, w_down_g),) = _attn_b_fwd(
        qtb, kb3, vtb, bias, sink_b, seq=seq, per_step=min(16, seq // BLOCK),
        jobs=[_gather_job(w_part, forward=True)])
    w_o2 = w_o_g.reshape(D_MODEL, D_MODEL)
    w_down2 = w_down_g.reshape(D_FF, D_MODEL)
    mix, x1, h2, o_cat, u, df, dy, loss_t, dg4 = _mix_ffn_fwd(
        oa, ob, w_o2, x2, g2, g3, w_up_g, w_down2, loss_target.reshape(tokens, D_MODEL), g4, tm=256)

    dz, dx1, dmix, dg3, dg2, doa, dob = _ffn_bwd_act(df, w_down2, u, w_up_g, x1, dy, mix, g3, g2, w_o2, tm=256)
    gw_down, _ = _tn_matmul(u, df, name="grad_w_down", tm=1024, tn=1024, tk=min(4096, tokens), square_a=True,
                            vmem_mb=56)
    gw_down = gw_down.reshape(N_CHIPS, FF_CHUNK, D_MODEL)
    gw_up, ((got_down,),) = _tn_matmul(h2, dz, name="grad_w_up", tm=1024, tn=1024, tk=min(4096, tokens), chunk=FF_CHUNK,
                                        vmem_mb=56, jobs=[_swap_job([gw_down])])
    gw_o, _ = _tn_matmul(o_cat, dmix, name="grad_w_o", tm=1024, tn=1024, tk=min(2048, tokens))
    gw_o = gw_o.reshape(N_CHIPS, O_CHUNK, D_MODEL)
    sum_down = _add_half(gw_down, got_down, where, name="add_half_w_down", tr=512)
    (dqa, dkta, dvta), ((ex_down,), (got_up,)) = _attn_a_bwd(
        qa, ka, vta, doa, oa, p_a, linv_a, seq=seq, bq=min(256, seq),
        jobs=[_exchange_job([sum_down]), _swap_job([gw_up])])
    full_down = _add_chips(sum_down, ex_down, where, name="add_chips_w_down", tr=512)
    sum_up = _add_half(gw_up, got_up, where, name="add_half_w_up", tr=512)
    (dqb, dkb, dvb, dbias, dsink), ((ex_up,), (g_down,), (got_o,)) = _attn_b_bwd(
        qtb, ktb, vb3, dob, ob, p_b, stat_b, seq=seq, per_step=min(16, seq // BLOCK),
        jobs=[_exchange_job([sum_up]), _join_job([full_down]), _swap_job([gw_o])])
    full_up = _add_chips(sum_up, ex_up, where, name="add_chips_w_up", tr=512)
    sum_o = _add_half(gw_o, got_o, where, name="add_half_w_o", tr=512)
    (grad_x, dproj, dg1, dgq, dgk), _ = _proj_bwd(
        dqa, dkta, dvta, dqb, dkb, dvb, raw, x2, dx1, g1, w_in_g, gq8, gk2, ck, sk,
        seq=seq, tm=min(512, seq), sub=128)
    packed, packed_rel = _pack_small(dg1, dg2, dg3, dg4, dgq, dgk, dsink, dbias, bucket, loss_t)
    gw_in, ((ex_o,), (g_up,), (gathered, gathered_rel)) = _tn_matmul(
        h1, dproj, name="grad_w_in", tm=1024, tn=2 * IN_CHUNK, tk=min(4096, tokens), chunk=IN_CHUNK,
        vmem_mb=56, jobs=[_exchange_job([sum_o]), _join_job([full_up]), _small_job([packed, packed_rel])])
    full_o = _add_chips(sum_o, ex_o, where, name="add_chips_w_o", tr=512)

    (g_o,), (got_in,) = _run_jobs("tail_swap", [_join_job([full_o]), _swap_job([gw_in])])
    sum_in = _add_half(gw_in, got_in, where, name="add_half_w_in", tr=512)
    ((ex_in,),) = _run_jobs("tail_exchange", [_exchange_job([sum_in])])
    full_in = _add_chips(sum_in, ex_in, where, name="add_chips_w_in", tr=512)
    ((g_in,),) = _run_jobs("tail_join", [_join_job([full_in])])

    big = [[t[None] for t in _adamw(w[0], g, m[0], v[0], name="adamw_" + nm, tr=512)] for nm, w, g, m, v in (
        ("w_in", w_in, g_in, m_w_in, v_w_in), ("w_o", w_o, g_o, m_w_o, v_w_o),
        ("w_up", w_ffn_up, g_up, m_w_ffn_up, v_w_ffn_up), ("w_down", w_ffn_down, g_down, m_w_ffn_down, v_w_ffn_down))]

    loss, small = _small_adamw(
        gathered, gathered_rel,
        (g1, g2, g3, g4, q_norm_a, k_norm_a, sink_b, rel_bias),
        (m_g_pre_mix, m_g_post_mix, m_g_pre_ffn, m_g_post_ffn, m_q_norm_a, m_k_norm_a, m_sink_b, m_rel_bias),
        (v_g_pre_mix, v_g_post_mix, v_g_pre_ffn, v_g_post_ffn, v_q_norm_a, v_k_norm_a, v_sink_b, v_rel_bias))
    s_g1, s_g2, s_g3, s_g4, s_gq, s_gk, s_sink, s_rel = small

    def leaves(i):
        return (big[0][i], big[1][i], s_g1[i], s_g2[i], s_gq[i], s_gk[i], s_sink[i], s_rel[i], s_g3[i],
                big[2][i], big[3][i], s_g4[i])

    loss = loss.reshape(())
    return (loss, grad_x.reshape(batch, seq, D_MODEL), *leaves(0), *leaves(1), *leaves(2), *leaves(3))
```

```python
import functools

import jax
import jax.numpy as jnp
import numpy as np
from jax import lax
from jax.experimental import pallas as pl
from jax.experimental.pallas import tpu as pltpu

F32 = jnp.float32
MM = jnp.bfloat16

D_MODEL = 1024
HEAD_DIM = 64
N_KV = 2
GROUP = 4
Q_WIDTH = 512
KV_WIDTH = 128
D_FF = 4096
GRID_W = 64
BLOCK = 128
N_BUCKETS = 32
MAX_DISTANCE = 128
ROPE_THETA = 10000.0
EPS = 1e-6
NEG_INF = -1e30
SCALE = HEAD_DIM ** -0.5
IN_TOTAL = 1536
N_CHIPS = 4
N_DEV = 8
IN_CHUNK = IN_TOTAL // N_CHIPS
FF_CHUNK = D_FF // N_CHIPS
O_CHUNK = D_MODEL // N_CHIPS
QK_RAW = 640

ADAM_LR = 0.001
ADAM_B1 = 0.9
ADAM_B2 = 0.999
ADAM_EPS = 1e-08
ADAM_WD = 0.01
ADAM_STEP = 10

LANES = 128
MESH = pl.DeviceIdType.MESH
HBM = pl.BlockSpec(memory_space=pl.ANY)
SMEM = pl.BlockSpec(memory_space=pltpu.SMEM)

ROW_G1, ROW_G2, ROW_G3, ROW_G4, ROW_MISC = 0, 1, 2, 3, 4
MISC_GQ, MISC_GK, MISC_SINK, MISC_LOSS = 0, 64, 128, 512


def _cparams(sem, vmem_mb):
    return pltpu.CompilerParams(dimension_semantics=sem, vmem_limit_bytes=vmem_mb * 1024 * 1024)


def _whole(a):
    return pl.BlockSpec(a.shape, lambda i: (0,) * len(a.shape))


def _from_hbm(a):
    return pltpu.with_memory_space_constraint(a, pltpu.HBM)


def _in_hbm(s):
    return pltpu.HBM(s.shape, s.dtype)


class _Job:
    def __init__(self, operands, out_shapes, sems, copies, alias=None):
        self.operands, self.out_shapes, self.sems, self.copies = list(operands), list(out_shapes), list(sems), copies
        self.alias = dict(alias or {})


def _place():
    return lax.axis_index("x"), lax.axis_index("y"), lax.axis_index("c")


_CHIP_FLIPS = ((1, 0), (0, 1), (1, 1))


def _flip(v, bit):
    return 1 - v if bit else v


def _remote(src, dst, send, recv, dev):
    return pltpu.make_async_remote_copy(src_ref=src, dst_ref=dst, send_sem=send, recv_sem=recv,
                                        device_id=dev, device_id_type=MESH)


def _swap_job(grads):
    n = len(grads)

    def copies(ins, outs, sems):
        x, y, c = _place()
        send, recv = sems
        cps = []
        for t in range(n):
            half = ins[t].shape[1] // 2
            cps.append(_remote(ins[t].at[:, pl.ds((1 - c) * half, half), :], outs[t], send.at[t], recv.at[t],
                               (x, y, 1 - c)))
        return cps

    shapes = [jax.ShapeDtypeStruct((g.shape[0], g.shape[1] // 2, g.shape[2]), F32) for g in grads]
    return _Job(grads, shapes, [pltpu.SemaphoreType.DMA((n,)), pltpu.SemaphoreType.DMA((n,))], copies)


def _exchange_job(sums):
    n = len(sums)

    def copies(ins, outs, sems):
        x, y, c = _place()
        send, recv = sems
        cps = []
        for t in range(n):
            for r, (fx, fy) in enumerate(_CHIP_FLIPS):
                kk = 2 * _flip(x, fx) + _flip(y, fy)
                cps.append(_remote(ins[t].at[kk], outs[t].at[r], send.at[t, r], recv.at[t, r],
                                   (_flip(x, fx), _flip(y, fy), c)))
        return cps

    shapes = [jax.ShapeDtypeStruct((3,) + s.shape[1:], s.dtype) for s in sums]
    return _Job(sums, shapes, [pltpu.SemaphoreType.DMA((n, 3)), pltpu.SemaphoreType.DMA((n, 3))], copies)


def _join_job(fulls):
    n = len(fulls)

    def copies(ins, outs, sems):
        x, y, c = _place()
        send, recv = sems
        cps = []
        for t in range(n):
            half = ins[t].shape[0] // 2
            rows = pl.ds(c * half, half)
            cps.append(_remote(ins[t].at[rows], outs[t].at[rows], send.at[t], recv.at[t], (x, y, 1 - c)))
        return cps

    shapes = [jax.ShapeDtypeStruct(f.shape, f.dtype) for f in fulls]
    return _Job(fulls, shapes, [pltpu.SemaphoreType.DMA((n,)), pltpu.SemaphoreType.DMA((n,))], copies,
                alias={t: t for t in range(n)})


def _gather_job(bufs, forward):
    n = len(bufs)

    def copies(ins, outs, sems):
        x, y, c = _place()
        send, recv = sems
        cps = []
        for t in range(n):
            half = ins[t].shape[1] // 2
            rows = pl.ds(c * half, half)
            for r, (fx, fy) in enumerate(_CHIP_FLIPS):
                if forward:
                    kk = 2 * _flip(x, fx) + _flip(y, fy)
                    dev = (x, y, 1 - c)
                else:
                    kk = 2 * x + y
                    dev = (_flip(x, fx), _flip(y, fy), c)
                cps.append(_remote(ins[t].at[kk, rows], outs[t].at[kk, rows], send.at[t, r], recv.at[t, r], dev))
        return cps

    shapes = [jax.ShapeDtypeStruct(b.shape, b.dtype) for b in bufs]
    return _Job(bufs, shapes, [pltpu.SemaphoreType.DMA((n, 3)), pltpu.SemaphoreType.DMA((n, 3))], copies,
                alias={t: t for t in range(n)})


def _call(body, args, *, name, grid, in_specs, out_specs, out_shape, scratch_shapes=(), params=None, jobs=()):
    n_in, n_out, n_scr = len(in_specs), len(out_specs), len(scratch_shapes)
    job_in = [len(j.operands) for j in jobs]
    job_out = [len(j.out_shapes) for j in jobs]
    job_sem = [len(j.sems) for j in jobs]

    def wrapped(*refs):
        pos = 0
        ins = refs[pos:pos + n_in]; pos += n_in
        jins = []
        for k in job_in:
            jins.append(refs[pos:pos + k]); pos += k
        outs = refs[pos:pos + n_out]; pos += n_out
        jouts = []
        for k in job_out:
            jouts.append(refs[pos:pos + k]); pos += k
        scr = refs[pos:pos + n_scr]; pos += n_scr
        jsems = []
        for k in job_sem:
            jsems.append(refs[pos:pos + k]); pos += k
        if jobs:
            ids = [pl.program_id(d) for d in range(len(grid))]
            first = functools.reduce(jnp.logical_and, [i == 0 for i in ids])
            last = functools.reduce(jnp.logical_and, [i == g - 1 for i, g in zip(ids, grid)])

            @pl.when(first)
            def _():
                for j, ji, jo, js in zip(jobs, jins, jouts, jsems):
                    for cp in j.copies(ji, jo, js):
                        cp.start()

        body(*ins, *outs, *scr)
        if jobs:
            @pl.when(last)
            def _():
                for j, ji, jo, js in zip(jobs, jins, jouts, jsems):
                    for cp in j.copies(ji, jo, js):
                        cp.wait()

    aliases = {}
    in_pos, out_pos = n_in, n_out
    for j in jobs:
        for i, o in j.alias.items():
            aliases[in_pos + i] = out_pos + o
        in_pos += len(j.operands)
        out_pos += len(j.out_shapes)
    res = pl.pallas_call(
        wrapped, name=name, grid=grid,
        in_specs=list(in_specs) + [HBM] * sum(job_in),
        out_specs=list(out_specs) + [HBM] * sum(job_out),
        out_shape=list(out_shape) + [_in_hbm(s) for j in jobs for s in j.out_shapes],
        scratch_shapes=list(scratch_shapes) + [s for j in jobs for s in j.sems],
        input_output_aliases=aliases,
        compiler_params=params,
    )(*[a if spec is SMEM else _from_hbm(a) for a, spec in zip(args, in_specs)],
      *[a for j in jobs for a in j.operands])
    own, rest = list(res[:n_out]), list(res[n_out:])
    job_res = []
    for k in job_out:
        job_res.append(rest[:k])
        rest = rest[k:]
    return own, job_res


def _run_jobs(name, jobs):
    def body():
        pass

    return _call(body, (), name=name, grid=(1,), in_specs=[], out_specs=[], out_shape=[], jobs=jobs)[1]


def _dot(a, b):
    return jnp.dot(a, b, preferred_element_type=F32)


def _dot_nt(a, b):
    return lax.dot_general(a, b, (((1,), (1,)), ((), ())), preferred_element_type=F32)


def _dot_tn(a, b):
    return lax.dot_general(a, b, (((0,), (0,)), ((), ())), preferred_element_type=F32)


def _rms_r(x):
    return lax.rsqrt(jnp.mean(x * x, axis=-1, keepdims=True) + EPS)


def _rms_bwd(x, r, g, dy):
    n = x * r
    dn = dy * g
    dx = r * (dn - n * jnp.mean(dn * n, axis=-1, keepdims=True))
    return dx, dy * n


def _seg64_sum(v):
    rows, width = v.shape
    lane = lax.broadcasted_iota(jnp.int32, (rows, LANES), 1)
    lo = lane < HEAD_DIM
    outs = []
    for c in range(width // LANES):
        ch = v[:, c * LANES:(c + 1) * LANES]
        s_lo = jnp.sum(jnp.where(lo, ch, 0.0), axis=-1, keepdims=True)
        s_hi = jnp.sum(jnp.where(lo, 0.0, ch), axis=-1, keepdims=True)
        outs.append(jnp.where(lo, s_lo, s_hi))
    return outs[0] if len(outs) == 1 else jnp.concatenate(outs, axis=-1)


def _head_r(v):
    return lax.rsqrt(_seg64_sum(v * v) * (1.0 / HEAD_DIM) + EPS)


def _swap16(ch):
    lane = lax.broadcasted_iota(jnp.int32, ch.shape, 1)
    return jnp.where((lane % 32) < 16, pltpu.roll(ch, LANES - 16, 1), pltpu.roll(ch, 16, 1))


def _by_chunk(fn, v):
    outs = [fn(v[:, c * LANES:(c + 1) * LANES]) for c in range(v.shape[1] // LANES)]
    return outs[0] if len(outs) == 1 else jnp.concatenate(outs, axis=-1)


def _rope(v, cos, sin_signed):
    return _by_chunk(lambda ch: ch * cos + _swap16(ch) * sin_signed, v)


def _rope_t(g, cos, sin_signed):
    return _by_chunk(lambda ch: ch * cos + _swap16(ch * sin_signed), g)


def _rope_tables(seq):
    nf = HEAD_DIM // 4
    freqs = ROPE_THETA ** (-jnp.arange(nf, dtype=F32) / nf)
    pos = jnp.arange(seq, dtype=jnp.int32)
    row = (pos // GRID_W).astype(F32)
    col = (pos % GRID_W).astype(F32)
    ang_r = row[:, None] * freqs[None, :]
    ang_c = col[:, None] * freqs[None, :]
    cr, sr, cc, sc = jnp.cos(ang_r), jnp.sin(ang_r), jnp.cos(ang_c), jnp.sin(ang_c)
    cos = jnp.concatenate([cr, cr, cc, cc], axis=1)
    sin = jnp.concatenate([-sr, sr, -sc, sc], axis=1)
    return cos, sin


def _t5_bucket(rel):
    nb = N_BUCKETS // 2
    ret = (rel > 0).astype(jnp.int32) * nb
    n = jnp.abs(rel)
    max_exact = nb // 2
    nf = jnp.maximum(n, 1).astype(jnp.float32)
    large = max_exact + (jnp.log(nf / max_exact) / np.float32(np.log(MAX_DISTANCE / max_exact))
                         * (nb - max_exact)).astype(jnp.int32)
    large = jnp.minimum(large, nb - 1)
    return ret + jnp.where(n < max_exact, n, large)


def _window_tables():
    a = jnp.arange(BLOCK, dtype=jnp.int32)
    c = jnp.arange(3 * BLOCK, dtype=jnp.int32)
    rel = c[None, :] - BLOCK - a[:, None]
    bucket = _t5_bucket(rel)
    band = (jnp.abs(rel) <= BLOCK).astype(jnp.int32)
    to3 = lambda t: t.reshape(BLOCK, 3, BLOCK).transpose(1, 2, 0)
    return to3(bucket), to3(band)


def _pre_proj(x, g1, w_in, gq, gk, ck, sk, *, seq, tm, sub):
    tokens = x.shape[0]
    n_seq = seq // tm
    nblk = tm // BLOCK
    batch = tokens // seq

    def body(x_ref, g1_ref, w_ref, gq_ref, gk_ref, ck_ref, sk_ref,
             h1_ref, raw_ref, qa_ref, ka_ref, kta_ref, va_ref, vta_ref,
             qtb_ref, kb_ref, ktb_ref, vb_ref, vtb_ref, proj):
        for r in range(tm // sub):
            rows = slice(r * sub, (r + 1) * sub)
            xv = x_ref[rows, :]
            h = (xv * _rms_r(xv) * g1_ref[...]).astype(MM)
            h1_ref[rows, :] = h
            for j in range(N_CHIPS):
                proj[rows, j * IN_CHUNK:(j + 1) * IN_CHUNK] = _dot(h, w_ref[j])
            qa = proj[rows, 0:Q_WIDTH]
            ka = proj[rows, Q_WIDTH:QK_RAW]
            raw_ref[rows, :] = proj[rows, 0:QK_RAW]
            qn = qa * _head_r(qa) * gq_ref[...]
            qa_ref[rows, :] = (_rope(qn, ck_ref[rows, :], sk_ref[rows, :]) * SCALE).astype(MM)
            kn = ka * _head_r(ka) * gk_ref[...]
            kr = _rope(kn, ck_ref[rows, :], sk_ref[rows, :])
            ka_ref[rows, :] = kr.astype(MM)
            kta_ref[0, :, rows] = kr.T.astype(MM)
            va = proj[rows, 640:768]
            va_ref[rows, :] = va.astype(MM)
            vta_ref[0, :, rows] = va.T.astype(MM)
            qb = proj[rows, 768:1280] * SCALE
            kb = proj[rows, 1280:1408]
            vb = proj[rows, 1408:1536]
            kb_ref[rows, :] = kb.astype(MM)
            vb_ref[rows, :] = vb.astype(MM)
            for j in range(sub // BLOCK):
                blk = slice(j * BLOCK, (j + 1) * BLOCK)
                qtb_ref[r * (sub // BLOCK) + j] = qb[blk, :].T.astype(MM)
                ktb_ref[r * (sub // BLOCK) + j] = kb[blk, :].T.astype(MM)
                vtb_ref[r * (sub // BLOCK) + j] = vb[blk, :].T.astype(MM)

    tok = lambda w: pl.BlockSpec((tm, w), lambda i: (i, 0))
    tab = lambda w: pl.BlockSpec((tm, w), lambda i: (i % n_seq, 0))
    row = lambda w: pl.BlockSpec((1, w), lambda i: (0, 0))
    tposed = pl.BlockSpec((1, LANES, tm), lambda i: (i // n_seq, 0, i % n_seq))
    blocks = pl.BlockSpec((nblk, BLOCK, LANES), lambda i: (i, 0, 0))
    qblocks = pl.BlockSpec((nblk, Q_WIDTH, BLOCK), lambda i: (i, 0, 0))
    tok_mm = lambda w: jax.ShapeDtypeStruct((tokens, w), MM)
    return pl.pallas_call(
        body, name="pre_proj",
        grid=(tokens // tm,),
        in_specs=[tok(D_MODEL), row(D_MODEL),
                  pl.BlockSpec((N_CHIPS, D_MODEL, IN_CHUNK), lambda i: (0, 0, 0)),
                  row(Q_WIDTH), row(KV_WIDTH), tab(KV_WIDTH), tab(KV_WIDTH)],
        out_specs=[tok(D_MODEL), tok(QK_RAW), tok(Q_WIDTH), tok(KV_WIDTH), tposed, tok(KV_WIDTH), tposed,
                   qblocks, tok(KV_WIDTH), blocks, tok(KV_WIDTH), blocks],
        out_shape=[
            tok_mm(D_MODEL),
            jax.ShapeDtypeStruct((tokens, QK_RAW), F32),
            tok_mm(Q_WIDTH),
            tok_mm(KV_WIDTH),
            jax.ShapeDtypeStruct((batch, KV_WIDTH, seq), MM),
            tok_mm(KV_WIDTH),
            jax.ShapeDtypeStruct((batch, KV_WIDTH, seq), MM),
            jax.ShapeDtypeStruct((tokens // BLOCK, Q_WIDTH, BLOCK), MM),
            tok_mm(KV_WIDTH),
            jax.ShapeDtypeStruct((tokens // BLOCK, KV_WIDTH, BLOCK), MM),
            tok_mm(KV_WIDTH),
            jax.ShapeDtypeStruct((tokens // BLOCK, KV_WIDTH, BLOCK), MM),
        ],
        scratch_shapes=[pltpu.VMEM((tm, IN_TOTAL), F32)],
        compiler_params=_cparams(("parallel",), 48),
    )(*map(_from_hbm, (x, g1, w_in, gq, gk, ck, sk)))


def _kv_half(v2, kv):
    return jnp.where(kv == 0, v2[:, :HEAD_DIM], v2[:, HEAD_DIM:])


def _attn_a_fwd(qa, kta, va, *, seq, bq, jobs=()):
    tokens = qa.shape[0]
    batch = tokens // seq
    nq = seq // bq

    def body(q_ref, kt_ref, v_ref, o_ref, p_ref, linv_ref):
        kv = pl.program_id(1)
        kt = kt_ref[0]
        lane = lax.broadcasted_iota(jnp.int32, (seq, KV_WIDTH), 1)
        v = jnp.where((lane < HEAD_DIM) == (kv == 0), v_ref[...], jnp.ones((), MM))
        for g in range(GROUP):
            sl = slice(g * HEAD_DIM, (g + 1) * HEAD_DIM)
            s = _dot(q_ref[:, sl], kt)
            pb = jnp.exp((s - jnp.max(s, axis=-1, keepdims=True)).astype(MM))
            p_ref[0, g] = pb
            o2 = _dot(pb, v)
            linv = 1.0 / _kv_half(o2, 1 - kv)[:, 0:1]
            o_ref[:, sl] = _kv_half(o2, kv) * linv
            linv_ref[0, :, g:g + 1] = linv

    return _call(
        body, (qa, kta, va), name="attn_a_fwd", jobs=jobs,
        grid=(batch, N_KV, nq),
        in_specs=[pl.BlockSpec((bq, GROUP * HEAD_DIM), lambda b, k, i: (b * nq + i, k)),
                  pl.BlockSpec((1, HEAD_DIM, seq), lambda b, k, i: (b, k, 0)),
                  pl.BlockSpec((seq, KV_WIDTH), lambda b, k, i: (b, 0))],
        out_specs=[pl.BlockSpec((bq, GROUP * HEAD_DIM), lambda b, k, i: (b * nq + i, k)),
                   pl.BlockSpec((1, GROUP, bq, seq), lambda b, k, i: (k, 0, b * nq + i, 0)),
                   pl.BlockSpec((1, bq, GROUP), lambda b, k, i: (k, b * nq + i, 0))],
        out_shape=[jax.ShapeDtypeStruct((tokens, Q_WIDTH), F32),
                   jax.ShapeDtypeStruct((N_KV, GROUP, tokens, seq), MM),
                   jax.ShapeDtypeStruct((N_KV, tokens, GROUP), F32)],
        params=_cparams(("arbitrary", "arbitrary", "arbitrary"), 56))


def _attn_a_bwd(qa, ka, vta, do, o, p, linv, *, seq, bq, jobs=()):
    tokens = qa.shape[0]
    batch = tokens // seq
    nq = seq // bq

    def body(q_ref, k_ref, vt_ref, do_ref, o_ref, p_ref, linv_ref, dq_ref, dkt_ref, dvt_ref):
        kv = pl.program_id(1)

        @pl.when(pl.program_id(2) == 0)
        def _():
            dkt_ref[...] = jnp.zeros_like(dkt_ref)
            dvt_ref[...] = jnp.zeros_like(dvt_ref)

        vt = vt_ref[0]
        k2 = k_ref[...]
        for g in range(GROUP):
            sl = slice(g * HEAD_DIM, (g + 1) * HEAD_DIM)
            dof = do_ref[:, sl]
            delta = jnp.sum(dof * o_ref[:, sl], axis=-1, keepdims=True)
            linv_g = linv_ref[0, :, g:g + 1]
            pb = p_ref[0, g]
            dp = _dot(dof.astype(MM), vt)
            ds = pb * ((dp - delta) * linv_g).astype(MM)
            dq_ref[:, sl] = _kv_half(_dot(ds, k2), kv)
            dkt_ref[0] += _dot_tn(q_ref[:, sl], ds)
            dvt_ref[0] += _dot_tn((dof * linv_g).astype(MM), pb)

    qspec = pl.BlockSpec((bq, GROUP * HEAD_DIM), lambda b, k, i: (b * nq + i, k))
    tspec = pl.BlockSpec((1, HEAD_DIM, seq), lambda b, k, i: (b, k, 0))
    return _call(
        body, (qa, ka, vta, do, o, p, linv), name="attn_a_bwd", jobs=jobs,
        grid=(batch, N_KV, nq),
        in_specs=[qspec, pl.BlockSpec((seq, KV_WIDTH), lambda b, k, i: (b, 0)), tspec, qspec, qspec,
                  pl.BlockSpec((1, GROUP, bq, seq), lambda b, k, i: (k, 0, b * nq + i, 0)),
                  pl.BlockSpec((1, bq, GROUP), lambda b, k, i: (k, b * nq + i, 0))],
        out_specs=[qspec, tspec, tspec],
        out_shape=[jax.ShapeDtypeStruct((tokens, Q_WIDTH), F32),
                   jax.ShapeDtypeStruct((batch, KV_WIDTH, seq), F32),
                   jax.ShapeDtypeStruct((batch, KV_WIDTH, seq), F32)],
        params=_cparams(("arbitrary", "arbitrary", "arbitrary"), 56))


def _bias_build(rel_bias_t, bucket_t, band_t):
    def body(tab_ref, bucket_ref, band_ref, bias_ref):
        for h in range(GROUP * N_KV):
            for piece in range(3):
                bk = bucket_ref[piece]
                acc = jnp.zeros((BLOCK, BLOCK), F32)
                for b in range(N_BUCKETS):
                    acc = jnp.where(bk == b, tab_ref[h, b], acc)
                g = h % GROUP
                bias_ref[h // GROUP, piece, :, g * BLOCK:(g + 1) * BLOCK] = jnp.where(band_ref[piece] != 0, acc, NEG_INF)

    out = jax.ShapeDtypeStruct((N_KV, 3, BLOCK, GROUP * BLOCK), F32)
    return pl.pallas_call(
        body, name="bias_build", grid=(1,),
        in_specs=[SMEM, _whole(bucket_t), _whole(band_t)], out_specs=_whole(out), out_shape=out,
    )(rel_bias_t, bucket_t, band_t)


def _pad_heads(t, kv):
    outs = []
    for g in range(GROUP):
        tg = t[g * HEAD_DIM:(g + 1) * HEAD_DIM, :]
        zero = jnp.zeros_like(tg)
        outs.append(jnp.concatenate([jnp.where(kv == 0, tg, zero), jnp.where(kv == 0, zero, tg)], axis=0))
    return jnp.concatenate(outs, axis=-1)


def _unpad_heads(t, kv):
    outs = [_kv_half(t[:, g * BLOCK:(g + 1) * BLOCK].T, kv) for g in range(GROUP)]
    return jnp.concatenate(outs, axis=-1)


def _sink_row(sink_ref, kv):
    lane_head = lax.broadcasted_iota(jnp.int32, (1, GROUP * BLOCK), 1) // BLOCK
    row = jnp.zeros((1, GROUP * BLOCK), F32)
    for g in range(GROUP):
        row = jnp.where(lane_head == g, sink_ref[0, kv * GROUP + g], row)
    return row


def _window_scores_t(k_ref, idx, qpad, bias_ref, n, nblk):
    pieces = []
    for piece in range(3):
        s = _dot(k_ref[idx[piece]], qpad) + bias_ref[0, piece]
        if piece == 0:
            s = jnp.where(n > 0, s, NEG_INF)
        if piece == 2:
            s = jnp.where(n < nblk - 1, s, NEG_INF)
        pieces.append(s)
    return pieces


def _attn_b_fwd(qtb, kb3, vtb, bias, sink, *, seq, per_step, jobs=()):
    nblk_all = qtb.shape[0]
    tokens = nblk_all * BLOCK
    batch = tokens // seq
    nblk = seq // BLOCK
    nstep = nblk // per_step

    def body(sink_ref, q_ref, k_ref, vt_ref, bias_ref, o_ref, p_ref, stat_ref):
        kv = pl.program_id(0)
        first = pl.program_id(2) * per_step
        sink_row = _sink_row(sink_ref, kv)
        stat_row = lax.broadcasted_iota(jnp.int32, (8, GROUP * BLOCK), 0)

        def block(i, carry):
            n = first + i
            idx = (jnp.maximum(n - 1, 0), n, jnp.minimum(n + 1, nblk - 1))
            rows = slice(i * BLOCK, (i + 1) * BLOCK)
            qpad = _pad_heads(q_ref[n], kv)
            ss = _window_scores_t(k_ref, idx, qpad, bias_ref, n, nblk)
            m = jnp.maximum(jnp.maximum(jnp.max(ss[0], axis=0, keepdims=True),
                                        jnp.max(ss[1], axis=0, keepdims=True)),
                            jnp.maximum(jnp.max(ss[2], axis=0, keepdims=True), sink_row))
            ps = [jnp.exp(s - m) for s in ss]
            e_sink = jnp.exp(sink_row - m)
            rinv = 1.0 / (jnp.sum(ps[0], axis=0, keepdims=True) + jnp.sum(ps[1], axis=0, keepdims=True)
                          + jnp.sum(ps[2], axis=0, keepdims=True) + e_sink)
            ot = jnp.zeros((KV_WIDTH, GROUP * BLOCK), F32)
            for piece in range(3):
                pb = ps[piece].astype(MM)
                p_ref[0, i, piece] = pb
                ot = ot + _dot(vt_ref[idx[piece]], pb)
            o_ref[rows, :] = _unpad_heads(ot * rinv, kv)
            stat_ref[0, i] = jnp.where(stat_row == 0, rinv, e_sink * rinv)
            return carry

        for i in range(per_step):
            block(i, 0)

    both = pl.BlockSpec((nblk, BLOCK, KV_WIDTH), lambda k, b, j: (b, 0, 0))
    return _call(
        body, (sink, qtb, kb3, vtb, bias), name="attn_b_fwd", jobs=jobs,
        grid=(N_KV, batch, nstep),
        in_specs=[SMEM, pl.BlockSpec((nblk, GROUP * HEAD_DIM, BLOCK), lambda k, b, j: (b, k, 0)), both, both,
                  pl.BlockSpec((1, 3, BLOCK, GROUP * BLOCK), lambda k, b, j: (k, 0, 0, 0))],
        out_specs=[pl.BlockSpec((per_step * BLOCK, GROUP * HEAD_DIM), lambda k, b, j: (b * nstep + j, k)),
                   pl.BlockSpec((1, per_step, 3, BLOCK, GROUP * BLOCK), lambda k, b, j: (k, b * nstep + j, 0, 0, 0)),
                   pl.BlockSpec((1, per_step, 8, GROUP * BLOCK), lambda k, b, j: (k, b * nstep + j, 0, 0))],
        out_shape=[jax.ShapeDtypeStruct((tokens, Q_WIDTH), F32),
                   jax.ShapeDtypeStruct((N_KV, nblk_all, 3, BLOCK, GROUP * BLOCK), MM),
                   jax.ShapeDtypeStruct((N_KV, nblk_all, 8, GROUP * BLOCK), F32)],
        params=_cparams(("arbitrary", "arbitrary", "arbitrary"), 48))


def _attn_b_bwd(qtb, ktb, vb3, do, o, p, stat, *, seq, per_step, jobs=()):
    nblk_all = qtb.shape[0]
    tokens = nblk_all * BLOCK
    batch = tokens // seq
    nblk = seq // BLOCK
    nstep = nblk // per_step

    def body(q_ref, kt_ref, v_ref, do_ref, o_ref, p_ref, stat_ref,
             dq_ref, dk_ref, dv_ref, dbias_ref, dsink_ref):
        kv = pl.program_id(0)
        step = pl.program_id(2)
        first = step * per_step

        @pl.when(jnp.logical_and(pl.program_id(1) == 0, step == 0))
        def _():
            dbias_ref[...] = jnp.zeros_like(dbias_ref)
            dsink_ref[...] = jnp.zeros_like(dsink_ref)

        @pl.when(step == 0)
        def _():
            dk_ref[...] = jnp.zeros_like(dk_ref)
            dv_ref[...] = jnp.zeros_like(dv_ref)

        def block(i, dsink):
            n = first + i
            idx = (jnp.maximum(n - 1, 0), n, jnp.minimum(n + 1, nblk - 1))
            rows = slice(i * BLOCK, (i + 1) * BLOCK)
            qpad = _pad_heads(q_ref[n], kv)
            dot_t = do_ref[rows, :].T
            prod = dot_t * o_ref[rows, :].T
            delta = jnp.concatenate(
                [jnp.sum(prod[g * HEAD_DIM:(g + 1) * HEAD_DIM, :], axis=0, keepdims=True) for g in range(GROUP)],
                axis=-1)
            stats = stat_ref[0, i]
            rinv, p_sink = stats[0:1, :], stats[1:2, :]
            dopad32 = _pad_heads(dot_t, kv)
            dopad = dopad32.astype(MM)
            dopad_n = (dopad32 * rinv).astype(MM)
            dqt = jnp.zeros((KV_WIDTH, GROUP * BLOCK), F32)
            for piece in range(3):
                pb = p_ref[0, i, piece]
                dst = pb.astype(F32) * ((_dot(v_ref[idx[piece]], dopad) - delta) * rinv)
                dsb = dst.astype(MM)
                dbias_ref[0, piece] += dst
                dqt = dqt + _dot(kt_ref[idx[piece]], dsb)
                dk_ref[0, idx[piece]] += _dot_nt(dsb, qpad)
                dv_ref[0, idx[piece]] += _dot_nt(pb, dopad_n)
            dq_ref[rows, :] = _unpad_heads(dqt, kv)
            return dsink - p_sink * delta

        dsink = jnp.zeros((1, GROUP * BLOCK), F32)
        for i in range(per_step):
            dsink = block(i, dsink)
        dsink_ref[0] += jnp.broadcast_to(dsink, (8, GROUP * BLOCK))

    qspec = pl.BlockSpec((per_step * BLOCK, GROUP * HEAD_DIM), lambda k, b, j: (b * nstep + j, k))
    both = pl.BlockSpec((nblk, BLOCK, KV_WIDTH), lambda k, b, j: (b, 0, 0))
    grad =pl.BlockSpec((1, nblk, BLOCK, KV_WIDTH), lambda k, b, j: (k, b, 0, 0))
    return _call(
        body, (qtb, ktb, vb3, do, o, p, stat), name="attn_b_bwd", jobs=jobs,
        grid=(N_KV, batch, nstep),
        in_specs=[pl.BlockSpec((nblk, GROUP * HEAD_DIM, BLOCK), lambda k, b, j: (b, k, 0)), both, both,
                  qspec, qspec,
                  pl.BlockSpec((1, per_step, 3, BLOCK, GROUP * BLOCK), lambda k, b, j: (k, b * nstep + j, 0, 0, 0)),
                  pl.BlockSpec((1, per_step, 8, GROUP * BLOCK), lambda k, b, j: (k, b * nstep + j, 0, 0))],
        out_specs=[qspec, grad, grad,
                   pl.BlockSpec((1, 3, BLOCK, GROUP * BLOCK), lambda k, b, j: (k, 0, 0, 0)),
                   pl.BlockSpec((1, 8, GROUP * BLOCK), lambda k, b, j: (k, 0, 0))],
        out_shape=[jax.ShapeDtypeStruct((tokens, Q_WIDTH), F32),
                   jax.ShapeDtypeStruct((N_KV, nblk_all, BLOCK, KV_WIDTH), F32),
                   jax.ShapeDtypeStruct((N_KV, nblk_all, BLOCK, KV_WIDTH), F32),
                   jax.ShapeDtypeStruct((N_KV, 3, BLOCK, GROUP * BLOCK), F32),
                   jax.ShapeDtypeStruct((N_KV, 8, GROUP * BLOCK), F32)],
        params=_cparams(("arbitrary", "arbitrary", "arbitrary"), 56))


def _resident(shape):
    return pl.BlockSpec(shape, lambda i: (0,) * len(shape), pipeline_mode=pl.Buffered(1))


def _mix_ffn_fwd(oa, ob, w_o, x, g2, g3, w_up, w_down, target, g4, *, tm):
    tokens = x.shape[0]
    nt = tokens // tm

    def body(oa_ref, ob_ref, wo_ref, x_ref, g2_ref, g3_ref, wu_ref, wd_ref, t_ref, g4_ref,
             mix_ref, x1_ref, h2_ref, o_ref, u_ref, df_ref, dy_ref, loss_ref, dg4_ref):
        o = jnp.concatenate([oa_ref[...].astype(MM), ob_ref[...].astype(MM)], axis=-1)
        o_ref[...] = o
        mix = _dot(o, wo_ref[...])
        mix_ref[...] = mix
        x1 = x_ref[...] + mix * _rms_r(mix) * g2_ref[...]
        x1_ref[...] = x1
        h2v = (x1 * _rms_r(x1) * g3_ref[...]).astype(MM)
        h2_ref[...] = h2v
        f = jnp.zeros((tm, D_MODEL), F32)
        for c in range(N_CHIPS):
            u = jnp.maximum(_dot(h2v, wu_ref[c]), 0.0)
            u_ref[:, c * FF_CHUNK:(c + 1) * FF_CHUNK] = u.astype(MM)
            f = f + _dot((u * u).astype(MM), wd_ref[c * FF_CHUNK:(c + 1) * FF_CHUNK, :])
        r = _rms_r(f)
        g4v = g4_ref[...]
        err = x1 + f * r * g4v - t_ref[...]
        sq = jnp.sum(err * err, axis=-1, keepdims=True)
        loss_ref[0] = jnp.broadcast_to(jnp.sum(sq, axis=0, keepdims=True) * (0.5 / D_MODEL), (8, LANES))
        dy = err * (1.0 / D_MODEL)
        dy_ref[...] = dy
        dfv, dgv = _rms_bwd(f, r, g4v, dy)
        df_ref[...] = dfv.astype(MM)
        dg4_ref[0] = jnp.sum(dgv, axis=0, keepdims=True)

    tok = pl.BlockSpec((tm, D_MODEL), lambda i: (i, 0))
    half = pl.BlockSpec((tm, Q_WIDTH), lambda i: (i, 0))
    row = pl.BlockSpec((1, D_MODEL), lambda i: (0, 0))
    tok_f32 = jax.ShapeDtypeStruct((tokens, D_MODEL), F32)
    tok_mm = jax.ShapeDtypeStruct((tokens, D_MODEL), MM)
    return pl.pallas_call(
        body, name="mix_ffn_fwd",
        grid=(nt,),
        in_specs=[half, half, _resident((D_MODEL, D_MODEL)), tok, row, row,
                  _resident((N_CHIPS, D_MODEL, FF_CHUNK)), _resident((D_FF, D_MODEL)), tok, row],
        out_specs=[tok, tok, tok, tok, pl.BlockSpec((tm, D_FF), lambda i: (i, 0)), tok, tok,
                   pl.BlockSpec((1, 8, LANES), lambda i: (i, 0, 0)),
                   pl.BlockSpec((1, 1, D_MODEL), lambda i: (i, 0, 0))],
        out_shape=[tok_f32,
                   tok_f32,
                   tok_mm,
                   tok_mm,
                   jax.ShapeDtypeStruct((tokens, D_FF), MM),
                   tok_mm,
                   tok_f32,
                   jax.ShapeDtypeStruct((nt, 8, LANES), F32),
                   jax.ShapeDtypeStruct((nt, 1, D_MODEL), F32)],
        compiler_params=_cparams(("parallel",), 56),
    )(*map(_from_hbm, (oa, ob, w_o, x, g2, g3, w_up, w_down, target, g4)))


def _ffn_bwd_act(df, w_down, u, w_up, x1, dy, mix, g3, g2, w_o, *, tm):
    tokens = df.shape[0]
    nt = tokens // tm

    def body(df_ref, wd_ref, u_ref, wu_ref, x1_ref, dy_ref, mix_ref, g3_ref, g2_ref, wo_ref,
             dz_ref, dx1_ref, dmix_ref, dg3_ref, dg2_ref, doa_ref, dob_ref):
        dfv = df_ref[...]
        dh2 = jnp.zeros((tm, D_MODEL), F32)
        for c in range(N_CHIPS):
            cols = slice(c * FF_CHUNK, (c + 1) * FF_CHUNK)
            da = _dot_nt(dfv, wd_ref[cols, :])
            dz = (da * (2.0 * u_ref[:, cols].astype(F32))).astype(MM)
            dz_ref[:, cols] = dz
            dh2 = dh2 + _dot_nt(dz, wu_ref[c])
        x1 = x1_ref[...]
        dxn, dg3v = _rms_bwd(x1, _rms_r(x1), g3_ref[...], dh2)
        dx1 = dy_ref[...] + dxn
        dx1_ref[...] = dx1
        dg3_ref[0] = jnp.sum(dg3v, axis=0, keepdims=True)
        mix = mix_ref[...]
        dmix, dg2v = _rms_bwd(mix, _rms_r(mix), g2_ref[...], dx1)
        dmb = dmix.astype(MM)
        dmix_ref[...] = dmb
        dg2_ref[0] = jnp.sum(dg2v, axis=0, keepdims=True)
        doa_ref[...] = _dot_nt(dmb, wo_ref[0:Q_WIDTH, :])
        dob_ref[...] = _dot_nt(dmb, wo_ref[Q_WIDTH:D_MODEL, :])

    tok = pl.BlockSpec((tm, D_MODEL), lambda i: (i, 0))
    half = pl.BlockSpec((tm, Q_WIDTH), lambda i: (i, 0))
    wide = pl.BlockSpec((tm, D_FF), lambda i: (i, 0))
    row = pl.BlockSpec((1, D_MODEL), lambda i: (0, 0))
    part = pl.BlockSpec((1, 1, D_MODEL), lambda i: (i, 0, 0))
    return pl.pallas_call(
        body, name="ffn_bwd_act",
        grid=(nt,),
        in_specs=[tok, _resident((D_FF, D_MODEL)), wide, _resident((N_CHIPS, D_MODEL, FF_CHUNK)),
                  tok, tok, tok, row, row, _resident((D_MODEL, D_MODEL))],
        out_specs=[wide, tok, tok, part, part, half, half],
        out_shape=[jax.ShapeDtypeStruct((tokens, D_FF), MM),
                   jax.ShapeDtypeStruct((tokens, D_MODEL), F32),
                   jax.ShapeDtypeStruct((tokens, D_MODEL), MM),
                   jax.ShapeDtypeStruct((nt, 1, D_MODEL), F32),
                   jax.ShapeDtypeStruct((nt, 1, D_MODEL), F32),
                   jax.ShapeDtypeStruct((tokens, Q_WIDTH), F32),
                   jax.ShapeDtypeStruct((tokens, Q_WIDTH), F32)],
        compiler_params=_cparams(("parallel",), 56),
    )(*map(_from_hbm, (df, w_down, u, w_up, x1, dy, mix, g3, g2, w_o)))


def _tn_matmul(a, b, *, name, tm, tn, tk, chunk=None, square_a=False, vmem_mb=48, jobs=()):
    tokens, m_dim = a.shape
    n_dim = b.shape[1]
    chunked = chunk is not None
    if chunked:
        assert tm == m_dim and tn % chunk == 0

    def body(a_ref, b_ref, o_ref):
        av = a_ref[...]
        if square_a:
            av = av.astype(F32)
            av = av * av
        part = _dot_tn(av.astype(MM), b_ref[...].astype(MM))
        if chunked:
            part = jnp.stack([part[:, c * chunk:(c + 1) * chunk] for c in range(tn // chunk)])

        @pl.when(pl.program_id(2) == 0)
        def _():
            o_ref[...] = part

        @pl.when(pl.program_id(2) > 0)
        def _():
            o_ref[...] += part

    if chunked:
        out_spec = pl.BlockSpec((tn // chunk, tm, chunk), lambda i, j, k: (j, 0, 0))
        out_shape = jax.ShapeDtypeStruct((n_dim // chunk, m_dim, chunk), F32)
    else:
        out_spec = pl.BlockSpec((tm, tn), lambda i, j, k: (i, j))
        out_shape = jax.ShapeDtypeStruct((m_dim, n_dim), F32)
    (out,), job_res = _call(
        body, (a, b), name=name, jobs=jobs,
        grid=(m_dim // tm, n_dim // tn, tokens // tk),
        in_specs=[pl.BlockSpec((tk, tm), lambda i, j, k: (k, i)),
                  pl.BlockSpec((tk, tn), lambda i, j, k: (k, j))],
        out_specs=[out_spec], out_shape=[_in_hbm(out_shape)],
        params=_cparams(("arbitrary", "arbitrary", "arbitrary"), vmem_mb))
    return out, job_res


def _proj_bwd(dqa, dkta, dvta, dqb, dktb, dvtb, raw, x, dx1, g1, w_in, gq, gk, ck, sk, *, seq, tm, sub, jobs=()):
    tokens = x.shape[0]
    nt = tokens // tm
    n_seq = seq // tm
    nblk = tm // BLOCK

    def body(dqa_ref, dkta_ref, dvta_ref, dqb_ref, dkb_ref, dvb_ref, raw_ref, x_ref, dx1_ref, g1_ref, w_ref,
             gq_ref, gk_ref, ck_ref, sk_ref,
             gx_ref, dproj_ref, dg1_ref, dgq_ref, dgk_ref, dp):
        parts = []
        for r in range(tm // sub):
            rows = slice(r * sub, (r + 1) * sub)
            qa = raw_ref[rows, 0:Q_WIDTH]
            dqn = _rope_t(dqa_ref[rows, :], ck_ref[rows, :], sk_ref[rows, :]) * SCALE
            rq = _head_r(qa)
            nq = qa * rq
            dnq = dqn * gq_ref[...]
            dp[rows, 0:Q_WIDTH] = rq * (dnq - nq * (_seg64_sum(dnq * nq) * (1.0 / HEAD_DIM)))

            ka = raw_ref[rows, Q_WIDTH:QK_RAW]
            dkn = _rope_t(dkta_ref[0, :, rows].T, ck_ref[rows, :], sk_ref[rows, :])
            rk = _head_r(ka)
            nk = ka * rk
            dnk = dkn * gk_ref[...]
            dp[rows, 512:640] = rk * (dnk - nk * (_seg64_sum(dnk * nk) * (1.0 / HEAD_DIM)))

            dp[rows, 640:768] = dvta_ref[0, :, rows].T
            dp[rows, 768:1280] = dqb_ref[rows, :] * SCALE
            for j in range(r * sub // BLOCK, (r + 1) * sub // BLOCK):
                dp[j * BLOCK:(j + 1) * BLOCK, 1280:1408] = dkb_ref[0, j] + dkb_ref[1, j]
                dp[j * BLOCK:(j + 1) * BLOCK, 1408:1536] = dvb_ref[0, j] + dvb_ref[1, j]

            dproj = dp[rows, :].astype(MM)
            dproj_ref[rows, :] = dproj
            dh1 = _dot_nt(dproj[:, 0:IN_CHUNK], w_ref[0])
            for j in range(1, N_CHIPS):
                dh1 = dh1 + _dot_nt(dproj[:, j * IN_CHUNK:(j + 1) * IN_CHUNK], w_ref[j])
            xv = x_ref[rows, :]
            dxn, dg1v = _rms_bwd(xv, _rms_r(xv), g1_ref[...], dh1)
            gx_ref[rows, :] = dx1_ref[rows, :] + dxn
            parts.append((jnp.sum(dqn * nq, axis=0, keepdims=True), jnp.sum(dkn * nk, axis=0, keepdims=True),
                          jnp.sum(dg1v, axis=0, keepdims=True)))
        dgq_ref[0] = functools.reduce(jnp.add, [p[0] for p in parts])
        dgk_ref[0] = functools.reduce(jnp.add, [p[1] for p in parts])
        dg1_ref[0] = functools.reduce(jnp.add, [p[2] for p in parts])

    tok = lambda w: pl.BlockSpec((tm, w), lambda i: (i, 0))
    tab = lambda w: pl.BlockSpec((tm, w), lambda i: (i % n_seq, 0))
    row = lambda w: pl.BlockSpec((1, w), lambda i: (0, 0))
    tposed = pl.BlockSpec((1, KV_WIDTH, tm), lambda i: (i // n_seq, 0, i % n_seq))
    blocks = pl.BlockSpec((N_KV, nblk, BLOCK, KV_WIDTH), lambda i: (0, i, 0, 0))
    part = lambda w: pl.BlockSpec((1, 1, w), lambda i: (i, 0, 0))
    return _call(
        body, (dqa, dkta, dvta, dqb, dktb, dvtb, raw, x, dx1, g1, w_in, gq, gk, ck, sk),
        name="proj_bwd", jobs=jobs,
        grid=(nt,),
        in_specs=[tok(Q_WIDTH), tposed, tposed, tok(Q_WIDTH), blocks, blocks, tok(QK_RAW), tok(D_MODEL),
                  tok(D_MODEL), row(D_MODEL),
                  pl.BlockSpec((N_CHIPS, D_MODEL, IN_CHUNK), lambda i: (0, 0, 0)),
                  row(Q_WIDTH), row(KV_WIDTH), tab(KV_WIDTH), tab(KV_WIDTH)],
        out_specs=[tok(D_MODEL), tok(IN_TOTAL), part(D_MODEL), part(Q_WIDTH), part(KV_WIDTH)],
        out_shape=[jax.ShapeDtypeStruct((tokens, D_MODEL), F32),
                   jax.ShapeDtypeStruct((tokens, IN_TOTAL), MM),
                   jax.ShapeDtypeStruct((nt, 1, D_MODEL), F32),
                   jax.ShapeDtypeStruct((nt, 1, Q_WIDTH), F32),
                   jax.ShapeDtypeStruct((nt, 1, KV_WIDTH), F32)],
        scratch_shapes=[pltpu.VMEM((tm, IN_TOTAL), F32)],
        params=_cparams(("arbitrary",), 56))


def _pack_small(dg1, dg2, dg3, dg4, dgq, dgk, dsink, dbias, bucket, loss):
    def body(dg1_ref, dg2_ref, dg3_ref, dg4_ref, dgq_ref, dgk_ref, dsink_ref, dbias_ref, bucket_ref, loss_ref,
             out_ref, rel_ref):
        out_ref[...] = jnp.zeros_like(out_ref)
        for r, ref in ((ROW_G1, dg1_ref), (ROW_G2, dg2_ref), (ROW_G3, dg3_ref), (ROW_G4, dg4_ref)):
            acc = ref[0]
            for t in range(1, ref.shape[0]):
                acc = acc + ref[t]
            out_ref[r:r + 1, :] = acc

        def fold(ref, heads):
            acc = ref[0]
            for t in range(1, ref.shape[0]):
                acc = acc + ref[t]
            tot = acc[:, 0:HEAD_DIM]
            for h in range(1, heads):
                tot = tot + acc[:, h * HEAD_DIM:(h + 1) * HEAD_DIM]
            return tot

        out_ref[ROW_MISC:ROW_MISC + 1, MISC_GQ:MISC_GQ + HEAD_DIM] = fold(dgq_ref, GROUP * N_KV)
        out_ref[ROW_MISC:ROW_MISC + 1, MISC_GK:MISC_GK + HEAD_DIM] = fold(dgk_ref, N_KV)
        for h in range(GROUP * N_KV):
            g = h % GROUP
            out_ref[ROW_MISC:ROW_MISC + 1, MISC_SINK + h:MISC_SINK + h + 1] = jnp.sum(
                dsink_ref[h // GROUP, 0:1, g * BLOCK:(g + 1) * BLOCK], axis=-1, keepdims=True)
        lacc = loss_ref[0, 0:1, 0:1]
        for t in range(1, loss_ref.shape[0]):
            lacc = lacc + loss_ref[t, 0:1, 0:1]
        out_ref[ROW_MISC:ROW_MISC + 1, MISC_LOSS:MISC_LOSS + 1] = lacc
        lane = lax.broadcasted_iota(jnp.int32, (N_BUCKETS, LANES), 1)
        row = lax.broadcasted_iota(jnp.int32, (N_BUCKETS, LANES), 0)

        def per_bucket(b, acc):
            for h in range(GROUP * N_KV):
                g = h % GROUP
                sel = jnp.zeros((BLOCK, BLOCK), F32)
                for piece in range(3):
                    sel = sel + jnp.where(bucket_ref[piece] == b,
                                          dbias_ref[h // GROUP, piece, :, g * BLOCK:(g + 1) * BLOCK], 0.0)
                tot = jnp.sum(jnp.sum(sel, axis=0, keepdims=True), axis=-1, keepdims=True)
                acc = jnp.where((row == b) & (lane == h), tot, acc)
            return acc

        rel_ref[...] = lax.fori_loop(0, N_BUCKETS, per_bucket, jnp.zeros((N_BUCKETS, LANES), F32))

    args = (dg1, dg2, dg3, dg4, dgq, dgk, dsink, dbias, bucket, loss)
    outs = [jax.ShapeDtypeStruct((8, D_MODEL), F32), jax.ShapeDtypeStruct((N_BUCKETS, LANES), F32)]
    return pl.pallas_call(
        body, name="pack_small", grid=(1,),
        in_specs=[_whole(a) for a in args], out_specs=[_whole(o) for o in outs], out_shape=outs,
        compiler_params=pltpu.CompilerParams(vmem_limit_bytes=32 * 1024 * 1024),
    )(*map(_from_hbm, args))


def _gather_weights(shards, whole):
    n = len(shards)
    full = [t for t in range(n) if whole[t]]

    def body(*refs):
        ins, outs = refs[:n], refs[n:2 * n]
        raw, stage = refs[2 * n:3 * n], refs[3 * n:4 * n]
        load_sem, local_sem, ici_send, ici_recv, d2d_send, d2d_recv = refs[4 * n:]
        x, y, c = _place()
        k = 2 * x + y
        sibling = (x, y, 1 - c)
        order = full + [t for t in range(n) if t not in full]
        loads = {t: pltpu.make_async_copy(ins[t], raw[t], load_sem.at[t]) for t in order}
        for t in order:
            loads[t].start()
        copies, sends = [], []
        for t in order:
            loads[t].wait()
            stage[t][...] = raw[t][...].astype(MM)
            mine = pltpu.make_async_copy(stage[t], outs[t].at[k], local_sem.at[t])
            mine.start()
            copies.append(mine)
            if t in full:
                half = ins[t].shape[0] // 2
                rows = pl.ds(c * half, half)
                for r, (fx, fy) in enumerate(_CHIP_FLIPS):
                    cp = _remote(stage[t].at[rows], outs[t].at[k, rows], ici_send.at[t, r], ici_recv.at[t, r],
                                 (_flip(x, fx), _flip(y, fy), c))
                    cp.start()
                    sends.append(cp)
        for t in full:
            half = ins[t].shape[0] // 2
            rows = pl.ds(c * half, half)
            for r, (fx, fy) in enumerate(_CHIP_FLIPS):
                kk = 2 * _flip(x, fx) + _flip(y, fy)
                landed = outs[t].at[kk, rows]
                _remote(landed, landed, ici_send.at[t, r], ici_recv.at[t, r], sibling).wait_recv()
                fwd = _remote(landed, landed, d2d_send.at[t, r], d2d_recv.at[t, r], sibling)
                fwd.start()
                sends.append(fwd)
        for t in full:
            half = ins[t].shape[0] // 2
            other = pl.ds((1 - c) * half, half)
            for r, (fx, fy) in enumerate(_CHIP_FLIPS):
                kk = 2 * _flip(x, fx) + _flip(y, fy)
                theirs = outs[t].at[kk, other]
                _remote(theirs, theirs, d2d_send.at[t, r], d2d_recv.at[t, r], sibling).wait_recv()
        for cp in sends:
            cp.wait_send()
        for cp in copies:
            cp.wait()

    return pl.pallas_call(
        body, name="gather_weights",
        in_specs=[HBM] * n, out_specs=[HBM] * n,
        out_shape=[pltpu.HBM((N_CHIPS,) + s.shape, MM) for s in shards],
        scratch_shapes=[pltpu.VMEM(s.shape, F32) for s in shards] + [pltpu.VMEM(s.shape, MM) for s in shards] + [
            pltpu.SemaphoreType.DMA((n,)), pltpu.SemaphoreType.DMA((n,)),
            pltpu.SemaphoreType.DMA((n, 3)), pltpu.SemaphoreType.DMA((n, 3)),
            pltpu.SemaphoreType.DMA((n, 3)), pltpu.SemaphoreType.DMA((n, 3))],
        compiler_params=pltpu.CompilerParams(vmem_limit_bytes=40 * 1024 * 1024),
    )(*shards)


def _add_half(grad, got, where, *, name, tr):
    nch, half, cols = got.shape
    tr = min(tr, half)
    nblk = half // tr

    def body(where_ref, g_ref, r_ref, o_ref):
        o_ref[...] = (g_ref[...] + r_ref[...]).astype(MM)

    return pl.pallas_call(
        body, name=name,
        grid_spec=pltpu.PrefetchScalarGridSpec(
            num_scalar_prefetch=1, grid=(nch, nblk),
            in_specs=[pl.BlockSpec((1, tr, cols), lambda j, i, where_ref: (j, where_ref[1] * nblk + i, 0)),
                      pl.BlockSpec((1, tr, cols), lambda j, i, where_ref: (j, i, 0))],
            out_specs=pl.BlockSpec((1, tr, cols), lambda j, i, where_ref: (j, i, 0))),
        out_shape=jax.ShapeDtypeStruct(got.shape, MM),
        compiler_params=_cparams(("parallel", "parallel"), 32),
    )(where, grad, got)


def _add_chips(own, got, where, *, name, tr):
    _, half, cols = own.shape
    tr = min(tr, half)
    nblk = half // tr

    def body(where_ref, o_ref, g_ref, out_ref):
        f = lambda v: v.astype(F32)
        out_ref[...] = ((f(o_ref[0]) + f(g_ref[0])) + f(g_ref[1])) + f(g_ref[2])

    return pl.pallas_call(
        body, name=name,
        grid_spec=pltpu.PrefetchScalarGridSpec(
            num_scalar_prefetch=1, grid=(nblk,),
            in_specs=[pl.BlockSpec((1, tr, cols), lambda i, where_ref: (where_ref[0], i, 0)),
                      pl.BlockSpec((3, tr, cols), lambda i, where_ref: (0, i, 0))],
            out_specs=pl.BlockSpec((tr, cols), lambda i, where_ref: (where_ref[1] * nblk + i, 0))),
        out_shape=pltpu.HBM((2 * half, cols), F32),
        compiler_params=_cparams(("parallel",), 32),
    )(where, own, got)


def _small_job(tiles):
    n = len(tiles)

    def copies(ins, outs, sems):
        x, y, c = _place()
        me = 4 * x + 2 * y + c
        local, send, recv = sems
        cps = []
        for t in range(n):
            cps.append(pltpu.make_async_copy(ins[t], outs[t].at[me], local.at[t]))
            for r in range(1, N_DEV):
                fx, fy, fc = (r >> 2) & 1, (r >> 1) & 1, r & 1
                cps.append(_remote(ins[t], outs[t].at[me], send.at[t, r - 1], recv.at[t, r - 1],
                                   (_flip(x, fx), _flip(y, fy), _flip(c, fc))))
        return cps

    return _Job(tiles, [jax.ShapeDtypeStruct((N_DEV,) + t.shape, F32) for t in tiles],
                [pltpu.SemaphoreType.DMA((n,)), pltpu.SemaphoreType.DMA((n, N_DEV - 1)),
                 pltpu.SemaphoreType.DMA((n, N_DEV - 1))], copies)


def _adamw_math(w, g, m, v):
    m = ADAM_B1 * m + (1.0 - ADAM_B1) * g
    v = ADAM_B2 * v + (1.0 - ADAM_B2) * (g * g)
    m_hat = m / (1.0 - ADAM_B1 ** ADAM_STEP)
    v_hat = v / (1.0 - ADAM_B2 ** ADAM_STEP)
    delta = -ADAM_LR * (m_hat / (jnp.sqrt(v_hat) + ADAM_EPS) + ADAM_WD * w)
    return delta, m, v


def _adamw(w, g, m, v, *, name, tr):
    rows, cols = w.shape
    tr = min(tr, rows)

    def body(w_ref, g_ref, m_ref, v_ref, go_ref, d_ref, nm_ref, nv_ref):
        g = g_ref[...]
        go_ref[...] = g
        d_ref[...], nm_ref[...], nv_ref[...] = _adamw_math(w_ref[...], g, m_ref[...], v_ref[...])

    spec = pl.BlockSpec((tr, cols), lambda i: (i, 0))
    return pl.pallas_call(
        body, name=name,
        grid=(rows // tr,),
        in_specs=[spec] * 4, out_specs=[spec] * 4,
        out_shape=[jax.ShapeDtypeStruct(w.shape, F32)] * 4,
        compiler_params=_cparams(("parallel",), 48),
    )(w, g, m, v)


def _small_adamw(gathered, gathered_rel, params, moments_m, moments_v):
    n = len(params)

    def body(all_ref, rel_all_ref, *refs):
        w_refs, m_refs, v_refs = refs[:n], refs[n:2 * n], refs[2 * n:3 * n]
        loss_ref = refs[3 * n]
        out_refs = refs[3 * n + 1:]
        g = all_ref[0]
        rel = rel_all_ref[0]
        for d in range(1, N_DEV):
            g = g + all_ref[d]
            rel = rel + rel_all_ref[d]
        misc = g[ROW_MISC:ROW_MISC + 1]
        loss_ref[...] = misc[:, MISC_LOSS:MISC_LOSS + 1]
        grads = (g[ROW_G1:ROW_G1 + 1], g[ROW_G2:ROW_G2 + 1], g[ROW_G3:ROW_G3 + 1], g[ROW_G4:ROW_G4 + 1],
                 misc[:, MISC_GQ:MISC_GQ + HEAD_DIM], misc[:, MISC_GK:MISC_GK + HEAD_DIM],
                 misc[:, MISC_SINK:MISC_SINK + GROUP * N_KV], rel[:, 0:GROUP * N_KV])
        for i in range(n):
            d, nm, nv = _adamw_math(w_refs[i][...], grads[i], m_refs[i][...], v_refs[i][...])
            for j, val in enumerate((grads[i], d, nm, nv)):
                out_refs[4 * i + j][...] = val

    args = (gathered, gathered_rel, *params, *moments_m, *moments_v)
    out_shape = [jax.ShapeDtypeStruct((1, 1), F32)] + [jax.ShapeDtypeStruct(p.shape, F32) for p in params
                                                       for _ in range(4)]
    outs = pl.pallas_call(
        body, name="small_adamw", grid=(1,),
        in_specs=[_whole(a) for a in args], out_specs=[_whole(o) for o in out_shape], out_shape=out_shape,
    )(*map(_from_hbm, args))
    return outs[0], [outs[1 + 4 * i:5 + 4 * i] for i in range(n)]


def kernel(x, w_in, w_o, g_pre_mix, g_post_mix, q_norm_a, k_norm_a, sink_b, rel_bias, g_pre_ffn, w_ffn_up, w_ffn_down, g_post_ffn, loss_target, m_w_in, m_w_o, m_g_pre_mix, m_g_post_mix, m_q_norm_a, m_k_norm_a, m_sink_b, m_rel_bias, m_g_pre_ffn, m_w_ffn_up, m_w_ffn_down, m_g_post_ffn, v_w_in, v_w_o, v_g_pre_mix, v_g_post_mix, v_q_norm_a, v_k_norm_a, v_sink_b, v_rel_bias, v_g_pre_ffn, v_w_ffn_up, v_w_ffn_down, v_g_post_ffn):
    batch, seq, _ = x.shape
    tokens = batch * seq
    where = jnp.stack([2 * lax.axis_index("x") + lax.axis_index("y"), lax.axis_index("c")]).astype(jnp.int32)
    x2 = x.reshape(tokens, D_MODEL)
    g1, g2, g3, g4 = g_pre_mix, g_post_mix, g_pre_ffn, g_post_ffn

    cos, sin = _rope_tables(seq)
    ck, sk = jnp.tile(cos, (1, 2)), jnp.tile(sin, (1, 2))
    gq8, gk2 = jnp.tile(q_norm_a, (1, 8)), jnp.tile(k_norm_a, (1, 2))
    bucket, band = _window_tables()
    bias = _bias_build(rel_bias.T, bucket, band)

    w_in_g, w_o_p, w_up_p, w_down_p = _gather_weights(
        (w_in[0], w_o[0], w_ffn_up[0], w_ffn_down[0]), whole=(True, False, False, False))
    (h1, raw, qa, ka, kta, va, vta, qtb, kb, ktb, vb, vtb) = _pre_proj(
        x2, g1, w_in_g, gq8, gk2, ck, sk, seq=seq, tm=min(1024, seq), sub=256)
    (oa, p_a, linv_a), (w_part,) = _attn_a_fwd(
        qa, kta, va, seq=seq, bq=min(256, seq), jobs=[_gather_job([w_o_p, w_up_p, w_down_p], forward=False)])
    kb3 = kb.reshape(tokens // BLOCK, BLOCK, KV_WIDTH)
    vb3 = vb.reshape(tokens // BLOCK, BLOCK, KV_WIDTH)
    (ob, p_b, stat_b), ((w_o_g, w_up_g, w_down_g),) = _attn_b_fwd(
        qtb, kb3, vtb, bias, sink_b, seq=seq, per_step=min(16, seq // BLOCK),
        jobs=[_gather_job(w_part, forward=True)])
    w_o2 = w_o_g.reshape(D_MODEL, D_MODEL)
    w_down2 = w_down_g.reshape(D_FF, D_MODEL)
    mix, x1, h2, o_cat, u, df, dy, loss_t, dg4 = _mix_ffn_fwd(
        oa, ob, w_o2, x2, g2, g3, w_up_g, w_down2, loss_target.reshape(tokens, D_MODEL), g4, tm=256)

    dz, dx1, dmix, dg3, dg2, doa, dob = _ffn_bwd_act(df, w_down2, u, w_up_g, x1, dy, mix, g3, g2, w_o2, tm=256)
    gw_down, _ = _tn_matmul(u, df, name="grad_w_down", tm=1024, tn=1024, tk=min(4096, tokens), square_a=True,
                            vmem_mb=56)
    gw_down = gw_down.reshape(N_CHIPS, FF_CHUNK, D_MODEL)
    gw_up, ((got_down,),) = _tn_matmul(h2, dz, name="grad_w_up", tm=1024, tn=1024, tk=min(4096, tokens), chunk=FF_CHUNK,
                                        vmem_mb=56, jobs=[_swap_job([gw_down])])
    gw_o, _ = _tn_matmul(o_cat, dmix, name="grad_w_o", tm=1024, tn=1024, tk=min(2048, tokens))
    gw_o = gw_o.reshape(N_CHIPS, O_CHUNK, D_MODEL)
    sum_down = _add_half(gw_down, got_down, where, name="add_half_w_down", tr=512)
    (dqa, dkta, dvta), ((ex_down,), (got_up,)) = _attn_a_bwd(
        qa, ka, vta, doa, oa, p_a, linv_a, seq=seq, bq=min(256, seq),
        jobs=[_exchange_job([sum_down]), _swap_job([gw_up])])
    full_down = _add_chips(sum_down, ex_down, where, name="add_chips_w_down", tr=512)
    sum_up = _add_half(gw_up, got_up, where, name="add_half_w_up", tr=512)
    (dqb, dkb, dvb, dbias, dsink), ((ex_up,), (g_down,), (got_o,)) = _attn_b_bwd(
        qtb, ktb, vb3, dob, ob, p_b, stat_b, seq=seq, per_step=min(16, seq // BLOCK),
        jobs=[_exchange_job([sum_up]), _join_job([full_down]), _swap_job([gw_o])])
    full_up = _add_chips(sum_up, ex_up, where, name="add_chips_w_up", tr=512)
    sum_o = _add_half(gw_o, got_o, where, name="add_half_w_o", tr=512)
    (grad_x, dproj, dg1, dgq, dgk), _ = _proj_bwd(
        dqa, dkta, dvta, dqb, dkb, dvb, raw, x2, dx1, g1, w_in_g, gq8, gk2, ck, sk,
        seq=seq, tm=min(512, seq), sub=128)
    packed, packed_rel = _pack_small(dg1, dg2, dg3, dg4, dgq, dgk, dsink, dbias, bucket, loss_t)
    gw_in, ((ex_o,), (g_up,), (gathered, gathered_rel)) = _tn_matmul(
        h1, dproj, name="grad_w_in", tm=1024, tn=2 * IN_CHUNK, tk=min(4096, tokens), chunk=IN_CHUNK,
        vmem_mb=56, jobs=[_exchange_job([sum_o]), _join_job([full_up]), _small_job([packed, packed_rel])])
    full_o = _add_chips(sum_o, ex_o, where, name="add_chips_w_o", tr=512)

    (g_o,), (got_in,) = _run_jobs("tail_swap", [_join_job([full_o]), _swap_job([gw_in])])
    sum_in = _add_half(gw_in, got_in, where, name="add_half_w_in", tr=512)
    ((ex_in,),) = _run_jobs("tail_exchange", [_exchange_job([sum_in])])
    full_in = _add_chips(sum_in, ex_in, where, name="add_chips_w_in", tr=512)
    ((g_in,),) = _run_jobs("tail_join", [_join_job([full_in])])

    big = [[t[None] for t in _adamw(w[0], g, m[0], v[0], name="adamw_" + nm, tr=256)] for nm, w, g, m, v in (
        ("w_in", w_in, g_in, m_w_in, v_w_in), ("w_o", w_o, g_o, m_w_o, v_w_o),
        ("w_up", w_ffn_up, g_up, m_w_ffn_up, v_w_ffn_up), ("w_down", w_ffn_down, g_down, m_w_ffn_down, v_w_ffn_down))]

    loss, small = _small_adamw(
        gathered, gathered_rel,
        (g1, g2, g3, g4, q_norm_a, k_norm_a, sink_b, rel_bias),
        (m_g_pre_mix, m_g_post_mix, m_g_pre_ffn, m_g_post_ffn, m_q_norm_a, m_k_norm_a, m_sink_b, m_rel_bias),
        (v_g_pre_mix, v_g_post_mix, v_g_pre_ffn, v_g_post_ffn, v_q_norm_a, v_k_norm_a, v_sink_b, v_rel_bias))
    s_g1, s_g2, s_g3, s_g4, s_gq, s_gk, s_sink, s_rel = small

    def leaves(i):
        return (big[0][i], big[1][i], s_g1[i], s_g2[i], s_gq[i], s_gk[i], s_sink[i], s_rel[i], s_g3[i],
                big[2][i], big[3][i], s_g4[i])

    loss = loss.reshape(())
    return (loss, grad_x.reshape(batch, seq, D_MODEL), *leaves(0), *leaves(1), *leaves(2), *leaves(3))
```

```python
import functools

import jax
import jax.numpy as jnp
import numpy as np
from jax import lax
from jax.experimental import pallas as pl
from jax.experimental.pallas import tpu as pltpu

F32 = jnp.float32
MM = jnp.bfloat16

D_MODEL = 1024
HEAD_DIM = 64
N_KV = 2
GROUP = 4
Q_WIDTH = 512
KV_WIDTH = 128
D_FF = 4096
GRID_W = 64
BLOCK = 128
N_BUCKETS = 32
MAX_DISTANCE = 128
ROPE_THETA = 10000.0
EPS = 1e-6
NEG_INF = -1e30
SCALE = HEAD_DIM ** -0.5
IN_TOTAL = 1536
N_CHIPS = 4
N_DEV = 8
IN_CHUNK = IN_TOTAL // N_CHIPS
FF_CHUNK = D_FF // N_CHIPS
O_CHUNK = D_MODEL // N_CHIPS
QK_RAW = 640

ADAM_LR = 0.001
ADAM_B1 = 0.9
ADAM_B2 = 0.999
ADAM_EPS = 1e-08
ADAM_WD = 0.01
ADAM_STEP = 10

LANES = 128
MESH = pl.DeviceIdType.MESH
HBM = pl.BlockSpec(memory_space=pl.ANY)
SMEM = pl.BlockSpec(memory_space=pltpu.SMEM)

ROW_G1, ROW_G2, ROW_G3, ROW_G4, ROW_MISC = 0, 1, 2, 3, 4
MISC_GQ, MISC_GK, MISC_SINK, MISC_LOSS = 0, 64, 128, 512


def _cparams(sem, vmem_mb):
    return pltpu.CompilerParams(dimension_semantics=sem, vmem_limit_bytes=vmem_mb * 1024 * 1024)


def _whole(a):
    return pl.BlockSpec(a.shape, lambda i: (0,) * len(a.shape))


def _from_hbm(a):
    return pltpu.with_memory_space_constraint(a, pltpu.HBM)


def _in_hbm(s):
    return pltpu.HBM(s.shape, s.dtype)


class _Job:
    def __init__(self, operands, out_shapes, sems, copies, alias=None):
        self.operands, self.out_shapes, self.sems, self.copies = list(operands), list(out_shapes), list(sems), copies
        self.alias = dict(alias or {})


def _place():
    return lax.axis_index("x"), lax.axis_index("y"), lax.axis_index("c")


_CHIP_FLIPS = ((1, 0), (0, 1), (1, 1))


def _flip(v, bit):
    return 1 - v if bit else v


def _remote(src, dst, send, recv, dev):
    return pltpu.make_async_remote_copy(src_ref=src, dst_ref=dst, send_sem=send, recv_sem=recv,
                                        device_id=dev, device_id_type=MESH)


def _swap_job(grads):
    n = len(grads)

    def copies(ins, outs, sems):
        x, y, c = _place()
        send, recv = sems
        cps = []
        for t in range(n):
            half = ins[t].shape[1] // 2
            cps.append(_remote(ins[t].at[:, pl.ds((1 - c) * half, half), :], outs[t], send.at[t], recv.at[t],
                               (x, y, 1 - c)))
        return cps

    shapes = [jax.ShapeDtypeStruct((g.shape[0], g.shape[1] // 2, g.shape[2]), F32) for g in grads]
    return _Job(grads, shapes, [pltpu.SemaphoreType.DMA((n,)), pltpu.SemaphoreType.DMA((n,))], copies)


def _exchange_job(sums):
    n = len(sums)

    def copies(ins, outs, sems):
        x, y, c = _place()
        send, recv = sems
        cps = []
        for t in range(n):
            for r, (fx, fy) in enumerate(_CHIP_FLIPS):
                kk = 2 * _flip(x, fx) + _flip(y, fy)
                cps.append(_remote(ins[t].at[kk], outs[t].at[r], send.at[t, r], recv.at[t, r],
                                   (_flip(x, fx), _flip(y, fy), c)))
        return cps

    shapes = [jax.ShapeDtypeStruct((3,) + s.shape[1:], s.dtype) for s in sums]
    return _Job(sums, shapes, [pltpu.SemaphoreType.DMA((n, 3)), pltpu.SemaphoreType.DMA((n, 3))], copies)


def _join_job(fulls):
    n = len(fulls)

    def copies(ins, outs, sems):
        x, y, c = _place()
        send, recv = sems
        cps = []
        for t in range(n):
            half = ins[t].shape[0] // 2
            rows = pl.ds(c * half, half)
            cps.append(_remote(ins[t].at[rows], outs[t].at[rows], send.at[t], recv.at[t], (x, y, 1 - c)))
        return cps

    shapes = [jax.ShapeDtypeStruct(f.shape, f.dtype) for f in fulls]
    return _Job(fulls, shapes, [pltpu.SemaphoreType.DMA((n,)), pltpu.SemaphoreType.DMA((n,))], copies,
                alias={t: t for t in range(n)})


def _gather_job(bufs, forward):
    n = len(bufs)

    def copies(ins, outs, sems):
        x, y, c = _place()
        send, recv = sems
        cps = []
        for t in range(n):
            half = ins[t].shape[1] // 2
            rows = pl.ds(c * half, half)
            for r, (fx, fy) in enumerate(_CHIP_FLIPS):
                if forward:
                    kk = 2 * _flip(x, fx) + _flip(y, fy)
                    dev = (x, y, 1 - c)
                else:
                    kk = 2 * x + y
                    dev = (_flip(x, fx), _flip(y, fy), c)
                cps.append(_remote(ins[t].at[kk, rows], outs[t].at[kk, rows], send.at[t, r], recv.at[t, r], dev))
        return cps

    shapes = [jax.ShapeDtypeStruct(b.shape, b.dtype) for b in bufs]
    return _Job(bufs, shapes, [pltpu.SemaphoreType.DMA((n, 3)), pltpu.SemaphoreType.DMA((n, 3))], copies,
                alias={t: t for t in range(n)})


def _call(body, args, *, name, grid, in_specs, out_specs, out_shape, scratch_shapes=(), params=None, jobs=()):
    n_in, n_out, n_scr = len(in_specs), len(out_specs), len(scratch_shapes)
    job_in = [len(j.operands) for j in jobs]
    job_out = [len(j.out_shapes) for j in jobs]
    job_sem = [len(j.sems) for j in jobs]

    def wrapped(*refs):
        pos = 0
        ins = refs[pos:pos + n_in]; pos += n_in
        jins = []
        for k in job_in:
            jins.append(refs[pos:pos + k]); pos += k
        outs = refs[pos:pos + n_out]; pos += n_out
        jouts = []
        for k in job_out:
            jouts.append(refs[pos:pos + k]); pos += k
        scr = refs[pos:pos + n_scr]; pos += n_scr
        jsems = []
        for k in job_sem:
            jsems.append(refs[pos:pos + k]); pos += k
        if jobs:
            ids = [pl.program_id(d) for d in range(len(grid))]
            first = functools.reduce(jnp.logical_and, [i == 0 for i in ids])
            last = functools.reduce(jnp.logical_and, [i == g - 1 for i, g in zip(ids, grid)])

            @pl.when(first)
            def _():
                for j, ji, jo, js in zip(jobs, jins, jouts, jsems):
                    for cp in j.copies(ji, jo, js):
                        cp.start()

        body(*ins, *outs, *scr)
        if jobs:
            @pl.when(last)
            def _():
                for j, ji, jo, js in zip(jobs, jins, jouts, jsems):
                    for cp in j.copies(ji, jo, js):
                        cp.wait()

    aliases = {}
    in_pos, out_pos = n_in, n_out
    for j in jobs:
        for i, o in j.alias.items():
            aliases[in_pos + i] = out_pos + o
        in_pos += len(j.operands)
        out_pos += len(j.out_shapes)
    res = pl.pallas_call(
        wrapped, name=name, grid=grid,
        in_specs=list(in_specs) + [HBM] * sum(job_in),
        out_specs=list(out_specs) + [HBM] * sum(job_out),
        out_shape=list(out_shape) + [_in_hbm(s) for j in jobs for s in j.out_shapes],
        scratch_shapes=list(scratch_shapes) + [s for j in jobs for s in j.sems],
        input_output_aliases=aliases,
        compiler_params=params,
    )(*[a if spec is SMEM else _from_hbm(a) for a, spec in zip(args, in_specs)],
      *[a for j in jobs for a in j.operands])
    own, rest = list(res[:n_out]), list(res[n_out:])
    job_res = []
    for k in job_out:
        job_res.append(rest[:k])
        rest = rest[k:]
    return own, job_res


def _run_jobs(name, jobs):
    def body():
        pass

    return _call(body, (), name=name, grid=(1,), in_specs=[], out_specs=[], out_shape=[], jobs=jobs)[1]


def _dot(a, b):
    return jnp.dot(a, b, preferred_element_type=F32)


def _dot_nt(a, b):
    return lax.dot_general(a, b, (((1,), (1,)), ((), ())), preferred_element_type=F32)


def _dot_tn(a, b):
    return lax.dot_general(a, b, (((0,), (0,)), ((), ())), preferred_element_type=F32)


def _rms_r(x):
    return lax.rsqrt(jnp.mean(x * x, axis=-1, keepdims=True) + EPS)


def _rms_bwd(x, r, g, dy):
    n = x * r
    dn = dy * g
    dx = r * (dn - n * jnp.mean(dn * n, axis=-1, keepdims=True))
    return dx, dy * n


def _seg64_sum(v):
    rows, width = v.shape
    lane = lax.broadcasted_iota(jnp.int32, (rows, LANES), 1)
    lo = lane < HEAD_DIM
    outs = []
    for c in range(width // LANES):
        ch = v[:, c * LANES:(c + 1) * LANES]
        s_lo = jnp.sum(jnp.where(lo, ch, 0.0), axis=-1, keepdims=True)
        s_hi = jnp.sum(jnp.where(lo, 0.0, ch), axis=-1, keepdims=True)
        outs.append(jnp.where(lo, s_lo, s_hi))
    return outs[0] if len(outs) == 1 else jnp.concatenate(outs, axis=-1)


def _head_r(v):
    return lax.rsqrt(_seg64_sum(v * v) * (1.0 / HEAD_DIM) + EPS)


def _swap16(ch):
    lane = lax.broadcasted_iota(jnp.int32, ch.shape, 1)
    return jnp.where((lane % 32) < 16, pltpu.roll(ch, LANES - 16, 1), pltpu.roll(ch, 16, 1))


def _by_chunk(fn, v):
    outs = [fn(v[:, c * LANES:(c + 1) * LANES]) for c in range(v.shape[1] // LANES)]
    return outs[0] if len(outs) == 1 else jnp.concatenate(outs, axis=-1)


def _rope(v, cos, sin_signed):
    return _by_chunk(lambda ch: ch * cos + _swap16(ch) * sin_signed, v)


def _rope_t(g, cos, sin_signed):
    return _by_chunk(lambda ch: ch * cos + _swap16(ch * sin_signed), g)


def _rope_tables(seq):
    nf = HEAD_DIM // 4
    freqs = ROPE_THETA ** (-jnp.arange(nf, dtype=F32) / nf)
    pos = jnp.arange(seq, dtype=jnp.int32)
    row = (pos // GRID_W).astype(F32)
    col = (pos % GRID_W).astype(F32)
    ang_r = row[:, None] * freqs[None, :]
    ang_c = col[:, None] * freqs[None, :]
    cr, sr, cc, sc = jnp.cos(ang_r), jnp.sin(ang_r), jnp.cos(ang_c), jnp.sin(ang_c)
    cos = jnp.concatenate([cr, cr, cc, cc], axis=1)
    sin = jnp.concatenate([-sr, sr, -sc, sc], axis=1)
    return cos, sin


def _t5_bucket(rel):
    nb = N_BUCKETS // 2
    ret = (rel > 0).astype(jnp.int32) * nb
    n = jnp.abs(rel)
    max_exact = nb // 2
    nf = jnp.maximum(n, 1).astype(jnp.float32)
    large = max_exact + (jnp.log(nf / max_exact) / np.float32(np.log(MAX_DISTANCE / max_exact))
                         * (nb - max_exact)).astype(jnp.int32)
    large = jnp.minimum(large, nb - 1)
    return ret + jnp.where(n < max_exact, n, large)


def _window_tables():
    a = jnp.arange(BLOCK, dtype=jnp.int32)
    c = jnp.arange(3 * BLOCK, dtype=jnp.int32)
    rel = c[None, :] - BLOCK - a[:, None]
    bucket = _t5_bucket(rel)
    band = (jnp.abs(rel) <= BLOCK).astype(jnp.int32)
    to3 = lambda t: t.reshape(BLOCK, 3, BLOCK).transpose(1, 2, 0)
    return to3(bucket), to3(band)


def _pre_proj(x, g1, w_in, gq, gk, ck, sk, *, seq, tm, sub):
    tokens = x.shape[0]
    n_seq = seq // tm
    nblk = tm // BLOCK
    batch = tokens // seq

    def body(x_ref, g1_ref, w_ref, gq_ref, gk_ref, ck_ref, sk_ref,
             h1_ref, raw_ref, qa_ref, ka_ref, kta_ref, va_ref, vta_ref,
             qtb_ref, kb_ref, ktb_ref, vb_ref, vtb_ref, proj):
        for r in range(tm // sub):
            rows = slice(r * sub, (r + 1) * sub)
            xv = x_ref[rows, :]
            h = (xv * _rms_r(xv) * g1_ref[...]).astype(MM)
            h1_ref[rows, :] = h
            for j in range(N_CHIPS):
                proj[rows, j * IN_CHUNK:(j + 1) * IN_CHUNK] = _dot(h, w_ref[j])
            qa = proj[rows, 0:Q_WIDTH]
            ka = proj[rows, Q_WIDTH:QK_RAW]
            raw_ref[rows, :] = proj[rows, 0:QK_RAW]
            qn = qa * _head_r(qa) * gq_ref[...]
            qa_ref[rows, :] = (_rope(qn, ck_ref[rows, :], sk_ref[rows, :]) * SCALE).astype(MM)
            kn = ka * _head_r(ka) * gk_ref[...]
            kr = _rope(kn, ck_ref[rows, :], sk_ref[rows, :])
            ka_ref[rows, :] = kr.astype(MM)
            kta_ref[0, :, rows] = kr.T.astype(MM)
            va = proj[rows, 640:768]
            va_ref[rows, :] = va.astype(MM)
            vta_ref[0, :, rows] = va.T.astype(MM)
            qb = proj[rows, 768:1280] * SCALE
            kb = proj[rows, 1280:1408]
            vb = proj[rows, 1408:1536]
            kb_ref[rows, :] = kb.astype(MM)
            vb_ref[rows, :] = vb.astype(MM)
            for j in range(sub // BLOCK):
                blk = slice(j * BLOCK, (j + 1) * BLOCK)
                qtb_ref[r * (sub // BLOCK) + j] = qb[blk, :].T.astype(MM)
                ktb_ref[r * (sub // BLOCK) + j] = kb[blk, :].T.astype(MM)
                vtb_ref[r * (sub // BLOCK) + j] = vb[blk, :].T.astype(MM)

    tok = lambda w: pl.BlockSpec((tm, w), lambda i: (i, 0))
    tab = lambda w: pl.BlockSpec((tm, w), lambda i: (i % n_seq, 0))
    row = lambda w: pl.BlockSpec((1, w), lambda i: (0, 0))
    tposed = pl.BlockSpec((1, LANES, tm), lambda i: (i // n_seq, 0, i % n_seq))
    blocks = pl.BlockSpec((nblk, BLOCK, LANES), lambda i: (i, 0, 0))
    qblocks = pl.BlockSpec((nblk, Q_WIDTH, BLOCK), lambda i: (i, 0, 0))
    tok_mm = lambda w: jax.ShapeDtypeStruct((tokens, w), MM)
    return pl.pallas_call(
        body, name="pre_proj",
        grid=(tokens // tm,),
        in_specs=[tok(D_MODEL), row(D_MODEL),
                  pl.BlockSpec((N_CHIPS, D_MODEL, IN_CHUNK), lambda i: (0, 0, 0)),
                  row(Q_WIDTH), row(KV_WIDTH), tab(KV_WIDTH), tab(KV_WIDTH)],
        out_specs=[tok(D_MODEL), tok(QK_RAW), tok(Q_WIDTH), tok(KV_WIDTH), tposed, tok(KV_WIDTH), tposed,
                   qblocks, tok(KV_WIDTH), blocks, tok(KV_WIDTH), blocks],
        out_shape=[
            tok_mm(D_MODEL),
            jax.ShapeDtypeStruct((tokens, QK_RAW), F32),
            tok_mm(Q_WIDTH),
            tok_mm(KV_WIDTH),
            jax.ShapeDtypeStruct((batch, KV_WIDTH, seq), MM),
            tok_mm(KV_WIDTH),
            jax.ShapeDtypeStruct((batch, KV_WIDTH, seq), MM),
            jax.ShapeDtypeStruct((tokens // BLOCK, Q_WIDTH, BLOCK), MM),
            tok_mm(KV_WIDTH),
            jax.ShapeDtypeStruct((tokens // BLOCK, KV_WIDTH, BLOCK), MM),
            tok_mm(KV_WIDTH),
            jax.ShapeDtypeStruct((tokens // BLOCK, KV_WIDTH, BLOCK), MM),
        ],
        scratch_shapes=[pltpu.VMEM((tm, IN_TOTAL), F32)],
        compiler_params=_cparams(("parallel",), 48),
    )(*map(_from_hbm, (x, g1, w_in, gq, gk, ck, sk)))


def _kv_half(v2, kv):
    return jnp.where(kv == 0, v2[:, :HEAD_DIM], v2[:, HEAD_DIM:])


def _attn_a_fwd(qa, kta, va, *, seq, bq, jobs=()):
    tokens = qa.shape[0]
    batch = tokens // seq
    nq = seq // bq

    def body(q_ref, kt_ref, v_ref, o_ref, p_ref, linv_ref):
        kv = pl.program_id(1)
        kt = kt_ref[0]
        lane = lax.broadcasted_iota(jnp.int32, (seq, KV_WIDTH), 1)
        v = jnp.where((lane < HEAD_DIM) == (kv == 0), v_ref[...], jnp.ones((), MM))
        q_rows = jnp.concatenate([q_ref[:, g * HEAD_DIM:(g + 1) * HEAD_DIM] for g in range(GROUP)], axis=0)
        s_all = _dot(q_rows, kt)
        for g in range(GROUP):
            sl = slice(g * HEAD_DIM, (g + 1) * HEAD_DIM)
            s = s_all[g * bq:(g + 1) * bq]
            pb = jnp.exp((s - jnp.max(s, axis=-1, keepdims=True)).astype(MM))
            p_ref[0, g] = pb
            o2 = _dot(pb, v)
            linv = 1.0 / _kv_half(o2, 1 - kv)[:, 0:1]
            o_ref[:, sl] = _kv_half(o2, kv) * linv
            linv_ref[0, :, g:g + 1] = linv

    return _call(
        body, (qa, kta, va), name="attn_a_fwd", jobs=jobs,
        grid=(batch, N_KV, nq),
        in_specs=[pl.BlockSpec((bq, GROUP * HEAD_DIM), lambda b, k, i: (b * nq + i, k)),
                  pl.BlockSpec((1, HEAD_DIM, seq), lambda b, k, i: (b, k, 0)),
                  pl.BlockSpec((seq, KV_WIDTH), lambda b, k, i: (b, 0))],
        out_specs=[pl.BlockSpec((bq, GROUP * HEAD_DIM), lambda b, k, i: (b * nq + i, k)),
                   pl.BlockSpec((1, GROUP, bq, seq), lambda b, k, i: (k, 0, b * nq + i, 0)),
                   pl.BlockSpec((1, bq, GROUP), lambda b, k, i: (k, b * nq + i, 0))],
        out_shape=[jax.ShapeDtypeStruct((tokens, Q_WIDTH), F32),
                   jax.ShapeDtypeStruct((N_KV, GROUP, tokens, seq), MM),
                   jax.ShapeDtypeStruct((N_KV, tokens, GROUP), F32)],
        params=_cparams(("arbitrary", "arbitrary", "arbitrary"), 56))


def _attn_a_bwd(qa, ka, vta, do, o, p, linv, *, seq, bq, jobs=()):
    tokens = qa.shape[0]
    batch = tokens // seq
    nq = seq // bq

    def body(q_ref, k_ref, vt_ref, do_ref, o_ref, p_ref, linv_ref, dq_ref, dkt_ref, dvt_ref):
        kv = pl.program_id(1)

        @pl.when(pl.program_id(2) == 0)
        def _():
            dkt_ref[...] = jnp.zeros_like(dkt_ref)
            dvt_ref[...] = jnp.zeros_like(dvt_ref)

        vt = vt_ref[0]
        k2 = k_ref[...]
        for g in range(GROUP):
            sl = slice(g * HEAD_DIM, (g + 1) * HEAD_DIM)
            dof = do_ref[:, sl]
            delta = jnp.sum(dof * o_ref[:, sl], axis=-1, keepdims=True)
            linv_g = linv_ref[0, :, g:g + 1]
            pb = p_ref[0, g]
            dp = _dot(dof.astype(MM), vt)
            ds = pb * ((dp - delta) * linv_g).astype(MM)
            dq_ref[:, sl] = _kv_half(_dot(ds, k2), kv)
            dkt_ref[0] += _dot_tn(q_ref[:, sl], ds)
            dvt_ref[0] += _dot_tn((dof * linv_g).astype(MM), pb)

    qspec = pl.BlockSpec((bq, GROUP * HEAD_DIM), lambda b, k, i: (b * nq + i, k))
    tspec = pl.BlockSpec((1, HEAD_DIM, seq), lambda b, k, i: (b, k, 0))
    return _call(
        body, (qa, ka, vta, do, o, p, linv), name="attn_a_bwd", jobs=jobs,
        grid=(batch, N_KV, nq),
        in_specs=[qspec, pl.BlockSpec((seq, KV_WIDTH), lambda b, k, i: (b, 0)), tspec, qspec, qspec,
                  pl.BlockSpec((1, GROUP, bq, seq), lambda b, k, i: (k, 0, b * nq + i, 0)),
                  pl.BlockSpec((1, bq, GROUP), lambda b, k, i: (k, b * nq + i, 0))],
        out_specs=[qspec, tspec, tspec],
        out_shape=[jax.ShapeDtypeStruct((tokens, Q_WIDTH), F32),
                   jax.ShapeDtypeStruct((batch, KV_WIDTH, seq), F32),
                   jax.ShapeDtypeStruct((batch, KV_WIDTH, seq), F32)],
        params=_cparams(("arbitrary", "arbitrary", "arbitrary"), 56))


def _bias_build(rel_bias_t, bucket_t, band_t):
    def body(tab_ref, bucket_ref, band_ref, bias_ref):
        for h in range(GROUP * N_KV):
            for piece in range(3):
                bk = bucket_ref[piece]
                acc = jnp.zeros((BLOCK, BLOCK), F32)
                for b in range(N_BUCKETS):
                    acc = jnp.where(bk == b, tab_ref[h, b], acc)
                g = h % GROUP
                bias_ref[h // GROUP, piece, :, g * BLOCK:(g + 1) * BLOCK] = jnp.where(band_ref[piece] != 0, acc, NEG_INF)

    out = jax.ShapeDtypeStruct((N_KV, 3, BLOCK, GROUP * BLOCK), F32)
    return pl.pallas_call(
        body, name="bias_build", grid=(1,),
        in_specs=[SMEM, _whole(bucket_t), _whole(band_t)], out_specs=_whole(out), out_shape=out,
    )(rel_bias_t, bucket_t, band_t)


def _pad_heads(t, kv):
    outs = []
    for g in range(GROUP):
        tg = t[g * HEAD_DIM:(g + 1) * HEAD_DIM, :]
        zero = jnp.zeros_like(tg)
        outs.append(jnp.concatenate([jnp.where(kv == 0, tg, zero), jnp.where(kv == 0, zero, tg)], axis=0))
    return jnp.concatenate(outs, axis=-1)


def _unpad_heads(t, kv):
    outs = [_kv_half(t[:, g * BLOCK:(g + 1) * BLOCK].T, kv) for g in range(GROUP)]
    return jnp.concatenate(outs, axis=-1)


def _sink_row(sink_ref, kv):
    lane_head = lax.broadcasted_iota(jnp.int32, (1, GROUP * BLOCK), 1) // BLOCK
    row = jnp.zeros((1, GROUP * BLOCK), F32)
    for g in range(GROUP):
        row = jnp.where(lane_head == g, sink_ref[0, kv * GROUP + g], row)
    return row


def _window_scores_t(k_ref, idx, qpad, bias_ref, n, nblk):
    pieces = []
    for piece in range(3):
        s = _dot(k_ref[idx[piece]], qpad) + bias_ref[0, piece]
        if piece == 0:
            s = jnp.where(n > 0, s, NEG_INF)
        if piece == 2:
            s = jnp.where(n < nblk - 1, s, NEG_INF)
        pieces.append(s)
    return pieces


def _attn_b_fwd(qtb, kb3, vtb, bias, sink, *, seq, per_step, jobs=()):
    nblk_all = qtb.shape[0]
    tokens = nblk_all * BLOCK
    batch = tokens // seq
    nblk = seq // BLOCK
    nstep = nblk // per_step

    def body(sink_ref, q_ref, k_ref, vt_ref, bias_ref, o_ref, p_ref, stat_ref):
        kv = pl.program_id(0)
        first = pl.program_id(2) * per_step
        sink_row = _sink_row(sink_ref, kv)
        stat_row = lax.broadcasted_iota(jnp.int32, (8, GROUP * BLOCK), 0)

        def block(i, carry):
            n = first + i
            idx = (jnp.maximum(n - 1, 0), n, jnp.minimum(n + 1, nblk - 1))
            rows = slice(i * BLOCK, (i + 1) * BLOCK)
            qpad = _pad_heads(q_ref[n], kv)
            ss = _window_scores_t(k_ref, idx, qpad, bias_ref, n, nblk)
            m = jnp.maximum(jnp.maximum(jnp.max(ss[0], axis=0, keepdims=True),
                                        jnp.max(ss[1], axis=0, keepdims=True)),
                            jnp.maximum(jnp.max(ss[2], axis=0, keepdims=True), sink_row))
            ps = [jnp.exp(s - m) for s in ss]
            e_sink = jnp.exp(sink_row - m)
            rinv = 1.0 / (jnp.sum(ps[0], axis=0, keepdims=True) + jnp.sum(ps[1], axis=0, keepdims=True)
                          + jnp.sum(ps[2], axis=0, keepdims=True) + e_sink)
            ot = jnp.zeros((KV_WIDTH, GROUP * BLOCK), F32)
            for piece in range(3):
                pb = ps[piece].astype(MM)
                p_ref[0, i, piece] = pb
                ot = ot + _dot(vt_ref[idx[piece]], pb)
            o_ref[rows, :] = _unpad_heads(ot * rinv, kv)
            stat_ref[0, i] = jnp.where(stat_row == 0, rinv, e_sink * rinv)
            return carry

        for i in range(per_step):
            block(i, 0)

    both = pl.BlockSpec((nblk, BLOCK, KV_WIDTH), lambda k, b, j: (b, 0, 0))
    return _call(
        body, (sink, qtb, kb3, vtb, bias), name="attn_b_fwd", jobs=jobs,
        grid=(N_KV, batch, nstep),
        in_specs=[SMEM, pl.BlockSpec((nblk, GROUP * HEAD_DIM, BLOCK), lambda k, b, j: (b, k, 0)), both, both,
                  pl.BlockSpec((1, 3, BLOCK, GROUP * BLOCK), lambda k, b, j: (k, 0, 0, 0))],
        out_specs=[pl.BlockSpec((per_step * BLOCK, GROUP * HEAD_DIM), lambda k, b, j: (b * nstep + j, k)),
                   pl.BlockSpec((1, per_step, 3, BLOCK, GROUP * BLOCK), lambda k, b, j: (k, b * nstep + j, 0, 0, 0)),
                   pl.BlockSpec((1, per_step, 8, GROUP * BLOCK), lambda k, b, j: (k, b * nstep + j, 0, 0))],
        out_shape=[jax.ShapeDtypeStruct((tokens, Q_WIDTH), F32),
                   jax.ShapeDtypeStruct((N_KV, nblk_all, 3, BLOCK, GROUP * BLOCK), MM),
                   jax.ShapeDtypeStruct((N_KV, nblk_all, 8, GROUP * BLOCK), F32)],
        params=_cparams(("arbitrary", "arbitrary", "arbitrary"), 48))


def _attn_b_bwd(qtb, ktb, vb3, do, o, p, stat, *, seq, per_step, jobs=()):
    nblk_all = qtb.shape[0]
    tokens = nblk_all * BLOCK
    batch = tokens // seq
    nblk = seq // BLOCK
    nstep = nblk // per_step

    def body(q_ref, kt_ref, v_ref, do_ref, o_ref, p_ref, stat_ref,
             dq_ref, dk_ref, dv_ref, dbias_ref, dsink_ref):
        kv = pl.program_id(0)
        step = pl.program_id(2)
        first = step * per_step

        @pl.when(jnp.logical_and(pl.program_id(1) == 0, step == 0))
        def _():
            dbias_ref[...] = jnp.zeros_like(dbias_ref)
            dsink_ref[...] = jnp.zeros_like(dsink_ref)

        @pl.when(step == 0)
        def _():
            dk_ref[...] = jnp.zeros_like(dk_ref)
            dv_ref[...] = jnp.zeros_like(dv_ref)

        def block(i, dsink):
            n = first + i
            idx = (jnp.maximum(n - 1, 0), n, jnp.minimum(n + 1, nblk - 1))
            rows = slice(i * BLOCK, (i + 1) * BLOCK)
            qpad = _pad_heads(q_ref[n], kv)
            dot_t = do_ref[rows, :].T
            prod = dot_t * o_ref[rows, :].T
            delta = jnp.concatenate(
                [jnp.sum(prod[g * HEAD_DIM:(g + 1) * HEAD_DIM, :], axis=0, keepdims=True) for g in range(GROUP)],
                axis=-1)
            stats = stat_ref[0, i]
            rinv, p_sink = stats[0:1, :], stats[1:2, :]
            dopad32 = _pad_heads(dot_t, kv)
            dopad = dopad32.astype(MM)
            dopad_n = (dopad32 * rinv).astype(MM)
            dqt = jnp.zeros((KV_WIDTH, GROUP * BLOCK), F32)
            for piece in range(3):
                pb = p_ref[0, i, piece]
                dst = pb.astype(F32) * ((_dot(v_ref[idx[piece]], dopad) - delta) * rinv)
                dsb = dst.astype(MM)
                dbias_ref[0, piece] += dst
                dqt = dqt + _dot(kt_ref[idx[piece]], dsb)
                dk_ref[0, idx[piece]] += _dot_nt(dsb, qpad)
                dv_ref[0, idx[piece]] += _dot_nt(pb, dopad_n)
            dq_ref[rows, :] = _unpad_heads(dqt, kv)
            return dsink - p_sink * delta

        dsink = jnp.zeros((1, GROUP * BLOCK), F32)
        for i in range(per_step):
            dsink = block(i, dsink)
        dsink_ref[0] += jnp.broadcast_to(dsink, (8, GROUP * BLOCK))

    qspec = pl.BlockSpec((per_step * BLOCK, GROUP * HEAD_DIM), lambda k, b, j: (b * nstep + j, k))
    both = pl.BlockSpec((nblk, BLOCK, KV_WIDTH), lambda k, b, j: (b, 0, 0))
    grad =pl.BlockSpec((1, nblk, BLOCK, KV_WIDTH), lambda k, b, j: (k, b, 0, 0))
    return _call(
        body, (qtb, ktb, vb3, do, o, p, stat), name="attn_b_bwd", jobs=jobs,
        grid=(N_KV, batch, nstep),
        in_specs=[pl.BlockSpec((nblk, GROUP * HEAD_DIM, BLOCK), lambda k, b, j: (b, k, 0)), both, both,
                  qspec, qspec,
                  pl.BlockSpec((1, per_step, 3, BLOCK, GROUP * BLOCK), lambda k, b, j: (k, b * nstep + j, 0, 0, 0)),
                  pl.BlockSpec((1, per_step, 8, GROUP * BLOCK), lambda k, b, j: (k, b * nstep + j, 0, 0))],
        out_specs=[qspec, grad, grad,
                   pl.BlockSpec((1, 3, BLOCK, GROUP * BLOCK), lambda k, b, j: (k, 0, 0, 0)),
                   pl.BlockSpec((1, 8, GROUP * BLOCK), lambda k, b, j: (k, 0, 0))],
        out_shape=[jax.ShapeDtypeStruct((tokens, Q_WIDTH), F32),
                   jax.ShapeDtypeStruct((N_KV, nblk_all, BLOCK, KV_WIDTH), F32),
                   jax.ShapeDtypeStruct((N_KV, nblk_all, BLOCK, KV_WIDTH), F32),
                   jax.ShapeDtypeStruct((N_KV, 3, BLOCK, GROUP * BLOCK), F32),
                   jax.ShapeDtypeStruct((N_KV, 8, GROUP * BLOCK), F32)],
        params=_cparams(("arbitrary", "arbitrary", "arbitrary"), 56))


def _resident(shape):
    return pl.BlockSpec(shape, lambda i: (0,) * len(shape), pipeline_mode=pl.Buffered(1))


def _mix_ffn_fwd(oa, ob, w_o, x, g2, g3, w_up, w_down, target, g4, *, tm):
    tokens = x.shape[0]
    nt = tokens // tm

    def body(oa_ref, ob_ref, wo_ref, x_ref, g2_ref, g3_ref, wu_ref, wd_ref, t_ref, g4_ref,
             mix_ref, x1_ref, h2_ref, o_ref, u_ref, df_ref, dy_ref, loss_ref, dg4_ref):
        o = jnp.concatenate([oa_ref[...].astype(MM), ob_ref[...].astype(MM)], axis=-1)
        o_ref[...] = o
        mix = _dot(o, wo_ref[...])
        mix_ref[...] = mix
        x1 = x_ref[...] + mix * _rms_r(mix) * g2_ref[...]
        x1_ref[...] = x1
        h2v = (x1 * _rms_r(x1) * g3_ref[...]).astype(MM)
        h2_ref[...] = h2v
        f = jnp.zeros((tm, D_MODEL), F32)
        for c in range(N_CHIPS):
            u = jnp.maximum(_dot(h2v, wu_ref[c]), 0.0)
            u_ref[:, c * FF_CHUNK:(c + 1) * FF_CHUNK] = u.astype(MM)
            f = f + _dot((u * u).astype(MM), wd_ref[c * FF_CHUNK:(c + 1) * FF_CHUNK, :])
        r = _rms_r(f)
        g4v = g4_ref[...]
        err = x1 + f * r * g4v - t_ref[...]
        sq = jnp.sum(err * err, axis=-1, keepdims=True)
        loss_ref[0] = jnp.broadcast_to(jnp.sum(sq, axis=0, keepdims=True) * (0.5 / D_MODEL), (8, LANES))
        dy = err * (1.0 / D_MODEL)
        dy_ref[...] = dy
        dfv, dgv = _rms_bwd(f, r, g4v, dy)
        df_ref[...] = dfv.astype(MM)
        dg4_ref[0] = jnp.sum(dgv, axis=0, keepdims=True)

    tok = pl.BlockSpec((tm, D_MODEL), lambda i: (i, 0))
    half = pl.BlockSpec((tm, Q_WIDTH), lambda i: (i, 0))
    row = pl.BlockSpec((1, D_MODEL), lambda i: (0, 0))
    tok_f32 = jax.ShapeDtypeStruct((tokens, D_MODEL), F32)
    tok_mm = jax.ShapeDtypeStruct((tokens, D_MODEL), MM)
    return pl.pallas_call(
        body, name="mix_ffn_fwd",
        grid=(nt,),
        in_specs=[half, half, _resident((D_MODEL, D_MODEL)), tok, row, row,
                  _resident((N_CHIPS, D_MODEL, FF_CHUNK)), _resident((D_FF, D_MODEL)), tok, row],
        out_specs=[tok, tok, tok, tok, pl.BlockSpec((tm, D_FF), lambda i: (i, 0)), tok, tok,
                   pl.BlockSpec((1, 8, LANES), lambda i: (i, 0, 0)),
                   pl.BlockSpec((1, 1, D_MODEL), lambda i: (i, 0, 0))],
        out_shape=[tok_f32,
                   tok_f32,
                   tok_mm,
                   tok_mm,
                   jax.ShapeDtypeStruct((tokens, D_FF), MM),
                   tok_mm,
                   tok_f32,
                   jax.ShapeDtypeStruct((nt, 8, LANES), F32),
                   jax.ShapeDtypeStruct((nt, 1, D_MODEL), F32)],
        compiler_params=_cparams(("parallel",), 56),
    )(*map(_from_hbm, (oa, ob, w_o, x, g2, g3, w_up, w_down, target, g4)))


def _ffn_bwd_act(df, w_down, u, w_up, x1, dy, mix, g3, g2, w_o, *, tm):
    tokens = df.shape[0]
    nt = tokens // tm

    def body(df_ref, wd_ref, u_ref, wu_ref, x1_ref, dy_ref, mix_ref, g3_ref, g2_ref, wo_ref,
             dz_ref, dx1_ref, dmix_ref, dg3_ref, dg2_ref, doa_ref, dob_ref):
        dfv = df_ref[...]
        dh2 = jnp.zeros((tm, D_MODEL), F32)
        for c in range(N_CHIPS):
            cols = slice(c * FF_CHUNK, (c + 1) * FF_CHUNK)
            da = _dot_nt(dfv, wd_ref[cols, :])
            dz = (da * (2.0 * u_ref[:, cols].astype(F32))).astype(MM)
            dz_ref[:, cols] = dz
            dh2 = dh2 + _dot_nt(dz, wu_ref[c])
        x1 = x1_ref[...]
        dxn, dg3v = _rms_bwd(x1, _rms_r(x1), g3_ref[...], dh2)
        dx1 = dy_ref[...] + dxn
        dx1_ref[...] = dx1
        dg3_ref[0] = jnp.sum(dg3v, axis=0, keepdims=True)
        mix = mix_ref[...]
        dmix, dg2v = _rms_bwd(mix, _rms_r(mix), g2_ref[...], dx1)
        dmb = dmix.astype(MM)
        dmix_ref[...] = dmb
        dg2_ref[0] = jnp.sum(dg2v, axis=0, keepdims=True)
        doa_ref[...] = _dot_nt(dmb, wo_ref[0:Q_WIDTH, :])
        dob_ref[...] = _dot_nt(dmb, wo_ref[Q_WIDTH:D_MODEL, :])

    tok = pl.BlockSpec((tm, D_MODEL), lambda i: (i, 0))
    half = pl.BlockSpec((tm, Q_WIDTH), lambda i: (i, 0))
    wide = pl.BlockSpec((tm, D_FF), lambda i: (i, 0))
    row = pl.BlockSpec((1, D_MODEL), lambda i: (0, 0))
    part = pl.BlockSpec((1, 1, D_MODEL), lambda i: (i, 0, 0))
    return pl.pallas_call(
        body, name="ffn_bwd_act",
        grid=(nt,),
        in_specs=[tok, _resident((D_FF, D_MODEL)), wide, _resident((N_CHIPS, D_MODEL, FF_CHUNK)),
                  tok, tok, tok, row, row, _resident((D_MODEL, D_MODEL))],
        out_specs=[wide, tok, tok, part, part, half, half],
        out_shape=[jax.ShapeDtypeStruct((tokens, D_FF), MM),
                   jax.ShapeDtypeStruct((tokens, D_MODEL), F32),
                   jax.ShapeDtypeStruct((tokens, D_MODEL), MM),
                   jax.ShapeDtypeStruct((nt, 1, D_MODEL), F32),
                   jax.ShapeDtypeStruct((nt, 1, D_MODEL), F32),
                   jax.ShapeDtypeStruct((tokens, Q_WIDTH), F32),
                   jax.ShapeDtypeStruct((tokens, Q_WIDTH), F32)],
        compiler_params=_cparams(("parallel",), 56),
    )(*map(_from_hbm, (df, w_down, u, w_up, x1, dy, mix, g3, g2, w_o)))


def _tn_matmul(a, b, *, name, tm, tn, tk, chunk=None, square_a=False, vmem_mb=48, jobs=()):
    tokens, m_dim = a.shape
    n_dim = b.shape[1]
    chunked = chunk is not None
    if chunked:
        assert tm == m_dim and tn % chunk == 0

    def body(a_ref, b_ref, o_ref):
        av = a_ref[...]
        if square_a:
            av = av.astype(F32)
            av = av * av
        part = _dot_tn(av.astype(MM), b_ref[...].astype(MM))
        if chunked:
            part = jnp.stack([part[:, c * chunk:(c + 1) * chunk] for c in range(tn // chunk)])

        @pl.when(pl.program_id(2) == 0)
        def _():
            o_ref[...] = part

        @pl.when(pl.program_id(2) > 0)
        def _():
            o_ref[...] += part

    if chunked:
        out_spec = pl.BlockSpec((tn // chunk, tm, chunk), lambda i, j, k: (j, 0, 0))
        out_shape = jax.ShapeDtypeStruct((n_dim // chunk, m_dim, chunk), F32)
    else:
        out_spec = pl.BlockSpec((tm, tn), lambda i, j, k: (i, j))
        out_shape = jax.ShapeDtypeStruct((m_dim, n_dim), F32)
    (out,), job_res = _call(
        body, (a, b), name=name, jobs=jobs,
        grid=(m_dim // tm, n_dim // tn, tokens // tk),
        in_specs=[pl.BlockSpec((tk, tm), lambda i, j, k: (k, i)),
                  pl.BlockSpec((tk, tn), lambda i, j, k: (k, j))],
        out_specs=[out_spec], out_shape=[_in_hbm(out_shape)],
        params=_cparams(("arbitrary", "arbitrary", "arbitrary"), vmem_mb))
    return out, job_res


def _proj_bwd(dqa, dkta, dvta, dqb, dktb, dvtb, raw, x, dx1, g1, w_in, gq, gk, ck, sk, *, seq, tm, sub, jobs=()):
    tokens = x.shape[0]
    nt = tokens // tm
    n_seq = seq // tm
    nblk = tm // BLOCK

    def body(dqa_ref, dkta_ref, dvta_ref, dqb_ref, dkb_ref, dvb_ref, raw_ref, x_ref, dx1_ref, g1_ref, w_ref,
             gq_ref, gk_ref, ck_ref, sk_ref,
             gx_ref, dproj_ref, dg1_ref, dgq_ref, dgk_ref, dp):
        parts = []
        for r in range(tm // sub):
            rows = slice(r * sub, (r + 1) * sub)
            qa = raw_ref[rows, 0:Q_WIDTH]
            dqn = _rope_t(dqa_ref[rows, :], ck_ref[rows, :], sk_ref[rows, :]) * SCALE
            rq = _head_r(qa)
            nq = qa * rq
            dnq = dqn * gq_ref[...]
            dp[rows, 0:Q_WIDTH] = rq * (dnq - nq * (_seg64_sum(dnq * nq) * (1.0 / HEAD_DIM)))

            ka = raw_ref[rows, Q_WIDTH:QK_RAW]
            dkn = _rope_t(dkta_ref[0, :, rows].T, ck_ref[rows, :], sk_ref[rows, :])
            rk = _head_r(ka)
            nk = ka * rk
            dnk = dkn * gk_ref[...]
            dp[rows, 512:640] = rk * (dnk - nk * (_seg64_sum(dnk * nk) * (1.0 / HEAD_DIM)))

            dp[rows, 640:768] = dvta_ref[0, :, rows].T
            dp[rows, 768:1280] = dqb_ref[rows, :] * SCALE
            for j in range(r * sub // BLOCK, (r + 1) * sub // BLOCK):
                dp[j * BLOCK:(j + 1) * BLOCK, 1280:1408] = dkb_ref[0, j] + dkb_ref[1, j]
                dp[j * BLOCK:(j + 1) * BLOCK, 1408:1536] = dvb_ref[0, j] + dvb_ref[1, j]

            dproj = dp[rows, :].astype(MM)
            dproj_ref[rows, :] = dproj
            dh1 = _dot_nt(dproj[:, 0:IN_CHUNK], w_ref[0])
            for j in range(1, N_CHIPS):
                dh1 = dh1 + _dot_nt(dproj[:, j * IN_CHUNK:(j + 1) * IN_CHUNK], w_ref[j])
            xv = x_ref[rows, :]
            dxn, dg1v = _rms_bwd(xv, _rms_r(xv), g1_ref[...], dh1)
            gx_ref[rows, :] = dx1_ref[rows, :] + dxn
            parts.append((jnp.sum(dqn * nq, axis=0, keepdims=True), jnp.sum(dkn * nk, axis=0, keepdims=True),
                          jnp.sum(dg1v, axis=0, keepdims=True)))
        dgq_ref[0] = functools.reduce(jnp.add, [p[0] for p in parts])
        dgk_ref[0] = functools.reduce(jnp.add, [p[1] for p in parts])
        dg1_ref[0] = functools.reduce(jnp.add, [p[2] for p in parts])

    tok = lambda w: pl.BlockSpec((tm, w), lambda i: (i, 0))
    tab = lambda w: pl.BlockSpec((tm, w), lambda i: (i % n_seq, 0))
    row = lambda w: pl.BlockSpec((1, w), lambda i: (0, 0))
    tposed = pl.BlockSpec((1, KV_WIDTH, tm), lambda i: (i // n_seq, 0, i % n_seq))
    blocks = pl.BlockSpec((N_KV, nblk, BLOCK, KV_WIDTH), lambda i: (0, i, 0, 0))
    part = lambda w: pl.BlockSpec((1, 1, w), lambda i: (i, 0, 0))
    return _call(
        body, (dqa, dkta, dvta, dqb, dktb, dvtb, raw, x, dx1, g1, w_in, gq, gk, ck, sk),
        name="proj_bwd", jobs=jobs,
        grid=(nt,),
        in_specs=[tok(Q_WIDTH), tposed, tposed, tok(Q_WIDTH), blocks, blocks, tok(QK_RAW), tok(D_MODEL),
                  tok(D_MODEL), row(D_MODEL),
                  pl.BlockSpec((N_CHIPS, D_MODEL, IN_CHUNK), lambda i: (0, 0, 0)),
                  row(Q_WIDTH), row(KV_WIDTH), tab(KV_WIDTH), tab(KV_WIDTH)],
        out_specs=[tok(D_MODEL), tok(IN_TOTAL), part(D_MODEL), part(Q_WIDTH), part(KV_WIDTH)],
        out_shape=[jax.ShapeDtypeStruct((tokens, D_MODEL), F32),
                   jax.ShapeDtypeStruct((tokens, IN_TOTAL), MM),
                   jax.ShapeDtypeStruct((nt, 1, D_MODEL), F32),
                   jax.ShapeDtypeStruct((nt, 1, Q_WIDTH), F32),
                   jax.ShapeDtypeStruct((nt, 1, KV_WIDTH), F32)],
        scratch_shapes=[pltpu.VMEM((tm, IN_TOTAL), F32)],
        params=_cparams(("arbitrary",), 56))


def _pack_small(dg1, dg2, dg3, dg4, dgq, dgk, dsink, dbias, bucket, loss):
    def body(dg1_ref, dg2_ref, dg3_ref, dg4_ref, dgq_ref, dgk_ref, dsink_ref, dbias_ref, bucket_ref, loss_ref,
             out_ref, rel_ref):
        out_ref[...] = jnp.zeros_like(out_ref)
        for r, ref in ((ROW_G1, dg1_ref), (ROW_G2, dg2_ref), (ROW_G3, dg3_ref), (ROW_G4, dg4_ref)):
            acc = ref[0]
            for t in range(1, ref.shape[0]):
                acc = acc + ref[t]
            out_ref[r:r + 1, :] = acc

        def fold(ref, heads):
            acc = ref[0]
            for t in range(1, ref.shape[0]):
                acc = acc + ref[t]
            tot = acc[:, 0:HEAD_DIM]
            for h in range(1, heads):
                tot = tot + acc[:, h * HEAD_DIM:(h + 1) * HEAD_DIM]
            return tot

        out_ref[ROW_MISC:ROW_MISC + 1, MISC_GQ:MISC_GQ + HEAD_DIM] = fold(dgq_ref, GROUP * N_KV)
        out_ref[ROW_MISC:ROW_MISC + 1, MISC_GK:MISC_GK + HEAD_DIM] = fold(dgk_ref, N_KV)
        for h in range(GROUP * N_KV):
            g = h % GROUP
            out_ref[ROW_MISC:ROW_MISC + 1, MISC_SINK + h:MISC_SINK + h + 1] = jnp.sum(
                dsink_ref[h // GROUP, 0:1, g * BLOCK:(g + 1) * BLOCK], axis=-1, keepdims=True)
        lacc = loss_ref[0, 0:1, 0:1]
        for t in range(1, loss_ref.shape[0]):
            lacc = lacc + loss_ref[t, 0:1, 0:1]
        out_ref[ROW_MISC:ROW_MISC + 1, MISC_LOSS:MISC_LOSS + 1] = lacc
        lane = lax.broadcasted_iota(jnp.int32, (N_BUCKETS, LANES), 1)
        row = lax.broadcasted_iota(jnp.int32, (N_BUCKETS, LANES), 0)

        def per_bucket(b, acc):
            for h in range(GROUP * N_KV):
                g = h % GROUP
                sel = jnp.zeros((BLOCK, BLOCK), F32)
                for piece in range(3):
                    sel = sel + jnp.where(bucket_ref[piece] == b,
                                          dbias_ref[h // GROUP, piece, :, g * BLOCK:(g + 1) * BLOCK], 0.0)
                tot = jnp.sum(jnp.sum(sel, axis=0, keepdims=True), axis=-1, keepdims=True)
                acc = jnp.where((row == b) & (lane == h), tot, acc)
            return acc

        rel_ref[...] = lax.fori_loop(0, N_BUCKETS, per_bucket, jnp.zeros((N_BUCKETS, LANES), F32))

    args = (dg1, dg2, dg3, dg4, dgq, dgk, dsink, dbias, bucket, loss)
    outs = [jax.ShapeDtypeStruct((8, D_MODEL), F32), jax.ShapeDtypeStruct((N_BUCKETS, LANES), F32)]
    return pl.pallas_call(
        body, name="pack_small", grid=(1,),
        in_specs=[_whole(a) for a in args], out_specs=[_whole(o) for o in outs], out_shape=outs,
        compiler_params=pltpu.CompilerParams(vmem_limit_bytes=32 * 1024 * 1024),
    )(*map(_from_hbm, args))


def _gather_weights(shards, whole):
    n = len(shards)
    full = [t for t in range(n) if whole[t]]

    def body(*refs):
        ins, outs = refs[:n], refs[n:2 * n]
        raw, stage = refs[2 * n:3 * n], refs[3 * n:4 * n]
        load_sem, local_sem, ici_send, ici_recv, d2d_send, d2d_recv = refs[4 * n:]
        x, y, c = _place()
        k = 2 * x + y
        sibling = (x, y, 1 - c)
        order = full + [t for t in range(n) if t not in full]
        loads = {t: pltpu.make_async_copy(ins[t], raw[t], load_sem.at[t]) for t in order}
        for t in order:
            loads[t].start()
        copies, sends = [], []
        for t in order:
            loads[t].wait()
            stage[t][...] = raw[t][...].astype(MM)
            mine = pltpu.make_async_copy(stage[t], outs[t].at[k], local_sem.at[t])
            mine.start()
            copies.append(mine)
            if t in full:
                half = ins[t].shape[0] // 2
                rows = pl.ds(c * half, half)
                for r, (fx, fy) in enumerate(_CHIP_FLIPS):
                    cp = _remote(stage[t].at[rows], outs[t].at[k, rows], ici_send.at[t, r], ici_recv.at[t, r],
                                 (_flip(x, fx), _flip(y, fy), c))
                    cp.start()
                    sends.append(cp)
        for t in full:
            half = ins[t].shape[0] // 2
            rows = pl.ds(c * half, half)
            for r, (fx, fy) in enumerate(_CHIP_FLIPS):
                kk = 2 * _flip(x, fx) + _flip(y, fy)
                landed = outs[t].at[kk, rows]
                _remote(landed, landed, ici_send.at[t, r], ici_recv.at[t, r], sibling).wait_recv()
                fwd = _remote(landed, landed, d2d_send.at[t, r], d2d_recv.at[t, r], sibling)
                fwd.start()
                sends.append(fwd)
        for t in full:
            half = ins[t].shape[0] // 2
            other = pl.ds((1 - c) * half, half)
            for r, (fx, fy) in enumerate(_CHIP_FLIPS):
                kk = 2 * _flip(x, fx) + _flip(y, fy)
                theirs = outs[t].at[kk, other]
                _remote(theirs, theirs, d2d_send.at[t, r], d2d_recv.at[t, r], sibling).wait_recv()
        for cp in sends:
            cp.wait_send()
        for cp in copies:
            cp.wait()

    return pl.pallas_call(
        body, name="gather_weights",
        in_specs=[HBM] * n, out_specs=[HBM] * n,
        out_shape=[pltpu.HBM((N_CHIPS,) + s.shape, MM) for s in shards],
        scratch_shapes=[pltpu.VMEM(s.shape, F32) for s in shards] + [pltpu.VMEM(s.shape, MM) for s in shards] + [
            pltpu.SemaphoreType.DMA((n,)), pltpu.SemaphoreType.DMA((n,)),
            pltpu.SemaphoreType.DMA((n, 3)), pltpu.SemaphoreType.DMA((n, 3)),
            pltpu.SemaphoreType.DMA((n, 3)), pltpu.SemaphoreType.DMA((n, 3))],
        compiler_params=pltpu.CompilerParams(vmem_limit_bytes=40 * 1024 * 1024),
    )(*shards)


def _add_half(grad, got, where, *, name, tr):
    nch, half, cols = got.shape
    tr = min(tr, half)
    nblk = half // tr

    def body(where_ref, g_ref, r_ref, o_ref):
        o_ref[...] = (g_ref[...] + r_ref[...]).astype(MM)

    return pl.pallas_call(
        body, name=name,
        grid_spec=pltpu.PrefetchScalarGridSpec(
            num_scalar_prefetch=1, grid=(nch, nblk),
            in_specs=[pl.BlockSpec((1, tr, cols), lambda j, i, where_ref: (j, where_ref[1] * nblk + i, 0)),
                      pl.BlockSpec((1, tr, cols), lambda j, i, where_ref: (j, i, 0))],
            out_specs=pl.BlockSpec((1, tr, cols), lambda j, i, where_ref: (j, i, 0))),
        out_shape=jax.ShapeDtypeStruct(got.shape, MM),
        compiler_params=_cparams(("parallel", "parallel"), 32),
    )(where, grad, got)


def _add_chips(own, got, where, *, name, tr):
    _, half, cols = own.shape
    tr = min(tr, half)
    nblk = half // tr

    def body(where_ref, o_ref, g_ref, out_ref):
        f = lambda v: v.astype(F32)
        out_ref[...] = ((f(o_ref[0]) + f(g_ref[0])) + f(g_ref[1])) + f(g_ref[2])

    return pl.pallas_call(
        body, name=name,
        grid_spec=pltpu.PrefetchScalarGridSpec(
            num_scalar_prefetch=1, grid=(nblk,),
            in_specs=[pl.BlockSpec((1, tr, cols), lambda i, where_ref: (where_ref[0], i, 0)),
                      pl.BlockSpec((3, tr, cols), lambda i, where_ref: (0, i, 0))],
            out_specs=pl.BlockSpec((tr, cols), lambda i, where_ref: (where_ref[1] * nblk + i, 0))),
        out_shape=pltpu.HBM((2 * half, cols), F32),
        compiler_params=_cparams(("parallel",), 32),
    )(where, own, got)


def _small_job(tiles):
    n = len(tiles)

    def copies(ins, outs, sems):
        x, y, c = _place()
        me = 4 * x + 2 * y + c
        local, send, recv = sems
        cps = []
        for t in range(n):
            cps.append(pltpu.make_async_copy(ins[t], outs[t].at[me], local.at[t]))
            for r in range(1, N_DEV):
                fx, fy, fc = (r >> 2) & 1, (r >> 1) & 1, r & 1
                cps.append(_remote(ins[t], outs[t].at[me], send.at[t, r - 1], recv.at[t, r - 1],
                                   (_flip(x, fx), _flip(y, fy), _flip(c, fc))))
        return cps

    return _Job(tiles, [jax.ShapeDtypeStruct((N_DEV,) + t.shape, F32) for t in tiles],
                [pltpu.SemaphoreType.DMA((n,)), pltpu.SemaphoreType.DMA((n, N_DEV - 1)),
                 pltpu.SemaphoreType.DMA((n, N_DEV - 1))], copies)


def _adamw_math(w, g, m, v):
    m = ADAM_B1 * m + (1.0 - ADAM_B1) * g
    v = ADAM_B2 * v + (1.0 - ADAM_B2) * (g * g)
    m_hat = m / (1.0 - ADAM_B1 ** ADAM_STEP)
    v_hat = v / (1.0 - ADAM_B2 ** ADAM_STEP)
    delta = -ADAM_LR * (m_hat / (jnp.sqrt(v_hat) + ADAM_EPS) + ADAM_WD * w)
    return delta, m, v


def _adamw(w, g, m, v, *, name, tr):
    rows, cols = w.shape
    tr = min(tr, rows)

    def body(w_ref, g_ref, m_ref, v_ref, go_ref, d_ref, nm_ref, nv_ref):
        g = g_ref[...]
        go_ref[...] = g
        d_ref[...], nm_ref[...], nv_ref[...] = _adamw_math(w_ref[...], g, m_ref[...], v_ref[...])

    spec = pl.BlockSpec((tr, cols), lambda i: (i, 0))
    return pl.pallas_call(
        body, name=name,
        grid=(rows // tr,),
        in_specs=[spec] * 4, out_specs=[spec] * 4,
        out_shape=[jax.ShapeDtypeStruct(w.shape, F32)] * 4,
        compiler_params=_cparams(("parallel",), 48),
    )(w, g, m, v)


def _small_adamw(gathered, gathered_rel, params, moments_m, moments_v):
    n = len(params)

    def body(all_ref, rel_all_ref, *refs):
        w_refs, m_refs, v_refs = refs[:n], refs[n:2 * n], refs[2 * n:3 * n]
        loss_ref = refs[3 * n]
        out_refs = refs[3 * n + 1:]
        g = all_ref[0]
        rel = rel_all_ref[0]
        for d in range(1, N_DEV):
            g = g + all_ref[d]
            rel = rel + rel_all_ref[d]
        misc = g[ROW_MISC:ROW_MISC + 1]
        loss_ref[...] = misc[:, MISC_LOSS:MISC_LOSS + 1]
        grads = (g[ROW_G1:ROW_G1 + 1], g[ROW_G2:ROW_G2 + 1], g[ROW_G3:ROW_G3 + 1], g[ROW_G4:ROW_G4 + 1],
                 misc[:, MISC_GQ:MISC_GQ + HEAD_DIM], misc[:, MISC_GK:MISC_GK + HEAD_DIM],
                 misc[:, MISC_SINK:MISC_SINK + GROUP * N_KV], rel[:, 0:GROUP * N_KV])
        for i in range(n):
            d, nm, nv = _adamw_math(w_refs[i][...], grads[i], m_refs[i][...], v_refs[i][...])
            for j, val in enumerate((grads[i], d, nm, nv)):
                out_refs[4 * i + j][...] = val

    args = (gathered, gathered_rel, *params, *moments_m, *moments_v)
    out_shape = [jax.ShapeDtypeStruct((1, 1), F32)] + [jax.ShapeDtypeStruct(p.shape, F32) for p in params
                                                       for _ in range(4)]
    outs = pl.pallas_call(
        body, name="small_adamw", grid=(1,),
        in_specs=[_whole(a) for a in args], out_specs=[_whole(o) for o in out_shape], out_shape=out_shape,
    )(*map(_from_hbm, args))
    return outs[0], [outs[1 + 4 * i:5 + 4 * i] for i in range(n)]


def kernel(x, w_in, w_o, g_pre_mix, g_post_mix, q_norm_a, k_norm_a, sink_b, rel_bias, g_pre_ffn, w_ffn_up, w_ffn_down, g_post_ffn, loss_target, m_w_in, m_w_o, m_g_pre_mix, m_g_post_mix, m_q_norm_a, m_k_norm_a, m_sink_b, m_rel_bias, m_g_pre_ffn, m_w_ffn_up, m_w_ffn_down, m_g_post_ffn, v_w_in, v_w_o, v_g_pre_mix, v_g_post_mix, v_q_norm_a, v_k_norm_a, v_sink_b, v_rel_bias, v_g_pre_ffn, v_w_ffn_up, v_w_ffn_down, v_g_post_ffn):
    batch, seq, _ = x.shape
    tokens = batch * seq
    where = jnp.stack([2 * lax.axis_index("x") + lax.axis_index("y"), lax.axis_index("c")]).astype(jnp.int32)
    x2 = x.reshape(tokens, D_MODEL)
    g1, g2, g3, g4 = g_pre_mix, g_post_mix, g_pre_ffn, g_post_ffn

    cos, sin = _rope_tables(seq)
    ck, sk = jnp.tile(cos, (1, 2)), jnp.tile(sin, (1, 2))
    gq8, gk2 = jnp.tile(q_norm_a, (1, 8)), jnp.tile(k_norm_a, (1, 2))
    bucket, band = _window_tables()
    bias = _bias_build(rel_bias.T, bucket, band)

    w_in_g, w_o_p, w_up_p, w_down_p = _gather_weights(
        (w_in[0], w_o[0], w_ffn_up[0], w_ffn_down[0]), whole=(True, False, False, False))
    (h1, raw, qa, ka, kta, va, vta, qtb, kb, ktb, vb, vtb) = _pre_proj(
        x2, g1, w_in_g, gq8, gk2, ck, sk, seq=seq, tm=min(1024, seq), sub=256)
    (oa, p_a, linv_a), (w_part,) = _attn_a_fwd(
        qa, kta, va, seq=seq, bq=min(256, seq), jobs=[_gather_job([w_o_p, w_up_p, w_down_p], forward=False)])
    kb3 = kb.reshape(tokens // BLOCK, BLOCK, KV_WIDTH)
    vb3 = vb.reshape(tokens // BLOCK, BLOCK, KV_WIDTH)
    (ob, p_b, stat_b), ((w_o_g, w_up_g, w_down_g),) = _attn_b_fwd(
        qtb, kb3, vtb, bias, sink_b, seq=seq, per_step=min(16, seq // BLOCK),
        jobs=[_gather_job(w_part, forward=True)])
    w_o2 = w_o_g.reshape(D_MODEL, D_MODEL)
    w_down2 = w_down_g.reshape(D_FF, D_MODEL)
    mix, x1, h2, o_cat, u, df, dy, loss_t, dg4 = _mix_ffn_fwd(
        oa, ob, w_o2, x2, g2, g3, w_up_g, w_down2, loss_target.reshape(tokens, D_MODEL), g4, tm=256)

    dz, dx1, dmix, dg3, dg2, doa, dob = _ffn_bwd_act(df, w_down2, u, w_up_g, x1, dy, mix, g3, g2, w_o2, tm=256)
    gw_down, _ = _tn_matmul(u, df, name="grad_w_down", tm=1024, tn=1024, tk=min(4096, tokens), square_a=True,
                            vmem_mb=56)
    gw_down = gw_down.reshape(N_CHIPS, FF_CHUNK, D_MODEL)
    gw_up, ((got_down,),) = _tn_matmul(h2, dz, name="grad_w_up", tm=1024, tn=1024, tk=min(4096, tokens), chunk=FF_CHUNK,
                                        vmem_mb=56, jobs=[_swap_job([gw_down])])
    gw_o, _ = _tn_matmul(o_cat, dmix, name="grad_w_o", tm=1024, tn=1024, tk=min(2048, tokens))
    gw_o = gw_o.reshape(N_CHIPS, O_CHUNK, D_MODEL)
    sum_down = _add_half(gw_down, got_down, where, name="add_half_w_down", tr=512)
    (dqa, dkta, dvta), ((ex_down,), (got_up,)) = _attn_a_bwd(
        qa, ka, vta, doa, oa, p_a, linv_a, seq=seq, bq=min(256, seq),
        jobs=[_exchange_job([sum_down]), _swap_job([gw_up])])
    full_down = _add_chips(sum_down, ex_down, where, name="add_chips_w_down", tr=512)
    sum_up = _add_half(gw_up, got_up, where, name="add_half_w_up", tr=512)
    (dqb, dkb, dvb, dbias, dsink), ((ex_up,), (g_down,), (got_o,)) = _attn_b_bwd(
        qtb, ktb, vb3, dob, ob, p_b, stat_b, seq=seq, per_step=min(16, seq // BLOCK),
        jobs=[_exchange_job([sum_up]), _join_job([full_down]), _swap_job([gw_o])])
    full_up = _add_chips(sum_up, ex_up, where, name="add_chips_w_up", tr=512)
    sum_o = _add_half(gw_o, got_o, where, name="add_half_w_o", tr=512)
    (grad_x, dproj, dg1, dgq, dgk), _ = _proj_bwd(
        dqa, dkta, dvta, dqb, dkb, dvb, raw, x2, dx1, g1, w_in_g, gq8, gk2, ck, sk,
        seq=seq, tm=min(512, seq), sub=128)
    packed, packed_rel = _pack_small(dg1, dg2, dg3, dg4, dgq, dgk, dsink, dbias, bucket, loss_t)
    gw_in, ((ex_o,), (g_up,), (gathered, gathered_rel)) = _tn_matmul(
        h1, dproj, name="grad_w_in", tm=1024, tn=2 * IN_CHUNK, tk=min(4096, tokens), chunk=IN_CHUNK,
        vmem_mb=56, jobs=[_exchange_job([sum_o]), _join_job([full_up]), _small_job([packed, packed_rel])])
    full_o = _add_chips(sum_o, ex_o, where, name="add_chips_w_o", tr=512)

    (g_o,), (got_in,) = _run_jobs("tail_swap", [_join_job([full_o]), _swap_job([gw_in])])
    sum_in = _add_half(gw_in, got_in, where, name="add_half_w_in", tr=512)
    ((ex_in,),) = _run_jobs("tail_exchange", [_exchange_job([sum_in])])
    full_in = _add_chips(sum_in, ex_in, where, name="add_chips_w_in", tr=512)
    ((g_in,),) = _run_jobs("tail_join", [_join_job([full_in])])

    big = [[t[None] for t in _adamw(w[0], g, m[0], v[0], name="adamw_" + nm, tr=512)] for nm, w, g, m, v in (
        ("w_in", w_in, g_in, m_w_in, v_w_in), ("w_o", w_o, g_o, m_w_o, v_w_o),
        ("w_up", w_ffn_up, g_up, m_w_ffn_up, v_w_ffn_up), ("w_down", w_ffn_down, g_down, m_w_ffn_down, v_w_ffn_down))]

    loss, small = _small_adamw(
        gathered, gathered_rel,
        (g1, g2, g3, g4, q_norm_a, k_norm_a, sink_b, rel_bias),
        (m_g_pre_mix, m_g_post_mix, m_g_pre_ffn, m_g_post_ffn, m_q_norm_a, m_k_norm_a, m_sink_b, m_rel_bias),
        (v_g_pre_mix, v_g_post_mix, v_g_pre_ffn, v_g_post_ffn, v_q_norm_a, v_k_norm_a, v_sink_b, v_rel_bias))
    s_g1, s_g2, s_g3, s_g4, s_gq, s_gk, s_sink, s_rel = small

    def leaves(i):
        return (big[0][i], big[1][i], s_g1[i], s_g2[i], s_gq[i], s_gk[i], s_sink[i], s_rel[i], s_g3[i],
                big[2][i], big[3][i], s_g4[i])

    loss = loss.reshape(())
    return (loss, grad_x.reshape(batch, seq, D_MODEL), *leaves(0), *leaves(1), *leaves(2), *leaves(3))
```

```python
import functools

import jax
import jax.numpy as jnp
import numpy as np
from jax import lax
from jax.experimental import pallas as pl
from jax.experimental.pallas import tpu as pltpu

F32 = jnp.float32
MM = jnp.bfloat16

D_MODEL = 1024
HEAD_DIM = 64
N_KV = 2
GROUP = 4
Q_WIDTH = 512
KV_WIDTH = 128
D_FF = 4096
GRID_W = 64
BLOCK = 128
N_BUCKETS = 32
MAX_DISTANCE = 128
ROPE_THETA = 10000.0
EPS = 1e-6
NEG_INF = -1e30
SCALE = HEAD_DIM ** -0.5
IN_TOTAL = 1536
N_CHIPS = 4
N_DEV = 8
IN_CHUNK = IN_TOTAL // N_CHIPS
FF_CHUNK = D_FF // N_CHIPS
O_CHUNK = D_MODEL // N_CHIPS
QK_RAW = 640

ADAM_LR = 0.001
ADAM_B1 = 0.9
ADAM_B2 = 0.999
ADAM_EPS = 1e-08
ADAM_WD = 0.01
ADAM_STEP = 10

LANES = 128
MESH = pl.DeviceIdType.MESH
HBM = pl.BlockSpec(memory_space=pl.ANY)
SMEM = pl.BlockSpec(memory_space=pltpu.SMEM)

ROW_G1, ROW_G2, ROW_G3, ROW_G4, ROW_MISC = 0, 1, 2, 3, 4
MISC_GQ, MISC_GK, MISC_SINK, MISC_LOSS = 0, 64, 128, 512


def _cparams(sem, vmem_mb):
    return pltpu.CompilerParams(dimension_semantics=sem, vmem_limit_bytes=vmem_mb * 1024 * 1024)


def _whole(a):
    return pl.BlockSpec(a.shape, lambda i: (0,) * len(a.shape))


def _from_hbm(a):
    return pltpu.with_memory_space_constraint(a, pltpu.HBM)


def _in_hbm(s):
    return pltpu.HBM(s.shape, s.dtype)


class _Job:
    def __init__(self, operands, out_shapes, sems, copies, alias=None):
        self.operands, self.out_shapes, self.sems, self.copies = list(operands), list(out_shapes), list(sems), copies
        self.alias = dict(alias or {})


def _place():
    return lax.axis_index("x"), lax.axis_index("y"), lax.axis_index("c")


_CHIP_FLIPS = ((1, 0), (0, 1), (1, 1))


def _flip(v, bit):
    return 1 - v if bit else v


def _remote(src, dst, send, recv, dev):
    return pltpu.make_async_remote_copy(src_ref=src, dst_ref=dst, send_sem=send, recv_sem=recv,
                                        device_id=dev, device_id_type=MESH)


def _swap_job(grads):
    n = len(grads)

    def copies(ins, outs, sems):
        x, y, c = _place()
        send, recv = sems
        cps = []
        for t in range(n):
            half = ins[t].shape[1] // 2
            cps.append(_remote(ins[t].at[:, pl.ds((1 - c) * half, half), :], outs[t], send.at[t], recv.at[t],
                               (x, y, 1 - c)))
        return cps

    shapes = [jax.ShapeDtypeStruct((g.shape[0], g.shape[1] // 2, g.shape[2]), F32) for g in grads]
    return _Job(grads, shapes, [pltpu.SemaphoreType.DMA((n,)), pltpu.SemaphoreType.DMA((n,))], copies)


def _exchange_job(sums):
    n = len(sums)

    def copies(ins, outs, sems):
        x, y, c = _place()
        send, recv = sems
        cps = []
        for t in range(n):
            for r, (fx, fy) in enumerate(_CHIP_FLIPS):
                kk = 2 * _flip(x, fx) + _flip(y, fy)
                cps.append(_remote(ins[t].at[kk], outs[t].at[r], send.at[t, r], recv.at[t, r],
                                   (_flip(x, fx), _flip(y, fy), c)))
        return cps

    shapes = [jax.ShapeDtypeStruct((3,) + s.shape[1:], s.dtype) for s in sums]
    return _Job(sums, shapes, [pltpu.SemaphoreType.DMA((n, 3)), pltpu.SemaphoreType.DMA((n, 3))], copies)


def _join_job(fulls):
    n = len(fulls)

    def copies(ins, outs, sems):
        x, y, c = _place()
        send, recv = sems
        cps = []
        for t in range(n):
            half = ins[t].shape[0] // 2
            rows = pl.ds(c * half, half)
            cps.append(_remote(ins[t].at[rows], outs[t].at[rows], send.at[t], recv.at[t], (x, y, 1 - c)))
        return cps

    shapes = [jax.ShapeDtypeStruct(f.shape, f.dtype) for f in fulls]
    return _Job(fulls, shapes, [pltpu.SemaphoreType.DMA((n,)), pltpu.SemaphoreType.DMA((n,))], copies,
                alias={t: t for t in range(n)})


def _gather_job(bufs, forward):
    n = len(bufs)

    def copies(ins, outs, sems):
        x, y, c = _place()
        send, recv = sems
        cps = []
        for t in range(n):
            half = ins[t].shape[1] // 2
            rows = pl.ds(c * half, half)
            for r, (fx, fy) in enumerate(_CHIP_FLIPS):
                if forward:
                    kk = 2 * _flip(x, fx) + _flip(y, fy)
                    dev = (x, y, 1 - c)
                else:
                    kk = 2 * x + y
                    dev = (_flip(x, fx), _flip(y, fy), c)
                cps.append(_remote(ins[t].at[kk, rows], outs[t].at[kk, rows], send.at[t, r], recv.at[t, r], dev))
        return cps

    shapes = [jax.ShapeDtypeStruct(b.shape, b.dtype) for b in bufs]
    return _Job(bufs, shapes, [pltpu.SemaphoreType.DMA((n, 3)), pltpu.SemaphoreType.DMA((n, 3))], copies,
                alias={t: t for t in range(n)})


def _call(body, args, *, name, grid, in_specs, out_specs, out_shape, scratch_shapes=(), params=None, jobs=()):
    n_in, n_out, n_scr = len(in_specs), len(out_specs), len(scratch_shapes)
    job_in = [len(j.operands) for j in jobs]
    job_out = [len(j.out_shapes) for j in jobs]
    job_sem = [len(j.sems) for j in jobs]

    def wrapped(*refs):
        pos = 0
        ins = refs[pos:pos + n_in]; pos += n_in
        jins = []
        for k in job_in:
            jins.append(refs[pos:pos + k]); pos += k
        outs = refs[pos:pos + n_out]; pos += n_out
        jouts = []
        for k in job_out:
            jouts.append(refs[pos:pos + k]); pos += k
        scr = refs[pos:pos + n_scr]; pos += n_scr
        jsems = []
        for k in job_sem:
            jsems.append(refs[pos:pos + k]); pos += k
        if jobs:
            ids = [pl.program_id(d) for d in range(len(grid))]
            first = functools.reduce(jnp.logical_and, [i == 0 for i in ids])
            last = functools.reduce(jnp.logical_and, [i == g - 1 for i, g in zip(ids, grid)])

            @pl.when(first)
            def _():
                for j, ji, jo, js in zip(jobs, jins, jouts, jsems):
                    for cp in j.copies(ji, jo, js):
                        cp.start()

        body(*ins, *outs, *scr)
        if jobs:
            @pl.when(last)
            def _():
                for j, ji, jo, js in zip(jobs, jins, jouts, jsems):
                    for cp in j.copies(ji, jo, js):
                        cp.wait()

    aliases = {}
    in_pos, out_pos = n_in, n_out
    for j in jobs:
        for i, o in j.alias.items():
            aliases[in_pos + i] = out_pos + o
        in_pos += len(j.operands)
        out_pos += len(j.out_shapes)
    res = pl.pallas_call(
        wrapped, name=name, grid=grid,
        in_specs=list(in_specs) + [HBM] * sum(job_in),
        out_specs=list(out_specs) + [HBM] * sum(job_out),
        out_shape=list(out_shape) + [_in_hbm(s) for j in jobs for s in j.out_shapes],
        scratch_shapes=list(scratch_shapes) + [s for j in jobs for s in j.sems],
        input_output_aliases=aliases,
        compiler_params=params,
    )(*[a if spec is SMEM else _from_hbm(a) for a, spec in zip(args, in_specs)],
      *[a for j in jobs for a in j.operands])
    own, rest = list(res[:n_out]), list(res[n_out:])
    job_res = []
    for k in job_out:
        job_res.append(rest[:k])
        rest = rest[k:]
    return own, job_res


def _run_jobs(name, jobs):
    def body():
        pass

    return _call(body, (), name=name, grid=(1,), in_specs=[], out_specs=[], out_shape=[], jobs=jobs)[1]


def _dot(a, b):
    return jnp.dot(a, b, preferred_element_type=F32)


def _dot_nt(a, b):
    return lax.dot_general(a, b, (((1,), (1,)), ((), ())), preferred_element_type=F32)


def _dot_tn(a, b):
    return lax.dot_general(a, b, (((0,), (0,)), ((), ())), preferred_element_type=F32)


def _rms_r(x):
    return lax.rsqrt(jnp.mean(x * x, axis=-1, keepdims=True) + EPS)


def _rms_bwd(x, r, g, dy):
    n = x * r
    dn = dy * g
    dx = r * (dn - n * jnp.mean(dn * n, axis=-1, keepdims=True))
    return dx, dy * n


def _seg64_sum(v):
    rows, width = v.shape
    lane = lax.broadcasted_iota(jnp.int32, (rows, LANES), 1)
    lo = lane < HEAD_DIM
    outs = []
    for c in range(width // LANES):
        ch = v[:, c * LANES:(c + 1) * LANES]
        s_lo = jnp.sum(jnp.where(lo, ch, 0.0), axis=-1, keepdims=True)
        s_hi = jnp.sum(jnp.where(lo, 0.0, ch), axis=-1, keepdims=True)
        outs.append(jnp.where(lo, s_lo, s_hi))
    return outs[0] if len(outs) == 1 else jnp.concatenate(outs, axis=-1)


def _head_r(v):
    return lax.rsqrt(_seg64_sum(v * v) * (1.0 / HEAD_DIM) + EPS)


def _swap16(ch):
    lane = lax.broadcasted_iota(jnp.int32, ch.shape, 1)
    return jnp.where((lane % 32) < 16, pltpu.roll(ch, LANES - 16, 1), pltpu.roll(ch, 16, 1))


def _by_chunk(fn, v):
    outs = [fn(v[:, c * LANES:(c + 1) * LANES]) for c in range(v.shape[1] // LANES)]
    return outs[0] if len(outs) == 1 else jnp.concatenate(outs, axis=-1)


def _rope(v, cos, sin_signed):
    return _by_chunk(lambda ch: ch * cos + _swap16(ch) * sin_signed, v)


def _rope_t(g, cos, sin_signed):
    return _by_chunk(lambda ch: ch * cos + _swap16(ch * sin_signed), g)


def _rope_tables(seq):
    nf = HEAD_DIM // 4
    freqs = ROPE_THETA ** (-jnp.arange(nf, dtype=F32) / nf)
    pos = jnp.arange(seq, dtype=jnp.int32)
    row = (pos // GRID_W).astype(F32)
    col = (pos % GRID_W).astype(F32)
    ang_r = row[:, None] * freqs[None, :]
    ang_c = col[:, None] * freqs[None, :]
    cr, sr, cc, sc = jnp.cos(ang_r), jnp.sin(ang_r), jnp.cos(ang_c), jnp.sin(ang_c)
    cos = jnp.concatenate([cr, cr, cc, cc], axis=1)
    sin = jnp.concatenate([-sr, sr, -sc, sc], axis=1)
    return cos, sin


def _t5_bucket(rel):
    nb = N_BUCKETS // 2
    ret = (rel > 0).astype(jnp.int32) * nb
    n = jnp.abs(rel)
    max_exact = nb // 2
    nf = jnp.maximum(n, 1).astype(jnp.float32)
    large = max_exact + (jnp.log(nf / max_exact) / np.float32(np.log(MAX_DISTANCE / max_exact))
                         * (nb - max_exact)).astype(jnp.int32)
    large = jnp.minimum(large, nb - 1)
    return ret + jnp.where(n < max_exact, n, large)


def _window_tables():
    a = jnp.arange(BLOCK, dtype=jnp.int32)
    c = jnp.arange(3 * BLOCK, dtype=jnp.int32)
    rel = c[None, :] - BLOCK - a[:, None]
    bucket = _t5_bucket(rel)
    band = (jnp.abs(rel) <= BLOCK).astype(jnp.int32)
    to3 = lambda t: t.reshape(BLOCK, 3, BLOCK).transpose(1, 2, 0)
    return to3(bucket), to3(band)


def _pre_proj(x, g1, w_in, gq, gk, ck, sk, *, seq, tm, sub):
    tokens = x.shape[0]
    n_seq = seq // tm
    nblk = tm // BLOCK
    batch = tokens // seq

    def body(x_ref, g1_ref, w_ref, gq_ref, gk_ref, ck_ref, sk_ref,
             h1_ref, raw_ref, qa_ref, ka_ref, kta_ref, va_ref, vta_ref,
             qtb_ref, kb_ref, ktb_ref, vb_ref, vtb_ref, proj):
        for r in range(tm // sub):
            rows = slice(r * sub, (r + 1) * sub)
            xv = x_ref[rows, :]
            h = (xv * _rms_r(xv) * g1_ref[...]).astype(MM)
            h1_ref[rows, :] = h
            for j in range(N_CHIPS):
                proj[rows, j * IN_CHUNK:(j + 1) * IN_CHUNK] = _dot(h, w_ref[j])
            qa = proj[rows, 0:Q_WIDTH]
            ka = proj[rows, Q_WIDTH:QK_RAW]
            raw_ref[rows, :] = proj[rows, 0:QK_RAW]
            qn = qa * _head_r(qa) * gq_ref[...]
            qa_ref[rows, :] = (_rope(qn, ck_ref[rows, :], sk_ref[rows, :]) * SCALE).astype(MM)
            kn = ka * _head_r(ka) * gk_ref[...]
            kr = _rope(kn, ck_ref[rows, :], sk_ref[rows, :])
            ka_ref[rows, :] = kr.astype(MM)
            kta_ref[0, :, rows] = kr.T.astype(MM)
            va = proj[rows, 640:768]
            va_ref[rows, :] = va.astype(MM)
            vta_ref[0, :, rows] = va.T.astype(MM)
            qb = proj[rows, 768:1280] * SCALE
            kb = proj[rows, 1280:1408]
            vb = proj[rows, 1408:1536]
            kb_ref[rows, :] = kb.astype(MM)
            vb_ref[rows, :] = vb.astype(MM)
            for j in range(sub // BLOCK):
                blk = slice(j * BLOCK, (j + 1) * BLOCK)
                qtb_ref[r * (sub // BLOCK) + j] = qb[blk, :].T.astype(MM)
                ktb_ref[r * (sub // BLOCK) + j] = kb[blk, :].T.astype(MM)
                vtb_ref[r * (sub // BLOCK) + j] = vb[blk, :].T.astype(MM)

    tok = lambda w: pl.BlockSpec((tm, w), lambda i: (i, 0))
    tab = lambda w: pl.BlockSpec((tm, w), lambda i: (i % n_seq, 0))
    row = lambda w: pl.BlockSpec((1, w), lambda i: (0, 0))
    tposed = pl.BlockSpec((1, LANES, tm), lambda i: (i // n_seq, 0, i % n_seq))
    blocks = pl.BlockSpec((nblk, BLOCK, LANES), lambda i: (i, 0, 0))
    qblocks = pl.BlockSpec((nblk, Q_WIDTH, BLOCK), lambda i: (i, 0, 0))
    tok_mm = lambda w: jax.ShapeDtypeStruct((tokens, w), MM)
    return pl.pallas_call(
        body, name="pre_proj",
        grid=(tokens // tm,),
        in_specs=[tok(D_MODEL), row(D_MODEL),
                  pl.BlockSpec((N_CHIPS, D_MODEL, IN_CHUNK), lambda i: (0, 0, 0)),
                  row(Q_WIDTH), row(KV_WIDTH), tab(KV_WIDTH), tab(KV_WIDTH)],
        out_specs=[tok(D_MODEL), tok(QK_RAW), tok(Q_WIDTH), tok(KV_WIDTH), tposed, tok(KV_WIDTH), tposed,
                   qblocks, tok(KV_WIDTH), blocks, tok(KV_WIDTH), blocks],
        out_shape=[
            tok_mm(D_MODEL),
            jax.ShapeDtypeStruct((tokens, QK_RAW), F32),
            tok_mm(Q_WIDTH),
            tok_mm(KV_WIDTH),
            jax.ShapeDtypeStruct((batch, KV_WIDTH, seq), MM),
            tok_mm(KV_WIDTH),
            jax.ShapeDtypeStruct((batch, KV_WIDTH, seq), MM),
            jax.ShapeDtypeStruct((tokens // BLOCK, Q_WIDTH, BLOCK), MM),
            tok_mm(KV_WIDTH),
            jax.ShapeDtypeStruct((tokens // BLOCK, KV_WIDTH, BLOCK), MM),
            tok_mm(KV_WIDTH),
            jax.ShapeDtypeStruct((tokens // BLOCK, KV_WIDTH, BLOCK), MM),
        ],
        scratch_shapes=[pltpu.VMEM((tm, IN_TOTAL), F32)],
        compiler_params=_cparams(("parallel",), 48),
    )(*map(_from_hbm, (x, g1, w_in, gq, gk, ck, sk)))


def _kv_half(v2, kv):
    return jnp.where(kv == 0, v2[:, :HEAD_DIM], v2[:, HEAD_DIM:])


def _attn_a_fwd(qa, kta, va, *, seq, bq, jobs=()):
    tokens = qa.shape[0]
    batch = tokens // seq
    nq = seq // bq

    def body(q_ref, kt_ref, v_ref, o_ref, p_ref, linv_ref):
        kv = pl.program_id(1)
        kt = kt_ref[0]
        lane = lax.broadcasted_iota(jnp.int32, (seq, KV_WIDTH), 1)
        v = jnp.where((lane < HEAD_DIM) == (kv == 0), v_ref[...], jnp.ones((), MM))
        q_rows = jnp.concatenate([q_ref[:, g * HEAD_DIM:(g + 1) * HEAD_DIM] for g in range(GROUP)], axis=0)
        s_all = _dot(q_rows, kt)
        for g in range(GROUP):
            sl = slice(g * HEAD_DIM, (g + 1) * HEAD_DIM)
            s = s_all[g * bq:(g + 1) * bq]
            pb = jnp.exp((s - jnp.max(s, axis=-1, keepdims=True)).astype(MM))
            p_ref[0, g] = pb
            o2 = _dot(pb, v)
            linv = 1.0 / _kv_half(o2, 1 - kv)[:, 0:1]
            o_ref[:, sl] = _kv_half(o2, kv) * linv
            linv_ref[0, :, g:g + 1] = linv

    return _call(
        body, (qa, kta, va), name="attn_a_fwd", jobs=jobs,
        grid=(batch, N_KV, nq),
        in_specs=[pl.BlockSpec((bq, GROUP * HEAD_DIM), lambda b, k, i: (b * nq + i, k)),
                  pl.BlockSpec((1, HEAD_DIM, seq), lambda b, k, i: (b, k, 0)),
                  pl.BlockSpec((seq, KV_WIDTH), lambda b, k, i: (b, 0))],
        out_specs=[pl.BlockSpec((bq, GROUP * HEAD_DIM), lambda b, k, i: (b * nq + i, k)),
                   pl.BlockSpec((1, GROUP, bq, seq), lambda b, k, i: (k, 0, b * nq + i, 0)),
                   pl.BlockSpec((1, bq, GROUP), lambda b, k, i: (k, b * nq + i, 0))],
        out_shape=[jax.ShapeDtypeStruct((tokens, Q_WIDTH), F32),
                   jax.ShapeDtypeStruct((N_KV, GROUP, tokens, seq), MM),
                   jax.ShapeDtypeStruct((N_KV, tokens, GROUP), F32)],
        params=_cparams(("arbitrary", "arbitrary", "arbitrary"), 56))


def _attn_a_bwd(qa, ka, vta, do, o, p, linv, *, seq, bq, jobs=()):
    tokens = qa.shape[0]
    batch = tokens // seq
    nq = seq // bq

    def body(q_ref, k_ref, vt_ref, do_ref, o_ref, p_ref, linv_ref, dq_ref, dkt_ref, dvt_ref):
        kv = pl.program_id(1)

        @pl.when(pl.program_id(2) == 0)
        def _():
            dkt_ref[...] = jnp.zeros_like(dkt_ref)
            dvt_ref[...] = jnp.zeros_like(dvt_ref)

        vt = vt_ref[0]
        k2 = k_ref[...]
        for g in range(GROUP):
            sl = slice(g * HEAD_DIM, (g + 1) * HEAD_DIM)
            dof = do_ref[:, sl]
            delta = jnp.sum(dof * o_ref[:, sl], axis=-1, keepdims=True)
            linv_g = linv_ref[0, :, g:g + 1]
            pb = p_ref[0, g]
            dp = _dot(dof.astype(MM), vt)
            ds = pb * ((dp - delta) * linv_g).astype(MM)
            dq_ref[:, sl] = _kv_half(_dot(ds, k2), kv)
            dkt_ref[0] += _dot_tn(q_ref[:, sl], ds)
            dvt_ref[0] += _dot_tn((dof * linv_g).astype(MM), pb)

    qspec = pl.BlockSpec((bq, GROUP * HEAD_DIM), lambda b, k, i: (b * nq + i, k))
    tspec = pl.BlockSpec((1, HEAD_DIM, seq), lambda b, k, i: (b, k, 0))
    return _call(
        body, (qa, ka, vta, do, o, p, linv), name="attn_a_bwd", jobs=jobs,
        grid=(batch, N_KV, nq),
        in_specs=[qspec, pl.BlockSpec((seq, KV_WIDTH), lambda b, k, i: (b, 0)), tspec, qspec, qspec,
                  pl.BlockSpec((1, GROUP, bq, seq), lambda b, k, i: (k, 0, b * nq + i, 0)),
                  pl.BlockSpec((1, bq, GROUP), lambda b, k, i: (k, b * nq + i, 0))],
        out_specs=[qspec, tspec, tspec],
        out_shape=[jax.ShapeDtypeStruct((tokens, Q_WIDTH), F32),
                   jax.ShapeDtypeStruct((batch, KV_WIDTH, seq), F32),
                   jax.ShapeDtypeStruct((batch, KV_WIDTH, seq), F32)],
        params=_cparams(("arbitrary", "arbitrary", "arbitrary"), 56))


def _bias_build(rel_bias_t, bucket_t, band_t):
    def body(tab_ref, bucket_ref, band_ref, bias_ref):
        for h in range(GROUP * N_KV):
            for piece in range(3):
                bk = bucket_ref[piece]
                acc = jnp.zeros((BLOCK, BLOCK), F32)
                for b in range(N_BUCKETS):
                    acc = jnp.where(bk == b, tab_ref[h, b], acc)
                g = h % GROUP
                bias_ref[h // GROUP, piece, :, g * BLOCK:(g + 1) * BLOCK] = jnp.where(band_ref[piece] != 0, acc, NEG_INF)

    out = jax.ShapeDtypeStruct((N_KV, 3, BLOCK, GROUP * BLOCK), F32)
    return pl.pallas_call(
        body, name="bias_build", grid=(1,),
        in_specs=[SMEM, _whole(bucket_t), _whole(band_t)], out_specs=_whole(out), out_shape=out,
    )(rel_bias_t, bucket_t, band_t)


def _pad_heads(t, kv):
    outs = []
    for g in range(GROUP):
        tg = t[g * HEAD_DIM:(g + 1) * HEAD_DIM, :]
        zero = jnp.zeros_like(tg)
        outs.append(jnp.concatenate([jnp.where(kv == 0, tg, zero), jnp.where(kv == 0, zero, tg)], axis=0))
    return jnp.concatenate(outs, axis=-1)


def _unpad_heads(t, kv):
    outs = [_kv_half(t[:, g * BLOCK:(g + 1) * BLOCK].T, kv) for g in range(GROUP)]
    return jnp.concatenate(outs, axis=-1)


def _sink_row(sink_ref, kv):
    lane_head = lax.broadcasted_iota(jnp.int32, (1, GROUP * BLOCK), 1) // BLOCK
    row = jnp.zeros((1, GROUP * BLOCK), F32)
    for g in range(GROUP):
        row = jnp.where(lane_head == g, sink_ref[0, kv * GROUP + g], row)
    return row


def _rows3(ref, idx):
    return jnp.concatenate([ref[i] for i in idx], axis=0)


def _lanes3(ref, idx):
    return jnp.concatenate([ref[i] for i in idx], axis=-1)


def _window_scores_t(k_ref, idx, qpad, bias_ref, n, nblk):
    s_all = _dot(_rows3(k_ref, idx), qpad)
    pieces = []
    for piece in range(3):
        s = s_all[piece * BLOCK:(piece + 1) * BLOCK] + bias_ref[0, piece]
        if piece == 0:
            s = jnp.where(n > 0, s, NEG_INF)
        if piece == 2:
            s = jnp.where(n < nblk - 1, s, NEG_INF)
        pieces.append(s)
    return pieces


def _attn_b_fwd(qtb, kb3, vtb, bias, sink, *, seq, per_step, jobs=()):
    nblk_all = qtb.shape[0]
    tokens = nblk_all * BLOCK
    batch = tokens // seq
    nblk = seq // BLOCK
    nstep = nblk // per_step

    def body(sink_ref, q_ref, k_ref, vt_ref, bias_ref, o_ref, p_ref, stat_ref):
        kv = pl.program_id(0)
        first = pl.program_id(2) * per_step
        sink_row = _sink_row(sink_ref, kv)
        stat_row = lax.broadcasted_iota(jnp.int32, (8, GROUP * BLOCK), 0)

        def block(i, carry):
            n = first + i
            idx = (jnp.maximum(n - 1, 0), n, jnp.minimum(n + 1, nblk - 1))
            rows = slice(i * BLOCK, (i + 1) * BLOCK)
            qpad = _pad_heads(q_ref[n], kv)
            ss = _window_scores_t(k_ref, idx, qpad, bias_ref, n, nblk)
            m = jnp.maximum(jnp.maximum(jnp.max(ss[0], axis=0, keepdims=True),
                                        jnp.max(ss[1], axis=0, keepdims=True)),
                            jnp.maximum(jnp.max(ss[2], axis=0, keepdims=True), sink_row))
            ps = [jnp.exp(s - m) for s in ss]
            e_sink = jnp.exp(sink_row - m)
            rinv = 1.0 / (jnp.sum(ps[0], axis=0, keepdims=True) + jnp.sum(ps[1], axis=0, keepdims=True)
                          + jnp.sum(ps[2], axis=0, keepdims=True) + e_sink)
            pbs = [p.astype(MM) for p in ps]
            for piece in range(3):
                p_ref[0, i, piece] = pbs[piece]
            ot = _dot(_lanes3(vt_ref, idx), jnp.concatenate(pbs, axis=0))
            o_ref[rows, :] = _unpad_heads(ot * rinv, kv)
            stat_ref[0, i] = jnp.where(stat_row == 0, rinv, e_sink * rinv)
            return carry

        for i in range(per_step):
            block(i, 0)

    both = pl.BlockSpec((nblk, BLOCK, KV_WIDTH), lambda k, b, j: (b, 0, 0))
    return _call(
        body, (sink, qtb, kb3, vtb, bias), name="attn_b_fwd", jobs=jobs,
        grid=(N_KV, batch, nstep),
        in_specs=[SMEM, pl.BlockSpec((nblk, GROUP * HEAD_DIM, BLOCK), lambda k, b, j: (b, k, 0)), both, both,
                  pl.BlockSpec((1, 3, BLOCK, GROUP * BLOCK), lambda k, b, j: (k, 0, 0, 0))],
        out_specs=[pl.BlockSpec((per_step * BLOCK, GROUP * HEAD_DIM), lambda k, b, j: (b * nstep + j, k)),
                   pl.BlockSpec((1, per_step, 3, BLOCK, GROUP * BLOCK), lambda k, b, j: (k, b * nstep + j, 0, 0, 0)),
                   pl.BlockSpec((1, per_step, 8, GROUP * BLOCK), lambda k, b, j: (k, b * nstep + j, 0, 0))],
        out_shape=[jax.ShapeDtypeStruct((tokens, Q_WIDTH), F32),
                   jax.ShapeDtypeStruct((N_KV, nblk_all, 3, BLOCK, GROUP * BLOCK), MM),
                   jax.ShapeDtypeStruct((N_KV, nblk_all, 8, GROUP * BLOCK), F32)],
        params=_cparams(("arbitrary", "arbitrary", "arbitrary"), 48))


def _attn_b_bwd(qtb, ktb, vb3, do, o, p, stat, *, seq, per_step, jobs=()):
    nblk_all = qtb.shape[0]
    tokens = nblk_all * BLOCK
    batch = tokens // seq
    nblk = seq // BLOCK
    nstep = nblk // per_step

    def body(q_ref, kt_ref, v_ref, do_ref, o_ref, p_ref, stat_ref,
             dq_ref, dk_ref, dv_ref, dbias_ref, dsink_ref):
        kv = pl.program_id(0)
        step = pl.program_id(2)
        first = step * per_step

        @pl.when(jnp.logical_and(pl.program_id(1) == 0, step == 0))
        def _():
            dbias_ref[...] = jnp.zeros_like(dbias_ref)
            dsink_ref[...] = jnp.zeros_like(dsink_ref)

        @pl.when(step == 0)
        def _():
            dk_ref[...] = jnp.zeros_like(dk_ref)
            dv_ref[...] = jnp.zeros_like(dv_ref)

        def block(i, dsink):
            n = first + i
            idx = (jnp.maximum(n - 1, 0), n, jnp.minimum(n + 1, nblk - 1))
            rows = slice(i * BLOCK, (i + 1) * BLOCK)
            qpad = _pad_heads(q_ref[n], kv)
            dot_t = do_ref[rows, :].T
            prod = dot_t * o_ref[rows, :].T
            delta = jnp.concatenate(
                [jnp.sum(prod[g * HEAD_DIM:(g + 1) * HEAD_DIM, :], axis=0, keepdims=True) for g in range(GROUP)],
                axis=-1)
            stats = stat_ref[0, i]
            rinv, p_sink = stats[0:1, :], stats[1:2, :]
            dopad32 = _pad_heads(dot_t, kv)
            dopad = dopad32.astype(MM)
            dopad_n = (dopad32 * rinv).astype(MM)
            dpt = _dot(_rows3(v_ref, idx), dopad)
            pbs, dsbs = [], []
            for piece in range(3):
                pb = p_ref[0, i, piece]
                dst = pb.astype(F32) * ((dpt[piece * BLOCK:(piece + 1) * BLOCK] - delta) * rinv)
                dbias_ref[0, piece] += dst
                pbs.append(pb)
                dsbs.append(dst.astype(MM))
            ds_cat = jnp.concatenate(dsbs, axis=0)
            dq_ref[rows, :] = _unpad_heads(_dot(_lanes3(kt_ref, idx), ds_cat), kv)
            dk_all = _dot_nt(ds_cat, qpad)
            dv_all = _dot_nt(jnp.concatenate(pbs, axis=0), dopad_n)
            for piece in range(3):
                dk_ref[0, idx[piece]] += dk_all[piece * BLOCK:(piece + 1) * BLOCK]
                dv_ref[0, idx[piece]] += dv_all[piece * BLOCK:(piece + 1) * BLOCK]
            return dsink - p_sink * delta

        dsink = jnp.zeros((1, GROUP * BLOCK), F32)
        for i in range(per_step):
            dsink = block(i, dsink)
        dsink_ref[0] += jnp.broadcast_to(dsink, (8, GROUP * BLOCK))

    qspec = pl.BlockSpec((per_step * BLOCK, GROUP * HEAD_DIM), lambda k, b, j: (b * nstep + j, k))
    both = pl.BlockSpec((nblk, BLOCK, KV_WIDTH), lambda k, b, j: (b, 0, 0))
    grad =pl.BlockSpec((1, nblk, BLOCK, KV_WIDTH), lambda k, b, j: (k, b, 0, 0))
    return _call(
        body, (qtb, ktb, vb3, do, o, p, stat), name="attn_b_bwd", jobs=jobs,
        grid=(N_KV, batch, nstep),
        in_specs=[pl.BlockSpec((nblk, GROUP * HEAD_DIM, BLOCK), lambda k, b, j: (b, k, 0)), both, both,
                  qspec, qspec,
                  pl.BlockSpec((1, per_step, 3, BLOCK, GROUP * BLOCK), lambda k, b, j: (k, b * nstep + j, 0, 0, 0)),
                  pl.BlockSpec((1, per_step, 8, GROUP * BLOCK), lambda k, b, j: (k, b * nstep + j, 0, 0))],
        out_specs=[qspec, grad, grad,
                   pl.BlockSpec((1, 3, BLOCK, GROUP * BLOCK), lambda k, b, j: (k, 0, 0, 0)),
                   pl.BlockSpec((1, 8, GROUP * BLOCK), lambda k, b, j: (k, 0, 0))],
        out_shape=[jax.ShapeDtypeStruct((tokens, Q_WIDTH), F32),
                   jax.ShapeDtypeStruct((N_KV, nblk_all, BLOCK, KV_WIDTH), F32),
                   jax.ShapeDtypeStruct((N_KV, nblk_all, BLOCK, KV_WIDTH), F32),
                   jax.ShapeDtypeStruct((N_KV, 3, BLOCK, GROUP * BLOCK), F32),
                   jax.ShapeDtypeStruct((N_KV, 8, GROUP * BLOCK), F32)],
        params=_cparams(("arbitrary", "arbitrary", "arbitrary"), 56))


def _resident(shape):
    return pl.BlockSpec(shape, lambda i: (0,) * len(shape), pipeline_mode=pl.Buffered(1))


def _mix_ffn_fwd(oa, ob, w_o, x, g2, g3, w_up, w_down, target, g4, *, tm):
    tokens = x.shape[0]
    nt = tokens // tm

    def body(oa_ref, ob_ref, wo_ref, x_ref, g2_ref, g3_ref, wu_ref, wd_ref, t_ref, g4_ref,
             mix_ref, x1_ref, h2_ref, o_ref, u_ref, df_ref, dy_ref, loss_ref, dg4_ref):
        o = jnp.concatenate([oa_ref[...].astype(MM), ob_ref[...].astype(MM)], axis=-1)
        o_ref[...] = o
        mix = _dot(o, wo_ref[...])
        mix_ref[...] = mix
        x1 = x_ref[...] + mix * _rms_r(mix) * g2_ref[...]
        x1_ref[...] = x1
        h2v = (x1 * _rms_r(x1) * g3_ref[...]).astype(MM)
        h2_ref[...] = h2v
        f = jnp.zeros((tm, D_MODEL), F32)
        for c in range(N_CHIPS):
            u = jnp.maximum(_dot(h2v, wu_ref[c]), 0.0)
            u_ref[:, c * FF_CHUNK:(c + 1) * FF_CHUNK] = u.astype(MM)
            f = f + _dot((u * u).astype(MM), wd_ref[c * FF_CHUNK:(c + 1) * FF_CHUNK, :])
        r = _rms_r(f)
        g4v = g4_ref[...]
        err = x1 + f * r * g4v - t_ref[...]
        sq = jnp.sum(err * err, axis=-1, keepdims=True)
        loss_ref[0] = jnp.broadcast_to(jnp.sum(sq, axis=0, keepdims=True) * (0.5 / D_MODEL), (8, LANES))
        dy = err * (1.0 / D_MODEL)
        dy_ref[...] = dy
        dfv, dgv = _rms_bwd(f, r, g4v, dy)
        df_ref[...] = dfv.astype(MM)
        dg4_ref[0] = jnp.sum(dgv, axis=0, keepdims=True)

    tok = pl.BlockSpec((tm, D_MODEL), lambda i: (i, 0))
    half = pl.BlockSpec((tm, Q_WIDTH), lambda i: (i, 0))
    row = pl.BlockSpec((1, D_MODEL), lambda i: (0, 0))
    tok_f32 = jax.ShapeDtypeStruct((tokens, D_MODEL), F32)
    tok_mm = jax.ShapeDtypeStruct((tokens, D_MODEL), MM)
    return pl.pallas_call(
        body, name="mix_ffn_fwd",
        grid=(nt,),
        in_specs=[half, half, _resident((D_MODEL, D_MODEL)), tok, row, row,
                  _resident((N_CHIPS, D_MODEL, FF_CHUNK)), _resident((D_FF, D_MODEL)), tok, row],
        out_specs=[tok, tok, tok, tok, pl.BlockSpec((tm, D_FF), lambda i: (i, 0)), tok, tok,
                   pl.BlockSpec((1, 8, LANES), lambda i: (i, 0, 0)),
                   pl.BlockSpec((1, 1, D_MODEL), lambda i: (i, 0, 0))],
        out_shape=[tok_f32,
                   tok_f32,
                   tok_mm,
                   tok_mm,
                   jax.ShapeDtypeStruct((tokens, D_FF), MM),
                   tok_mm,
                   tok_f32,
                   jax.ShapeDtypeStruct((nt, 8, LANES), F32),
                   jax.ShapeDtypeStruct((nt, 1, D_MODEL), F32)],
        compiler_params=_cparams(("parallel",), 56),
    )(*map(_from_hbm, (oa, ob, w_o, x, g2, g3, w_up, w_down, target, g4)))


def _ffn_bwd_act(df, w_down, u, w_up, x1, dy, mix, g3, g2, w_o, *, tm):
    tokens = df.shape[0]
    nt = tokens // tm

    def body(df_ref, wd_ref, u_ref, wu_ref, x1_ref, dy_ref, mix_ref, g3_ref, g2_ref, wo_ref,
             dz_ref, dx1_ref, dmix_ref, dg3_ref, dg2_ref, doa_ref, dob_ref):
        dfv = df_ref[...]
        dh2 = jnp.zeros((tm, D_MODEL), F32)
        for c in range(N_CHIPS):
            cols = slice(c * FF_CHUNK, (c + 1) * FF_CHUNK)
            da = _dot_nt(dfv, wd_ref[cols, :])
            dz = (da * (2.0 * u_ref[:, cols].astype(F32))).astype(MM)
            dz_ref[:, cols] = dz
            dh2 = dh2 + _dot_nt(dz, wu_ref[c])
        x1 = x1_ref[...]
        dxn, dg3v = _rms_bwd(x1, _rms_r(x1), g3_ref[...], dh2)
        dx1 = dy_ref[...] + dxn
        dx1_ref[...] = dx1
        dg3_ref[0] = jnp.sum(dg3v, axis=0, keepdims=True)
        mix = mix_ref[...]
        dmix, dg2v = _rms_bwd(mix, _rms_r(mix), g2_ref[...], dx1)
        dmb = dmix.astype(MM)
        dmix_ref[...] = dmb
        dg2_ref[0] = jnp.sum(dg2v, axis=0, keepdims=True)
        doa_ref[...] = _dot_nt(dmb, wo_ref[0:Q_WIDTH, :])
        dob_ref[...] = _dot_nt(dmb, wo_ref[Q_WIDTH:D_MODEL, :])

    tok = pl.BlockSpec((tm, D_MODEL), lambda i: (i, 0))
    half = pl.BlockSpec((tm, Q_WIDTH), lambda i: (i, 0))
    wide = pl.BlockSpec((tm, D_FF), lambda i: (i, 0))
    row = pl.BlockSpec((1, D_MODEL), lambda i: (0, 0))
    part = pl.BlockSpec((1, 1, D_MODEL), lambda i: (i, 0, 0))
    return pl.pallas_call(
        body, name="ffn_bwd_act",
        grid=(nt,),
        in_specs=[tok, _resident((D_FF, D_MODEL)), wide, _resident((N_CHIPS, D_MODEL, FF_CHUNK)),
                  tok, tok, tok, row, row, _resident((D_MODEL, D_MODEL))],
        out_specs=[wide, tok, tok, part, part, half, half],
        out_shape=[jax.ShapeDtypeStruct((tokens, D_FF), MM),
                   jax.ShapeDtypeStruct((tokens, D_MODEL), F32),
                   jax.ShapeDtypeStruct((tokens, D_MODEL), MM),
                   jax.ShapeDtypeStruct((nt, 1, D_MODEL), F32),
                   jax.ShapeDtypeStruct((nt, 1, D_MODEL), F32),
                   jax.ShapeDtypeStruct((tokens, Q_WIDTH), F32),
                   jax.ShapeDtypeStruct((tokens, Q_WIDTH), F32)],
        compiler_params=_cparams(("parallel",), 56),
    )(*map(_from_hbm, (df, w_down, u, w_up, x1, dy, mix, g3, g2, w_o)))


def _tn_matmul(a, b, *, name, tm, tn, tk, chunk=None, square_a=False, vmem_mb=48, jobs=()):
    tokens, m_dim = a.shape
    n_dim = b.shape[1]
    chunked = chunk is not None
    if chunked:
        assert tm == m_dim and tn % chunk == 0

    def body(a_ref, b_ref, o_ref):
        av = a_ref[...]
        if square_a:
            av = av.astype(F32)
            av = av * av
        part = _dot_tn(av.astype(MM), b_ref[...].astype(MM))
        if chunked:
            part = jnp.stack([part[:, c * chunk:(c + 1) * chunk] for c in range(tn // chunk)])

        @pl.when(pl.program_id(2) == 0)
        def _():
            o_ref[...] = part

        @pl.when(pl.program_id(2) > 0)
        def _():
            o_ref[...] += part

    if chunked:
        out_spec = pl.BlockSpec((tn // chunk, tm, chunk), lambda i, j, k: (j, 0, 0))
        out_shape = jax.ShapeDtypeStruct((n_dim // chunk, m_dim, chunk), F32)
    else:
        out_spec = pl.BlockSpec((tm, tn), lambda i, j, k: (i, j))
        out_shape = jax.ShapeDtypeStruct((m_dim, n_dim), F32)
    (out,), job_res = _call(
        body, (a, b), name=name, jobs=jobs,
        grid=(m_dim // tm, n_dim // tn, tokens // tk),
        in_specs=[pl.BlockSpec((tk, tm), lambda i, j, k: (k, i)),
                  pl.BlockSpec((tk, tn), lambda i, j, k: (k, j))],
        out_specs=[out_spec], out_shape=[_in_hbm(out_shape)],
        params=_cparams(("arbitrary", "arbitrary", "arbitrary"), vmem_mb))
    return out, job_res


def _proj_bwd(dqa, dkta, dvta, dqb, dktb, dvtb, raw, x, dx1, g1, w_in, gq, gk, ck, sk, *, seq, tm, sub, jobs=()):
    tokens = x.shape[0]
    nt = tokens // tm
    n_seq = seq // tm
    nblk = tm // BLOCK

    def body(dqa_ref, dkta_ref, dvta_ref, dqb_ref, dkb_ref, dvb_ref, raw_ref, x_ref, dx1_ref, g1_ref, w_ref,
             gq_ref, gk_ref, ck_ref, sk_ref,
             gx_ref, dproj_ref, dg1_ref, dgq_ref, dgk_ref, dp):
        parts = []
        for r in range(tm // sub):
            rows = slice(r * sub, (r + 1) * sub)
            qa = raw_ref[rows, 0:Q_WIDTH]
            dqn = _rope_t(dqa_ref[rows, :], ck_ref[rows, :], sk_ref[rows, :]) * SCALE
            rq = _head_r(qa)
            nq = qa * rq
            dnq = dqn * gq_ref[...]
            dp[rows, 0:Q_WIDTH] = rq * (dnq - nq * (_seg64_sum(dnq * nq) * (1.0 / HEAD_DIM)))

            ka = raw_ref[rows, Q_WIDTH:QK_RAW]
            dkn = _rope_t(dkta_ref[0, :, rows].T, ck_ref[rows, :], sk_ref[rows, :])
            rk = _head_r(ka)
            nk = ka * rk
            dnk = dkn * gk_ref[...]
            dp[rows, 512:640] = rk * (dnk - nk * (_seg64_sum(dnk * nk) * (1.0 / HEAD_DIM)))

            dp[rows, 640:768] = dvta_ref[0, :, rows].T
            dp[rows, 768:1280] = dqb_ref[rows, :] * SCALE
            for j in range(r * sub // BLOCK, (r + 1) * sub // BLOCK):
                dp[j * BLOCK:(j + 1) * BLOCK, 1280:1408] = dkb_ref[0, j] + dkb_ref[1, j]
                dp[j * BLOCK:(j + 1) * BLOCK, 1408:1536] = dvb_ref[0, j] + dvb_ref[1, j]

            dproj = dp[rows, :].astype(MM)
            dproj_ref[rows, :] = dproj
            dh1 = _dot_nt(dproj[:, 0:IN_CHUNK], w_ref[0])
            for j in range(1, N_CHIPS):
                dh1 = dh1 + _dot_nt(dproj[:, j * IN_CHUNK:(j + 1) * IN_CHUNK], w_ref[j])
            xv = x_ref[rows, :]
            dxn, dg1v = _rms_bwd(xv, _rms_r(xv), g1_ref[...], dh1)
            gx_ref[rows, :] = dx1_ref[rows, :] + dxn
            parts.append((jnp.sum(dqn * nq, axis=0, keepdims=True), jnp.sum(dkn * nk, axis=0, keepdims=True),
                          jnp.sum(dg1v, axis=0, keepdims=True)))
        dgq_ref[0] = functools.reduce(jnp.add, [p[0] for p in parts])
        dgk_ref[0] = functools.reduce(jnp.add, [p[1] for p in parts])
        dg1_ref[0] = functools.reduce(jnp.add, [p[2] for p in parts])

    tok = lambda w: pl.BlockSpec((tm, w), lambda i: (i, 0))
    tab = lambda w: pl.BlockSpec((tm, w), lambda i: (i % n_seq, 0))
    row = lambda w: pl.BlockSpec((1, w), lambda i: (0, 0))
    tposed = pl.BlockSpec((1, KV_WIDTH, tm), lambda i: (i // n_seq, 0, i % n_seq))
    blocks = pl.BlockSpec((N_KV, nblk, BLOCK, KV_WIDTH), lambda i: (0, i, 0, 0))
    part = lambda w: pl.BlockSpec((1, 1, w), lambda i: (i, 0, 0))
    return _call(
        body, (dqa, dkta, dvta, dqb, dktb, dvtb, raw, x, dx1, g1, w_in, gq, gk, ck, sk),
        name="proj_bwd", jobs=jobs,
        grid=(nt,),
        in_specs=[tok(Q_WIDTH), tposed, tposed, tok(Q_WIDTH), blocks, blocks, tok(QK_RAW), tok(D_MODEL),
                  tok(D_MODEL), row(D_MODEL),
                  pl.BlockSpec((N_CHIPS, D_MODEL, IN_CHUNK), lambda i: (0, 0, 0)),
                  row(Q_WIDTH), row(KV_WIDTH), tab(KV_WIDTH), tab(KV_WIDTH)],
        out_specs=[tok(D_MODEL), tok(IN_TOTAL), part(D_MODEL), part(Q_WIDTH), part(KV_WIDTH)],
        out_shape=[jax.ShapeDtypeStruct((tokens, D_MODEL), F32),
                   jax.ShapeDtypeStruct((tokens, IN_TOTAL), MM),
                   jax.ShapeDtypeStruct((nt, 1, D_MODEL), F32),
                   jax.ShapeDtypeStruct((nt, 1, Q_WIDTH), F32),
                   jax.ShapeDtypeStruct((nt, 1, KV_WIDTH), F32)],
        scratch_shapes=[pltpu.VMEM((tm, IN_TOTAL), F32)],
        params=_cparams(("arbitrary",), 56))


def _pack_small(dg1, dg2, dg3, dg4, dgq, dgk, dsink, dbias, bucket, loss):
    def body(dg1_ref, dg2_ref, dg3_ref, dg4_ref, dgq_ref, dgk_ref, dsink_ref, dbias_ref, bucket_ref, loss_ref,
             out_ref, rel_ref):
        out_ref[...] = jnp.zeros_like(out_ref)
        for r, ref in ((ROW_G1, dg1_ref), (ROW_G2, dg2_ref), (ROW_G3, dg3_ref), (ROW_G4, dg4_ref)):
            acc = ref[0]
            for t in range(1, ref.shape[0]):
                acc = acc + ref[t]
            out_ref[r:r + 1, :] = acc

        def fold(ref, heads):
            acc = ref[0]
            for t in range(1, ref.shape[0]):
                acc = acc + ref[t]
            tot = acc[:, 0:HEAD_DIM]
            for h in range(1, heads):
                tot = tot + acc[:, h * HEAD_DIM:(h + 1) * HEAD_DIM]
            return tot

        out_ref[ROW_MISC:ROW_MISC + 1, MISC_GQ:MISC_GQ + HEAD_DIM] = fold(dgq_ref, GROUP * N_KV)
        out_ref[ROW_MISC:ROW_MISC + 1, MISC_GK:MISC_GK + HEAD_DIM] = fold(dgk_ref, N_KV)
        for h in range(GROUP * N_KV):
            g = h % GROUP
            out_ref[ROW_MISC:ROW_MISC + 1, MISC_SINK + h:MISC_SINK + h + 1] = jnp.sum(
                dsink_ref[h // GROUP, 0:1, g * BLOCK:(g + 1) * BLOCK], axis=-1, keepdims=True)
        lacc = loss_ref[0, 0:1, 0:1]
        for t in range(1, loss_ref.shape[0]):
            lacc = lacc + loss_ref[t, 0:1, 0:1]
        out_ref[ROW_MISC:ROW_MISC + 1, MISC_LOSS:MISC_LOSS + 1] = lacc
        lane = lax.broadcasted_iota(jnp.int32, (N_BUCKETS, LANES), 1)
        row = lax.broadcasted_iota(jnp.int32, (N_BUCKETS, LANES), 0)

        def per_bucket(b, acc):
            for h in range(GROUP * N_KV):
                g = h % GROUP
                sel = jnp.zeros((BLOCK, BLOCK), F32)
                for piece in range(3):
                    sel = sel + jnp.where(bucket_ref[piece] == b,
                                          dbias_ref[h // GROUP, piece, :, g * BLOCK:(g + 1) * BLOCK], 0.0)
                tot = jnp.sum(jnp.sum(sel, axis=0, keepdims=True), axis=-1, keepdims=True)
                acc = jnp.where((row == b) & (lane == h), tot, acc)
            return acc

        rel_ref[...] = lax.fori_loop(0, N_BUCKETS, per_bucket, jnp.zeros((N_BUCKETS, LANES), F32))

    args = (dg1, dg2, dg3, dg4, dgq, dgk, dsink, dbias, bucket, loss)
    outs = [jax.ShapeDtypeStruct((8, D_MODEL), F32), jax.ShapeDtypeStruct((N_BUCKETS, LANES), F32)]
    return pl.pallas_call(
        body, name="pack_small", grid=(1,),
        in_specs=[_whole(a) for a in args], out_specs=[_whole(o) for o in outs], out_shape=outs,
        compiler_params=pltpu.CompilerParams(vmem_limit_bytes=32 * 1024 * 1024),
    )(*map(_from_hbm, args))


def _gather_weights(shards, whole):
    n = len(shards)
    full = [t for t in range(n) if whole[t]]

    def body(*refs):
        ins, outs = refs[:n], refs[n:2 * n]
        raw, stage = refs[2 * n:3 * n], refs[3 * n:4 * n]
        load_sem, local_sem, ici_send, ici_recv, d2d_send, d2d_recv = refs[4 * n:]
        x, y, c = _place()
        k = 2 * x + y
        sibling = (x, y, 1 - c)
        order = full + [t for t in range(n) if t not in full]
        loads = {t: pltpu.make_async_copy(ins[t], raw[t], load_sem.at[t]) for t in order}
        for t in order:
            loads[t].start()
        copies, sends = [], []
        for t in order:
            loads[t].wait()
            stage[t][...] = raw[t][...].astype(MM)
            mine = pltpu.make_async_copy(stage[t], outs[t].at[k], local_sem.at[t])
            mine.start()
            copies.append(mine)
            if t in full:
                half = ins[t].shape[0] // 2
                rows = pl.ds(c * half, half)
                for r, (fx, fy) in enumerate(_CHIP_FLIPS):
                    cp = _remote(stage[t].at[rows], outs[t].at[k, rows], ici_send.at[t, r], ici_recv.at[t, r],
                                 (_flip(x, fx), _flip(y, fy), c))
                    cp.start()
                    sends.append(cp)
        for t in full:
            half = ins[t].shape[0] // 2
            rows = pl.ds(c * half, half)
            for r, (fx, fy) in enumerate(_CHIP_FLIPS):
                kk = 2 * _flip(x, fx) + _flip(y, fy)
                landed = outs[t].at[kk, rows]
                _remote(landed, landed, ici_send.at[t, r], ici_recv.at[t, r], sibling).wait_recv()
                fwd = _remote(landed, landed, d2d_send.at[t, r], d2d_recv.at[t, r], sibling)
                fwd.start()
                sends.append(fwd)
        for t in full:
            half = ins[t].shape[0] // 2
            other = pl.ds((1 - c) * half, half)
            for r, (fx, fy) in enumerate(_CHIP_FLIPS):
                kk = 2 * _flip(x, fx) + _flip(y, fy)
                theirs = outs[t].at[kk, other]
                _remote(theirs, theirs, d2d_send.at[t, r], d2d_recv.at[t, r], sibling).wait_recv()
        for cp in sends:
            cp.wait_send()
        for cp in copies:
            cp.wait()

    return pl.pallas_call(
        body, name="gather_weights",
        in_specs=[HBM] * n, out_specs=[HBM] * n,
        out_shape=[pltpu.HBM((N_CHIPS,) + s.shape, MM) for s in shards],
        scratch_shapes=[pltpu.VMEM(s.shape, F32) for s in shards] + [pltpu.VMEM(s.shape, MM) for s in shards] + [
            pltpu.SemaphoreType.DMA((n,)), pltpu.SemaphoreType.DMA((n,)),
            pltpu.SemaphoreType.DMA((n, 3)), pltpu.SemaphoreType.DMA((n, 3)),
            pltpu.SemaphoreType.DMA((n, 3)), pltpu.SemaphoreType.DMA((n, 3))],
        compiler_params=pltpu.CompilerParams(vmem_limit_bytes=40 * 1024 * 1024),
    )(*shards)


def _add_half(grad, got, where, *, name, tr):
    nch, half, cols = got.shape
    tr = min(tr, half)
    nblk = half // tr

    def body(where_ref, g_ref, r_ref, o_ref):
        o_ref[...] = (g_ref[...] + r_ref[...]).astype(MM)

    return pl.pallas_call(
        body, name=name,
        grid_spec=pltpu.PrefetchScalarGridSpec(
            num_scalar_prefetch=1, grid=(nch, nblk),
            in_specs=[pl.BlockSpec((1, tr, cols), lambda j, i, where_ref: (j, where_ref[1] * nblk + i, 0)),
                      pl.BlockSpec((1, tr, cols), lambda j, i, where_ref: (j, i, 0))],
            out_specs=pl.BlockSpec((1, tr, cols), lambda j, i, where_ref: (j, i, 0))),
        out_shape=jax.ShapeDtypeStruct(got.shape, MM),
        compiler_params=_cparams(("parallel", "parallel"), 32),
    )(where, grad, got)


def _add_chips(own, got, where, *, name, tr):
    _, half, cols = own.shape
    tr = min(tr, half)
    nblk = half // tr

    def body(where_ref, o_ref, g_ref, out_ref):
        f = lambda v: v.astype(F32)
        out_ref[...] = ((f(o_ref[0]) + f(g_ref[0])) + f(g_ref[1])) + f(g_ref[2])

    return pl.pallas_call(
        body, name=name,
        grid_spec=pltpu.PrefetchScalarGridSpec(
            num_scalar_prefetch=1, grid=(nblk,),
            in_specs=[pl.BlockSpec((1, tr, cols), lambda i, where_ref: (where_ref[0], i, 0)),
                      pl.BlockSpec((3, tr, cols), lambda i, where_ref: (0, i, 0))],
            out_specs=pl.BlockSpec((tr, cols), lambda i, where_ref: (where_ref[1] * nblk + i, 0))),
        out_shape=pltpu.HBM((2 * half, cols), F32),
        compiler_params=_cparams(("parallel",), 32),
    )(where, own, got)


def _small_job(tiles):
    n = len(tiles)

    def copies(ins, outs, sems):
        x, y, c = _place()
        me = 4 * x + 2 * y + c
        local, send, recv = sems
        cps = []
        for t in range(n):
            cps.append(pltpu.make_async_copy(ins[t], outs[t].at[me], local.at[t]))
            for r in range(1, N_DEV):
                fx, fy, fc = (r >> 2) & 1, (r >> 1) & 1, r & 1
                cps.append(_remote(ins[t], outs[t].at[me], send.at[t, r - 1], recv.at[t, r - 1],
                                   (_flip(x, fx), _flip(y, fy), _flip(c, fc))))
        return cps

    return _Job(tiles, [jax.ShapeDtypeStruct((N_DEV,) + t.shape, F32) for t in tiles],
                [pltpu.SemaphoreType.DMA((n,)), pltpu.SemaphoreType.DMA((n, N_DEV - 1)),
                 pltpu.SemaphoreType.DMA((n, N_DEV - 1))], copies)


def _adamw_math(w, g, m, v):
    m = ADAM_B1 * m + (1.0 - ADAM_B1) * g
    v = ADAM_B2 * v + (1.0 - ADAM_B2) * (g * g)
    m_hat = m / (1.0 - ADAM_B1 ** ADAM_STEP)
    v_hat = v / (1.0 - ADAM_B2 ** ADAM_STEP)
    delta = -ADAM_LR * (m_hat / (jnp.sqrt(v_hat) + ADAM_EPS) + ADAM_WD * w)
    return delta, m, v


def _adamw(w, g, m, v, *, name, tr):
    rows, cols = w.shape
    tr = min(tr, rows)

    def body(w_ref, g_ref, m_ref, v_ref, go_ref, d_ref, nm_ref, nv_ref):
        g = g_ref[...]
        go_ref[...] = g
        d_ref[...], nm_ref[...], nv_ref[...] = _adamw_math(w_ref[...], g, m_ref[...], v_ref[...])

    spec = pl.BlockSpec((tr, cols), lambda i: (i, 0))
    return pl.pallas_call(
        body, name=name,
        grid=(rows // tr,),
        in_specs=[spec] * 4, out_specs=[spec] * 4,
        out_shape=[jax.ShapeDtypeStruct(w.shape, F32)] * 4,
        compiler_params=_cparams(("parallel",), 48),
    )(w, g, m, v)


def _small_adamw(gathered, gathered_rel, params, moments_m, moments_v):
    n = len(params)

    def body(all_ref, rel_all_ref, *refs):
        w_refs, m_refs, v_refs = refs[:n], refs[n:2 * n], refs[2 * n:3 * n]
        loss_ref = refs[3 * n]
        out_refs = refs[3 * n + 1:]
        g = all_ref[0]
        rel = rel_all_ref[0]
        for d in range(1, N_DEV):
            g = g + all_ref[d]
            rel = rel + rel_all_ref[d]
        misc = g[ROW_MISC:ROW_MISC + 1]
        loss_ref[...] = misc[:, MISC_LOSS:MISC_LOSS + 1]
        grads = (g[ROW_G1:ROW_G1 + 1], g[ROW_G2:ROW_G2 + 1], g[ROW_G3:ROW_G3 + 1], g[ROW_G4:ROW_G4 + 1],
                 misc[:, MISC_GQ:MISC_GQ + HEAD_DIM], misc[:, MISC_GK:MISC_GK + HEAD_DIM],
                 misc[:, MISC_SINK:MISC_SINK + GROUP * N_KV], rel[:, 0:GROUP * N_KV])
        for i in range(n):
            d, nm, nv = _adamw_math(w_refs[i][...], grads[i], m_refs[i][...], v_refs[i][...])
            for j, val in enumerate((grads[i], d, nm, nv)):
                out_refs[4 * i + j][...] = val

    args = (gathered, gathered_rel, *params, *moments_m, *moments_v)
    out_shape = [jax.ShapeDtypeStruct((1, 1), F32)] + [jax.ShapeDtypeStruct(p.shape, F32) for p in params
                                                       for _ in range(4)]
    outs = pl.pallas_call(
        body, name="small_adamw", grid=(1,),
        in_specs=[_whole(a) for a in args], out_specs=[_whole(o) for o in out_shape], out_shape=out_shape,
    )(*map(_from_hbm, args))
    return outs[0], [outs[1 + 4 * i:5 + 4 * i] for i in range(n)]


def kernel(x, w_in, w_o, g_pre_mix, g_post_mix, q_norm_a, k_norm_a, sink_b, rel_bias, g_pre_ffn, w_ffn_up, w_ffn_down, g_post_ffn, loss_target, m_w_in, m_w_o, m_g_pre_mix, m_g_post_mix, m_q_norm_a, m_k_norm_a, m_sink_b, m_rel_bias, m_g_pre_ffn, m_w_ffn_up, m_w_ffn_down, m_g_post_ffn, v_w_in, v_w_o, v_g_pre_mix, v_g_post_mix, v_q_norm_a, v_k_norm_a, v_sink_b, v_rel_bias, v_g_pre_ffn, v_w_ffn_up, v_w_ffn_down, v_g_post_ffn):
    batch, seq, _ = x.shape
    tokens = batch * seq
    where = jnp.stack([2 * lax.axis_index("x") + lax.axis_index("y"), lax.axis_index("c")]).astype(jnp.int32)
    x2 = x.reshape(tokens, D_MODEL)
    g1, g2, g3, g4 = g_pre_mix, g_post_mix, g_pre_ffn, g_post_ffn

    cos, sin = _rope_tables(seq)
    ck, sk = jnp.tile(cos, (1, 2)), jnp.tile(sin, (1, 2))
    gq8, gk2 = jnp.tile(q_norm_a, (1, 8)), jnp.tile(k_norm_a, (1, 2))
    bucket, band = _window_tables()
    bias = _bias_build(rel_bias.T, bucket, band)

    w_in_g, w_o_p, w_up_p, w_down_p = _gather_weights(
        (w_in[0], w_o[0], w_ffn_up[0], w_ffn_down[0]), whole=(True, False, False, False))
    (h1, raw, qa, ka, kta, va, vta, qtb, kb, ktb, vb, vtb) = _pre_proj(
        x2, g1, w_in_g, gq8, gk2, ck, sk, seq=seq, tm=min(1024, seq), sub=256)
    (oa, p_a, linv_a), (w_part,) = _attn_a_fwd(
        qa, kta, va, seq=seq, bq=min(256, seq), jobs=[_gather_job([w_o_p, w_up_p, w_down_p], forward=False)])
    kb3 = kb.reshape(tokens // BLOCK, BLOCK, KV_WIDTH)
    vb3 = vb.reshape(tokens // BLOCK, BLOCK, KV_WIDTH)
    (ob, p_b, stat_b), ((w_o_g, w_up_g, w_down_g),) = _attn_b_fwd(
        qtb, kb3, vtb, bias, sink_b, seq=seq, per_step=min(16, seq // BLOCK),
        jobs=[_gather_job(w_part, forward=True)])
    w_o2 = w_o_g.reshape(D_MODEL, D_MODEL)
    w_down2 = w_down_g.reshape(D_FF, D_MODEL)
    mix, x1, h2, o_cat, u, df, dy, loss_t, dg4 = _mix_ffn_fwd(
        oa, ob, w_o2, x2, g2, g3, w_up_g, w_down2, loss_target.reshape(tokens, D_MODEL), g4, tm=256)

    dz, dx1, dmix, dg3, dg2, doa, dob = _ffn_bwd_act(df, w_down2, u, w_up_g, x1, dy, mix, g3, g2, w_o2, tm=256)
    gw_down, _ = _tn_matmul(u, df, name="grad_w_down", tm=1024, tn=1024, tk=min(4096, tokens), square_a=True,
                            vmem_mb=56)
    gw_down = gw_down.reshape(N_CHIPS, FF_CHUNK, D_MODEL)
    gw_up, ((got_down,),) = _tn_matmul(h2, dz, name="grad_w_up", tm=1024, tn=1024, tk=min(4096, tokens), chunk=FF_CHUNK,
                                        vmem_mb=56, jobs=[_swap_job([gw_down])])
    gw_o, _ = _tn_matmul(o_cat, dmix, name="grad_w_o", tm=1024, tn=1024, tk=min(2048, tokens))
    gw_o = gw_o.reshape(N_CHIPS, O_CHUNK, D_MODEL)
    sum_down = _add_half(gw_down, got_down, where, name="add_half_w_down", tr=512)
    (dqa, dkta, dvta), ((ex_down,), (got_up,)) = _attn_a_bwd(
        qa, ka, vta, doa, oa, p_a, linv_a, seq=seq, bq=min(256, seq),
        jobs=[_exchange_job([sum_down]), _swap_job([gw_up])])
    full_down = _add_chips(sum_down, ex_down, where, name="add_chips_w_down", tr=512)
    sum_up = _add_half(gw_up, got_up, where, name="add_half_w_up", tr=512)
    (dqb, dkb, dvb, dbias, dsink), ((ex_up,), (g_down,), (got_o,)) = _attn_b_bwd(
        qtb, ktb, vb3, dob, ob, p_b, stat_b, seq=seq, per_step=min(16, seq // BLOCK),
        jobs=[_exchange_job([sum_up]), _join_job([full_down]), _swap_job([gw_o])])
    full_up = _add_chips(sum_up, ex_up, where, name="add_chips_w_up", tr=512)
    sum_o = _add_half(gw_o, got_o, where, name="add_half_w_o", tr=512)
    (grad_x, dproj, dg1, dgq, dgk), _ = _proj_bwd(
        dqa, dkta, dvta, dqb, dkb, dvb, raw, x2, dx1, g1, w_in_g, gq8, gk2, ck, sk,
        seq=seq, tm=min(512, seq), sub=128)
    packed, packed_rel = _pack_small(dg1, dg2, dg3, dg4, dgq, dgk, dsink, dbias, bucket, loss_t)
    gw_in, ((ex_o,), (g_up,), (gathered, gathered_rel)) = _tn_matmul(
        h1, dproj, name="grad_w_in", tm=1024, tn=2 * IN_CHUNK, tk=min(4096, tokens), chunk=IN_CHUNK,
        vmem_mb=56, jobs=[_exchange_job([sum_o]), _join_job([full_up]), _small_job([packed, packed_rel])])
    full_o = _add_chips(sum_o, ex_o, where, name="add_chips_w_o", tr=512)

    (g_o,), (got_in,) = _run_jobs("tail_swap", [_join_job([full_o]), _swap_job([gw_in])])
    sum_in = _add_half(gw_in, got_in, where, name="add_half_w_in", tr=512)
    ((ex_in,),) = _run_jobs("tail_exchange", [_exchange_job([sum_in])])
    full_in = _add_chips(sum_in, ex_in, where, name="add_chips_w_in", tr=512)
    ((g_in,),) = _run_jobs("tail_join", [_join_job([full_in])])

    big = [[t[None] for t in _adamw(w[0], g, m[0], v[0], name="adamw_" + nm, tr=512)] for nm, w, g, m, v in (
        ("w_in", w_in, g_in, m_w_in, v_w_in), ("w_o", w_o, g_o, m_w_o, v_w_o),
        ("w_up", w_ffn_up, g_up, m_w_ffn_up, v_w_ffn_up), ("w_down", w_ffn_down, g_down, m_w_ffn_down, v_w_ffn_down))]

    loss, small = _small_adamw(
        gathered, gathered_rel,
        (g1, g2, g3, g4, q_norm_a, k_norm_a, sink_b, rel_bias),
        (m_g_pre_mix, m_g_post_mix, m_g_pre_ffn, m_g_post_ffn, m_q_norm_a, m_k_norm_a, m_sink_b, m_rel_bias),
        (v_g_pre_mix, v_g_post_mix, v_g_pre_ffn, v_g_post_ffn, v_q_norm_a, v_k_norm_a, v_sink_b, v_rel_bias))
    s_g1, s_g2, s_g3, s_g4, s_gq, s_gk, s_sink, s_rel = small

    def leaves(i):
        return (big[0][i], big[1][i], s_g1[i], s_g2[i], s_gq[i], s_gk[i], s_sink[i], s_rel[i], s_g3[i],
                big[2][i], big[3][i], s_g4[i])

    loss = loss.reshape(())
    return (loss, grad_x.reshape(batch, seq, D_MODEL), *leaves(0), *leaves(1), *leaves(2), *leaves(3))
```

```python
import functools

import jax
import jax.numpy as jnp
import numpy as np
from jax import lax
from jax.experimental import pallas as pl
from jax.experimental.pallas import tpu as pltpu

F32 = jnp.float32
MM = jnp.bfloat16

D_MODEL = 1024
HEAD_DIM = 64
N_KV = 2
GROUP = 4
Q_WIDTH = 512
KV_WIDTH = 128
D_FF = 4096
GRID_W = 64
BLOCK = 128
N_BUCKETS = 32
MAX_DISTANCE = 128
ROPE_THETA = 10000.0
EPS = 1e-6
NEG_INF = -1e30
SCALE = HEAD_DIM ** -0.5
IN_TOTAL = 1536
N_CHIPS = 4
N_DEV = 8
IN_CHUNK = IN_TOTAL // N_CHIPS
FF_CHUNK = D_FF // N_CHIPS
O_CHUNK = D_MODEL // N_CHIPS
QK_RAW = 640

ADAM_LR = 0.001
ADAM_B1 = 0.9
ADAM_B2 = 0.999
ADAM_EPS = 1e-08
ADAM_WD = 0.01
ADAM_STEP = 10

LANES = 128
MESH = pl.DeviceIdType.MESH
HBM = pl.BlockSpec(memory_space=pl.ANY)
SMEM = pl.BlockSpec(memory_space=pltpu.SMEM)

ROW_G1, ROW_G2, ROW_G3, ROW_G4, ROW_MISC = 0, 1, 2, 3, 4
MISC_GQ, MISC_GK, MISC_SINK, MISC_LOSS = 0, 64, 128, 512


def _cparams(sem, vmem_mb):
    return pltpu.CompilerParams(dimension_semantics=sem, vmem_limit_bytes=vmem_mb * 1024 * 1024)


def _whole(a):
    return pl.BlockSpec(a.shape, lambda i: (0,) * len(a.shape))


def _from_hbm(a):
    return pltpu.with_memory_space_constraint(a, pltpu.HBM)


def _in_hbm(s):
    return pltpu.HBM(s.shape, s.dtype)


class _Job:
    def __init__(self, operands, out_shapes, sems, copies, alias=None):
        self.operands, self.out_shapes, self.sems, self.copies = list(operands), list(out_shapes), list(sems), copies
        self.alias = dict(alias or {})


def _place():
    return lax.axis_index("x"), lax.axis_index("y"), lax.axis_index("c")


_CHIP_FLIPS = ((1, 0), (0, 1), (1, 1))


def _flip(v, bit):
    return 1 - v if bit else v


def _remote(src, dst, send, recv, dev):
    return pltpu.make_async_remote_copy(src_ref=src, dst_ref=dst, send_sem=send, recv_sem=recv,
                                        device_id=dev, device_id_type=MESH)


def _swap_job(grads):
    n = len(grads)

    def copies(ins, outs, sems):
        x, y, c = _place()
        send, recv = sems
        cps = []
        for t in range(n):
            half = ins[t].shape[1] // 2
            cps.append(_remote(ins[t].at[:, pl.ds((1 - c) * half, half), :], outs[t], send.at[t], recv.at[t],
                               (x, y, 1 - c)))
        return cps

    shapes = [jax.ShapeDtypeStruct((g.shape[0], g.shape[1] // 2, g.shape[2]), F32) for g in grads]
    return _Job(grads, shapes, [pltpu.SemaphoreType.DMA((n,)), pltpu.SemaphoreType.DMA((n,))], copies)


def _exchange_job(sums):
    n = len(sums)

    def copies(ins, outs, sems):
        x, y, c = _place()
        send, recv = sems
        cps = []
        for t in range(n):
            for r, (fx, fy) in enumerate(_CHIP_FLIPS):
                kk = 2 * _flip(x, fx) + _flip(y, fy)
                cps.append(_remote(ins[t].at[kk], outs[t].at[r], send.at[t, r], recv.at[t, r],
                                   (_flip(x, fx), _flip(y, fy), c)))
        return cps

    shapes = [jax.ShapeDtypeStruct((3,) + s.shape[1:], s.dtype) for s in sums]
    return _Job(sums, shapes, [pltpu.SemaphoreType.DMA((n, 3)), pltpu.SemaphoreType.DMA((n, 3))], copies)


def _join_job(fulls):
    n = len(fulls)

    def copies(ins, outs, sems):
        x, y, c = _place()
        send, recv = sems
        cps = []
        for t in range(n):
            half = ins[t].shape[0] // 2
            rows = pl.ds(c * half, half)
            cps.append(_remote(ins[t].at[rows], outs[t].at[rows], send.at[t], recv.at[t], (x, y, 1 - c)))
        return cps

    shapes = [jax.ShapeDtypeStruct(f.shape, f.dtype) for f in fulls]
    return _Job(fulls, shapes, [pltpu.SemaphoreType.DMA((n,)), pltpu.SemaphoreType.DMA((n,))], copies,
                alias={t: t for t in range(n)})


def _gather_job(bufs, forward):
    n = len(bufs)

    def copies(ins, outs, sems):
        x, y, c = _place()
        send, recv = sems
        cps = []
        for t in range(n):
            half = ins[t].shape[1] // 2
            rows = pl.ds(c * half, half)
            for r, (fx, fy) in enumerate(_CHIP_FLIPS):
                if forward:
                    kk = 2 * _flip(x, fx) + _flip(y, fy)
                    dev = (x, y, 1 - c)
                else:
                    kk = 2 * x + y
                    dev = (_flip(x, fx), _flip(y, fy), c)
                cps.append(_remote(ins[t].at[kk, rows], outs[t].at[kk, rows], send.at[t, r], recv.at[t, r], dev))
        return cps

    shapes = [jax.ShapeDtypeStruct(b.shape, b.dtype) for b in bufs]
    return _Job(bufs, shapes, [pltpu.SemaphoreType.DMA((n, 3)), pltpu.SemaphoreType.DMA((n, 3))], copies,
                alias={t: t for t in range(n)})


def _call(body, args, *, name, grid, in_specs, out_specs, out_shape, scratch_shapes=(), params=None, jobs=()):
    n_in, n_out, n_scr = len(in_specs), len(out_specs), len(scratch_shapes)
    job_in = [len(j.operands) for j in jobs]
    job_out = [len(j.out_shapes) for j in jobs]
    job_sem = [len(j.sems) for j in jobs]

    def wrapped(*refs):
        pos = 0
        ins = refs[pos:pos + n_in]; pos += n_in
        jins = []
        for k in job_in:
            jins.append(refs[pos:pos + k]); pos += k
        outs = refs[pos:pos + n_out]; pos += n_out
        jouts = []
        for k in job_out:
            jouts.append(refs[pos:pos + k]); pos += k
        scr = refs[pos:pos + n_scr]; pos += n_scr
        jsems = []
        for k in job_sem:
            jsems.append(refs[pos:pos + k]); pos += k
        if jobs:
            ids = [pl.program_id(d) for d in range(len(grid))]
            first = functools.reduce(jnp.logical_and, [i == 0 for i in ids])
            last = functools.reduce(jnp.logical_and, [i == g - 1 for i, g in zip(ids, grid)])

            @pl.when(first)
            def _():
                for j, ji, jo, js in zip(jobs, jins, jouts, jsems):
                    for cp in j.copies(ji, jo, js):
                        cp.start()

        body(*ins, *outs, *scr)
        if jobs:
            @pl.when(last)
            def _():
                for j, ji, jo, js in zip(jobs, jins, jouts, jsems):
                    for cp in j.copies(ji, jo, js):
                        cp.wait()

    aliases = {}
    in_pos, out_pos = n_in, n_out
    for j in jobs:
        for i, o in j.alias.items():
            aliases[in_pos + i] = out_pos + o
        in_pos += len(j.operands)
        out_pos += len(j.out_shapes)
    res = pl.pallas_call(
        wrapped, name=name, grid=grid,
        in_specs=list(in_specs) + [HBM] * sum(job_in),
        out_specs=list(out_specs) + [HBM] * sum(job_out),
        out_shape=list(out_shape) + [_in_hbm(s) for j in jobs for s in j.out_shapes],
        scratch_shapes=list(scratch_shapes) + [s for j in jobs for s in j.sems],
        input_output_aliases=aliases,
        compiler_params=params,
    )(*[a if spec is SMEM else _from_hbm(a) for a, spec in zip(args, in_specs)],
      *[a for j in jobs for a in j.operands])
    own, rest = list(res[:n_out]), list(res[n_out:])
    job_res = []
    for k in job_out:
        job_res.append(rest[:k])
        rest = rest[k:]
    return own, job_res


def _run_jobs(name, jobs):
    def body():
        pass

    return _call(body, (), name=name, grid=(1,), in_specs=[], out_specs=[], out_shape=[], jobs=jobs)[1]


def _dot(a, b):
    return jnp.dot(a, b, preferred_element_type=F32)


def _dot_nt(a, b):
    return lax.dot_general(a, b, (((1,), (1,)), ((), ())), preferred_element_type=F32)


def _dot_tn(a, b):
    return lax.dot_general(a, b, (((0,), (0,)), ((), ())), preferred_element_type=F32)


def _rms_r(x):
    return lax.rsqrt(jnp.mean(x * x, axis=-1, keepdims=True) + EPS)


def _rms_bwd(x, r, g, dy):
    n = x * r
    dn = dy * g
    dx = r * (dn - n * jnp.mean(dn * n, axis=-1, keepdims=True))
    return dx, dy * n


def _seg64_sum(v):
    rows, width = v.shape
    lane = lax.broadcasted_iota(jnp.int32, (rows, LANES), 1)
    lo = lane < HEAD_DIM
    outs = []
    for c in range(width // LANES):
        ch = v[:, c * LANES:(c + 1) * LANES]
        s_lo = jnp.sum(jnp.where(lo, ch, 0.0), axis=-1, keepdims=True)
        s_hi = jnp.sum(jnp.where(lo, 0.0, ch), axis=-1, keepdims=True)
        outs.append(jnp.where(lo, s_lo, s_hi))
    return outs[0] if len(outs) == 1 else jnp.concatenate(outs, axis=-1)


def _head_r(v):
    return lax.rsqrt(_seg64_sum(v * v) * (1.0 / HEAD_DIM) + EPS)


def _swap16(ch):
    lane = lax.broadcasted_iota(jnp.int32, ch.shape, 1)
    return jnp.where((lane % 32) < 16, pltpu.roll(ch, LANES - 16, 1), pltpu.roll(ch, 16, 1))


def _by_chunk(fn, v):
    outs = [fn(v[:, c * LANES:(c + 1) * LANES]) for c in range(v.shape[1] // LANES)]
    return outs[0] if len(outs) == 1 else jnp.concatenate(outs, axis=-1)


def _rope(v, cos, sin_signed):
    return _by_chunk(lambda ch: ch * cos + _swap16(ch) * sin_signed, v)


def _rope_t(g, cos, sin_signed):
    return _by_chunk(lambda ch: ch * cos + _swap16(ch * sin_signed), g)


def _rope_tables(seq):
    nf = HEAD_DIM // 4
    freqs = ROPE_THETA ** (-jnp.arange(nf, dtype=F32) / nf)
    pos = jnp.arange(seq, dtype=jnp.int32)
    row = (pos // GRID_W).astype(F32)
    col = (pos % GRID_W).astype(F32)
    ang_r = row[:, None] * freqs[None, :]
    ang_c = col[:, None] * freqs[None, :]
    cr, sr, cc, sc = jnp.cos(ang_r), jnp.sin(ang_r), jnp.cos(ang_c), jnp.sin(ang_c)
    cos = jnp.concatenate([cr, cr, cc, cc], axis=1)
    sin = jnp.concatenate([-sr, sr, -sc, sc], axis=1)
    return cos, sin


def _t5_bucket(rel):
    nb = N_BUCKETS // 2
    ret = (rel > 0).astype(jnp.int32) * nb
    n = jnp.abs(rel)
    max_exact = nb // 2
    nf = jnp.maximum(n, 1).astype(jnp.float32)
    large = max_exact + (jnp.log(nf / max_exact) / np.float32(np.log(MAX_DISTANCE / max_exact))
                         * (nb - max_exact)).astype(jnp.int32)
    large = jnp.minimum(large, nb - 1)
    return ret + jnp.where(n < max_exact, n, large)


def _window_tables():
    a = jnp.arange(BLOCK, dtype=jnp.int32)
    c = jnp.arange(3 * BLOCK, dtype=jnp.int32)
    rel = c[None, :] - BLOCK - a[:, None]
    bucket = _t5_bucket(rel)
    band = (jnp.abs(rel) <= BLOCK).astype(jnp.int32)
    to3 = lambda t: t.reshape(BLOCK, 3, BLOCK).transpose(1, 2, 0)
    return to3(bucket), to3(band)


def _pre_proj(x, g1, w_in, gq, gk, ck, sk, *, seq, tm, sub):
    tokens = x.shape[0]
    n_seq = seq // tm
    nblk = tm // BLOCK
    batch = tokens // seq

    def body(x_ref, g1_ref, w_ref, gq_ref, gk_ref, ck_ref, sk_ref,
             h1_ref, raw_ref, qa_ref, ka_ref, kta_ref, va_ref, vta_ref,
             qtb_ref, kb_ref, ktb_ref, vb_ref, vtb_ref, proj):
        for r in range(tm // sub):
            rows = slice(r * sub, (r + 1) * sub)
            xv = x_ref[rows, :]
            h = (xv * _rms_r(xv) * g1_ref[...]).astype(MM)
            h1_ref[rows, :] = h
            for j in range(N_CHIPS):
                proj[rows, j * IN_CHUNK:(j + 1) * IN_CHUNK] = _dot(h, w_ref[j])
            qa = proj[rows, 0:Q_WIDTH]
            ka = proj[rows, Q_WIDTH:QK_RAW]
            raw_ref[rows, :] = proj[rows, 0:QK_RAW]
            qn = qa * _head_r(qa) * gq_ref[...]
            qa_ref[rows, :] = (_rope(qn, ck_ref[rows, :], sk_ref[rows, :]) * SCALE).astype(MM)
            kn = ka * _head_r(ka) * gk_ref[...]
            kr = _rope(kn, ck_ref[rows, :], sk_ref[rows, :])
            ka_ref[rows, :] = kr.astype(MM)
            kta_ref[0, :, rows] = kr.T.astype(MM)
            va = proj[rows, 640:768]
            va_ref[rows, :] = va.astype(MM)
            vta_ref[0, :, rows] = va.T.astype(MM)
            qb = proj[rows, 768:1280] * SCALE
            kb = proj[rows, 1280:1408]
            vb = proj[rows, 1408:1536]
            kb_ref[rows, :] = kb.astype(MM)
            vb_ref[rows, :] = vb.astype(MM)
            for j in range(sub // BLOCK):
                blk = slice(j * BLOCK, (j + 1) * BLOCK)
                qtb_ref[r * (sub // BLOCK) + j] = qb[blk, :].T.astype(MM)
                ktb_ref[r * (sub // BLOCK) + j] = kb[blk, :].T.astype(MM)
                vtb_ref[r * (sub // BLOCK) + j] = vb[blk, :].T.astype(MM)

    tok = lambda w: pl.BlockSpec((tm, w), lambda i: (i, 0))
    tab = lambda w: pl.BlockSpec((tm, w), lambda i: (i % n_seq, 0))
    row = lambda w: pl.BlockSpec((1, w), lambda i: (0, 0))
    tposed = pl.BlockSpec((1, LANES, tm), lambda i: (i // n_seq, 0, i % n_seq))
    blocks = pl.BlockSpec((nblk, BLOCK, LANES), lambda i: (i, 0, 0))
    qblocks = pl.BlockSpec((nblk, Q_WIDTH, BLOCK), lambda i: (i, 0, 0))
    tok_mm = lambda w: jax.ShapeDtypeStruct((tokens, w), MM)
    return pl.pallas_call(
        body, name="pre_proj",
        grid=(tokens // tm,),
        in_specs=[tok(D_MODEL), row(D_MODEL),
                  pl.BlockSpec((N_CHIPS, D_MODEL, IN_CHUNK), lambda i: (0, 0, 0)),
                  row(Q_WIDTH), row(KV_WIDTH), tab(KV_WIDTH), tab(KV_WIDTH)],
        out_specs=[tok(D_MODEL), tok(QK_RAW), tok(Q_WIDTH), tok(KV_WIDTH), tposed, tok(KV_WIDTH), tposed,
                   qblocks, tok(KV_WIDTH), blocks, tok(KV_WIDTH), blocks],
        out_shape=[
            tok_mm(D_MODEL),
            jax.ShapeDtypeStruct((tokens, QK_RAW), F32),
            tok_mm(Q_WIDTH),
            tok_mm(KV_WIDTH),
            jax.ShapeDtypeStruct((batch, KV_WIDTH, seq), MM),
            tok_mm(KV_WIDTH),
            jax.ShapeDtypeStruct((batch, KV_WIDTH, seq), MM),
            jax.ShapeDtypeStruct((tokens // BLOCK, Q_WIDTH, BLOCK), MM),
            tok_mm(KV_WIDTH),
            jax.ShapeDtypeStruct((tokens // BLOCK, KV_WIDTH, BLOCK), MM),
            tok_mm(KV_WIDTH),
            jax.ShapeDtypeStruct((tokens // BLOCK, KV_WIDTH, BLOCK), MM),
        ],
        scratch_shapes=[pltpu.VMEM((tm, IN_TOTAL), F32)],
        compiler_params=_cparams(("parallel",), 48),
    )(*map(_from_hbm, (x, g1, w_in, gq, gk, ck, sk)))


def _kv_half(v2, kv):
    return jnp.where(kv == 0, v2[:, :HEAD_DIM], v2[:, HEAD_DIM:])


def _attn_a_fwd(qa, kta, va, *, seq, bq, jobs=()):
    tokens = qa.shape[0]
    batch = tokens // seq
    nq = seq // bq

    def body(q_ref, kt_ref, v_ref, o_ref, p_ref, linv_ref):
        kv = pl.program_id(1)
        kt = kt_ref[0]
        lane = lax.broadcasted_iota(jnp.int32, (seq, KV_WIDTH), 1)
        v = jnp.where((lane < HEAD_DIM) == (kv == 0), v_ref[...], jnp.ones((), MM))
        q_rows = jnp.concatenate([q_ref[:, g * HEAD_DIM:(g + 1) * HEAD_DIM] for g in range(GROUP)], axis=0)
        s_all = _dot(q_rows, kt)
        for g in range(GROUP):
            sl = slice(g * HEAD_DIM, (g + 1) * HEAD_DIM)
            s = s_all[g * bq:(g + 1) * bq]
            pb = jnp.exp((s - jnp.max(s, axis=-1, keepdims=True)).astype(MM))
            p_ref[0, g] = pb
            o2 = _dot(pb, v)
            linv = 1.0 / _kv_half(o2, 1 - kv)[:, 0:1]
            o_ref[:, sl] = _kv_half(o2, kv) * linv
            linv_ref[0, :, g:g + 1] = linv

    return _call(
        body, (qa, kta, va), name="attn_a_fwd", jobs=jobs,
        grid=(batch, N_KV, nq),
        in_specs=[pl.BlockSpec((bq, GROUP * HEAD_DIM), lambda b, k, i: (b * nq + i, k)),
                  pl.BlockSpec((1, HEAD_DIM, seq), lambda b, k, i: (b, k, 0)),
                  pl.BlockSpec((seq, KV_WIDTH), lambda b, k, i: (b, 0))],
        out_specs=[pl.BlockSpec((bq, GROUP * HEAD_DIM), lambda b, k, i: (b * nq + i, k)),
                   pl.BlockSpec((1, GROUP, bq, seq), lambda b, k, i: (k, 0, b * nq + i, 0)),
                   pl.BlockSpec((1, bq, GROUP), lambda b, k, i: (k, b * nq + i, 0))],
        out_shape=[jax.ShapeDtypeStruct((tokens, Q_WIDTH), F32),
                   jax.ShapeDtypeStruct((N_KV, GROUP, tokens, seq), MM),
                   jax.ShapeDtypeStruct((N_KV, tokens, GROUP), F32)],
        params=_cparams(("arbitrary", "arbitrary", "arbitrary"), 56))


def _attn_a_bwd(qa, ka, vta, do, o, p, linv, *, seq, bq, kt_tile, jobs=()):
    tokens = qa.shape[0]
    batch = tokens // seq
    nq = seq // bq

    def body(q_ref, k_ref, vt_ref, do_ref, o_ref, p_ref, linv_ref, dq_ref, dkt_ref, dvt_ref):
        kv = pl.program_id(1)

        @pl.when(pl.program_id(2) == 0)
        def _():
            dkt_ref[...] = jnp.zeros_like(dkt_ref)
            dvt_ref[...] = jnp.zeros_like(dvt_ref)

        vt = vt_ref[0]
        k2 = k_ref[...]
        for g in range(GROUP):
            sl = slice(g * HEAD_DIM, (g + 1) * HEAD_DIM)
            dof = do_ref[:, sl]
            delta = jnp.sum(dof * o_ref[:, sl], axis=-1, keepdims=True)
            linv_g = linv_ref[0, :, g:g + 1]
            dob = dof.astype(MM)
            don = (dof * linv_g).astype(MM)
            dq2 = jnp.zeros((bq, KV_WIDTH), F32)
            for t in range(seq // kt_tile):
                keys = slice(t * kt_tile, (t + 1) * kt_tile)
                pb = p_ref[0, g, :, keys]
                ds = pb * ((_dot(dob, vt[:, keys]) - delta) * linv_g).astype(MM)
                dq2 = dq2 + _dot(ds, k2[keys, :])
                dkt_ref[0, :, keys] += _dot_tn(q_ref[:, sl], ds)
                dvt_ref[0, :, keys] += _dot_tn(don, pb)
            dq_ref[:, sl] = _kv_half(dq2, kv)

    qspec = pl.BlockSpec((bq, GROUP * HEAD_DIM), lambda b, k, i: (b * nq + i, k))
    tspec = pl.BlockSpec((1, HEAD_DIM, seq), lambda b, k, i: (b, k, 0))
    return _call(
        body, (qa, ka, vta, do, o, p, linv), name="attn_a_bwd", jobs=jobs,
        grid=(batch, N_KV, nq),
        in_specs=[qspec, pl.BlockSpec((seq, KV_WIDTH), lambda b, k, i: (b, 0)), tspec, qspec, qspec,
                  pl.BlockSpec((1, GROUP, bq, seq), lambda b, k, i: (k, 0, b * nq + i, 0)),
                  pl.BlockSpec((1, bq, GROUP), lambda b, k, i: (k, b * nq + i, 0))],
        out_specs=[qspec, tspec, tspec],
        out_shape=[jax.ShapeDtypeStruct((tokens, Q_WIDTH), F32),
                   jax.ShapeDtypeStruct((batch, KV_WIDTH, seq), F32),
                   jax.ShapeDtypeStruct((batch, KV_WIDTH, seq), F32)],
        params=_cparams(("arbitrary", "arbitrary", "arbitrary"), 56))


def _bias_build(rel_bias_t, bucket_t, band_t):
    def body(tab_ref, bucket_ref, band_ref, bias_ref):
        for h in range(GROUP * N_KV):
            for piece in range(3):
                bk = bucket_ref[piece]
                acc = jnp.zeros((BLOCK, BLOCK), F32)
                for b in range(N_BUCKETS):
                    acc = jnp.where(bk == b, tab_ref[h, b], acc)
                g = h % GROUP
                bias_ref[h // GROUP, piece, :, g * BLOCK:(g + 1) * BLOCK] = jnp.where(band_ref[piece] != 0, acc, NEG_INF)

    out = jax.ShapeDtypeStruct((N_KV, 3, BLOCK, GROUP * BLOCK), F32)
    return pl.pallas_call(
        body, name="bias_build", grid=(1,),
        in_specs=[SMEM, _whole(bucket_t), _whole(band_t)], out_specs=_whole(out), out_shape=out,
    )(rel_bias_t, bucket_t, band_t)


def _pad_heads(t, kv):
    outs = []
    for g in range(GROUP):
        tg = t[g * HEAD_DIM:(g + 1) * HEAD_DIM, :]
        zero = jnp.zeros_like(tg)
        outs.append(jnp.concatenate([jnp.where(kv == 0, tg, zero), jnp.where(kv == 0, zero, tg)], axis=0))
    return jnp.concatenate(outs, axis=-1)


def _unpad_heads(t, kv):
    outs = [_kv_half(t[:, g * BLOCK:(g + 1) * BLOCK].T, kv) for g in range(GROUP)]
    return jnp.concatenate(outs, axis=-1)


def _sink_row(sink_ref, kv):
    lane_head = lax.broadcasted_iota(jnp.int32, (1, GROUP * BLOCK), 1) // BLOCK
    row = jnp.zeros((1, GROUP * BLOCK), F32)
    for g in range(GROUP):
        row = jnp.where(lane_head == g, sink_ref[0, kv * GROUP + g], row)
    return row


def _rows3(ref, idx):
    return jnp.concatenate([ref[i] for i in idx], axis=0)


def _lanes3(ref, idx):
    return jnp.concatenate([ref[i] for i in idx], axis=-1)


def _window_scores_t(k_ref, idx, qpad, bias_ref, n, nblk):
    s_all = _dot(_rows3(k_ref, idx), qpad)
    pieces = []
    for piece in range(3):
        s = s_all[piece * BLOCK:(piece + 1) * BLOCK] + bias_ref[0, piece]
        if piece == 0:
            s = jnp.where(n > 0, s, NEG_INF)
        if piece == 2:
            s = jnp.where(n < nblk - 1, s, NEG_INF)
        pieces.append(s)
    return pieces


def _attn_b_fwd(qtb, kb3, vtb, bias, sink, *, seq, per_step, jobs=()):
    nblk_all = qtb.shape[0]
    tokens = nblk_all * BLOCK
    batch = tokens // seq
    nblk = seq // BLOCK
    nstep = nblk // per_step

    def body(sink_ref, q_ref, k_ref, vt_ref, bias_ref, o_ref, p_ref, stat_ref):
        kv = pl.program_id(0)
        first = pl.program_id(2) * per_step
        sink_row = _sink_row(sink_ref, kv)
        stat_row = lax.broadcasted_iota(jnp.int32, (8, GROUP * BLOCK), 0)

        def block(i, carry):
            n = first + i
            idx = (jnp.maximum(n - 1, 0), n, jnp.minimum(n + 1, nblk - 1))
            rows = slice(i * BLOCK, (i + 1) * BLOCK)
            qpad = _pad_heads(q_ref[n], kv)
            ss = _window_scores_t(k_ref, idx, qpad, bias_ref, n, nblk)
            m = jnp.maximum(jnp.maximum(jnp.max(ss[0], axis=0, keepdims=True),
                                        jnp.max(ss[1], axis=0, keepdims=True)),
                            jnp.maximum(jnp.max(ss[2], axis=0, keepdims=True), sink_row))
            ps = [jnp.exp(s - m) for s in ss]
            e_sink = jnp.exp(sink_row - m)
            rinv = 1.0 / (jnp.sum(ps[0], axis=0, keepdims=True) + jnp.sum(ps[1], axis=0, keepdims=True)
                          + jnp.sum(ps[2], axis=0, keepdims=True) + e_sink)
            pbs = [p.astype(MM) for p in ps]
            for piece in range(3):
                p_ref[0, i, piece] = pbs[piece]
            ot = _dot(_lanes3(vt_ref, idx), jnp.concatenate(pbs, axis=0))
            o_ref[rows, :] = _unpad_heads(ot * rinv, kv)
            stat_ref[0, i] = jnp.where(stat_row == 0, rinv, e_sink * rinv)
            return carry

        for i in range(per_step):
            block(i, 0)

    both = pl.BlockSpec((nblk, BLOCK, KV_WIDTH), lambda k, b, j: (b, 0, 0))
    return _call(
        body, (sink, qtb, kb3, vtb, bias), name="attn_b_fwd", jobs=jobs,
        grid=(N_KV, batch, nstep),
        in_specs=[SMEM, pl.BlockSpec((nblk, GROUP * HEAD_DIM, BLOCK), lambda k, b, j: (b, k, 0)), both, both,
                  pl.BlockSpec((1, 3, BLOCK, GROUP * BLOCK), lambda k, b, j: (k, 0, 0, 0))],
        out_specs=[pl.BlockSpec((per_step * BLOCK, GROUP * HEAD_DIM), lambda k, b, j: (b * nstep + j, k)),
                   pl.BlockSpec((1, per_step, 3, BLOCK, GROUP * BLOCK), lambda k, b, j: (k, b * nstep + j, 0, 0, 0)),
                   pl.BlockSpec((1, per_step, 8, GROUP * BLOCK), lambda k, b, j: (k, b * nstep + j, 0, 0))],
        out_shape=[jax.ShapeDtypeStruct((tokens, Q_WIDTH), F32),
                   jax.ShapeDtypeStruct((N_KV, nblk_all, 3, BLOCK, GROUP * BLOCK), MM),
                   jax.ShapeDtypeStruct((N_KV, nblk_all, 8, GROUP * BLOCK), F32)],
        params=_cparams(("arbitrary", "arbitrary", "arbitrary"), 48))


def _attn_b_bwd(qtb, ktb, vb3, do, o, p, stat, *, seq, per_step, jobs=()):
    nblk_all = qtb.shape[0]
    tokens = nblk_all * BLOCK
    batch = tokens // seq
    nblk = seq // BLOCK
    nstep = nblk // per_step

    def body(q_ref, kt_ref, v_ref, do_ref, o_ref, p_ref, stat_ref,
             dq_ref, dk_ref, dv_ref, dbias_ref, dsink_ref):
        kv = pl.program_id(0)
        step = pl.program_id(2)
        first = step * per_step

        @pl.when(jnp.logical_and(pl.program_id(1) == 0, step == 0))
        def _():
            dbias_ref[...] = jnp.zeros_like(dbias_ref)
            dsink_ref[...] = jnp.zeros_like(dsink_ref)

        @pl.when(step == 0)
        def _():
            dk_ref[...] = jnp.zeros_like(dk_ref)
            dv_ref[...] = jnp.zeros_like(dv_ref)

        def block(i, dsink):
            n = first + i
            idx = (jnp.maximum(n - 1, 0), n, jnp.minimum(n + 1, nblk - 1))
            rows = slice(i * BLOCK, (i + 1) * BLOCK)
            qpad = _pad_heads(q_ref[n], kv)
            dot_t = do_ref[rows, :].T
            prod = dot_t * o_ref[rows, :].T
            delta = jnp.concatenate(
                [jnp.sum(prod[g * HEAD_DIM:(g + 1) * HEAD_DIM, :], axis=0, keepdims=True) for g in range(GROUP)],
                axis=-1)
            stats = stat_ref[0, i]
            rinv, p_sink = stats[0:1, :], stats[1:2, :]
            dopad32 = _pad_heads(dot_t, kv)
            dopad = dopad32.astype(MM)
            dopad_n = (dopad32 * rinv).astype(MM)
            dpt = _dot(_rows3(v_ref, idx), dopad)
            pbs, dsbs = [], []
            for piece in range(3):
                pb = p_ref[0, i, piece]
                dst = pb.astype(F32) * ((dpt[piece * BLOCK:(piece + 1) * BLOCK] - delta) * rinv)
                dbias_ref[0, piece] += dst
                pbs.append(pb)
                dsbs.append(dst.astype(MM))
            ds_cat = jnp.concatenate(dsbs, axis=0)
            dq_ref[rows, :] = _unpad_heads(_dot(_lanes3(kt_ref, idx), ds_cat), kv)
            dk_all = _dot_nt(ds_cat, qpad)
            dv_all = _dot_nt(jnp.concatenate(pbs, axis=0), dopad_n)
            for piece in range(3):
                dk_ref[0, idx[piece]] += dk_all[piece * BLOCK:(piece + 1) * BLOCK]
                dv_ref[0, idx[piece]] += dv_all[piece * BLOCK:(piece + 1) * BLOCK]
            return dsink - p_sink * delta

        dsink = jnp.zeros((1, GROUP * BLOCK), F32)
        for i in range(per_step):
            dsink = block(i, dsink)
        dsink_ref[0] += jnp.broadcast_to(dsink, (8, GROUP * BLOCK))

    qspec = pl.BlockSpec((per_step * BLOCK, GROUP * HEAD_DIM), lambda k, b, j: (b * nstep + j, k))
    both = pl.BlockSpec((nblk, BLOCK, KV_WIDTH), lambda k, b, j: (b, 0, 0))
    grad =pl.BlockSpec((1, nblk, BLOCK, KV_WIDTH), lambda k, b, j: (k, b, 0, 0))
    return _call(
        body, (qtb, ktb, vb3, do, o, p, stat), name="attn_b_bwd", jobs=jobs,
        grid=(N_KV, batch, nstep),
        in_specs=[pl.BlockSpec((nblk, GROUP * HEAD_DIM, BLOCK), lambda k, b, j: (b, k, 0)), both, both,
                  qspec, qspec,
                  pl.BlockSpec((1, per_step, 3, BLOCK, GROUP * BLOCK), lambda k, b, j: (k, b * nstep + j, 0, 0, 0)),
                  pl.BlockSpec((1, per_step, 8, GROUP * BLOCK), lambda k, b, j: (k, b * nstep + j, 0, 0))],
        out_specs=[qspec, grad, grad,
                   pl.BlockSpec((1, 3, BLOCK, GROUP * BLOCK), lambda k, b, j: (k, 0, 0, 0)),
                   pl.BlockSpec((1, 8, GROUP * BLOCK), lambda k, b, j: (k, 0, 0))],
        out_shape=[jax.ShapeDtypeStruct((tokens, Q_WIDTH), F32),
                   jax.ShapeDtypeStruct((N_KV, nblk_all, BLOCK, KV_WIDTH), F32),
                   jax.ShapeDtypeStruct((N_KV, nblk_all, BLOCK, KV_WIDTH), F32),
                   jax.ShapeDtypeStruct((N_KV, 3, BLOCK, GROUP * BLOCK), F32),
                   jax.ShapeDtypeStruct((N_KV, 8, GROUP * BLOCK), F32)],
        params=_cparams(("arbitrary", "arbitrary", "arbitrary"), 56))


def _resident(shape):
    return pl.BlockSpec(shape, lambda i: (0,) * len(shape), pipeline_mode=pl.Buffered(1))


def _mix_ffn_fwd(oa, ob, w_o, x, g2, g3, w_up, w_down, target, g4, *, tm):
    tokens = x.shape[0]
    nt = tokens // tm

    def body(oa_ref, ob_ref, wo_ref, x_ref, g2_ref, g3_ref, wu_ref, wd_ref, t_ref, g4_ref,
             mix_ref, x1_ref, h2_ref, o_ref, u_ref, df_ref, dy_ref, loss_ref, dg4_ref):
        o = jnp.concatenate([oa_ref[...].astype(MM), ob_ref[...].astype(MM)], axis=-1)
        o_ref[...] = o
        mix = _dot(o, wo_ref[...])
        mix_ref[...] = mix
        x1 = x_ref[...] + mix * _rms_r(mix) * g2_ref[...]
        x1_ref[...] = x1
        h2v = (x1 * _rms_r(x1) * g3_ref[...]).astype(MM)
        h2_ref[...] = h2v
        f = jnp.zeros((tm, D_MODEL), F32)
        for c in range(N_CHIPS):
            u = jnp.maximum(_dot(h2v, wu_ref[c]), 0.0)
            u_ref[:, c * FF_CHUNK:(c + 1) * FF_CHUNK] = u.astype(MM)
            f = f + _dot((u * u).astype(MM), wd_ref[c * FF_CHUNK:(c + 1) * FF_CHUNK, :])
        r = _rms_r(f)
        g4v = g4_ref[...]
        err = x1 + f * r * g4v - t_ref[...]
        sq = jnp.sum(err * err, axis=-1, keepdims=True)
        loss_ref[0] = jnp.broadcast_to(jnp.sum(sq, axis=0, keepdims=True) * (0.5 / D_MODEL), (8, LANES))
        dy = err * (1.0 / D_MODEL)
        dy_ref[...] = dy
        dfv, dgv = _rms_bwd(f, r, g4v, dy)
        df_ref[...] = dfv.astype(MM)
        dg4_ref[0] = jnp.sum(dgv, axis=0, keepdims=True)

    tok = pl.BlockSpec((tm, D_MODEL), lambda i: (i, 0))
    half = pl.BlockSpec((tm, Q_WIDTH), lambda i: (i, 0))
    row = pl.BlockSpec((1, D_MODEL), lambda i: (0, 0))
    tok_f32 = jax.ShapeDtypeStruct((tokens, D_MODEL), F32)
    tok_mm = jax.ShapeDtypeStruct((tokens, D_MODEL), MM)
    return pl.pallas_call(
        body, name="mix_ffn_fwd",
        grid=(nt,),
        in_specs=[half, half, _resident((D_MODEL, D_MODEL)), tok, row, row,
                  _resident((N_CHIPS, D_MODEL, FF_CHUNK)), _resident((D_FF, D_MODEL)), tok, row],
        out_specs=[tok, tok, tok, tok, pl.BlockSpec((tm, D_FF), lambda i: (i, 0)), tok, tok,
                   pl.BlockSpec((1, 8, LANES), lambda i: (i, 0, 0)),
                   pl.BlockSpec((1, 1, D_MODEL), lambda i: (i, 0, 0))],
        out_shape=[tok_f32,
                   tok_f32,
                   tok_mm,
                   tok_mm,
                   jax.ShapeDtypeStruct((tokens, D_FF), MM),
                   tok_mm,
                   tok_f32,
                   jax.ShapeDtypeStruct((nt, 8, LANES), F32),
                   jax.ShapeDtypeStruct((nt, 1, D_MODEL), F32)],
        compiler_params=_cparams(("parallel",), 56),
    )(*map(_from_hbm, (oa, ob, w_o, x, g2, g3, w_up, w_down, target, g4)))


def _ffn_bwd_act(df, w_down, u, w_up, x1, dy, mix, g3, g2, w_o, *, tm):
    tokens = df.shape[0]
    nt = tokens // tm

    def body(df_ref, wd_ref, u_ref, wu_ref, x1_ref, dy_ref, mix_ref, g3_ref, g2_ref, wo_ref,
             dz_ref, dx1_ref, dmix_ref, dg3_ref, dg2_ref, doa_ref, dob_ref):
        dfv = df_ref[...]
        dh2 = jnp.zeros((tm, D_MODEL), F32)
        for c in range(N_CHIPS):
            cols = slice(c * FF_CHUNK, (c + 1) * FF_CHUNK)
            da = _dot_nt(dfv, wd_ref[cols, :])
            dz = (da * (2.0 * u_ref[:, cols].astype(F32))).astype(MM)
            dz_ref[:, cols] = dz
            dh2 = dh2 + _dot_nt(dz, wu_ref[c])
        x1 = x1_ref[...]
        dxn, dg3v = _rms_bwd(x1, _rms_r(x1), g3_ref[...], dh2)
        dx1 = dy_ref[...] + dxn
        dx1_ref[...] = dx1
        dg3_ref[0] = jnp.sum(dg3v, axis=0, keepdims=True)
        mix = mix_ref[...]
        dmix, dg2v = _rms_bwd(mix, _rms_r(mix), g2_ref[...], dx1)
        dmb = dmix.astype(MM)
        dmix_ref[...] = dmb
        dg2_ref[0] = jnp.sum(dg2v, axis=0, keepdims=True)
        doa_ref[...] = _dot_nt(dmb, wo_ref[0:Q_WIDTH, :])
        dob_ref[...] = _dot_nt(dmb, wo_ref[Q_WIDTH:D_MODEL, :])

    tok = pl.BlockSpec((tm, D_MODEL), lambda i: (i, 0))
    half = pl.BlockSpec((tm, Q_WIDTH), lambda i: (i, 0))
    wide = pl.BlockSpec((tm, D_FF), lambda i: (i, 0))
    row = pl.BlockSpec((1, D_MODEL), lambda i: (0, 0))
    part = pl.BlockSpec((1, 1, D_MODEL), lambda i: (i, 0, 0))
    return pl.pallas_call(
        body, name="ffn_bwd_act",
        grid=(nt,),
        in_specs=[tok, _resident((D_FF, D_MODEL)), wide, _resident((N_CHIPS, D_MODEL, FF_CHUNK)),
                  tok, tok, tok, row, row, _resident((D_MODEL, D_MODEL))],
        out_specs=[wide, tok, tok, part, part, half, half],
        out_shape=[jax.ShapeDtypeStruct((tokens, D_FF), MM),
                   jax.ShapeDtypeStruct((tokens, D_MODEL), F32),
                   jax.ShapeDtypeStruct((tokens, D_MODEL), MM),
                   jax.ShapeDtypeStruct((nt, 1, D_MODEL), F32),
                   jax.ShapeDtypeStruct((nt, 1, D_MODEL), F32),
                   jax.ShapeDtypeStruct((tokens, Q_WIDTH), F32),
                   jax.ShapeDtypeStruct((tokens, Q_WIDTH), F32)],
        compiler_params=_cparams(("parallel",), 56),
    )(*map(_from_hbm, (df, w_down, u, w_up, x1, dy, mix, g3, g2, w_o)))


def _tn_matmul(a, b, *, name, tm, tn, tk, chunk=None, square_a=False, vmem_mb=48, jobs=()):
    tokens, m_dim = a.shape
    n_dim = b.shape[1]
    chunked = chunk is not None
    if chunked:
        assert tm == m_dim and tn % chunk == 0

    def body(a_ref, b_ref, o_ref):
        av = a_ref[...]
        if square_a:
            av = av.astype(F32)
            av = av * av
        part = _dot_tn(av.astype(MM), b_ref[...].astype(MM))
        if chunked:
            part = jnp.stack([part[:, c * chunk:(c + 1) * chunk] for c in range(tn // chunk)])

        @pl.when(pl.program_id(2) == 0)
        def _():
            o_ref[...] = part

        @pl.when(pl.program_id(2) > 0)
        def _():
            o_ref[...] += part

    if chunked:
        out_spec = pl.BlockSpec((tn // chunk, tm, chunk), lambda i, j, k: (j, 0, 0))
        out_shape = jax.ShapeDtypeStruct((n_dim // chunk, m_dim, chunk), F32)
    else:
        out_spec = pl.BlockSpec((tm, tn), lambda i, j, k: (i, j))
        out_shape = jax.ShapeDtypeStruct((m_dim, n_dim), F32)
    (out,), job_res = _call(
        body, (a, b), name=name, jobs=jobs,
        grid=(m_dim // tm, n_dim // tn, tokens // tk),
        in_specs=[pl.BlockSpec((tk, tm), lambda i, j, k: (k, i)),
                  pl.BlockSpec((tk, tn), lambda i, j, k: (k, j))],
        out_specs=[out_spec], out_shape=[_in_hbm(out_shape)],
        params=_cparams(("arbitrary", "arbitrary", "arbitrary"), vmem_mb))
    return out, job_res


def _proj_bwd(dqa, dkta, dvta, dqb, dktb, dvtb, raw, x, dx1, g1, w_in, gq, gk, ck, sk, *, seq, tm, sub, jobs=()):
    tokens = x.shape[0]
    nt = tokens // tm
    n_seq = seq // tm
    nblk = tm // BLOCK

    def body(dqa_ref, dkta_ref, dvta_ref, dqb_ref, dkb_ref, dvb_ref, raw_ref, x_ref, dx1_ref, g1_ref, w_ref,
             gq_ref, gk_ref, ck_ref, sk_ref,
             gx_ref, dproj_ref, dg1_ref, dgq_ref, dgk_ref, dp):
        parts = []
        for r in range(tm // sub):
            rows = slice(r * sub, (r + 1) * sub)
            qa = raw_ref[rows, 0:Q_WIDTH]
            dqn = _rope_t(dqa_ref[rows, :], ck_ref[rows, :], sk_ref[rows, :]) * SCALE
            rq = _head_r(qa)
            nq = qa * rq
            dnq = dqn * gq_ref[...]
            dp[rows, 0:Q_WIDTH] = rq * (dnq - nq * (_seg64_sum(dnq * nq) * (1.0 / HEAD_DIM)))

            ka = raw_ref[rows, Q_WIDTH:QK_RAW]
            dkn = _rope_t(dkta_ref[0, :, rows].T, ck_ref[rows, :], sk_ref[rows, :])
            rk = _head_r(ka)
            nk = ka * rk
            dnk = dkn * gk_ref[...]
            dp[rows, 512:640] = rk * (dnk - nk * (_seg64_sum(dnk * nk) * (1.0 / HEAD_DIM)))

            dp[rows, 640:768] = dvta_ref[0, :, rows].T
            dp[rows, 768:1280] = dqb_ref[rows, :] * SCALE
            for j in range(r * sub // BLOCK, (r + 1) * sub // BLOCK):
                dp[j * BLOCK:(j + 1) * BLOCK, 1280:1408] = dkb_ref[0, j] + dkb_ref[1, j]
                dp[j * BLOCK:(j + 1) * BLOCK, 1408:1536] = dvb_ref[0, j] + dvb_ref[1, j]

            dproj = dp[rows, :].astype(MM)
            dproj_ref[rows, :] = dproj
            dh1 = _dot_nt(dproj[:, 0:IN_CHUNK], w_ref[0])
            for j in range(1, N_CHIPS):
                dh1 = dh1 + _dot_nt(dproj[:, j * IN_CHUNK:(j + 1) * IN_CHUNK], w_ref[j])
            xv = x_ref[rows, :]
            dxn, dg1v = _rms_bwd(xv, _rms_r(xv), g1_ref[...], dh1)
            gx_ref[rows, :] = dx1_ref[rows, :] + dxn
            parts.append((jnp.sum(dqn * nq, axis=0, keepdims=True), jnp.sum(dkn * nk, axis=0, keepdims=True),
                          jnp.sum(dg1v, axis=0, keepdims=True)))
        dgq_ref[0] = functools.reduce(jnp.add, [p[0] for p in parts])
        dgk_ref[0] = functools.reduce(jnp.add, [p[1] for p in parts])
        dg1_ref[0] = functools.reduce(jnp.add, [p[2] for p in parts])

    tok = lambda w: pl.BlockSpec((tm, w), lambda i: (i, 0))
    tab = lambda w: pl.BlockSpec((tm, w), lambda i: (i % n_seq, 0))
    row = lambda w: pl.BlockSpec((1, w), lambda i: (0, 0))
    tposed = pl.BlockSpec((1, KV_WIDTH, tm), lambda i: (i // n_seq, 0, i % n_seq))
    blocks = pl.BlockSpec((N_KV, nblk, BLOCK, KV_WIDTH), lambda i: (0, i, 0, 0))
    part = lambda w: pl.BlockSpec((1, 1, w), lambda i: (i, 0, 0))
    return _call(
        body, (dqa, dkta, dvta, dqb, dktb, dvtb, raw, x, dx1, g1, w_in, gq, gk, ck, sk),
        name="proj_bwd", jobs=jobs,
        grid=(nt,),
        in_specs=[tok(Q_WIDTH), tposed, tposed, tok(Q_WIDTH), blocks, blocks, tok(QK_RAW), tok(D_MODEL),
                  tok(D_MODEL), row(D_MODEL),
                  pl.BlockSpec((N_CHIPS, D_MODEL, IN_CHUNK), lambda i: (0, 0, 0)),
                  row(Q_WIDTH), row(KV_WIDTH), tab(KV_WIDTH), tab(KV_WIDTH)],
        out_specs=[tok(D_MODEL), tok(IN_TOTAL), part(D_MODEL), part(Q_WIDTH), part(KV_WIDTH)],
        out_shape=[jax.ShapeDtypeStruct((tokens, D_MODEL), F32),
                   jax.ShapeDtypeStruct((tokens, IN_TOTAL), MM),
                   jax.ShapeDtypeStruct((nt, 1, D_MODEL), F32),
                   jax.ShapeDtypeStruct((nt, 1, Q_WIDTH), F32),
                   jax.ShapeDtypeStruct((nt, 1, KV_WIDTH), F32)],
        scratch_shapes=[pltpu.VMEM((tm, IN_TOTAL), F32)],
        params=_cparams(("arbitrary",), 56))


def _pack_small(dg1, dg2, dg3, dg4, dgq, dgk, dsink, dbias, bucket, loss):
    def body(dg1_ref, dg2_ref, dg3_ref, dg4_ref, dgq_ref, dgk_ref, dsink_ref, dbias_ref, bucket_ref, loss_ref,
             out_ref, rel_ref):
        out_ref[...] = jnp.zeros_like(out_ref)
        for r, ref in ((ROW_G1, dg1_ref), (ROW_G2, dg2_ref), (ROW_G3, dg3_ref), (ROW_G4, dg4_ref)):
            acc = ref[0]
            for t in range(1, ref.shape[0]):
                acc = acc + ref[t]
            out_ref[r:r + 1, :] = acc

        def fold(ref, heads):
            acc = ref[0]
            for t in range(1, ref.shape[0]):
                acc = acc + ref[t]
            tot = acc[:, 0:HEAD_DIM]
            for h in range(1, heads):
                tot = tot + acc[:, h * HEAD_DIM:(h + 1) * HEAD_DIM]
            return tot

        out_ref[ROW_MISC:ROW_MISC + 1, MISC_GQ:MISC_GQ + HEAD_DIM] = fold(dgq_ref, GROUP * N_KV)
        out_ref[ROW_MISC:ROW_MISC + 1, MISC_GK:MISC_GK + HEAD_DIM] = fold(dgk_ref, N_KV)
        for h in range(GROUP * N_KV):
            g = h % GROUP
            out_ref[ROW_MISC:ROW_MISC + 1, MISC_SINK + h:MISC_SINK + h + 1] = jnp.sum(
                dsink_ref[h // GROUP, 0:1, g * BLOCK:(g + 1) * BLOCK], axis=-1, keepdims=True)
        lacc = loss_ref[0, 0:1, 0:1]
        for t in range(1, loss_ref.shape[0]):
            lacc = lacc + loss_ref[t, 0:1, 0:1]
        out_ref[ROW_MISC:ROW_MISC + 1, MISC_LOSS:MISC_LOSS + 1] = lacc
        lane = lax.broadcasted_iota(jnp.int32, (N_BUCKETS, LANES), 1)
        row = lax.broadcasted_iota(jnp.int32, (N_BUCKETS, LANES), 0)

        def per_bucket(b, acc):
            for h in range(GROUP * N_KV):
                g = h % GROUP
                sel = jnp.zeros((BLOCK, BLOCK), F32)
                for piece in range(3):
                    sel = sel + jnp.where(bucket_ref[piece] == b,
                                          dbias_ref[h // GROUP, piece, :, g * BLOCK:(g + 1) * BLOCK], 0.0)
                tot = jnp.sum(jnp.sum(sel, axis=0, keepdims=True), axis=-1, keepdims=True)
                acc = jnp.where((row == b) & (lane == h), tot, acc)
            return acc

        rel_ref[...] = lax.fori_loop(0, N_BUCKETS, per_bucket, jnp.zeros((N_BUCKETS, LANES), F32))

    args = (dg1, dg2, dg3, dg4, dgq, dgk, dsink, dbias, bucket, loss)
    outs = [jax.ShapeDtypeStruct((8, D_MODEL), F32), jax.ShapeDtypeStruct((N_BUCKETS, LANES), F32)]
    return pl.pallas_call(
        body, name="pack_small", grid=(1,),
        in_specs=[_whole(a) for a in args], out_specs=[_whole(o) for o in outs], out_shape=outs,
        compiler_params=pltpu.CompilerParams(vmem_limit_bytes=32 * 1024 * 1024),
    )(*map(_from_hbm, args))


def _gather_weights(shards, whole):
    n = len(shards)
    full = [t for t in range(n) if whole[t]]

    def body(*refs):
        ins, outs = refs[:n], refs[n:2 * n]
        raw, stage = refs[2 * n:3 * n], refs[3 * n:4 * n]
        load_sem, local_sem, ici_send, ici_recv, d2d_send, d2d_recv = refs[4 * n:]
        x, y, c = _place()
        k = 2 * x + y
        sibling = (x, y, 1 - c)
        order = full + [t for t in range(n) if t not in full]
        loads = {t: pltpu.make_async_copy(ins[t], raw[t], load_sem.at[t]) for t in order}
        for t in order:
            loads[t].start()
        copies, sends = [], []
        for t in order:
            loads[t].wait()
            stage[t][...] = raw[t][...].astype(MM)
            mine = pltpu.make_async_copy(stage[t], outs[t].at[k], local_sem.at[t])
            mine.start()
            copies.append(mine)
            if t in full:
                half = ins[t].shape[0] // 2
                rows = pl.ds(c * half, half)
                for r, (fx, fy) in enumerate(_CHIP_FLIPS):
                    cp = _remote(stage[t].at[rows], outs[t].at[k, rows], ici_send.at[t, r], ici_recv.at[t, r],
                                 (_flip(x, fx), _flip(y, fy), c))
                    cp.start()
                    sends.append(cp)
        for t in full:
            half = ins[t].shape[0] // 2
            rows = pl.ds(c * half, half)
            for r, (fx, fy) in enumerate(_CHIP_FLIPS):
                kk = 2 * _flip(x, fx) + _flip(y, fy)
                landed = outs[t].at[kk, rows]
                _remote(landed, landed, ici_send.at[t, r], ici_recv.at[t, r], sibling).wait_recv()
                fwd = _remote(landed, landed, d2d_send.at[t, r], d2d_recv.at[t, r], sibling)
                fwd.start()
                sends.append(fwd)
        for t in full:
            half = ins[t].shape[0] // 2
            other = pl.ds((1 - c) * half, half)
            for r, (fx, fy) in enumerate(_CHIP_FLIPS):
                kk = 2 * _flip(x, fx) + _flip(y, fy)
                theirs = outs[t].at[kk, other]
                _remote(theirs, theirs, d2d_send.at[t, r], d2d_recv.at[t, r], sibling).wait_recv()
        for cp in sends:
            cp.wait_send()
        for cp in copies:
            cp.wait()

    return pl.pallas_call(
        body, name="gather_weights",
        in_specs=[HBM] * n, out_specs=[HBM] * n,
        out_shape=[pltpu.HBM((N_CHIPS,) + s.shape, MM) for s in shards],
        scratch_shapes=[pltpu.VMEM(s.shape, F32) for s in shards] + [pltpu.VMEM(s.shape, MM) for s in shards] + [
            pltpu.SemaphoreType.DMA((n,)), pltpu.SemaphoreType.DMA((n,)),
            pltpu.SemaphoreType.DMA((n, 3)), pltpu.SemaphoreType.DMA((n, 3)),
            pltpu.SemaphoreType.DMA((n, 3)), pltpu.SemaphoreType.DMA((n, 3))],
        compiler_params=pltpu.CompilerParams(vmem_limit_bytes=40 * 1024 * 1024),
    )(*shards)


def _add_half(grad, got, where, *, name, tr):
    nch, half, cols = got.shape
    tr = min(tr, half)
    nblk = half // tr

    def body(where_ref, g_ref, r_ref, o_ref):
        o_ref[...] = (g_ref[...] + r_ref[...]).astype(MM)

    return pl.pallas_call(
        body, name=name,
        grid_spec=pltpu.PrefetchScalarGridSpec(
            num_scalar_prefetch=1, grid=(nch, nblk),
            in_specs=[pl.BlockSpec((1, tr, cols), lambda j, i, where_ref: (j, where_ref[1] * nblk + i, 0)),
                      pl.BlockSpec((1, tr, cols), lambda j, i, where_ref: (j, i, 0))],
            out_specs=pl.BlockSpec((1, tr, cols), lambda j, i, where_ref: (j, i, 0))),
        out_shape=jax.ShapeDtypeStruct(got.shape, MM),
        compiler_params=_cparams(("parallel", "parallel"), 32),
    )(where, grad, got)


def _add_chips(own, got, where, *, name, tr):
    _, half, cols = own.shape
    tr = min(tr, half)
    nblk = half // tr

    def body(where_ref, o_ref, g_ref, out_ref):
        f = lambda v: v.astype(F32)
        out_ref[...] = ((f(o_ref[0]) + f(g_ref[0])) + f(g_ref[1])) + f(g_ref[2])

    return pl.pallas_call(
        body, name=name,
        grid_spec=pltpu.PrefetchScalarGridSpec(
            num_scalar_prefetch=1, grid=(nblk,),
            in_specs=[pl.BlockSpec((1, tr, cols), lambda i, where_ref: (where_ref[0], i, 0)),
                      pl.BlockSpec((3, tr, cols), lambda i, where_ref: (0, i, 0))],
            out_specs=pl.BlockSpec((tr, cols), lambda i, where_ref: (where_ref[1] * nblk + i, 0))),
        out_shape=pltpu.HBM((2 * half, cols), F32),
        compiler_params=_cparams(("parallel",), 32),
    )(where, own, got)


def _small_job(tiles):
    n = len(tiles)

    def copies(ins, outs, sems):
        x, y, c = _place()
        me = 4 * x + 2 * y + c
        local, send, recv = sems
        cps = []
        for t in range(n):
            cps.append(pltpu.make_async_copy(ins[t], outs[t].at[me], local.at[t]))
            for r in range(1, N_DEV):
                fx, fy, fc = (r >> 2) & 1, (r >> 1) & 1, r & 1
                cps.append(_remote(ins[t], outs[t].at[me], send.at[t, r - 1], recv.at[t, r - 1],
                                   (_flip(x, fx), _flip(y, fy), _flip(c, fc))))
        return cps

    return _Job(tiles, [jax.ShapeDtypeStruct((N_DEV,) + t.shape, F32) for t in tiles],
                [pltpu.SemaphoreType.DMA((n,)), pltpu.SemaphoreType.DMA((n, N_DEV - 1)),
                 pltpu.SemaphoreType.DMA((n, N_DEV - 1))], copies)


def _adamw_math(w, g, m, v):
    m = ADAM_B1 * m + (1.0 - ADAM_B1) * g
    v = ADAM_B2 * v + (1.0 - ADAM_B2) * (g * g)
    m_hat = m / (1.0 - ADAM_B1 ** ADAM_STEP)
    v_hat = v / (1.0 - ADAM_B2 ** ADAM_STEP)
    delta = -ADAM_LR * (m_hat / (jnp.sqrt(v_hat) + ADAM_EPS) + ADAM_WD * w)
    return delta, m, v


def _adamw(w, g, m, v, *, name, tr):
    rows, cols = w.shape
    tr = min(tr, rows)

    def body(w_ref, g_ref, m_ref, v_ref, go_ref, d_ref, nm_ref, nv_ref):
        g = g_ref[...]
        go_ref[...] = g
        d_ref[...], nm_ref[...], nv_ref[...] = _adamw_math(w_ref[...], g, m_ref[...], v_ref[...])

    spec = pl.BlockSpec((tr, cols), lambda i: (i, 0))
    return pl.pallas_call(
        body, name=name,
        grid=(rows // tr,),
        in_specs=[spec] * 4, out_specs=[spec] * 4,
        out_shape=[jax.ShapeDtypeStruct(w.shape, F32)] * 4,
        compiler_params=_cparams(("parallel",), 48),
    )(w, g, m, v)


def _small_adamw(gathered, gathered_rel, params, moments_m, moments_v):
    n = len(params)

    def body(all_ref, rel_all_ref, *refs):
        w_refs, m_refs, v_refs = refs[:n], refs[n:2 * n], refs[2 * n:3 * n]
        loss_ref = refs[3 * n]
        out_refs = refs[3 * n + 1:]
        g = all_ref[0]
        rel = rel_all_ref[0]
        for d in range(1, N_DEV):
            g = g + all_ref[d]
            rel = rel + rel_all_ref[d]
        misc = g[ROW_MISC:ROW_MISC + 1]
        loss_ref[...] = misc[:, MISC_LOSS:MISC_LOSS + 1]
        grads = (g[ROW_G1:ROW_G1 + 1], g[ROW_G2:ROW_G2 + 1], g[ROW_G3:ROW_G3 + 1], g[ROW_G4:ROW_G4 + 1],
                 misc[:, MISC_GQ:MISC_GQ + HEAD_DIM], misc[:, MISC_GK:MISC_GK + HEAD_DIM],
                 misc[:, MISC_SINK:MISC_SINK + GROUP * N_KV], rel[:, 0:GROUP * N_KV])
        for i in range(n):
            d, nm, nv = _adamw_math(w_refs[i][...], grads[i], m_refs[i][...], v_refs[i][...])
            for j, val in enumerate((grads[i], d, nm, nv)):
                out_refs[4 * i + j][...] = val

    args = (gathered, gathered_rel, *params, *moments_m, *moments_v)
    out_shape = [jax.ShapeDtypeStruct((1, 1), F32)] + [jax.ShapeDtypeStruct(p.shape, F32) for p in params
                                                       for _ in range(4)]
    outs = pl.pallas_call(
        body, name="small_adamw", grid=(1,),
        in_specs=[_whole(a) for a in args], out_specs=[_whole(o) for o in out_shape], out_shape=out_shape,
    )(*map(_from_hbm, args))
    return outs[0], [outs[1 + 4 * i:5 + 4 * i] for i in range(n)]


def kernel(x, w_in, w_o, g_pre_mix, g_post_mix, q_norm_a, k_norm_a, sink_b, rel_bias, g_pre_ffn, w_ffn_up, w_ffn_down, g_post_ffn, loss_target, m_w_in, m_w_o, m_g_pre_mix, m_g_post_mix, m_q_norm_a, m_k_norm_a, m_sink_b, m_rel_bias, m_g_pre_ffn, m_w_ffn_up, m_w_ffn_down, m_g_post_ffn, v_w_in, v_w_o, v_g_pre_mix, v_g_post_mix, v_q_norm_a, v_k_norm_a, v_sink_b, v_rel_bias, v_g_pre_ffn, v_w_ffn_up, v_w_ffn_down, v_g_post_ffn):
    batch, seq, _ = x.shape
    tokens = batch * seq
    where = jnp.stack([2 * lax.axis_index("x") + lax.axis_index("y"), lax.axis_index("c")]).astype(jnp.int32)
    x2 = x.reshape(tokens, D_MODEL)
    g1, g2, g3, g4 = g_pre_mix, g_post_mix, g_pre_ffn, g_post_ffn

    cos, sin = _rope_tables(seq)
    ck, sk = jnp.tile(cos, (1, 2)), jnp.tile(sin, (1, 2))
    gq8, gk2 = jnp.tile(q_norm_a, (1, 8)), jnp.tile(k_norm_a, (1, 2))
    bucket, band = _window_tables()
    bias = _bias_build(rel_bias.T, bucket, band)

    w_in_g, w_o_p, w_up_p, w_down_p = _gather_weights(
        (w_in[0], w_o[0], w_ffn_up[0], w_ffn_down[0]), whole=(True, False, False, False))
    (h1, raw, qa, ka, kta, va, vta, qtb, kb, ktb, vb, vtb) = _pre_proj(
        x2, g1, w_in_g, gq8, gk2, ck, sk, seq=seq, tm=min(1024, seq), sub=256)
    (oa, p_a, linv_a), (w_part,) = _attn_a_fwd(
        qa, kta, va, seq=seq, bq=min(256, seq), jobs=[_gather_job([w_o_p, w_up_p, w_down_p], forward=False)])
    kb3 = kb.reshape(tokens // BLOCK, BLOCK, KV_WIDTH)
    vb3 = vb.reshape(tokens // BLOCK, BLOCK, KV_WIDTH)
    (ob, p_b, stat_b), ((w_o_g, w_up_g, w_down_g),) = _attn_b_fwd(
        qtb, kb3, vtb, bias, sink_b, seq=seq, per_step=min(16, seq // BLOCK),
        jobs=[_gather_job(w_part, forward=True)])
    w_o2 = w_o_g.reshape(D_MODEL, D_MODEL)
    w_down2 = w_down_g.reshape(D_FF, D_MODEL)
    mix, x1, h2, o_cat, u, df, dy, loss_t, dg4 = _mix_ffn_fwd(
        oa, ob, w_o2, x2, g2, g3, w_up_g, w_down2, loss_target.reshape(tokens, D_MODEL), g4, tm=256)

    dz, dx1, dmix, dg3, dg2, doa, dob = _ffn_bwd_act(df, w_down2, u, w_up_g, x1, dy, mix, g3, g2, w_o2, tm=256)
    gw_down, _ = _tn_matmul(u, df, name="grad_w_down", tm=1024, tn=1024, tk=min(4096, tokens), square_a=True,
                            vmem_mb=56)
    gw_down = gw_down.reshape(N_CHIPS, FF_CHUNK, D_MODEL)
    gw_up, ((got_down,),) = _tn_matmul(h2, dz, name="grad_w_up", tm=1024, tn=1024, tk=min(4096, tokens), chunk=FF_CHUNK,
                                        vmem_mb=56, jobs=[_swap_job([gw_down])])
    gw_o, _ = _tn_matmul(o_cat, dmix, name="grad_w_o", tm=1024, tn=1024, tk=min(2048, tokens))
    gw_o = gw_o.reshape(N_CHIPS, O_CHUNK, D_MODEL)
    sum_down = _add_half(gw_down, got_down, where, name="add_half_w_down", tr=512)
    (dqa, dkta, dvta), ((ex_down,), (got_up,)) = _attn_a_bwd(
        qa, ka, vta, doa, oa, p_a, linv_a, seq=seq, bq=min(256, seq), kt_tile=min(2048, seq),
        jobs=[_exchange_job([sum_down]), _swap_job([gw_up])])
    full_down = _add_chips(sum_down, ex_down, where, name="add_chips_w_down", tr=512)
    sum_up = _add_half(gw_up, got_up, where, name="add_half_w_up", tr=512)
    (dqb, dkb, dvb, dbias, dsink), ((ex_up,), (g_down,), (got_o,)) = _attn_b_bwd(
        qtb, ktb, vb3, dob, ob, p_b, stat_b, seq=seq, per_step=min(16, seq // BLOCK),
        jobs=[_exchange_job([sum_up]), _join_job([full_down]), _swap_job([gw_o])])
    full_up = _add_chips(sum_up, ex_up, where, name="add_chips_w_up", tr=512)
    sum_o = _add_half(gw_o, got_o, where, name="add_half_w_o", tr=512)
    (grad_x, dproj, dg1, dgq, dgk), _ = _proj_bwd(
        dqa, dkta, dvta, dqb, dkb, dvb, raw, x2, dx1, g1, w_in_g, gq8, gk2, ck, sk,
        seq=seq, tm=min(512, seq), sub=128)
    packed, packed_rel = _pack_small(dg1, dg2, dg3, dg4, dgq, dgk, dsink, dbias, bucket, loss_t)
    gw_in, ((ex_o,), (g_up,), (gathered, gathered_rel)) = _tn_matmul(
        h1, dproj, name="grad_w_in", tm=1024, tn=2 * IN_CHUNK, tk=min(4096, tokens), chunk=IN_CHUNK,
        vmem_mb=56, jobs=[_exchange_job([sum_o]), _join_job([full_up]), _small_job([packed, packed_rel])])
    full_o = _add_chips(sum_o, ex_o, where, name="add_chips_w_o", tr=512)

    (g_o,), (got_in,) = _run_jobs("tail_swap", [_join_job([full_o]), _swap_job([gw_in])])
    sum_in = _add_half(gw_in, got_in, where, name="add_half_w_in", tr=512)
    ((ex_in,),) = _run_jobs("tail_exchange", [_exchange_job([sum_in])])
    full_in = _add_chips(sum_in, ex_in, where, name="add_chips_w_in", tr=512)
    ((g_in,),) = _run_jobs("tail_join", [_join_job([full_in])])

    big = [[t[None] for t in _adamw(w[0], g, m[0], v[0], name="adamw_" + nm, tr=512)] for nm, w, g, m, v in (
        ("w_in", w_in, g_in, m_w_in, v_w_in), ("w_o", w_o, g_o, m_w_o, v_w_o),
        ("w_up", w_ffn_up, g_up, m_w_ffn_up, v_w_ffn_up), ("w_down", w_ffn_down, g_down, m_w_ffn_down, v_w_ffn_down))]

    loss, small = _small_adamw(
        gathered, gathered_rel,
        (g1, g2, g3, g4, q_norm_a, k_norm_a, sink_b, rel_bias),
        (m_g_pre_mix, m_g_post_mix, m_g_pre_ffn, m_g_post_ffn, m_q_norm_a, m_k_norm_a, m_sink_b, m_rel_bias),
        (v_g_pre_mix, v_g_post_mix, v_g_pre_ffn, v_g_post_ffn, v_q_norm_a, v_k_norm_a, v_sink_b, v_rel_bias))
    s_g1, s_g2, s_g3, s_g4, s_gq, s_gk, s_sink, s_rel = small

    def leaves(i):
        return (big[0][i], big[1][i], s_g1[i], s_g2[i], s_gq[i], s_gk[i], s_sink[i], s_rel[i], s_g3[i],
                big[2][i], big[3][i], s_g4[i])

    loss = loss.reshape(())
    return (loss, grad_x.reshape(batch, seq, D_MODEL), *leaves(0), *leaves(1), *leaves(2), *leaves(3))
```

```python
import functools

import jax
import jax.numpy as jnp
import numpy as np
from jax import lax
from jax.experimental import pallas as pl
from jax.experimental.pallas import tpu as pltpu

F32 = jnp.float32
MM = jnp.bfloat16

D_MODEL = 1024
HEAD_DIM = 64
N_KV = 2
GROUP = 4
Q_WIDTH = 512
KV_WIDTH = 128
D_FF = 4096
GRID_W = 64
BLOCK = 128
N_BUCKETS = 32
MAX_DISTANCE = 128
ROPE_THETA = 10000.0
EPS = 1e-6
NEG_INF = -1e30
SCALE = HEAD_DIM ** -0.5
IN_TOTAL = 1536
N_CHIPS = 4
N_DEV = 8
IN_CHUNK = IN_TOTAL // N_CHIPS
FF_CHUNK = D_FF // N_CHIPS
O_CHUNK = D_MODEL // N_CHIPS
QK_RAW = 640

ADAM_LR = 0.001
ADAM_B1 = 0.9
ADAM_B2 = 0.999
ADAM_EPS = 1e-08
ADAM_WD = 0.01
ADAM_STEP = 10

LANES = 128
MESH = pl.DeviceIdType.MESH
HBM = pl.BlockSpec(memory_space=pl.ANY)
SMEM = pl.BlockSpec(memory_space=pltpu.SMEM)

ROW_G1, ROW_G2, ROW_G3, ROW_G4, ROW_MISC = 0, 1, 2, 3, 4
MISC_GQ, MISC_GK, MISC_SINK, MISC_LOSS = 0, 64, 128, 512


def _cparams(sem, vmem_mb):
    return pltpu.CompilerParams(dimension_semantics=sem, vmem_limit_bytes=vmem_mb * 1024 * 1024)


def _whole(a):
    return pl.BlockSpec(a.shape, lambda i: (0,) * len(a.shape))


def _from_hbm(a):
    return pltpu.with_memory_space_constraint(a, pltpu.HBM)


def _in_hbm(s):
    return pltpu.HBM(s.shape, s.dtype)


class _Job:
    def __init__(self, operands, out_shapes, sems, copies, alias=None):
        self.operands, self.out_shapes, self.sems, self.copies = list(operands), list(out_shapes), list(sems), copies
        self.alias = dict(alias or {})


def _place():
    return lax.axis_index("x"), lax.axis_index("y"), lax.axis_index("c")


_CHIP_FLIPS = ((1, 0), (0, 1), (1, 1))


def _flip(v, bit):
    return 1 - v if bit else v


def _remote(src, dst, send, recv, dev):
    return pltpu.make_async_remote_copy(src_ref=src, dst_ref=dst, send_sem=send, recv_sem=recv,
                                        device_id=dev, device_id_type=MESH)


def _swap_job(grads):
    n = len(grads)

    def copies(ins, outs, sems):
        x, y, c = _place()
        send, recv = sems
        cps = []
        for t in range(n):
            half = ins[t].shape[1] // 2
            cps.append(_remote(ins[t].at[:, pl.ds((1 - c) * half, half), :], outs[t], send.at[t], recv.at[t],
                               (x, y, 1 - c)))
        return cps

    shapes = [jax.ShapeDtypeStruct((g.shape[0], g.shape[1] // 2, g.shape[2]), F32) for g in grads]
    return _Job(grads, shapes, [pltpu.SemaphoreType.DMA((n,)), pltpu.SemaphoreType.DMA((n,))], copies)


def _exchange_job(sums):
    n = len(sums)

    def copies(ins, outs, sems):
        x, y, c = _place()
        send, recv = sems
        cps = []
        for t in range(n):
            for r, (fx, fy) in enumerate(_CHIP_FLIPS):
                kk = 2 * _flip(x, fx) + _flip(y, fy)
                cps.append(_remote(ins[t].at[kk], outs[t].at[r], send.at[t, r], recv.at[t, r],
                                   (_flip(x, fx), _flip(y, fy), c)))
        return cps

    shapes = [jax.ShapeDtypeStruct((3,) + s.shape[1:], s.dtype) for s in sums]
    return _Job(sums, shapes, [pltpu.SemaphoreType.DMA((n, 3)), pltpu.SemaphoreType.DMA((n, 3))], copies)


def _join_job(fulls):
    n = len(fulls)

    def copies(ins, outs, sems):
        x, y, c = _place()
        send, recv = sems
        cps = []
        for t in range(n):
            half = ins[t].shape[0] // 2
            rows = pl.ds(c * half, half)
            cps.append(_remote(ins[t].at[rows], outs[t].at[rows], send.at[t], recv.at[t], (x, y, 1 - c)))
        return cps

    shapes = [jax.ShapeDtypeStruct(f.shape, f.dtype) for f in fulls]
    return _Job(fulls, shapes, [pltpu.SemaphoreType.DMA((n,)), pltpu.SemaphoreType.DMA((n,))], copies,
                alias={t: t for t in range(n)})


def _gather_job(bufs, forward):
    n = len(bufs)

    def copies(ins, outs, sems):
        x, y, c = _place()
        send, recv = sems
        cps = []
        for t in range(n):
            half = ins[t].shape[1] // 2
            rows = pl.ds(c * half, half)
            for r, (fx, fy) in enumerate(_CHIP_FLIPS):
                if forward:
                    kk = 2 * _flip(x, fx) + _flip(y, fy)
                    dev = (x, y, 1 - c)
                else:
                    kk = 2 * x + y
                    dev = (_flip(x, fx), _flip(y, fy), c)
                cps.append(_remote(ins[t].at[kk, rows], outs[t].at[kk, rows], send.at[t, r], recv.at[t, r], dev))
        return cps

    shapes = [jax.ShapeDtypeStruct(b.shape, b.dtype) for b in bufs]
    return _Job(bufs, shapes, [pltpu.SemaphoreType.DMA((n, 3)), pltpu.SemaphoreType.DMA((n, 3))], copies,
                alias={t: t for t in range(n)})


def _call(body, args, *, name, grid, in_specs, out_specs, out_shape, scratch_shapes=(), params=None, jobs=()):
    n_in, n_out, n_scr = len(in_specs), len(out_specs), len(scratch_shapes)
    job_in = [len(j.operands) for j in jobs]
    job_out = [len(j.out_shapes) for j in jobs]
    job_sem = [len(j.sems) for j in jobs]

    def wrapped(*refs):
        pos = 0
        ins = refs[pos:pos + n_in]; pos += n_in
        jins = []
        for k in job_in:
            jins.append(refs[pos:pos + k]); pos += k
        outs = refs[pos:pos + n_out]; pos += n_out
        jouts = []
        for k in job_out:
            jouts.append(refs[pos:pos + k]); pos += k
        scr = refs[pos:pos + n_scr]; pos += n_scr
        jsems = []
        for k in job_sem:
            jsems.append(refs[pos:pos + k]); pos += k
        if jobs:
            ids = [pl.program_id(d) for d in range(len(grid))]
            first = functools.reduce(jnp.logical_and, [i == 0 for i in ids])
            last = functools.reduce(jnp.logical_and, [i == g - 1 for i, g in zip(ids, grid)])

            @pl.when(first)
            def _():
                for j, ji, jo, js in zip(jobs, jins, jouts, jsems):
                    for cp in j.copies(ji, jo, js):
                        cp.start()

        body(*ins, *outs, *scr)
        if jobs:
            @pl.when(last)
            def _():
                for j, ji, jo, js in zip(jobs, jins, jouts, jsems):
                    for cp in j.copies(ji, jo, js):
                        cp.wait()

    aliases = {}
    in_pos, out_pos = n_in, n_out
    for j in jobs:
        for i, o in j.alias.items():
            aliases[in_pos + i] = out_pos + o
        in_pos += len(j.operands)
        out_pos += len(j.out_shapes)
    res = pl.pallas_call(
        wrapped, name=name, grid=grid,
        in_specs=list(in_specs) + [HBM] * sum(job_in),
        out_specs=list(out_specs) + [HBM] * sum(job_out),
        out_shape=list(out_shape) + [_in_hbm(s) for j in jobs for s in j.out_shapes],
        scratch_shapes=list(scratch_shapes) + [s for j in jobs for s in j.sems],
        input_output_aliases=aliases,
        compiler_params=params,
    )(*[a if spec is SMEM else _from_hbm(a) for a, spec in zip(args, in_specs)],
      *[a for j in jobs for a in j.operands])
    own, rest = list(res[:n_out]), list(res[n_out:])
    job_res = []
    for k in job_out:
        job_res.append(rest[:k])
        rest = rest[k:]
    return own, job_res


def _run_jobs(name, jobs):
    def body():
        pass

    return _call(body, (), name=name, grid=(1,), in_specs=[], out_specs=[], out_shape=[], jobs=jobs)[1]


def _dot(a, b):
    return jnp.dot(a, b, preferred_element_type=F32)


def _dot_nt(a, b):
    return lax.dot_general(a, b, (((1,), (1,)), ((), ())), preferred_element_type=F32)


def _dot_tn(a, b):
    return lax.dot_general(a, b, (((0,), (0,)), ((), ())), preferred_element_type=F32)


def _rms_r(x):
    return lax.rsqrt(jnp.mean(x * x, axis=-1, keepdims=True) + EPS)


def _rms_bwd(x, r, g, dy):
    n = x * r
    dn = dy * g
    dx = r * (dn - n * jnp.mean(dn * n, axis=-1, keepdims=True))
    return dx, dy * n


def _seg64_sum(v):
    rows, width = v.shape
    lane = lax.broadcasted_iota(jnp.int32, (rows, LANES), 1)
    lo = lane < HEAD_DIM
    outs = []
    for c in range(width // LANES):
        ch = v[:, c * LANES:(c + 1) * LANES]
        s_lo = jnp.sum(jnp.where(lo, ch, 0.0), axis=-1, keepdims=True)
        s_hi = jnp.sum(jnp.where(lo, 0.0, ch), axis=-1, keepdims=True)
        outs.append(jnp.where(lo, s_lo, s_hi))
    return outs[0] if len(outs) == 1 else jnp.concatenate(outs, axis=-1)


def _head_r(v):
    return lax.rsqrt(_seg64_sum(v * v) * (1.0 / HEAD_DIM) + EPS)


def _swap16(ch):
    lane = lax.broadcasted_iota(jnp.int32, ch.shape, 1)
    return jnp.where((lane % 32) < 16, pltpu.roll(ch, LANES - 16, 1), pltpu.roll(ch, 16, 1))


def _by_chunk(fn, v):
    outs = [fn(v[:, c * LANES:(c + 1) * LANES]) for c in range(v.shape[1] // LANES)]
    return outs[0] if len(outs) == 1 else jnp.concatenate(outs, axis=-1)


def _rope(v, cos, sin_signed):
    return _by_chunk(lambda ch: ch * cos + _swap16(ch) * sin_signed, v)


def _rope_t(g, cos, sin_signed):
    return _by_chunk(lambda ch: ch * cos + _swap16(ch * sin_signed), g)


def _rope_tables(seq):
    nf = HEAD_DIM // 4
    freqs = ROPE_THETA ** (-jnp.arange(nf, dtype=F32) / nf)
    pos = jnp.arange(seq, dtype=jnp.int32)
    row = (pos // GRID_W).astype(F32)
    col = (pos % GRID_W).astype(F32)
    ang_r = row[:, None] * freqs[None, :]
    ang_c = col[:, None] * freqs[None, :]
    cr, sr, cc, sc = jnp.cos(ang_r), jnp.sin(ang_r), jnp.cos(ang_c), jnp.sin(ang_c)
    cos = jnp.concatenate([cr, cr, cc, cc], axis=1)
    sin = jnp.concatenate([-sr, sr, -sc, sc], axis=1)
    return cos, sin


def _t5_bucket(rel):
    nb = N_BUCKETS // 2
    ret = (rel > 0).astype(jnp.int32) * nb
    n = jnp.abs(rel)
    max_exact = nb // 2
    nf = jnp.maximum(n, 1).astype(jnp.float32)
    large = max_exact + (jnp.log(nf / max_exact) / np.float32(np.log(MAX_DISTANCE / max_exact))
                         * (nb - max_exact)).astype(jnp.int32)
    large = jnp.minimum(large, nb - 1)
    return ret + jnp.where(n < max_exact, n, large)


def _window_tables():
    a = jnp.arange(BLOCK, dtype=jnp.int32)
    c = jnp.arange(3 * BLOCK, dtype=jnp.int32)
    rel = c[None, :] - BLOCK - a[:, None]
    bucket = _t5_bucket(rel)
    band = (jnp.abs(rel) <= BLOCK).astype(jnp.int32)
    to3 = lambda t: t.reshape(BLOCK, 3, BLOCK).transpose(1, 2, 0)
    return to3(bucket), to3(band)


def _pre_proj(x, g1, w_in, gq, gk, ck, sk, *, seq, tm, sub):
    tokens = x.shape[0]
    n_seq = seq // tm
    nblk = tm // BLOCK
    batch = tokens // seq

    def body(x_ref, g1_ref, w_ref, gq_ref, gk_ref, ck_ref, sk_ref,
             h1_ref, raw_ref, qa_ref, ka_ref, kta_ref, va_ref, vta_ref,
             qtb_ref, kb_ref, ktb_ref, vb_ref, vtb_ref, proj):
        for r in range(tm // sub):
            rows = slice(r * sub, (r + 1) * sub)
            xv = x_ref[rows, :]
            h = (xv * _rms_r(xv) * g1_ref[...]).astype(MM)
            h1_ref[rows, :] = h
            for j in range(N_CHIPS):
                proj[rows, j * IN_CHUNK:(j + 1) * IN_CHUNK] = _dot(h, w_ref[j])
            qa = proj[rows, 0:Q_WIDTH]
            ka = proj[rows, Q_WIDTH:QK_RAW]
            raw_ref[rows, :] = proj[rows, 0:QK_RAW]
            qn = qa * _head_r(qa) * gq_ref[...]
            qa_ref[rows, :] = (_rope(qn, ck_ref[rows, :], sk_ref[rows, :]) * SCALE).astype(MM)
            kn = ka * _head_r(ka) * gk_ref[...]
            kr = _rope(kn, ck_ref[rows, :], sk_ref[rows, :])
            ka_ref[rows, :] = kr.astype(MM)
            kta_ref[0, :, rows] = kr.T.astype(MM)
            va = proj[rows, 640:768]
            va_ref[rows, :] = va.astype(MM)
            vta_ref[0, :, rows] = va.T.astype(MM)
            qb = proj[rows, 768:1280] * SCALE
            kb = proj[rows, 1280:1408]
            vb = proj[rows, 1408:1536]
            kb_ref[rows, :] = kb.astype(MM)
            vb_ref[rows, :] = vb.astype(MM)
            for j in range(sub // BLOCK):
                blk = slice(j * BLOCK, (j + 1) * BLOCK)
                qtb_ref[r * (sub // BLOCK) + j] = qb[blk, :].T.astype(MM)
                ktb_ref[r * (sub // BLOCK) + j] = kb[blk, :].T.astype(MM)
                vtb_ref[r * (sub // BLOCK) + j] = vb[blk, :].T.astype(MM)

    tok = lambda w: pl.BlockSpec((tm, w), lambda i: (i, 0))
    tab = lambda w: pl.BlockSpec((tm, w), lambda i: (i % n_seq, 0))
    row = lambda w: pl.BlockSpec((1, w), lambda i: (0, 0))
    tposed = pl.BlockSpec((1, LANES, tm), lambda i: (i // n_seq, 0, i % n_seq))
    blocks = pl.BlockSpec((nblk, BLOCK, LANES), lambda i: (i, 0, 0))
    qblocks = pl.BlockSpec((nblk, Q_WIDTH, BLOCK), lambda i: (i, 0, 0))
    tok_mm = lambda w: jax.ShapeDtypeStruct((tokens, w), MM)
    return pl.pallas_call(
        body, name="pre_proj",
        grid=(tokens // tm,),
        in_specs=[tok(D_MODEL), row(D_MODEL),
                  pl.BlockSpec((N_CHIPS, D_MODEL, IN_CHUNK), lambda i: (0, 0, 0)),
                  row(Q_WIDTH), row(KV_WIDTH), tab(KV_WIDTH), tab(KV_WIDTH)],
        out_specs=[tok(D_MODEL), tok(QK_RAW), tok(Q_WIDTH), tok(KV_WIDTH), tposed, tok(KV_WIDTH), tposed,
                   qblocks, tok(KV_WIDTH), blocks, tok(KV_WIDTH), blocks],
        out_shape=[
            tok_mm(D_MODEL),
            jax.ShapeDtypeStruct((tokens, QK_RAW), F32),
            tok_mm(Q_WIDTH),
            tok_mm(KV_WIDTH),
            jax.ShapeDtypeStruct((batch, KV_WIDTH, seq), MM),
            tok_mm(KV_WIDTH),
            jax.ShapeDtypeStruct((batch, KV_WIDTH, seq), MM),
            jax.ShapeDtypeStruct((tokens // BLOCK, Q_WIDTH, BLOCK), MM),
            tok_mm(KV_WIDTH),
            jax.ShapeDtypeStruct((tokens // BLOCK, KV_WIDTH, BLOCK), MM),
            tok_mm(KV_WIDTH),
            jax.ShapeDtypeStruct((tokens // BLOCK, KV_WIDTH, BLOCK), MM),
        ],
        scratch_shapes=[pltpu.VMEM((tm, IN_TOTAL), F32)],
        compiler_params=_cparams(("parallel",), 48),
    )(*map(_from_hbm, (x, g1, w_in, gq, gk, ck, sk)))


def _kv_half(v2, kv):
    return jnp.where(kv == 0, v2[:, :HEAD_DIM], v2[:, HEAD_DIM:])


def _attn_a_fwd(qa, kta, va, *, seq, bq, jobs=()):
    tokens = qa.shape[0]
    batch = tokens // seq
    nq = seq // bq

    def body(q_ref, kt_ref, v_ref, o_ref, p_ref, linv_ref):
        kv = pl.program_id(1)
        kt = kt_ref[0]
        lane = lax.broadcasted_iota(jnp.int32, (seq, KV_WIDTH), 1)
        v = jnp.where((lane < HEAD_DIM) == (kv == 0), v_ref[...], jnp.ones((), MM))
        q_rows = jnp.concatenate([q_ref[:, g * HEAD_DIM:(g + 1) * HEAD_DIM] for g in range(GROUP)], axis=0)
        s_all = _dot(q_rows, kt)
        for g in range(GROUP):
            sl = slice(g * HEAD_DIM, (g + 1) * HEAD_DIM)
            s = s_all[g * bq:(g + 1) * bq]
            pb = jnp.exp((s - jnp.max(s, axis=-1, keepdims=True)).astype(MM))
            p_ref[0, g] = pb
            o2 = _dot(pb, v)
            linv = 1.0 / _kv_half(o2, 1 - kv)[:, 0:1]
            o_ref[:, sl] = _kv_half(o2, kv) * linv
            linv_ref[0, :, g:g + 1] = linv

    return _call(
        body, (qa, kta, va), name="attn_a_fwd", jobs=jobs,
        grid=(batch, N_KV, nq),
        in_specs=[pl.BlockSpec((bq, GROUP * HEAD_DIM), lambda b, k, i: (b * nq + i, k)),
                  pl.BlockSpec((1, HEAD_DIM, seq), lambda b, k, i: (b, k, 0)),
                  pl.BlockSpec((seq, KV_WIDTH), lambda b, k, i: (b, 0))],
        out_specs=[pl.BlockSpec((bq, GROUP * HEAD_DIM), lambda b, k, i: (b * nq + i, k)),
                   pl.BlockSpec((1, GROUP, bq, seq), lambda b, k, i: (k, 0, b * nq + i, 0)),
                   pl.BlockSpec((1, bq, GROUP), lambda b, k, i: (k, b * nq + i, 0))],
        out_shape=[jax.ShapeDtypeStruct((tokens, Q_WIDTH), F32),
                   jax.ShapeDtypeStruct((N_KV, GROUP, tokens, seq), MM),
                   jax.ShapeDtypeStruct((N_KV, tokens, GROUP), F32)],
        params=_cparams(("arbitrary", "arbitrary", "arbitrary"), 56))


def _attn_a_bwd(qa, ka, vta, do, o, p, linv, *, seq, bq, kt_tile, jobs=()):
    tokens = qa.shape[0]
    batch = tokens // seq
    nq = seq // bq

    def body(q_ref, k_ref, vt_ref, do_ref, o_ref, p_ref, linv_ref, dq_ref, dkt_ref, dvt_ref):
        kv = pl.program_id(1)

        @pl.when(pl.program_id(2) == 0)
        def _():
            dkt_ref[...] = jnp.zeros_like(dkt_ref)
            dvt_ref[...] = jnp.zeros_like(dvt_ref)

        vt = vt_ref[0]
        k2 = k_ref[...]
        for g in range(GROUP):
            sl = slice(g * HEAD_DIM, (g + 1) * HEAD_DIM)
            dof = do_ref[:, sl]
            delta = jnp.sum(dof * o_ref[:, sl], axis=-1, keepdims=True)
            linv_g = linv_ref[0, :, g:g + 1]
            don = (dof * linv_g).astype(MM)
            delta_n = delta * linv_g
            dq2 = jnp.zeros((bq, KV_WIDTH), F32)
            for t in range(seq // kt_tile):
                keys = slice(t * kt_tile, (t + 1) * kt_tile)
                pb = p_ref[0, g, :, keys]
                ds = pb * (_dot(don, vt[:, keys]) - delta_n).astype(MM)
                dq2 = dq2 + _dot(ds, k2[keys, :])
                dkt_ref[0, :, keys] += _dot_tn(q_ref[:, sl], ds)
                dvt_ref[0, :, keys] += _dot_tn(don, pb)
            dq_ref[:, sl] = _kv_half(dq2, kv)

    qspec = pl.BlockSpec((bq, GROUP * HEAD_DIM), lambda b, k, i: (b * nq + i, k))
    tspec = pl.BlockSpec((1, HEAD_DIM, seq), lambda b, k, i: (b, k, 0))
    return _call(
        body, (qa, ka, vta, do, o, p, linv), name="attn_a_bwd", jobs=jobs,
        grid=(batch, N_KV, nq),
        in_specs=[qspec, pl.BlockSpec((seq, KV_WIDTH), lambda b, k, i: (b, 0)), tspec, qspec, qspec,
                  pl.BlockSpec((1, GROUP, bq, seq), lambda b, k, i: (k, 0, b * nq + i, 0)),
                  pl.BlockSpec((1, bq, GROUP), lambda b, k, i: (k, b * nq + i, 0))],
        out_specs=[qspec, tspec, tspec],
        out_shape=[jax.ShapeDtypeStruct((tokens, Q_WIDTH), F32),
                   jax.ShapeDtypeStruct((batch, KV_WIDTH, seq), F32),
                   jax.ShapeDtypeStruct((batch, KV_WIDTH, seq), F32)],
        params=_cparams(("arbitrary", "arbitrary", "arbitrary"), 56))


def _bias_build(rel_bias_t, bucket_t, band_t):
    def body(tab_ref, bucket_ref, band_ref, bias_ref):
        for h in range(GROUP * N_KV):
            for piece in range(3):
                bk = bucket_ref[piece]
                acc = jnp.zeros((BLOCK, BLOCK), F32)
                for b in range(N_BUCKETS):
                    acc = jnp.where(bk == b, tab_ref[h, b], acc)
                g = h % GROUP
                bias_ref[h // GROUP, piece, :, g * BLOCK:(g + 1) * BLOCK] = jnp.where(band_ref[piece] != 0, acc, NEG_INF)

    out = jax.ShapeDtypeStruct((N_KV, 3, BLOCK, GROUP * BLOCK), F32)
    return pl.pallas_call(
        body, name="bias_build", grid=(1,),
        in_specs=[SMEM, _whole(bucket_t), _whole(band_t)], out_specs=_whole(out), out_shape=out,
    )(rel_bias_t, bucket_t, band_t)


def _pad_heads(t, kv):
    outs = []
    for g in range(GROUP):
        tg = t[g * HEAD_DIM:(g + 1) * HEAD_DIM, :]
        zero = jnp.zeros_like(tg)
        outs.append(jnp.concatenate([jnp.where(kv == 0, tg, zero), jnp.where(kv == 0, zero, tg)], axis=0))
    return jnp.concatenate(outs, axis=-1)


def _unpad_heads(t, kv):
    outs = [_kv_half(t[:, g * BLOCK:(g + 1) * BLOCK].T, kv) for g in range(GROUP)]
    return jnp.concatenate(outs, axis=-1)


def _sink_row(sink_ref, kv):
    lane_head = lax.broadcasted_iota(jnp.int32, (1, GROUP * BLOCK), 1) // BLOCK
    row = jnp.zeros((1, GROUP * BLOCK), F32)
    for g in range(GROUP):
        row = jnp.where(lane_head == g, sink_ref[0, kv * GROUP + g], row)
    return row


def _rows3(ref, idx):
    return jnp.concatenate([ref[i] for i in idx], axis=0)


def _lanes3(ref, idx):
    return jnp.concatenate([ref[i] for i in idx], axis=-1)


def _window_scores_t(k_ref, idx, qpad, bias_ref, n, nblk):
    s_all = _dot(_rows3(k_ref, idx), qpad)
    pieces = []
    for piece in range(3):
        s = s_all[piece * BLOCK:(piece + 1) * BLOCK] + bias_ref[0, piece]
        if piece == 0:
            s = jnp.where(n > 0, s, NEG_INF)
        if piece == 2:
            s = jnp.where(n < nblk - 1, s, NEG_INF)
        pieces.append(s)
    return pieces


def _attn_b_fwd(qtb, kb3, vtb, bias, sink, *, seq, per_step, jobs=()):
    nblk_all = qtb.shape[0]
    tokens = nblk_all * BLOCK
    batch = tokens // seq
    nblk = seq // BLOCK
    nstep = nblk // per_step

    def body(sink_ref, q_ref, k_ref, vt_ref, bias_ref, o_ref, p_ref, stat_ref):
        kv = pl.program_id(0)
        first = pl.program_id(2) * per_step
        sink_row = _sink_row(sink_ref, kv)
        stat_row = lax.broadcasted_iota(jnp.int32, (8, GROUP * BLOCK), 0)

        def block(i, carry):
            n = first + i
            idx = (jnp.maximum(n - 1, 0), n, jnp.minimum(n + 1, nblk - 1))
            rows = slice(i * BLOCK, (i + 1) * BLOCK)
            qpad = _pad_heads(q_ref[n], kv)
            ss = _window_scores_t(k_ref, idx, qpad, bias_ref, n, nblk)
            m = jnp.maximum(jnp.maximum(jnp.max(ss[0], axis=0, keepdims=True),
                                        jnp.max(ss[1], axis=0, keepdims=True)),
                            jnp.maximum(jnp.max(ss[2], axis=0, keepdims=True), sink_row))
            ps = [jnp.exp(s - m) for s in ss]
            e_sink = jnp.exp(sink_row - m)
            rinv = 1.0 / (jnp.sum(ps[0], axis=0, keepdims=True) + jnp.sum(ps[1], axis=0, keepdims=True)
                          + jnp.sum(ps[2], axis=0, keepdims=True) + e_sink)
            pbs = [p.astype(MM) for p in ps]
            for piece in range(3):
                p_ref[0, i, piece] = pbs[piece]
            ot = _dot(_lanes3(vt_ref, idx), jnp.concatenate(pbs, axis=0))
            o_ref[rows, :] = _unpad_heads(ot * rinv, kv)
            stat_ref[0, i] = jnp.where(stat_row == 0, rinv, e_sink * rinv)
            return carry

        for i in range(per_step):
            block(i, 0)

    both = pl.BlockSpec((nblk, BLOCK, KV_WIDTH), lambda k, b, j: (b, 0, 0))
    return _call(
        body, (sink, qtb, kb3, vtb, bias), name="attn_b_fwd", jobs=jobs,
        grid=(N_KV, batch, nstep),
        in_specs=[SMEM, pl.BlockSpec((nblk, GROUP * HEAD_DIM, BLOCK), lambda k, b, j: (b, k, 0)), both, both,
                  pl.BlockSpec((1, 3, BLOCK, GROUP * BLOCK), lambda k, b, j: (k, 0, 0, 0))],
        out_specs=[pl.BlockSpec((per_step * BLOCK, GROUP * HEAD_DIM), lambda k, b, j: (b * nstep + j, k)),
                   pl.BlockSpec((1, per_step, 3, BLOCK, GROUP * BLOCK), lambda k, b, j: (k, b * nstep + j, 0, 0, 0)),
                   pl.BlockSpec((1, per_step, 8, GROUP * BLOCK), lambda k, b, j: (k, b * nstep + j, 0, 0))],
        out_shape=[jax.ShapeDtypeStruct((tokens, Q_WIDTH), F32),
                   jax.ShapeDtypeStruct((N_KV, nblk_all, 3, BLOCK, GROUP * BLOCK), MM),
                   jax.ShapeDtypeStruct((N_KV, nblk_all, 8, GROUP * BLOCK), F32)],
        params=_cparams(("arbitrary", "arbitrary", "arbitrary"), 48))


def _attn_b_bwd(qtb, ktb, vb3, do, o, p, stat, *, seq, per_step, jobs=()):
    nblk_all = qtb.shape[0]
    tokens = nblk_all * BLOCK
    batch = tokens // seq
    nblk = seq // BLOCK
    nstep = nblk // per_step

    def body(q_ref, kt_ref, v_ref, do_ref, o_ref, p_ref, stat_ref,
             dq_ref, dk_ref, dv_ref, dbias_ref, dsink_ref):
        kv = pl.program_id(0)
        step = pl.program_id(2)
        first = step * per_step

        @pl.when(jnp.logical_and(pl.program_id(1) == 0, step == 0))
        def _():
            dbias_ref[...] = jnp.zeros_like(dbias_ref)
            dsink_ref[...] = jnp.zeros_like(dsink_ref)

        @pl.when(step == 0)
        def _():
            dk_ref[...] = jnp.zeros_like(dk_ref)
            dv_ref[...] = jnp.zeros_like(dv_ref)

        def block(i, dsink):
            n = first + i
            idx = (jnp.maximum(n - 1, 0), n, jnp.minimum(n + 1, nblk - 1))
            rows = slice(i * BLOCK, (i + 1) * BLOCK)
            qpad = _pad_heads(q_ref[n], kv)
            dot_t = do_ref[rows, :].T
            prod = dot_t * o_ref[rows, :].T
            delta = jnp.concatenate(
                [jnp.sum(prod[g * HEAD_DIM:(g + 1) * HEAD_DIM, :], axis=0, keepdims=True) for g in range(GROUP)],
                axis=-1)
            stats = stat_ref[0, i]
            rinv, p_sink = stats[0:1, :], stats[1:2, :]
            dopad32 = _pad_heads(dot_t, kv)
            dopad = dopad32.astype(MM)
            dopad_n = (dopad32 * rinv).astype(MM)
            dpt = _dot(_rows3(v_ref, idx), dopad)
            pbs, dsbs = [], []
            for piece in range(3):
                pb = p_ref[0, i, piece]
                dst = pb.astype(F32) * ((dpt[piece * BLOCK:(piece + 1) * BLOCK] - delta) * rinv)
                dbias_ref[0, piece] += dst
                pbs.append(pb)
                dsbs.append(dst.astype(MM))
            ds_cat = jnp.concatenate(dsbs, axis=0)
            dq_ref[rows, :] = _unpad_heads(_dot(_lanes3(kt_ref, idx), ds_cat), kv)
            dk_all = _dot_nt(ds_cat, qpad)
            dv_all = _dot_nt(jnp.concatenate(pbs, axis=0), dopad_n)
            for piece in range(3):
                dk_ref[0, idx[piece]] += dk_all[piece * BLOCK:(piece + 1) * BLOCK]
                dv_ref[0, idx[piece]] += dv_all[piece * BLOCK:(piece + 1) * BLOCK]
            return dsink - p_sink * delta

        dsink = jnp.zeros((1, GROUP * BLOCK), F32)
        for i in range(per_step):
            dsink = block(i, dsink)
        dsink_ref[0] += jnp.broadcast_to(dsink, (8, GROUP * BLOCK))

    qspec = pl.BlockSpec((per_step * BLOCK, GROUP * HEAD_DIM), lambda k, b, j: (b * nstep + j, k))
    both = pl.BlockSpec((nblk, BLOCK, KV_WIDTH), lambda k, b, j: (b, 0, 0))
    grad =pl.BlockSpec((1, nblk, BLOCK, KV_WIDTH), lambda k, b, j: (k, b, 0, 0))
    return _call(
        body, (qtb, ktb, vb3, do, o, p, stat), name="attn_b_bwd", jobs=jobs,
        grid=(N_KV, batch, nstep),
        in_specs=[pl.BlockSpec((nblk, GROUP * HEAD_DIM, BLOCK), lambda k, b, j: (b, k, 0)), both, both,
                  qspec, qspec,
                  pl.BlockSpec((1, per_step, 3, BLOCK, GROUP * BLOCK), lambda k, b, j: (k, b * nstep + j, 0, 0, 0)),
                  pl.BlockSpec((1, per_step, 8, GROUP * BLOCK), lambda k, b, j: (k, b * nstep + j, 0, 0))],
        out_specs=[qspec, grad, grad,
                   pl.BlockSpec((1, 3, BLOCK, GROUP * BLOCK), lambda k, b, j: (k, 0, 0, 0)),
                   pl.BlockSpec((1, 8, GROUP * BLOCK), lambda k, b, j: (k, 0, 0))],
        out_shape=[jax.ShapeDtypeStruct((tokens, Q_WIDTH), F32),
                   jax.ShapeDtypeStruct((N_KV, nblk_all, BLOCK, KV_WIDTH), F32),
                   jax.ShapeDtypeStruct((N_KV, nblk_all, BLOCK, KV_WIDTH), F32),
                   jax.ShapeDtypeStruct((N_KV, 3, BLOCK, GROUP * BLOCK), F32),
                   jax.ShapeDtypeStruct((N_KV, 8, GROUP * BLOCK), F32)],
        params=_cparams(("arbitrary", "arbitrary", "arbitrary"), 56))


def _resident(shape):
    return pl.BlockSpec(shape, lambda i: (0,) * len(shape), pipeline_mode=pl.Buffered(1))


def _mix_ffn_fwd(oa, ob, w_o, x, g2, g3, w_up, w_down, target, g4, *, tm):
    tokens = x.shape[0]
    nt = tokens // tm

    def body(oa_ref, ob_ref, wo_ref, x_ref, g2_ref, g3_ref, wu_ref, wd_ref, t_ref, g4_ref,
             mix_ref, x1_ref, h2_ref, o_ref, u_ref, df_ref, dy_ref, loss_ref, dg4_ref):
        o = jnp.concatenate([oa_ref[...].astype(MM), ob_ref[...].astype(MM)], axis=-1)
        o_ref[...] = o
        mix = _dot(o, wo_ref[...])
        mix_ref[...] = mix
        x1 = x_ref[...] + mix * _rms_r(mix) * g2_ref[...]
        x1_ref[...] = x1
        h2v = (x1 * _rms_r(x1) * g3_ref[...]).astype(MM)
        h2_ref[...] = h2v
        f = jnp.zeros((tm, D_MODEL), F32)
        for c in range(N_CHIPS):
            u = jnp.maximum(_dot(h2v, wu_ref[c]), 0.0)
            u_ref[:, c * FF_CHUNK:(c + 1) * FF_CHUNK] = u.astype(MM)
            f = f + _dot((u * u).astype(MM), wd_ref[c * FF_CHUNK:(c + 1) * FF_CHUNK, :])
        r = _rms_r(f)
        g4v = g4_ref[...]
        err = x1 + f * r * g4v - t_ref[...]
        sq = jnp.sum(err * err, axis=-1, keepdims=True)
        loss_ref[0] = jnp.broadcast_to(jnp.sum(sq, axis=0, keepdims=True) * (0.5 / D_MODEL), (8, LANES))
        dy = err * (1.0 / D_MODEL)
        dy_ref[...] = dy
        dfv, dgv = _rms_bwd(f, r, g4v, dy)
        df_ref[...] = dfv.astype(MM)
        dg4_ref[0] = jnp.sum(dgv, axis=0, keepdims=True)

    tok = pl.BlockSpec((tm, D_MODEL), lambda i: (i, 0))
    half = pl.BlockSpec((tm, Q_WIDTH), lambda i: (i, 0))
    row = pl.BlockSpec((1, D_MODEL), lambda i: (0, 0))
    tok_f32 = jax.ShapeDtypeStruct((tokens, D_MODEL), F32)
    tok_mm = jax.ShapeDtypeStruct((tokens, D_MODEL), MM)
    return pl.pallas_call(
        body, name="mix_ffn_fwd",
        grid=(nt,),
        in_specs=[half, half, _resident((D_MODEL, D_MODEL)), tok, row, row,
                  _resident((N_CHIPS, D_MODEL, FF_CHUNK)), _resident((D_FF, D_MODEL)), tok, row],
        out_specs=[tok, tok, tok, tok, pl.BlockSpec((tm, D_FF), lambda i: (i, 0)), tok, tok,
                   pl.BlockSpec((1, 8, LANES), lambda i: (i, 0, 0)),
                   pl.BlockSpec((1, 1, D_MODEL), lambda i: (i, 0, 0))],
        out_shape=[tok_f32,
                   tok_f32,
                   tok_mm,
                   tok_mm,
                   jax.ShapeDtypeStruct((tokens, D_FF), MM),
                   tok_mm,
                   tok_f32,
                   jax.ShapeDtypeStruct((nt, 8, LANES), F32),
                   jax.ShapeDtypeStruct((nt, 1, D_MODEL), F32)],
        compiler_params=_cparams(("parallel",), 56),
    )(*map(_from_hbm, (oa, ob, w_o, x, g2, g3, w_up, w_down, target, g4)))


def _ffn_bwd_act(df, w_down, u, w_up, x1, dy, mix, g3, g2, w_o, *, tm):
    tokens = df.shape[0]
    nt = tokens // tm

    def body(df_ref, wd_ref, u_ref, wu_ref, x1_ref, dy_ref, mix_ref, g3_ref, g2_ref, wo_ref,
             dz_ref, dx1_ref, dmix_ref, dg3_ref, dg2_ref, doa_ref, dob_ref):
        dfv = df_ref[...]
        dh2 = jnp.zeros((tm, D_MODEL), F32)
        for c in range(N_CHIPS):
            cols = slice(c * FF_CHUNK, (c + 1) * FF_CHUNK)
            da = _dot_nt(dfv, wd_ref[cols, :])
            dz = (da * (2.0 * u_ref[:, cols].astype(F32))).astype(MM)
            dz_ref[:, cols] = dz
            dh2 = dh2 + _dot_nt(dz, wu_ref[c])
        x1 = x1_ref[...]
        dxn, dg3v = _rms_bwd(x1, _rms_r(x1), g3_ref[...], dh2)
        dx1 = dy_ref[...] + dxn
        dx1_ref[...] = dx1
        dg3_ref[0] = jnp.sum(dg3v, axis=0, keepdims=True)
        mix = mix_ref[...]
        dmix, dg2v = _rms_bwd(mix, _rms_r(mix), g2_ref[...], dx1)
        dmb = dmix.astype(MM)
        dmix_ref[...] = dmb
        dg2_ref[0] = jnp.sum(dg2v, axis=0, keepdims=True)
        doa_ref[...] = _dot_nt(dmb, wo_ref[0:Q_WIDTH, :])
        dob_ref[...] = _dot_nt(dmb, wo_ref[Q_WIDTH:D_MODEL, :])

    tok = pl.BlockSpec((tm, D_MODEL), lambda i: (i, 0))
    half = pl.BlockSpec((tm, Q_WIDTH), lambda i: (i, 0))
    wide = pl.BlockSpec((tm, D_FF), lambda i: (i, 0))
    row = pl.BlockSpec((1, D_MODEL), lambda i: (0, 0))
    part = pl.BlockSpec((1, 1, D_MODEL), lambda i: (i, 0, 0))
    return pl.pallas_call(
        body, name="ffn_bwd_act",
        grid=(nt,),
        in_specs=[tok, _resident((D_FF, D_MODEL)), wide, _resident((N_CHIPS, D_MODEL, FF_CHUNK)),
                  tok, tok, tok, row, row, _resident((D_MODEL, D_MODEL))],
        out_specs=[wide, tok, tok, part, part, half, half],
        out_shape=[jax.ShapeDtypeStruct((tokens, D_FF), MM),
                   jax.ShapeDtypeStruct((tokens, D_MODEL), F32),
                   jax.ShapeDtypeStruct((tokens, D_MODEL), MM),
                   jax.ShapeDtypeStruct((nt, 1, D_MODEL), F32),
                   jax.ShapeDtypeStruct((nt, 1, D_MODEL), F32),
                   jax.ShapeDtypeStruct((tokens, Q_WIDTH), F32),
                   jax.ShapeDtypeStruct((tokens, Q_WIDTH), F32)],
        compiler_params=_cparams(("parallel",), 56),
    )(*map(_from_hbm, (df, w_down, u, w_up, x1, dy, mix, g3, g2, w_o)))


def _tn_matmul(a, b, *, name, tm, tn, tk, chunk=None, square_a=False, vmem_mb=48, jobs=()):
    tokens, m_dim = a.shape
    n_dim = b.shape[1]
    chunked = chunk is not None
    if chunked:
        assert tm == m_dim and tn % chunk == 0

    def body(a_ref, b_ref, o_ref):
        av = a_ref[...]
        if square_a:
            av = av.astype(F32)
            av = av * av
        part = _dot_tn(av.astype(MM), b_ref[...].astype(MM))
        if chunked:
            part = jnp.stack([part[:, c * chunk:(c + 1) * chunk] for c in range(tn // chunk)])

        @pl.when(pl.program_id(2) == 0)
        def _():
            o_ref[...] = part

        @pl.when(pl.program_id(2) > 0)
        def _():
            o_ref[...] += part

    if chunked:
        out_spec = pl.BlockSpec((tn // chunk, tm, chunk), lambda i, j, k: (j, 0, 0))
        out_shape = jax.ShapeDtypeStruct((n_dim // chunk, m_dim, chunk), F32)
    else:
        out_spec = pl.BlockSpec((tm, tn), lambda i, j, k: (i, j))
        out_shape = jax.ShapeDtypeStruct((m_dim, n_dim), F32)
    (out,), job_res = _call(
        body, (a, b), name=name, jobs=jobs,
        grid=(m_dim // tm, n_dim // tn, tokens // tk),
        in_specs=[pl.BlockSpec((tk, tm), lambda i, j, k: (k, i)),
                  pl.BlockSpec((tk, tn), lambda i, j, k: (k, j))],
        out_specs=[out_spec], out_shape=[_in_hbm(out_shape)],
        params=_cparams(("arbitrary", "arbitrary", "arbitrary"), vmem_mb))
    return out, job_res


def _proj_bwd(dqa, dkta, dvta, dqb, dktb, dvtb, raw, x, dx1, g1, w_in, gq, gk, ck, sk, *, seq, tm, sub, jobs=()):
    tokens = x.shape[0]
    nt = tokens // tm
    n_seq = seq // tm
    nblk = tm // BLOCK

    def body(dqa_ref, dkta_ref, dvta_ref, dqb_ref, dkb_ref, dvb_ref, raw_ref, x_ref, dx1_ref, g1_ref, w_ref,
             gq_ref, gk_ref, ck_ref, sk_ref,
             gx_ref, dproj_ref, dg1_ref, dgq_ref, dgk_ref, dp):
        parts = []
        for r in range(tm // sub):
            rows = slice(r * sub, (r + 1) * sub)
            qa = raw_ref[rows, 0:Q_WIDTH]
            dqn = _rope_t(dqa_ref[rows, :], ck_ref[rows, :], sk_ref[rows, :]) * SCALE
            rq = _head_r(qa)
            nq = qa * rq
            dnq = dqn * gq_ref[...]
            dp[rows, 0:Q_WIDTH] = rq * (dnq - nq * (_seg64_sum(dnq * nq) * (1.0 / HEAD_DIM)))

            ka = raw_ref[rows, Q_WIDTH:QK_RAW]
            dkn = _rope_t(dkta_ref[0, :, rows].T, ck_ref[rows, :], sk_ref[rows, :])
            rk = _head_r(ka)
            nk = ka * rk
            dnk = dkn * gk_ref[...]
            dp[rows, 512:640] = rk * (dnk - nk * (_seg64_sum(dnk * nk) * (1.0 / HEAD_DIM)))

            dp[rows, 640:768] = dvta_ref[0, :, rows].T
            dp[rows, 768:1280] = dqb_ref[rows, :] * SCALE
            for j in range(r * sub // BLOCK, (r + 1) * sub // BLOCK):
                dp[j * BLOCK:(j + 1) * BLOCK, 1280:1408] = dkb_ref[0, j] + dkb_ref[1, j]
                dp[j * BLOCK:(j + 1) * BLOCK, 1408:1536] = dvb_ref[0, j] + dvb_ref[1, j]

            dproj = dp[rows, :].astype(MM)
            dproj_ref[rows, :] = dproj
            dh1 = _dot_nt(dproj[:, 0:IN_CHUNK], w_ref[0])
            for j in range(1, N_CHIPS):
                dh1 = dh1 + _dot_nt(dproj[:, j * IN_CHUNK:(j + 1) * IN_CHUNK], w_ref[j])
            xv = x_ref[rows, :]
            dxn, dg1v = _rms_bwd(xv, _rms_r(xv), g1_ref[...], dh1)
            gx_ref[rows, :] = dx1_ref[rows, :] + dxn
            parts.append((jnp.sum(dqn * nq, axis=0, keepdims=True), jnp.sum(dkn * nk, axis=0, keepdims=True),
                          jnp.sum(dg1v, axis=0, keepdims=True)))
        dgq_ref[0] = functools.reduce(jnp.add, [p[0] for p in parts])
        dgk_ref[0] = functools.reduce(jnp.add, [p[1] for p in parts])
        dg1_ref[0] = functools.reduce(jnp.add, [p[2] for p in parts])

    tok = lambda w: pl.BlockSpec((tm, w), lambda i: (i, 0))
    tab = lambda w: pl.BlockSpec((tm, w), lambda i: (i % n_seq, 0))
    row = lambda w: pl.BlockSpec((1, w), lambda i: (0, 0))
    tposed = pl.BlockSpec((1, KV_WIDTH, tm), lambda i: (i // n_seq, 0, i % n_seq))
    blocks = pl.BlockSpec((N_KV, nblk, BLOCK, KV_WIDTH), lambda i: (0, i, 0, 0))
    part = lambda w: pl.BlockSpec((1, 1, w), lambda i: (i, 0, 0))
    return _call(
        body, (dqa, dkta, dvta, dqb, dktb, dvtb, raw, x, dx1, g1, w_in, gq, gk, ck, sk),
        name="proj_bwd", jobs=jobs,
        grid=(nt,),
        in_specs=[tok(Q_WIDTH), tposed, tposed, tok(Q_WIDTH), blocks, blocks, tok(QK_RAW), tok(D_MODEL),
                  tok(D_MODEL), row(D_MODEL),
                  pl.BlockSpec((N_CHIPS, D_MODEL, IN_CHUNK), lambda i: (0, 0, 0)),
                  row(Q_WIDTH), row(KV_WIDTH), tab(KV_WIDTH), tab(KV_WIDTH)],
        out_specs=[tok(D_MODEL), tok(IN_TOTAL), part(D_MODEL), part(Q_WIDTH), part(KV_WIDTH)],
        out_shape=[jax.ShapeDtypeStruct((tokens, D_MODEL), F32),
                   jax.ShapeDtypeStruct((tokens, IN_TOTAL), MM),
                   jax.ShapeDtypeStruct((nt, 1, D_MODEL), F32),
                   jax.ShapeDtypeStruct((nt, 1, Q_WIDTH), F32),
                   jax.ShapeDtypeStruct((nt, 1, KV_WIDTH), F32)],
        scratch_shapes=[pltpu.VMEM((tm, IN_TOTAL), F32)],
        params=_cparams(("arbitrary",), 56))


def _pack_small(dg1, dg2, dg3, dg4, dgq, dgk, dsink, dbias, bucket, loss):
    def body(dg1_ref, dg2_ref, dg3_ref, dg4_ref, dgq_ref, dgk_ref, dsink_ref, dbias_ref, bucket_ref, loss_ref,
             out_ref, rel_ref):
        out_ref[...] = jnp.zeros_like(out_ref)
        for r, ref in ((ROW_G1, dg1_ref), (ROW_G2, dg2_ref), (ROW_G3, dg3_ref), (ROW_G4, dg4_ref)):
            acc = ref[0]
            for t in range(1, ref.shape[0]):
                acc = acc + ref[t]
            out_ref[r:r + 1, :] = acc

        def fold(ref, heads):
            acc = ref[0]
            for t in range(1, ref.shape[0]):
                acc = acc + ref[t]
            tot = acc[:, 0:HEAD_DIM]
            for h in range(1, heads):
                tot = tot + acc[:, h * HEAD_DIM:(h + 1) * HEAD_DIM]
            return tot

        out_ref[ROW_MISC:ROW_MISC + 1, MISC_GQ:MISC_GQ + HEAD_DIM] = fold(dgq_ref, GROUP * N_KV)
        out_ref[ROW_MISC:ROW_MISC + 1, MISC_GK:MISC_GK + HEAD_DIM] = fold(dgk_ref, N_KV)
        for h in range(GROUP * N_KV):
            g = h % GROUP
            out_ref[ROW_MISC:ROW_MISC + 1, MISC_SINK + h:MISC_SINK + h + 1] = jnp.sum(
                dsink_ref[h // GROUP, 0:1, g * BLOCK:(g + 1) * BLOCK], axis=-1, keepdims=True)
        lacc = loss_ref[0, 0:1, 0:1]
        for t in range(1, loss_ref.shape[0]):
            lacc = lacc + loss_ref[t, 0:1, 0:1]
        out_ref[ROW_MISC:ROW_MISC + 1, MISC_LOSS:MISC_LOSS + 1] = lacc
        lane = lax.broadcasted_iota(jnp.int32, (N_BUCKETS, LANES), 1)
        row = lax.broadcasted_iota(jnp.int32, (N_BUCKETS, LANES), 0)

        def per_bucket(b, acc):
            for h in range(GROUP * N_KV):
                g = h % GROUP
                sel = jnp.zeros((BLOCK, BLOCK), F32)
                for piece in range(3):
                    sel = sel + jnp.where(bucket_ref[piece] == b,
                                          dbias_ref[h // GROUP, piece, :, g * BLOCK:(g + 1) * BLOCK], 0.0)
                tot = jnp.sum(jnp.sum(sel, axis=0, keepdims=True), axis=-1, keepdims=True)
                acc = jnp.where((row == b) & (lane == h), tot, acc)
            return acc

        rel_ref[...] = lax.fori_loop(0, N_BUCKETS, per_bucket, jnp.zeros((N_BUCKETS, LANES), F32))

    args = (dg1, dg2, dg3, dg4, dgq, dgk, dsink, dbias, bucket, loss)
    outs = [jax.ShapeDtypeStruct((8, D_MODEL), F32), jax.ShapeDtypeStruct((N_BUCKETS, LANES), F32)]
    return pl.pallas_call(
        body, name="pack_small", grid=(1,),
        in_specs=[_whole(a) for a in args], out_specs=[_whole(o) for o in outs], out_shape=outs,
        compiler_params=pltpu.CompilerParams(vmem_limit_bytes=32 * 1024 * 1024),
    )(*map(_from_hbm, args))


def _gather_weights(shards, whole):
    n = len(shards)
    full = [t for t in range(n) if whole[t]]

    def body(*refs):
        ins, outs = refs[:n], refs[n:2 * n]
        raw, stage = refs[2 * n:3 * n], refs[3 * n:4 * n]
        load_sem, local_sem, ici_send, ici_recv, d2d_send, d2d_recv = refs[4 * n:]
        x, y, c = _place()
        k = 2 * x + y
        sibling = (x, y, 1 - c)
        order = full + [t for t in range(n) if t not in full]
        loads = {t: pltpu.make_async_copy(ins[t], raw[t], load_sem.at[t]) for t in order}
        for t in order:
            loads[t].start()
        copies, sends = [], []
        for t in order:
            loads[t].wait()
            stage[t][...] = raw[t][...].astype(MM)
            mine = pltpu.make_async_copy(stage[t], outs[t].at[k], local_sem.at[t])
            mine.start()
            copies.append(mine)
            if t in full:
                half = ins[t].shape[0] // 2
                rows = pl.ds(c * half, half)
                for r, (fx, fy) in enumerate(_CHIP_FLIPS):
                    cp = _remote(stage[t].at[rows], outs[t].at[k, rows], ici_send.at[t, r], ici_recv.at[t, r],
                                 (_flip(x, fx), _flip(y, fy), c))
                    cp.start()
                    sends.append(cp)
        for t in full:
            half = ins[t].shape[0] // 2
            rows = pl.ds(c * half, half)
            for r, (fx, fy) in enumerate(_CHIP_FLIPS):
                kk = 2 * _flip(x, fx) + _flip(y, fy)
                landed = outs[t].at[kk, rows]
                _remote(landed, landed, ici_send.at[t, r], ici_recv.at[t, r], sibling).wait_recv()
                fwd = _remote(landed, landed, d2d_send.at[t, r], d2d_recv.at[t, r], sibling)
                fwd.start()
                sends.append(fwd)
        for t in full:
            half = ins[t].shape[0] // 2
            other = pl.ds((1 - c) * half, half)
            for r, (fx, fy) in enumerate(_CHIP_FLIPS):
                kk = 2 * _flip(x, fx) + _flip(y, fy)
                theirs = outs[t].at[kk, other]
                _remote(theirs, theirs, d2d_send.at[t, r], d2d_recv.at[t, r], sibling).wait_recv()
        for cp in sends:
            cp.wait_send()
        for cp in copies:
            cp.wait()

    return pl.pallas_call(
        body, name="gather_weights",
        in_specs=[HBM] * n, out_specs=[HBM] * n,
        out_shape=[pltpu.HBM((N_CHIPS,) + s.shape, MM) for s in shards],
        scratch_shapes=[pltpu.VMEM(s.shape, F32) for s in shards] + [pltpu.VMEM(s.shape, MM) for s in shards] + [
            pltpu.SemaphoreType.DMA((n,)), pltpu.SemaphoreType.DMA((n,)),
            pltpu.SemaphoreType.DMA((n, 3)), pltpu.SemaphoreType.DMA((n, 3)),
            pltpu.SemaphoreType.DMA((n, 3)), pltpu.SemaphoreType.DMA((n, 3))],
        compiler_params=pltpu.CompilerParams(vmem_limit_bytes=40 * 1024 * 1024),
    )(*shards)


def _add_half(grad, got, where, *, name, tr):
    nch, half, cols = got.shape
    tr = min(tr, half)
    nblk = half // tr

    def body(where_ref, g_ref, r_ref, o_ref):
        o_ref[...] = (g_ref[...] + r_ref[...]).astype(MM)

    return pl.pallas_call(
        body, name=name,
        grid_spec=pltpu.PrefetchScalarGridSpec(
            num_scalar_prefetch=1, grid=(nch, nblk),
            in_specs=[pl.BlockSpec((1, tr, cols), lambda j, i, where_ref: (j, where_ref[1] * nblk + i, 0)),
                      pl.BlockSpec((1, tr, cols), lambda j, i, where_ref: (j, i, 0))],
            out_specs=pl.BlockSpec((1, tr, cols), lambda j, i, where_ref: (j, i, 0))),
        out_shape=jax.ShapeDtypeStruct(got.shape, MM),
        compiler_params=_cparams(("parallel", "parallel"), 32),
    )(where, grad, got)


def _add_chips(own, got, where, *, name, tr):
    _, half, cols = own.shape
    tr = min(tr, half)
    nblk = half // tr

    def body(where_ref, o_ref, g_ref, out_ref):
        f = lambda v: v.astype(F32)
        out_ref[...] = ((f(o_ref[0]) + f(g_ref[0])) + f(g_ref[1])) + f(g_ref[2])

    return pl.pallas_call(
        body, name=name,
        grid_spec=pltpu.PrefetchScalarGridSpec(
            num_scalar_prefetch=1, grid=(nblk,),
            in_specs=[pl.BlockSpec((1, tr, cols), lambda i, where_ref: (where_ref[0], i, 0)),
                      pl.BlockSpec((3, tr, cols), lambda i, where_ref: (0, i, 0))],
            out_specs=pl.BlockSpec((tr, cols), lambda i, where_ref: (where_ref[1] * nblk + i, 0))),
        out_shape=pltpu.HBM((2 * half, cols), F32),
        compiler_params=_cparams(("parallel",), 32),
    )(where, own, got)


def _small_job(tiles):
    n = len(tiles)

    def copies(ins, outs, sems):
        x, y, c = _place()
        me = 4 * x + 2 * y + c
        local, send, recv = sems
        cps = []
        for t in range(n):
            cps.append(pltpu.make_async_copy(ins[t], outs[t].at[me], local.at[t]))
            for r in range(1, N_DEV):
                fx, fy, fc = (r >> 2) & 1, (r >> 1) & 1, r & 1
                cps.append(_remote(ins[t], outs[t].at[me], send.at[t, r - 1], recv.at[t, r - 1],
                                   (_flip(x, fx), _flip(y, fy), _flip(c, fc))))
        return cps

    return _Job(tiles, [jax.ShapeDtypeStruct((N_DEV,) + t.shape, F32) for t in tiles],
                [pltpu.SemaphoreType.DMA((n,)), pltpu.SemaphoreType.DMA((n, N_DEV - 1)),
                 pltpu.SemaphoreType.DMA((n, N_DEV - 1))], copies)


def _adamw_math(w, g, m, v):
    m = ADAM_B1 * m + (1.0 - ADAM_B1) * g
    v = ADAM_B2 * v + (1.0 - ADAM_B2) * (g * g)
    m_hat = m / (1.0 - ADAM_B1 ** ADAM_STEP)
    v_hat = v / (1.0 - ADAM_B2 ** ADAM_STEP)
    delta = -ADAM_LR * (m_hat / (jnp.sqrt(v_hat) + ADAM_EPS) + ADAM_WD * w)
    return delta, m, v


def _adamw(w, g, m, v, *, name, tr):
    rows, cols = w.shape
    tr = min(tr, rows)

    def body(w_ref, g_ref, m_ref, v_ref, go_ref, d_ref, nm_ref, nv_ref):
        g = g_ref[...]
        go_ref[...] = g
        d_ref[...], nm_ref[...], nv_ref[...] = _adamw_math(w_ref[...], g, m_ref[...], v_ref[...])

    spec = pl.BlockSpec((tr, cols), lambda i: (i, 0))
    return pl.pallas_call(
        body, name=name,
        grid=(rows // tr,),
        in_specs=[spec] * 4, out_specs=[spec] * 4,
        out_shape=[jax.ShapeDtypeStruct(w.shape, F32)] * 4,
        compiler_params=_cparams(("parallel",), 48),
    )(w, g, m, v)


def _small_adamw(gathered, gathered_rel, params, moments_m, moments_v):
    n = len(params)

    def body(all_ref, rel_all_ref, *refs):
        w_refs, m_refs, v_refs = refs[:n], refs[n:2 * n], refs[2 * n:3 * n]
        loss_ref = refs[3 * n]
        out_refs = refs[3 * n + 1:]
        g = all_ref[0]
        rel = rel_all_ref[0]
        for d in range(1, N_DEV):
            g = g + all_ref[d]
            rel = rel + rel_all_ref[d]
        misc = g[ROW_MISC:ROW_MISC + 1]
        loss_ref[...] = misc[:, MISC_LOSS:MISC_LOSS + 1]
        grads = (g[ROW_G1:ROW_G1 + 1], g[ROW_G2:ROW_G2 + 1], g[ROW_G3:ROW_G3 + 1], g[ROW_G4:ROW_G4 + 1],
                 misc[:, MISC_GQ:MISC_GQ + HEAD_DIM], misc[:, MISC_GK:MISC_GK + HEAD_DIM],
                 misc[:, MISC_SINK:MISC_SINK + GROUP * N_KV], rel[:, 0:GROUP * N_KV])
        for i in range(n):
            d, nm, nv = _adamw_math(w_refs[i][...], grads[i], m_refs[i][...], v_refs[i][...])
            for j, val in enumerate((grads[i], d, nm, nv)):
                out_refs[4 * i + j][...] = val

    args = (gathered, gathered_rel, *params, *moments_m, *moments_v)
    out_shape = [jax.ShapeDtypeStruct((1, 1), F32)] + [jax.ShapeDtypeStruct(p.shape, F32) for p in params
                                                       for _ in range(4)]
    outs = pl.pallas_call(
        body, name="small_adamw", grid=(1,),
        in_specs=[_whole(a) for a in args], out_specs=[_whole(o) for o in out_shape], out_shape=out_shape,
    )(*map(_from_hbm, args))
    return outs[0], [outs[1 + 4 * i:5 + 4 * i] for i in range(n)]


def kernel(x, w_in, w_o, g_pre_mix, g_post_mix, q_norm_a, k_norm_a, sink_b, rel_bias, g_pre_ffn, w_ffn_up, w_ffn_down, g_post_ffn, loss_target, m_w_in, m_w_o, m_g_pre_mix, m_g_post_mix, m_q_norm_a, m_k_norm_a, m_sink_b, m_rel_bias, m_g_pre_ffn, m_w_ffn_up, m_w_ffn_down, m_g_post_ffn, v_w_in, v_w_o, v_g_pre_mix, v_g_post_mix, v_q_norm_a, v_k_norm_a, v_sink_b, v_rel_bias, v_g_pre_ffn, v_w_ffn_up, v_w_ffn_down, v_g_post_ffn):
    batch, seq, _ = x.shape
    tokens = batch * seq
    where = jnp.stack([2 * lax.axis_index("x") + lax.axis_index("y"), lax.axis_index("c")]).astype(jnp.int32)
    x2 = x.reshape(tokens, D_MODEL)
    g1, g2, g3, g4 = g_pre_mix, g_post_mix, g_pre_ffn, g_post_ffn

    cos, sin = _rope_tables(seq)
    ck, sk = jnp.tile(cos, (1, 2)), jnp.tile(sin, (1, 2))
    gq8, gk2 = jnp.tile(q_norm_a, (1, 8)), jnp.tile(k_norm_a, (1, 2))
    bucket, band = _window_tables()
    bias = _bias_build(rel_bias.T, bucket, band)

    w_in_g, w_o_p, w_up_p, w_down_p = _gather_weights(
        (w_in[0], w_o[0], w_ffn_up[0], w_ffn_down[0]), whole=(True, False, False, False))
    (h1, raw, qa, ka, kta, va, vta, qtb, kb, ktb, vb, vtb) = _pre_proj(
        x2, g1, w_in_g, gq8, gk2, ck, sk, seq=seq, tm=min(1024, seq), sub=256)
    (oa, p_a, linv_a), (w_part,) = _attn_a_fwd(
        qa, kta, va, seq=seq, bq=min(256, seq), jobs=[_gather_job([w_o_p, w_up_p, w_down_p], forward=False)])
    kb3 = kb.reshape(tokens // BLOCK, BLOCK, KV_WIDTH)
    vb3 = vb.reshape(tokens // BLOCK, BLOCK, KV_WIDTH)
    (ob, p_b, stat_b), ((w_o_g, w_up_g, w_down_g),) = _attn_b_fwd(
        qtb, kb3, vtb, bias, sink_b, seq=seq, per_step=min(16, seq // BLOCK),
        jobs=[_gather_job(w_part, forward=True)])
    w_o2 = w_o_g.reshape(D_MODEL, D_MODEL)
    w_down2 = w_down_g.reshape(D_FF, D_MODEL)
    mix, x1, h2, o_cat, u, df, dy, loss_t, dg4 = _mix_ffn_fwd(
        oa, ob, w_o2, x2, g2, g3, w_up_g, w_down2, loss_target.reshape(tokens, D_MODEL), g4, tm=256)

    dz, dx1, dmix, dg3, dg2, doa, dob = _ffn_bwd_act(df, w_down2, u, w_up_g, x1, dy, mix, g3, g2, w_o2, tm=256)
    gw_down, _ = _tn_matmul(u, df, name="grad_w_down", tm=1024, tn=1024, tk=min(4096, tokens), square_a=True,
                            vmem_mb=56)
    gw_down = gw_down.reshape(N_CHIPS, FF_CHUNK, D_MODEL)
    gw_up, ((got_down,),) = _tn_matmul(h2, dz, name="grad_w_up", tm=1024, tn=1024, tk=min(4096, tokens), chunk=FF_CHUNK,
                                        vmem_mb=56, jobs=[_swap_job([gw_down])])
    gw_o, _ = _tn_matmul(o_cat, dmix, name="grad_w_o", tm=1024, tn=1024, tk=min(2048, tokens))
    gw_o = gw_o.reshape(N_CHIPS, O_CHUNK, D_MODEL)
    sum_down = _add_half(gw_down, got_down, where, name="add_half_w_down", tr=512)
    (dqa, dkta, dvta), ((ex_down,), (got_up,)) = _attn_a_bwd(
        qa, ka, vta, doa, oa, p_a, linv_a, seq=seq, bq=min(256, seq), kt_tile=min(2048, seq),
        jobs=[_exchange_job([sum_down]), _swap_job([gw_up])])
    full_down = _add_chips(sum_down, ex_down, where, name="add_chips_w_down", tr=512)
    sum_up = _add_half(gw_up, got_up, where, name="add_half_w_up", tr=512)
    (dqb, dkb, dvb, dbias, dsink), ((ex_up,), (g_down,), (got_o,)) = _attn_b_bwd(
        qtb, ktb, vb3, dob, ob, p_b, stat_b, seq=seq, per_step=min(16, seq // BLOCK),
        jobs=[_exchange_job([sum_up]), _join_job([full_down]), _swap_job([gw_o])])
    full_up = _add_chips(sum_up, ex_up, where, name="add_chips_w_up", tr=512)
    sum_o = _add_half(gw_o, got_o, where, name="add_half_w_o", tr=512)
    (grad_x, dproj, dg1, dgq, dgk), _ = _proj_bwd(
        dqa, dkta, dvta, dqb, dkb, dvb, raw, x2, dx1, g1, w_in_g, gq8, gk2, ck, sk,
        seq=seq, tm=min(512, seq), sub=128)
    packed, packed_rel = _pack_small(dg1, dg2, dg3, dg4, dgq, dgk, dsink, dbias, bucket, loss_t)
    gw_in, ((ex_o,), (g_up,), (gathered, gathered_rel)) = _tn_matmul(
        h1, dproj, name="grad_w_in", tm=1024, tn=2 * IN_CHUNK, tk=min(4096, tokens), chunk=IN_CHUNK,
        vmem_mb=56, jobs=[_exchange_job([sum_o]), _join_job([full_up]), _small_job([packed, packed_rel])])
    full_o = _add_chips(sum_o, ex_o, where, name="add_chips_w_o", tr=512)

    (g_o,), (got_in,) = _run_jobs("tail_swap", [_join_job([full_o]), _swap_job([gw_in])])
    sum_in = _add_half(gw_in, got_in, where, name="add_half_w_in", tr=512)
    ((ex_in,),) = _run_jobs("tail_exchange", [_exchange_job([sum_in])])
    full_in = _add_chips(sum_in, ex_in, where, name="add_chips_w_in", tr=512)
    ((g_in,),) = _run_jobs("tail_join", [_join_job([full_in])])

    big = [[t[None] for t in _adamw(w[0], g, m[0], v[0], name="adamw_" + nm, tr=512)] for nm, w, g, m, v in (
        ("w_in", w_in, g_in, m_w_in, v_w_in), ("w_o", w_o, g_o, m_w_o, v_w_o),
        ("w_up", w_ffn_up, g_up, m_w_ffn_up, v_w_ffn_up), ("w_down", w_ffn_down, g_down, m_w_ffn_down, v_w_ffn_down))]

    loss, small = _small_adamw(
        gathered, gathered_rel,
        (g1, g2, g3, g4, q_norm_a, k_norm_a, sink_b, rel_bias),
        (m_g_pre_mix, m_g_post_mix, m_g_pre_ffn, m_g_post_ffn, m_q_norm_a, m_k_norm_a, m_sink_b, m_rel_bias),
        (v_g_pre_mix, v_g_post_mix, v_g_pre_ffn, v_g_post_ffn, v_q_norm_a, v_k_norm_a, v_sink_b, v_rel_bias))
    s_g1, s_g2, s_g3, s_g4, s_gq, s_gk, s_sink, s_rel = small

    def leaves(i):
        return (big[0][i], big[1][i], s_g1[i], s_g2[i], s_gq[i], s_gk[i], s_sink[i], s_rel[i], s_g3[i],
                big[2][i], big[3][i], s_g4[i])

    loss = loss.reshape(())
    return (loss, grad_x.reshape(batch, seq, D_MODEL), *leaves(0), *leaves(1), *leaves(2), *leaves(3))
```

```python
import functools

import jax
import jax.numpy as jnp
import numpy as np
from jax import lax
from jax.experimental import pallas as pl
from jax.experimental.pallas import tpu as pltpu

F32 = jnp.float32
MM = jnp.bfloat16

D_MODEL = 1024
HEAD_DIM = 64
N_KV = 2
GROUP = 4
Q_WIDTH = 512
KV_WIDTH = 128
D_FF = 4096
GRID_W = 64
BLOCK = 128
N_BUCKETS = 32
MAX_DISTANCE = 128
ROPE_THETA = 10000.0
EPS = 1e-6
NEG_INF = -1e30
SCALE = HEAD_DIM ** -0.5
IN_TOTAL = 1536
N_CHIPS = 4
N_DEV = 8
IN_CHUNK = IN_TOTAL // N_CHIPS
FF_CHUNK = D_FF // N_CHIPS
O_CHUNK = D_MODEL // N_CHIPS
QK_RAW = 640

ADAM_LR = 0.001
ADAM_B1 = 0.9
ADAM_B2 = 0.999
ADAM_EPS = 1e-08
ADAM_WD = 0.01
ADAM_STEP = 10

LANES = 128
MESH = pl.DeviceIdType.MESH
HBM = pl.BlockSpec(memory_space=pl.ANY)
SMEM = pl.BlockSpec(memory_space=pltpu.SMEM)

ROW_G1, ROW_G2, ROW_G3, ROW_G4, ROW_MISC = 0, 1, 2, 3, 4
MISC_GQ, MISC_GK, MISC_SINK, MISC_LOSS = 0, 64, 128, 512


def _cparams(sem, vmem_mb):
    return pltpu.CompilerParams(dimension_semantics=sem, vmem_limit_bytes=vmem_mb * 1024 * 1024)


def _whole(a):
    return pl.BlockSpec(a.shape, lambda i: (0,) * len(a.shape))


def _from_hbm(a):
    return pltpu.with_memory_space_constraint(a, pltpu.HBM)


def _in_hbm(s):
    return pltpu.HBM(s.shape, s.dtype)


class _Job:
    def __init__(self, operands, out_shapes, sems, copies, alias=None):
        self.operands, self.out_shapes, self.sems, self.copies = list(operands), list(out_shapes), list(sems), copies
        self.alias = dict(alias or {})


def _place():
    return lax.axis_index("x"), lax.axis_index("y"), lax.axis_index("c")


_CHIP_FLIPS = ((1, 0), (0, 1), (1, 1))


def _flip(v, bit):
    return 1 - v if bit else v


def _remote(src, dst, send, recv, dev):
    return pltpu.make_async_remote_copy(src_ref=src, dst_ref=dst, send_sem=send, recv_sem=recv,
                                        device_id=dev, device_id_type=MESH)


def _swap_job(grads):
    n = len(grads)

    def copies(ins, outs, sems):
        x, y, c = _place()
        send, recv = sems
        cps = []
        for t in range(n):
            half = ins[t].shape[1] // 2
            cps.append(_remote(ins[t].at[:, pl.ds((1 - c) * half, half), :], outs[t], send.at[t], recv.at[t],
                               (x, y, 1 - c)))
        return cps

    shapes = [jax.ShapeDtypeStruct((g.shape[0], g.shape[1] // 2, g.shape[2]), F32) for g in grads]
    return _Job(grads, shapes, [pltpu.SemaphoreType.DMA((n,)), pltpu.SemaphoreType.DMA((n,))], copies)


def _exchange_job(sums):
    n = len(sums)

    def copies(ins, outs, sems):
        x, y, c = _place()
        send, recv = sems
        cps = []
        for t in range(n):
            for r, (fx, fy) in enumerate(_CHIP_FLIPS):
                kk = 2 * _flip(x, fx) + _flip(y, fy)
                cps.append(_remote(ins[t].at[kk], outs[t].at[r], send.at[t, r], recv.at[t, r],
                                   (_flip(x, fx), _flip(y, fy), c)))
        return cps

    shapes = [jax.ShapeDtypeStruct((3,) + s.shape[1:], s.dtype) for s in sums]
    return _Job(sums, shapes, [pltpu.SemaphoreType.DMA((n, 3)), pltpu.SemaphoreType.DMA((n, 3))], copies)


def _join_job(fulls):
    n = len(fulls)

    def copies(ins, outs, sems):
        x, y, c = _place()
        send, recv = sems
        cps = []
        for t in range(n):
            half = ins[t].shape[0] // 2
            rows = pl.ds(c * half, half)
            cps.append(_remote(ins[t].at[rows], outs[t].at[rows], send.at[t], recv.at[t], (x, y, 1 - c)))
        return cps

    shapes = [jax.ShapeDtypeStruct(f.shape, f.dtype) for f in fulls]
    return _Job(fulls, shapes, [pltpu.SemaphoreType.DMA((n,)), pltpu.SemaphoreType.DMA((n,))], copies,
                alias={t: t for t in range(n)})


def _gather_job(bufs, forward):
    n = len(bufs)

    def copies(ins, outs, sems):
        x, y, c = _place()
        send, recv = sems
        cps = []
        for t in range(n):
            half = ins[t].shape[1] // 2
            rows = pl.ds(c * half, half)
            for r, (fx, fy) in enumerate(_CHIP_FLIPS):
                if forward:
                    kk = 2 * _flip(x, fx) + _flip(y, fy)
                    dev = (x, y, 1 - c)
                else:
                    kk = 2 * x + y
                    dev = (_flip(x, fx), _flip(y, fy), c)
                cps.append(_remote(ins[t].at[kk, rows], outs[t].at[kk, rows], send.at[t, r], recv.at[t, r], dev))
        return cps

    shapes = [jax.ShapeDtypeStruct(b.shape, b.dtype) for b in bufs]
    return _Job(bufs, shapes, [pltpu.SemaphoreType.DMA((n, 3)), pltpu.SemaphoreType.DMA((n, 3))], copies,
                alias={t: t for t in range(n)})


def _call(body, args, *, name, grid, in_specs, out_specs, out_shape, scratch_shapes=(), params=None, jobs=()):
    n_in, n_out, n_scr = len(in_specs), len(out_specs), len(scratch_shapes)
    job_in = [len(j.operands) for j in jobs]
    job_out = [len(j.out_shapes) for j in jobs]
    job_sem = [len(j.sems) for j in jobs]

    def wrapped(*refs):
        pos = 0
        ins = refs[pos:pos + n_in]; pos += n_in
        jins = []
        for k in job_in:
            jins.append(refs[pos:pos + k]); pos += k
        outs = refs[pos:pos + n_out]; pos += n_out
        jouts = []
        for k in job_out:
            jouts.append(refs[pos:pos + k]); pos += k
        scr = refs[pos:pos + n_scr]; pos += n_scr
        jsems = []
        for k in job_sem:
            jsems.append(refs[pos:pos + k]); pos += k
        if jobs:
            ids = [pl.program_id(d) for d in range(len(grid))]
            first = functools.reduce(jnp.logical_and, [i == 0 for i in ids])
            last = functools.reduce(jnp.logical_and, [i == g - 1 for i, g in zip(ids, grid)])

            @pl.when(first)
            def _():
                for j, ji, jo, js in zip(jobs, jins, jouts, jsems):
                    for cp in j.copies(ji, jo, js):
                        cp.start()

        body(*ins, *outs, *scr)
        if jobs:
            @pl.when(last)
            def _():
                for j, ji, jo, js in zip(jobs, jins, jouts, jsems):
                    for cp in j.copies(ji, jo, js):
                        cp.wait()

    aliases = {}
    in_pos, out_pos = n_in, n_out
    for j in jobs:
        for i, o in j.alias.items():
            aliases[in_pos + i] = out_pos + o
        in_pos += len(j.operands)
        out_pos += len(j.out_shapes)
    res = pl.pallas_call(
        wrapped, name=name, grid=grid,
        in_specs=list(in_specs) + [HBM] * sum(job_in),
        out_specs=list(out_specs) + [HBM] * sum(job_out),
        out_shape=list(out_shape) + [_in_hbm(s) for j in jobs for s in j.out_shapes],
        scratch_shapes=list(scratch_shapes) + [s for j in jobs for s in j.sems],
        input_output_aliases=aliases,
        compiler_params=params,
    )(*[a if spec is SMEM else _from_hbm(a) for a, spec in zip(args, in_specs)],
      *[a for j in jobs for a in j.operands])
    own, rest = list(res[:n_out]), list(res[n_out:])
    job_res = []
    for k in job_out:
        job_res.append(rest[:k])
        rest = rest[k:]
    return own, job_res


def _run_jobs(name, jobs):
    def body():
        pass

    return _call(body, (), name=name, grid=(1,), in_specs=[], out_specs=[], out_shape=[], jobs=jobs)[1]


def _dot(a, b):
    return jnp.dot(a, b, preferred_element_type=F32)


def _dot_nt(a, b):
    return lax.dot_general(a, b, (((1,), (1,)), ((), ())), preferred_element_type=F32)


def _dot_tn(a, b):
    return lax.dot_general(a, b, (((0,), (0,)), ((), ())), preferred_element_type=F32)


def _rms_r(x):
    return lax.rsqrt(jnp.mean(x * x, axis=-1, keepdims=True) + EPS)


def _rms_bwd(x, r, g, dy):
    n = x * r
    dn = dy * g
    dx = r * (dn - n * jnp.mean(dn * n, axis=-1, keepdims=True))
    return dx, dy * n


def _seg64_sum(v):
    rows, width = v.shape
    lane = lax.broadcasted_iota(jnp.int32, (rows, LANES), 1)
    lo = lane < HEAD_DIM
    outs = []
    for c in range(width // LANES):
        ch = v[:, c * LANES:(c + 1) * LANES]
        s_lo = jnp.sum(jnp.where(lo, ch, 0.0), axis=-1, keepdims=True)
        s_hi = jnp.sum(jnp.where(lo, 0.0, ch), axis=-1, keepdims=True)
        outs.append(jnp.where(lo, s_lo, s_hi))
    return outs[0] if len(outs) == 1 else jnp.concatenate(outs, axis=-1)


def _head_r(v):
    return lax.rsqrt(_seg64_sum(v * v) * (1.0 / HEAD_DIM) + EPS)


def _swap16(ch):
    lane = lax.broadcasted_iota(jnp.int32, ch.shape, 1)
    return jnp.where((lane % 32) < 16, pltpu.roll(ch, LANES - 16, 1), pltpu.roll(ch, 16, 1))


def _by_chunk(fn, v):
    outs = [fn(v[:, c * LANES:(c + 1) * LANES]) for c in range(v.shape[1] // LANES)]
    return outs[0] if len(outs) == 1 else jnp.concatenate(outs, axis=-1)


def _rope(v, cos, sin_signed):
    return _by_chunk(lambda ch: ch * cos + _swap16(ch) * sin_signed, v)


def _rope_t(g, cos, sin_signed):
    return _by_chunk(lambda ch: ch * cos + _swap16(ch * sin_signed), g)


def _rope_tables(seq):
    nf = HEAD_DIM // 4
    freqs = ROPE_THETA ** (-jnp.arange(nf, dtype=F32) / nf)
    pos = jnp.arange(seq, dtype=jnp.int32)
    row = (pos // GRID_W).astype(F32)
    col = (pos % GRID_W).astype(F32)
    ang_r = row[:, None] * freqs[None, :]
    ang_c = col[:, None] * freqs[None, :]
    cr, sr, cc, sc = jnp.cos(ang_r), jnp.sin(ang_r), jnp.cos(ang_c), jnp.sin(ang_c)
    cos = jnp.concatenate([cr, cr, cc, cc], axis=1)
    sin = jnp.concatenate([-sr, sr, -sc, sc], axis=1)
    return cos, sin


def _t5_bucket(rel):
    nb = N_BUCKETS // 2
    ret = (rel > 0).astype(jnp.int32) * nb
    n = jnp.abs(rel)
    max_exact = nb // 2
    nf = jnp.maximum(n, 1).astype(jnp.float32)
    large = max_exact + (jnp.log(nf / max_exact) / np.float32(np.log(MAX_DISTANCE / max_exact))
                         * (nb - max_exact)).astype(jnp.int32)
    large = jnp.minimum(large, nb - 1)
    return ret + jnp.where(n < max_exact, n, large)


def _window_tables():
    a = jnp.arange(BLOCK, dtype=jnp.int32)
    c = jnp.arange(3 * BLOCK, dtype=jnp.int32)
    rel = c[None, :] - BLOCK - a[:, None]
    bucket = _t5_bucket(rel)
    band = (jnp.abs(rel) <= BLOCK).astype(jnp.int32)
    to3 = lambda t: t.reshape(BLOCK, 3, BLOCK).transpose(1, 2, 0)
    return to3(bucket), to3(band)


def _pre_proj(x, g1, w_in, gq, gk, ck, sk, *, seq, tm, sub):
    tokens = x.shape[0]
    n_seq = seq // tm
    nblk = tm // BLOCK
    batch = tokens // seq

    def body(x_ref, g1_ref, w_ref, gq_ref, gk_ref, ck_ref, sk_ref,
             h1_ref, raw_ref, qa_ref, ka_ref, kta_ref, va_ref, vta_ref,
             qtb_ref, kb_ref, ktb_ref, vb_ref, vtb_ref, proj):
        for r in range(tm // sub):
            rows = slice(r * sub, (r + 1) * sub)
            xv = x_ref[rows, :]
            h = (xv * _rms_r(xv) * g1_ref[...]).astype(MM)
            h1_ref[rows, :] = h
            for j in range(N_CHIPS):
                proj[rows, j * IN_CHUNK:(j + 1) * IN_CHUNK] = _dot(h, w_ref[j])
            qa = proj[rows, 0:Q_WIDTH]
            ka = proj[rows, Q_WIDTH:QK_RAW]
            raw_ref[rows, :] = proj[rows, 0:QK_RAW]
            qn = qa * _head_r(qa) * gq_ref[...]
            qa_ref[rows, :] = (_rope(qn, ck_ref[rows, :], sk_ref[rows, :]) * SCALE).astype(MM)
            kn = ka * _head_r(ka) * gk_ref[...]
            kr = _rope(kn, ck_ref[rows, :], sk_ref[rows, :])
            ka_ref[rows, :] = kr.astype(MM)
            kta_ref[0, :, rows] = kr.T.astype(MM)
            va = proj[rows, 640:768]
            va_ref[rows, :] = va.astype(MM)
            vta_ref[0, :, rows] = va.T.astype(MM)
            qb = proj[rows, 768:1280] * SCALE
            kb = proj[rows, 1280:1408]
            vb = proj[rows, 1408:1536]
            kb_ref[rows, :] = kb.astype(MM)
            vb_ref[rows, :] = vb.astype(MM)
            for j in range(sub // BLOCK):
                blk = slice(j * BLOCK, (j + 1) * BLOCK)
                qtb_ref[r * (sub // BLOCK) + j] = qb[blk, :].T.astype(MM)
                ktb_ref[r * (sub // BLOCK) + j] = kb[blk, :].T.astype(MM)
                vtb_ref[r * (sub // BLOCK) + j] = vb[blk, :].T.astype(MM)

    tok = lambda w: pl.BlockSpec((tm, w), lambda i: (i, 0))
    tab = lambda w: pl.BlockSpec((tm, w), lambda i: (i % n_seq, 0))
    row = lambda w: pl.BlockSpec((1, w), lambda i: (0, 0))
    tposed = pl.BlockSpec((1, LANES, tm), lambda i: (i // n_seq, 0, i % n_seq))
    blocks = pl.BlockSpec((nblk, BLOCK, LANES), lambda i: (i, 0, 0))
    qblocks = pl.BlockSpec((nblk, Q_WIDTH, BLOCK), lambda i: (i, 0, 0))
    tok_mm = lambda w: jax.ShapeDtypeStruct((tokens, w), MM)
    return pl.pallas_call(
        body, name="pre_proj",
        grid=(tokens // tm,),
        in_specs=[tok(D_MODEL), row(D_MODEL),
                  pl.BlockSpec((N_CHIPS, D_MODEL, IN_CHUNK), lambda i: (0, 0, 0)),
                  row(Q_WIDTH), row(KV_WIDTH), tab(KV_WIDTH), tab(KV_WIDTH)],
        out_specs=[tok(D_MODEL), tok(QK_RAW), tok(Q_WIDTH), tok(KV_WIDTH), tposed, tok(KV_WIDTH), tposed,
                   qblocks, tok(KV_WIDTH), blocks, tok(KV_WIDTH), blocks],
        out_shape=[
            tok_mm(D_MODEL),
            jax.ShapeDtypeStruct((tokens, QK_RAW), F32),
            tok_mm(Q_WIDTH),
            tok_mm(KV_WIDTH),
            jax.ShapeDtypeStruct((batch, KV_WIDTH, seq), MM),
            tok_mm(KV_WIDTH),
            jax.ShapeDtypeStruct((batch, KV_WIDTH, seq), MM),
            jax.ShapeDtypeStruct((tokens // BLOCK, Q_WIDTH, BLOCK), MM),
            tok_mm(KV_WIDTH),
            jax.ShapeDtypeStruct((tokens // BLOCK, KV_WIDTH, BLOCK), MM),
            tok_mm(KV_WIDTH),
            jax.ShapeDtypeStruct((tokens // BLOCK, KV_WIDTH, BLOCK), MM),
        ],
        scratch_shapes=[pltpu.VMEM((tm, IN_TOTAL), F32)],
        compiler_params=_cparams(("parallel",), 48),
    )(*map(_from_hbm, (x, g1, w_in, gq, gk, ck, sk)))


def _kv_half(v2, kv):
    return jnp.where(kv == 0, v2[:, :HEAD_DIM], v2[:, HEAD_DIM:])


def _attn_a_fwd(qa, kta, va, *, seq, bq, jobs=()):
    tokens = qa.shape[0]
    batch = tokens // seq
    nq = seq // bq

    def body(q_ref, kt_ref, v_ref, o_ref, p_ref, linv_ref):
        kv = pl.program_id(1)
        kt = kt_ref[0]
        lane = lax.broadcasted_iota(jnp.int32, (seq, KV_WIDTH), 1)
        v = jnp.where((lane < HEAD_DIM) == (kv == 0), v_ref[...], jnp.ones((), MM))
        q_rows = jnp.concatenate([q_ref[:, g * HEAD_DIM:(g + 1) * HEAD_DIM] for g in range(GROUP)], axis=0)
        s_all = _dot(q_rows, kt)
        for g in range(GROUP):
            sl = slice(g * HEAD_DIM, (g + 1) * HEAD_DIM)
            s = s_all[g * bq:(g + 1) * bq]
            pb = jnp.exp((s - jnp.max(s, axis=-1, keepdims=True)).astype(MM))
            p_ref[0, g] = pb
            o2 = _dot(pb, v)
            linv = 1.0 / _kv_half(o2, 1 - kv)[:, 0:1]
            o_ref[:, sl] = _kv_half(o2, kv) * linv
            linv_ref[0, :, g:g + 1] = linv

    return _call(
        body, (qa, kta, va), name="attn_a_fwd", jobs=jobs,
        grid=(batch, N_KV, nq),
        in_specs=[pl.BlockSpec((bq, GROUP * HEAD_DIM), lambda b, k, i: (b * nq + i, k)),
                  pl.BlockSpec((1, HEAD_DIM, seq), lambda b, k, i: (b, k, 0)),
                  pl.BlockSpec((seq, KV_WIDTH), lambda b, k, i: (b, 0))],
        out_specs=[pl.BlockSpec((bq, GROUP * HEAD_DIM), lambda b, k, i: (b * nq + i, k)),
                   pl.BlockSpec((1, GROUP, bq, seq), lambda b, k, i: (k, 0, b * nq + i, 0)),
                   pl.BlockSpec((1, bq, GROUP), lambda b, k, i: (k, b * nq + i, 0))],
        out_shape=[jax.ShapeDtypeStruct((tokens, Q_WIDTH), F32),
                   jax.ShapeDtypeStruct((N_KV, GROUP, tokens, seq), MM),
                   jax.ShapeDtypeStruct((N_KV, tokens, GROUP), F32)],
        params=_cparams(("arbitrary", "arbitrary", "arbitrary"), 56))


def _attn_a_bwd(qa, ka, vta, do, o, p, linv, *, seq, bq, kt_tile, jobs=()):
    tokens = qa.shape[0]
    batch = tokens // seq
    nq = seq // bq

    def body(q_ref, k_ref, vt_ref, do_ref, o_ref, p_ref, linv_ref, dq_ref, dkt_ref, dvt_ref):
        kv = pl.program_id(1)

        @pl.when(pl.program_id(2) == 0)
        def _():
            dkt_ref[...] = jnp.zeros_like(dkt_ref)
            dvt_ref[...] = jnp.zeros_like(dvt_ref)

        vt = vt_ref[0]
        k2 = k_ref[...]
        for g in range(GROUP):
            sl = slice(g * HEAD_DIM, (g + 1) * HEAD_DIM)
            dof = do_ref[:, sl]
            delta = jnp.sum(dof * o_ref[:, sl], axis=-1, keepdims=True)
            linv_g = linv_ref[0, :, g:g + 1]
            don = (dof * linv_g).astype(MM)
            delta_n = delta * linv_g
            dq2 = jnp.zeros((bq, KV_WIDTH), F32)
            for t in range(seq // kt_tile):
                keys = slice(t * kt_tile, (t + 1) * kt_tile)
                pb = p_ref[0, g, :, keys]
                ds = pb * (_dot(don, vt[:, keys]) - delta_n).astype(MM)
                dq2 = dq2 + _dot(ds, k2[keys, :])
                dkt_ref[0, :, keys] += _dot_tn(q_ref[:, sl], ds)
                dvt_ref[0, :, keys] += _dot_tn(don, pb)
            dq_ref[:, sl] = _kv_half(dq2, kv)

    qspec = pl.BlockSpec((bq, GROUP * HEAD_DIM), lambda b, k, i: (b * nq + i, k))
    tspec = pl.BlockSpec((1, HEAD_DIM, seq), lambda b, k, i: (b, k, 0))
    return _call(
        body, (qa, ka, vta, do, o, p, linv), name="attn_a_bwd", jobs=jobs,
        grid=(batch, N_KV, nq),
        in_specs=[qspec, pl.BlockSpec((seq, KV_WIDTH), lambda b, k, i: (b, 0)), tspec, qspec, qspec,
                  pl.BlockSpec((1, GROUP, bq, seq), lambda b, k, i: (k, 0, b * nq + i, 0)),
                  pl.BlockSpec((1, bq, GROUP), lambda b, k, i: (k, b * nq + i, 0))],
        out_specs=[qspec, tspec, tspec],
        out_shape=[jax.ShapeDtypeStruct((tokens, Q_WIDTH), F32),
                   jax.ShapeDtypeStruct((batch, KV_WIDTH, seq), F32),
                   jax.ShapeDtypeStruct((batch, KV_WIDTH, seq), F32)],
        params=_cparams(("arbitrary", "arbitrary", "arbitrary"), 56))


def _bias_build(rel_bias_t, bucket_t, band_t):
    def body(tab_ref, bucket_ref, band_ref, bias_ref):
        for h in range(GROUP * N_KV):
            for piece in range(3):
                bk = bucket_ref[piece]
                acc = jnp.zeros((BLOCK, BLOCK), F32)
                for b in range(N_BUCKETS):
                    acc = jnp.where(bk == b, tab_ref[h, b], acc)
                g = h % GROUP
                bias_ref[h // GROUP, piece, :, g * BLOCK:(g + 1) * BLOCK] = jnp.where(band_ref[piece] != 0, acc, NEG_INF)

    out = jax.ShapeDtypeStruct((N_KV, 3, BLOCK, GROUP * BLOCK), F32)
    return pl.pallas_call(
        body, name="bias_build", grid=(1,),
        in_specs=[SMEM, _whole(bucket_t), _whole(band_t)], out_specs=_whole(out), out_shape=out,
    )(rel_bias_t, bucket_t, band_t)


def _pad_heads(t, kv):
    outs = []
    for g in range(GROUP):
        tg = t[g * HEAD_DIM:(g + 1) * HEAD_DIM, :]
        zero = jnp.zeros_like(tg)
        outs.append(jnp.concatenate([jnp.where(kv == 0, tg, zero), jnp.where(kv == 0, zero, tg)], axis=0))
    return jnp.concatenate(outs, axis=-1)


def _unpad_heads(t, kv):
    outs = [_kv_half(t[:, g * BLOCK:(g + 1) * BLOCK].T, kv) for g in range(GROUP)]
    return jnp.concatenate(outs, axis=-1)


def _sink_row(sink_ref, kv):
    lane_head = lax.broadcasted_iota(jnp.int32, (1, GROUP * BLOCK), 1) // BLOCK
    row = jnp.zeros((1, GROUP * BLOCK), F32)
    for g in range(GROUP):
        row = jnp.where(lane_head == g, sink_ref[0, kv * GROUP + g], row)
    return row


def _rows3(ref, idx):
    return jnp.concatenate([ref[i] for i in idx], axis=0)


def _lanes3(ref, idx):
    return jnp.concatenate([ref[i] for i in idx], axis=-1)


def _window_scores_t(k_ref, idx, qpad, bias_ref, n, nblk):
    s_all = _dot(_rows3(k_ref, idx), qpad)
    pieces = []
    for piece in range(3):
        s = s_all[piece * BLOCK:(piece + 1) * BLOCK] + bias_ref[0, piece]
        if piece == 0:
            s = jnp.where(n > 0, s, NEG_INF)
        if piece == 2:
            s = jnp.where(n < nblk - 1, s, NEG_INF)
        pieces.append(s)
    return pieces


def _attn_b_fwd(qtb, kb3, vtb, bias, sink, *, seq, per_step, jobs=()):
    nblk_all = qtb.shape[0]
    tokens = nblk_all * BLOCK
    batch = tokens // seq
    nblk = seq // BLOCK
    nstep = nblk // per_step

    def body(sink_ref, q_ref, k_ref, vt_ref, bias_ref, o_ref, p_ref, stat_ref):
        kv = pl.program_id(0)
        first = pl.program_id(2) * per_step
        sink_row = _sink_row(sink_ref, kv)
        stat_row = lax.broadcasted_iota(jnp.int32, (8, GROUP * BLOCK), 0)

        def block(i, carry):
            n = first + i
            idx = (jnp.maximum(n - 1, 0), n, jnp.minimum(n + 1, nblk - 1))
            rows = slice(i * BLOCK, (i + 1) * BLOCK)
            qpad = _pad_heads(q_ref[n], kv)
            ss = _window_scores_t(k_ref, idx, qpad, bias_ref, n, nblk)
            m = jnp.maximum(jnp.maximum(jnp.max(ss[0], axis=0, keepdims=True),
                                        jnp.max(ss[1], axis=0, keepdims=True)),
                            jnp.maximum(jnp.max(ss[2], axis=0, keepdims=True), sink_row))
            ps = [jnp.exp(s - m) for s in ss]
            e_sink = jnp.exp(sink_row - m)
            rinv = 1.0 / (jnp.sum(ps[0], axis=0, keepdims=True) + jnp.sum(ps[1], axis=0, keepdims=True)
                          + jnp.sum(ps[2], axis=0, keepdims=True) + e_sink)
            pbs = [p.astype(MM) for p in ps]
            for piece in range(3):
                p_ref[0, i, piece] = pbs[piece]
            ot = _dot(_lanes3(vt_ref, idx), jnp.concatenate(pbs, axis=0))
            o_ref[rows, :] = _unpad_heads(ot * rinv, kv)
            stat_ref[0, i] = jnp.where(stat_row == 0, rinv, e_sink * rinv)
            return carry

        for i in range(per_step):
            block(i, 0)

    both = pl.BlockSpec((nblk, BLOCK, KV_WIDTH), lambda k, b, j: (b, 0, 0))
    return _call(
        body, (sink, qtb, kb3, vtb, bias), name="attn_b_fwd", jobs=jobs,
        grid=(N_KV, batch, nstep),
        in_specs=[SMEM, pl.BlockSpec((nblk, GROUP * HEAD_DIM, BLOCK), lambda k, b, j: (b, k, 0)), both, both,
                  pl.BlockSpec((1, 3, BLOCK, GROUP * BLOCK), lambda k, b, j: (k, 0, 0, 0))],
        out_specs=[pl.BlockSpec((per_step * BLOCK, GROUP * HEAD_DIM), lambda k, b, j: (b * nstep + j, k)),
                   pl.BlockSpec((1, per_step, 3, BLOCK, GROUP * BLOCK), lambda k, b, j: (k, b * nstep + j, 0, 0, 0)),
                   pl.BlockSpec((1, per_step, 8, GROUP * BLOCK), lambda k, b, j: (k, b * nstep + j, 0, 0))],
        out_shape=[jax.ShapeDtypeStruct((tokens, Q_WIDTH), F32),
                   jax.ShapeDtypeStruct((N_KV, nblk_all, 3, BLOCK, GROUP * BLOCK), MM),
                   jax.ShapeDtypeStruct((N_KV, nblk_all, 8, GROUP * BLOCK), F32)],
        params=_cparams(("arbitrary", "arbitrary", "arbitrary"), 48))


def _attn_b_bwd(qtb, ktb, vb3, do, o, p, stat, *, seq, per_step, jobs=()):
    nblk_all = qtb.shape[0]
    tokens = nblk_all * BLOCK
    batch = tokens // seq
    nblk = seq // BLOCK
    nstep = nblk // per_step

    def body(q_ref, kt_ref, v_ref, do_ref, o_ref, p_ref, stat_ref,
             dq_ref, dk_ref, dv_ref, dbias_ref, dsink_ref):
        kv = pl.program_id(0)
        step = pl.program_id(2)
        first = step * per_step

        @pl.when(jnp.logical_and(pl.program_id(1) == 0, step == 0))
        def _():
            dbias_ref[...] = jnp.zeros_like(dbias_ref)
            dsink_ref[...] = jnp.zeros_like(dsink_ref)

        @pl.when(step == 0)
        def _():
            dk_ref[...] = jnp.zeros_like(dk_ref)
            dv_ref[...] = jnp.zeros_like(dv_ref)

        def block(i, dsink):
            n = first + i
            idx = (jnp.maximum(n - 1, 0), n, jnp.minimum(n + 1, nblk - 1))
            rows = slice(i * BLOCK, (i + 1) * BLOCK)
            qpad = _pad_heads(q_ref[n], kv)
            dot_t = do_ref[rows, :].T
            prod = dot_t * o_ref[rows, :].T
            delta = jnp.concatenate(
                [jnp.sum(prod[g * HEAD_DIM:(g + 1) * HEAD_DIM, :], axis=0, keepdims=True) for g in range(GROUP)],
                axis=-1)
            stats = stat_ref[0, i]
            rinv, p_sink = stats[0:1, :], stats[1:2, :]
            dopad_n = (_pad_heads(dot_t, kv) * rinv).astype(MM)
            delta_n = delta * rinv
            dpt = _dot(_rows3(v_ref, idx), dopad_n)
            pbs, dsbs = [], []
            for piece in range(3):
                pb = p_ref[0, i, piece]
                dst = pb.astype(F32) * (dpt[piece * BLOCK:(piece + 1) * BLOCK] - delta_n)
                dbias_ref[0, piece] += dst
                pbs.append(pb)
                dsbs.append(dst.astype(MM))
            ds_cat = jnp.concatenate(dsbs, axis=0)
            dq_ref[rows, :] = _unpad_heads(_dot(_lanes3(kt_ref, idx), ds_cat), kv)
            dk_all = _dot_nt(ds_cat, qpad)
            dv_all = _dot_nt(jnp.concatenate(pbs, axis=0), dopad_n)
            for piece in range(3):
                dk_ref[0, idx[piece]] += dk_all[piece * BLOCK:(piece + 1) * BLOCK]
                dv_ref[0, idx[piece]] += dv_all[piece * BLOCK:(piece + 1) * BLOCK]
            return dsink - p_sink * delta

        dsink = jnp.zeros((1, GROUP * BLOCK), F32)
        for i in range(per_step):
            dsink = block(i, dsink)
        dsink_ref[0] += jnp.broadcast_to(dsink, (8, GROUP * BLOCK))

    qspec = pl.BlockSpec((per_step * BLOCK, GROUP * HEAD_DIM), lambda k, b, j: (b * nstep + j, k))
    both = pl.BlockSpec((nblk, BLOCK, KV_WIDTH), lambda k, b, j: (b, 0, 0))
    grad =pl.BlockSpec((1, nblk, BLOCK, KV_WIDTH), lambda k, b, j: (k, b, 0, 0))
    return _call(
        body, (qtb, ktb, vb3, do, o, p, stat), name="attn_b_bwd", jobs=jobs,
        grid=(N_KV, batch, nstep),
        in_specs=[pl.BlockSpec((nblk, GROUP * HEAD_DIM, BLOCK), lambda k, b, j: (b, k, 0)), both, both,
                  qspec, qspec,
                  pl.BlockSpec((1, per_step, 3, BLOCK, GROUP * BLOCK), lambda k, b, j: (k, b * nstep + j, 0, 0, 0)),
                  pl.BlockSpec((1, per_step, 8, GROUP * BLOCK), lambda k, b, j: (k, b * nstep + j, 0, 0))],
        out_specs=[qspec, grad, grad,
                   pl.BlockSpec((1, 3, BLOCK, GROUP * BLOCK), lambda k, b, j: (k, 0, 0, 0)),
                   pl.BlockSpec((1, 8, GROUP * BLOCK), lambda k, b, j: (k, 0, 0))],
        out_shape=[jax.ShapeDtypeStruct((tokens, Q_WIDTH), F32),
                   jax.ShapeDtypeStruct((N_KV, nblk_all, BLOCK, KV_WIDTH), F32),
                   jax.ShapeDtypeStruct((N_KV, nblk_all, BLOCK, KV_WIDTH), F32),
                   jax.ShapeDtypeStruct((N_KV, 3, BLOCK, GROUP * BLOCK), F32),
                   jax.ShapeDtypeStruct((N_KV, 8, GROUP * BLOCK), F32)],
        params=_cparams(("arbitrary", "arbitrary", "arbitrary"), 56))


def _resident(shape):
    return pl.BlockSpec(shape, lambda i: (0,) * len(shape), pipeline_mode=pl.Buffered(1))


def _mix_ffn_fwd(oa, ob, w_o, x, g2, g3, w_up, w_down, target, g4, *, tm):
    tokens = x.shape[0]
    nt = tokens // tm

    def body(oa_ref, ob_ref, wo_ref, x_ref, g2_ref, g3_ref, wu_ref, wd_ref, t_ref, g4_ref,
             mix_ref, x1_ref, h2_ref, o_ref, u_ref, df_ref, dy_ref, loss_ref, dg4_ref):
        o = jnp.concatenate([oa_ref[...].astype(MM), ob_ref[...].astype(MM)], axis=-1)
        o_ref[...] = o
        mix = _dot(o, wo_ref[...])
        mix_ref[...] = mix
        x1 = x_ref[...] + mix * _rms_r(mix) * g2_ref[...]
        x1_ref[...] = x1
        h2v = (x1 * _rms_r(x1) * g3_ref[...]).astype(MM)
        h2_ref[...] = h2v
        f = jnp.zeros((tm, D_MODEL), F32)
        for c in range(N_CHIPS):
            u = jnp.maximum(_dot(h2v, wu_ref[c]), 0.0)
            u_ref[:, c * FF_CHUNK:(c + 1) * FF_CHUNK] = u.astype(MM)
            f = f + _dot((u * u).astype(MM), wd_ref[c * FF_CHUNK:(c + 1) * FF_CHUNK, :])
        r = _rms_r(f)
        g4v = g4_ref[...]
        err = x1 + f * r * g4v - t_ref[...]
        sq = jnp.sum(err * err, axis=-1, keepdims=True)
        loss_ref[0] = jnp.broadcast_to(jnp.sum(sq, axis=0, keepdims=True) * (0.5 / D_MODEL), (8, LANES))
        dy = err * (1.0 / D_MODEL)
        dy_ref[...] = dy
        dfv, dgv = _rms_bwd(f, r, g4v, dy)
        df_ref[...] = dfv.astype(MM)
        dg4_ref[0] = jnp.sum(dgv, axis=0, keepdims=True)

    tok = pl.BlockSpec((tm, D_MODEL), lambda i: (i, 0))
    half = pl.BlockSpec((tm, Q_WIDTH), lambda i: (i, 0))
    row = pl.BlockSpec((1, D_MODEL), lambda i: (0, 0))
    tok_f32 = jax.ShapeDtypeStruct((tokens, D_MODEL), F32)
    tok_mm = jax.ShapeDtypeStruct((tokens, D_MODEL), MM)
    return pl.pallas_call(
        body, name="mix_ffn_fwd",
        grid=(nt,),
        in_specs=[half, half, _resident((D_MODEL, D_MODEL)), tok, row, row,
                  _resident((N_CHIPS, D_MODEL, FF_CHUNK)), _resident((D_FF, D_MODEL)), tok, row],
        out_specs=[tok, tok, tok, tok, pl.BlockSpec((tm, D_FF), lambda i: (i, 0)), tok, tok,
                   pl.BlockSpec((1, 8, LANES), lambda i: (i, 0, 0)),
                   pl.BlockSpec((1, 1, D_MODEL), lambda i: (i, 0, 0))],
        out_shape=[tok_f32,
                   tok_f32,
                   tok_mm,
                   tok_mm,
                   jax.ShapeDtypeStruct((tokens, D_FF), MM),
                   tok_mm,
                   tok_f32,
                   jax.ShapeDtypeStruct((nt, 8, LANES), F32),
                   jax.ShapeDtypeStruct((nt, 1, D_MODEL), F32)],
        compiler_params=_cparams(("parallel",), 56),
    )(*map(_from_hbm, (oa, ob, w_o, x, g2, g3, w_up, w_down, target, g4)))


def _ffn_bwd_act(df, w_down, u, w_up, x1, dy, mix, g3, g2, w_o, *, tm):
    tokens = df.shape[0]
    nt = tokens // tm

    def body(df_ref, wd_ref, u_ref, wu_ref, x1_ref, dy_ref, mix_ref, g3_ref, g2_ref, wo_ref,
             dz_ref, dx1_ref, dmix_ref, dg3_ref, dg2_ref, doa_ref, dob_ref):
        dfv = df_ref[...]
        dh2 = jnp.zeros((tm, D_MODEL), F32)
        for c in range(N_CHIPS):
            cols = slice(c * FF_CHUNK, (c + 1) * FF_CHUNK)
            da = _dot_nt(dfv, wd_ref[cols, :])
            dz = (da * (2.0 * u_ref[:, cols].astype(F32))).astype(MM)
            dz_ref[:, cols] = dz
            dh2 = dh2 + _dot_nt(dz, wu_ref[c])
        x1 = x1_ref[...]
        dxn, dg3v = _rms_bwd(x1, _rms_r(x1), g3_ref[...], dh2)
        dx1 = dy_ref[...] + dxn
        dx1_ref[...] = dx1
        dg3_ref[0] = jnp.sum(dg3v, axis=0, keepdims=True)
        mix = mix_ref[...]
        dmix, dg2v = _rms_bwd(mix, _rms_r(mix), g2_ref[...], dx1)
        dmb = dmix.astype(MM)
        dmix_ref[...] = dmb
        dg2_ref[0] = jnp.sum(dg2v, axis=0, keepdims=True)
        doa_ref[...] = _dot_nt(dmb, wo_ref[0:Q_WIDTH, :])
        dob_ref[...] = _dot_nt(dmb, wo_ref[Q_WIDTH:D_MODEL, :])

    tok = pl.BlockSpec((tm, D_MODEL), lambda i: (i, 0))
    half = pl.BlockSpec((tm, Q_WIDTH), lambda i: (i, 0))
    wide = pl.BlockSpec((tm, D_FF), lambda i: (i, 0))
    row = pl.BlockSpec((1, D_MODEL), lambda i: (0, 0))
    part = pl.BlockSpec((1, 1, D_MODEL), lambda i: (i, 0, 0))
    return pl.pallas_call(
        body, name="ffn_bwd_act",
        grid=(nt,),
        in_specs=[tok, _resident((D_FF, D_MODEL)), wide, _resident((N_CHIPS, D_MODEL, FF_CHUNK)),
                  tok, tok, tok, row, row, _resident((D_MODEL, D_MODEL))],
        out_specs=[wide, tok, tok, part, part, half, half],
        out_shape=[jax.ShapeDtypeStruct((tokens, D_FF), MM),
                   jax.ShapeDtypeStruct((tokens, D_MODEL), F32),
                   jax.ShapeDtypeStruct((tokens, D_MODEL), MM),
                   jax.ShapeDtypeStruct((nt, 1, D_MODEL), F32),
                   jax.ShapeDtypeStruct((nt, 1, D_MODEL), F32),
                   jax.ShapeDtypeStruct((tokens, Q_WIDTH), F32),
                   jax.ShapeDtypeStruct((tokens, Q_WIDTH), F32)],
        compiler_params=_cparams(("parallel",), 56),
    )(*map(_from_hbm, (df, w_down, u, w_up, x1, dy, mix, g3, g2, w_o)))


def _tn_matmul(a, b, *, name, tm, tn, tk, chunk=None, square_a=False, vmem_mb=48, jobs=()):
    tokens, m_dim = a.shape
    n_dim = b.shape[1]
    chunked = chunk is not None
    if chunked:
        assert tm == m_dim and tn % chunk == 0

    def body(a_ref, b_ref, o_ref):
        av = a_ref[...]
        if square_a:
            av = av.astype(F32)
            av = av * av
        part = _dot_tn(av.astype(MM), b_ref[...].astype(MM))
        if chunked:
            part = jnp.stack([part[:, c * chunk:(c + 1) * chunk] for c in range(tn // chunk)])

        @pl.when(pl.program_id(2) == 0)
        def _():
            o_ref[...] = part

        @pl.when(pl.program_id(2) > 0)
        def _():
            o_ref[...] += part

    if chunked:
        out_spec = pl.BlockSpec((tn // chunk, tm, chunk), lambda i, j, k: (j, 0, 0))
        out_shape = jax.ShapeDtypeStruct((n_dim // chunk, m_dim, chunk), F32)
    else:
        out_spec = pl.BlockSpec((tm, tn), lambda i, j, k: (i, j))
        out_shape = jax.ShapeDtypeStruct((m_dim, n_dim), F32)
    (out,), job_res = _call(
        body, (a, b), name=name, jobs=jobs,
        grid=(m_dim // tm, n_dim // tn, tokens // tk),
        in_specs=[pl.BlockSpec((tk, tm), lambda i, j, k: (k, i)),
                  pl.BlockSpec((tk, tn), lambda i, j, k: (k, j))],
        out_specs=[out_spec], out_shape=[_in_hbm(out_shape)],
        params=_cparams(("arbitrary", "arbitrary", "arbitrary"), vmem_mb))
    return out, job_res


def _proj_bwd(dqa, dkta, dvta, dqb, dktb, dvtb, raw, x, dx1, g1, w_in, gq, gk, ck, sk, *, seq, tm, sub, jobs=()):
    tokens = x.shape[0]
    nt = tokens // tm
    n_seq = seq // tm
    nblk = tm // BLOCK

    def body(dqa_ref, dkta_ref, dvta_ref, dqb_ref, dkb_ref, dvb_ref, raw_ref, x_ref, dx1_ref, g1_ref, w_ref,
             gq_ref, gk_ref, ck_ref, sk_ref,
             gx_ref, dproj_ref, dg1_ref, dgq_ref, dgk_ref, dp):
        parts = []
        for r in range(tm // sub):
            rows = slice(r * sub, (r + 1) * sub)
            qa = raw_ref[rows, 0:Q_WIDTH]
            dqn = _rope_t(dqa_ref[rows, :], ck_ref[rows, :], sk_ref[rows, :]) * SCALE
            rq = _head_r(qa)
            nq = qa * rq
            dnq = dqn * gq_ref[...]
            dp[rows, 0:Q_WIDTH] = rq * (dnq - nq * (_seg64_sum(dnq * nq) * (1.0 / HEAD_DIM)))

            ka = raw_ref[rows, Q_WIDTH:QK_RAW]
            dkn = _rope_t(dkta_ref[0, :, rows].T, ck_ref[rows, :], sk_ref[rows, :])
            rk = _head_r(ka)
            nk = ka * rk
            dnk = dkn * gk_ref[...]
            dp[rows, 512:640] = rk * (dnk - nk * (_seg64_sum(dnk * nk) * (1.0 / HEAD_DIM)))

            dp[rows, 640:768] = dvta_ref[0, :, rows].T
            dp[rows, 768:1280] = dqb_ref[rows, :] * SCALE
            for j in range(r * sub // BLOCK, (r + 1) * sub // BLOCK):
                dp[j * BLOCK:(j + 1) * BLOCK, 1280:1408] = dkb_ref[0, j] + dkb_ref[1, j]
                dp[j * BLOCK:(j + 1) * BLOCK, 1408:1536] = dvb_ref[0, j] + dvb_ref[1, j]

            dproj = dp[rows, :].astype(MM)
            dproj_ref[rows, :] = dproj
            dh1 = _dot_nt(dproj[:, 0:IN_CHUNK], w_ref[0])
            for j in range(1, N_CHIPS):
                dh1 = dh1 + _dot_nt(dproj[:, j * IN_CHUNK:(j + 1) * IN_CHUNK], w_ref[j])
            xv = x_ref[rows, :]
            dxn, dg1v = _rms_bwd(xv, _rms_r(xv), g1_ref[...], dh1)
            gx_ref[rows, :] = dx1_ref[rows, :] + dxn
            parts.append((jnp.sum(dqn * nq, axis=0, keepdims=True), jnp.sum(dkn * nk, axis=0, keepdims=True),
                          jnp.sum(dg1v, axis=0, keepdims=True)))
        dgq_ref[0] = functools.reduce(jnp.add, [p[0] for p in parts])
        dgk_ref[0] = functools.reduce(jnp.add, [p[1] for p in parts])
        dg1_ref[0] = functools.reduce(jnp.add, [p[2] for p in parts])

    tok = lambda w: pl.BlockSpec((tm, w), lambda i: (i, 0))
    tab = lambda w: pl.BlockSpec((tm, w), lambda i: (i % n_seq, 0))
    row = lambda w: pl.BlockSpec((1, w), lambda i: (0, 0))
    tposed = pl.BlockSpec((1, KV_WIDTH, tm), lambda i: (i // n_seq, 0, i % n_seq))
    blocks = pl.BlockSpec((N_KV, nblk, BLOCK, KV_WIDTH), lambda i: (0, i, 0, 0))
    part = lambda w: pl.BlockSpec((1, 1, w), lambda i: (i, 0, 0))
    return _call(
        body, (dqa, dkta, dvta, dqb, dktb, dvtb, raw, x, dx1, g1, w_in, gq, gk, ck, sk),
        name="proj_bwd", jobs=jobs,
        grid=(nt,),
        in_specs=[tok(Q_WIDTH), tposed, tposed, tok(Q_WIDTH), blocks, blocks, tok(QK_RAW), tok(D_MODEL),
                  tok(D_MODEL), row(D_MODEL),
                  pl.BlockSpec((N_CHIPS, D_MODEL, IN_CHUNK), lambda i: (0, 0, 0)),
                  row(Q_WIDTH), row(KV_WIDTH), tab(KV_WIDTH), tab(KV_WIDTH)],
        out_specs=[tok(D_MODEL), tok(IN_TOTAL), part(D_MODEL), part(Q_WIDTH), part(KV_WIDTH)],
        out_shape=[jax.ShapeDtypeStruct((tokens, D_MODEL), F32),
                   jax.ShapeDtypeStruct((tokens, IN_TOTAL), MM),
                   jax.ShapeDtypeStruct((nt, 1, D_MODEL), F32),
                   jax.ShapeDtypeStruct((nt, 1, Q_WIDTH), F32),
                   jax.ShapeDtypeStruct((nt, 1, KV_WIDTH), F32)],
        scratch_shapes=[pltpu.VMEM((tm, IN_TOTAL), F32)],
        params=_cparams(("arbitrary",), 56))


def _pack_small(dg1, dg2, dg3, dg4, dgq, dgk, dsink, dbias, bucket, loss):
    def body(dg1_ref, dg2_ref, dg3_ref, dg4_ref, dgq_ref, dgk_ref, dsink_ref, dbias_ref, bucket_ref, loss_ref,
             out_ref, rel_ref):
        out_ref[...] = jnp.zeros_like(out_ref)
        for r, ref in ((ROW_G1, dg1_ref), (ROW_G2, dg2_ref), (ROW_G3, dg3_ref), (ROW_G4, dg4_ref)):
            acc = ref[0]
            for t in range(1, ref.shape[0]):
                acc = acc + ref[t]
            out_ref[r:r + 1, :] = acc

        def fold(ref, heads):
            acc = ref[0]
            for t in range(1, ref.shape[0]):
                acc = acc + ref[t]
            tot = acc[:, 0:HEAD_DIM]
            for h in range(1, heads):
                tot = tot + acc[:, h * HEAD_DIM:(h + 1) * HEAD_DIM]
            return tot

        out_ref[ROW_MISC:ROW_MISC + 1, MISC_GQ:MISC_GQ + HEAD_DIM] = fold(dgq_ref, GROUP * N_KV)
        out_ref[ROW_MISC:ROW_MISC + 1, MISC_GK:MISC_GK + HEAD_DIM] = fold(dgk_ref, N_KV)
        for h in range(GROUP * N_KV):
            g = h % GROUP
            out_ref[ROW_MISC:ROW_MISC + 1, MISC_SINK + h:MISC_SINK + h + 1] = jnp.sum(
                dsink_ref[h // GROUP, 0:1, g * BLOCK:(g + 1) * BLOCK], axis=-1, keepdims=True)
        lacc = loss_ref[0, 0:1, 0:1]
        for t in range(1, loss_ref.shape[0]):
            lacc = lacc + loss_ref[t, 0:1, 0:1]
        out_ref[ROW_MISC:ROW_MISC + 1, MISC_LOSS:MISC_LOSS + 1] = lacc
        lane = lax.broadcasted_iota(jnp.int32, (N_BUCKETS, LANES), 1)
        row = lax.broadcasted_iota(jnp.int32, (N_BUCKETS, LANES), 0)

        def per_bucket(b, acc):
            for h in range(GROUP * N_KV):
                g = h % GROUP
                sel = jnp.zeros((BLOCK, BLOCK), F32)
                for piece in range(3):
                    sel = sel + jnp.where(bucket_ref[piece] == b,
                                          dbias_ref[h // GROUP, piece, :, g * BLOCK:(g + 1) * BLOCK], 0.0)
                tot = jnp.sum(jnp.sum(sel, axis=0, keepdims=True), axis=-1, keepdims=True)
                acc = jnp.where((row == b) & (lane == h), tot, acc)
            return acc

        rel_ref[...] = lax.fori_loop(0, N_BUCKETS, per_bucket, jnp.zeros((N_BUCKETS, LANES), F32))

    args = (dg1, dg2, dg3, dg4, dgq, dgk, dsink, dbias, bucket, loss)
    outs = [jax.ShapeDtypeStruct((8, D_MODEL), F32), jax.ShapeDtypeStruct((N_BUCKETS, LANES), F32)]
    return pl.pallas_call(
        body, name="pack_small", grid=(1,),
        in_specs=[_whole(a) for a in args], out_specs=[_whole(o) for o in outs], out_shape=outs,
        compiler_params=pltpu.CompilerParams(vmem_limit_bytes=32 * 1024 * 1024),
    )(*map(_from_hbm, args))


def _gather_weights(shards, whole):
    n = len(shards)
    full = [t for t in range(n) if whole[t]]

    def body(*refs):
        ins, outs = refs[:n], refs[n:2 * n]
        raw, stage = refs[2 * n:3 * n], refs[3 * n:4 * n]
        load_sem, local_sem, ici_send, ici_recv, d2d_send, d2d_recv = refs[4 * n:]
        x, y, c = _place()
        k = 2 * x + y
        sibling = (x, y, 1 - c)
        order = full + [t for t in range(n) if t not in full]
        loads = {t: pltpu.make_async_copy(ins[t], raw[t], load_sem.at[t]) for t in order}
        for t in order:
            loads[t].start()
        copies, sends = [], []
        for t in order:
            loads[t].wait()
            stage[t][...] = raw[t][...].astype(MM)
            mine = pltpu.make_async_copy(stage[t], outs[t].at[k], local_sem.at[t])
            mine.start()
            copies.append(mine)
            if t in full:
                half = ins[t].shape[0] // 2
                rows = pl.ds(c * half, half)
                for r, (fx, fy) in enumerate(_CHIP_FLIPS):
                    cp = _remote(stage[t].at[rows], outs[t].at[k, rows], ici_send.at[t, r], ici_recv.at[t, r],
                                 (_flip(x, fx), _flip(y, fy), c))
                    cp.start()
                    sends.append(cp)
        for t in full:
            half = ins[t].shape[0] // 2
            rows = pl.ds(c * half, half)
            for r, (fx, fy) in enumerate(_CHIP_FLIPS):
                kk = 2 * _flip(x, fx) + _flip(y, fy)
                landed = outs[t].at[kk, rows]
                _remote(landed, landed, ici_send.at[t, r], ici_recv.at[t, r], sibling).wait_recv()
                fwd = _remote(landed, landed, d2d_send.at[t, r], d2d_recv.at[t, r], sibling)
                fwd.start()
                sends.append(fwd)
        for t in full:
            half = ins[t].shape[0] // 2
            other = pl.ds((1 - c) * half, half)
            for r, (fx, fy) in enumerate(_CHIP_FLIPS):
                kk = 2 * _flip(x, fx) + _flip(y, fy)
                theirs = outs[t].at[kk, other]
                _remote(theirs, theirs, d2d_send.at[t, r], d2d_recv.at[t, r], sibling).wait_recv()
        for cp in sends:
            cp.wait_send()
        for cp in copies:
            cp.wait()

    return pl.pallas_call(
        body, name="gather_weights",
        in_specs=[HBM] * n, out_specs=[HBM] * n,
        out_shape=[pltpu.HBM((N_CHIPS,) + s.shape, MM) for s in shards],
        scratch_shapes=[pltpu.VMEM(s.shape, F32) for s in shards] + [pltpu.VMEM(s.shape, MM) for s in shards] + [
            pltpu.SemaphoreType.DMA((n,)), pltpu.SemaphoreType.DMA((n,)),
            pltpu.SemaphoreType.DMA((n, 3)), pltpu.SemaphoreType.DMA((n, 3)),
            pltpu.SemaphoreType.DMA((n, 3)), pltpu.SemaphoreType.DMA((n, 3))],
        compiler_params=pltpu.CompilerParams(vmem_limit_bytes=40 * 1024 * 1024),
    )(*shards)


def _add_half(grad, got, where, *, name, tr):
    nch, half, cols = got.shape
    tr = min(tr, half)
    nblk = half // tr

    def body(where_ref, g_ref, r_ref, o_ref):
        o_ref[...] = (g_ref[...] + r_ref[...]).astype(MM)

    return pl.pallas_call(
        body, name=name,
        grid_spec=pltpu.PrefetchScalarGridSpec(
            num_scalar_prefetch=1, grid=(nch, nblk),
            in_specs=[pl.BlockSpec((1, tr, cols), lambda j, i, where_ref: (j, where_ref[1] * nblk + i, 0)),
                      pl.BlockSpec((1, tr, cols), lambda j, i, where_ref: (j, i, 0))],
            out_specs=pl.BlockSpec((1, tr, cols), lambda j, i, where_ref: (j, i, 0))),
        out_shape=jax.ShapeDtypeStruct(got.shape, MM),
        compiler_params=_cparams(("parallel", "parallel"), 32),
    )(where, grad, got)


def _add_chips(own, got, where, *, name, tr):
    _, half, cols = own.shape
    tr = min(tr, half)
    nblk = half // tr

    def body(where_ref, o_ref, g_ref, out_ref):
        f = lambda v: v.astype(F32)
        out_ref[...] = ((f(o_ref[0]) + f(g_ref[0])) + f(g_ref[1])) + f(g_ref[2])

    return pl.pallas_call(
        body, name=name,
        grid_spec=pltpu.PrefetchScalarGridSpec(
            num_scalar_prefetch=1, grid=(nblk,),
            in_specs=[pl.BlockSpec((1, tr, cols), lambda i, where_ref: (where_ref[0], i, 0)),
                      pl.BlockSpec((3, tr, cols), lambda i, where_ref: (0, i, 0))],
            out_specs=pl.BlockSpec((tr, cols), lambda i, where_ref: (where_ref[1] * nblk + i, 0))),
        out_shape=pltpu.HBM((2 * half, cols), F32),
        compiler_params=_cparams(("parallel",), 32),
    )(where, own, got)


def _small_job(tiles):
    n = len(tiles)

    def copies(ins, outs, sems):
        x, y, c = _place()
        me = 4 * x + 2 * y + c
        local, send, recv = sems
        cps = []
        for t in range(n):
            cps.append(pltpu.make_async_copy(ins[t], outs[t].at[me], local.at[t]))
            for r in range(1, N_DEV):
                fx, fy, fc = (r >> 2) & 1, (r >> 1) & 1, r & 1
                cps.append(_remote(ins[t], outs[t].at[me], send.at[t, r - 1], recv.at[t, r - 1],
                                   (_flip(x, fx), _flip(y, fy), _flip(c, fc))))
        return cps

    return _Job(tiles, [jax.ShapeDtypeStruct((N_DEV,) + t.shape, F32) for t in tiles],
                [pltpu.SemaphoreType.DMA((n,)), pltpu.SemaphoreType.DMA((n, N_DEV - 1)),
                 pltpu.SemaphoreType.DMA((n, N_DEV - 1))], copies)


def _adamw_math(w, g, m, v):
    m = ADAM_B1 * m + (1.0 - ADAM_B1) * g
    v = ADAM_B2 * v + (1.0 - ADAM_B2) * (g * g)
    m_hat = m / (1.0 - ADAM_B1 ** ADAM_STEP)
    v_hat = v / (1.0 - ADAM_B2 ** ADAM_STEP)
    delta = -ADAM_LR * (m_hat / (jnp.sqrt(v_hat) + ADAM_EPS) + ADAM_WD * w)
    return delta, m, v


def _adamw(w, g, m, v, *, name, tr):
    rows, cols = w.shape
    tr = min(tr, rows)

    def body(w_ref, g_ref, m_ref, v_ref, go_ref, d_ref, nm_ref, nv_ref):
        g = g_ref[...]
        go_ref[...] = g
        d_ref[...], nm_ref[...], nv_ref[...] = _adamw_math(w_ref[...], g, m_ref[...], v_ref[...])

    spec = pl.BlockSpec((tr, cols), lambda i: (i, 0))
    return pl.pallas_call(
        body, name=name,
        grid=(rows // tr,),
        in_specs=[spec] * 4, out_specs=[spec] * 4,
        out_shape=[jax.ShapeDtypeStruct(w.shape, F32)] * 4,
        compiler_params=_cparams(("parallel",), 48),
    )(w, g, m, v)


def _small_adamw(gathered, gathered_rel, params, moments_m, moments_v):
    n = len(params)

    def body(all_ref, rel_all_ref, *refs):
        w_refs, m_refs, v_refs = refs[:n], refs[n:2 * n], refs[2 * n:3 * n]
        loss_ref = refs[3 * n]
        out_refs = refs[3 * n + 1:]
        g = all_ref[0]
        rel = rel_all_ref[0]
        for d in range(1, N_DEV):
            g = g + all_ref[d]
            rel = rel + rel_all_ref[d]
        misc = g[ROW_MISC:ROW_MISC + 1]
        loss_ref[...] = misc[:, MISC_LOSS:MISC_LOSS + 1]
        grads = (g[ROW_G1:ROW_G1 + 1], g[ROW_G2:ROW_G2 + 1], g[ROW_G3:ROW_G3 + 1], g[ROW_G4:ROW_G4 + 1],
                 misc[:, MISC_GQ:MISC_GQ + HEAD_DIM], misc[:, MISC_GK:MISC_GK + HEAD_DIM],
                 misc[:, MISC_SINK:MISC_SINK + GROUP * N_KV], rel[:, 0:GROUP * N_KV])
        for i in range(n):
            d, nm, nv = _adamw_math(w_refs[i][...], grads[i], m_refs[i][...], v_refs[i][...])
            for j, val in enumerate((grads[i], d, nm, nv)):
                out_refs[4 * i + j][...] = val

    args = (gathered, gathered_rel, *params, *moments_m, *moments_v)
    out_shape = [jax.ShapeDtypeStruct((1, 1), F32)] + [jax.ShapeDtypeStruct(p.shape, F32) for p in params
                                                       for _ in range(4)]
    outs = pl.pallas_call(
        body, name="small_adamw", grid=(1,),
        in_specs=[_whole(a) for a in args], out_specs=[_whole(o) for o in out_shape], out_shape=out_shape,
    )(*map(_from_hbm, args))
    return outs[0], [outs[1 + 4 * i:5 + 4 * i] for i in range(n)]


def kernel(x, w_in, w_o, g_pre_mix, g_post_mix, q_norm_a, k_norm_a, sink_b, rel_bias, g_pre_ffn, w_ffn_up, w_ffn_down, g_post_ffn, loss_target, m_w_in, m_w_o, m_g_pre_mix, m_g_post_mix, m_q_norm_a, m_k_norm_a, m_sink_b, m_rel_bias, m_g_pre_ffn, m_w_ffn_up, m_w_ffn_down, m_g_post_ffn, v_w_in, v_w_o, v_g_pre_mix, v_g_post_mix, v_q_norm_a, v_k_norm_a, v_sink_b, v_rel_bias, v_g_pre_ffn, v_w_ffn_up, v_w_ffn_down, v_g_post_ffn):
    batch, seq, _ = x.shape
    tokens = batch * seq
    where = jnp.stack([2 * lax.axis_index("x") + lax.axis_index("y"), lax.axis_index("c")]).astype(jnp.int32)
    x2 = x.reshape(tokens, D_MODEL)
    g1, g2, g3, g4 = g_pre_mix, g_post_mix, g_pre_ffn, g_post_ffn

    cos, sin = _rope_tables(seq)
    ck, sk = jnp.tile(cos, (1, 2)), jnp.tile(sin, (1, 2))
    gq8, gk2 = jnp.tile(q_norm_a, (1, 8)), jnp.tile(k_norm_a, (1, 2))
    bucket, band = _window_tables()
    bias = _bias_build(rel_bias.T, bucket, band)

    w_in_g, w_o_p, w_up_p, w_down_p = _gather_weights(
        (w_in[0], w_o[0], w_ffn_up[0], w_ffn_down[0]), whole=(True, False, False, False))
    (h1, raw, qa, ka, kta, va, vta, qtb, kb, ktb, vb, vtb) = _pre_proj(
        x2, g1, w_in_g, gq8, gk2, ck, sk, seq=seq, tm=min(1024, seq), sub=256)
    (oa, p_a, linv_a), (w_part,) = _attn_a_fwd(
        qa, kta, va, seq=seq, bq=min(256, seq), jobs=[_gather_job([w_o_p, w_up_p, w_down_p], forward=False)])
    kb3 = kb.reshape(tokens // BLOCK, BLOCK, KV_WIDTH)
    vb3 = vb.reshape(tokens // BLOCK, BLOCK, KV_WIDTH)
    (ob, p_b, stat_b), ((w_o_g, w_up_g, w_down_g),) = _attn_b_fwd(
        qtb, kb3, vtb, bias, sink_b, seq=seq, per_step=min(16, seq // BLOCK),
        jobs=[_gather_job(w_part, forward=True)])
    w_o2 = w_o_g.reshape(D_MODEL, D_MODEL)
    w_down2 = w_down_g.reshape(D_FF, D_MODEL)
    mix, x1, h2, o_cat, u, df, dy, loss_t, dg4 = _mix_ffn_fwd(
        oa, ob, w_o2, x2, g2, g3, w_up_g, w_down2, loss_target.reshape(tokens, D_MODEL), g4, tm=256)

    dz, dx1, dmix, dg3, dg2, doa, dob = _ffn_bwd_act(df, w_down2, u, w_up_g, x1, dy, mix, g3, g2, w_o2, tm=256)
    gw_down, _ = _tn_matmul(u, df, name="grad_w_down", tm=1024, tn=1024, tk=min(4096, tokens), square_a=True,
                            vmem_mb=56)
    gw_down = gw_down.reshape(N_CHIPS, FF_CHUNK, D_MODEL)
    gw_up, ((got_down,),) = _tn_matmul(h2, dz, name="grad_w_up", tm=1024, tn=1024, tk=min(4096, tokens), chunk=FF_CHUNK,
                                        vmem_mb=56, jobs=[_swap_job([gw_down])])
    gw_o, _ = _tn_matmul(o_cat, dmix, name="grad_w_o", tm=1024, tn=1024, tk=min(2048, tokens))
    gw_o = gw_o.reshape(N_CHIPS, O_CHUNK, D_MODEL)
    sum_down = _add_half(gw_down, got_down, where, name="add_half_w_down", tr=512)
    (dqa, dkta, dvta), ((ex_down,), (got_up,)) = _attn_a_bwd(
        qa, ka, vta, doa, oa, p_a, linv_a, seq=seq, bq=min(256, seq), kt_tile=min(2048, seq),
        jobs=[_exchange_job([sum_down]), _swap_job([gw_up])])
    full_down = _add_chips(sum_down, ex_down, where, name="add_chips_w_down", tr=512)
    sum_up = _add_half(gw_up, got_up, where, name="add_half_w_up", tr=512)
    (dqb, dkb, dvb, dbias, dsink), ((ex_up,), (g_down,), (got_o,)) = _attn_b_bwd(
        qtb, ktb, vb3, dob, ob, p_b, stat_b, seq=seq, per_step=min(16, seq // BLOCK),
        jobs=[_exchange_job([sum_up]), _join_job([full_down]), _swap_job([gw_o])])
    full_up = _add_chips(sum_up, ex_up, where, name="add_chips_w_up", tr=512)
    sum_o = _add_half(gw_o, got_o, where, name="add_half_w_o", tr=512)
    (grad_x, dproj, dg1, dgq, dgk), _ = _proj_bwd(
        dqa, dkta, dvta, dqb, dkb, dvb, raw, x2, dx1, g1, w_in_g, gq8, gk2, ck, sk,
        seq=seq, tm=min(512, seq), sub=128)
    packed, packed_rel = _pack_small(dg1, dg2, dg3, dg4, dgq, dgk, dsink, dbias, bucket, loss_t)
    gw_in, ((ex_o,), (g_up,), (gathered, gathered_rel)) = _tn_matmul(
        h1, dproj, name="grad_w_in", tm=1024, tn=2 * IN_CHUNK, tk=min(4096, tokens), chunk=IN_CHUNK,
        vmem_mb=56, jobs=[_exchange_job([sum_o]), _join_job([full_up]), _small_job([packed, packed_rel])])
    full_o = _add_chips(sum_o, ex_o, where, name="add_chips_w_o", tr=512)

    (g_o,), (got_in,) = _run_jobs("tail_swap", [_join_job([full_o]), _swap_job([gw_in])])
    sum_in = _add_half(gw_in, got_in, where, name="add_half_w_in", tr=512)
    ((ex_in,),) = _run_jobs("tail_exchange", [_exchange_job([sum_in])])
    full_in = _add_chips(sum_in, ex_in, where, name="add_chips_w_in", tr=512)
    ((g_in,),) = _run_jobs("tail_join", [_join_job([full_in])])

    big = [[t[None] for t in _adamw(w[0], g, m[0], v[0], name="adamw_" + nm, tr=512)] for nm, w, g, m, v in (
        ("w_in", w_in, g_in, m_w_in, v_w_in), ("w_o", w_o, g_o, m_w_o, v_w_o),
        ("w_up", w_ffn_up, g_up, m_w_ffn_up, v_w_ffn_up), ("w_down", w_ffn_down, g_down, m_w_ffn_down, v_w_ffn_down))]

    loss, small = _small_adamw(
        gathered, gathered_rel,
        (g1, g2, g3, g4, q_norm_a, k_norm_a, sink_b, rel_bias),
        (m_g_pre_mix, m_g_post_mix, m_g_pre_ffn, m_g_post_ffn, m_q_norm_a, m_k_norm_a, m_sink_b, m_rel_bias),
        (v_g_pre_mix, v_g_post_mix, v_g_pre_ffn, v_g_post_ffn, v_q_norm_a, v_k_norm_a, v_sink_b, v_rel_bias))
    s_g1, s_g2, s_g3, s_g4, s_gq, s_gk, s_sink, s_rel = small

    def leaves(i):
        return (big[0][i], big[1][i], s_g1[i], s_g2[i], s_gq[i], s_gk[i], s_sink[i], s_rel[i], s_g3[i],
                big[2][i], big[3][i], s_g4[i])

    loss = loss.reshape(())
    return (loss, grad_x.reshape(batch, seq, D_MODEL), *leaves(0), *leaves(1), *leaves(2), *leaves(3))
```
